```python
import math
import jax, jax.numpy as jnp
from jax import lax
import numpy as np

D_MODEL = 1024
BATCH = 32
SEQ = 256
DEPTH = 4
DEC_BATCH = 4
DEC_SEQ = 4096
PAST_LEN = 512

GRID_W = 64
HEAD_DIM = 64
N_EVEN = (DEPTH + 1) // 2
N_ODD = DEPTH // 2
Q_BLOCK = 128
A_HEADS = D_MODEL // 128
A_KV_HEADS = A_HEADS // 4
A_WINDOW = 128
A_BLOCK = 128
ROPE_BASE = 10000.0
HY_CH = D_MODEL // 2
HY_ORDER = 2
HY_SHORT = 3
HY_EMB = 33
HY_FILT_W = 64
HY_FAST_DECAY = 0.3
HY_SLOW_DECAY = 1.5
HY_DECAY_TARGET = 1e-2
HY_SHIFT = 0.05
C_HEADS = D_MODEL // 128
C_SHORT = 3
C_CHUNK = 64
D_HEADS = D_MODEL // 128
NA_KH_MAX = 8
NA_KW = 16
N_EXPERTS = 16
EC_CAPACITY = 2
MOE_D_FF = D_MODEL
EPS = 1e-6
NEG_INF = -1e30

A_Q_W = A_HEADS * HEAD_DIM
A_KV_W = A_KV_HEADS * HEAD_DIM
EVEN_IN = A_Q_W + 2 * A_KV_W + 3 * HY_CH
EVEN_MIX = A_Q_W + HY_CH
C_W = C_HEADS * HEAD_DIM
D_W = D_HEADS * HEAD_DIM
ODD_IN = 4 * C_W + 4 * C_HEADS + 3 * D_W
ODD_MIX = C_W + D_W

kernel_name = 'hybrid_diffusion_trunk_step'


def rms_norm(x, g):
    xf = x.astype(jnp.float32)
    y = xf * lax.rsqrt(jnp.mean(xf * xf, axis=-1, keepdims=True) + EPS)
    return (y * g.astype(jnp.float32)).astype(x.dtype)


def ada_params(cond, w, b):
    m = jax.nn.silu(cond) @ w + b
    return jnp.split(m[:, None, :], 6, axis=-1)


def modulate(x, g, shift, scale):
    return rms_norm(x, g) * (1 + scale) + shift


def axial_rope(T):
    t = jnp.arange(T)
    n_freq = HEAD_DIM // 4
    inv = ROPE_BASE ** (-jnp.arange(n_freq, dtype=jnp.float32) / n_freq)
    ang = jnp.concatenate([(t // GRID_W).astype(jnp.float32)[:, None] * inv,
                           (t % GRID_W).astype(jnp.float32)[:, None] * inv], axis=-1)
    return jnp.cos(ang)[:, None, :], jnp.sin(ang)[:, None, :]


def apply_rope(x, cos, sin):
    xf = x.astype(jnp.float32)
    x1, x2 = jnp.split(xf, 2, axis=-1)
    return jnp.concatenate([x1 * cos - x2 * sin, x2 * cos + x1 * sin], axis=-1).astype(x.dtype)


def softmax_parts(parts, sink=None):
    sizes = [p.shape[-1] for p in parts]
    cols = list(parts)
    if sink is not None:
        cols.append(jnp.broadcast_to(sink, parts[0].shape[:-1] + (1,)))
    p = jax.nn.softmax(jnp.concatenate(cols, axis=-1), axis=-1)
    pieces = jnp.split(p, np.cumsum(sizes).tolist(), axis=-1)
    return pieces[:len(sizes)]


def context_attention(q, k, v, sink):
    B, S, HQ, hd = q.shape
    HK = k.shape[2]
    G = HQ // HK
    nb = S // Q_BLOCK
    scale = hd ** -0.5
    sink_b = None if sink is None else sink.astype(jnp.float32).reshape(1, HK, G, 1, 1)
    qb = q.reshape(B, nb, Q_BLOCK, HK, G, hd).swapaxes(0, 1)

    def one(qi):
        s = jnp.einsum('bqkgd,bskd->bkgqs', qi, k, preferred_element_type=jnp.float32) * scale
        (p,) = softmax_parts([s], sink_b)
        return jnp.einsum('bkgqs,bskd->bqkgd', p.astype(v.dtype), v)

    o = lax.map(one, qb)
    return o.swapaxes(0, 1).reshape(B, S, HQ, hd)


def window_attention(q, k, v, ck, cv, sink):
    B, T, HQ, hd = q.shape
    HK = k.shape[2]
    G = HQ // HK
    nb = T // A_BLOCK
    scale = hd ** -0.5
    qb = q.reshape(B, nb, A_BLOCK, HK, G, hd).swapaxes(0, 1)

    def band(x):
        xp = jnp.pad(x, ((0, 0), (A_BLOCK, A_BLOCK), (0, 0), (0, 0))).reshape(B, nb + 2, A_BLOCK, HK, hd)
        return jnp.concatenate([xp[:, :-2], xp[:, 1:-1], xp[:, 2:]], axis=2).swapaxes(0, 1)

    kb, vb = band(k), band(v)
    qpos = jnp.arange(nb)[:, None, None] * A_BLOCK + jnp.arange(A_BLOCK)[None, :, None]
    kpos = jnp.arange(nb)[:, None, None] * A_BLOCK - A_BLOCK + jnp.arange(3 * A_BLOCK)[None, None, :]
    mask = (jnp.abs(kpos - qpos) <= A_WINDOW) & (kpos >= 0) & (kpos < T)
    sink_b = sink.astype(jnp.float32).reshape(1, HK, G, 1, 1)

    def one(xs):
        qi, ki, vi, mi = xs
        s_loc = jnp.einsum('bqkgd,bskd->bkgqs', qi, ki, preferred_element_type=jnp.float32) * scale
        s_loc = jnp.where(mi[None, None, None], s_loc, NEG_INF)
        s_ctx = jnp.einsum('bqkgd,bpkd->bkgqp', qi, ck, preferred_element_type=jnp.float32) * scale
        p_loc, p_ctx = softmax_parts([s_loc, s_ctx], sink_b)
        return (jnp.einsum('bkgqs,bskd->bqkgd', p_loc.astype(vi.dtype), vi)
                + jnp.einsum('bkgqp,bpkd->bqkgd', p_ctx.astype(cv.dtype), cv))

    o = lax.map(one, (qb, kb, vb, mask))
    return o.swapaxes(0, 1).reshape(B, T, HQ, hd)


def short_conv(x, w):
    K = w.shape[0]
    L = x.shape[1]
    pad = K // 2
    xp = jnp.pad(x, ((0, 0), (pad, pad), (0, 0)))
    return sum(xp[:, i:i + L] * w[i] for i in range(K))


def hyena_filter_bank(L, w1, b1, w2, b2, w3, freq):
    f32 = jnp.float32
    t = jnp.linspace(0.0, 1.0, L, dtype=f32)[:, None]
    bands = (HY_EMB - 1) // 2
    omega = 2.0 * math.pi * jnp.arange(L, dtype=f32)[:, None] / L
    fb = jnp.linspace(1e-4, bands - 1, bands, dtype=f32)[None, :]
    feats = jnp.concatenate([t, jnp.cos(fb * omega), -jnp.sin(fb * omega)], axis=-1)
    fr = freq.astype(f32)
    h = jnp.sin(fr * (feats @ w1.astype(f32) + b1.astype(f32)))
    h = jnp.sin(fr * (h @ w2.astype(f32) + b2.astype(f32)))
    h = (h @ w3.astype(f32)).reshape(L, 2, HY_ORDER, HY_CH)
    max_decay = math.log(HY_DECAY_TARGET) / HY_FAST_DECAY
    min_decay = math.log(HY_DECAY_TARGET) / HY_SLOW_DECAY
    deltas = jnp.abs(jnp.linspace(min_decay, max_decay, HY_CH, dtype=f32))
    h = h * (jnp.exp(-t * deltas) + HY_SHIFT)[:, None, None, :]
    taps = jnp.concatenate([h[:, 0], jnp.zeros((1, HY_ORDER, HY_CH), f32), h[:0:-1, 1]], axis=0)
    taps = taps / jnp.sum(jnp.abs(taps), axis=0, keepdims=True)
    return jnp.fft.rfft(taps, axis=0)


def hyena_mixer(z, conv_w, conv_b, w1, b1, w2, b2, w3, freq, skip):
    L = z.shape[1]
    zf = (short_conv(z, conv_w) + conv_b).astype(jnp.float32)
    v, x1, x2 = jnp.split(zf, 3, axis=-1)
    filt = hyena_filter_bank(L, w1, b1, w2, b2, w3, freq)
    skip = skip.astype(jnp.float32)

    def long_conv(u, o):
        y = jnp.fft.irfft(jnp.fft.rfft(u, n=2 * L, axis=1) * filt[None, :, o], n=2 * L, axis=1)[:, :L]
        return y + u * skip[o]

    y = x1 * long_conv(v, 0)
    y = x2 * long_conv(y, 1)
    return y.astype(z.dtype)


def l2norm(x):
    xf = x.astype(jnp.float32)
    return xf * lax.rsqrt(jnp.sum(xf * xf, axis=-1, keepdims=True) + EPS)


def chunk_gated_delta(q, k, v, g, beta, s0):
    B, L, H, dk = q.shape
    dv = v.shape[-1]
    n = L // C_CHUNK

    def chunks(x):
        x = x.reshape((B, n, C_CHUNK, H) + x.shape[3:])
        return jnp.moveaxis(jnp.moveaxis(x, 1, 0), 3, 2)

    qc, kc, vc, bc = chunks(q), chunks(k), chunks(v), chunks(beta)
    gc = jnp.cumsum(chunks(g), axis=-1)
    tri = jnp.tril(jnp.ones((C_CHUNK, C_CHUNK), bool))
    strict = jnp.tril(jnp.ones((C_CHUNK, C_CHUNK), bool), k=-1)
    gamma = jnp.exp(jnp.where(tri, gc[..., :, None] - gc[..., None, :], NEG_INF))
    kb = kc * bc[..., None]
    a_mat = jnp.where(strict, jnp.einsum('nbhid,nbhjd->nbhij', kb, kc) * gamma, 0.0) + jnp.eye(C_CHUNK, dtype=jnp.float32)
    rhs = jnp.concatenate([vc * bc[..., None], kb * jnp.exp(gc)[..., None]], axis=-1)
    sol = lax.linalg.triangular_solve(a_mat, rhs, left_side=True, lower=True)
    u, w = sol[..., :dv], sol[..., dv:]
    attn = jnp.where(tri, jnp.einsum('nbhid,nbhjd->nbhij', qc, kc) * gamma, 0.0)
    g_last = gc[..., -1]
    q_dec = qc * jnp.exp(gc)[..., None]
    k_dec = kc * jnp.exp(g_last[..., None] - gc)[..., None]

    def step(S, xs):
        u_i, w_i, a_i, qd, kd, gl = xs
        v_new = u_i - jnp.einsum('bhck,bhkv->bhcv', w_i, S)
        o = jnp.einsum('bhck,bhkv->bhcv', qd, S) + jnp.einsum('bhij,bhjv->bhiv', a_i, v_new)
        S = S * jnp.exp(gl)[..., None, None] + jnp.einsum('bhck,bhcv->bhkv', kd, v_new)
        return S, o

    S, o = lax.scan(step, s0.astype(jnp.float32), (u, w, attn, q_dec, k_dec, g_last))
    o = jnp.moveaxis(jnp.moveaxis(o, 2, 3), 0, 1).reshape(B, L, H, dv)
    return o, S


def deltanet_mixer(zq, zk, zv, zg, za, zb, conv_w, a_log, dt_bias, norm_w, s0):
    B, L, _ = zq.shape
    qkv = jax.nn.silu(short_conv(jnp.concatenate([zq, zk, zv], axis=-1), conv_w))
    q, k, v = [t.reshape(B, L, C_HEADS, HEAD_DIM) for t in jnp.split(qkv, 3, axis=-1)]
    q = l2norm(q) * (HEAD_DIM ** -0.5)
    k = l2norm(k)
    v = v.astype(jnp.float32)
    beta = jax.nn.sigmoid(zb.astype(jnp.float32))
    g = -jnp.exp(a_log.astype(jnp.float32)) * jax.nn.softplus(za.astype(jnp.float32) + dt_bias.astype(jnp.float32))
    o_f, s_f = chunk_gated_delta(q, k, v, g[:, :, 0], beta[:, :, 0], s0[:, 0])
    o_b, s_b = chunk_gated_delta(q[:, ::-1], k[:, ::-1], v[:, ::-1], g[:, ::-1, 1], beta[:, ::-1, 1], s0[:, 1])
    o = o_f + o_b[:, ::-1]
    gate = jax.nn.silu(zg.reshape(B, L, C_HEADS, HEAD_DIM).astype(jnp.float32))
    o = rms_norm(o, norm_w) * gate
    return o.reshape(B, L, C_W).astype(zq.dtype), jnp.stack([s_f, s_b], axis=1)


def neighbourhood_attention(q, k, v, ck, cv, rpb):
    B, T, H, hd = q.shape
    rows = T // GRID_W
    kh = min(NA_KH_MAX, rows)
    scale = hd ** -0.5
    r = jnp.arange(rows)
    rs = jnp.clip(r - kh // 2, 0, rows - kh)
    key_rows = rs[:, None] + jnp.arange(kh)[None, :]
    idx = (key_rows[:, :, None] * GRID_W + jnp.arange(GRID_W)).reshape(rows, kh * GRID_W)
    col = jnp.arange(GRID_W)
    cs = jnp.clip(col - NA_KW // 2, 0, GRID_W - NA_KW)
    kcol = jnp.tile(col, kh)
    col_ok = (kcol[None, :] >= cs[:, None]) & (kcol[None, :] < cs[:, None] + NA_KW)
    roff = jnp.repeat(key_rows - r[:, None], GRID_W, axis=1) + NA_KH_MAX - 1
    coff = jnp.clip(kcol[None, :] - col[:, None] + NA_KW - 1, 0, 2 * NA_KW - 2)
    qr = q.reshape(B, rows, GRID_W, H, hd).swapaxes(0, 1)
    rpb_f = rpb.astype(jnp.float32)

    def one(xs):
        qi, ii, ro = xs
        ki = k[:, ii]
        vi = v[:, ii]
        bias = rpb_f[:, ro[None, :], coff]
        s_loc = jnp.einsum('bqhd,bkhd->bhqk', qi, ki, preferred_element_type=jnp.float32) * scale + bias[None]
        s_loc = jnp.where(col_ok[None, None], s_loc, NEG_INF)
        s_ctx = jnp.einsum('bqhd,bphd->bhqp', qi, ck, preferred_element_type=jnp.float32) * scale
        p_loc, p_ctx = softmax_parts([s_loc, s_ctx])
        return (jnp.einsum('bhqk,bkhd->bqhd', p_loc.astype(vi.dtype), vi)
                + jnp.einsum('bhqp,bphd->bqhd', p_ctx.astype(cv.dtype), cv))

    o = lax.map(one, (qr, idx, roff))
    return o.swapaxes(0, 1).reshape(B, T, H, hd)


def expert_choice_ffn(h, w_router, w_gate, w_up, w_down):
    B, T, D = h.shape
    cap = EC_CAPACITY * T // N_EXPERTS
    aff = jax.nn.softmax(jnp.einsum('btd,de->bte', h, w_router, preferred_element_type=jnp.float32), axis=-1)
    gate, idx = lax.top_k(aff.swapaxes(1, 2), cap)
    xg = jax.vmap(lambda hb, ib: hb[ib])(h, idx)
    a = jnp.einsum('becd,edf->becf', xg, w_gate)
    u = jnp.einsum('becd,edf->becf', xg, w_up)
    y = jnp.einsum('becf,efd->becd', jax.nn.silu(a) * u, w_down) * gate[..., None].astype(h.dtype)
    return jax.vmap(lambda yb, ib: jnp.zeros((T, D), yb.dtype).at[ib.reshape(-1)].add(yb.reshape(-1, D)))(y, idx)


def split_even(z):
    B, L = z.shape[:2]
    q = z[..., :A_Q_W].reshape(B, L, A_HEADS, HEAD_DIM)
    k = z[..., A_Q_W:A_Q_W + A_KV_W].reshape(B, L, A_KV_HEADS, HEAD_DIM)
    v = z[..., A_Q_W + A_KV_W:A_Q_W + 2 * A_KV_W].reshape(B, L, A_KV_HEADS, HEAD_DIM)
    return q, k, v, z[..., A_Q_W + 2 * A_KV_W:]


def split_odd(z):
    B, L = z.shape[:2]
    zq, zk, zv, zg = [z[..., i * C_W:(i + 1) * C_W] for i in range(4)]
    off = 4 * C_W
    za = z[..., off:off + 2 * C_HEADS].reshape(B, L, 2, C_HEADS)
    zb = z[..., off + 2 * C_HEADS:off + 4 * C_HEADS].reshape(B, L, 2, C_HEADS)
    off = off + 4 * C_HEADS
    nq, nk, nv = [z[..., off + i * D_W:off + (i + 1) * D_W].reshape(B, L, D_HEADS, HEAD_DIM) for i in range(3)]
    return zq, zk, zv, zg, za, zb, nq, nk, nv


def setup_inputs(seed: int = 0) -> dict:
    key = jax.random.key(seed)
    ks = iter(jax.random.split(key, 48))
    f32 = jnp.float32

    def nrm(shape, s):
        return jax.random.normal(next(ks), shape, f32) * s

    def gain(shape):
        return 1.0 + nrm(shape, 0.02)

    dt = jnp.exp(jax.random.uniform(next(ks), (N_ODD, 2, C_HEADS), f32, math.log(1e-3), math.log(1e-1)))
    a_log = jnp.log(jax.random.uniform(next(ks), (N_ODD, 2, C_HEADS), f32, 1.0, 16.0))
    return {
        'x_prompt': nrm((BATCH, SEQ, D_MODEL), 1.0),
        'x_sample': nrm((DEC_BATCH, DEC_SEQ, D_MODEL), 1.0),
        'cache_attn_k': nrm((DEC_BATCH, N_EVEN, PAST_LEN, A_KV_HEADS, HEAD_DIM), 1.0),
        'cache_attn_v': nrm((DEC_BATCH, N_EVEN, PAST_LEN, A_KV_HEADS, HEAD_DIM), 1.0),
        'state_delta': nrm((DEC_BATCH, N_ODD, 2, C_HEADS, HEAD_DIM, HEAD_DIM), 0.1),
        'cache_na_k': nrm((DEC_BATCH, N_ODD, PAST_LEN, D_HEADS, HEAD_DIM), 1.0),
        'cache_na_v': nrm((DEC_BATCH, N_ODD, PAST_LEN, D_HEADS, HEAD_DIM), 1.0),
        'c': nrm((DEC_BATCH, D_MODEL), 1.0),
        'c_ctx': nrm((D_MODEL,), 1.0),
        'w_ada': nrm((DEPTH, D_MODEL, 6 * D_MODEL), 0.5 * D_MODEL ** -0.5),
        'b_ada': nrm((DEPTH, 6 * D_MODEL), 0.02),
        'norm_mix': gain((DEPTH, D_MODEL)),
        'norm_ffn': gain((DEPTH, D_MODEL)),
        'norm_final': gain((D_MODEL,)),
        'even_w_in': nrm((N_EVEN, D_MODEL, EVEN_IN), D_MODEL ** -0.5),
        'even_w_out': nrm((N_EVEN, EVEN_MIX, D_MODEL), EVEN_MIX ** -0.5),
        'attn_sink': nrm((N_EVEN, A_HEADS), 0.5),
        'hy_conv_w': nrm((N_EVEN, HY_SHORT, 3 * HY_CH), HY_SHORT ** -0.5),
        'hy_conv_b': nrm((N_EVEN, 3 * HY_CH), 0.02),
        'hy_w1': nrm((N_EVEN, HY_EMB, HY_FILT_W), HY_EMB ** -0.5),
        'hy_b1': nrm((N_EVEN, HY_FILT_W), 0.02),
        'hy_w2': nrm((N_EVEN, HY_FILT_W, HY_FILT_W), HY_FILT_W ** -0.5),
        'hy_b2': nrm((N_EVEN, HY_FILT_W), 0.02),
        'hy_w3': nrm((N_EVEN, HY_FILT_W, 2 * HY_ORDER * HY_CH), HY_FILT_W ** -0.5),
        'hy_freq': gain((N_EVEN, HY_FILT_W)),
        'hy_skip': nrm((N_EVEN, HY_ORDER, HY_CH), 0.5),
        'odd_w_in': nrm((N_ODD, D_MODEL, ODD_IN), D_MODEL ** -0.5),
        'odd_w_out': nrm((N_ODD, ODD_MIX, D_MODEL), ODD_MIX ** -0.5),
        'gdn_conv_w': nrm((N_ODD, C_SHORT, 3 * C_W), C_SHORT ** -0.5),
        'gdn_a_log': a_log,
        'gdn_dt_bias': dt + jnp.log(-jnp.expm1(-dt)),
        'gdn_norm': gain((N_ODD, HEAD_DIM)),
        'na_rpb': nrm((N_ODD, D_HEADS, 2 * NA_KH_MAX - 1, 2 * NA_KW - 1), 0.1),
        'moe_router': nrm((DEPTH, D_MODEL, N_EXPERTS), D_MODEL ** -0.5),
        'moe_w_gate': nrm((DEPTH, N_EXPERTS, D_MODEL, MOE_D_FF), D_MODEL ** -0.5),
        'moe_w_up': nrm((DEPTH, N_EXPERTS, D_MODEL, MOE_D_FF), D_MODEL ** -0.5),
        'moe_w_down': nrm((DEPTH, N_EXPERTS, MOE_D_FF, D_MODEL), MOE_D_FF ** -0.5),
    }


def reference(x_prompt, x_sample, cache_attn_k, cache_attn_v, state_delta, cache_na_k, cache_na_v,
              c, c_ctx, w_ada, b_ada, norm_mix, norm_ffn, norm_final,
              even_w_in, even_w_out, attn_sink, hy_conv_w, hy_conv_b, hy_w1, hy_b1, hy_w2, hy_b2,
              hy_w3, hy_freq, hy_skip, odd_w_in, odd_w_out, gdn_conv_w, gdn_a_log, gdn_dt_bias,
              gdn_norm, na_rpb, moe_router, moe_w_gate, moe_w_up, moe_w_down):
    xp, xs = x_prompt, x_sample
    bp = xp.shape[0]
    cos, sin = axial_rope(xs.shape[1])
    new_ak, new_av, new_st, new_nk, new_nv = [], [], [], [], []
    for l in range(DEPTH):
        j = l // 2
        mp = ada_params(c_ctx[None, :], w_ada[l], b_ada[l])
        ms = ada_params(c, w_ada[l], b_ada[l])
        hp = modulate(xp, norm_mix[l], mp[0], mp[1])
        hs = modulate(xs, norm_mix[l], ms[0], ms[1])
        if l % 2 == 0:
            hy = (hy_conv_w[j], hy_conv_b[j], hy_w1[j], hy_b1[j], hy_w2[j], hy_b2[j], hy_w3[j], hy_freq[j], hy_skip[j])
            q, k, v, zh = split_even(hp @ even_w_in[j])
            oa = context_attention(q, k, v, attn_sink[j])
            mix_p = jnp.concatenate([oa.reshape(oa.shape[:2] + (A_Q_W,)), hyena_mixer(zh, *hy)], axis=-1) @ even_w_out[j]
            new_ak.append(k)
            new_av.append(v)
            q, k, v, zh = split_even(hs @ even_w_in[j])
            q, k = apply_rope(q, cos, sin), apply_rope(k, cos, sin)
            oa = window_attention(q, k, v, cache_attn_k[:, j], cache_attn_v[:, j], attn_sink[j])
            mix_s = jnp.concatenate([oa.reshape(oa.shape[:2] + (A_Q_W,)), hyena_mixer(zh, *hy)], axis=-1) @ even_w_out[j]
        else:
            gdn = (gdn_conv_w[j], gdn_a_log[j], gdn_dt_bias[j], gdn_norm[j])
            zq, zk, zv, zg, za, zb, nq, nk, nv = split_odd(hp @ odd_w_in[j])
            s0 = jnp.zeros((bp, 2, C_HEADS, HEAD_DIM, HEAD_DIM), jnp.float32)
            oc, st = deltanet_mixer(zq, zk, zv, zg, za, zb, *gdn, s0)
            od = context_attention(nq, nk, nv, None)
            mix_p = jnp.concatenate([oc, od.reshape(od.shape[:2] + (D_W,))], axis=-1) @ odd_w_out[j]
            new_st.append(st)
            new_nk.append(nk)
            new_nv.append(nv)
            zq, zk, zv, zg, za, zb, nq, nk, nv = split_odd(hs @ odd_w_in[j])
            oc, _ = deltanet_mixer(zq, zk, zv, zg, za, zb, *gdn, state_delta[:, j])
            od = neighbourhood_attention(nq, nk, nv, cache_na_k[:, j], cache_na_v[:, j], na_rpb[j])
            mix_s = jnp.concatenate([oc, od.reshape(od.shape[:2] + (D_W,))], axis=-1) @ odd_w_out[j]
        xp = xp + mp[2] * mix_p
        xs = xs + ms[2] * mix_s
        moe = (moe_router[l], moe_w_gate[l], moe_w_up[l], moe_w_down[l])
        xp = xp + mp[5] * expert_choice_ffn(modulate(xp, norm_ffn[l], mp[3], mp[4]), *moe)
        xs = xs + ms[5] * expert_choice_ffn(modulate(xs, norm_ffn[l], ms[3], ms[4]), *moe)
    y_prompt = rms_norm(xp, norm_final)
    y_sample = rms_norm(xs, norm_final)
    return (y_prompt, y_sample, jnp.stack(new_ak, axis=1), jnp.stack(new_av, axis=1), jnp.stack(new_st, axis=1), jnp.stack(new_nk, axis=1), jnp.stack(new_nv, axis=1))
```

```python
import functools
import math
import jax, jax.numpy as jnp
from jax import lax
import numpy as np
from jax.experimental import pallas as pl
from jax.experimental.pallas import tpu as pltpu

D_MODEL = 1024
BATCH = 32
SEQ = 256
DEPTH = 4
DEC_BATCH = 4
DEC_SEQ = 4096
PAST_LEN = 512

GRID_W = 64
HEAD_DIM = 64
N_EVEN = (DEPTH + 1) // 2
N_ODD = DEPTH // 2
Q_BLOCK = 128
A_HEADS = D_MODEL // 128
A_KV_HEADS = A_HEADS // 4
A_WINDOW = 128
A_BLOCK = 128
ROPE_BASE = 10000.0
HY_CH = D_MODEL // 2
HY_ORDER = 2
HY_SHORT = 3
HY_EMB = 33
HY_FILT_W = 64
HY_FAST_DECAY = 0.3
HY_SLOW_DECAY = 1.5
HY_DECAY_TARGET = 1e-2
HY_SHIFT = 0.05
C_HEADS = D_MODEL // 128
C_SHORT = 3
C_CHUNK = 64
D_HEADS = D_MODEL // 128
NA_KH_MAX = 8
NA_KW = 16
N_EXPERTS = 16
EC_CAPACITY = 2
MOE_D_FF = D_MODEL
EPS = 1e-6
NEG_INF = -1e30

A_Q_W = A_HEADS * HEAD_DIM
A_KV_W = A_KV_HEADS * HEAD_DIM
EVEN_IN = A_Q_W + 2 * A_KV_W + 3 * HY_CH
EVEN_MIX = A_Q_W + HY_CH
C_W = C_HEADS * HEAD_DIM
D_W = D_HEADS * HEAD_DIM
ODD_IN = 4 * C_W + 4 * C_HEADS + 3 * D_W
ODD_MIX = C_W + D_W

VMEM_LIMIT_BYTES = 48 * 1024 * 1024


def _mm_kernel(x_ref, w_ref, o_ref):
    o_ref[...] = jnp.dot(x_ref[...].astype(jnp.bfloat16), w_ref[...], preferred_element_type=jnp.float32)


def pallas_matmul(x, w, tm=256):
    M, K = x.shape
    N = w.shape[1]
    assert M % tm == 0
    return pl.pallas_call(
        _mm_kernel,
        grid=(M // tm,),
        in_specs=[pl.BlockSpec((tm, K), lambda i: (i, 0)), pl.BlockSpec((K, N), lambda i: (0, 0))],
        out_specs=pl.BlockSpec((tm, N), lambda i: (i, 0)),
        out_shape=jax.ShapeDtypeStruct((M, N), jnp.float32),
        compiler_params=pltpu.CompilerParams(dimension_semantics=("arbitrary",), vmem_limit_bytes=VMEM_LIMIT_BYTES),
    )(x, w.astype(jnp.bfloat16))


def proj(x, w):
    B, L, K = x.shape
    return pallas_matmul(x.reshape(B * L, K), w).reshape(B, L, w.shape[1])


def rms_norm(x, g):
    xf = x.astype(jnp.float32)
    y = xf * lax.rsqrt(jnp.mean(xf * xf, axis=-1, keepdims=True) + EPS)
    return (y * g.astype(jnp.float32)).astype(x.dtype)


def ada_params(cond, w, b):
    m = jax.nn.silu(cond) @ w + b
    return jnp.split(m[:, None, :], 6, axis=-1)


def modulate(x, g, shift, scale):
    return rms_norm(x, g) * (1 + scale) + shift


def axial_rope(T):
    t = jnp.arange(T)
    n_freq = HEAD_DIM // 4
    inv = ROPE_BASE ** (-jnp.arange(n_freq, dtype=jnp.float32) / n_freq)
    ang = jnp.concatenate([(t // GRID_W).astype(jnp.float32)[:, None] * inv,
                           (t % GRID_W).astype(jnp.float32)[:, None] * inv], axis=-1)
    return jnp.cos(ang)[:, None, :], jnp.sin(ang)[:, None, :]


def apply_rope(x, cos, sin):
    xf = x.astype(jnp.float32)
    x1, x2 = jnp.split(xf, 2, axis=-1)
    return jnp.concatenate([x1 * cos - x2 * sin, x2 * cos + x1 * sin], axis=-1).astype(x.dtype)


def softmax_parts(parts, sink=None):
    sizes = [p.shape[-1] for p in parts]
    cols = list(parts)
    if sink is not None:
        cols.append(jnp.broadcast_to(sink, parts[0].shape[:-1] + (1,)))
    p = jax.nn.softmax(jnp.concatenate(cols, axis=-1), axis=-1)
    pieces = jnp.split(p, np.cumsum(sizes).tolist(), axis=-1)
    return pieces[:len(sizes)]


def context_attention(q, k, v, sink):
    B, S, HQ, hd = q.shape
    HK = k.shape[2]
    G = HQ // HK
    nb = S // Q_BLOCK
    scale = hd ** -0.5
    sink_b = None if sink is None else sink.astype(jnp.float32).reshape(1, HK, G, 1, 1)
    qb = q.reshape(B, nb, Q_BLOCK, HK, G, hd).swapaxes(0, 1)

    def one(qi):
        s = jnp.einsum('bqkgd,bskd->bkgqs', qi, k, preferred_element_type=jnp.float32) * scale
        (p,) = softmax_parts([s], sink_b)
        return jnp.einsum('bkgqs,bskd->bqkgd', p.astype(v.dtype), v)

    o = lax.map(one, qb)
    return o.swapaxes(0, 1).reshape(B, S, HQ, hd)


def window_attention(q, k, v, ck, cv, sink):
    B, T, HQ, hd = q.shape
    HK = k.shape[2]
    G = HQ // HK
    nb = T // A_BLOCK
    scale = hd ** -0.5
    qb = q.reshape(B, nb, A_BLOCK, HK, G, hd).swapaxes(0, 1)

    def band(x):
        xp = jnp.pad(x, ((0, 0), (A_BLOCK, A_BLOCK), (0, 0), (0, 0))).reshape(B, nb + 2, A_BLOCK, HK, hd)
        return jnp.concatenate([xp[:, :-2], xp[:, 1:-1], xp[:, 2:]], axis=2).swapaxes(0, 1)

    kb, vb = band(k), band(v)
    qpos = jnp.arange(nb)[:, None, None] * A_BLOCK + jnp.arange(A_BLOCK)[None, :, None]
    kpos = jnp.arange(nb)[:, None, None] * A_BLOCK - A_BLOCK + jnp.arange(3 * A_BLOCK)[None, None, :]
    mask = (jnp.abs(kpos - qpos) <= A_WINDOW) & (kpos >= 0) & (kpos < T)
    sink_b = sink.astype(jnp.float32).reshape(1, HK, G, 1, 1)

    def one(xs):
        qi, ki, vi, mi = xs
        s_loc = jnp.einsum('bqkgd,bskd->bkgqs', qi, ki, preferred_element_type=jnp.float32) * scale
        s_loc = jnp.where(mi[None, None, None], s_loc, NEG_INF)
        s_ctx = jnp.einsum('bqkgd,bpkd->bkgqp', qi, ck, preferred_element_type=jnp.float32) * scale
        p_loc, p_ctx = softmax_parts([s_loc, s_ctx], sink_b)
        return (jnp.einsum('bkgqs,bskd->bqkgd', p_loc.astype(vi.dtype), vi)
                + jnp.einsum('bkgqp,bpkd->bqkgd', p_ctx.astype(cv.dtype), cv))

    o = lax.map(one, (qb, kb, vb, mask))
    return o.swapaxes(0, 1).reshape(B, T, HQ, hd)


def short_conv(x, w):
    K = w.shape[0]
    L = x.shape[1]
    pad = K // 2
    xp = jnp.pad(x, ((0, 0), (pad, pad), (0, 0)))
    return sum(xp[:, i:i + L] * w[i] for i in range(K))


def hyena_filter_bank(L, w1, b1, w2, b2, w3, freq):
    f32 = jnp.float32
    t = jnp.linspace(0.0, 1.0, L, dtype=f32)[:, None]
    bands = (HY_EMB - 1) // 2
    omega = 2.0 * math.pi * jnp.arange(L, dtype=f32)[:, None] / L
    fb = jnp.linspace(1e-4, bands - 1, bands, dtype=f32)[None, :]
    feats = jnp.concatenate([t, jnp.cos(fb * omega), -jnp.sin(fb * omega)], axis=-1)
    fr = freq.astype(f32)
    h = jnp.sin(fr * (feats @ w1.astype(f32) + b1.astype(f32)))
    h = jnp.sin(fr * (h @ w2.astype(f32) + b2.astype(f32)))
    h = (h @ w3.astype(f32)).reshape(L, 2, HY_ORDER, HY_CH)
    max_decay = math.log(HY_DECAY_TARGET) / HY_FAST_DECAY
    min_decay = math.log(HY_DECAY_TARGET) / HY_SLOW_DECAY
    deltas = jnp.abs(jnp.linspace(min_decay, max_decay, HY_CH, dtype=f32))
    h = h * (jnp.exp(-t * deltas) + HY_SHIFT)[:, None, None, :]
    taps = jnp.concatenate([h[:, 0], jnp.zeros((1, HY_ORDER, HY_CH), f32), h[:0:-1, 1]], axis=0)
    taps = taps / jnp.sum(jnp.abs(taps), axis=0, keepdims=True)
    return jnp.fft.rfft(taps, axis=0)


def hyena_mixer(z, conv_w, conv_b, w1, b1, w2, b2, w3, freq, skip):
    L = z.shape[1]
    zf = (short_conv(z, conv_w) + conv_b).astype(jnp.float32)
    v, x1, x2 = jnp.split(zf, 3, axis=-1)
    filt = hyena_filter_bank(L, w1, b1, w2, b2, w3, freq)
    skip = skip.astype(jnp.float32)

    def long_conv(u, o):
        y = jnp.fft.irfft(jnp.fft.rfft(u, n=2 * L, axis=1) * filt[None, :, o], n=2 * L, axis=1)[:, :L]
        return y + u * skip[o]

    y = x1 * long_conv(v, 0)
    y = x2 * long_conv(y, 1)
    return y.astype(z.dtype)


def l2norm(x):
    xf = x.astype(jnp.float32)
    return xf * lax.rsqrt(jnp.sum(xf * xf, axis=-1, keepdims=True) + EPS)


def chunk_gated_delta(q, k, v, g, beta, s0):
    B, L, H, dk = q.shape
    dv = v.shape[-1]
    n = L // C_CHUNK

    def chunks(x):
        x = x.reshape((B, n, C_CHUNK, H) + x.shape[3:])
        return jnp.moveaxis(jnp.moveaxis(x, 1, 0), 3, 2)

    qc, kc, vc, bc = chunks(q), chunks(k), chunks(v), chunks(beta)
    gc = jnp.cumsum(chunks(g), axis=-1)
    tri = jnp.tril(jnp.ones((C_CHUNK, C_CHUNK), bool))
    strict = jnp.tril(jnp.ones((C_CHUNK, C_CHUNK), bool), k=-1)
    gamma = jnp.exp(jnp.where(tri, gc[..., :, None] - gc[..., None, :], NEG_INF))
    kb = kc * bc[..., None]
    a_mat = jnp.where(strict, jnp.einsum('nbhid,nbhjd->nbhij', kb, kc) * gamma, 0.0) + jnp.eye(C_CHUNK, dtype=jnp.float32)
    rhs = jnp.concatenate([vc * bc[..., None], kb * jnp.exp(gc)[..., None]], axis=-1)
    sol = lax.linalg.triangular_solve(a_mat, rhs, left_side=True, lower=True)
    u, w = sol[..., :dv], sol[..., dv:]
    attn = jnp.where(tri, jnp.einsum('nbhid,nbhjd->nbhij', qc, kc) * gamma, 0.0)
    g_last = gc[..., -1]
    q_dec = qc * jnp.exp(gc)[..., None]
    k_dec = kc * jnp.exp(g_last[..., None] - gc)[..., None]

    def step(S, xs):
        u_i, w_i, a_i, qd, kd, gl = xs
        v_new = u_i - jnp.einsum('bhck,bhkv->bhcv', w_i, S)
        o = jnp.einsum('bhck,bhkv->bhcv', qd, S) + jnp.einsum('bhij,bhjv->bhiv', a_i, v_new)
        S = S * jnp.exp(gl)[..., None, None] + jnp.einsum('bhck,bhcv->bhkv', kd, v_new)
        return S, o

    S, o = lax.scan(step, s0.astype(jnp.float32), (u, w, attn, q_dec, k_dec, g_last))
    o = jnp.moveaxis(jnp.moveaxis(o, 2, 3), 0, 1).reshape(B, L, H, dv)
    return o, S


def deltanet_mixer(zq, zk, zv, zg, za, zb, conv_w, a_log, dt_bias, norm_w, s0):
    B, L, _ = zq.shape
    qkv = jax.nn.silu(short_conv(jnp.concatenate([zq, zk, zv], axis=-1), conv_w))
    q, k, v = [t.reshape(B, L, C_HEADS, HEAD_DIM) for t in jnp.split(qkv, 3, axis=-1)]
    q = l2norm(q) * (HEAD_DIM ** -0.5)
    k = l2norm(k)
    v = v.astype(jnp.float32)
    beta = jax.nn.sigmoid(zb.astype(jnp.float32))
    g = -jnp.exp(a_log.astype(jnp.float32)) * jax.nn.softplus(za.astype(jnp.float32) + dt_bias.astype(jnp.float32))
    o_f, s_f = chunk_gated_delta(q, k, v, g[:, :, 0], beta[:, :, 0], s0[:, 0])
    o_b, s_b = chunk_gated_delta(q[:, ::-1], k[:, ::-1], v[:, ::-1], g[:, ::-1, 1], beta[:, ::-1, 1], s0[:, 1])
    o = o_f + o_b[:, ::-1]
    gate = jax.nn.silu(zg.reshape(B, L, C_HEADS, HEAD_DIM).astype(jnp.float32))
    o = rms_norm(o, norm_w) * gate
    return o.reshape(B, L, C_W).astype(zq.dtype), jnp.stack([s_f, s_b], axis=1)


def neighbourhood_attention(q, k, v, ck, cv, rpb):
    B, T, H, hd = q.shape
    rows = T // GRID_W
    kh = min(NA_KH_MAX, rows)
    scale = hd ** -0.5
    r = jnp.arange(rows)
    rs = jnp.clip(r - kh // 2, 0, rows - kh)
    key_rows = rs[:, None] + jnp.arange(kh)[None, :]
    idx = (key_rows[:, :, None] * GRID_W + jnp.arange(GRID_W)).reshape(rows, kh * GRID_W)
    col = jnp.arange(GRID_W)
    cs = jnp.clip(col - NA_KW // 2, 0, GRID_W - NA_KW)
    kcol = jnp.tile(col, kh)
    col_ok = (kcol[None, :] >= cs[:, None]) & (kcol[None, :] < cs[:, None] + NA_KW)
    roff = jnp.repeat(key_rows - r[:, None], GRID_W, axis=1) + NA_KH_MAX - 1
    coff = jnp.clip(kcol[None, :] - col[:, None] + NA_KW - 1, 0, 2 * NA_KW - 2)
    qr = q.reshape(B, rows, GRID_W, H, hd).swapaxes(0, 1)
    rpb_f = rpb.astype(jnp.float32)

    def one(xs):
        qi, ii, ro = xs
        ki = k[:, ii]
        vi = v[:, ii]
        bias = rpb_f[:, ro[None, :], coff]
        s_loc = jnp.einsum('bqhd,bkhd->bhqk', qi, ki, preferred_element_type=jnp.float32) * scale + bias[None]
        s_loc = jnp.where(col_ok[None, None], s_loc, NEG_INF)
        s_ctx = jnp.einsum('bqhd,bphd->bhqp', qi, ck, preferred_element_type=jnp.float32) * scale
        p_loc, p_ctx = softmax_parts([s_loc, s_ctx])
        return (jnp.einsum('bhqk,bkhd->bqhd', p_loc.astype(vi.dtype), vi)
                + jnp.einsum('bhqp,bphd->bqhd', p_ctx.astype(cv.dtype), cv))

    o = lax.map(one, (qr, idx, roff))
    return o.swapaxes(0, 1).reshape(B, T, H, hd)


def expert_choice_ffn(h, w_router, w_gate, w_up, w_down):
    B, T, D = h.shape
    cap = EC_CAPACITY * T // N_EXPERTS
    aff = jax.nn.softmax(jnp.einsum('btd,de->bte', h, w_router, preferred_element_type=jnp.float32), axis=-1)
    gate, idx = lax.top_k(aff.swapaxes(1, 2), cap)
    xg = jax.vmap(lambda hb, ib: hb[ib])(h, idx)
    a = jnp.einsum('becd,edf->becf', xg, w_gate)
    u = jnp.einsum('becd,edf->becf', xg, w_up)
    y = jnp.einsum('becf,efd->becd', jax.nn.silu(a) * u, w_down) * gate[..., None].astype(h.dtype)
    return jax.vmap(lambda yb, ib: jnp.zeros((T, D), yb.dtype).at[ib.reshape(-1)].add(yb.reshape(-1, D)))(y, idx)


def split_even(z):
    B, L = z.shape[:2]
    q = z[..., :A_Q_W].reshape(B, L, A_HEADS, HEAD_DIM)
    k = z[..., A_Q_W:A_Q_W + A_KV_W].reshape(B, L, A_KV_HEADS, HEAD_DIM)
    v = z[..., A_Q_W + A_KV_W:A_Q_W + 2 * A_KV_W].reshape(B, L, A_KV_HEADS, HEAD_DIM)
    return q, k, v, z[..., A_Q_W + 2 * A_KV_W:]


def split_odd(z):
    B, L = z.shape[:2]
    zq, zk, zv, zg = [z[..., i * C_W:(i + 1) * C_W] for i in range(4)]
    off = 4 * C_W
    za = z[..., off:off + 2 * C_HEADS].reshape(B, L, 2, C_HEADS)
    zb = z[..., off + 2 * C_HEADS:off + 4 * C_HEADS].reshape(B, L, 2, C_HEADS)
    off = off + 4 * C_HEADS
    nq, nk, nv = [z[..., off + i * D_W:off + (i + 1) * D_W].reshape(B, L, D_HEADS, HEAD_DIM) for i in range(3)]
    return zq, zk, zv, zg, za, zb, nq, nk, nv


def kernel(x_prompt, x_sample, cache_attn_k, cache_attn_v, state_delta, cache_na_k, cache_na_v,
           c, c_ctx, w_ada, b_ada, norm_mix, norm_ffn, norm_final,
           even_w_in, even_w_out, attn_sink, hy_conv_w, hy_conv_b, hy_w1, hy_b1, hy_w2, hy_b2,
           hy_w3, hy_freq, hy_skip, odd_w_in, odd_w_out, gdn_conv_w, gdn_a_log, gdn_dt_bias,
           gdn_norm, na_rpb, moe_router, moe_w_gate, moe_w_up, moe_w_down):
    xp, xs = x_prompt, x_sample
    bp = xp.shape[0]
    cos, sin = axial_rope(xs.shape[1])
    new_ak, new_av, new_st, new_nk, new_nv = [], [], [], [], []
    for l in range(DEPTH):
        j = l // 2
        mp = ada_params(c_ctx[None, :], w_ada[l], b_ada[l])
        ms = ada_params(c, w_ada[l], b_ada[l])
        hp = modulate(xp, norm_mix[l], mp[0], mp[1])
        hs = modulate(xs, norm_mix[l], ms[0], ms[1])
        if l % 2 == 0:
            hy = (hy_conv_w[j], hy_conv_b[j], hy_w1[j], hy_b1[j], hy_w2[j], hy_b2[j], hy_w3[j], hy_freq[j], hy_skip[j])
            q, k, v, zh = split_even(proj(hp, even_w_in[j]))
            oa = context_attention(q, k, v, attn_sink[j])
            mix_p = proj(jnp.concatenate([oa.reshape(oa.shape[:2] + (A_Q_W,)), hyena_mixer(zh, *hy)], axis=-1), even_w_out[j])
            new_ak.append(k)
            new_av.append(v)
            q, k, v, zh = split_even(proj(hs, even_w_in[j]))
            q, k = apply_rope(q, cos, sin), apply_rope(k, cos, sin)
            oa = window_attention(q, k, v, cache_attn_k[:, j], cache_attn_v[:, j], attn_sink[j])
            mix_s = proj(jnp.concatenate([oa.reshape(oa.shape[:2] + (A_Q_W,)), hyena_mixer(zh, *hy)], axis=-1), even_w_out[j])
        else:
            gdn = (gdn_conv_w[j], gdn_a_log[j], gdn_dt_bias[j], gdn_norm[j])
            zq, zk, zv, zg, za, zb, nq, nk, nv = split_odd(proj(hp, odd_w_in[j]))
            s0 = jnp.zeros((bp, 2, C_HEADS, HEAD_DIM, HEAD_DIM), jnp.float32)
            oc, st = deltanet_mixer(zq, zk, zv, zg, za, zb, *gdn, s0)
            od = context_attention(nq, nk, nv, None)
            mix_p = proj(jnp.concatenate([oc, od.reshape(od.shape[:2] + (D_W,))], axis=-1), odd_w_out[j])
            new_st.append(st)
            new_nk.append(nk)
            new_nv.append(nv)
            zq, zk, zv, zg, za, zb, nq, nk, nv = split_odd(proj(hs, odd_w_in[j]))
            oc, _ = deltanet_mixer(zq, zk, zv, zg, za, zb, *gdn, state_delta[:, j])
            od = neighbourhood_attention(nq, nk, nv, cache_na_k[:, j], cache_na_v[:, j], na_rpb[j])
            mix_s = proj(jnp.concatenate([oc, od.reshape(od.shape[:2] + (D_W,))], axis=-1), odd_w_out[j])
        xp = xp + mp[2] * mix_p
        xs = xs + ms[2] * mix_s
        moe = (moe_router[l], moe_w_gate[l], moe_w_up[l], moe_w_down[l])
        xp = xp + mp[5] * expert_choice_ffn(modulate(xp, norm_ffn[l], mp[3], mp[4]), *moe)
        xs = xs + ms[5] * expert_choice_ffn(modulate(xs, norm_ffn[l], ms[3], ms[4]), *moe)
    y_prompt = rms_norm(xp, norm_final)
    y_sample = rms_norm(xs, norm_final)
    return (y_prompt, y_sample, jnp.stack(new_ak, axis=1), jnp.stack(new_av, axis=1), jnp.stack(new_st, axis=1),
            jnp.stack(new_nk, axis=1), jnp.stack(new_nv, axis=1))
```

```python
import functools
import math
import jax, jax.numpy as jnp
from jax import lax
import numpy as np
from jax.experimental import pallas as pl
from jax.experimental.pallas import tpu as pltpu

D_MODEL = 1024
BATCH = 32
SEQ = 256
DEPTH = 4
DEC_BATCH = 4
DEC_SEQ = 4096
PAST_LEN = 512

GRID_W = 64
HEAD_DIM = 64
N_EVEN = (DEPTH + 1) // 2
N_ODD = DEPTH // 2
Q_BLOCK = 128
A_HEADS = D_MODEL // 128
A_KV_HEADS = A_HEADS // 4
A_WINDOW = 128
A_BLOCK = 128
ROPE_BASE = 10000.0
HY_CH = D_MODEL // 2
HY_ORDER = 2
HY_SHORT = 3
HY_EMB = 33
HY_FILT_W = 64
HY_FAST_DECAY = 0.3
HY_SLOW_DECAY = 1.5
HY_DECAY_TARGET = 1e-2
HY_SHIFT = 0.05
C_HEADS = D_MODEL // 128
C_SHORT = 3
C_CHUNK = 64
D_HEADS = D_MODEL // 128
NA_KH_MAX = 8
NA_KW = 16
N_EXPERTS = 16
EC_CAPACITY = 2
MOE_D_FF = D_MODEL
EPS = 1e-6
NEG_INF = -1e30

A_Q_W = A_HEADS * HEAD_DIM
A_KV_W = A_KV_HEADS * HEAD_DIM
EVEN_IN = A_Q_W + 2 * A_KV_W + 3 * HY_CH
EVEN_MIX = A_Q_W + HY_CH
C_W = C_HEADS * HEAD_DIM
D_W = D_HEADS * HEAD_DIM
ODD_IN = 4 * C_W + 4 * C_HEADS + 3 * D_W
ODD_MIX = C_W + D_W

VMEM_LIMIT_BYTES = 48 * 1024 * 1024


def _mm_kernel(x_ref, w_ref, o_ref):
    o_ref[...] = jnp.dot(x_ref[...].astype(jnp.bfloat16), w_ref[...], preferred_element_type=jnp.float32)


def pallas_matmul(x, w, tm=256):
    M, K = x.shape
    N = w.shape[1]
    assert M % tm == 0
    return pl.pallas_call(
        _mm_kernel,
        grid=(M // tm,),
        in_specs=[pl.BlockSpec((tm, K), lambda i: (i, 0)), pl.BlockSpec((K, N), lambda i: (0, 0))],
        out_specs=pl.BlockSpec((tm, N), lambda i: (i, 0)),
        out_shape=jax.ShapeDtypeStruct((M, N), jnp.float32),
        compiler_params=pltpu.CompilerParams(dimension_semantics=("arbitrary",), vmem_limit_bytes=VMEM_LIMIT_BYTES),
    )(x, w.astype(jnp.bfloat16))


def proj(x, w):
    B, L, K = x.shape
    return pallas_matmul(x.reshape(B * L, K), w).reshape(B, L, w.shape[1])


def _mm2_kernel(a_ref, b_ref, wa_ref, wb_ref, o_ref):
    o_ref[...] = (jnp.dot(a_ref[...].astype(jnp.bfloat16), wa_ref[...], preferred_element_type=jnp.float32)
                  + jnp.dot(b_ref[...].astype(jnp.bfloat16), wb_ref[...], preferred_element_type=jnp.float32))


def proj_concat(a, b, w, tm=512):
    B, L, Ka = a.shape
    Kb = b.shape[2]
    N = w.shape[1]
    M = B * L
    assert M % tm == 0 and w.shape[0] == Ka + Kb
    wb16 = w.astype(jnp.bfloat16)
    out = pl.pallas_call(
        _mm2_kernel,
        grid=(M // tm,),
        in_specs=[pl.BlockSpec((tm, Ka), lambda i: (i, 0)), pl.BlockSpec((tm, Kb), lambda i: (i, 0)),
                  pl.BlockSpec((Ka, N), lambda i: (0, 0)), pl.BlockSpec((Kb, N), lambda i: (0, 0))],
        out_specs=pl.BlockSpec((tm, N), lambda i: (i, 0)),
        out_shape=jax.ShapeDtypeStruct((M, N), jnp.float32),
        compiler_params=pltpu.CompilerParams(dimension_semantics=("arbitrary",), vmem_limit_bytes=VMEM_LIMIT_BYTES),
        name="out_projection",
    )(a.reshape(M, Ka), b.reshape(M, Kb), wb16[:Ka], wb16[Ka:])
    return out.reshape(B, L, N)


LANES = 128
BF16 = jnp.bfloat16
F32 = jnp.float32


def _dot_nt(a, b):
    return lax.dot_general(a, b, (((1,), (1,)), ((), ())), preferred_element_type=F32)


def _dot(a, b):
    return jnp.dot(a, b, preferred_element_type=F32)


def _low_half(shape):
    return lax.broadcasted_iota(jnp.int32, shape, 1) < HEAD_DIM


def _softmax_pv(scores, values, extra_logit=None):
    m = scores[0].max(axis=-1, keepdims=True)
    for s in scores[1:]:
        m = jnp.maximum(m, s.max(axis=-1, keepdims=True))
    if extra_logit is not None:
        m = jnp.maximum(m, extra_logit)
    l = None if extra_logit is None else jnp.exp(extra_logit - m)
    o = None
    for s, v in zip(scores, values):
        p = jnp.exp(s - m)
        ps = p.sum(axis=-1, keepdims=True)
        l = ps if l is None else l + ps
        pv = _dot(p.astype(BF16), v)
        o = pv if o is None else o + pv
    return o / l


def _place_head(q_slab, src_half, dst_half, low):
    x = q_slab if src_half == dst_half else pltpu.roll(q_slab, HEAD_DIM, axis=1)
    return jnp.where(low if dst_half == 0 else ~low, x, jnp.zeros_like(x))


NA_KEYS = NA_KH_MAX * GRID_W


def _na_kernel(q_ref, k_ref, v_ref, ck_ref, cv_ref, tab_ref, o_ref):
    r = pl.program_id(1)
    rows = k_ref.shape[1] // GRID_W
    rs = jnp.clip(r - NA_KH_MAX // 2, 0, rows - NA_KH_MAX)
    start = pl.multiple_of(rs * GRID_W, GRID_W)
    scale = HEAD_DIM ** -0.5
    low = _low_half((GRID_W, LANES))
    for p in range(D_HEADS // 2):
        cols = slice(p * LANES, (p + 1) * LANES)
        qp = q_ref[0, :, cols]
        kp = k_ref[0, pl.ds(start, NA_KEYS), cols]
        vp = v_ref[0, pl.ds(start, NA_KEYS), cols]
        ckp = ck_ref[0, :, cols]
        cvp = cv_ref[0, :, cols]
        outs = []
        for half in range(2):
            qm = jnp.where(low if half == 0 else ~low, qp, jnp.zeros_like(qp))
            s_loc = _dot_nt(qm, kp) * scale + tab_ref[2 * p + half, 0]
            s_ctx = _dot_nt(qm, ckp) * scale
            outs.append(_softmax_pv([s_loc, s_ctx], [vp, cvp]))
        o_ref[0, :, cols] = jnp.where(low, outs[0], outs[1]).astype(o_ref.dtype)


def na_bias_table(rpb):
    col = jnp.arange(GRID_W)
    cs = jnp.clip(col - NA_KW // 2, 0, GRID_W - NA_KW)
    col_ok = (col[None, :] >= cs[:, None]) & (col[None, :] < cs[:, None] + NA_KW)
    coff = jnp.clip(col[None, :] - col[:, None] + NA_KW - 1, 0, 2 * NA_KW - 2)
    base = jnp.where(col_ok[None, None], rpb.astype(F32)[:, :, coff], NEG_INF)
    tab = jnp.stack([base[:, o:o + NA_KH_MAX] for o in range(NA_KH_MAX)], axis=1)
    return tab.transpose(0, 1, 3, 2, 4).reshape(D_HEADS, NA_KH_MAX, GRID_W, NA_KEYS)


def neighbourhood_attention_pallas(q, k, v, ck, cv, rpb):
    B, T, W = q.shape
    P = ck.shape[1]
    rows = T // GRID_W
    assert rows >= NA_KH_MAX and W == D_W
    tab = na_bias_table(rpb)

    def tab_index(b, r):
        rs = jnp.clip(r - NA_KH_MAX // 2, 0, rows - NA_KH_MAX)
        return (0, rs - r + NA_KH_MAX - 1, 0, 0)

    return pl.pallas_call(
        _na_kernel,
        grid=(B, rows),
        in_specs=[
            pl.BlockSpec((1, GRID_W, W), lambda b, r: (b, r, 0)),
            pl.BlockSpec((1, T, W), lambda b, r: (b, 0, 0)),
            pl.BlockSpec((1, T, W), lambda b, r: (b, 0, 0)),
            pl.BlockSpec((1, P, W), lambda b, r: (b, 0, 0)),
            pl.BlockSpec((1, P, W), lambda b, r: (b, 0, 0)),
            pl.BlockSpec((D_HEADS, 1, GRID_W, NA_KEYS), tab_index),
        ],
        out_specs=pl.BlockSpec((1, GRID_W, W), lambda b, r: (b, r, 0)),
        out_shape=jax.ShapeDtypeStruct((B, T, W), BF16),
        compiler_params=pltpu.CompilerParams(dimension_semantics=("arbitrary", "arbitrary"),
                                             vmem_limit_bytes=VMEM_LIMIT_BYTES),
        name="na_attention",
    )(q, k, v, ck, cv, tab)


def rope_tables(T):
    cos, sin = axial_rope(T)
    cos, sin = cos[:, 0, :], sin[:, 0, :]
    cos_t = jnp.concatenate([cos, cos, cos, cos], axis=-1)
    sin_t = jnp.concatenate([-sin, sin, -sin, sin], axis=-1)
    return cos_t, sin_t


def _rope(x, cos_t, sin_t):
    half = HEAD_DIM // 2
    lane = lax.broadcasted_iota(jnp.int32, x.shape, 1)
    first = (lane % HEAD_DIM) < half
    swapped = jnp.where(first, pltpu.roll(x, LANES - half, axis=1), pltpu.roll(x, half, axis=1))
    return x * cos_t + swapped * sin_t


def _win_kernel(sink_ref, q_ref, k_ref, v_ref, ck_ref, cv_ref, cos_ref, sin_ref, o_ref):
    i = pl.program_id(1)
    T = k_ref.shape[1]
    span = 3 * A_BLOCK
    start = pl.multiple_of(jnp.clip((i - 1) * A_BLOCK, 0, T - span), A_BLOCK)
    delta = i * A_BLOCK - start
    q0 = pl.multiple_of(i * A_BLOCK, A_BLOCK)
    scale = HEAD_DIM ** -0.5
    kw = _rope(k_ref[0, pl.ds(start, span), :], cos_ref[pl.ds(start, span), :], sin_ref[pl.ds(start, span), :]).astype(BF16)
    vw = v_ref[0, pl.ds(start, span), :].astype(BF16)
    ck = ck_ref[0]
    cv = cv_ref[0]
    cos_q = cos_ref[pl.ds(q0, A_BLOCK), :]
    sin_q = sin_ref[pl.ds(q0, A_BLOCK), :]
    qi = lax.broadcasted_iota(jnp.int32, (A_BLOCK, span), 0)
    kj = lax.broadcasted_iota(jnp.int32, (A_BLOCK, span), 1)
    band = jnp.abs(kj - delta - qi) <= A_WINDOW
    low = _low_half((A_BLOCK, LANES))
    group = A_HEADS // A_KV_HEADS
    for p in range(A_HEADS // 2):
        cols = slice(p * LANES, (p + 1) * LANES)
        q_slab = _rope(q_ref[0, :, cols], cos_q, sin_q)
        outs = []
        for half in range(2):
            h = 2 * p + half
            kv = h // group
            qm = _place_head(q_slab, half, kv, low).astype(BF16)
            s_loc = jnp.where(band, _dot_nt(qm, kw) * scale, NEG_INF)
            s_ctx = _dot_nt(qm, ck) * scale
            o = _softmax_pv([s_loc, s_ctx], [vw, cv], extra_logit=sink_ref[h])
            outs.append(o if kv == half else pltpu.roll(o, HEAD_DIM, axis=1))
        o_ref[0, :, cols] = jnp.where(low, outs[0], outs[1]).astype(o_ref.dtype)


def window_attention_pallas(q, k, v, ck, cv, sink):
    B, T, QW = q.shape
    KW = k.shape[2]
    P = ck.shape[1]
    assert KW == LANES and QW == A_Q_W and T % A_BLOCK == 0 and T >= 3 * A_BLOCK
    cos_t, sin_t = rope_tables(T)
    return pl.pallas_call(
        _win_kernel,
        grid=(B, T // A_BLOCK),
        in_specs=[
            pl.BlockSpec(memory_space=pltpu.SMEM),
            pl.BlockSpec((1, A_BLOCK, QW), lambda b, i: (b, i, 0)),
            pl.BlockSpec((1, T, KW), lambda b, i: (b, 0, 0)),
            pl.BlockSpec((1, T, KW), lambda b, i: (b, 0, 0)),
            pl.BlockSpec((1, P, KW), lambda b, i: (b, 0, 0)),
            pl.BlockSpec((1, P, KW), lambda b, i: (b, 0, 0)),
            pl.BlockSpec((T, LANES), lambda b, i: (0, 0)),
            pl.BlockSpec((T, LANES), lambda b, i: (0, 0)),
        ],
        out_specs=pl.BlockSpec((1, A_BLOCK, QW), lambda b, i: (b, i, 0)),
        out_shape=jax.ShapeDtypeStruct((B, T, QW), BF16),
        compiler_params=pltpu.CompilerParams(dimension_semantics=("arbitrary", "arbitrary"),
                                             vmem_limit_bytes=VMEM_LIMIT_BYTES),
        name="window_attention",
    )(sink.astype(F32), q, k, v, ck, cv, cos_t, sin_t)


def _ctx_kernel(sink_ref, q_ref, k_ref, v_ref, o_ref, *, n_q_heads, n_kv_heads, use_sink):
    S = q_ref.shape[1]
    scale = HEAD_DIM ** -0.5
    low = _low_half((S, LANES))
    group = n_q_heads // n_kv_heads
    for p in range(n_q_heads // 2):
        cols = slice(p * LANES, (p + 1) * LANES)
        q_slab = q_ref[0, :, cols]
        outs = []
        for half in range(2):
            h = 2 * p + half
            kv = h // group
            kcols = slice((kv // 2) * LANES, (kv // 2 + 1) * LANES)
            qm = _place_head(q_slab, half, kv % 2, low).astype(BF16)
            s = _dot_nt(qm, k_ref[0, :, kcols].astype(BF16)) * scale
            o = _softmax_pv([s], [v_ref[0, :, kcols].astype(BF16)], extra_logit=sink_ref[h] if use_sink else None)
            outs.append(o if kv % 2 == half else pltpu.roll(o, HEAD_DIM, axis=1))
        o_ref[0, :, cols] = jnp.where(low, outs[0], outs[1]).astype(o_ref.dtype)


def context_attention_pallas(q, k, v, sink, n_q_heads, n_kv_heads):
    B, S, QW = q.shape
    KW = k.shape[2]
    use_sink = sink is not None
    sink_arr = sink.astype(F32) if use_sink else jnp.zeros((n_q_heads,), F32)
    return pl.pallas_call(
        functools.partial(_ctx_kernel, n_q_heads=n_q_heads, n_kv_heads=n_kv_heads, use_sink=use_sink),
        grid=(B,),
        in_specs=[
            pl.BlockSpec(memory_space=pltpu.SMEM),
            pl.BlockSpec((1, S, QW), lambda b: (b, 0, 0)),
            pl.BlockSpec((1, S, KW), lambda b: (b, 0, 0)),
            pl.BlockSpec((1, S, KW), lambda b: (b, 0, 0)),
        ],
        out_specs=pl.BlockSpec((1, S, QW), lambda b: (b, 0, 0)),
        out_shape=jax.ShapeDtypeStruct((B, S, QW), BF16),
        compiler_params=pltpu.CompilerParams(dimension_semantics=("arbitrary",), vmem_limit_bytes=VMEM_LIMIT_BYTES),
        name="context_attention",
    )(sink_arr, q, k, v)


def rms_norm(x, g):
    xf = x.astype(jnp.float32)
    y = xf * lax.rsqrt(jnp.mean(xf * xf, axis=-1, keepdims=True) + EPS)
    return (y * g.astype(jnp.float32)).astype(x.dtype)


def ada_params(cond, w, b):
    m = jax.nn.silu(cond) @ w + b
    return jnp.split(m[:, None, :], 6, axis=-1)


def modulate(x, g, shift, scale):
    return rms_norm(x, g) * (1 + scale) + shift


def axial_rope(T):
    t = jnp.arange(T)
    n_freq = HEAD_DIM // 4
    inv = ROPE_BASE ** (-jnp.arange(n_freq, dtype=jnp.float32) / n_freq)
    ang = jnp.concatenate([(t // GRID_W).astype(jnp.float32)[:, None] * inv,
                           (t % GRID_W).astype(jnp.float32)[:, None] * inv], axis=-1)
    return jnp.cos(ang)[:, None, :], jnp.sin(ang)[:, None, :]


def apply_rope(x, cos, sin):
    xf = x.astype(jnp.float32)
    x1, x2 = jnp.split(xf, 2, axis=-1)
    return jnp.concatenate([x1 * cos - x2 * sin, x2 * cos + x1 * sin], axis=-1).astype(x.dtype)


def softmax_parts(parts, sink=None):
    sizes = [p.shape[-1] for p in parts]
    cols = list(parts)
    if sink is not None:
        cols.append(jnp.broadcast_to(sink, parts[0].shape[:-1] + (1,)))
    p = jax.nn.softmax(jnp.concatenate(cols, axis=-1), axis=-1)
    pieces = jnp.split(p, np.cumsum(sizes).tolist(), axis=-1)
    return pieces[:len(sizes)]


def context_attention(q, k, v, sink):
    B, S, HQ, hd = q.shape
    HK = k.shape[2]
    G = HQ // HK
    nb = S // Q_BLOCK
    scale = hd ** -0.5
    sink_b = None if sink is None else sink.astype(jnp.float32).reshape(1, HK, G, 1, 1)
    qb = q.reshape(B, nb, Q_BLOCK, HK, G, hd).swapaxes(0, 1)

    def one(qi):
        s = jnp.einsum('bqkgd,bskd->bkgqs', qi, k, preferred_element_type=jnp.float32) * scale
        (p,) = softmax_parts([s], sink_b)
        return jnp.einsum('bkgqs,bskd->bqkgd', p.astype(v.dtype), v)

    o = lax.map(one, qb)
    return o.swapaxes(0, 1).reshape(B, S, HQ, hd)


def window_attention(q, k, v, ck, cv, sink):
    B, T, HQ, hd = q.shape
    HK = k.shape[2]
    G = HQ // HK
    nb = T // A_BLOCK
    scale = hd ** -0.5
    qb = q.reshape(B, nb, A_BLOCK, HK, G, hd).swapaxes(0, 1)

    def band(x):
        xp = jnp.pad(x, ((0, 0), (A_BLOCK, A_BLOCK), (0, 0), (0, 0))).reshape(B, nb + 2, A_BLOCK, HK, hd)
        return jnp.concatenate([xp[:, :-2], xp[:, 1:-1], xp[:, 2:]], axis=2).swapaxes(0, 1)

    kb, vb = band(k), band(v)
    qpos = jnp.arange(nb)[:, None, None] * A_BLOCK + jnp.arange(A_BLOCK)[None, :, None]
    kpos = jnp.arange(nb)[:, None, None] * A_BLOCK - A_BLOCK + jnp.arange(3 * A_BLOCK)[None, None, :]
    mask = (jnp.abs(kpos - qpos) <= A_WINDOW) & (kpos >= 0) & (kpos < T)
    sink_b = sink.astype(jnp.float32).reshape(1, HK, G, 1, 1)

    def one(xs):
        qi, ki, vi, mi = xs
        s_loc = jnp.einsum('bqkgd,bskd->bkgqs', qi, ki, preferred_element_type=jnp.float32) * scale
        s_loc = jnp.where(mi[None, None, None], s_loc, NEG_INF)
        s_ctx = jnp.einsum('bqkgd,bpkd->bkgqp', qi, ck, preferred_element_type=jnp.float32) * scale
        p_loc, p_ctx = softmax_parts([s_loc, s_ctx], sink_b)
        return (jnp.einsum('bkgqs,bskd->bqkgd', p_loc.astype(vi.dtype), vi)
                + jnp.einsum('bkgqp,bpkd->bqkgd', p_ctx.astype(cv.dtype), cv))

    o = lax.map(one, (qb, kb, vb, mask))
    return o.swapaxes(0, 1).reshape(B, T, HQ, hd)


def short_conv(x, w):
    K = w.shape[0]
    L = x.shape[1]
    pad = K // 2
    xp = jnp.pad(x, ((0, 0), (pad, pad), (0, 0)))
    return sum(xp[:, i:i + L] * w[i] for i in range(K))


def hyena_filter_bank(L, w1, b1, w2, b2, w3, freq):
    f32 = jnp.float32
    t = jnp.linspace(0.0, 1.0, L, dtype=f32)[:, None]
    bands = (HY_EMB - 1) // 2
    omega = 2.0 * math.pi * jnp.arange(L, dtype=f32)[:, None] / L
    fb = jnp.linspace(1e-4, bands - 1, bands, dtype=f32)[None, :]
    feats = jnp.concatenate([t, jnp.cos(fb * omega), -jnp.sin(fb * omega)], axis=-1)
    fr = freq.astype(f32)
    h = jnp.sin(fr * (feats @ w1.astype(f32) + b1.astype(f32)))
    h = jnp.sin(fr * (h @ w2.astype(f32) + b2.astype(f32)))
    h = (h @ w3.astype(f32)).reshape(L, 2, HY_ORDER, HY_CH)
    max_decay = math.log(HY_DECAY_TARGET) / HY_FAST_DECAY
    min_decay = math.log(HY_DECAY_TARGET) / HY_SLOW_DECAY
    deltas = jnp.abs(jnp.linspace(min_decay, max_decay, HY_CH, dtype=f32))
    h = h * (jnp.exp(-t * deltas) + HY_SHIFT)[:, None, None, :]
    taps = jnp.concatenate([h[:, 0], jnp.zeros((1, HY_ORDER, HY_CH), f32), h[:0:-1, 1]], axis=0)
    taps = taps / jnp.sum(jnp.abs(taps), axis=0, keepdims=True)
    return jnp.fft.rfft(taps, axis=0)


def hyena_mixer(z, conv_w, conv_b, w1, b1, w2, b2, w3, freq, skip):
    L = z.shape[1]
    zf = (short_conv(z, conv_w) + conv_b).astype(jnp.float32)
    v, x1, x2 = jnp.split(zf, 3, axis=-1)
    filt = hyena_filter_bank(L, w1, b1, w2, b2, w3, freq)
    skip = skip.astype(jnp.float32)

    def long_conv(u, o):
        y = jnp.fft.irfft(jnp.fft.rfft(u, n=2 * L, axis=1) * filt[None, :, o], n=2 * L, axis=1)[:, :L]
        return y + u * skip[o]

    y = x1 * long_conv(v, 0)
    y = x2 * long_conv(y, 1)
    return y.astype(z.dtype)


def l2norm(x):
    xf = x.astype(jnp.float32)
    return xf * lax.rsqrt(jnp.sum(xf * xf, axis=-1, keepdims=True) + EPS)


def chunk_gated_delta(q, k, v, g, beta, s0):
    B, L, H, dk = q.shape
    dv = v.shape[-1]
    n = L // C_CHUNK

    def chunks(x):
        x = x.reshape((B, n, C_CHUNK, H) + x.shape[3:])
        return jnp.moveaxis(jnp.moveaxis(x, 1, 0), 3, 2)

    qc, kc, vc, bc = chunks(q), chunks(k), chunks(v), chunks(beta)
    gc = jnp.cumsum(chunks(g), axis=-1)
    tri = jnp.tril(jnp.ones((C_CHUNK, C_CHUNK), bool))
    strict = jnp.tril(jnp.ones((C_CHUNK, C_CHUNK), bool), k=-1)
    gamma = jnp.exp(jnp.where(tri, gc[..., :, None] - gc[..., None, :], NEG_INF))
    kb = kc * bc[..., None]
    a_mat = jnp.where(strict, jnp.einsum('nbhid,nbhjd->nbhij', kb, kc) * gamma, 0.0) + jnp.eye(C_CHUNK, dtype=jnp.float32)
    rhs = jnp.concatenate([vc * bc[..., None], kb * jnp.exp(gc)[..., None]], axis=-1)
    sol = lax.linalg.triangular_solve(a_mat, rhs, left_side=True, lower=True)
    u, w = sol[..., :dv], sol[..., dv:]
    attn = jnp.where(tri, jnp.einsum('nbhid,nbhjd->nbhij', qc, kc) * gamma, 0.0)
    g_last = gc[..., -1]
    q_dec = qc * jnp.exp(gc)[..., None]
    k_dec = kc * jnp.exp(g_last[..., None] - gc)[..., None]

    def step(S, xs):
        u_i, w_i, a_i, qd, kd, gl = xs
        v_new = u_i - jnp.einsum('bhck,bhkv->bhcv', w_i, S)
        o = jnp.einsum('bhck,bhkv->bhcv', qd, S) + jnp.einsum('bhij,bhjv->bhiv', a_i, v_new)
        S = S * jnp.exp(gl)[..., None, None] + jnp.einsum('bhck,bhcv->bhkv', kd, v_new)
        return S, o

    S, o = lax.scan(step, s0.astype(jnp.float32), (u, w, attn, q_dec, k_dec, g_last))
    o = jnp.moveaxis(jnp.moveaxis(o, 2, 3), 0, 1).reshape(B, L, H, dv)
    return o, S


def deltanet_mixer(zq, zk, zv, zg, za, zb, conv_w, a_log, dt_bias, norm_w, s0):
    B, L, _ = zq.shape
    qkv = jax.nn.silu(short_conv(jnp.concatenate([zq, zk, zv], axis=-1), conv_w))
    q, k, v = [t.reshape(B, L, C_HEADS, HEAD_DIM) for t in jnp.split(qkv, 3, axis=-1)]
    q = l2norm(q) * (HEAD_DIM ** -0.5)
    k = l2norm(k)
    v = v.astype(jnp.float32)
    beta = jax.nn.sigmoid(zb.astype(jnp.float32))
    g = -jnp.exp(a_log.astype(jnp.float32)) * jax.nn.softplus(za.astype(jnp.float32) + dt_bias.astype(jnp.float32))
    o_f, s_f = chunk_gated_delta(q, k, v, g[:, :, 0], beta[:, :, 0], s0[:, 0])
    o_b, s_b = chunk_gated_delta(q[:, ::-1], k[:, ::-1], v[:, ::-1], g[:, ::-1, 1], beta[:, ::-1, 1], s0[:, 1])
    o = o_f + o_b[:, ::-1]
    gate = jax.nn.silu(zg.reshape(B, L, C_HEADS, HEAD_DIM).astype(jnp.float32))
    o = rms_norm(o, norm_w) * gate
    return o.reshape(B, L, C_W).astype(zq.dtype), jnp.stack([s_f, s_b], axis=1)


def neighbourhood_attention(q, k, v, ck, cv, rpb):
    B, T, H, hd = q.shape
    rows = T // GRID_W
    kh = min(NA_KH_MAX, rows)
    scale = hd ** -0.5
    r = jnp.arange(rows)
    rs = jnp.clip(r - kh // 2, 0, rows - kh)
    key_rows = rs[:, None] + jnp.arange(kh)[None, :]
    idx = (key_rows[:, :, None] * GRID_W + jnp.arange(GRID_W)).reshape(rows, kh * GRID_W)
    col = jnp.arange(GRID_W)
    cs = jnp.clip(col - NA_KW // 2, 0, GRID_W - NA_KW)
    kcol = jnp.tile(col, kh)
    col_ok = (kcol[None, :] >= cs[:, None]) & (kcol[None, :] < cs[:, None] + NA_KW)
    roff = jnp.repeat(key_rows - r[:, None], GRID_W, axis=1) + NA_KH_MAX - 1
    coff = jnp.clip(kcol[None, :] - col[:, None] + NA_KW - 1, 0, 2 * NA_KW - 2)
    qr = q.reshape(B, rows, GRID_W, H, hd).swapaxes(0, 1)
    rpb_f = rpb.astype(jnp.float32)

    def one(xs):
        qi, ii, ro = xs
        ki = k[:, ii]
        vi = v[:, ii]
        bias = rpb_f[:, ro[None, :], coff]
        s_loc = jnp.einsum('bqhd,bkhd->bhqk', qi, ki, preferred_element_type=jnp.float32) * scale + bias[None]
        s_loc = jnp.where(col_ok[None, None], s_loc, NEG_INF)
        s_ctx = jnp.einsum('bqhd,bphd->bhqp', qi, ck, preferred_element_type=jnp.float32) * scale
        p_loc, p_ctx = softmax_parts([s_loc, s_ctx])
        return (jnp.einsum('bhqk,bkhd->bqhd', p_loc.astype(vi.dtype), vi)
                + jnp.einsum('bhqp,bphd->bqhd', p_ctx.astype(cv.dtype), cv))

    o = lax.map(one, (qr, idx, roff))
    return o.swapaxes(0, 1).reshape(B, T, H, hd)


def expert_choice_ffn(h, w_router, w_gate, w_up, w_down):
    B, T, D = h.shape
    cap = EC_CAPACITY * T // N_EXPERTS
    aff = jax.nn.softmax(jnp.einsum('btd,de->bte', h, w_router, preferred_element_type=jnp.float32), axis=-1)
    gate, idx = lax.top_k(aff.swapaxes(1, 2), cap)
    xg = jax.vmap(lambda hb, ib: hb[ib])(h, idx)
    a = jnp.einsum('becd,edf->becf', xg, w_gate)
    u = jnp.einsum('becd,edf->becf', xg, w_up)
    y = jnp.einsum('becf,efd->becd', jax.nn.silu(a) * u, w_down) * gate[..., None].astype(h.dtype)
    return jax.vmap(lambda yb, ib: jnp.zeros((T, D), yb.dtype).at[ib.reshape(-1)].add(yb.reshape(-1, D)))(y, idx)


def split_even(z):
    B, L = z.shape[:2]
    q = z[..., :A_Q_W].reshape(B, L, A_HEADS, HEAD_DIM)
    k = z[..., A_Q_W:A_Q_W + A_KV_W].reshape(B, L, A_KV_HEADS, HEAD_DIM)
    v = z[..., A_Q_W + A_KV_W:A_Q_W + 2 * A_KV_W].reshape(B, L, A_KV_HEADS, HEAD_DIM)
    return q, k, v, z[..., A_Q_W + 2 * A_KV_W:]


def split_odd(z):
    B, L = z.shape[:2]
    zq, zk, zv, zg = [z[..., i * C_W:(i + 1) * C_W] for i in range(4)]
    off = 4 * C_W
    za = z[..., off:off + 2 * C_HEADS].reshape(B, L, 2, C_HEADS)
    zb = z[..., off + 2 * C_HEADS:off + 4 * C_HEADS].reshape(B, L, 2, C_HEADS)
    off = off + 4 * C_HEADS
    nq, nk, nv = [z[..., off + i * D_W:off + (i + 1) * D_W].reshape(B, L, D_HEADS, HEAD_DIM) for i in range(3)]
    return zq, zk, zv, zg, za, zb, nq, nk, nv


def kernel(x_prompt, x_sample, cache_attn_k, cache_attn_v, state_delta, cache_na_k, cache_na_v,
           c, c_ctx, w_ada, b_ada, norm_mix, norm_ffn, norm_final,
           even_w_in, even_w_out, attn_sink, hy_conv_w, hy_conv_b, hy_w1, hy_b1, hy_w2, hy_b2,
           hy_w3, hy_freq, hy_skip, odd_w_in, odd_w_out, gdn_conv_w, gdn_a_log, gdn_dt_bias,
           gdn_norm, na_rpb, moe_router, moe_w_gate, moe_w_up, moe_w_down):
    xp, xs = x_prompt, x_sample
    bp = xp.shape[0]
    cos, sin = axial_rope(xs.shape[1])
    new_ak, new_av, new_st, new_nk, new_nv = [], [], [], [], []
    for l in range(DEPTH):
        j = l // 2
        mp = ada_params(c_ctx[None, :], w_ada[l], b_ada[l])
        ms = ada_params(c, w_ada[l], b_ada[l])
        hp = modulate(xp, norm_mix[l], mp[0], mp[1])
        hs = modulate(xs, norm_mix[l], ms[0], ms[1])
        if l % 2 == 0:
            hy = (hy_conv_w[j], hy_conv_b[j], hy_w1[j], hy_b1[j], hy_w2[j], hy_b2[j], hy_w3[j], hy_freq[j], hy_skip[j])
            z = proj(hp, even_w_in[j])
            q, k, v, zh = z[..., :A_Q_W], z[..., A_Q_W:A_Q_W + A_KV_W], z[..., A_Q_W + A_KV_W:A_Q_W + 2 * A_KV_W], z[..., A_Q_W + 2 * A_KV_W:]
            oa = context_attention_pallas(q, k, v, attn_sink[j], A_HEADS, A_KV_HEADS)
            mix_p = proj_concat(oa, hyena_mixer(zh, *hy), even_w_out[j])
            new_ak.append(k.reshape(bp, SEQ, A_KV_HEADS, HEAD_DIM))
            new_av.append(v.reshape(bp, SEQ, A_KV_HEADS, HEAD_DIM))
            z = proj(hs, even_w_in[j])
            q, k, v, zh = z[..., :A_Q_W], z[..., A_Q_W:A_Q_W + A_KV_W], z[..., A_Q_W + A_KV_W:A_Q_W + 2 * A_KV_W], z[..., A_Q_W + 2 * A_KV_W:]
            ck = cache_attn_k[:, j].reshape(DEC_BATCH, PAST_LEN, A_KV_W).astype(BF16)
            cv = cache_attn_v[:, j].reshape(DEC_BATCH, PAST_LEN, A_KV_W).astype(BF16)
            oa = window_attention_pallas(q, k, v, ck, cv, attn_sink[j])
            mix_s = proj_concat(oa, hyena_mixer(zh, *hy), even_w_out[j])
        else:
            gdn = (gdn_conv_w[j], gdn_a_log[j], gdn_dt_bias[j], gdn_norm[j])
            zq, zk, zv, zg, za, zb, nq, nk, nv = split_odd(proj(hp, odd_w_in[j]))
            s0 = jnp.zeros((bp, 2, C_HEADS, HEAD_DIM, HEAD_DIM), jnp.float32)
            oc, st = deltanet_mixer(zq, zk, zv, zg, za, zb, *gdn, s0)
            flat = lambda t: t.reshape(t.shape[0], t.shape[1], D_W)
            od = context_attention_pallas(flat(nq), flat(nk), flat(nv), None, D_HEADS, D_HEADS)
            mix_p = proj_concat(oc, od, odd_w_out[j])
            new_st.append(st)
            new_nk.append(nk)
            new_nv.append(nv)
            zq, zk, zv, zg, za, zb, nq, nk, nv = split_odd(proj(hs, odd_w_in[j]))
            oc, _ = deltanet_mixer(zq, zk, zv, zg, za, zb, *gdn, state_delta[:, j])
            ck = cache_na_k[:, j].reshape(DEC_BATCH, PAST_LEN, D_W).astype(BF16)
            cv = cache_na_v[:, j].reshape(DEC_BATCH, PAST_LEN, D_W).astype(BF16)
            od = neighbourhood_attention_pallas(flat(nq).astype(BF16), flat(nk).astype(BF16), flat(nv).astype(BF16),
                                                ck, cv, na_rpb[j])
            mix_s = proj_concat(oc, od, odd_w_out[j])
        xp = xp + mp[2] * mix_p
        xs = xs + ms[2] * mix_s
        moe = (moe_router[l], moe_w_gate[l], moe_w_up[l], moe_w_down[l])
        xp = xp + mp[5] * expert_choice_ffn(modulate(xp, norm_ffn[l], mp[3], mp[4]), *moe)
        xs = xs + ms[5] * expert_choice_ffn(modulate(xs, norm_ffn[l], ms[3], ms[4]), *moe)
    y_prompt = rms_norm(xp, norm_final)
    y_sample = rms_norm(xs, norm_final)
    return (y_prompt, y_sample, jnp.stack(new_ak, axis=1), jnp.stack(new_av, axis=1), jnp.stack(new_st, axis=1),
            jnp.stack(new_nk, axis=1), jnp.stack(new_nv, axis=1))
```

```python
import functools
import math
import jax, jax.numpy as jnp
from jax import lax
import numpy as np
from jax.experimental import pallas as pl
from jax.experimental.pallas import tpu as pltpu

D_MODEL = 1024
BATCH = 32
SEQ = 256
DEPTH = 4
DEC_BATCH = 4
DEC_SEQ = 4096
PAST_LEN = 512

GRID_W = 64
HEAD_DIM = 64
N_EVEN = (DEPTH + 1) // 2
N_ODD = DEPTH // 2
Q_BLOCK = 128
A_HEADS = D_MODEL // 128
A_KV_HEADS = A_HEADS // 4
A_WINDOW = 128
A_BLOCK = 128
ROPE_BASE = 10000.0
HY_CH = D_MODEL // 2
HY_ORDER = 2
HY_SHORT = 3
HY_EMB = 33
HY_FILT_W = 64
HY_FAST_DECAY = 0.3
HY_SLOW_DECAY = 1.5
HY_DECAY_TARGET = 1e-2
HY_SHIFT = 0.05
C_HEADS = D_MODEL // 128
C_SHORT = 3
C_CHUNK = 64
D_HEADS = D_MODEL // 128
NA_KH_MAX = 8
NA_KW = 16
N_EXPERTS = 16
EC_CAPACITY = 2
MOE_D_FF = D_MODEL
EPS = 1e-6
NEG_INF = -1e30

A_Q_W = A_HEADS * HEAD_DIM
A_KV_W = A_KV_HEADS * HEAD_DIM
EVEN_IN = A_Q_W + 2 * A_KV_W + 3 * HY_CH
EVEN_MIX = A_Q_W + HY_CH
C_W = C_HEADS * HEAD_DIM
D_W = D_HEADS * HEAD_DIM
ODD_IN = 4 * C_W + 4 * C_HEADS + 3 * D_W
ODD_MIX = C_W + D_W

VMEM_LIMIT_BYTES = 48 * 1024 * 1024


def _mm_kernel(x_ref, w_ref, o_ref):
    o_ref[...] = jnp.dot(x_ref[...].astype(jnp.bfloat16), w_ref[...], preferred_element_type=jnp.float32)


def pallas_matmul(x, w, tm=256):
    M, K = x.shape
    N = w.shape[1]
    assert M % tm == 0
    return pl.pallas_call(
        _mm_kernel,
        grid=(M // tm,),
        in_specs=[pl.BlockSpec((tm, K), lambda i: (i, 0)), pl.BlockSpec((K, N), lambda i: (0, 0))],
        out_specs=pl.BlockSpec((tm, N), lambda i: (i, 0)),
        out_shape=jax.ShapeDtypeStruct((M, N), jnp.float32),
        compiler_params=pltpu.CompilerParams(dimension_semantics=("arbitrary",), vmem_limit_bytes=VMEM_LIMIT_BYTES),
    )(x, w.astype(jnp.bfloat16))


def proj(x, w):
    B, L, K = x.shape
    return pallas_matmul(x.reshape(B * L, K), w).reshape(B, L, w.shape[1])


def _mm_multi_kernel(x_ref, *refs):
    n = len(refs) // 2
    x = x_ref[...].astype(jnp.bfloat16)
    for w_ref, o_ref in zip(refs[:n], refs[n:]):
        o_ref[...] = jnp.dot(x, w_ref[...], preferred_element_type=jnp.float32).astype(o_ref.dtype)


def proj_multi(x, weights, out_dtypes, tm=256):
    B, L, K = x.shape
    M = B * L
    assert M % tm == 0
    outs = pl.pallas_call(
        _mm_multi_kernel,
        grid=(M // tm,),
        in_specs=[pl.BlockSpec((tm, K), lambda i: (i, 0))] + [pl.BlockSpec(w.shape, lambda i: (0, 0)) for w in weights],
        out_specs=[pl.BlockSpec((tm, w.shape[1]), lambda i: (i, 0)) for w in weights],
        out_shape=[jax.ShapeDtypeStruct((M, w.shape[1]), dt) for w, dt in zip(weights, out_dtypes)],
        compiler_params=pltpu.CompilerParams(dimension_semantics=("arbitrary",), vmem_limit_bytes=VMEM_LIMIT_BYTES),
        name="in_projection",
    )(x.reshape(M, K), *[w.astype(jnp.bfloat16) for w in weights])
    return [o.reshape(B, L, o.shape[1]) for o in outs]


def _mm2_kernel(a_ref, b_ref, wa_ref, wb_ref, o_ref):
    o_ref[...] = (jnp.dot(a_ref[...].astype(jnp.bfloat16), wa_ref[...], preferred_element_type=jnp.float32)
                  + jnp.dot(b_ref[...].astype(jnp.bfloat16), wb_ref[...], preferred_element_type=jnp.float32))


def proj_concat(a, b, w, tm=512):
    B, L, Ka = a.shape
    Kb = b.shape[2]
    N = w.shape[1]
    M = B * L
    assert M % tm == 0 and w.shape[0] == Ka + Kb
    wb16 = w.astype(jnp.bfloat16)
    out = pl.pallas_call(
        _mm2_kernel,
        grid=(M // tm,),
        in_specs=[pl.BlockSpec((tm, Ka), lambda i: (i, 0)), pl.BlockSpec((tm, Kb), lambda i: (i, 0)),
                  pl.BlockSpec((Ka, N), lambda i: (0, 0)), pl.BlockSpec((Kb, N), lambda i: (0, 0))],
        out_specs=pl.BlockSpec((tm, N), lambda i: (i, 0)),
        out_shape=jax.ShapeDtypeStruct((M, N), jnp.float32),
        compiler_params=pltpu.CompilerParams(dimension_semantics=("arbitrary",), vmem_limit_bytes=VMEM_LIMIT_BYTES),
        name="out_projection",
    )(a.reshape(M, Ka), b.reshape(M, Kb), wb16[:Ka], wb16[Ka:])
    return out.reshape(B, L, N)


LANES = 128
BF16 = jnp.bfloat16
F32 = jnp.float32


def _dot_nt(a, b):
    return lax.dot_general(a, b, (((1,), (1,)), ((), ())), preferred_element_type=F32)


def _dot(a, b):
    return jnp.dot(a, b, preferred_element_type=F32)


def _low_half(shape):
    return lax.broadcasted_iota(jnp.int32, shape, 1) < HEAD_DIM


def _softmax_pv(scores, values, extra_logit=None):
    m = scores[0].max(axis=-1, keepdims=True)
    for s in scores[1:]:
        m = jnp.maximum(m, s.max(axis=-1, keepdims=True))
    if extra_logit is not None:
        m = jnp.maximum(m, extra_logit)
    l = None if extra_logit is None else jnp.exp(extra_logit - m)
    o = None
    for s, v in zip(scores, values):
        p = jnp.exp(s - m)
        ps = p.sum(axis=-1, keepdims=True)
        l = ps if l is None else l + ps
        pv = _dot(p.astype(BF16), v)
        o = pv if o is None else o + pv
    return o / l


def _place_head(q_slab, src_half, dst_half, low):
    x = q_slab if src_half == dst_half else pltpu.roll(q_slab, HEAD_DIM, axis=1)
    return jnp.where(low if dst_half == 0 else ~low, x, jnp.zeros_like(x))


NA_KEYS = NA_KH_MAX * GRID_W


def _na_kernel(q_ref, k_ref, v_ref, ck_ref, cv_ref, tab_ref, o_ref):
    r = pl.program_id(1)
    rows = k_ref.shape[1] // GRID_W
    rs = jnp.clip(r - NA_KH_MAX // 2, 0, rows - NA_KH_MAX)
    start = pl.multiple_of(rs * GRID_W, GRID_W)
    scale = HEAD_DIM ** -0.5
    low = _low_half((GRID_W, LANES))
    for p in range(D_HEADS // 2):
        cols = slice(p * LANES, (p + 1) * LANES)
        qp = q_ref[0, :, cols]
        kp = k_ref[0, pl.ds(start, NA_KEYS), cols]
        vp = v_ref[0, pl.ds(start, NA_KEYS), cols]
        ckp = ck_ref[0, :, cols]
        cvp = cv_ref[0, :, cols]
        outs = []
        for half in range(2):
            qm = jnp.where(low if half == 0 else ~low, qp, jnp.zeros_like(qp))
            s_loc = _dot_nt(qm, kp) * scale + tab_ref[2 * p + half, 0]
            s_ctx = _dot_nt(qm, ckp) * scale
            outs.append(_softmax_pv([s_loc, s_ctx], [vp, cvp]))
        o_ref[0, :, cols] = jnp.where(low, outs[0], outs[1]).astype(o_ref.dtype)


def na_bias_table(rpb):
    col = jnp.arange(GRID_W)
    cs = jnp.clip(col - NA_KW // 2, 0, GRID_W - NA_KW)
    col_ok = (col[None, :] >= cs[:, None]) & (col[None, :] < cs[:, None] + NA_KW)
    coff = jnp.clip(col[None, :] - col[:, None] + NA_KW - 1, 0, 2 * NA_KW - 2)
    base = jnp.where(col_ok[None, None], rpb.astype(F32)[:, :, coff], NEG_INF)
    tab = jnp.stack([base[:, o:o + NA_KH_MAX] for o in range(NA_KH_MAX)], axis=1)
    return tab.transpose(0, 1, 3, 2, 4).reshape(D_HEADS, NA_KH_MAX, GRID_W, NA_KEYS)


def neighbourhood_attention_pallas(q, k, v, ck, cv, rpb):
    B, T, W = q.shape
    P = ck.shape[1]
    rows = T // GRID_W
    assert rows >= NA_KH_MAX and W == D_W
    tab = na_bias_table(rpb)

    def tab_index(b, r):
        rs = jnp.clip(r - NA_KH_MAX // 2, 0, rows - NA_KH_MAX)
        return (0, rs - r + NA_KH_MAX - 1, 0, 0)

    return pl.pallas_call(
        _na_kernel,
        grid=(B, rows),
        in_specs=[
            pl.BlockSpec((1, GRID_W, W), lambda b, r: (b, r, 0)),
            pl.BlockSpec((1, T, W), lambda b, r: (b, 0, 0)),
            pl.BlockSpec((1, T, W), lambda b, r: (b, 0, 0)),
            pl.BlockSpec((1, P, W), lambda b, r: (b, 0, 0)),
            pl.BlockSpec((1, P, W), lambda b, r: (b, 0, 0)),
            pl.BlockSpec((D_HEADS, 1, GRID_W, NA_KEYS), tab_index),
        ],
        out_specs=pl.BlockSpec((1, GRID_W, W), lambda b, r: (b, r, 0)),
        out_shape=jax.ShapeDtypeStruct((B, T, W), BF16),
        compiler_params=pltpu.CompilerParams(dimension_semantics=("arbitrary", "arbitrary"),
                                             vmem_limit_bytes=VMEM_LIMIT_BYTES),
        name="na_attention",
    )(q, k, v, ck, cv, tab)


def rope_tables(T):
    cos, sin = axial_rope(T)
    cos, sin = cos[:, 0, :], sin[:, 0, :]
    cos_t = jnp.concatenate([cos, cos, cos, cos], axis=-1)
    sin_t = jnp.concatenate([-sin, sin, -sin, sin], axis=-1)
    return cos_t, sin_t


def _rope(x, cos_t, sin_t):
    half = HEAD_DIM // 2
    lane = lax.broadcasted_iota(jnp.int32, x.shape, 1)
    first = (lane % HEAD_DIM) < half
    swapped = jnp.where(first, pltpu.roll(x, LANES - half, axis=1), pltpu.roll(x, half, axis=1))
    return x * cos_t + swapped * sin_t


def _win_kernel(sink_ref, q_ref, k_ref, v_ref, ck_ref, cv_ref, cos_ref, sin_ref, o_ref):
    i = pl.program_id(1)
    T = k_ref.shape[1]
    span = 3 * A_BLOCK
    start = pl.multiple_of(jnp.clip((i - 1) * A_BLOCK, 0, T - span), A_BLOCK)
    delta = i * A_BLOCK - start
    q0 = pl.multiple_of(i * A_BLOCK, A_BLOCK)
    scale = HEAD_DIM ** -0.5
    kw = _rope(k_ref[0, pl.ds(start, span), :], cos_ref[pl.ds(start, span), :], sin_ref[pl.ds(start, span), :]).astype(BF16)
    vw = v_ref[0, pl.ds(start, span), :].astype(BF16)
    ck = ck_ref[0]
    cv = cv_ref[0]
    cos_q = cos_ref[pl.ds(q0, A_BLOCK), :]
    sin_q = sin_ref[pl.ds(q0, A_BLOCK), :]
    qi = lax.broadcasted_iota(jnp.int32, (A_BLOCK, span), 0)
    kj = lax.broadcasted_iota(jnp.int32, (A_BLOCK, span), 1)
    band = jnp.abs(kj - delta - qi) <= A_WINDOW
    low = _low_half((A_BLOCK, LANES))
    group = A_HEADS // A_KV_HEADS
    for p in range(A_HEADS // 2):
        cols = slice(p * LANES, (p + 1) * LANES)
        q_slab = _rope(q_ref[0, :, cols], cos_q, sin_q)
        outs = []
        for half in range(2):
            h = 2 * p + half
            kv = h // group
            qm = _place_head(q_slab, half, kv, low).astype(BF16)
            s_loc = jnp.where(band, _dot_nt(qm, kw) * scale, NEG_INF)
            s_ctx = _dot_nt(qm, ck) * scale
            o = _softmax_pv([s_loc, s_ctx], [vw, cv], extra_logit=sink_ref[h])
            outs.append(o if kv == half else pltpu.roll(o, HEAD_DIM, axis=1))
        o_ref[0, :, cols] = jnp.where(low, outs[0], outs[1]).astype(o_ref.dtype)


def window_attention_pallas(q, k, v, ck, cv, sink):
    B, T, QW = q.shape
    KW = k.shape[2]
    P = ck.shape[1]
    assert KW == LANES and QW == A_Q_W and T % A_BLOCK == 0 and T >= 3 * A_BLOCK
    cos_t, sin_t = rope_tables(T)
    return pl.pallas_call(
        _win_kernel,
        grid=(B, T // A_BLOCK),
        in_specs=[
            pl.BlockSpec(memory_space=pltpu.SMEM),
            pl.BlockSpec((1, A_BLOCK, QW), lambda b, i: (b, i, 0)),
            pl.BlockSpec((1, T, KW), lambda b, i: (b, 0, 0)),
            pl.BlockSpec((1, T, KW), lambda b, i: (b, 0, 0)),
            pl.BlockSpec((1, P, KW), lambda b, i: (b, 0, 0)),
            pl.BlockSpec((1, P, KW), lambda b, i: (b, 0, 0)),
            pl.BlockSpec((T, LANES), lambda b, i: (0, 0)),
            pl.BlockSpec((T, LANES), lambda b, i: (0, 0)),
        ],
        out_specs=pl.BlockSpec((1, A_BLOCK, QW), lambda b, i: (b, i, 0)),
        out_shape=jax.ShapeDtypeStruct((B, T, QW), BF16),
        compiler_params=pltpu.CompilerParams(dimension_semantics=("arbitrary", "arbitrary"),
                                             vmem_limit_bytes=VMEM_LIMIT_BYTES),
        name="window_attention",
    )(sink.astype(F32), q, k, v, ck, cv, cos_t, sin_t)


def _ctx_kernel(sink_ref, q_ref, k_ref, v_ref, o_ref, *, n_q_heads, n_kv_heads, use_sink):
    S = q_ref.shape[1]
    scale = HEAD_DIM ** -0.5
    low = _low_half((S, LANES))
    group = n_q_heads // n_kv_heads
    for p in range(n_q_heads // 2):
        cols = slice(p * LANES, (p + 1) * LANES)
        q_slab = q_ref[0, :, cols]
        outs = []
        for half in range(2):
            h = 2 * p + half
            kv = h // group
            kcols = slice((kv // 2) * LANES, (kv // 2 + 1) * LANES)
            qm = _place_head(q_slab, half, kv % 2, low).astype(BF16)
            s = _dot_nt(qm, k_ref[0, :, kcols].astype(BF16)) * scale
            o = _softmax_pv([s], [v_ref[0, :, kcols].astype(BF16)], extra_logit=sink_ref[h] if use_sink else None)
            outs.append(o if kv % 2 == half else pltpu.roll(o, HEAD_DIM, axis=1))
        o_ref[0, :, cols] = jnp.where(low, outs[0], outs[1]).astype(o_ref.dtype)


def context_attention_pallas(q, k, v, sink, n_q_heads, n_kv_heads):
    B, S, QW = q.shape
    KW = k.shape[2]
    use_sink = sink is not None
    sink_arr = sink.astype(F32) if use_sink else jnp.zeros((n_q_heads,), F32)
    return pl.pallas_call(
        functools.partial(_ctx_kernel, n_q_heads=n_q_heads, n_kv_heads=n_kv_heads, use_sink=use_sink),
        grid=(B,),
        in_specs=[
            pl.BlockSpec(memory_space=pltpu.SMEM),
            pl.BlockSpec((1, S, QW), lambda b: (b, 0, 0)),
            pl.BlockSpec((1, S, KW), lambda b: (b, 0, 0)),
            pl.BlockSpec((1, S, KW), lambda b: (b, 0, 0)),
        ],
        out_specs=pl.BlockSpec((1, S, QW), lambda b: (b, 0, 0)),
        out_shape=jax.ShapeDtypeStruct((B, S, QW), BF16),
        compiler_params=pltpu.CompilerParams(dimension_semantics=("arbitrary",), vmem_limit_bytes=VMEM_LIMIT_BYTES),
        name="context_attention",
    )(sink_arr, q, k, v)


HIGHEST = lax.Precision.HIGHEST
GDN_TM = 256
SUBLANES = 8


def _head_pair_sum_matrix():
    a = lax.broadcasted_iota(jnp.int32, (LANES, LANES), 0) // HEAD_DIM
    b = lax.broadcasted_iota(jnp.int32, (LANES, LANES), 1) // HEAD_DIM
    return (a == b).astype(F32)


def _gdn_prep_kernel(x_ref, prev_ref, next_ref, ab_ref, cw_ref, a_ref, dtb_ref, q_ref, k_ref, v_ref, gb_ref):
    i = pl.program_id(1)
    n = pl.num_programs(1)
    x = x_ref[0]
    tm = x.shape[0]
    row = lax.broadcasted_iota(jnp.int32, x.shape, 0)
    prev_row = jnp.where(i > 0, prev_ref[0, SUBLANES - 1:SUBLANES, :], 0.0)
    next_row = jnp.where(i < n - 1, next_ref[0, 0:1, :], 0.0)
    x_prev = jnp.where(row == 0, prev_row, pltpu.roll(x, 1, axis=0))
    x_next = jnp.where(row == tm - 1, next_row, pltpu.roll(x, tm - 1, axis=0))
    y = x_prev * cw_ref[0:1, :] + x * cw_ref[1:2, :] + x_next * cw_ref[2:3, :]
    y = y * jax.nn.sigmoid(y)
    pmat = _head_pair_sum_matrix()
    for p in range(C_W // LANES):
        qs = y[:, p * LANES:(p + 1) * LANES]
        ks = y[:, C_W + p * LANES:C_W + (p + 1) * LANES]
        q_ref[0, :, p * LANES:(p + 1) * LANES] = qs * lax.rsqrt(jnp.dot(qs * qs, pmat, precision=HIGHEST, preferred_element_type=F32) + EPS) * (HEAD_DIM ** -0.5)
        k_ref[0, :, p * LANES:(p + 1) * LANES] = ks * lax.rsqrt(jnp.dot(ks * ks, pmat, precision=HIGHEST, preferred_element_type=F32) + EPS)
    v_ref[0] = y[:, 2 * C_W:]
    ab = ab_ref[0, :, 0:4 * C_HEADS]
    lane = lax.broadcasted_iota(jnp.int32, ab.shape, 1)
    is_beta = (lane // C_HEADS) % 2 == 1
    t = ab + dtb_ref[...]
    softplus = jnp.maximum(t, 0.0) + jnp.log1p(jnp.exp(-jnp.abs(t)))
    gb = jnp.where(is_beta, jax.nn.sigmoid(ab), -jnp.exp(a_ref[...]) * softplus)
    gb_ref[0, 0] = gb[:, 0:2 * C_HEADS]
    gb_ref[0, 1] = gb[:, 2 * C_HEADS:4 * C_HEADS]


def gdn_prep(zqkv, zab, conv_w, a_log, dt_bias):
    B, L, W3 = zqkv.shape
    tm = GDN_TM
    assert L % tm == 0
    nb = tm // SUBLANES
    zero = jnp.zeros((C_HEADS,), F32)
    a_lane = jnp.concatenate([a_log[0], zero, a_log[1], zero]).astype(F32)[None, :]
    dtb_lane = jnp.concatenate([dt_bias[0], zero, dt_bias[1], zero]).astype(F32)[None, :]
    outs = pl.pallas_call(
        _gdn_prep_kernel,
        grid=(B, L // tm),
        in_specs=[
            pl.BlockSpec((1, tm, W3), lambda b, i: (b, i, 0)),
            pl.BlockSpec((1, SUBLANES, W3), lambda b, i: (b, jnp.maximum(i * nb - 1, 0), 0)),
            pl.BlockSpec((1, SUBLANES, W3), lambda b, i: (b, jnp.minimum((i + 1) * nb, L // SUBLANES - 1), 0)),
            pl.BlockSpec((1, tm, LANES), lambda b, i: (b, i, 0)),
            pl.BlockSpec((C_SHORT, W3), lambda b, i: (0, 0)),
            pl.BlockSpec((1, 4 * C_HEADS), lambda b, i: (0, 0)),
            pl.BlockSpec((1, 4 * C_HEADS), lambda b, i: (0, 0)),
        ],
        out_specs=[
            pl.BlockSpec((1, tm, C_W), lambda b, i: (b, i, 0)),
            pl.BlockSpec((1, tm, C_W), lambda b, i: (b, i, 0)),
            pl.BlockSpec((1, tm, C_W), lambda b, i: (b, i, 0)),
            pl.BlockSpec((1, 2, tm, 2 * C_HEADS), lambda b, i: (b, 0, i, 0)),
        ],
        out_shape=[jax.ShapeDtypeStruct((B, L, C_W), F32)] * 3 + [jax.ShapeDtypeStruct((B, 2, L, 2 * C_HEADS), F32)],
        compiler_params=pltpu.CompilerParams(dimension_semantics=("arbitrary", "arbitrary"),
                                             vmem_limit_bytes=VMEM_LIMIT_BYTES),
        name="gdn_prep",
    )(zqkv, zqkv, zqkv, zab, conv_w.astype(F32), a_lane, dtb_lane)
    return outs


def _gdn_kernel(q_ref, k_ref, v_ref, gb_ref, zg_ref, s0_ref, nw_ref, o_ref, st_ref, s_scr, of_scr):
    d = pl.program_id(1)
    c = pl.program_id(2)
    n = pl.num_programs(2)
    C = C_CHUNK
    fwd = d == 0
    chunk = jnp.where(fwd, c, n - 1 - c)
    r0 = pl.multiple_of(chunk * C, C)

    @pl.when(c == 0)
    def _():
        s_scr[...] = s0_ref[0, 0]

    row = lax.broadcasted_iota(jnp.int32, (C, C), 0)
    col = lax.broadcasted_iota(jnp.int32, (C, C), 1)
    ahead = jnp.where(fwd, row - col, col - row)
    incl = ahead >= 0
    strict = ahead > 0
    g = gb_ref[0, 0, :, 0:C_HEADS]
    beta = gb_ref[0, 0, :, C_HEADS:2 * C_HEADS]
    gc = jnp.dot(incl.astype(F32), g, precision=HIGHEST, preferred_element_type=F32)
    gc_t = gc.T
    g_last = jnp.where(fwd, gc[C - 1:C, :], gc[0:1, :])
    H = range(C_HEADS)
    heads = [slice(h * HEAD_DIM, (h + 1) * HEAD_DIM) for h in H]
    s_old = [s_scr[h] for h in H]
    qs = [q_ref[0, :, hs] for hs in heads]
    ks = [k_ref[0, :, hs] for hs in heads]
    vs = [v_ref[0, :, hs] for hs in heads]
    gcol = [gc[:, h:h + 1] for h in H]
    bcol = [beta[:, h:h + 1] for h in H]
    gl = [g_last[:, h:h + 1] for h in H]
    gamma = [jnp.exp(jnp.where(incl, gcol[h] - gc_t[h:h + 1, :], NEG_INF)) for h in H]
    egc = [jnp.exp(gcol[h]) for h in H]
    kb = [ks[h].astype(BF16) for h in H]
    nmat = [jnp.where(strict, _dot_nt(kb[h], kb[h]) * gamma[h], 0.0) * bcol[h] for h in H]
    attn = [(_dot_nt(qs[h].astype(BF16), kb[h]) * gamma[h]).astype(BF16) for h in H]
    xr = row ^ col
    eye = (row == col).astype(F32)
    tinv = [eye - jnp.where((xr >> 1) == 0, nmat[h], 0.0) for h in H]
    for lvl in range(1, 6):
        off_diag = (xr >> lvl) == 1
        wmat = [_dot(jnp.where(off_diag, nmat[h], 0.0).astype(BF16), tinv[h].astype(BF16)).astype(BF16) for h in H]
        tinv = [tinv[h] - _dot(tinv[h].astype(BF16), wmat[h]) for h in H]
    x = [jnp.concatenate([vs[h] * bcol[h], ks[h] * (bcol[h] * egc[h])], axis=1) for h in H]
    x = [x[h] + _dot((tinv[h] - eye).astype(BF16), x[h].astype(BF16)) for h in H]
    sb = [s_old[h].astype(BF16) for h in H]
    v_new = [x[h][:, :HEAD_DIM] - _dot(x[h][:, HEAD_DIM:].astype(BF16), sb[h]) for h in H]
    vb = [v_new[h].astype(BF16) for h in H]
    outs = [_dot((qs[h] * egc[h]).astype(BF16), sb[h]) + _dot(attn[h], vb[h]) for h in H]
    kd = [(ks[h] * jnp.exp(gl[h] - gcol[h])).astype(BF16) for h in H]
    s_new = [s_old[h] * jnp.exp(gl[h]) + lax.dot_general(kd[h], vb[h], (((0,), (0,)), ((), ())), preferred_element_type=F32)
             for h in H]
    for h in H:
        s_scr[h] = s_new[h]
    o = jnp.concatenate(outs, axis=1)

    @pl.when(fwd)
    def _():
        of_scr[pl.ds(r0, C), :] = o

    @pl.when(jnp.logical_not(fwd))
    def _():
        tot = of_scr[pl.ds(r0, C), :] + o
        pmat = _head_pair_sum_matrix()
        zg = zg_ref[0]
        gate = zg * jax.nn.sigmoid(zg)
        for p in range(C_W // LANES):
            cols = slice(p * LANES, (p + 1) * LANES)
            t = tot[:, cols]
            ms = jnp.dot(t * t, pmat, precision=HIGHEST, preferred_element_type=F32) * (1.0 / HEAD_DIM)
            o_ref[0, :, cols] = (t * lax.rsqrt(ms + EPS) * nw_ref[:, cols] * gate[:, cols]).astype(o_ref.dtype)

    @pl.when(c == n - 1)
    def _():
        st_ref[0, 0] = s_scr[...]


def gdn_scan(q, k, v, gb, zg, s0, norm_w):
    B, L, W = q.shape
    C = C_CHUNK
    n = L // C
    assert L % C == 0 and W == C_W
    chunk_of = lambda d, c: jnp.where(d == 0, c, n - 1 - c)
    seq_spec = pl.BlockSpec((1, C, W), lambda b, d, c: (b, chunk_of(d, c), 0))
    state_spec = pl.BlockSpec((1, 1, C_HEADS, HEAD_DIM, HEAD_DIM), lambda b, d, c: (b, d, 0, 0, 0))
    nw = jnp.tile(norm_w.astype(F32), C_HEADS)[None, :]
    return pl.pallas_call(
        _gdn_kernel,
        grid=(B, 2, n),
        in_specs=[
            seq_spec, seq_spec, seq_spec,
            pl.BlockSpec((1, 1, C, 2 * C_HEADS), lambda b, d, c: (b, d, chunk_of(d, c), 0)),
            seq_spec,
            state_spec,
            pl.BlockSpec((1, W), lambda b, d, c: (0, 0)),
        ],
        out_specs=[
            pl.BlockSpec((1, C, W), lambda b, d, c: (b, jnp.where(d == 0, n - 1, n - 1 - c), 0)),
            state_spec,
        ],
        out_shape=[jax.ShapeDtypeStruct((B, L, W), BF16), jax.ShapeDtypeStruct(s0.shape, F32)],
        scratch_shapes=[pltpu.VMEM((C_HEADS, HEAD_DIM, HEAD_DIM), F32), pltpu.VMEM((L, W), F32)],
        compiler_params=pltpu.CompilerParams(dimension_semantics=("arbitrary", "arbitrary", "arbitrary"),
                                             vmem_limit_bytes=VMEM_LIMIT_BYTES),
        name="gdn_scan",
    )(q, k, v, gb, zg, s0.astype(F32), nw)


def rms_norm(x, g):
    xf = x.astype(jnp.float32)
    y = xf * lax.rsqrt(jnp.mean(xf * xf, axis=-1, keepdims=True) + EPS)
    return (y * g.astype(jnp.float32)).astype(x.dtype)


def ada_params(cond, w, b):
    m = jax.nn.silu(cond) @ w + b
    return jnp.split(m[:, None, :], 6, axis=-1)


def modulate(x, g, shift, scale):
    return rms_norm(x, g) * (1 + scale) + shift


def axial_rope(T):
    t = jnp.arange(T)
    n_freq = HEAD_DIM // 4
    inv = ROPE_BASE ** (-jnp.arange(n_freq, dtype=jnp.float32) / n_freq)
    ang = jnp.concatenate([(t // GRID_W).astype(jnp.float32)[:, None] * inv,
                           (t % GRID_W).astype(jnp.float32)[:, None] * inv], axis=-1)
    return jnp.cos(ang)[:, None, :], jnp.sin(ang)[:, None, :]


def apply_rope(x, cos, sin):
    xf = x.astype(jnp.float32)
    x1, x2 = jnp.split(xf, 2, axis=-1)
    return jnp.concatenate([x1 * cos - x2 * sin, x2 * cos + x1 * sin], axis=-1).astype(x.dtype)


def softmax_parts(parts, sink=None):
    sizes = [p.shape[-1] for p in parts]
    cols = list(parts)
    if sink is not None:
        cols.append(jnp.broadcast_to(sink, parts[0].shape[:-1] + (1,)))
    p = jax.nn.softmax(jnp.concatenate(cols, axis=-1), axis=-1)
    pieces = jnp.split(p, np.cumsum(sizes).tolist(), axis=-1)
    return pieces[:len(sizes)]


def context_attention(q, k, v, sink):
    B, S, HQ, hd = q.shape
    HK = k.shape[2]
    G = HQ // HK
    nb = S // Q_BLOCK
    scale = hd ** -0.5
    sink_b = None if sink is None else sink.astype(jnp.float32).reshape(1, HK, G, 1, 1)
    qb = q.reshape(B, nb, Q_BLOCK, HK, G, hd).swapaxes(0, 1)

    def one(qi):
        s = jnp.einsum('bqkgd,bskd->bkgqs', qi, k, preferred_element_type=jnp.float32) * scale
        (p,) = softmax_parts([s], sink_b)
        return jnp.einsum('bkgqs,bskd->bqkgd', p.astype(v.dtype), v)

    o = lax.map(one, qb)
    return o.swapaxes(0, 1).reshape(B, S, HQ, hd)


def window_attention(q, k, v, ck, cv, sink):
    B, T, HQ, hd = q.shape
    HK = k.shape[2]
    G = HQ // HK
    nb = T // A_BLOCK
    scale = hd ** -0.5
    qb = q.reshape(B, nb, A_BLOCK, HK, G, hd).swapaxes(0, 1)

    def band(x):
        xp = jnp.pad(x, ((0, 0), (A_BLOCK, A_BLOCK), (0, 0), (0, 0))).reshape(B, nb + 2, A_BLOCK, HK, hd)
        return jnp.concatenate([xp[:, :-2], xp[:, 1:-1], xp[:, 2:]], axis=2).swapaxes(0, 1)

    kb, vb = band(k), band(v)
    qpos = jnp.arange(nb)[:, None, None] * A_BLOCK + jnp.arange(A_BLOCK)[None, :, None]
    kpos = jnp.arange(nb)[:, None, None] * A_BLOCK - A_BLOCK + jnp.arange(3 * A_BLOCK)[None, None, :]
    mask = (jnp.abs(kpos - qpos) <= A_WINDOW) & (kpos >= 0) & (kpos < T)
    sink_b = sink.astype(jnp.float32).reshape(1, HK, G, 1, 1)

    def one(xs):
        qi, ki, vi, mi = xs
        s_loc = jnp.einsum('bqkgd,bskd->bkgqs', qi, ki, preferred_element_type=jnp.float32) * scale
        s_loc = jnp.where(mi[None, None, None], s_loc, NEG_INF)
        s_ctx = jnp.einsum('bqkgd,bpkd->bkgqp', qi, ck, preferred_element_type=jnp.float32) * scale
        p_loc, p_ctx = softmax_parts([s_loc, s_ctx], sink_b)
        return (jnp.einsum('bkgqs,bskd->bqkgd', p_loc.astype(vi.dtype), vi)
                + jnp.einsum('bkgqp,bpkd->bqkgd', p_ctx.astype(cv.dtype), cv))

    o = lax.map(one, (qb, kb, vb, mask))
    return o.swapaxes(0, 1).reshape(B, T, HQ, hd)


def short_conv(x, w):
    K = w.shape[0]
    L = x.shape[1]
    pad = K // 2
    xp = jnp.pad(x, ((0, 0), (pad, pad), (0, 0)))
    return sum(xp[:, i:i + L] * w[i] for i in range(K))


def hyena_filter_bank(L, w1, b1, w2, b2, w3, freq):
    f32 = jnp.float32
    t = jnp.linspace(0.0, 1.0, L, dtype=f32)[:, None]
    bands = (HY_EMB - 1) // 2
    omega = 2.0 * math.pi * jnp.arange(L, dtype=f32)[:, None] / L
    fb = jnp.linspace(1e-4, bands - 1, bands, dtype=f32)[None, :]
    feats = jnp.concatenate([t, jnp.cos(fb * omega), -jnp.sin(fb * omega)], axis=-1)
    fr = freq.astype(f32)
    h = jnp.sin(fr * (feats @ w1.astype(f32) + b1.astype(f32)))
    h = jnp.sin(fr * (h @ w2.astype(f32) + b2.astype(f32)))
    h = (h @ w3.astype(f32)).reshape(L, 2, HY_ORDER, HY_CH)
    max_decay = math.log(HY_DECAY_TARGET) / HY_FAST_DECAY
    min_decay = math.log(HY_DECAY_TARGET) / HY_SLOW_DECAY
    deltas = jnp.abs(jnp.linspace(min_decay, max_decay, HY_CH, dtype=f32))
    h = h * (jnp.exp(-t * deltas) + HY_SHIFT)[:, None, None, :]
    taps = jnp.concatenate([h[:, 0], jnp.zeros((1, HY_ORDER, HY_CH), f32), h[:0:-1, 1]], axis=0)
    taps = taps / jnp.sum(jnp.abs(taps), axis=0, keepdims=True)
    return jnp.fft.rfft(taps, axis=0)


def hyena_mixer(z, conv_w, conv_b, w1, b1, w2, b2, w3, freq, skip):
    L = z.shape[1]
    zf = (short_conv(z, conv_w) + conv_b).astype(jnp.float32)
    v, x1, x2 = jnp.split(zf, 3, axis=-1)
    filt = hyena_filter_bank(L, w1, b1, w2, b2, w3, freq)
    skip = skip.astype(jnp.float32)

    def long_conv(u, o):
        y = jnp.fft.irfft(jnp.fft.rfft(u, n=2 * L, axis=1) * filt[None, :, o], n=2 * L, axis=1)[:, :L]
        return y + u * skip[o]

    y = x1 * long_conv(v, 0)
    y = x2 * long_conv(y, 1)
    return y.astype(z.dtype)


def l2norm(x):
    xf = x.astype(jnp.float32)
    return xf * lax.rsqrt(jnp.sum(xf * xf, axis=-1, keepdims=True) + EPS)


def chunk_gated_delta(q, k, v, g, beta, s0):
    B, L, H, dk = q.shape
    dv = v.shape[-1]
    n = L // C_CHUNK

    def chunks(x):
        x = x.reshape((B, n, C_CHUNK, H) + x.shape[3:])
        return jnp.moveaxis(jnp.moveaxis(x, 1, 0), 3, 2)

    qc, kc, vc, bc = chunks(q), chunks(k), chunks(v), chunks(beta)
    gc = jnp.cumsum(chunks(g), axis=-1)
    tri = jnp.tril(jnp.ones((C_CHUNK, C_CHUNK), bool))
    strict = jnp.tril(jnp.ones((C_CHUNK, C_CHUNK), bool), k=-1)
    gamma = jnp.exp(jnp.where(tri, gc[..., :, None] - gc[..., None, :], NEG_INF))
    kb = kc * bc[..., None]
    a_mat = jnp.where(strict, jnp.einsum('nbhid,nbhjd->nbhij', kb, kc) * gamma, 0.0) + jnp.eye(C_CHUNK, dtype=jnp.float32)
    rhs = jnp.concatenate([vc * bc[..., None], kb * jnp.exp(gc)[..., None]], axis=-1)
    sol = lax.linalg.triangular_solve(a_mat, rhs, left_side=True, lower=True)
    u, w = sol[..., :dv], sol[..., dv:]
    attn = jnp.where(tri, jnp.einsum('nbhid,nbhjd->nbhij', qc, kc) * gamma, 0.0)
    g_last = gc[..., -1]
    q_dec = qc * jnp.exp(gc)[..., None]
    k_dec = kc * jnp.exp(g_last[..., None] - gc)[..., None]

    def step(S, xs):
        u_i, w_i, a_i, qd, kd, gl = xs
        v_new = u_i - jnp.einsum('bhck,bhkv->bhcv', w_i, S)
        o = jnp.einsum('bhck,bhkv->bhcv', qd, S) + jnp.einsum('bhij,bhjv->bhiv', a_i, v_new)
        S = S * jnp.exp(gl)[..., None, None] + jnp.einsum('bhck,bhcv->bhkv', kd, v_new)
        return S, o

    S, o = lax.scan(step, s0.astype(jnp.float32), (u, w, attn, q_dec, k_dec, g_last))
    o = jnp.moveaxis(jnp.moveaxis(o, 2, 3), 0, 1).reshape(B, L, H, dv)
    return o, S


def deltanet_mixer(zq, zk, zv, zg, za, zb, conv_w, a_log, dt_bias, norm_w, s0):
    B, L, _ = zq.shape
    qkv = jax.nn.silu(short_conv(jnp.concatenate([zq, zk, zv], axis=-1), conv_w))
    q, k, v = [t.reshape(B, L, C_HEADS, HEAD_DIM) for t in jnp.split(qkv, 3, axis=-1)]
    q = l2norm(q) * (HEAD_DIM ** -0.5)
    k = l2norm(k)
    v = v.astype(jnp.float32)
    beta = jax.nn.sigmoid(zb.astype(jnp.float32))
    g = -jnp.exp(a_log.astype(jnp.float32)) * jax.nn.softplus(za.astype(jnp.float32) + dt_bias.astype(jnp.float32))
    o_f, s_f = chunk_gated_delta(q, k, v, g[:, :, 0], beta[:, :, 0], s0[:, 0])
    o_b, s_b = chunk_gated_delta(q[:, ::-1], k[:, ::-1], v[:, ::-1], g[:, ::-1, 1], beta[:, ::-1, 1], s0[:, 1])
    o = o_f + o_b[:, ::-1]
    gate = jax.nn.silu(zg.reshape(B, L, C_HEADS, HEAD_DIM).astype(jnp.float32))
    o = rms_norm(o, norm_w) * gate
    return o.reshape(B, L, C_W).astype(zq.dtype), jnp.stack([s_f, s_b], axis=1)


def neighbourhood_attention(q, k, v, ck, cv, rpb):
    B, T, H, hd = q.shape
    rows = T // GRID_W
    kh = min(NA_KH_MAX, rows)
    scale = hd ** -0.5
    r = jnp.arange(rows)
    rs = jnp.clip(r - kh // 2, 0, rows - kh)
    key_rows = rs[:, None] + jnp.arange(kh)[None, :]
    idx = (key_rows[:, :, None] * GRID_W + jnp.arange(GRID_W)).reshape(rows, kh * GRID_W)
    col = jnp.arange(GRID_W)
    cs = jnp.clip(col - NA_KW // 2, 0, GRID_W - NA_KW)
    kcol = jnp.tile(col, kh)
    col_ok = (kcol[None, :] >= cs[:, None]) & (kcol[None, :] < cs[:, None] + NA_KW)
    roff = jnp.repeat(key_rows - r[:, None], GRID_W, axis=1) + NA_KH_MAX - 1
    coff = jnp.clip(kcol[None, :] - col[:, None] + NA_KW - 1, 0, 2 * NA_KW - 2)
    qr = q.reshape(B, rows, GRID_W, H, hd).swapaxes(0, 1)
    rpb_f = rpb.astype(jnp.float32)

    def one(xs):
        qi, ii, ro = xs
        ki = k[:, ii]
        vi = v[:, ii]
        bias = rpb_f[:, ro[None, :], coff]
        s_loc = jnp.einsum('bqhd,bkhd->bhqk', qi, ki, preferred_element_type=jnp.float32) * scale + bias[None]
        s_loc = jnp.where(col_ok[None, None], s_loc, NEG_INF)
        s_ctx = jnp.einsum('bqhd,bphd->bhqp', qi, ck, preferred_element_type=jnp.float32) * scale
        p_loc, p_ctx = softmax_parts([s_loc, s_ctx])
        return (jnp.einsum('bhqk,bkhd->bqhd', p_loc.astype(vi.dtype), vi)
                + jnp.einsum('bhqp,bphd->bqhd', p_ctx.astype(cv.dtype), cv))

    o = lax.map(one, (qr, idx, roff))
    return o.swapaxes(0, 1).reshape(B, T, H, hd)


def expert_choice_ffn(h, w_router, w_gate, w_up, w_down):
    B, T, D = h.shape
    cap = EC_CAPACITY * T // N_EXPERTS
    aff = jax.nn.softmax(jnp.einsum('btd,de->bte', h, w_router, preferred_element_type=jnp.float32), axis=-1)
    gate, idx = lax.top_k(aff.swapaxes(1, 2), cap)
    xg = jax.vmap(lambda hb, ib: hb[ib])(h, idx)
    a = jnp.einsum('becd,edf->becf', xg, w_gate)
    u = jnp.einsum('becd,edf->becf', xg, w_up)
    y = jnp.einsum('becf,efd->becd', jax.nn.silu(a) * u, w_down) * gate[..., None].astype(h.dtype)
    return jax.vmap(lambda yb, ib: jnp.zeros((T, D), yb.dtype).at[ib.reshape(-1)].add(yb.reshape(-1, D)))(y, idx)


def split_even(z):
    B, L = z.shape[:2]
    q = z[..., :A_Q_W].reshape(B, L, A_HEADS, HEAD_DIM)
    k = z[..., A_Q_W:A_Q_W + A_KV_W].reshape(B, L, A_KV_HEADS, HEAD_DIM)
    v = z[..., A_Q_W + A_KV_W:A_Q_W + 2 * A_KV_W].reshape(B, L, A_KV_HEADS, HEAD_DIM)
    return q, k, v, z[..., A_Q_W + 2 * A_KV_W:]


def split_odd(z):
    B, L = z.shape[:2]
    zq, zk, zv, zg = [z[..., i * C_W:(i + 1) * C_W] for i in range(4)]
    off = 4 * C_W
    za = z[..., off:off + 2 * C_HEADS].reshape(B, L, 2, C_HEADS)
    zb = z[..., off + 2 * C_HEADS:off + 4 * C_HEADS].reshape(B, L, 2, C_HEADS)
    off = off + 4 * C_HEADS
    nq, nk, nv = [z[..., off + i * D_W:off + (i + 1) * D_W].reshape(B, L, D_HEADS, HEAD_DIM) for i in range(3)]
    return zq, zk, zv, zg, za, zb, nq, nk, nv


def kernel(x_prompt, x_sample, cache_attn_k, cache_attn_v, state_delta, cache_na_k, cache_na_v,
           c, c_ctx, w_ada, b_ada, norm_mix, norm_ffn, norm_final,
           even_w_in, even_w_out, attn_sink, hy_conv_w, hy_conv_b, hy_w1, hy_b1, hy_w2, hy_b2,
           hy_w3, hy_freq, hy_skip, odd_w_in, odd_w_out, gdn_conv_w, gdn_a_log, gdn_dt_bias,
           gdn_norm, na_rpb, moe_router, moe_w_gate, moe_w_up, moe_w_down):
    xp, xs = x_prompt, x_sample
    bp = xp.shape[0]
    cos, sin = axial_rope(xs.shape[1])
    new_ak, new_av, new_st, new_nk, new_nv = [], [], [], [], []
    for l in range(DEPTH):
        j = l // 2
        mp = ada_params(c_ctx[None, :], w_ada[l], b_ada[l])
        ms = ada_params(c, w_ada[l], b_ada[l])
        hp = modulate(xp, norm_mix[l], mp[0], mp[1])
        hs = modulate(xs, norm_mix[l], ms[0], ms[1])
        if l % 2 == 0:
            hy = (hy_conv_w[j], hy_conv_b[j], hy_w1[j], hy_b1[j], hy_w2[j], hy_b2[j], hy_w3[j], hy_freq[j], hy_skip[j])
            w_in = even_w_in[j]
            w_groups = [w_in[:, :A_Q_W], w_in[:, A_Q_W:A_Q_W + A_KV_W], w_in[:, A_Q_W + A_KV_W:A_Q_W + 2 * A_KV_W],
                        w_in[:, A_Q_W + 2 * A_KV_W:]]
            q, k, v, zh = proj_multi(hp, w_groups, [F32] * 4)
            oa = context_attention_pallas(q, k, v, attn_sink[j], A_HEADS, A_KV_HEADS)
            mix_p = proj_concat(oa, hyena_mixer(zh, *hy), even_w_out[j])
            new_ak.append(k.reshape(bp, SEQ, A_KV_HEADS, HEAD_DIM))
            new_av.append(v.reshape(bp, SEQ, A_KV_HEADS, HEAD_DIM))
            q, k, v, zh = proj_multi(hs, w_groups, [F32] * 4)
            ck = cache_attn_k[:, j].reshape(DEC_BATCH, PAST_LEN, A_KV_W).astype(BF16)
            cv = cache_attn_v[:, j].reshape(DEC_BATCH, PAST_LEN, A_KV_W).astype(BF16)
            oa = window_attention_pallas(q, k, v, ck, cv, attn_sink[j])
            mix_s = proj_concat(oa, hyena_mixer(zh, *hy), even_w_out[j])
        else:
            w_in = odd_w_in[j]
            ab0 = 4 * C_W
            ab_cols = [w_in[:, ab0 + o * C_HEADS:ab0 + (o + 1) * C_HEADS] for o in (0, 2, 1, 3)]
            w_ab = jnp.concatenate(ab_cols + [jnp.zeros((D_MODEL, LANES - 4 * C_HEADS), w_in.dtype)], axis=1)
            n0 = ab0 + 4 * C_HEADS
            w_groups = [w_in[:, :3 * C_W], w_in[:, 3 * C_W:4 * C_W], w_ab,
                        w_in[:, n0:n0 + D_W], w_in[:, n0 + D_W:n0 + 2 * D_W], w_in[:, n0 + 2 * D_W:]]

            def deltanet(zqkv, zab, zg, s0):
                qd, kd, vd, gb = gdn_prep(zqkv, zab, gdn_conv_w[j], gdn_a_log[j], gdn_dt_bias[j])
                return gdn_scan(qd, kd, vd, gb, zg, s0, gdn_norm[j])

            zqkv, zg, zab, nq, nk, nv = proj_multi(hp, w_groups, [F32] * 6)
            oc, st = deltanet(zqkv, zab, zg, jnp.zeros((bp, 2, C_HEADS, HEAD_DIM, HEAD_DIM), F32))
            od = context_attention_pallas(nq, nk, nv, None, D_HEADS, D_HEADS)
            mix_p = proj_concat(oc, od, odd_w_out[j])
            new_st.append(st)
            new_nk.append(nk.reshape(bp, SEQ, D_HEADS, HEAD_DIM))
            new_nv.append(nv.reshape(bp, SEQ, D_HEADS, HEAD_DIM))
            zqkv, zg, zab, nq, nk, nv = proj_multi(hs, w_groups, [F32, F32, F32, BF16, BF16, BF16])
            oc, _ = deltanet(zqkv, zab, zg, state_delta[:, j])
            ck = cache_na_k[:, j].reshape(DEC_BATCH, PAST_LEN, D_W).astype(BF16)
            cv = cache_na_v[:, j].reshape(DEC_BATCH, PAST_LEN, D_W).astype(BF16)
            od = neighbourhood_attention_pallas(nq, nk, nv, ck, cv, na_rpb[j])
            mix_s = proj_concat(oc, od, odd_w_out[j])
        xp = xp + mp[2] * mix_p
        xs = xs + ms[2] * mix_s
        moe = (moe_router[l], moe_w_gate[l], moe_w_up[l], moe_w_down[l])
        xp = xp + mp[5] * expert_choice_ffn(modulate(xp, norm_ffn[l], mp[3], mp[4]), *moe)
        xs = xs + ms[5] * expert_choice_ffn(modulate(xs, norm_ffn[l], ms[3], ms[4]), *moe)
    y_prompt = rms_norm(xp, norm_final)
    y_sample = rms_norm(xs, norm_final)
    return (y_prompt, y_sample, jnp.stack(new_ak, axis=1), jnp.stack(new_av, axis=1), jnp.stack(new_st, axis=1),
            jnp.stack(new_nk, axis=1), jnp.stack(new_nv, axis=1))
```

```python
import functools
import math
import jax, jax.numpy as jnp
from jax import lax
import numpy as np
from jax.experimental import pallas as pl
from jax.experimental.pallas import tpu as pltpu

D_MODEL = 1024
BATCH = 32
SEQ = 256
DEPTH = 4
DEC_BATCH = 4
DEC_SEQ = 4096
PAST_LEN = 512

GRID_W = 64
HEAD_DIM = 64
N_EVEN = (DEPTH + 1) // 2
N_ODD = DEPTH // 2
Q_BLOCK = 128
A_HEADS = D_MODEL // 128
A_KV_HEADS = A_HEADS // 4
A_WINDOW = 128
A_BLOCK = 128
ROPE_BASE = 10000.0
HY_CH = D_MODEL // 2
HY_ORDER = 2
HY_SHORT = 3
HY_EMB = 33
HY_FILT_W = 64
HY_FAST_DECAY = 0.3
HY_SLOW_DECAY = 1.5
HY_DECAY_TARGET = 1e-2
HY_SHIFT = 0.05
C_HEADS = D_MODEL // 128
C_SHORT = 3
C_CHUNK = 64
D_HEADS = D_MODEL // 128
NA_KH_MAX = 8
NA_KW = 16
N_EXPERTS = 16
EC_CAPACITY = 2
MOE_D_FF = D_MODEL
EPS = 1e-6
NEG_INF = -1e30

A_Q_W = A_HEADS * HEAD_DIM
A_KV_W = A_KV_HEADS * HEAD_DIM
EVEN_IN = A_Q_W + 2 * A_KV_W + 3 * HY_CH
EVEN_MIX = A_Q_W + HY_CH
C_W = C_HEADS * HEAD_DIM
D_W = D_HEADS * HEAD_DIM
ODD_IN = 4 * C_W + 4 * C_HEADS + 3 * D_W
ODD_MIX = C_W + D_W

VMEM_LIMIT_BYTES = 48 * 1024 * 1024


def _mm_kernel(x_ref, w_ref, o_ref):
    o_ref[...] = jnp.dot(x_ref[...].astype(jnp.bfloat16), w_ref[...], preferred_element_type=jnp.float32)


def pallas_matmul(x, w, tm=256):
    M, K = x.shape
    N = w.shape[1]
    assert M % tm == 0
    return pl.pallas_call(
        _mm_kernel,
        grid=(M // tm,),
        in_specs=[pl.BlockSpec((tm, K), lambda i: (i, 0)), pl.BlockSpec((K, N), lambda i: (0, 0))],
        out_specs=pl.BlockSpec((tm, N), lambda i: (i, 0)),
        out_shape=jax.ShapeDtypeStruct((M, N), jnp.float32),
        compiler_params=pltpu.CompilerParams(dimension_semantics=("arbitrary",), vmem_limit_bytes=VMEM_LIMIT_BYTES),
    )(x, w.astype(jnp.bfloat16))


def proj(x, w):
    B, L, K = x.shape
    return pallas_matmul(x.reshape(B * L, K), w).reshape(B, L, w.shape[1])


def _mm_multi_kernel(x_ref, *refs):
    n = len(refs) // 2
    x = x_ref[...].astype(jnp.bfloat16)
    for w_ref, o_ref in zip(refs[:n], refs[n:]):
        o_ref[...] = jnp.dot(x, w_ref[...], preferred_element_type=jnp.float32).astype(o_ref.dtype)


def proj_multi(x, weights, out_dtypes, tm=256):
    B, L, K = x.shape
    M = B * L
    assert M % tm == 0
    outs = pl.pallas_call(
        _mm_multi_kernel,
        grid=(M // tm,),
        in_specs=[pl.BlockSpec((tm, K), lambda i: (i, 0))] + [pl.BlockSpec(w.shape, lambda i: (0, 0)) for w in weights],
        out_specs=[pl.BlockSpec((tm, w.shape[1]), lambda i: (i, 0)) for w in weights],
        out_shape=[jax.ShapeDtypeStruct((M, w.shape[1]), dt) for w, dt in zip(weights, out_dtypes)],
        compiler_params=pltpu.CompilerParams(dimension_semantics=("arbitrary",), vmem_limit_bytes=VMEM_LIMIT_BYTES),
        name="in_projection",
    )(x.reshape(M, K), *[w.astype(jnp.bfloat16) for w in weights])
    return [o.reshape(B, L, o.shape[1]) for o in outs]


def _mm2_kernel(a_ref, b_ref, wa_ref, wb_ref, o_ref):
    o_ref[...] = (jnp.dot(a_ref[...].astype(jnp.bfloat16), wa_ref[...], preferred_element_type=jnp.float32)
                  + jnp.dot(b_ref[...].astype(jnp.bfloat16), wb_ref[...], preferred_element_type=jnp.float32))


def proj_concat(a, b, w, tm=512):
    B, L, Ka = a.shape
    Kb = b.shape[2]
    N = w.shape[1]
    M = B * L
    assert M % tm == 0 and w.shape[0] == Ka + Kb
    wb16 = w.astype(jnp.bfloat16)
    out = pl.pallas_call(
        _mm2_kernel,
        grid=(M // tm,),
        in_specs=[pl.BlockSpec((tm, Ka), lambda i: (i, 0)), pl.BlockSpec((tm, Kb), lambda i: (i, 0)),
                  pl.BlockSpec((Ka, N), lambda i: (0, 0)), pl.BlockSpec((Kb, N), lambda i: (0, 0))],
        out_specs=pl.BlockSpec((tm, N), lambda i: (i, 0)),
        out_shape=jax.ShapeDtypeStruct((M, N), jnp.float32),
        compiler_params=pltpu.CompilerParams(dimension_semantics=("arbitrary",), vmem_limit_bytes=VMEM_LIMIT_BYTES),
        name="out_projection",
    )(a.reshape(M, Ka), b.reshape(M, Kb), wb16[:Ka], wb16[Ka:])
    return out.reshape(B, L, N)


LANES = 128
BF16 = jnp.bfloat16
F32 = jnp.float32


def _dot_nt(a, b):
    return lax.dot_general(a, b, (((1,), (1,)), ((), ())), preferred_element_type=F32)


def _dot(a, b):
    return jnp.dot(a, b, preferred_element_type=F32)


def _low_half(shape):
    return lax.broadcasted_iota(jnp.int32, shape, 1) < HEAD_DIM


def _softmax_pv(scores, values, extra_logit=None):
    m = scores[0].max(axis=-1, keepdims=True)
    for s in scores[1:]:
        m = jnp.maximum(m, s.max(axis=-1, keepdims=True))
    if extra_logit is not None:
        m = jnp.maximum(m, extra_logit)
    l = None if extra_logit is None else jnp.exp(extra_logit - m)
    o = None
    for s, v in zip(scores, values):
        p = jnp.exp(s - m)
        ps = p.sum(axis=-1, keepdims=True)
        l = ps if l is None else l + ps
        pv = _dot(p.astype(BF16), v)
        o = pv if o is None else o + pv
    return o / l


def _place_head(q_slab, src_half, dst_half, low):
    x = q_slab if src_half == dst_half else pltpu.roll(q_slab, HEAD_DIM, axis=1)
    return jnp.where(low if dst_half == 0 else ~low, x, jnp.zeros_like(x))


NA_KEYS = NA_KH_MAX * GRID_W


def _na_kernel(q_ref, k_ref, v_ref, ck_ref, cv_ref, tab_ref, o_ref):
    r = pl.program_id(1)
    rows = k_ref.shape[1] // GRID_W
    rs = jnp.clip(r - NA_KH_MAX // 2, 0, rows - NA_KH_MAX)
    start = pl.multiple_of(rs * GRID_W, GRID_W)
    scale = HEAD_DIM ** -0.5
    low = _low_half((GRID_W, LANES))
    for p in range(D_HEADS // 2):
        cols = slice(p * LANES, (p + 1) * LANES)
        qp = q_ref[0, :, cols]
        kp = k_ref[0, pl.ds(start, NA_KEYS), cols]
        vp = v_ref[0, pl.ds(start, NA_KEYS), cols]
        ckp = ck_ref[0, :, cols]
        cvp = cv_ref[0, :, cols]
        outs = []
        for half in range(2):
            qm = jnp.where(low if half == 0 else ~low, qp, jnp.zeros_like(qp))
            s_loc = _dot_nt(qm, kp) * scale + tab_ref[2 * p + half, 0]
            s_ctx = _dot_nt(qm, ckp) * scale
            outs.append(_softmax_pv([s_loc, s_ctx], [vp, cvp]))
        o_ref[0, :, cols] = jnp.where(low, outs[0], outs[1]).astype(o_ref.dtype)


def na_bias_table(rpb):
    col = jnp.arange(GRID_W)
    cs = jnp.clip(col - NA_KW // 2, 0, GRID_W - NA_KW)
    col_ok = (col[None, :] >= cs[:, None]) & (col[None, :] < cs[:, None] + NA_KW)
    coff = jnp.clip(col[None, :] - col[:, None] + NA_KW - 1, 0, 2 * NA_KW - 2)
    base = jnp.where(col_ok[None, None], rpb.astype(F32)[:, :, coff], NEG_INF)
    tab = jnp.stack([base[:, o:o + NA_KH_MAX] for o in range(NA_KH_MAX)], axis=1)
    return tab.transpose(0, 1, 3, 2, 4).reshape(D_HEADS, NA_KH_MAX, GRID_W, NA_KEYS)


def neighbourhood_attention_pallas(q, k, v, ck, cv, rpb):
    B, T, W = q.shape
    P = ck.shape[1]
    rows = T // GRID_W
    assert rows >= NA_KH_MAX and W == D_W
    tab = na_bias_table(rpb)

    def tab_index(b, r):
        rs = jnp.clip(r - NA_KH_MAX // 2, 0, rows - NA_KH_MAX)
        return (0, rs - r + NA_KH_MAX - 1, 0, 0)

    return pl.pallas_call(
        _na_kernel,
        grid=(B, rows),
        in_specs=[
            pl.BlockSpec((1, GRID_W, W), lambda b, r: (b, r, 0)),
            pl.BlockSpec((1, T, W), lambda b, r: (b, 0, 0)),
            pl.BlockSpec((1, T, W), lambda b, r: (b, 0, 0)),
            pl.BlockSpec((1, P, W), lambda b, r: (b, 0, 0)),
            pl.BlockSpec((1, P, W), lambda b, r: (b, 0, 0)),
            pl.BlockSpec((D_HEADS, 1, GRID_W, NA_KEYS), tab_index),
        ],
        out_specs=pl.BlockSpec((1, GRID_W, W), lambda b, r: (b, r, 0)),
        out_shape=jax.ShapeDtypeStruct((B, T, W), BF16),
        compiler_params=pltpu.CompilerParams(dimension_semantics=("arbitrary", "arbitrary"),
                                             vmem_limit_bytes=VMEM_LIMIT_BYTES),
        name="na_attention",
    )(q, k, v, ck, cv, tab)


def rope_tables(T):
    cos, sin = axial_rope(T)
    cos, sin = cos[:, 0, :], sin[:, 0, :]
    cos_t = jnp.concatenate([cos, cos, cos, cos], axis=-1)
    sin_t = jnp.concatenate([-sin, sin, -sin, sin], axis=-1)
    return cos_t, sin_t


def _rope(x, cos_t, sin_t):
    half = HEAD_DIM // 2
    lane = lax.broadcasted_iota(jnp.int32, x.shape, 1)
    first = (lane % HEAD_DIM) < half
    swapped = jnp.where(first, pltpu.roll(x, LANES - half, axis=1), pltpu.roll(x, half, axis=1))
    return x * cos_t + swapped * sin_t


def _win_kernel(sink_ref, q_ref, k_ref, v_ref, ck_ref, cv_ref, cos_ref, sin_ref, o_ref):
    i = pl.program_id(1)
    T = k_ref.shape[1]
    span = 3 * A_BLOCK
    start = pl.multiple_of(jnp.clip((i - 1) * A_BLOCK, 0, T - span), A_BLOCK)
    delta = i * A_BLOCK - start
    q0 = pl.multiple_of(i * A_BLOCK, A_BLOCK)
    scale = HEAD_DIM ** -0.5
    kw = _rope(k_ref[0, pl.ds(start, span), :], cos_ref[pl.ds(start, span), :], sin_ref[pl.ds(start, span), :]).astype(BF16)
    vw = v_ref[0, pl.ds(start, span), :].astype(BF16)
    ck = ck_ref[0]
    cv = cv_ref[0]
    cos_q = cos_ref[pl.ds(q0, A_BLOCK), :]
    sin_q = sin_ref[pl.ds(q0, A_BLOCK), :]
    qi = lax.broadcasted_iota(jnp.int32, (A_BLOCK, span), 0)
    kj = lax.broadcasted_iota(jnp.int32, (A_BLOCK, span), 1)
    band = jnp.abs(kj - delta - qi) <= A_WINDOW
    low = _low_half((A_BLOCK, LANES))
    group = A_HEADS // A_KV_HEADS
    for p in range(A_HEADS // 2):
        cols = slice(p * LANES, (p + 1) * LANES)
        q_slab = _rope(q_ref[0, :, cols], cos_q, sin_q)
        outs = []
        for half in range(2):
            h = 2 * p + half
            kv = h // group
            qm = _place_head(q_slab, half, kv, low).astype(BF16)
            s_loc = jnp.where(band, _dot_nt(qm, kw) * scale, NEG_INF)
            s_ctx = _dot_nt(qm, ck) * scale
            o = _softmax_pv([s_loc, s_ctx], [vw, cv], extra_logit=sink_ref[h])
            outs.append(o if kv == half else pltpu.roll(o, HEAD_DIM, axis=1))
        o_ref[0, :, cols] = jnp.where(low, outs[0], outs[1]).astype(o_ref.dtype)


def window_attention_pallas(q, k, v, ck, cv, sink):
    B, T, QW = q.shape
    KW = k.shape[2]
    P = ck.shape[1]
    assert KW == LANES and QW == A_Q_W and T % A_BLOCK == 0 and T >= 3 * A_BLOCK
    cos_t, sin_t = rope_tables(T)
    return pl.pallas_call(
        _win_kernel,
        grid=(B, T // A_BLOCK),
        in_specs=[
            pl.BlockSpec(memory_space=pltpu.SMEM),
            pl.BlockSpec((1, A_BLOCK, QW), lambda b, i: (b, i, 0)),
            pl.BlockSpec((1, T, KW), lambda b, i: (b, 0, 0)),
            pl.BlockSpec((1, T, KW), lambda b, i: (b, 0, 0)),
            pl.BlockSpec((1, P, KW), lambda b, i: (b, 0, 0)),
            pl.BlockSpec((1, P, KW), lambda b, i: (b, 0, 0)),
            pl.BlockSpec((T, LANES), lambda b, i: (0, 0)),
            pl.BlockSpec((T, LANES), lambda b, i: (0, 0)),
        ],
        out_specs=pl.BlockSpec((1, A_BLOCK, QW), lambda b, i: (b, i, 0)),
        out_shape=jax.ShapeDtypeStruct((B, T, QW), BF16),
        compiler_params=pltpu.CompilerParams(dimension_semantics=("arbitrary", "arbitrary"),
                                             vmem_limit_bytes=VMEM_LIMIT_BYTES),
        name="window_attention",
    )(sink.astype(F32), q, k, v, ck, cv, cos_t, sin_t)


def _ctx_kernel(sink_ref, q_ref, k_ref, v_ref, o_ref, *, n_q_heads, n_kv_heads, use_sink):
    S = q_ref.shape[1]
    scale = HEAD_DIM ** -0.5
    low = _low_half((S, LANES))
    group = n_q_heads // n_kv_heads
    for p in range(n_q_heads // 2):
        cols = slice(p * LANES, (p + 1) * LANES)
        q_slab = q_ref[0, :, cols]
        outs = []
        for half in range(2):
            h = 2 * p + half
            kv = h // group
            kcols = slice((kv // 2) * LANES, (kv // 2 + 1) * LANES)
            qm = _place_head(q_slab, half, kv % 2, low).astype(BF16)
            s = _dot_nt(qm, k_ref[0, :, kcols].astype(BF16)) * scale
            o = _softmax_pv([s], [v_ref[0, :, kcols].astype(BF16)], extra_logit=sink_ref[h] if use_sink else None)
            outs.append(o if kv % 2 == half else pltpu.roll(o, HEAD_DIM, axis=1))
        o_ref[0, :, cols] = jnp.where(low, outs[0], outs[1]).astype(o_ref.dtype)


def context_attention_pallas(q, k, v, sink, n_q_heads, n_kv_heads):
    B, S, QW = q.shape
    KW = k.shape[2]
    use_sink = sink is not None
    sink_arr = sink.astype(F32) if use_sink else jnp.zeros((n_q_heads,), F32)
    return pl.pallas_call(
        functools.partial(_ctx_kernel, n_q_heads=n_q_heads, n_kv_heads=n_kv_heads, use_sink=use_sink),
        grid=(B,),
        in_specs=[
            pl.BlockSpec(memory_space=pltpu.SMEM),
            pl.BlockSpec((1, S, QW), lambda b: (b, 0, 0)),
            pl.BlockSpec((1, S, KW), lambda b: (b, 0, 0)),
            pl.BlockSpec((1, S, KW), lambda b: (b, 0, 0)),
        ],
        out_specs=pl.BlockSpec((1, S, QW), lambda b: (b, 0, 0)),
        out_shape=jax.ShapeDtypeStruct((B, S, QW), BF16),
        compiler_params=pltpu.CompilerParams(dimension_semantics=("arbitrary",), vmem_limit_bytes=VMEM_LIMIT_BYTES),
        name="context_attention",
    )(sink_arr, q, k, v)


HIGHEST = lax.Precision.HIGHEST
GDN_TM = 256
SUBLANES = 8


def _head_pair_sum_matrix():
    a = lax.broadcasted_iota(jnp.int32, (LANES, LANES), 0) // HEAD_DIM
    b = lax.broadcasted_iota(jnp.int32, (LANES, LANES), 1) // HEAD_DIM
    return (a == b).astype(F32)


def _gdn_prep_kernel(x_ref, prev_ref, next_ref, ab_ref, cw_ref, a_ref, dtb_ref, q_ref, k_ref, v_ref, gb_ref):
    i = pl.program_id(1)
    n = pl.num_programs(1)
    x = x_ref[0]
    tm = x.shape[0]
    row = lax.broadcasted_iota(jnp.int32, x.shape, 0)
    prev_row = jnp.where(i > 0, prev_ref[0, SUBLANES - 1:SUBLANES, :], 0.0)
    next_row = jnp.where(i < n - 1, next_ref[0, 0:1, :], 0.0)
    x_prev = jnp.where(row == 0, prev_row, pltpu.roll(x, 1, axis=0))
    x_next = jnp.where(row == tm - 1, next_row, pltpu.roll(x, tm - 1, axis=0))
    y = x_prev * cw_ref[0:1, :] + x * cw_ref[1:2, :] + x_next * cw_ref[2:3, :]
    y = y * jax.nn.sigmoid(y)
    pmat = _head_pair_sum_matrix()
    for p in range(C_W // LANES):
        qs = y[:, p * LANES:(p + 1) * LANES]
        ks = y[:, C_W + p * LANES:C_W + (p + 1) * LANES]
        q_ref[0, :, p * LANES:(p + 1) * LANES] = qs * lax.rsqrt(jnp.dot(qs * qs, pmat, precision=HIGHEST, preferred_element_type=F32) + EPS) * (HEAD_DIM ** -0.5)
        k_ref[0, :, p * LANES:(p + 1) * LANES] = ks * lax.rsqrt(jnp.dot(ks * ks, pmat, precision=HIGHEST, preferred_element_type=F32) + EPS)
    v_ref[0] = y[:, 2 * C_W:]
    ab = ab_ref[0, :, 0:4 * C_HEADS]
    lane = lax.broadcasted_iota(jnp.int32, ab.shape, 1)
    is_beta = (lane // C_HEADS) % 2 == 1
    t = ab + dtb_ref[...]
    softplus = jnp.maximum(t, 0.0) + jnp.log1p(jnp.exp(-jnp.abs(t)))
    gb = jnp.where(is_beta, jax.nn.sigmoid(ab), -jnp.exp(a_ref[...]) * softplus)
    gb_ref[0, 0] = gb[:, 0:2 * C_HEADS]
    gb_ref[0, 1] = gb[:, 2 * C_HEADS:4 * C_HEADS]


def gdn_prep(zqkv, zab, conv_w, a_log, dt_bias):
    B, L, W3 = zqkv.shape
    tm = GDN_TM
    assert L % tm == 0
    nb = tm // SUBLANES
    zero = jnp.zeros((C_HEADS,), F32)
    a_lane = jnp.concatenate([a_log[0], zero, a_log[1], zero]).astype(F32)[None, :]
    dtb_lane = jnp.concatenate([dt_bias[0], zero, dt_bias[1], zero]).astype(F32)[None, :]
    outs = pl.pallas_call(
        _gdn_prep_kernel,
        grid=(B, L // tm),
        in_specs=[
            pl.BlockSpec((1, tm, W3), lambda b, i: (b, i, 0)),
            pl.BlockSpec((1, SUBLANES, W3), lambda b, i: (b, jnp.maximum(i * nb - 1, 0), 0)),
            pl.BlockSpec((1, SUBLANES, W3), lambda b, i: (b, jnp.minimum((i + 1) * nb, L // SUBLANES - 1), 0)),
            pl.BlockSpec((1, tm, LANES), lambda b, i: (b, i, 0)),
            pl.BlockSpec((C_SHORT, W3), lambda b, i: (0, 0)),
            pl.BlockSpec((1, 4 * C_HEADS), lambda b, i: (0, 0)),
            pl.BlockSpec((1, 4 * C_HEADS), lambda b, i: (0, 0)),
        ],
        out_specs=[
            pl.BlockSpec((1, tm, C_W), lambda b, i: (b, i, 0)),
            pl.BlockSpec((1, tm, C_W), lambda b, i: (b, i, 0)),
            pl.BlockSpec((1, tm, C_W), lambda b, i: (b, i, 0)),
            pl.BlockSpec((1, 2, tm, 2 * C_HEADS), lambda b, i: (b, 0, i, 0)),
        ],
        out_shape=[jax.ShapeDtypeStruct((B, L, C_W), F32)] * 3 + [jax.ShapeDtypeStruct((B, 2, L, 2 * C_HEADS), F32)],
        compiler_params=pltpu.CompilerParams(dimension_semantics=("arbitrary", "arbitrary"),
                                             vmem_limit_bytes=VMEM_LIMIT_BYTES),
        name="gdn_prep",
    )(zqkv, zqkv, zqkv, zab, conv_w.astype(F32), a_lane, dtb_lane)
    return outs


def _gdn_kernel(q_ref, k_ref, v_ref, gb_ref, zg_ref, s0_ref, nw_ref, o_ref, st_ref, s_scr, of_scr):
    d = pl.program_id(1)
    c = pl.program_id(2)
    n = pl.num_programs(2)
    C = C_CHUNK
    fwd = d == 0
    chunk = jnp.where(fwd, c, n - 1 - c)
    r0 = pl.multiple_of(chunk * C, C)

    @pl.when(c == 0)
    def _():
        s_scr[...] = s0_ref[0, 0]

    row = lax.broadcasted_iota(jnp.int32, (C, C), 0)
    col = lax.broadcasted_iota(jnp.int32, (C, C), 1)
    ahead = jnp.where(fwd, row - col, col - row)
    incl = ahead >= 0
    strict = ahead > 0
    g = gb_ref[0, 0, :, 0:C_HEADS]
    beta = gb_ref[0, 0, :, C_HEADS:2 * C_HEADS]
    gc = jnp.dot(incl.astype(F32), g, precision=HIGHEST, preferred_element_type=F32)
    gc_t = gc.T
    g_last = jnp.where(fwd, gc[C - 1:C, :], gc[0:1, :])
    H = range(C_HEADS)
    heads = [slice(h * HEAD_DIM, (h + 1) * HEAD_DIM) for h in H]
    s_old = [s_scr[h] for h in H]
    qs = [q_ref[0, :, hs] for hs in heads]
    ks = [k_ref[0, :, hs] for hs in heads]
    vs = [v_ref[0, :, hs] for hs in heads]
    gcol = [gc[:, h:h + 1] for h in H]
    bcol = [beta[:, h:h + 1] for h in H]
    gl = [g_last[:, h:h + 1] for h in H]
    gamma = [jnp.exp(jnp.where(incl, gcol[h] - gc_t[h:h + 1, :], NEG_INF)) for h in H]
    egc = [jnp.exp(gcol[h]) for h in H]
    kb = [ks[h].astype(BF16) for h in H]
    nmat = [jnp.where(strict, _dot_nt(kb[h], kb[h]) * gamma[h], 0.0) * bcol[h] for h in H]
    attn = [(_dot_nt(qs[h].astype(BF16), kb[h]) * gamma[h]).astype(BF16) for h in H]
    xr = row ^ col
    eye = (row == col).astype(F32)
    tinv = [eye - jnp.where((xr >> 1) == 0, nmat[h], 0.0) for h in H]
    for lvl in range(1, 6):
        off_diag = (xr >> lvl) == 1
        wmat = [_dot(jnp.where(off_diag, nmat[h], 0.0).astype(BF16), tinv[h].astype(BF16)).astype(BF16) for h in H]
        tinv = [tinv[h] - _dot(tinv[h].astype(BF16), wmat[h]) for h in H]
    x = [jnp.concatenate([vs[h] * bcol[h], ks[h] * (bcol[h] * egc[h])], axis=1) for h in H]
    x = [x[h] + _dot((tinv[h] - eye).astype(BF16), x[h].astype(BF16)) for h in H]
    sb = [s_old[h].astype(BF16) for h in H]
    v_new = [x[h][:, :HEAD_DIM] - _dot(x[h][:, HEAD_DIM:].astype(BF16), sb[h]) for h in H]
    vb = [v_new[h].astype(BF16) for h in H]
    outs = [_dot((qs[h] * egc[h]).astype(BF16), sb[h]) + _dot(attn[h], vb[h]) for h in H]
    kd = [(ks[h] * jnp.exp(gl[h] - gcol[h])).astype(BF16) for h in H]
    s_new = [s_old[h] * jnp.exp(gl[h]) + lax.dot_general(kd[h], vb[h], (((0,), (0,)), ((), ())), preferred_element_type=F32)
             for h in H]
    for h in H:
        s_scr[h] = s_new[h]
    o = jnp.concatenate(outs, axis=1)

    @pl.when(fwd)
    def _():
        of_scr[pl.ds(r0, C), :] = o

    @pl.when(jnp.logical_not(fwd))
    def _():
        tot = of_scr[pl.ds(r0, C), :] + o
        pmat = _head_pair_sum_matrix()
        zg = zg_ref[0]
        gate = zg * jax.nn.sigmoid(zg)
        for p in range(C_W // LANES):
            cols = slice(p * LANES, (p + 1) * LANES)
            t = tot[:, cols]
            ms = jnp.dot(t * t, pmat, precision=HIGHEST, preferred_element_type=F32) * (1.0 / HEAD_DIM)
            o_ref[0, :, cols] = (t * lax.rsqrt(ms + EPS) * nw_ref[:, cols] * gate[:, cols]).astype(o_ref.dtype)

    @pl.when(c == n - 1)
    def _():
        st_ref[0, 0] = s_scr[...]


def gdn_scan(q, k, v, gb, zg, s0, norm_w):
    B, L, W = q.shape
    C = C_CHUNK
    n = L // C
    assert L % C == 0 and W == C_W
    chunk_of = lambda d, c: jnp.where(d == 0, c, n - 1 - c)
    seq_spec = pl.BlockSpec((1, C, W), lambda b, d, c: (b, chunk_of(d, c), 0))
    state_spec = pl.BlockSpec((1, 1, C_HEADS, HEAD_DIM, HEAD_DIM), lambda b, d, c: (b, d, 0, 0, 0))
    nw = jnp.tile(norm_w.astype(F32), C_HEADS)[None, :]
    return pl.pallas_call(
        _gdn_kernel,
        grid=(B, 2, n),
        in_specs=[
            seq_spec, seq_spec, seq_spec,
            pl.BlockSpec((1, 1, C, 2 * C_HEADS), lambda b, d, c: (b, d, chunk_of(d, c), 0)),
            seq_spec,
            state_spec,
            pl.BlockSpec((1, W), lambda b, d, c: (0, 0)),
        ],
        out_specs=[
            pl.BlockSpec((1, C, W), lambda b, d, c: (b, jnp.where(d == 0, n - 1, n - 1 - c), 0)),
            state_spec,
        ],
        out_shape=[jax.ShapeDtypeStruct((B, L, W), BF16), jax.ShapeDtypeStruct(s0.shape, F32)],
        scratch_shapes=[pltpu.VMEM((C_HEADS, HEAD_DIM, HEAD_DIM), F32), pltpu.VMEM((L, W), F32)],
        compiler_params=pltpu.CompilerParams(dimension_semantics=("arbitrary", "arbitrary", "arbitrary"),
                                             vmem_limit_bytes=VMEM_LIMIT_BYTES),
        name="gdn_scan",
    )(q, k, v, gb, zg, s0.astype(F32), nw)


MOE_TT = 512
MOE_RT = 512


def _moe_router_kernel(x_ref, g_ref, shift_ref, scale_ref, wr_ref, h_ref, aff_ref):
    x = x_ref[0]
    y = x * lax.rsqrt(jnp.mean(x * x, axis=-1, keepdims=True) + EPS) * g_ref[...]
    h = y * (1.0 + scale_ref[0]) + shift_ref[0]
    h_ref[0] = h.astype(BF16)
    logits = jnp.dot(h, wr_ref[...], precision=HIGHEST, preferred_element_type=F32)
    lane = lax.broadcasted_iota(jnp.int32, logits.shape, 1)
    logits = jnp.where(lane < N_EXPERTS, logits, NEG_INF)
    e = jnp.exp(logits - logits.max(axis=-1, keepdims=True))
    aff_ref[0] = e / e.sum(axis=-1, keepdims=True)


def _moe_select_kernel(aff_ref, slot_ref, *, cap):
    T = aff_ref.shape[1]
    bits = pltpu.bitcast(aff_ref[0], jnp.int32)

    def bisect(i, v):
        cand = v | (1 << (30 - i))
        cnt = jnp.sum((bits >= cand).astype(jnp.int32), axis=0, keepdims=True)
        return jnp.where(cnt >= cap, cand, v)

    thr = lax.fori_loop(0, 31, bisect, jnp.zeros((1, LANES), jnp.int32))
    gt = (bits > thr).astype(F32)
    eq = (bits == thr).astype(F32)
    need = cap - jnp.sum(gt, axis=0, keepdims=True)
    blk = min(T, MOE_TT)
    r = lax.broadcasted_iota(jnp.int32, (blk, blk), 0)
    c = lax.broadcasted_iota(jnp.int32, (blk, blk), 1)
    before = (c < r).astype(BF16)
    carry_gt = jnp.zeros((1, LANES), F32)
    carry_eq = jnp.zeros((1, LANES), F32)
    for b in range(T // blk):
        rows = slice(b * blk, (b + 1) * blk)
        gt_b, eq_b = gt[rows], eq[rows]
        pos_gt = _dot(before, gt_b.astype(BF16)) + carry_gt
        pos_eq = _dot(before, eq_b.astype(BF16)) + carry_eq
        chosen = gt_b + eq_b * (pos_eq < need).astype(F32)
        slot_ref[0, rows, :] = jnp.where(chosen > 0.5, pos_gt + jnp.minimum(pos_eq, need), -1.0)
        carry_gt = carry_gt + jnp.sum(gt_b, axis=0, keepdims=True)
        carry_eq = carry_eq + jnp.sum(eq_b, axis=0, keepdims=True)


def _expert_column(a, e):
    lane = lax.broadcasted_iota(jnp.int32, a.shape, 1)
    return jnp.sum(jnp.where(lane == e, a, 0.0), axis=1, keepdims=True)


def _one_hot_slots(slot_col, cap):
    s = lax.broadcasted_iota(jnp.int32, (slot_col.shape[0], cap), 1).astype(F32)
    return (slot_col == s).astype(BF16)


def _moe_gather_kernel(slot_ref, h_ref, xg_ref, acc_ref, *, cap):
    e, k = pl.program_id(1), pl.program_id(2)

    @pl.when(k == 0)
    def _():
        acc_ref[...] = jnp.zeros_like(acc_ref)

    pt = _one_hot_slots(_expert_column(slot_ref[0], e), cap)
    acc_ref[...] += lax.dot_general(pt, h_ref[0], (((0,), (0,)), ((), ())), preferred_element_type=F32)

    @pl.when(k == pl.num_programs(2) - 1)
    def _():
        xg_ref[0] = acc_ref[...].astype(xg_ref.dtype)


def _moe_ffn_kernel(x_ref, wg_ref, wu_ref, wd_ref, y_ref, wg_s, wu_s, wd_s):
    @pl.when(pl.program_id(1) == 0)
    def _():
        wg_s[...] = wg_ref[0].astype(BF16)
        wu_s[...] = wu_ref[0].astype(BF16)
        wd_s[...] = wd_ref[0].astype(BF16)

    x = x_ref[0]
    a = _dot(x, wg_s[...])
    u = _dot(x, wu_s[...])
    hid = (a * jax.nn.sigmoid(a) * u).astype(BF16)
    y_ref[0] = _dot(hid, wd_s[...]).astype(y_ref.dtype)


def _moe_scatter_kernel(slot_ref, aff_ref, y_ref, x_ref, gate_ref, o_ref, acc_ref, *, cap):
    e = pl.program_id(2)

    @pl.when(e == 0)
    def _():
        acc_ref[...] = jnp.zeros_like(acc_ref)

    pt = _one_hot_slots(_expert_column(slot_ref[0], e), cap)
    acc_ref[...] += _expert_column(aff_ref[0], e) * _dot(pt, y_ref[0])

    @pl.when(e == pl.num_programs(2) - 1)
    def _():
        o_ref[0] = x_ref[0] + gate_ref[0] * acc_ref[...]


def moe_route(x, g, shift, scale, w_router):
    B, T, D = x.shape
    tt = min(T, MOE_TT)
    cap = EC_CAPACITY * T // N_EXPERTS
    per_request = shift.shape[0] == B
    mod_spec = pl.BlockSpec((1, 1, D), (lambda b, k: (b, 0, 0)) if per_request else (lambda b, k: (0, 0, 0)))
    wr = jnp.concatenate([w_router.astype(F32), jnp.zeros((D, LANES - N_EXPERTS), F32)], axis=1)
    h, aff = pl.pallas_call(
        _moe_router_kernel,
        grid=(B, T // tt),
        in_specs=[pl.BlockSpec((1, tt, D), lambda b, k: (b, k, 0)), pl.BlockSpec((1, D), lambda b, k: (0, 0)),
                  mod_spec, mod_spec, pl.BlockSpec((D, LANES), lambda b, k: (0, 0))],
        out_specs=[pl.BlockSpec((1, tt, D), lambda b, k: (b, k, 0)), pl.BlockSpec((1, tt, LANES), lambda b, k: (b, k, 0))],
        out_shape=[jax.ShapeDtypeStruct((B, T, D), BF16), jax.ShapeDtypeStruct((B, T, LANES), F32)],
        compiler_params=pltpu.CompilerParams(dimension_semantics=("arbitrary", "arbitrary"),
                                             vmem_limit_bytes=VMEM_LIMIT_BYTES),
        name="moe_router",
    )(x, g.astype(F32)[None, :], shift, scale, wr)
    slot = pl.pallas_call(
        functools.partial(_moe_select_kernel, cap=cap),
        grid=(B,),
        in_specs=[pl.BlockSpec((1, T, LANES), lambda b: (b, 0, 0))],
        out_specs=pl.BlockSpec((1, T, LANES), lambda b: (b, 0, 0)),
        out_shape=jax.ShapeDtypeStruct((B, T, LANES), F32),
        compiler_params=pltpu.CompilerParams(dimension_semantics=("arbitrary",), vmem_limit_bytes=VMEM_LIMIT_BYTES),
        name="moe_select",
    )(aff)
    return h, aff, slot


def moe_gather(h, slot):
    B, T, D = h.shape
    tt = min(T, MOE_TT)
    cap = EC_CAPACITY * T // N_EXPERTS
    return pl.pallas_call(
        functools.partial(_moe_gather_kernel, cap=cap),
        grid=(B, N_EXPERTS, T // tt),
        in_specs=[pl.BlockSpec((1, tt, LANES), lambda b, e, k: (b, k, 0)), pl.BlockSpec((1, tt, D), lambda b, e, k: (b, k, 0))],
        out_specs=pl.BlockSpec((1, cap, D), lambda b, e, k: (e, b, 0)),
        out_shape=jax.ShapeDtypeStruct((N_EXPERTS, B * cap, D), BF16),
        scratch_shapes=[pltpu.VMEM((cap, D), F32)],
        compiler_params=pltpu.CompilerParams(dimension_semantics=("arbitrary",) * 3, vmem_limit_bytes=VMEM_LIMIT_BYTES),
        name="moe_gather",
    )(slot, h)


def moe_ffn(xg, w_gate, w_up, w_down):
    E, R, D = xg.shape
    rt = min(R, MOE_RT)
    w_spec = pl.BlockSpec((1, D, MOE_D_FF), lambda e, r: (e, 0, 0))
    return pl.pallas_call(
        _moe_ffn_kernel,
        grid=(E, R // rt),
        in_specs=[pl.BlockSpec((1, rt, D), lambda e, r: (e, r, 0)), w_spec, w_spec,
                  pl.BlockSpec((1, MOE_D_FF, D), lambda e, r: (e, 0, 0))],
        out_specs=pl.BlockSpec((1, rt, D), lambda e, r: (e, r, 0)),
        out_shape=jax.ShapeDtypeStruct((E, R, D), BF16),
        scratch_shapes=[pltpu.VMEM((D, MOE_D_FF), BF16), pltpu.VMEM((D, MOE_D_FF), BF16), pltpu.VMEM((MOE_D_FF, D), BF16)],
        compiler_params=pltpu.CompilerParams(dimension_semantics=("arbitrary", "arbitrary"),
                                             vmem_limit_bytes=VMEM_LIMIT_BYTES),
        name="moe_ffn",
    )(xg, w_gate, w_up, w_down)


def moe_scatter(slot, aff, y, x, gate):
    B, T, D = x.shape
    tt = min(T, MOE_TT)
    cap = EC_CAPACITY * T // N_EXPERTS
    per_request = gate.shape[0] == B
    tile = lambda b, k, e: (b, k, 0)
    return pl.pallas_call(
        functools.partial(_moe_scatter_kernel, cap=cap),
        grid=(B, T // tt, N_EXPERTS),
        in_specs=[pl.BlockSpec((1, tt, LANES), tile), pl.BlockSpec((1, tt, LANES), tile),
                  pl.BlockSpec((1, cap, D), lambda b, k, e: (e, b, 0)), pl.BlockSpec((1, tt, D), tile),
                  pl.BlockSpec((1, 1, D), (lambda b, k, e: (b, 0, 0)) if per_request else (lambda b, k, e: (0, 0, 0)))],
        out_specs=pl.BlockSpec((1, tt, D), tile),
        out_shape=jax.ShapeDtypeStruct((B, T, D), F32),
        scratch_shapes=[pltpu.VMEM((tt, D), F32)],
        compiler_params=pltpu.CompilerParams(dimension_semantics=("arbitrary",) * 3, vmem_limit_bytes=VMEM_LIMIT_BYTES),
        name="moe_scatter",
    )(slot, aff, y, x, gate)


def moe_block(x, g, shift, scale, gate, w_router, w_gate, w_up, w_down):
    h, aff, slot = moe_route(x, g, shift, scale, w_router)
    y = moe_ffn(moe_gather(h, slot), w_gate, w_up, w_down)
    return moe_scatter(slot, aff, y, x, gate)


def rms_norm(x, g):
    xf = x.astype(jnp.float32)
    y = xf * lax.rsqrt(jnp.mean(xf * xf, axis=-1, keepdims=True) + EPS)
    return (y * g.astype(jnp.float32)).astype(x.dtype)


def ada_params(cond, w, b):
    m = jax.nn.silu(cond) @ w + b
    return jnp.split(m[:, None, :], 6, axis=-1)


def modulate(x, g, shift, scale):
    return rms_norm(x, g) * (1 + scale) + shift


def axial_rope(T):
    t = jnp.arange(T)
    n_freq = HEAD_DIM // 4
    inv = ROPE_BASE ** (-jnp.arange(n_freq, dtype=jnp.float32) / n_freq)
    ang = jnp.concatenate([(t // GRID_W).astype(jnp.float32)[:, None] * inv,
                           (t % GRID_W).astype(jnp.float32)[:, None] * inv], axis=-1)
    return jnp.cos(ang)[:, None, :], jnp.sin(ang)[:, None, :]


def apply_rope(x, cos, sin):
    xf = x.astype(jnp.float32)
    x1, x2 = jnp.split(xf, 2, axis=-1)
    return jnp.concatenate([x1 * cos - x2 * sin, x2 * cos + x1 * sin], axis=-1).astype(x.dtype)


def softmax_parts(parts, sink=None):
    sizes = [p.shape[-1] for p in parts]
    cols = list(parts)
    if sink is not None:
        cols.append(jnp.broadcast_to(sink, parts[0].shape[:-1] + (1,)))
    p = jax.nn.softmax(jnp.concatenate(cols, axis=-1), axis=-1)
    pieces = jnp.split(p, np.cumsum(sizes).tolist(), axis=-1)
    return pieces[:len(sizes)]


def context_attention(q, k, v, sink):
    B, S, HQ, hd = q.shape
    HK = k.shape[2]
    G = HQ // HK
    nb = S // Q_BLOCK
    scale = hd ** -0.5
    sink_b = None if sink is None else sink.astype(jnp.float32).reshape(1, HK, G, 1, 1)
    qb = q.reshape(B, nb, Q_BLOCK, HK, G, hd).swapaxes(0, 1)

    def one(qi):
        s = jnp.einsum('bqkgd,bskd->bkgqs', qi, k, preferred_element_type=jnp.float32) * scale
        (p,) = softmax_parts([s], sink_b)
        return jnp.einsum('bkgqs,bskd->bqkgd', p.astype(v.dtype), v)

    o = lax.map(one, qb)
    return o.swapaxes(0, 1).reshape(B, S, HQ, hd)


def window_attention(q, k, v, ck, cv, sink):
    B, T, HQ, hd = q.shape
    HK = k.shape[2]
    G = HQ // HK
    nb = T // A_BLOCK
    scale = hd ** -0.5
    qb = q.reshape(B, nb, A_BLOCK, HK, G, hd).swapaxes(0, 1)

    def band(x):
        xp = jnp.pad(x, ((0, 0), (A_BLOCK, A_BLOCK), (0, 0), (0, 0))).reshape(B, nb + 2, A_BLOCK, HK, hd)
        return jnp.concatenate([xp[:, :-2], xp[:, 1:-1], xp[:, 2:]], axis=2).swapaxes(0, 1)

    kb, vb = band(k), band(v)
    qpos = jnp.arange(nb)[:, None, None] * A_BLOCK + jnp.arange(A_BLOCK)[None, :, None]
    kpos = jnp.arange(nb)[:, None, None] * A_BLOCK - A_BLOCK + jnp.arange(3 * A_BLOCK)[None, None, :]
    mask = (jnp.abs(kpos - qpos) <= A_WINDOW) & (kpos >= 0) & (kpos < T)
    sink_b = sink.astype(jnp.float32).reshape(1, HK, G, 1, 1)

    def one(xs):
        qi, ki, vi, mi = xs
        s_loc = jnp.einsum('bqkgd,bskd->bkgqs', qi, ki, preferred_element_type=jnp.float32) * scale
        s_loc = jnp.where(mi[None, None, None], s_loc, NEG_INF)
        s_ctx = jnp.einsum('bqkgd,bpkd->bkgqp', qi, ck, preferred_element_type=jnp.float32) * scale
        p_loc, p_ctx = softmax_parts([s_loc, s_ctx], sink_b)
        return (jnp.einsum('bkgqs,bskd->bqkgd', p_loc.astype(vi.dtype), vi)
                + jnp.einsum('bkgqp,bpkd->bqkgd', p_ctx.astype(cv.dtype), cv))

    o = lax.map(one, (qb, kb, vb, mask))
    return o.swapaxes(0, 1).reshape(B, T, HQ, hd)


def short_conv(x, w):
    K = w.shape[0]
    L = x.shape[1]
    pad = K // 2
    xp = jnp.pad(x, ((0, 0), (pad, pad), (0, 0)))
    return sum(xp[:, i:i + L] * w[i] for i in range(K))


def hyena_filter_bank(L, w1, b1, w2, b2, w3, freq):
    f32 = jnp.float32
    t = jnp.linspace(0.0, 1.0, L, dtype=f32)[:, None]
    bands = (HY_EMB - 1) // 2
    omega = 2.0 * math.pi * jnp.arange(L, dtype=f32)[:, None] / L
    fb = jnp.linspace(1e-4, bands - 1, bands, dtype=f32)[None, :]
    feats = jnp.concatenate([t, jnp.cos(fb * omega), -jnp.sin(fb * omega)], axis=-1)
    fr = freq.astype(f32)
    h = jnp.sin(fr * (feats @ w1.astype(f32) + b1.astype(f32)))
    h = jnp.sin(fr * (h @ w2.astype(f32) + b2.astype(f32)))
    h = (h @ w3.astype(f32)).reshape(L, 2, HY_ORDER, HY_CH)
    max_decay = math.log(HY_DECAY_TARGET) / HY_FAST_DECAY
    min_decay = math.log(HY_DECAY_TARGET) / HY_SLOW_DECAY
    deltas = jnp.abs(jnp.linspace(min_decay, max_decay, HY_CH, dtype=f32))
    h = h * (jnp.exp(-t * deltas) + HY_SHIFT)[:, None, None, :]
    taps = jnp.concatenate([h[:, 0], jnp.zeros((1, HY_ORDER, HY_CH), f32), h[:0:-1, 1]], axis=0)
    taps = taps / jnp.sum(jnp.abs(taps), axis=0, keepdims=True)
    return jnp.fft.rfft(taps, axis=0)


def hyena_mixer(z, conv_w, conv_b, w1, b1, w2, b2, w3, freq, skip):
    L = z.shape[1]
    zf = (short_conv(z, conv_w) + conv_b).astype(jnp.float32)
    v, x1, x2 = jnp.split(zf, 3, axis=-1)
    filt = hyena_filter_bank(L, w1, b1, w2, b2, w3, freq)
    skip = skip.astype(jnp.float32)

    def long_conv(u, o):
        y = jnp.fft.irfft(jnp.fft.rfft(u, n=2 * L, axis=1) * filt[None, :, o], n=2 * L, axis=1)[:, :L]
        return y + u * skip[o]

    y = x1 * long_conv(v, 0)
    y = x2 * long_conv(y, 1)
    return y.astype(z.dtype)


def l2norm(x):
    xf = x.astype(jnp.float32)
    return xf * lax.rsqrt(jnp.sum(xf * xf, axis=-1, keepdims=True) + EPS)


def chunk_gated_delta(q, k, v, g, beta, s0):
    B, L, H, dk = q.shape
    dv = v.shape[-1]
    n = L // C_CHUNK

    def chunks(x):
        x = x.reshape((B, n, C_CHUNK, H) + x.shape[3:])
        return jnp.moveaxis(jnp.moveaxis(x, 1, 0), 3, 2)

    qc, kc, vc, bc = chunks(q), chunks(k), chunks(v), chunks(beta)
    gc = jnp.cumsum(chunks(g), axis=-1)
    tri = jnp.tril(jnp.ones((C_CHUNK, C_CHUNK), bool))
    strict = jnp.tril(jnp.ones((C_CHUNK, C_CHUNK), bool), k=-1)
    gamma = jnp.exp(jnp.where(tri, gc[..., :, None] - gc[..., None, :], NEG_INF))
    kb = kc * bc[..., None]
    a_mat = jnp.where(strict, jnp.einsum('nbhid,nbhjd->nbhij', kb, kc) * gamma, 0.0) + jnp.eye(C_CHUNK, dtype=jnp.float32)
    rhs = jnp.concatenate([vc * bc[..., None], kb * jnp.exp(gc)[..., None]], axis=-1)
    sol = lax.linalg.triangular_solve(a_mat, rhs, left_side=True, lower=True)
    u, w = sol[..., :dv], sol[..., dv:]
    attn = jnp.where(tri, jnp.einsum('nbhid,nbhjd->nbhij', qc, kc) * gamma, 0.0)
    g_last = gc[..., -1]
    q_dec = qc * jnp.exp(gc)[..., None]
    k_dec = kc * jnp.exp(g_last[..., None] - gc)[..., None]

    def step(S, xs):
        u_i, w_i, a_i, qd, kd, gl = xs
        v_new = u_i - jnp.einsum('bhck,bhkv->bhcv', w_i, S)
        o = jnp.einsum('bhck,bhkv->bhcv', qd, S) + jnp.einsum('bhij,bhjv->bhiv', a_i, v_new)
        S = S * jnp.exp(gl)[..., None, None] + jnp.einsum('bhck,bhcv->bhkv', kd, v_new)
        return S, o

    S, o = lax.scan(step, s0.astype(jnp.float32), (u, w, attn, q_dec, k_dec, g_last))
    o = jnp.moveaxis(jnp.moveaxis(o, 2, 3), 0, 1).reshape(B, L, H, dv)
    return o, S


def deltanet_mixer(zq, zk, zv, zg, za, zb, conv_w, a_log, dt_bias, norm_w, s0):
    B, L, _ = zq.shape
    qkv = jax.nn.silu(short_conv(jnp.concatenate([zq, zk, zv], axis=-1), conv_w))
    q, k, v = [t.reshape(B, L, C_HEADS, HEAD_DIM) for t in jnp.split(qkv, 3, axis=-1)]
    q = l2norm(q) * (HEAD_DIM ** -0.5)
    k = l2norm(k)
    v = v.astype(jnp.float32)
    beta = jax.nn.sigmoid(zb.astype(jnp.float32))
    g = -jnp.exp(a_log.astype(jnp.float32)) * jax.nn.softplus(za.astype(jnp.float32) + dt_bias.astype(jnp.float32))
    o_f, s_f = chunk_gated_delta(q, k, v, g[:, :, 0], beta[:, :, 0], s0[:, 0])
    o_b, s_b = chunk_gated_delta(q[:, ::-1], k[:, ::-1], v[:, ::-1], g[:, ::-1, 1], beta[:, ::-1, 1], s0[:, 1])
    o = o_f + o_b[:, ::-1]
    gate = jax.nn.silu(zg.reshape(B, L, C_HEADS, HEAD_DIM).astype(jnp.float32))
    o = rms_norm(o, norm_w) * gate
    return o.reshape(B, L, C_W).astype(zq.dtype), jnp.stack([s_f, s_b], axis=1)


def neighbourhood_attention(q, k, v, ck, cv, rpb):
    B, T, H, hd = q.shape
    rows = T // GRID_W
    kh = min(NA_KH_MAX, rows)
    scale = hd ** -0.5
    r = jnp.arange(rows)
    rs = jnp.clip(r - kh // 2, 0, rows - kh)
    key_rows = rs[:, None] + jnp.arange(kh)[None, :]
    idx = (key_rows[:, :, None] * GRID_W + jnp.arange(GRID_W)).reshape(rows, kh * GRID_W)
    col = jnp.arange(GRID_W)
    cs = jnp.clip(col - NA_KW // 2, 0, GRID_W - NA_KW)
    kcol = jnp.tile(col, kh)
    col_ok = (kcol[None, :] >= cs[:, None]) & (kcol[None, :] < cs[:, None] + NA_KW)
    roff = jnp.repeat(key_rows - r[:, None], GRID_W, axis=1) + NA_KH_MAX - 1
    coff = jnp.clip(kcol[None, :] - col[:, None] + NA_KW - 1, 0, 2 * NA_KW - 2)
    qr = q.reshape(B, rows, GRID_W, H, hd).swapaxes(0, 1)
    rpb_f = rpb.astype(jnp.float32)

    def one(xs):
        qi, ii, ro = xs
        ki = k[:, ii]
        vi = v[:, ii]
        bias = rpb_f[:, ro[None, :], coff]
        s_loc = jnp.einsum('bqhd,bkhd->bhqk', qi, ki, preferred_element_type=jnp.float32) * scale + bias[None]
        s_loc = jnp.where(col_ok[None, None], s_loc, NEG_INF)
        s_ctx = jnp.einsum('bqhd,bphd->bhqp', qi, ck, preferred_element_type=jnp.float32) * scale
        p_loc, p_ctx = softmax_parts([s_loc, s_ctx])
        return (jnp.einsum('bhqk,bkhd->bqhd', p_loc.astype(vi.dtype), vi)
                + jnp.einsum('bhqp,bphd->bqhd', p_ctx.astype(cv.dtype), cv))

    o = lax.map(one, (qr, idx, roff))
    return o.swapaxes(0, 1).reshape(B, T, H, hd)


def expert_choice_ffn(h, w_router, w_gate, w_up, w_down):
    B, T, D = h.shape
    cap = EC_CAPACITY * T // N_EXPERTS
    aff = jax.nn.softmax(jnp.einsum('btd,de->bte', h, w_router, preferred_element_type=jnp.float32), axis=-1)
    gate, idx = lax.top_k(aff.swapaxes(1, 2), cap)
    xg = jax.vmap(lambda hb, ib: hb[ib])(h, idx)
    a = jnp.einsum('becd,edf->becf', xg, w_gate)
    u = jnp.einsum('becd,edf->becf', xg, w_up)
    y = jnp.einsum('becf,efd->becd', jax.nn.silu(a) * u, w_down) * gate[..., None].astype(h.dtype)
    return jax.vmap(lambda yb, ib: jnp.zeros((T, D), yb.dtype).at[ib.reshape(-1)].add(yb.reshape(-1, D)))(y, idx)


def split_even(z):
    B, L = z.shape[:2]
    q = z[..., :A_Q_W].reshape(B, L, A_HEADS, HEAD_DIM)
    k = z[..., A_Q_W:A_Q_W + A_KV_W].reshape(B, L, A_KV_HEADS, HEAD_DIM)
    v = z[..., A_Q_W + A_KV_W:A_Q_W + 2 * A_KV_W].reshape(B, L, A_KV_HEADS, HEAD_DIM)
    return q, k, v, z[..., A_Q_W + 2 * A_KV_W:]


def split_odd(z):
    B, L = z.shape[:2]
    zq, zk, zv, zg = [z[..., i * C_W:(i + 1) * C_W] for i in range(4)]
    off = 4 * C_W
    za = z[..., off:off + 2 * C_HEADS].reshape(B, L, 2, C_HEADS)
    zb = z[..., off + 2 * C_HEADS:off + 4 * C_HEADS].reshape(B, L, 2, C_HEADS)
    off = off + 4 * C_HEADS
    nq, nk, nv = [z[..., off + i * D_W:off + (i + 1) * D_W].reshape(B, L, D_HEADS, HEAD_DIM) for i in range(3)]
    return zq, zk, zv, zg, za, zb, nq, nk, nv


def kernel(x_prompt, x_sample, cache_attn_k, cache_attn_v, state_delta, cache_na_k, cache_na_v,
           c, c_ctx, w_ada, b_ada, norm_mix, norm_ffn, norm_final,
           even_w_in, even_w_out, attn_sink, hy_conv_w, hy_conv_b, hy_w1, hy_b1, hy_w2, hy_b2,
           hy_w3, hy_freq, hy_skip, odd_w_in, odd_w_out, gdn_conv_w, gdn_a_log, gdn_dt_bias,
           gdn_norm, na_rpb, moe_router, moe_w_gate, moe_w_up, moe_w_down):
    xp, xs = x_prompt, x_sample
    bp = xp.shape[0]
    cos, sin = axial_rope(xs.shape[1])
    new_ak, new_av, new_st, new_nk, new_nv = [], [], [], [], []
    for l in range(DEPTH):
        j = l // 2
        mp = ada_params(c_ctx[None, :], w_ada[l], b_ada[l])
        ms = ada_params(c, w_ada[l], b_ada[l])
        hp = modulate(xp, norm_mix[l], mp[0], mp[1])
        hs = modulate(xs, norm_mix[l], ms[0], ms[1])
        if l % 2 == 0:
            hy = (hy_conv_w[j], hy_conv_b[j], hy_w1[j], hy_b1[j], hy_w2[j], hy_b2[j], hy_w3[j], hy_freq[j], hy_skip[j])
            w_in = even_w_in[j]
            w_groups = [w_in[:, :A_Q_W], w_in[:, A_Q_W:A_Q_W + A_KV_W], w_in[:, A_Q_W + A_KV_W:A_Q_W + 2 * A_KV_W],
                        w_in[:, A_Q_W + 2 * A_KV_W:]]
            q, k, v, zh = proj_multi(hp, w_groups, [F32] * 4)
            oa = context_attention_pallas(q, k, v, attn_sink[j], A_HEADS, A_KV_HEADS)
            mix_p = proj_concat(oa, hyena_mixer(zh, *hy), even_w_out[j])
            new_ak.append(k.reshape(bp, SEQ, A_KV_HEADS, HEAD_DIM))
            new_av.append(v.reshape(bp, SEQ, A_KV_HEADS, HEAD_DIM))
            q, k, v, zh = proj_multi(hs, w_groups, [F32] * 4)
            ck = cache_attn_k[:, j].reshape(DEC_BATCH, PAST_LEN, A_KV_W).astype(BF16)
            cv = cache_attn_v[:, j].reshape(DEC_BATCH, PAST_LEN, A_KV_W).astype(BF16)
            oa = window_attention_pallas(q, k, v, ck, cv, attn_sink[j])
            mix_s = proj_concat(oa, hyena_mixer(zh, *hy), even_w_out[j])
        else:
            w_in = odd_w_in[j]
            ab0 = 4 * C_W
            ab_cols = [w_in[:, ab0 + o * C_HEADS:ab0 + (o + 1) * C_HEADS] for o in (0, 2, 1, 3)]
            w_ab = jnp.concatenate(ab_cols + [jnp.zeros((D_MODEL, LANES - 4 * C_HEADS), w_in.dtype)], axis=1)
            n0 = ab0 + 4 * C_HEADS
            w_groups = [w_in[:, :3 * C_W], w_in[:, 3 * C_W:4 * C_W], w_ab,
                        w_in[:, n0:n0 + D_W], w_in[:, n0 + D_W:n0 + 2 * D_W], w_in[:, n0 + 2 * D_W:]]

            def deltanet(zqkv, zab, zg, s0):
                qd, kd, vd, gb = gdn_prep(zqkv, zab, gdn_conv_w[j], gdn_a_log[j], gdn_dt_bias[j])
                return gdn_scan(qd, kd, vd, gb, zg, s0, gdn_norm[j])

            zqkv, zg, zab, nq, nk, nv = proj_multi(hp, w_groups, [F32] * 6)
            oc, st = deltanet(zqkv, zab, zg, jnp.zeros((bp, 2, C_HEADS, HEAD_DIM, HEAD_DIM), F32))
            od = context_attention_pallas(nq, nk, nv, None, D_HEADS, D_HEADS)
            mix_p = proj_concat(oc, od, odd_w_out[j])
            new_st.append(st)
            new_nk.append(nk.reshape(bp, SEQ, D_HEADS, HEAD_DIM))
            new_nv.append(nv.reshape(bp, SEQ, D_HEADS, HEAD_DIM))
            zqkv, zg, zab, nq, nk, nv = proj_multi(hs, w_groups, [F32, F32, F32, BF16, BF16, BF16])
            oc, _ = deltanet(zqkv, zab, zg, state_delta[:, j])
            ck = cache_na_k[:, j].reshape(DEC_BATCH, PAST_LEN, D_W).astype(BF16)
            cv = cache_na_v[:, j].reshape(DEC_BATCH, PAST_LEN, D_W).astype(BF16)
            od = neighbourhood_attention_pallas(nq, nk, nv, ck, cv, na_rpb[j])
            mix_s = proj_concat(oc, od, odd_w_out[j])
        xp = xp + mp[2] * mix_p
        xs = xs + ms[2] * mix_s
        moe = (moe_router[l], moe_w_gate[l], moe_w_up[l], moe_w_down[l])
        xp = moe_block(xp, norm_ffn[l], mp[3], mp[4], mp[5], *moe)
        xs = moe_block(xs, norm_ffn[l], ms[3], ms[4], ms[5], *moe)
    y_prompt = rms_norm(xp, norm_final)
    y_sample = rms_norm(xs, norm_final)
    return (y_prompt, y_sample, jnp.stack(new_ak, axis=1), jnp.stack(new_av, axis=1), jnp.stack(new_st, axis=1),
            jnp.stack(new_nk, axis=1), jnp.stack(new_nv, axis=1))
```

```python
import functools
import math
import jax, jax.numpy as jnp
from jax import lax
import numpy as np
from jax.experimental import pallas as pl
from jax.experimental.pallas import tpu as pltpu

D_MODEL = 1024
BATCH = 32
SEQ = 256
DEPTH = 4
DEC_BATCH = 4
DEC_SEQ = 4096
PAST_LEN = 512

GRID_W = 64
HEAD_DIM = 64
N_EVEN = (DEPTH + 1) // 2
N_ODD = DEPTH // 2
Q_BLOCK = 128
A_HEADS = D_MODEL // 128
A_KV_HEADS = A_HEADS // 4
A_WINDOW = 128
A_BLOCK = 128
ROPE_BASE = 10000.0
HY_CH = D_MODEL // 2
HY_ORDER = 2
HY_SHORT = 3
HY_EMB = 33
HY_FILT_W = 64
HY_FAST_DECAY = 0.3
HY_SLOW_DECAY = 1.5
HY_DECAY_TARGET = 1e-2
HY_SHIFT = 0.05
C_HEADS = D_MODEL // 128
C_SHORT = 3
C_CHUNK = 64
D_HEADS = D_MODEL // 128
NA_KH_MAX = 8
NA_KW = 16
N_EXPERTS = 16
EC_CAPACITY = 2
MOE_D_FF = D_MODEL
EPS = 1e-6
NEG_INF = -1e30

A_Q_W = A_HEADS * HEAD_DIM
A_KV_W = A_KV_HEADS * HEAD_DIM
EVEN_IN = A_Q_W + 2 * A_KV_W + 3 * HY_CH
EVEN_MIX = A_Q_W + HY_CH
C_W = C_HEADS * HEAD_DIM
D_W = D_HEADS * HEAD_DIM
ODD_IN = 4 * C_W + 4 * C_HEADS + 3 * D_W
ODD_MIX = C_W + D_W

VMEM_LIMIT_BYTES = 48 * 1024 * 1024


def _mm_kernel(x_ref, w_ref, o_ref):
    o_ref[...] = jnp.dot(x_ref[...].astype(jnp.bfloat16), w_ref[...], preferred_element_type=jnp.float32)


def pallas_matmul(x, w, tm=256):
    M, K = x.shape
    N = w.shape[1]
    assert M % tm == 0
    return pl.pallas_call(
        _mm_kernel,
        grid=(M // tm,),
        in_specs=[pl.BlockSpec((tm, K), lambda i: (i, 0)), pl.BlockSpec((K, N), lambda i: (0, 0))],
        out_specs=pl.BlockSpec((tm, N), lambda i: (i, 0)),
        out_shape=jax.ShapeDtypeStruct((M, N), jnp.float32),
        compiler_params=pltpu.CompilerParams(dimension_semantics=("arbitrary",), vmem_limit_bytes=VMEM_LIMIT_BYTES),
    )(x, w.astype(jnp.bfloat16))


def proj(x, w):
    B, L, K = x.shape
    return pallas_matmul(x.reshape(B * L, K), w).reshape(B, L, w.shape[1])


def _request_of_tile(tm, L, per_request):
    return (lambda i: ((i * tm) // L, 0, 0)) if per_request else (lambda i: (0, 0, 0))


def _mm_multi_kernel(x_ref, g_ref, shift_ref, scale_ref, *refs):
    n = len(refs) // 2
    x = x_ref[...]
    y = x * lax.rsqrt(jnp.mean(x * x, axis=-1, keepdims=True) + EPS) * g_ref[...]
    h = (y * (1.0 + scale_ref[0]) + shift_ref[0]).astype(jnp.bfloat16)
    for w_ref, o_ref in zip(refs[:n], refs[n:]):
        o_ref[...] = jnp.dot(h, w_ref[...], preferred_element_type=jnp.float32).astype(o_ref.dtype)


def proj_multi(x, g, shift, scale, weights, out_dtypes, tm=256):
    B, L, K = x.shape
    M = B * L
    assert L % tm == 0
    mod_spec = pl.BlockSpec((1, 1, K), _request_of_tile(tm, L, shift.shape[0] == B))
    outs = pl.pallas_call(
        _mm_multi_kernel,
        grid=(M // tm,),
        in_specs=[pl.BlockSpec((tm, K), lambda i: (i, 0)), pl.BlockSpec((1, K), lambda i: (0, 0)), mod_spec, mod_spec]
        + [pl.BlockSpec(w.shape, lambda i: (0, 0)) for w in weights],
        out_specs=[pl.BlockSpec((tm, w.shape[1]), lambda i: (i, 0)) for w in weights],
        out_shape=[jax.ShapeDtypeStruct((M, w.shape[1]), dt) for w, dt in zip(weights, out_dtypes)],
        compiler_params=pltpu.CompilerParams(dimension_semantics=("arbitrary",), vmem_limit_bytes=VMEM_LIMIT_BYTES),
        name="in_projection",
    )(x.reshape(M, K), g.astype(jnp.float32)[None, :], shift, scale, *[w.astype(jnp.bfloat16) for w in weights])
    return [o.reshape(B, L, o.shape[1]) for o in outs]


def _mm2_kernel(a_ref, b_ref, wa_ref, wb_ref, x_ref, gate_ref, o_ref):
    mix = (jnp.dot(a_ref[...].astype(jnp.bfloat16), wa_ref[...], preferred_element_type=jnp.float32)
           + jnp.dot(b_ref[...].astype(jnp.bfloat16), wb_ref[...], preferred_element_type=jnp.float32))
    o_ref[...] = x_ref[...] + gate_ref[0] * mix


def proj_concat(a, b, w, x, gate, tm=256):
    B, L, Ka = a.shape
    Kb = b.shape[2]
    N = w.shape[1]
    M = B * L
    assert L % tm == 0 and w.shape[0] == Ka + Kb
    wb16 = w.astype(jnp.bfloat16)
    out = pl.pallas_call(
        _mm2_kernel,
        grid=(M // tm,),
        in_specs=[pl.BlockSpec((tm, Ka), lambda i: (i, 0)), pl.BlockSpec((tm, Kb), lambda i: (i, 0)),
                  pl.BlockSpec((Ka, N), lambda i: (0, 0)), pl.BlockSpec((Kb, N), lambda i: (0, 0)),
                  pl.BlockSpec((tm, N), lambda i: (i, 0)),
                  pl.BlockSpec((1, 1, N), _request_of_tile(tm, L, gate.shape[0] == B))],
        out_specs=pl.BlockSpec((tm, N), lambda i: (i, 0)),
        out_shape=jax.ShapeDtypeStruct((M, N), jnp.float32),
        compiler_params=pltpu.CompilerParams(dimension_semantics=("arbitrary",), vmem_limit_bytes=VMEM_LIMIT_BYTES),
        name="out_projection",
    )(a.reshape(M, Ka), b.reshape(M, Kb), wb16[:Ka], wb16[Ka:], x.reshape(M, N), gate)
    return out.reshape(B, L, N)


LANES = 128
BF16 = jnp.bfloat16
F32 = jnp.float32


def _dot_nt(a, b):
    return lax.dot_general(a, b, (((1,), (1,)), ((), ())), preferred_element_type=F32)


def _dot(a, b):
    return jnp.dot(a, b, preferred_element_type=F32)


def _low_half(shape):
    return lax.broadcasted_iota(jnp.int32, shape, 1) < HEAD_DIM


def _softmax_pv(scores, values, extra_logit=None):
    m = scores[0].max(axis=-1, keepdims=True)
    for s in scores[1:]:
        m = jnp.maximum(m, s.max(axis=-1, keepdims=True))
    if extra_logit is not None:
        m = jnp.maximum(m, extra_logit)
    l = None if extra_logit is None else jnp.exp(extra_logit - m)
    o = None
    for s, v in zip(scores, values):
        p = jnp.exp(s - m)
        ps = p.sum(axis=-1, keepdims=True)
        l = ps if l is None else l + ps
        pv = _dot(p.astype(BF16), v)
        o = pv if o is None else o + pv
    return o / l


def _place_head(q_slab, src_half, dst_half, low):
    x = q_slab if src_half == dst_half else pltpu.roll(q_slab, HEAD_DIM, axis=1)
    return jnp.where(low if dst_half == 0 else ~low, x, jnp.zeros_like(x))


NA_KEYS = NA_KH_MAX * GRID_W


def _na_kernel(q_ref, k_ref, v_ref, ck_ref, cv_ref, tab_ref, o_ref):
    r = pl.program_id(1)
    rows = k_ref.shape[1] // GRID_W
    rs = jnp.clip(r - NA_KH_MAX // 2, 0, rows - NA_KH_MAX)
    start = pl.multiple_of(rs * GRID_W, GRID_W)
    scale = HEAD_DIM ** -0.5
    low = _low_half((GRID_W, LANES))
    for p in range(D_HEADS // 2):
        cols = slice(p * LANES, (p + 1) * LANES)
        qp = q_ref[0, :, cols]
        kp = k_ref[0, pl.ds(start, NA_KEYS), cols]
        vp = v_ref[0, pl.ds(start, NA_KEYS), cols]
        ckp = ck_ref[0, :, cols]
        cvp = cv_ref[0, :, cols]
        outs = []
        for half in range(2):
            qm = jnp.where(low if half == 0 else ~low, qp, jnp.zeros_like(qp))
            s_loc = _dot_nt(qm, kp) * scale + tab_ref[2 * p + half, 0]
            s_ctx = _dot_nt(qm, ckp) * scale
            outs.append(_softmax_pv([s_loc, s_ctx], [vp, cvp]))
        o_ref[0, :, cols] = jnp.where(low, outs[0], outs[1]).astype(o_ref.dtype)


def na_bias_table(rpb):
    col = jnp.arange(GRID_W)
    cs = jnp.clip(col - NA_KW // 2, 0, GRID_W - NA_KW)
    col_ok = (col[None, :] >= cs[:, None]) & (col[None, :] < cs[:, None] + NA_KW)
    coff = jnp.clip(col[None, :] - col[:, None] + NA_KW - 1, 0, 2 * NA_KW - 2)
    base = jnp.where(col_ok[None, None], rpb.astype(F32)[:, :, coff], NEG_INF)
    tab = jnp.stack([base[:, o:o + NA_KH_MAX] for o in range(NA_KH_MAX)], axis=1)
    return tab.transpose(0, 1, 3, 2, 4).reshape(D_HEADS, NA_KH_MAX, GRID_W, NA_KEYS)


def neighbourhood_attention_pallas(q, k, v, ck, cv, rpb):
    B, T, W = q.shape
    P = ck.shape[1]
    rows = T // GRID_W
    assert rows >= NA_KH_MAX and W == D_W
    tab = na_bias_table(rpb)

    def tab_index(b, r):
        rs = jnp.clip(r - NA_KH_MAX // 2, 0, rows - NA_KH_MAX)
        return (0, rs - r + NA_KH_MAX - 1, 0, 0)

    return pl.pallas_call(
        _na_kernel,
        grid=(B, rows),
        in_specs=[
            pl.BlockSpec((1, GRID_W, W), lambda b, r: (b, r, 0)),
            pl.BlockSpec((1, T, W), lambda b, r: (b, 0, 0)),
            pl.BlockSpec((1, T, W), lambda b, r: (b, 0, 0)),
            pl.BlockSpec((1, P, W), lambda b, r: (b, 0, 0)),
            pl.BlockSpec((1, P, W), lambda b, r: (b, 0, 0)),
            pl.BlockSpec((D_HEADS, 1, GRID_W, NA_KEYS), tab_index),
        ],
        out_specs=pl.BlockSpec((1, GRID_W, W), lambda b, r: (b, r, 0)),
        out_shape=jax.ShapeDtypeStruct((B, T, W), BF16),
        compiler_params=pltpu.CompilerParams(dimension_semantics=("arbitrary", "arbitrary"),
                                             vmem_limit_bytes=VMEM_LIMIT_BYTES),
        name="na_attention",
    )(q, k, v, ck, cv, tab)


def rope_tables(T):
    cos, sin = axial_rope(T)
    cos, sin = cos[:, 0, :], sin[:, 0, :]
    cos_t = jnp.concatenate([cos, cos, cos, cos], axis=-1)
    sin_t = jnp.concatenate([-sin, sin, -sin, sin], axis=-1)
    return cos_t, sin_t


def _rope(x, cos_t, sin_t):
    half = HEAD_DIM // 2
    lane = lax.broadcasted_iota(jnp.int32, x.shape, 1)
    first = (lane % HEAD_DIM) < half
    swapped = jnp.where(first, pltpu.roll(x, LANES - half, axis=1), pltpu.roll(x, half, axis=1))
    return x * cos_t + swapped * sin_t


def _win_kernel(sink_ref, q_ref, k_ref, v_ref, ck_ref, cv_ref, cos_ref, sin_ref, o_ref):
    i = pl.program_id(1)
    T = k_ref.shape[1]
    span = 3 * A_BLOCK
    start = pl.multiple_of(jnp.clip((i - 1) * A_BLOCK, 0, T - span), A_BLOCK)
    delta = i * A_BLOCK - start
    q0 = pl.multiple_of(i * A_BLOCK, A_BLOCK)
    scale = HEAD_DIM ** -0.5
    kw = _rope(k_ref[0, pl.ds(start, span), :], cos_ref[pl.ds(start, span), :], sin_ref[pl.ds(start, span), :]).astype(BF16)
    vw = v_ref[0, pl.ds(start, span), :].astype(BF16)
    ck = ck_ref[0]
    cv = cv_ref[0]
    cos_q = cos_ref[pl.ds(q0, A_BLOCK), :]
    sin_q = sin_ref[pl.ds(q0, A_BLOCK), :]
    qi = lax.broadcasted_iota(jnp.int32, (A_BLOCK, span), 0)
    kj = lax.broadcasted_iota(jnp.int32, (A_BLOCK, span), 1)
    band = jnp.abs(kj - delta - qi) <= A_WINDOW
    low = _low_half((A_BLOCK, LANES))
    group = A_HEADS // A_KV_HEADS
    for p in range(A_HEADS // 2):
        cols = slice(p * LANES, (p + 1) * LANES)
        q_slab = _rope(q_ref[0, :, cols], cos_q, sin_q)
        outs = []
        for half in range(2):
            h = 2 * p + half
            kv = h // group
            qm = _place_head(q_slab, half, kv, low).astype(BF16)
            s_loc = jnp.where(band, _dot_nt(qm, kw) * scale, NEG_INF)
            s_ctx = _dot_nt(qm, ck) * scale
            o = _softmax_pv([s_loc, s_ctx], [vw, cv], extra_logit=sink_ref[h])
            outs.append(o if kv == half else pltpu.roll(o, HEAD_DIM, axis=1))
        o_ref[0, :, cols] = jnp.where(low, outs[0], outs[1]).astype(o_ref.dtype)


def window_attention_pallas(q, k, v, ck, cv, sink):
    B, T, QW = q.shape
    KW = k.shape[2]
    P = ck.shape[1]
    assert KW == LANES and QW == A_Q_W and T % A_BLOCK == 0 and T >= 3 * A_BLOCK
    cos_t, sin_t = rope_tables(T)
    return pl.pallas_call(
        _win_kernel,
        grid=(B, T // A_BLOCK),
        in_specs=[
            pl.BlockSpec(memory_space=pltpu.SMEM),
            pl.BlockSpec((1, A_BLOCK, QW), lambda b, i: (b, i, 0)),
            pl.BlockSpec((1, T, KW), lambda b, i: (b, 0, 0)),
            pl.BlockSpec((1, T, KW), lambda b, i: (b, 0, 0)),
            pl.BlockSpec((1, P, KW), lambda b, i: (b, 0, 0)),
            pl.BlockSpec((1, P, KW), lambda b, i: (b, 0, 0)),
            pl.BlockSpec((T, LANES), lambda b, i: (0, 0)),
            pl.BlockSpec((T, LANES), lambda b, i: (0, 0)),
        ],
        out_specs=pl.BlockSpec((1, A_BLOCK, QW), lambda b, i: (b, i, 0)),
        out_shape=jax.ShapeDtypeStruct((B, T, QW), BF16),
        compiler_params=pltpu.CompilerParams(dimension_semantics=("arbitrary", "arbitrary"),
                                             vmem_limit_bytes=VMEM_LIMIT_BYTES),
        name="window_attention",
    )(sink.astype(F32), q, k, v, ck, cv, cos_t, sin_t)


def _ctx_kernel(sink_ref, q_ref, k_ref, v_ref, o_ref, *, n_q_heads, n_kv_heads, use_sink):
    S = q_ref.shape[1]
    scale = HEAD_DIM ** -0.5
    low = _low_half((S, LANES))
    group = n_q_heads // n_kv_heads
    for p in range(n_q_heads // 2):
        cols = slice(p * LANES, (p + 1) * LANES)
        q_slab = q_ref[0, :, cols]
        outs = []
        for half in range(2):
            h = 2 * p + half
            kv = h // group
            kcols = slice((kv // 2) * LANES, (kv // 2 + 1) * LANES)
            qm = _place_head(q_slab, half, kv % 2, low).astype(BF16)
            s = _dot_nt(qm, k_ref[0, :, kcols].astype(BF16)) * scale
            o = _softmax_pv([s], [v_ref[0, :, kcols].astype(BF16)], extra_logit=sink_ref[h] if use_sink else None)
            outs.append(o if kv % 2 == half else pltpu.roll(o, HEAD_DIM, axis=1))
        o_ref[0, :, cols] = jnp.where(low, outs[0], outs[1]).astype(o_ref.dtype)


def context_attention_pallas(q, k, v, sink, n_q_heads, n_kv_heads):
    B, S, QW = q.shape
    KW = k.shape[2]
    use_sink = sink is not None
    sink_arr = sink.astype(F32) if use_sink else jnp.zeros((n_q_heads,), F32)
    return pl.pallas_call(
        functools.partial(_ctx_kernel, n_q_heads=n_q_heads, n_kv_heads=n_kv_heads, use_sink=use_sink),
        grid=(B,),
        in_specs=[
            pl.BlockSpec(memory_space=pltpu.SMEM),
            pl.BlockSpec((1, S, QW), lambda b: (b, 0, 0)),
            pl.BlockSpec((1, S, KW), lambda b: (b, 0, 0)),
            pl.BlockSpec((1, S, KW), lambda b: (b, 0, 0)),
        ],
        out_specs=pl.BlockSpec((1, S, QW), lambda b: (b, 0, 0)),
        out_shape=jax.ShapeDtypeStruct((B, S, QW), BF16),
        compiler_params=pltpu.CompilerParams(dimension_semantics=("arbitrary",), vmem_limit_bytes=VMEM_LIMIT_BYTES),
        name="context_attention",
    )(sink_arr, q, k, v)


HIGHEST = lax.Precision.HIGHEST
GDN_TM = 256
SUBLANES = 8


def _head_pair_sum_matrix():
    a = lax.broadcasted_iota(jnp.int32, (LANES, LANES), 0) // HEAD_DIM
    b = lax.broadcasted_iota(jnp.int32, (LANES, LANES), 1) // HEAD_DIM
    return (a == b).astype(F32)


def _gdn_prep_kernel(x_ref, prev_ref, next_ref, ab_ref, cw_ref, a_ref, dtb_ref, q_ref, k_ref, v_ref, gb_ref):
    i = pl.program_id(1)
    n = pl.num_programs(1)
    x = x_ref[0]
    tm = x.shape[0]
    row = lax.broadcasted_iota(jnp.int32, x.shape, 0)
    prev_row = jnp.where(i > 0, prev_ref[0, SUBLANES - 1:SUBLANES, :], 0.0)
    next_row = jnp.where(i < n - 1, next_ref[0, 0:1, :], 0.0)
    x_prev = jnp.where(row == 0, prev_row, pltpu.roll(x, 1, axis=0))
    x_next = jnp.where(row == tm - 1, next_row, pltpu.roll(x, tm - 1, axis=0))
    y = x_prev * cw_ref[0:1, :] + x * cw_ref[1:2, :] + x_next * cw_ref[2:3, :]
    y = y * jax.nn.sigmoid(y)
    pmat = _head_pair_sum_matrix()
    for p in range(C_W // LANES):
        qs = y[:, p * LANES:(p + 1) * LANES]
        ks = y[:, C_W + p * LANES:C_W + (p + 1) * LANES]
        q_ref[0, :, p * LANES:(p + 1) * LANES] = qs * lax.rsqrt(jnp.dot(qs * qs, pmat, precision=HIGHEST, preferred_element_type=F32) + EPS) * (HEAD_DIM ** -0.5)
        k_ref[0, :, p * LANES:(p + 1) * LANES] = ks * lax.rsqrt(jnp.dot(ks * ks, pmat, precision=HIGHEST, preferred_element_type=F32) + EPS)
    v_ref[0] = y[:, 2 * C_W:]
    ab = ab_ref[0, :, 0:4 * C_HEADS]
    lane = lax.broadcasted_iota(jnp.int32, ab.shape, 1)
    is_beta = (lane // C_HEADS) % 2 == 1
    t = ab + dtb_ref[...]
    softplus = jnp.maximum(t, 0.0) + jnp.log1p(jnp.exp(-jnp.abs(t)))
    gb = jnp.where(is_beta, jax.nn.sigmoid(ab), -jnp.exp(a_ref[...]) * softplus)
    gb_ref[0, 0] = gb[:, 0:2 * C_HEADS]
    gb_ref[0, 1] = gb[:, 2 * C_HEADS:4 * C_HEADS]


def gdn_prep(zqkv, zab, conv_w, a_log, dt_bias):
    B, L, W3 = zqkv.shape
    tm = GDN_TM
    assert L % tm == 0
    nb = tm // SUBLANES
    zero = jnp.zeros((C_HEADS,), F32)
    a_lane = jnp.concatenate([a_log[0], zero, a_log[1], zero]).astype(F32)[None, :]
    dtb_lane = jnp.concatenate([dt_bias[0], zero, dt_bias[1], zero]).astype(F32)[None, :]
    outs = pl.pallas_call(
        _gdn_prep_kernel,
        grid=(B, L // tm),
        in_specs=[
            pl.BlockSpec((1, tm, W3), lambda b, i: (b, i, 0)),
            pl.BlockSpec((1, SUBLANES, W3), lambda b, i: (b, jnp.maximum(i * nb - 1, 0), 0)),
            pl.BlockSpec((1, SUBLANES, W3), lambda b, i: (b, jnp.minimum((i + 1) * nb, L // SUBLANES - 1), 0)),
            pl.BlockSpec((1, tm, LANES), lambda b, i: (b, i, 0)),
            pl.BlockSpec((C_SHORT, W3), lambda b, i: (0, 0)),
            pl.BlockSpec((1, 4 * C_HEADS), lambda b, i: (0, 0)),
            pl.BlockSpec((1, 4 * C_HEADS), lambda b, i: (0, 0)),
        ],
        out_specs=[
            pl.BlockSpec((1, tm, C_W), lambda b, i: (b, i, 0)),
            pl.BlockSpec((1, tm, C_W), lambda b, i: (b, i, 0)),
            pl.BlockSpec((1, tm, C_W), lambda b, i: (b, i, 0)),
            pl.BlockSpec((1, 2, tm, 2 * C_HEADS), lambda b, i: (b, 0, i, 0)),
        ],
        out_shape=[jax.ShapeDtypeStruct((B, L, C_W), F32)] * 3 + [jax.ShapeDtypeStruct((B, 2, L, 2 * C_HEADS), F32)],
        compiler_params=pltpu.CompilerParams(dimension_semantics=("arbitrary", "arbitrary"),
                                             vmem_limit_bytes=VMEM_LIMIT_BYTES),
        name="gdn_prep",
    )(zqkv, zqkv, zqkv, zab, conv_w.astype(F32), a_lane, dtb_lane)
    return outs


def _gdn_kernel(q_ref, k_ref, v_ref, gb_ref, zg_ref, s0_ref, nw_ref, o_ref, st_ref, s_scr, of_scr):
    d = pl.program_id(1)
    c = pl.program_id(2)
    n = pl.num_programs(2)
    C = C_CHUNK
    fwd = d == 0
    chunk = jnp.where(fwd, c, n - 1 - c)
    r0 = pl.multiple_of(chunk * C, C)

    @pl.when(c == 0)
    def _():
        s_scr[...] = s0_ref[0, 0]

    row = lax.broadcasted_iota(jnp.int32, (C, C), 0)
    col = lax.broadcasted_iota(jnp.int32, (C, C), 1)
    ahead = jnp.where(fwd, row - col, col - row)
    incl = ahead >= 0
    strict = ahead > 0
    g = gb_ref[0, 0, :, 0:C_HEADS]
    beta = gb_ref[0, 0, :, C_HEADS:2 * C_HEADS]
    gc = jnp.dot(incl.astype(F32), g, precision=HIGHEST, preferred_element_type=F32)
    gc_t = gc.T
    g_last = jnp.where(fwd, gc[C - 1:C, :], gc[0:1, :])
    H = range(C_HEADS)
    heads = [slice(h * HEAD_DIM, (h + 1) * HEAD_DIM) for h in H]
    s_old = [s_scr[h] for h in H]
    qs = [q_ref[0, :, hs] for hs in heads]
    ks = [k_ref[0, :, hs] for hs in heads]
    vs = [v_ref[0, :, hs] for hs in heads]
    gcol = [gc[:, h:h + 1] for h in H]
    bcol = [beta[:, h:h + 1] for h in H]
    gl = [g_last[:, h:h + 1] for h in H]
    gamma = [jnp.exp(jnp.where(incl, gcol[h] - gc_t[h:h + 1, :], NEG_INF)) for h in H]
    egc = [jnp.exp(gcol[h]) for h in H]
    kb = [ks[h].astype(BF16) for h in H]
    nmat = [jnp.where(strict, _dot_nt(kb[h], kb[h]) * gamma[h], 0.0) * bcol[h] for h in H]
    attn = [(_dot_nt(qs[h].astype(BF16), kb[h]) * gamma[h]).astype(BF16) for h in H]
    xr = row ^ col
    eye = (row == col).astype(F32)
    tinv = [eye - jnp.where((xr >> 1) == 0, nmat[h], 0.0) for h in H]
    for lvl in range(1, 6):
        off_diag = (xr >> lvl) == 1
        wmat = [_dot(jnp.where(off_diag, nmat[h], 0.0).astype(BF16), tinv[h].astype(BF16)).astype(BF16) for h in H]
        tinv = [tinv[h] - _dot(tinv[h].astype(BF16), wmat[h]) for h in H]
    x = [jnp.concatenate([vs[h] * bcol[h], ks[h] * (bcol[h] * egc[h])], axis=1) for h in H]
    x = [x[h] + _dot((tinv[h] - eye).astype(BF16), x[h].astype(BF16)) for h in H]
    sb = [s_old[h].astype(BF16) for h in H]
    v_new = [x[h][:, :HEAD_DIM] - _dot(x[h][:, HEAD_DIM:].astype(BF16), sb[h]) for h in H]
    vb = [v_new[h].astype(BF16) for h in H]
    outs = [_dot((qs[h] * egc[h]).astype(BF16), sb[h]) + _dot(attn[h], vb[h]) for h in H]
    kd = [(ks[h] * jnp.exp(gl[h] - gcol[h])).astype(BF16) for h in H]
    s_new = [s_old[h] * jnp.exp(gl[h]) + lax.dot_general(kd[h], vb[h], (((0,), (0,)), ((), ())), preferred_element_type=F32)
             for h in H]
    for h in H:
        s_scr[h] = s_new[h]
    o = jnp.concatenate(outs, axis=1)

    @pl.when(fwd)
    def _():
        of_scr[pl.ds(r0, C), :] = o

    @pl.when(jnp.logical_not(fwd))
    def _():
        tot = of_scr[pl.ds(r0, C), :] + o
        pmat = _head_pair_sum_matrix()
        zg = zg_ref[0]
        gate = zg * jax.nn.sigmoid(zg)
        for p in range(C_W // LANES):
            cols = slice(p * LANES, (p + 1) * LANES)
            t = tot[:, cols]
            ms = jnp.dot(t * t, pmat, precision=HIGHEST, preferred_element_type=F32) * (1.0 / HEAD_DIM)
            o_ref[0, :, cols] = (t * lax.rsqrt(ms + EPS) * nw_ref[:, cols] * gate[:, cols]).astype(o_ref.dtype)

    @pl.when(c == n - 1)
    def _():
        st_ref[0, 0] = s_scr[...]


def gdn_scan(q, k, v, gb, zg, s0, norm_w):
    B, L, W = q.shape
    C = C_CHUNK
    n = L // C
    assert L % C == 0 and W == C_W
    chunk_of = lambda d, c: jnp.where(d == 0, c, n - 1 - c)
    seq_spec = pl.BlockSpec((1, C, W), lambda b, d, c: (b, chunk_of(d, c), 0))
    state_spec = pl.BlockSpec((1, 1, C_HEADS, HEAD_DIM, HEAD_DIM), lambda b, d, c: (b, d, 0, 0, 0))
    nw = jnp.tile(norm_w.astype(F32), C_HEADS)[None, :]
    return pl.pallas_call(
        _gdn_kernel,
        grid=(B, 2, n),
        in_specs=[
            seq_spec, seq_spec, seq_spec,
            pl.BlockSpec((1, 1, C, 2 * C_HEADS), lambda b, d, c: (b, d, chunk_of(d, c), 0)),
            seq_spec,
            state_spec,
            pl.BlockSpec((1, W), lambda b, d, c: (0, 0)),
        ],
        out_specs=[
            pl.BlockSpec((1, C, W), lambda b, d, c: (b, jnp.where(d == 0, n - 1, n - 1 - c), 0)),
            state_spec,
        ],
        out_shape=[jax.ShapeDtypeStruct((B, L, W), BF16), jax.ShapeDtypeStruct(s0.shape, F32)],
        scratch_shapes=[pltpu.VMEM((C_HEADS, HEAD_DIM, HEAD_DIM), F32), pltpu.VMEM((L, W), F32)],
        compiler_params=pltpu.CompilerParams(dimension_semantics=("arbitrary", "arbitrary", "arbitrary"),
                                             vmem_limit_bytes=VMEM_LIMIT_BYTES),
        name="gdn_scan",
    )(q, k, v, gb, zg, s0.astype(F32), nw)


MOE_TT = 512
MOE_RT = 512


def _moe_router_kernel(x_ref, g_ref, shift_ref, scale_ref, wr_ref, h_ref, aff_ref):
    x = x_ref[0]
    y = x * lax.rsqrt(jnp.mean(x * x, axis=-1, keepdims=True) + EPS) * g_ref[...]
    h = y * (1.0 + scale_ref[0]) + shift_ref[0]
    h_ref[0] = h.astype(BF16)
    logits = jnp.dot(h, wr_ref[...], precision=HIGHEST, preferred_element_type=F32)
    lane = lax.broadcasted_iota(jnp.int32, logits.shape, 1)
    logits = jnp.where(lane < N_EXPERTS, logits, NEG_INF)
    e = jnp.exp(logits - logits.max(axis=-1, keepdims=True))
    aff_ref[0] = e / e.sum(axis=-1, keepdims=True)


def _moe_select_kernel(aff_ref, slot_ref, *, cap):
    T = aff_ref.shape[1]
    bits = pltpu.bitcast(aff_ref[0], jnp.int32)

    def bisect(i, v):
        cand = v | (1 << (30 - i))
        cnt = jnp.sum((bits >= cand).astype(jnp.int32), axis=0, keepdims=True)
        return jnp.where(cnt >= cap, cand, v)

    thr = lax.fori_loop(0, 31, bisect, jnp.zeros((1, LANES), jnp.int32))
    gt = (bits > thr).astype(F32)
    eq = (bits == thr).astype(F32)
    need = cap - jnp.sum(gt, axis=0, keepdims=True)
    blk = min(T, MOE_TT)
    r = lax.broadcasted_iota(jnp.int32, (blk, blk), 0)
    c = lax.broadcasted_iota(jnp.int32, (blk, blk), 1)
    before = (c < r).astype(BF16)
    carry_gt = jnp.zeros((1, LANES), F32)
    carry_eq = jnp.zeros((1, LANES), F32)
    for b in range(T // blk):
        rows = slice(b * blk, (b + 1) * blk)
        gt_b, eq_b = gt[rows], eq[rows]
        pos_gt = _dot(before, gt_b.astype(BF16)) + carry_gt
        pos_eq = _dot(before, eq_b.astype(BF16)) + carry_eq
        chosen = gt_b + eq_b * (pos_eq < need).astype(F32)
        slot_ref[0, rows, :] = jnp.where(chosen > 0.5, pos_gt + jnp.minimum(pos_eq, need), -1.0)
        carry_gt = carry_gt + jnp.sum(gt_b, axis=0, keepdims=True)
        carry_eq = carry_eq + jnp.sum(eq_b, axis=0, keepdims=True)


def _expert_column(a, e):
    lane = lax.broadcasted_iota(jnp.int32, a.shape, 1)
    return jnp.sum(jnp.where(lane == e, a, 0.0), axis=1, keepdims=True)


def _one_hot_slots(slot_col, cap):
    s = lax.broadcasted_iota(jnp.int32, (slot_col.shape[0], cap), 1).astype(F32)
    return (slot_col == s).astype(BF16)


def _moe_gather_kernel(slot_ref, h_ref, xg_ref, acc_ref, *, cap):
    e, k = pl.program_id(1), pl.program_id(2)

    @pl.when(k == 0)
    def _():
        acc_ref[...] = jnp.zeros_like(acc_ref)

    pt = _one_hot_slots(_expert_column(slot_ref[0], e), cap)
    acc_ref[...] += lax.dot_general(pt, h_ref[0], (((0,), (0,)), ((), ())), preferred_element_type=F32)

    @pl.when(k == pl.num_programs(2) - 1)
    def _():
        xg_ref[0] = acc_ref[...].astype(xg_ref.dtype)


def _moe_ffn_kernel(x_ref, wg_ref, wu_ref, wd_ref, y_ref, wg_s, wu_s, wd_s):
    @pl.when(pl.program_id(1) == 0)
    def _():
        wg_s[...] = wg_ref[0].astype(BF16)
        wu_s[...] = wu_ref[0].astype(BF16)
        wd_s[...] = wd_ref[0].astype(BF16)

    x = x_ref[0]
    a = _dot(x, wg_s[...])
    u = _dot(x, wu_s[...])
    hid = (a * jax.nn.sigmoid(a) * u).astype(BF16)
    y_ref[0] = _dot(hid, wd_s[...]).astype(y_ref.dtype)


def _moe_scatter_kernel(slot_ref, aff_ref, y_ref, x_ref, gate_ref, o_ref, acc_ref, *, cap):
    e = pl.program_id(2)

    @pl.when(e == 0)
    def _():
        acc_ref[...] = jnp.zeros_like(acc_ref)

    pt = _one_hot_slots(_expert_column(slot_ref[0], e), cap)
    acc_ref[...] += _expert_column(aff_ref[0], e) * _dot(pt, y_ref[0])

    @pl.when(e == pl.num_programs(2) - 1)
    def _():
        o_ref[0] = x_ref[0] + gate_ref[0] * acc_ref[...]


def moe_route(x, g, shift, scale, w_router):
    B, T, D = x.shape
    tt = min(T, MOE_TT)
    cap = EC_CAPACITY * T // N_EXPERTS
    per_request = shift.shape[0] == B
    mod_spec = pl.BlockSpec((1, 1, D), (lambda b, k: (b, 0, 0)) if per_request else (lambda b, k: (0, 0, 0)))
    wr = jnp.concatenate([w_router.astype(F32), jnp.zeros((D, LANES - N_EXPERTS), F32)], axis=1)
    h, aff = pl.pallas_call(
        _moe_router_kernel,
        grid=(B, T // tt),
        in_specs=[pl.BlockSpec((1, tt, D), lambda b, k: (b, k, 0)), pl.BlockSpec((1, D), lambda b, k: (0, 0)),
                  mod_spec, mod_spec, pl.BlockSpec((D, LANES), lambda b, k: (0, 0))],
        out_specs=[pl.BlockSpec((1, tt, D), lambda b, k: (b, k, 0)), pl.BlockSpec((1, tt, LANES), lambda b, k: (b, k, 0))],
        out_shape=[jax.ShapeDtypeStruct((B, T, D), BF16), jax.ShapeDtypeStruct((B, T, LANES), F32)],
        compiler_params=pltpu.CompilerParams(dimension_semantics=("arbitrary", "arbitrary"),
                                             vmem_limit_bytes=VMEM_LIMIT_BYTES),
        name="moe_router",
    )(x, g.astype(F32)[None, :], shift, scale, wr)
    slot = pl.pallas_call(
        functools.partial(_moe_select_kernel, cap=cap),
        grid=(B,),
        in_specs=[pl.BlockSpec((1, T, LANES), lambda b: (b, 0, 0))],
        out_specs=pl.BlockSpec((1, T, LANES), lambda b: (b, 0, 0)),
        out_shape=jax.ShapeDtypeStruct((B, T, LANES), F32),
        compiler_params=pltpu.CompilerParams(dimension_semantics=("arbitrary",), vmem_limit_bytes=VMEM_LIMIT_BYTES),
        name="moe_select",
    )(aff)
    return h, aff, slot


def moe_gather(h, slot):
    B, T, D = h.shape
    tt = min(T, MOE_TT)
    cap = EC_CAPACITY * T // N_EXPERTS
    return pl.pallas_call(
        functools.partial(_moe_gather_kernel, cap=cap),
        grid=(B, N_EXPERTS, T // tt),
        in_specs=[pl.BlockSpec((1, tt, LANES), lambda b, e, k: (b, k, 0)), pl.BlockSpec((1, tt, D), lambda b, e, k: (b, k, 0))],
        out_specs=pl.BlockSpec((1, cap, D), lambda b, e, k: (e, b, 0)),
        out_shape=jax.ShapeDtypeStruct((N_EXPERTS, B * cap, D), BF16),
        scratch_shapes=[pltpu.VMEM((cap, D), F32)],
        compiler_params=pltpu.CompilerParams(dimension_semantics=("arbitrary",) * 3, vmem_limit_bytes=VMEM_LIMIT_BYTES),
        name="moe_gather",
    )(slot, h)


def moe_ffn(xg, w_gate, w_up, w_down, layer):
    E, R, D = xg.shape
    rt = min(R, MOE_RT)
    w_spec = pl.BlockSpec((None, 1, D, MOE_D_FF), lambda e, r: (layer, e, 0, 0))
    return pl.pallas_call(
        _moe_ffn_kernel,
        grid=(E, R // rt),
        in_specs=[pl.BlockSpec((1, rt, D), lambda e, r: (e, r, 0)), w_spec, w_spec,
                  pl.BlockSpec((None, 1, MOE_D_FF, D), lambda e, r: (layer, e, 0, 0))],
        out_specs=pl.BlockSpec((1, rt, D), lambda e, r: (e, r, 0)),
        out_shape=jax.ShapeDtypeStruct((E, R, D), BF16),
        scratch_shapes=[pltpu.VMEM((D, MOE_D_FF), BF16), pltpu.VMEM((D, MOE_D_FF), BF16), pltpu.VMEM((MOE_D_FF, D), BF16)],
        compiler_params=pltpu.CompilerParams(dimension_semantics=("arbitrary", "arbitrary"),
                                             vmem_limit_bytes=VMEM_LIMIT_BYTES),
        name="moe_ffn",
    )(xg, w_gate, w_up, w_down)


def moe_scatter(slot, aff, y, x, gate):
    B, T, D = x.shape
    tt = min(T, MOE_TT)
    cap = EC_CAPACITY * T // N_EXPERTS
    per_request = gate.shape[0] == B
    tile = lambda b, k, e: (b, k, 0)
    return pl.pallas_call(
        functools.partial(_moe_scatter_kernel, cap=cap),
        grid=(B, T // tt, N_EXPERTS),
        in_specs=[pl.BlockSpec((1, tt, LANES), tile), pl.BlockSpec((1, tt, LANES), tile),
                  pl.BlockSpec((1, cap, D), lambda b, k, e: (e, b, 0)), pl.BlockSpec((1, tt, D), tile),
                  pl.BlockSpec((1, 1, D), (lambda b, k, e: (b, 0, 0)) if per_request else (lambda b, k, e: (0, 0, 0)))],
        out_specs=pl.BlockSpec((1, tt, D), tile),
        out_shape=jax.ShapeDtypeStruct((B, T, D), F32),
        scratch_shapes=[pltpu.VMEM((tt, D), F32)],
        compiler_params=pltpu.CompilerParams(dimension_semantics=("arbitrary",) * 3, vmem_limit_bytes=VMEM_LIMIT_BYTES),
        name="moe_scatter",
    )(slot, aff, y, x, gate)


def moe_block(x, g, shift, scale, gate, w_router, w_gate, w_up, w_down, layer):
    h, aff, slot = moe_route(x, g, shift, scale, w_router)
    y = moe_ffn(moe_gather(h, slot), w_gate, w_up, w_down, layer)
    return moe_scatter(slot, aff, y, x, gate)


HY_TM = 512
HY_BG = 8
HY_BG_LONG = 2


def dft_tables(L):
    f = jnp.arange(L, dtype=jnp.int32)
    r = (f[:, None] * f[None, :]) % (2 * L)
    ang = r.astype(F32) * (math.pi / L)
    out = []
    for tab in (jnp.cos(ang), jnp.sin(ang)):
        hi = tab.astype(BF16)
        out += [hi, (tab - hi.astype(F32)).astype(BF16)]
    return out


def _split_bf16(x):
    hi = x.astype(BF16)
    return hi, (x - hi.astype(F32)).astype(BF16)


def _mm3(t_hi, t_lo, x_hi, x_lo):
    return _dot(t_hi, x_hi) + _dot(t_hi, x_lo) + _dot(t_lo, x_hi)


def _alt_sign(rows, first_row):
    t = lax.broadcasted_iota(jnp.int32, (rows, 1), 0) + first_row
    return jnp.where(t % 2 == 0, 1.0, -1.0).astype(F32)


def _hy_prep_kernel(x_ref, prev_ref, next_ref, cw_ref, cb_ref, o_ref):
    i = pl.program_id(1)
    n = pl.num_programs(1)
    x = x_ref[0]
    tm = x.shape[0]
    row = lax.broadcasted_iota(jnp.int32, x.shape, 0)
    prev_row = jnp.where(i > 0, prev_ref[0, SUBLANES - 1:SUBLANES, :], 0.0)
    next_row = jnp.where(i < n - 1, next_ref[0, 0:1, :], 0.0)
    x_prev = jnp.where(row == 0, prev_row, pltpu.roll(x, 1, axis=0))
    x_next = jnp.where(row == tm - 1, next_row, pltpu.roll(x, tm - 1, axis=0))
    o_ref[0] = x_prev * cw_ref[0:1, :] + x * cw_ref[1:2, :] + x_next * cw_ref[2:3, :] + cb_ref[...]


def hyena_prep(zh, conv_w, conv_b):
    B, L, W = zh.shape
    tm = min(L, GDN_TM)
    nb = tm // SUBLANES
    return pl.pallas_call(
        _hy_prep_kernel,
        grid=(B, L // tm),
        in_specs=[
            pl.BlockSpec((1, tm, W), lambda b, i: (b, i, 0)),
            pl.BlockSpec((1, SUBLANES, W), lambda b, i: (b, jnp.maximum(i * nb - 1, 0), 0)),
            pl.BlockSpec((1, SUBLANES, W), lambda b, i: (b, jnp.minimum((i + 1) * nb, L // SUBLANES - 1), 0)),
            pl.BlockSpec((HY_SHORT, W), lambda b, i: (0, 0)),
            pl.BlockSpec((1, W), lambda b, i: (0, 0)),
        ],
        out_specs=pl.BlockSpec((1, tm, W), lambda b, i: (b, i, 0)),
        out_shape=jax.ShapeDtypeStruct((B, L, W), F32),
        compiler_params=pltpu.CompilerParams(dimension_semantics=("arbitrary", "arbitrary"),
                                             vmem_limit_bytes=VMEM_LIMIT_BYTES),
        name="hyena_prep",
    )(zh, zh, zh, conv_w.astype(F32), conv_b.astype(F32)[None, :])


def _hy_taps_kernel(feat_ref, w1_ref, b1_ref, w2_ref, b2_ref, fr_ref, w3f_ref, w3b_ref, dec_ref, sum_ref, dif_ref):
    L = feat_ref.shape[0]
    fr = fr_ref[...]
    h = jnp.sin(fr * (jnp.dot(feat_ref[...], w1_ref[...], precision=HIGHEST, preferred_element_type=F32) + b1_ref[...]))
    h = jnp.sin(fr * (jnp.dot(h, w2_ref[...], precision=HIGHEST, preferred_element_type=F32) + b2_ref[...]))
    t = lax.broadcasted_iota(jnp.int32, (L, 1), 0)
    window = jnp.exp(-(t.astype(F32) / (L - 1)) * dec_ref[...]) + HY_SHIFT
    fwd = jnp.dot(h, w3f_ref[...], precision=HIGHEST, preferred_element_type=F32) * window
    bwd = jnp.where(t == 0, 0.0, jnp.dot(h, w3b_ref[...], precision=HIGHEST, preferred_element_type=F32) * window)
    inv = 1.0 / (jnp.sum(jnp.abs(fwd), axis=0, keepdims=True) + jnp.sum(jnp.abs(bwd), axis=0, keepdims=True))
    sum_ref[...] = (fwd + bwd) * inv
    dif_ref[...] = (bwd - fwd) * inv


def hyena_taps(L, w1, b1, w2, b2, w3, freq):
    f32 = F32
    t = jnp.linspace(0.0, 1.0, L, dtype=f32)[:, None]
    bands = (HY_EMB - 1) // 2
    omega = 2.0 * math.pi * jnp.arange(L, dtype=f32)[:, None] / L
    fb = jnp.linspace(1e-4, bands - 1, bands, dtype=f32)[None, :]
    feats = jnp.concatenate([t, jnp.cos(fb * omega), -jnp.sin(fb * omega)], axis=-1)
    max_decay = math.log(HY_DECAY_TARGET) / HY_FAST_DECAY
    min_decay = math.log(HY_DECAY_TARGET) / HY_SLOW_DECAY
    deltas = jnp.abs(jnp.linspace(min_decay, max_decay, HY_CH, dtype=f32))
    dec = jnp.tile(deltas, HY_ORDER)[None, :]
    n_col = HY_ORDER * HY_CH
    cb = 256
    full = lambda shape: pl.BlockSpec(shape, lambda j: (0, 0))
    col = lambda rows: pl.BlockSpec((rows, cb), lambda j: (0, j))
    return pl.pallas_call(
        _hy_taps_kernel,
        grid=(n_col // cb,),
        in_specs=[full((L, HY_EMB)), full((HY_EMB, HY_FILT_W)), full((1, HY_FILT_W)), full((HY_FILT_W, HY_FILT_W)),
                  full((1, HY_FILT_W)), full((1, HY_FILT_W)), col(HY_FILT_W), col(HY_FILT_W), col(1)],
        out_specs=[col(L), col(L)],
        out_shape=[jax.ShapeDtypeStruct((L, n_col), f32)] * 2,
        compiler_params=pltpu.CompilerParams(dimension_semantics=("arbitrary",), vmem_limit_bytes=VMEM_LIMIT_BYTES),
        name="hyena_taps",
    )(feats, w1.astype(f32), b1.astype(f32)[None, :], w2.astype(f32), b2.astype(f32)[None, :], freq.astype(f32)[None, :],
      w3.astype(f32)[:, :n_col], w3.astype(f32)[:, n_col:], dec)


def _hy_spec_kernel(ch_ref, cl_ref, sh_ref, sl_ref, sum_ref, dif_ref, hr_ref, hi_ref, ny_ref, acc_r, acc_i, acc_n):
    m, k = pl.program_id(1), pl.program_id(2)
    tk = sum_ref.shape[0]

    @pl.when(k == 0)
    def _():
        acc_r[...] = jnp.zeros_like(acc_r)
        acc_i[...] = jnp.zeros_like(acc_i)

    @pl.when(jnp.logical_and(k == 0, m == 0))
    def _():
        acc_n[...] = jnp.zeros_like(acc_n)

    a = sum_ref[...]
    acc_r[...] += _mm3(ch_ref[...], cl_ref[...], *_split_bf16(a))
    acc_i[...] += _mm3(sh_ref[...], sl_ref[...], *_split_bf16(dif_ref[...]))

    @pl.when(m == 0)
    def _():
        acc_n[...] += jnp.sum(a * _alt_sign(tk, k * tk), axis=0, keepdims=True)

    @pl.when(k == pl.num_programs(2) - 1)
    def _():
        hr_ref[...] = acc_r[...]
        hi_ref[...] = acc_i[...]
        ny_ref[...] = jnp.broadcast_to(acc_n[...], ny_ref.shape)


def hyena_spectrum(tables, tap_sum, tap_dif):
    L, N = tap_sum.shape
    tm = min(L, HY_TM)
    cb = 512
    tab = pl.BlockSpec((tm, tm), lambda j, m, k: (m, k))
    dat = pl.BlockSpec((tm, cb), lambda j, m, k: (k, j))
    return pl.pallas_call(
        _hy_spec_kernel,
        grid=(N // cb, L // tm, L // tm),
        in_specs=[tab, tab, tab, tab, dat, dat],
        out_specs=[pl.BlockSpec((tm, cb), lambda j, m, k: (m, j)), pl.BlockSpec((tm, cb), lambda j, m, k: (m, j)),
                   pl.BlockSpec((SUBLANES, cb), lambda j, m, k: (0, j))],
        out_shape=[jax.ShapeDtypeStruct((L, N), F32), jax.ShapeDtypeStruct((L, N), F32),
                   jax.ShapeDtypeStruct((SUBLANES, N), F32)],
        scratch_shapes=[pltpu.VMEM((tm, cb), F32), pltpu.VMEM((tm, cb), F32), pltpu.VMEM((1, cb), F32)],
        compiler_params=pltpu.CompilerParams(dimension_semantics=("arbitrary",) * 3, vmem_limit_bytes=VMEM_LIMIT_BYTES),
        name="hyena_spectrum",
    )(*tables, tap_sum, tap_dif)


def _hy_fwd_kernel(ch_ref, cl_ref, sh_ref, sl_ref, u_ref, hr_ref, hi_ref, hny_ref, yr_ref, yi_ref, yny_ref,
                   acc_c, acc_s, acc_n):
    m, k = pl.program_id(1), pl.program_id(2)
    nb, tk = u_ref.shape[0], u_ref.shape[1]
    tm = acc_c.shape[1]

    @pl.when(k == 0)
    def _():
        acc_c[...] = jnp.zeros_like(acc_c)
        acc_s[...] = jnp.zeros_like(acc_s)

    @pl.when(jnp.logical_and(k == 0, m == 0))
    def _():
        acc_n[...] = jnp.zeros_like(acc_n)

    sign = _alt_sign(tk, k * tk)
    for b in range(nb):
        u = u_ref[b]
        uh, ul = _split_bf16(u)
        acc_c[b] += _mm3(ch_ref[...], cl_ref[...], uh, ul)
        acc_s[b] += _mm3(sh_ref[...], sl_ref[...], uh, ul)

        @pl.when(m == 0)
        def _():
            acc_n[b] += jnp.sum(u * sign, axis=0, keepdims=True)

    @pl.when(k == pl.num_programs(2) - 1)
    def _():
        f = lax.broadcasted_iota(jnp.int32, (tm, 1), 0) + m * tm
        dc = jnp.where(f == 0, 0.5, 1.0).astype(F32)
        hr, hi = hr_ref[...], hi_ref[...]
        for b in range(nb):
            xr, xs = acc_c[b], acc_s[b]
            yr_ref[b] = (xr * hr + xs * hi) * dc
            yi_ref[b] = xr * hi - xs * hr
            yny_ref[b] = jnp.broadcast_to(acc_n[b] * hny_ref[0:1, :], yny_ref.shape[1:])


def _hy_inv_kernel(ch_ref, cl_ref, sh_ref, sl_ref, yr_ref, yi_ref, yny_ref, u_ref, xg_ref, skip_ref, o_ref, acc):
    m, k = pl.program_id(1), pl.program_id(2)
    nb = yr_ref.shape[0]
    tm = acc.shape[1]
    L = tm * pl.num_programs(1)

    @pl.when(k == 0)
    def _():
        acc[...] = jnp.zeros_like(acc)

    for b in range(nb):
        acc[b] += (_mm3(ch_ref[...], cl_ref[...], *_split_bf16(yr_ref[b]))
                   - _mm3(sh_ref[...], sl_ref[...], *_split_bf16(yi_ref[b])))

    @pl.when(k == pl.num_programs(2) - 1)
    def _():
        sign = _alt_sign(tm, m * tm)
        for b in range(nb):
            u = u_ref[b]
            y = acc[b] * (1.0 / L) + sign * yny_ref[b, 0:1, :] * (0.5 / L)
            o_ref[b] = (xg_ref[b] * (y + u * skip_ref[...])).astype(o_ref.dtype)


def hyena_long_conv(tables, zf, u, u_col, gate_col, hr, hi, hny, order, skip, out_dtype):
    B, L, _ = u.shape
    C = HY_CH
    tm = min(L, HY_TM)
    bg = min(B, HY_BG_LONG if L >= HY_TM else HY_BG)
    assert B % bg == 0
    grid = (B // bg, L // tm, L // tm)
    tab = pl.BlockSpec((tm, tm), lambda g, m, k: (m, k))
    params = pltpu.CompilerParams(dimension_semantics=("arbitrary",) * 3, vmem_limit_bytes=VMEM_LIMIT_BYTES)
    spec_m = pl.BlockSpec((tm, C), lambda g, m, k: (m, order))
    yr, yi, yny = pl.pallas_call(
        _hy_fwd_kernel,
        grid=grid,
        in_specs=[tab, tab, tab, tab, pl.BlockSpec((bg, tm, C), lambda g, m, k: (g, k, u_col)), spec_m, spec_m,
                  pl.BlockSpec((SUBLANES, C), lambda g, m, k: (0, order))],
        out_specs=[pl.BlockSpec((bg, tm, C), lambda g, m, k: (g, m, 0)), pl.BlockSpec((bg, tm, C), lambda g, m, k: (g, m, 0)),
                   pl.BlockSpec((bg, SUBLANES, C), lambda g, m, k: (g, 0, 0))],
        out_shape=[jax.ShapeDtypeStruct((B, L, C), F32), jax.ShapeDtypeStruct((B, L, C), F32),
                   jax.ShapeDtypeStruct((B, SUBLANES, C), F32)],
        scratch_shapes=[pltpu.VMEM((bg, tm, C), F32), pltpu.VMEM((bg, tm, C), F32), pltpu.VMEM((bg, 1, C), F32)],
        compiler_params=params,
        name="hyena_fwd",
    )(*tables, u, hr, hi, hny)
    return pl.pallas_call(
        _hy_inv_kernel,
        grid=grid,
        in_specs=[tab, tab, tab, tab, pl.BlockSpec((bg, tm, C), lambda g, m, k: (g, k, 0)),
                  pl.BlockSpec((bg, tm, C), lambda g, m, k: (g, k, 0)),
                  pl.BlockSpec((bg, SUBLANES, C), lambda g, m, k: (g, 0, 0)),
                  pl.BlockSpec((bg, tm, C), lambda g, m, k: (g, m, u_col)),
                  pl.BlockSpec((bg, tm, C), lambda g, m, k: (g, m, gate_col)),
                  pl.BlockSpec((1, C), lambda g, m, k: (0, 0))],
        out_specs=pl.BlockSpec((bg, tm, C), lambda g, m, k: (g, m, 0)),
        out_shape=jax.ShapeDtypeStruct((B, L, C), out_dtype),
        scratch_shapes=[pltpu.VMEM((bg, tm, C), F32)],
        compiler_params=params,
        name="hyena_inv",
    )(*tables, yr, yi, yny, u, zf, skip.astype(F32)[order][None, :])


def hyena_filter_spectrum(L, w1, b1, w2, b2, w3, freq):
    tables = dft_tables(L)
    tap_sum, tap_dif = hyena_taps(L, w1, b1, w2, b2, w3, freq)
    return tables, hyena_spectrum(tables, tap_sum, tap_dif)


def hyena_mixer_pallas(zh, conv_w, conv_b, skip, tables, spectrum):
    hr, hi, hny = spectrum
    zf = hyena_prep(zh, conv_w, conv_b)
    y1 = hyena_long_conv(tables, zf, zf, 0, 1, hr, hi, hny, 0, skip, F32)
    return hyena_long_conv(tables, zf, y1, 0, 2, hr, hi, hny, 1, skip, BF16)


def rms_norm(x, g):
    xf = x.astype(jnp.float32)
    y = xf * lax.rsqrt(jnp.mean(xf * xf, axis=-1, keepdims=True) + EPS)
    return (y * g.astype(jnp.float32)).astype(x.dtype)


def _ada_kernel(c_ref, w_ref, b_ref, o_ref):
    cnd = c_ref[...]
    act = cnd * jax.nn.sigmoid(cnd)
    o_ref[0] = jnp.dot(act, w_ref[0], precision=HIGHEST, preferred_element_type=F32) + b_ref[0]


def ada_params_all(cond, w_ada, b_ada):
    N, D = cond.shape
    depth, _, W = w_ada.shape
    rows = -(-N // SUBLANES) * SUBLANES
    cond_p = jnp.concatenate([cond.astype(F32), jnp.zeros((rows - N, D), F32)], axis=0)
    cb = D
    out = pl.pallas_call(
        _ada_kernel,
        grid=(depth, W // cb),
        in_specs=[pl.BlockSpec((rows, D), lambda l, j: (0, 0)), pl.BlockSpec((1, D, cb), lambda l, j: (l, 0, j)),
                  pl.BlockSpec((1, 1, cb), lambda l, j: (l, 0, j))],
        out_specs=pl.BlockSpec((1, rows, cb), lambda l, j: (l, 0, j)),
        out_shape=jax.ShapeDtypeStruct((depth, rows, W), F32),
        compiler_params=pltpu.CompilerParams(dimension_semantics=("arbitrary", "arbitrary"),
                                             vmem_limit_bytes=VMEM_LIMIT_BYTES),
        name="ada_params",
    )(cond_p, w_ada.astype(F32), b_ada.astype(F32)[:, None, :])
    return out[:, :N]


def _final_norm_kernel(x_ref, g_ref, o_ref):
    x = x_ref[...]
    o_ref[...] = x * lax.rsqrt(jnp.mean(x * x, axis=-1, keepdims=True) + EPS) * g_ref[...]


def final_norm(x, g, tm=512):
    B, L, D = x.shape
    M = B * L
    assert M % tm == 0
    out = pl.pallas_call(
        _final_norm_kernel,
        grid=(M // tm,),
        in_specs=[pl.BlockSpec((tm, D), lambda i: (i, 0)), pl.BlockSpec((1, D), lambda i: (0, 0))],
        out_specs=pl.BlockSpec((tm, D), lambda i: (i, 0)),
        out_shape=jax.ShapeDtypeStruct((M, D), F32),
        compiler_params=pltpu.CompilerParams(dimension_semantics=("arbitrary",), vmem_limit_bytes=VMEM_LIMIT_BYTES),
        name="final_norm",
    )(x.reshape(M, D), g.astype(F32)[None, :])
    return out.reshape(B, L, D)


def ada_params(cond, w, b):
    m = jax.nn.silu(cond) @ w + b
    return jnp.split(m[:, None, :], 6, axis=-1)


def modulate(x, g, shift, scale):
    return rms_norm(x, g) * (1 + scale) + shift


def axial_rope(T):
    t = jnp.arange(T)
    n_freq = HEAD_DIM // 4
    inv = ROPE_BASE ** (-jnp.arange(n_freq, dtype=jnp.float32) / n_freq)
    ang = jnp.concatenate([(t // GRID_W).astype(jnp.float32)[:, None] * inv,
                           (t % GRID_W).astype(jnp.float32)[:, None] * inv], axis=-1)
    return jnp.cos(ang)[:, None, :], jnp.sin(ang)[:, None, :]


def apply_rope(x, cos, sin):
    xf = x.astype(jnp.float32)
    x1, x2 = jnp.split(xf, 2, axis=-1)
    return jnp.concatenate([x1 * cos - x2 * sin, x2 * cos + x1 * sin], axis=-1).astype(x.dtype)


def softmax_parts(parts, sink=None):
    sizes = [p.shape[-1] for p in parts]
    cols = list(parts)
    if sink is not None:
        cols.append(jnp.broadcast_to(sink, parts[0].shape[:-1] + (1,)))
    p = jax.nn.softmax(jnp.concatenate(cols, axis=-1), axis=-1)
    pieces = jnp.split(p, np.cumsum(sizes).tolist(), axis=-1)
    return pieces[:len(sizes)]


def context_attention(q, k, v, sink):
    B, S, HQ, hd = q.shape
    HK = k.shape[2]
    G = HQ // HK
    nb = S // Q_BLOCK
    scale = hd ** -0.5
    sink_b = None if sink is None else sink.astype(jnp.float32).reshape(1, HK, G, 1, 1)
    qb = q.reshape(B, nb, Q_BLOCK, HK, G, hd).swapaxes(0, 1)

    def one(qi):
        s = jnp.einsum('bqkgd,bskd->bkgqs', qi, k, preferred_element_type=jnp.float32) * scale
        (p,) = softmax_parts([s], sink_b)
        return jnp.einsum('bkgqs,bskd->bqkgd', p.astype(v.dtype), v)

    o = lax.map(one, qb)
    return o.swapaxes(0, 1).reshape(B, S, HQ, hd)


def window_attention(q, k, v, ck, cv, sink):
    B, T, HQ, hd = q.shape
    HK = k.shape[2]
    G = HQ // HK
    nb = T // A_BLOCK
    scale = hd ** -0.5
    qb = q.reshape(B, nb, A_BLOCK, HK, G, hd).swapaxes(0, 1)

    def band(x):
        xp = jnp.pad(x, ((0, 0), (A_BLOCK, A_BLOCK), (0, 0), (0, 0))).reshape(B, nb + 2, A_BLOCK, HK, hd)
        return jnp.concatenate([xp[:, :-2], xp[:, 1:-1], xp[:, 2:]], axis=2).swapaxes(0, 1)

    kb, vb = band(k), band(v)
    qpos = jnp.arange(nb)[:, None, None] * A_BLOCK + jnp.arange(A_BLOCK)[None, :, None]
    kpos = jnp.arange(nb)[:, None, None] * A_BLOCK - A_BLOCK + jnp.arange(3 * A_BLOCK)[None, None, :]
    mask = (jnp.abs(kpos - qpos) <= A_WINDOW) & (kpos >= 0) & (kpos < T)
    sink_b = sink.astype(jnp.float32).reshape(1, HK, G, 1, 1)

    def one(xs):
        qi, ki, vi, mi = xs
        s_loc = jnp.einsum('bqkgd,bskd->bkgqs', qi, ki, preferred_element_type=jnp.float32) * scale
        s_loc = jnp.where(mi[None, None, None], s_loc, NEG_INF)
        s_ctx = jnp.einsum('bqkgd,bpkd->bkgqp', qi, ck, preferred_element_type=jnp.float32) * scale
        p_loc, p_ctx = softmax_parts([s_loc, s_ctx], sink_b)
        return (jnp.einsum('bkgqs,bskd->bqkgd', p_loc.astype(vi.dtype), vi)
                + jnp.einsum('bkgqp,bpkd->bqkgd', p_ctx.astype(cv.dtype), cv))

    o = lax.map(one, (qb, kb, vb, mask))
    return o.swapaxes(0, 1).reshape(B, T, HQ, hd)


def short_conv(x, w):
    K = w.shape[0]
    L = x.shape[1]
    pad = K // 2
    xp = jnp.pad(x, ((0, 0), (pad, pad), (0, 0)))
    return sum(xp[:, i:i + L] * w[i] for i in range(K))


def hyena_filter_bank(L, w1, b1, w2, b2, w3, freq):
    f32 = jnp.float32
    t = jnp.linspace(0.0, 1.0, L, dtype=f32)[:, None]
    bands = (HY_EMB - 1) // 2
    omega = 2.0 * math.pi * jnp.arange(L, dtype=f32)[:, None] / L
    fb = jnp.linspace(1e-4, bands - 1, bands, dtype=f32)[None, :]
    feats = jnp.concatenate([t, jnp.cos(fb * omega), -jnp.sin(fb * omega)], axis=-1)
    fr = freq.astype(f32)
    h = jnp.sin(fr * (feats @ w1.astype(f32) + b1.astype(f32)))
    h = jnp.sin(fr * (h @ w2.astype(f32) + b2.astype(f32)))
    h = (h @ w3.astype(f32)).reshape(L, 2, HY_ORDER, HY_CH)
    max_decay = math.log(HY_DECAY_TARGET) / HY_FAST_DECAY
    min_decay = math.log(HY_DECAY_TARGET) / HY_SLOW_DECAY
    deltas = jnp.abs(jnp.linspace(min_decay, max_decay, HY_CH, dtype=f32))
    h = h * (jnp.exp(-t * deltas) + HY_SHIFT)[:, None, None, :]
    taps = jnp.concatenate([h[:, 0], jnp.zeros((1, HY_ORDER, HY_CH), f32), h[:0:-1, 1]], axis=0)
    taps = taps / jnp.sum(jnp.abs(taps), axis=0, keepdims=True)
    return jnp.fft.rfft(taps, axis=0)


def hyena_mixer(z, conv_w, conv_b, w1, b1, w2, b2, w3, freq, skip):
    L = z.shape[1]
    zf = (short_conv(z, conv_w) + conv_b).astype(jnp.float32)
    v, x1, x2 = jnp.split(zf, 3, axis=-1)
    filt = hyena_filter_bank(L, w1, b1, w2, b2, w3, freq)
    skip = skip.astype(jnp.float32)

    def long_conv(u, o):
        y = jnp.fft.irfft(jnp.fft.rfft(u, n=2 * L, axis=1) * filt[None, :, o], n=2 * L, axis=1)[:, :L]
        return y + u * skip[o]

    y = x1 * long_conv(v, 0)
    y = x2 * long_conv(y, 1)
    return y.astype(z.dtype)


def l2norm(x):
    xf = x.astype(jnp.float32)
    return xf * lax.rsqrt(jnp.sum(xf * xf, axis=-1, keepdims=True) + EPS)


def chunk_gated_delta(q, k, v, g, beta, s0):
    B, L, H, dk = q.shape
    dv = v.shape[-1]
    n = L // C_CHUNK

    def chunks(x):
        x = x.reshape((B, n, C_CHUNK, H) + x.shape[3:])
        return jnp.moveaxis(jnp.moveaxis(x, 1, 0), 3, 2)

    qc, kc, vc, bc = chunks(q), chunks(k), chunks(v), chunks(beta)
    gc = jnp.cumsum(chunks(g), axis=-1)
    tri = jnp.tril(jnp.ones((C_CHUNK, C_CHUNK), bool))
    strict = jnp.tril(jnp.ones((C_CHUNK, C_CHUNK), bool), k=-1)
    gamma = jnp.exp(jnp.where(tri, gc[..., :, None] - gc[..., None, :], NEG_INF))
    kb = kc * bc[..., None]
    a_mat = jnp.where(strict, jnp.einsum('nbhid,nbhjd->nbhij', kb, kc) * gamma, 0.0) + jnp.eye(C_CHUNK, dtype=jnp.float32)
    rhs = jnp.concatenate([vc * bc[..., None], kb * jnp.exp(gc)[..., None]], axis=-1)
    sol = lax.linalg.triangular_solve(a_mat, rhs, left_side=True, lower=True)
    u, w = sol[..., :dv], sol[..., dv:]
    attn = jnp.where(tri, jnp.einsum('nbhid,nbhjd->nbhij', qc, kc) * gamma, 0.0)
    g_last = gc[..., -1]
    q_dec = qc * jnp.exp(gc)[..., None]
    k_dec = kc * jnp.exp(g_last[..., None] - gc)[..., None]

    def step(S, xs):
        u_i, w_i, a_i, qd, kd, gl = xs
        v_new = u_i - jnp.einsum('bhck,bhkv->bhcv', w_i, S)
        o = jnp.einsum('bhck,bhkv->bhcv', qd, S) + jnp.einsum('bhij,bhjv->bhiv', a_i, v_new)
        S = S * jnp.exp(gl)[..., None, None] + jnp.einsum('bhck,bhcv->bhkv', kd, v_new)
        return S, o

    S, o = lax.scan(step, s0.astype(jnp.float32), (u, w, attn, q_dec, k_dec, g_last))
    o = jnp.moveaxis(jnp.moveaxis(o, 2, 3), 0, 1).reshape(B, L, H, dv)
    return o, S


def deltanet_mixer(zq, zk, zv, zg, za, zb, conv_w, a_log, dt_bias, norm_w, s0):
    B, L, _ = zq.shape
    qkv = jax.nn.silu(short_conv(jnp.concatenate([zq, zk, zv], axis=-1), conv_w))
    q, k, v = [t.reshape(B, L, C_HEADS, HEAD_DIM) for t in jnp.split(qkv, 3, axis=-1)]
    q = l2norm(q) * (HEAD_DIM ** -0.5)
    k = l2norm(k)
    v = v.astype(jnp.float32)
    beta = jax.nn.sigmoid(zb.astype(jnp.float32))
    g = -jnp.exp(a_log.astype(jnp.float32)) * jax.nn.softplus(za.astype(jnp.float32) + dt_bias.astype(jnp.float32))
    o_f, s_f = chunk_gated_delta(q, k, v, g[:, :, 0], beta[:, :, 0], s0[:, 0])
    o_b, s_b = chunk_gated_delta(q[:, ::-1], k[:, ::-1], v[:, ::-1], g[:, ::-1, 1], beta[:, ::-1, 1], s0[:, 1])
    o = o_f + o_b[:, ::-1]
    gate = jax.nn.silu(zg.reshape(B, L, C_HEADS, HEAD_DIM).astype(jnp.float32))
    o = rms_norm(o, norm_w) * gate
    return o.reshape(B, L, C_W).astype(zq.dtype), jnp.stack([s_f, s_b], axis=1)


def neighbourhood_attention(q, k, v, ck, cv, rpb):
    B, T, H, hd = q.shape
    rows = T // GRID_W
    kh = min(NA_KH_MAX, rows)
    scale = hd ** -0.5
    r = jnp.arange(rows)
    rs = jnp.clip(r - kh // 2, 0, rows - kh)
    key_rows = rs[:, None] + jnp.arange(kh)[None, :]
    idx = (key_rows[:, :, None] * GRID_W + jnp.arange(GRID_W)).reshape(rows, kh * GRID_W)
    col = jnp.arange(GRID_W)
    cs = jnp.clip(col - NA_KW // 2, 0, GRID_W - NA_KW)
    kcol = jnp.tile(col, kh)
    col_ok = (kcol[None, :] >= cs[:, None]) & (kcol[None, :] < cs[:, None] + NA_KW)
    roff = jnp.repeat(key_rows - r[:, None], GRID_W, axis=1) + NA_KH_MAX - 1
    coff = jnp.clip(kcol[None, :] - col[:, None] + NA_KW - 1, 0, 2 * NA_KW - 2)
    qr = q.reshape(B, rows, GRID_W, H, hd).swapaxes(0, 1)
    rpb_f = rpb.astype(jnp.float32)

    def one(xs):
        qi, ii, ro = xs
        ki = k[:, ii]
        vi = v[:, ii]
        bias = rpb_f[:, ro[None, :], coff]
        s_loc = jnp.einsum('bqhd,bkhd->bhqk', qi, ki, preferred_element_type=jnp.float32) * scale + bias[None]
        s_loc = jnp.where(col_ok[None, None], s_loc, NEG_INF)
        s_ctx = jnp.einsum('bqhd,bphd->bhqp', qi, ck, preferred_element_type=jnp.float32) * scale
        p_loc, p_ctx = softmax_parts([s_loc, s_ctx])
        return (jnp.einsum('bhqk,bkhd->bqhd', p_loc.astype(vi.dtype), vi)
                + jnp.einsum('bhqp,bphd->bqhd', p_ctx.astype(cv.dtype), cv))

    o = lax.map(one, (qr, idx, roff))
    return o.swapaxes(0, 1).reshape(B, T, H, hd)


def expert_choice_ffn(h, w_router, w_gate, w_up, w_down):
    B, T, D = h.shape
    cap = EC_CAPACITY * T // N_EXPERTS
    aff = jax.nn.softmax(jnp.einsum('btd,de->bte', h, w_router, preferred_element_type=jnp.float32), axis=-1)
    gate, idx = lax.top_k(aff.swapaxes(1, 2), cap)
    xg = jax.vmap(lambda hb, ib: hb[ib])(h, idx)
    a = jnp.einsum('becd,edf->becf', xg, w_gate)
    u = jnp.einsum('becd,edf->becf', xg, w_up)
    y = jnp.einsum('becf,efd->becd', jax.nn.silu(a) * u, w_down) * gate[..., None].astype(h.dtype)
    return jax.vmap(lambda yb, ib: jnp.zeros((T, D), yb.dtype).at[ib.reshape(-1)].add(yb.reshape(-1, D)))(y, idx)


def split_even(z):
    B, L = z.shape[:2]
    q = z[..., :A_Q_W].reshape(B, L, A_HEADS, HEAD_DIM)
    k = z[..., A_Q_W:A_Q_W + A_KV_W].reshape(B, L, A_KV_HEADS, HEAD_DIM)
    v = z[..., A_Q_W + A_KV_W:A_Q_W + 2 * A_KV_W].reshape(B, L, A_KV_HEADS, HEAD_DIM)
    return q, k, v, z[..., A_Q_W + 2 * A_KV_W:]


def split_odd(z):
    B, L = z.shape[:2]
    zq, zk, zv, zg = [z[..., i * C_W:(i + 1) * C_W] for i in range(4)]
    off = 4 * C_W
    za = z[..., off:off + 2 * C_HEADS].reshape(B, L, 2, C_HEADS)
    zb = z[..., off + 2 * C_HEADS:off + 4 * C_HEADS].reshape(B, L, 2, C_HEADS)
    off = off + 4 * C_HEADS
    nq, nk, nv = [z[..., off + i * D_W:off + (i + 1) * D_W].reshape(B, L, D_HEADS, HEAD_DIM) for i in range(3)]
    return zq, zk, zv, zg, za, zb, nq, nk, nv


def kernel(x_prompt, x_sample, cache_attn_k, cache_attn_v, state_delta, cache_na_k, cache_na_v,
           c, c_ctx, w_ada, b_ada, norm_mix, norm_ffn, norm_final,
           even_w_in, even_w_out, attn_sink, hy_conv_w, hy_conv_b, hy_w1, hy_b1, hy_w2, hy_b2,
           hy_w3, hy_freq, hy_skip, odd_w_in, odd_w_out, gdn_conv_w, gdn_a_log, gdn_dt_bias,
           gdn_norm, na_rpb, moe_router, moe_w_gate, moe_w_up, moe_w_down):
    xp, xs = x_prompt, x_sample
    bp = xp.shape[0]
    dft_p, dft_s = dft_tables(xp.shape[1]), dft_tables(xs.shape[1])
    ada = ada_params_all(jnp.concatenate([c_ctx[None, :], c], axis=0), w_ada, b_ada)
    new_ak, new_av, new_st, new_nk, new_nv = [], [], [], [], []
    for l in range(DEPTH):
        j = l // 2
        mp = jnp.split(ada[l, :1, None, :], 6, axis=-1)
        ms = jnp.split(ada[l, 1:, None, :], 6, axis=-1)
        mod_p = (norm_mix[l], mp[0], mp[1])
        mod_s = (norm_mix[l], ms[0], ms[1])
        if l % 2 == 0:
            def hyena(zh, tables):
                taps = hyena_taps(zh.shape[1], hy_w1[j], hy_b1[j], hy_w2[j], hy_b2[j], hy_w3[j], hy_freq[j])
                return hyena_mixer_pallas(zh, hy_conv_w[j], hy_conv_b[j], hy_skip[j], tables, hyena_spectrum(tables, *taps))

            w_in = even_w_in[j]
            w_groups = [w_in[:, :A_Q_W], w_in[:, A_Q_W:A_Q_W + A_KV_W], w_in[:, A_Q_W + A_KV_W:A_Q_W + 2 * A_KV_W],
                        w_in[:, A_Q_W + 2 * A_KV_W:]]
            q, k, v, zh = proj_multi(xp, *mod_p, w_groups, [F32] * 4)
            oa = context_attention_pallas(q, k, v, attn_sink[j], A_HEADS, A_KV_HEADS)
            xp_new = proj_concat(oa, hyena(zh, dft_p), even_w_out[j], xp, mp[2])
            new_ak.append(k.reshape(bp, SEQ, A_KV_HEADS, HEAD_DIM))
            new_av.append(v.reshape(bp, SEQ, A_KV_HEADS, HEAD_DIM))
            q, k, v, zh = proj_multi(xs, *mod_s, w_groups, [F32] * 4)
            ck = cache_attn_k[:, j].reshape(DEC_BATCH, PAST_LEN, A_KV_W).astype(BF16)
            cv = cache_attn_v[:, j].reshape(DEC_BATCH, PAST_LEN, A_KV_W).astype(BF16)
            oa = window_attention_pallas(q, k, v, ck, cv, attn_sink[j])
            xs_new = proj_concat(oa, hyena(zh, dft_s), even_w_out[j], xs, ms[2])
        else:
            w_in = odd_w_in[j]
            ab0 = 4 * C_W
            ab_cols = [w_in[:, ab0 + o * C_HEADS:ab0 + (o + 1) * C_HEADS] for o in (0, 2, 1, 3)]
            w_ab = jnp.concatenate(ab_cols + [jnp.zeros((D_MODEL, LANES - 4 * C_HEADS), w_in.dtype)], axis=1)
            n0 = ab0 + 4 * C_HEADS
            w_groups = [w_in[:, :3 * C_W], w_in[:, 3 * C_W:4 * C_W], w_ab,
                        w_in[:, n0:n0 + D_W], w_in[:, n0 + D_W:n0 + 2 * D_W], w_in[:, n0 + 2 * D_W:]]

            def deltanet(zqkv, zab, zg, s0):
                qd, kd, vd, gb = gdn_prep(zqkv, zab, gdn_conv_w[j], gdn_a_log[j], gdn_dt_bias[j])
                return gdn_scan(qd, kd, vd, gb, zg, s0, gdn_norm[j])

            zqkv, zg, zab, nq, nk, nv = proj_multi(xp, *mod_p, w_groups, [F32] * 6)
            oc, st = deltanet(zqkv, zab, zg, jnp.zeros((bp, 2, C_HEADS, HEAD_DIM, HEAD_DIM), F32))
            od = context_attention_pallas(nq, nk, nv, None, D_HEADS, D_HEADS)
            xp_new = proj_concat(oc, od, odd_w_out[j], xp, mp[2])
            new_st.append(st)
            new_nk.append(nk.reshape(bp, SEQ, D_HEADS, HEAD_DIM))
            new_nv.append(nv.reshape(bp, SEQ, D_HEADS, HEAD_DIM))
            zqkv, zg, zab, nq, nk, nv = proj_multi(xs, *mod_s, w_groups, [F32, F32, F32, BF16, BF16, BF16])
            oc, _ = deltanet(zqkv, zab, zg, state_delta[:, j])
            ck = cache_na_k[:, j].reshape(DEC_BATCH, PAST_LEN, D_W).astype(BF16)
            cv = cache_na_v[:, j].reshape(DEC_BATCH, PAST_LEN, D_W).astype(BF16)
            od = neighbourhood_attention_pallas(nq, nk, nv, ck, cv, na_rpb[j])
            xs_new = proj_concat(oc, od, odd_w_out[j], xs, ms[2])
        xp, xs = xp_new, xs_new
        moe = (moe_router[l], moe_w_gate, moe_w_up, moe_w_down, l)
        xp = moe_block(xp, norm_ffn[l], mp[3], mp[4], mp[5], *moe)
        xs = moe_block(xs, norm_ffn[l], ms[3], ms[4], ms[5], *moe)
    y_prompt = final_norm(xp, norm_final)
    y_sample = final_norm(xs, norm_final)
    return (y_prompt, y_sample, jnp.stack(new_ak, axis=1), jnp.stack(new_av, axis=1), jnp.stack(new_st, axis=1),
            jnp.stack(new_nk, axis=1), jnp.stack(new_nv, axis=1))
```

```python
import functools
import math
import jax, jax.numpy as jnp
from jax import lax
import numpy as np
from jax.experimental import pallas as pl
from jax.experimental.pallas import tpu as pltpu

D_MODEL = 1024
BATCH = 32
SEQ = 256
DEPTH = 4
DEC_BATCH = 4
DEC_SEQ = 4096
PAST_LEN = 512

GRID_W = 64
HEAD_DIM = 64
N_EVEN = (DEPTH + 1) // 2
N_ODD = DEPTH // 2
Q_BLOCK = 128
A_HEADS = D_MODEL // 128
A_KV_HEADS = A_HEADS // 4
A_WINDOW = 128
A_BLOCK = 128
ROPE_BASE = 10000.0
HY_CH = D_MODEL // 2
HY_ORDER = 2
HY_SHORT = 3
HY_EMB = 33
HY_FILT_W = 64
HY_FAST_DECAY = 0.3
HY_SLOW_DECAY = 1.5
HY_DECAY_TARGET = 1e-2
HY_SHIFT = 0.05
C_HEADS = D_MODEL // 128
C_SHORT = 3
C_CHUNK = 64
D_HEADS = D_MODEL // 128
NA_KH_MAX = 8
NA_KW = 16
N_EXPERTS = 16
EC_CAPACITY = 2
MOE_D_FF = D_MODEL
EPS = 1e-6
NEG_INF = -1e30

A_Q_W = A_HEADS * HEAD_DIM
A_KV_W = A_KV_HEADS * HEAD_DIM
EVEN_IN = A_Q_W + 2 * A_KV_W + 3 * HY_CH
EVEN_MIX = A_Q_W + HY_CH
C_W = C_HEADS * HEAD_DIM
D_W = D_HEADS * HEAD_DIM
ODD_IN = 4 * C_W + 4 * C_HEADS + 3 * D_W
ODD_MIX = C_W + D_W

VMEM_LIMIT_BYTES = 48 * 1024 * 1024


def _mm_kernel(x_ref, w_ref, o_ref):
    o_ref[...] = jnp.dot(x_ref[...].astype(jnp.bfloat16), w_ref[...], preferred_element_type=jnp.float32)


def pallas_matmul(x, w, tm=256):
    M, K = x.shape
    N = w.shape[1]
    assert M % tm == 0
    return pl.pallas_call(
        _mm_kernel,
        grid=(M // tm,),
        in_specs=[pl.BlockSpec((tm, K), lambda i: (i, 0)), pl.BlockSpec((K, N), lambda i: (0, 0))],
        out_specs=pl.BlockSpec((tm, N), lambda i: (i, 0)),
        out_shape=jax.ShapeDtypeStruct((M, N), jnp.float32),
        compiler_params=pltpu.CompilerParams(dimension_semantics=("arbitrary",), vmem_limit_bytes=VMEM_LIMIT_BYTES),
    )(x, w.astype(jnp.bfloat16))


def proj(x, w):
    B, L, K = x.shape
    return pallas_matmul(x.reshape(B * L, K), w).reshape(B, L, w.shape[1])


def _request_of_tile(tm, L, per_request):
    return (lambda i: ((i * tm) // L, 0, 0)) if per_request else (lambda i: (0, 0, 0))


def _mm_multi_kernel(x_ref, g_ref, shift_ref, scale_ref, *refs):
    n = len(refs) // 2
    x = x_ref[...]
    y = x * lax.rsqrt(jnp.mean(x * x, axis=-1, keepdims=True) + EPS) * g_ref[...]
    h = (y * (1.0 + scale_ref[0]) + shift_ref[0]).astype(jnp.bfloat16)
    for w_ref, o_ref in zip(refs[:n], refs[n:]):
        o_ref[...] = jnp.dot(h, w_ref[...], preferred_element_type=jnp.float32).astype(o_ref.dtype)


def proj_multi(x, g, shift, scale, weights, out_dtypes, tm=256):
    B, L, K = x.shape
    M = B * L
    assert L % tm == 0
    mod_spec = pl.BlockSpec((1, 1, K), _request_of_tile(tm, L, shift.shape[0] == B))
    outs = pl.pallas_call(
        _mm_multi_kernel,
        grid=(M // tm,),
        in_specs=[pl.BlockSpec((tm, K), lambda i: (i, 0)), pl.BlockSpec((1, K), lambda i: (0, 0)), mod_spec, mod_spec]
        + [pl.BlockSpec(w.shape, lambda i: (0, 0)) for w in weights],
        out_specs=[pl.BlockSpec((tm, w.shape[1]), lambda i: (i, 0)) for w in weights],
        out_shape=[jax.ShapeDtypeStruct((M, w.shape[1]), dt) for w, dt in zip(weights, out_dtypes)],
        compiler_params=pltpu.CompilerParams(dimension_semantics=("arbitrary",), vmem_limit_bytes=VMEM_LIMIT_BYTES),
        name="in_projection",
    )(x.reshape(M, K), g.astype(jnp.float32)[None, :], shift, scale, *[w.astype(jnp.bfloat16) for w in weights])
    return [o.reshape(B, L, o.shape[1]) for o in outs]


def _mm2_kernel(a_ref, b_ref, wa_ref, wb_ref, x_ref, gate_ref, o_ref):
    mix = (jnp.dot(a_ref[...].astype(jnp.bfloat16), wa_ref[...], preferred_element_type=jnp.float32)
           + jnp.dot(b_ref[...].astype(jnp.bfloat16), wb_ref[...], preferred_element_type=jnp.float32))
    o_ref[...] = x_ref[...] + gate_ref[0] * mix


def proj_concat(a, b, w, x, gate, tm=256):
    B, L, Ka = a.shape
    Kb = b.shape[2]
    N = w.shape[1]
    M = B * L
    assert L % tm == 0 and w.shape[0] == Ka + Kb
    wb16 = w.astype(jnp.bfloat16)
    out = pl.pallas_call(
        _mm2_kernel,
        grid=(M // tm,),
        in_specs=[pl.BlockSpec((tm, Ka), lambda i: (i, 0)), pl.BlockSpec((tm, Kb), lambda i: (i, 0)),
                  pl.BlockSpec((Ka, N), lambda i: (0, 0)), pl.BlockSpec((Kb, N), lambda i: (0, 0)),
                  pl.BlockSpec((tm, N), lambda i: (i, 0)),
                  pl.BlockSpec((1, 1, N), _request_of_tile(tm, L, gate.shape[0] == B))],
        out_specs=pl.BlockSpec((tm, N), lambda i: (i, 0)),
        out_shape=jax.ShapeDtypeStruct((M, N), jnp.float32),
        compiler_params=pltpu.CompilerParams(dimension_semantics=("arbitrary",), vmem_limit_bytes=VMEM_LIMIT_BYTES),
        name="out_projection",
    )(a.reshape(M, Ka), b.reshape(M, Kb), wb16[:Ka], wb16[Ka:], x.reshape(M, N), gate)
    return out.reshape(B, L, N)


LANES = 128
BF16 = jnp.bfloat16
F32 = jnp.float32


def _dot_nt(a, b):
    return lax.dot_general(a, b, (((1,), (1,)), ((), ())), preferred_element_type=F32)


def _dot(a, b):
    return jnp.dot(a, b, preferred_element_type=F32)


def _low_half(shape):
    return lax.broadcasted_iota(jnp.int32, shape, 1) < HEAD_DIM


def _softmax_pv(scores, values, extra_logit=None):
    m = scores[0].max(axis=-1, keepdims=True)
    for s in scores[1:]:
        m = jnp.maximum(m, s.max(axis=-1, keepdims=True))
    if extra_logit is not None:
        m = jnp.maximum(m, extra_logit)
    l = None if extra_logit is None else jnp.exp(extra_logit - m)
    o = None
    for s, v in zip(scores, values):
        p = jnp.exp(s - m)
        ps = p.sum(axis=-1, keepdims=True)
        l = ps if l is None else l + ps
        pv = _dot(p.astype(BF16), v)
        o = pv if o is None else o + pv
    return o / l


def _place_head(q_slab, src_half, dst_half, low):
    x = q_slab if src_half == dst_half else pltpu.roll(q_slab, HEAD_DIM, axis=1)
    return jnp.where(low if dst_half == 0 else ~low, x, jnp.zeros_like(x))


NA_KEYS = NA_KH_MAX * GRID_W


def _na_kernel(q_ref, k_ref, v_ref, ck_ref, cv_ref, tab_ref, o_ref):
    r = pl.program_id(1)
    rows = k_ref.shape[1] // GRID_W
    rs = jnp.clip(r - NA_KH_MAX // 2, 0, rows - NA_KH_MAX)
    start = pl.multiple_of(rs * GRID_W, GRID_W)
    scale = HEAD_DIM ** -0.5
    low = _low_half((GRID_W, LANES))
    for p in range(D_HEADS // 2):
        cols = slice(p * LANES, (p + 1) * LANES)
        qp = q_ref[0, :, cols]
        kp = k_ref[0, pl.ds(start, NA_KEYS), cols]
        vp = v_ref[0, pl.ds(start, NA_KEYS), cols]
        ckp = ck_ref[0, :, cols]
        cvp = cv_ref[0, :, cols]
        outs = []
        for half in range(2):
            qm = jnp.where(low if half == 0 else ~low, qp, jnp.zeros_like(qp))
            s_loc = _dot_nt(qm, kp) * scale + tab_ref[2 * p + half, 0]
            s_ctx = _dot_nt(qm, ckp) * scale
            outs.append(_softmax_pv([s_loc, s_ctx], [vp, cvp]))
        o_ref[0, :, cols] = jnp.where(low, outs[0], outs[1]).astype(o_ref.dtype)


def na_bias_table(rpb):
    col = jnp.arange(GRID_W)
    cs = jnp.clip(col - NA_KW // 2, 0, GRID_W - NA_KW)
    col_ok = (col[None, :] >= cs[:, None]) & (col[None, :] < cs[:, None] + NA_KW)
    coff = jnp.clip(col[None, :] - col[:, None] + NA_KW - 1, 0, 2 * NA_KW - 2)
    base = jnp.where(col_ok[None, None], rpb.astype(F32)[:, :, coff], NEG_INF)
    tab = jnp.stack([base[:, o:o + NA_KH_MAX] for o in range(NA_KH_MAX)], axis=1)
    return tab.transpose(0, 1, 3, 2, 4).reshape(D_HEADS, NA_KH_MAX, GRID_W, NA_KEYS)


def neighbourhood_attention_pallas(q, k, v, ck, cv, rpb):
    B, T, W = q.shape
    P = ck.shape[1]
    rows = T // GRID_W
    assert rows >= NA_KH_MAX and W == D_W
    tab = na_bias_table(rpb)

    def tab_index(b, r):
        rs = jnp.clip(r - NA_KH_MAX // 2, 0, rows - NA_KH_MAX)
        return (0, rs - r + NA_KH_MAX - 1, 0, 0)

    return pl.pallas_call(
        _na_kernel,
        grid=(B, rows),
        in_specs=[
            pl.BlockSpec((1, GRID_W, W), lambda b, r: (b, r, 0)),
            pl.BlockSpec((1, T, W), lambda b, r: (b, 0, 0)),
            pl.BlockSpec((1, T, W), lambda b, r: (b, 0, 0)),
            pl.BlockSpec((1, P, W), lambda b, r: (b, 0, 0)),
            pl.BlockSpec((1, P, W), lambda b, r: (b, 0, 0)),
            pl.BlockSpec((D_HEADS, 1, GRID_W, NA_KEYS), tab_index),
        ],
        out_specs=pl.BlockSpec((1, GRID_W, W), lambda b, r: (b, r, 0)),
        out_shape=jax.ShapeDtypeStruct((B, T, W), BF16),
        compiler_params=pltpu.CompilerParams(dimension_semantics=("arbitrary", "arbitrary"),
                                             vmem_limit_bytes=VMEM_LIMIT_BYTES),
        name="na_attention",
    )(q, k, v, ck, cv, tab)


def rope_tables(T):
    cos, sin = axial_rope(T)
    cos, sin = cos[:, 0, :], sin[:, 0, :]
    cos_t = jnp.concatenate([cos, cos, cos, cos], axis=-1)
    sin_t = jnp.concatenate([-sin, sin, -sin, sin], axis=-1)
    return cos_t, sin_t


def _rope(x, cos_t, sin_t):
    half = HEAD_DIM // 2
    lane = lax.broadcasted_iota(jnp.int32, x.shape, 1)
    first = (lane % HEAD_DIM) < half
    swapped = jnp.where(first, pltpu.roll(x, LANES - half, axis=1), pltpu.roll(x, half, axis=1))
    return x * cos_t + swapped * sin_t


def _win_kernel(sink_ref, q_ref, k_ref, v_ref, ck_ref, cv_ref, cos_ref, sin_ref, o_ref):
    i = pl.program_id(1)
    T = k_ref.shape[1]
    span = 3 * A_BLOCK
    start = pl.multiple_of(jnp.clip((i - 1) * A_BLOCK, 0, T - span), A_BLOCK)
    delta = i * A_BLOCK - start
    q0 = pl.multiple_of(i * A_BLOCK, A_BLOCK)
    scale = HEAD_DIM ** -0.5
    kw = _rope(k_ref[0, pl.ds(start, span), :], cos_ref[pl.ds(start, span), :], sin_ref[pl.ds(start, span), :]).astype(BF16)
    vw = v_ref[0, pl.ds(start, span), :].astype(BF16)
    ck = ck_ref[0]
    cv = cv_ref[0]
    cos_q = cos_ref[pl.ds(q0, A_BLOCK), :]
    sin_q = sin_ref[pl.ds(q0, A_BLOCK), :]
    qi = lax.broadcasted_iota(jnp.int32, (A_BLOCK, span), 0)
    kj = lax.broadcasted_iota(jnp.int32, (A_BLOCK, span), 1)
    band = jnp.abs(kj - delta - qi) <= A_WINDOW
    low = _low_half((A_BLOCK, LANES))
    group = A_HEADS // A_KV_HEADS
    for p in range(A_HEADS // 2):
        cols = slice(p * LANES, (p + 1) * LANES)
        q_slab = _rope(q_ref[0, :, cols], cos_q, sin_q)
        outs = []
        for half in range(2):
            h = 2 * p + half
            kv = h // group
            qm = _place_head(q_slab, half, kv, low).astype(BF16)
            s_loc = jnp.where(band, _dot_nt(qm, kw) * scale, NEG_INF)
            s_ctx = _dot_nt(qm, ck) * scale
            o = _softmax_pv([s_loc, s_ctx], [vw, cv], extra_logit=sink_ref[h])
            outs.append(o if kv == half else pltpu.roll(o, HEAD_DIM, axis=1))
        o_ref[0, :, cols] = jnp.where(low, outs[0], outs[1]).astype(o_ref.dtype)


def window_attention_pallas(q, k, v, ck, cv, sink):
    B, T, QW = q.shape
    KW = k.shape[2]
    P = ck.shape[1]
    assert KW == LANES and QW == A_Q_W and T % A_BLOCK == 0 and T >= 3 * A_BLOCK
    cos_t, sin_t = rope_tables(T)
    return pl.pallas_call(
        _win_kernel,
        grid=(B, T // A_BLOCK),
        in_specs=[
            pl.BlockSpec(memory_space=pltpu.SMEM),
            pl.BlockSpec((1, A_BLOCK, QW), lambda b, i: (b, i, 0)),
            pl.BlockSpec((1, T, KW), lambda b, i: (b, 0, 0)),
            pl.BlockSpec((1, T, KW), lambda b, i: (b, 0, 0)),
            pl.BlockSpec((1, P, KW), lambda b, i: (b, 0, 0)),
            pl.BlockSpec((1, P, KW), lambda b, i: (b, 0, 0)),
            pl.BlockSpec((T, LANES), lambda b, i: (0, 0)),
            pl.BlockSpec((T, LANES), lambda b, i: (0, 0)),
        ],
        out_specs=pl.BlockSpec((1, A_BLOCK, QW), lambda b, i: (b, i, 0)),
        out_shape=jax.ShapeDtypeStruct((B, T, QW), BF16),
        compiler_params=pltpu.CompilerParams(dimension_semantics=("arbitrary", "arbitrary"),
                                             vmem_limit_bytes=VMEM_LIMIT_BYTES),
        name="window_attention",
    )(sink.astype(F32), q, k, v, ck, cv, cos_t, sin_t)


def _ctx_kernel(sink_ref, q_ref, k_ref, v_ref, o_ref, *, n_q_heads, n_kv_heads, use_sink):
    S = q_ref.shape[1]
    scale = HEAD_DIM ** -0.5
    low = _low_half((S, LANES))
    group = n_q_heads // n_kv_heads
    for p in range(n_q_heads // 2):
        cols = slice(p * LANES, (p + 1) * LANES)
        q_slab = q_ref[0, :, cols]
        outs = []
        for half in range(2):
            h = 2 * p + half
            kv = h // group
            kcols = slice((kv // 2) * LANES, (kv // 2 + 1) * LANES)
            qm = _place_head(q_slab, half, kv % 2, low).astype(BF16)
            s = _dot_nt(qm, k_ref[0, :, kcols].astype(BF16)) * scale
            o = _softmax_pv([s], [v_ref[0, :, kcols].astype(BF16)], extra_logit=sink_ref[h] if use_sink else None)
            outs.append(o if kv % 2 == half else pltpu.roll(o, HEAD_DIM, axis=1))
        o_ref[0, :, cols] = jnp.where(low, outs[0], outs[1]).astype(o_ref.dtype)


def context_attention_pallas(q, k, v, sink, n_q_heads, n_kv_heads):
    B, S, QW = q.shape
    KW = k.shape[2]
    use_sink = sink is not None
    sink_arr = sink.astype(F32) if use_sink else jnp.zeros((n_q_heads,), F32)
    return pl.pallas_call(
        functools.partial(_ctx_kernel, n_q_heads=n_q_heads, n_kv_heads=n_kv_heads, use_sink=use_sink),
        grid=(B,),
        in_specs=[
            pl.BlockSpec(memory_space=pltpu.SMEM),
            pl.BlockSpec((1, S, QW), lambda b: (b, 0, 0)),
            pl.BlockSpec((1, S, KW), lambda b: (b, 0, 0)),
            pl.BlockSpec((1, S, KW), lambda b: (b, 0, 0)),
        ],
        out_specs=pl.BlockSpec((1, S, QW), lambda b: (b, 0, 0)),
        out_shape=jax.ShapeDtypeStruct((B, S, QW), BF16),
        compiler_params=pltpu.CompilerParams(dimension_semantics=("arbitrary",), vmem_limit_bytes=VMEM_LIMIT_BYTES),
        name="context_attention",
    )(sink_arr, q, k, v)


HIGHEST = lax.Precision.HIGHEST
GDN_TM = 256
GDN_REQS = 2
SUBLANES = 8


def _head_pair_sum_matrix():
    a = lax.broadcasted_iota(jnp.int32, (LANES, LANES), 0) // HEAD_DIM
    b = lax.broadcasted_iota(jnp.int32, (LANES, LANES), 1) // HEAD_DIM
    return (a == b).astype(F32)


def _gdn_prep_kernel(x_ref, prev_ref, next_ref, ab_ref, cw_ref, a_ref, dtb_ref, q_ref, k_ref, v_ref, gb_ref):
    i = pl.program_id(1)
    n = pl.num_programs(1)
    x = x_ref[0]
    tm = x.shape[0]
    row = lax.broadcasted_iota(jnp.int32, x.shape, 0)
    prev_row = jnp.where(i > 0, prev_ref[0, SUBLANES - 1:SUBLANES, :], 0.0)
    next_row = jnp.where(i < n - 1, next_ref[0, 0:1, :], 0.0)
    x_prev = jnp.where(row == 0, prev_row, pltpu.roll(x, 1, axis=0))
    x_next = jnp.where(row == tm - 1, next_row, pltpu.roll(x, tm - 1, axis=0))
    y = x_prev * cw_ref[0:1, :] + x * cw_ref[1:2, :] + x_next * cw_ref[2:3, :]
    y = y * jax.nn.sigmoid(y)
    pmat = _head_pair_sum_matrix()
    for p in range(C_W // LANES):
        qs = y[:, p * LANES:(p + 1) * LANES]
        ks = y[:, C_W + p * LANES:C_W + (p + 1) * LANES]
        q_ref[0, :, p * LANES:(p + 1) * LANES] = qs * lax.rsqrt(jnp.dot(qs * qs, pmat, precision=HIGHEST, preferred_element_type=F32) + EPS) * (HEAD_DIM ** -0.5)
        k_ref[0, :, p * LANES:(p + 1) * LANES] = ks * lax.rsqrt(jnp.dot(ks * ks, pmat, precision=HIGHEST, preferred_element_type=F32) + EPS)
    v_ref[0] = y[:, 2 * C_W:]
    ab = ab_ref[0, :, 0:4 * C_HEADS]
    lane = lax.broadcasted_iota(jnp.int32, ab.shape, 1)
    is_beta = (lane // C_HEADS) % 2 == 1
    t = ab + dtb_ref[...]
    softplus = jnp.maximum(t, 0.0) + jnp.log1p(jnp.exp(-jnp.abs(t)))
    gb = jnp.where(is_beta, jax.nn.sigmoid(ab), -jnp.exp(a_ref[...]) * softplus)
    gb_ref[0, 0] = gb[:, 0:2 * C_HEADS]
    gb_ref[0, 1] = gb[:, 2 * C_HEADS:4 * C_HEADS]


def gdn_prep(zqkv, zab, conv_w, a_log, dt_bias):
    B, L, W3 = zqkv.shape
    tm = GDN_TM
    assert L % tm == 0
    nb = tm // SUBLANES
    zero = jnp.zeros((C_HEADS,), F32)
    a_lane = jnp.concatenate([a_log[0], zero, a_log[1], zero]).astype(F32)[None, :]
    dtb_lane = jnp.concatenate([dt_bias[0], zero, dt_bias[1], zero]).astype(F32)[None, :]
    outs = pl.pallas_call(
        _gdn_prep_kernel,
        grid=(B, L // tm),
        in_specs=[
            pl.BlockSpec((1, tm, W3), lambda b, i: (b, i, 0)),
            pl.BlockSpec((1, SUBLANES, W3), lambda b, i: (b, jnp.maximum(i * nb - 1, 0), 0)),
            pl.BlockSpec((1, SUBLANES, W3), lambda b, i: (b, jnp.minimum((i + 1) * nb, L // SUBLANES - 1), 0)),
            pl.BlockSpec((1, tm, LANES), lambda b, i: (b, i, 0)),
            pl.BlockSpec((C_SHORT, W3), lambda b, i: (0, 0)),
            pl.BlockSpec((1, 4 * C_HEADS), lambda b, i: (0, 0)),
            pl.BlockSpec((1, 4 * C_HEADS), lambda b, i: (0, 0)),
        ],
        out_specs=[
            pl.BlockSpec((1, tm, C_W), lambda b, i: (b, i, 0)),
            pl.BlockSpec((1, tm, C_W), lambda b, i: (b, i, 0)),
            pl.BlockSpec((1, tm, C_W), lambda b, i: (b, i, 0)),
            pl.BlockSpec((1, 2, tm, 2 * C_HEADS), lambda b, i: (b, 0, i, 0)),
        ],
        out_shape=[jax.ShapeDtypeStruct((B, L, C_W), F32)] * 3 + [jax.ShapeDtypeStruct((B, 2, L, 2 * C_HEADS), F32)],
        compiler_params=pltpu.CompilerParams(dimension_semantics=("arbitrary", "arbitrary"),
                                             vmem_limit_bytes=VMEM_LIMIT_BYTES),
        name="gdn_prep",
    )(zqkv, zqkv, zqkv, zab, conv_w.astype(F32), a_lane, dtb_lane)
    return outs


def _gdn_kernel(q_ref, k_ref, v_ref, gb_ref, zg_ref, s0_ref, nw_ref, o_ref, st_ref, s_scr, of_scr):
    d = pl.program_id(1)
    c = pl.program_id(2)
    n = pl.num_programs(2)
    C = C_CHUNK
    fwd = d == 0
    chunk = jnp.where(fwd, c, n - 1 - c)
    r0 = pl.multiple_of(chunk * C, C)

    @pl.when(c == 0)
    def _():
        s_scr[...] = s0_ref[:, 0]

    row = lax.broadcasted_iota(jnp.int32, (C, C), 0)
    col = lax.broadcasted_iota(jnp.int32, (C, C), 1)
    ahead = jnp.where(fwd, row - col, col - row)
    incl = ahead >= 0
    strict = ahead > 0
    incl_f = incl.astype(F32)
    nreq = q_ref.shape[0]
    gc, gc_t, g_last, beta = [], [], [], []
    for r in range(nreq):
        g = gb_ref[r, 0, :, 0:C_HEADS]
        beta.append(gb_ref[r, 0, :, C_HEADS:2 * C_HEADS])
        gc.append(jnp.dot(incl_f, g, precision=HIGHEST, preferred_element_type=F32))
        gc_t.append(gc[r].T)
        g_last.append(jnp.where(fwd, gc[r][C - 1:C, :], gc[r][0:1, :]))
    units = [(r, h) for r in range(nreq) for h in range(C_HEADS)]
    H = range(len(units))
    heads = [slice(h * HEAD_DIM, (h + 1) * HEAD_DIM) for _, h in units]
    s_old = [s_scr[r, h] for r, h in units]
    qs = [q_ref[r, :, heads[i]] for i, (r, _) in enumerate(units)]
    ks = [k_ref[r, :, heads[i]] for i, (r, _) in enumerate(units)]
    vs = [v_ref[r, :, heads[i]] for i, (r, _) in enumerate(units)]
    gcol = [gc[r][:, h:h + 1] for r, h in units]
    bcol = [beta[r][:, h:h + 1] for r, h in units]
    gl = [g_last[r][:, h:h + 1] for r, h in units]
    gamma = [jnp.exp(jnp.where(incl, gcol[i] - gc_t[r][h:h + 1, :], NEG_INF)) for i, (r, h) in enumerate(units)]
    egc = [jnp.exp(gcol[h]) for h in H]
    kb = [ks[h].astype(BF16) for h in H]
    nmat = [jnp.where(strict, _dot_nt(kb[h], kb[h]) * gamma[h], 0.0) * bcol[h] for h in H]
    attn = [(_dot_nt(qs[h].astype(BF16), kb[h]) * gamma[h]).astype(BF16) for h in H]
    xr = row ^ col
    eye = (row == col).astype(F32)
    tinv = [eye - jnp.where((xr >> 1) == 0, nmat[h], 0.0) for h in H]
    for lvl in range(1, 6):
        off_diag = (xr >> lvl) == 1
        wmat = [_dot(jnp.where(off_diag, nmat[h], 0.0).astype(BF16), tinv[h].astype(BF16)).astype(BF16) for h in H]
        tinv = [tinv[h] - _dot(tinv[h].astype(BF16), wmat[h]) for h in H]
    x = [jnp.concatenate([vs[h] * bcol[h], ks[h] * (bcol[h] * egc[h])], axis=1) for h in H]
    x = [x[h] + _dot((tinv[h] - eye).astype(BF16), x[h].astype(BF16)) for h in H]
    sb = [s_old[h].astype(BF16) for h in H]
    v_new = [x[h][:, :HEAD_DIM] - _dot(x[h][:, HEAD_DIM:].astype(BF16), sb[h]) for h in H]
    vb = [v_new[h].astype(BF16) for h in H]
    outs = [_dot((qs[h] * egc[h]).astype(BF16), sb[h]) + _dot(attn[h], vb[h]) for h in H]
    kd = [(ks[h] * jnp.exp(gl[h] - gcol[h])).astype(BF16) for h in H]
    s_new = [s_old[h] * jnp.exp(gl[h]) + lax.dot_general(kd[h], vb[h], (((0,), (0,)), ((), ())), preferred_element_type=F32)
             for h in H]
    for i, (r, h) in enumerate(units):
        s_scr[r, h] = s_new[i]
    o = [jnp.concatenate(outs[r * C_HEADS:(r + 1) * C_HEADS], axis=1) for r in range(nreq)]

    @pl.when(fwd)
    def _():
        for r in range(nreq):
            of_scr[r, pl.ds(r0, C), :] = o[r]

    @pl.when(jnp.logical_not(fwd))
    def _():
        pmat = _head_pair_sum_matrix()
        for r in range(nreq):
            tot = of_scr[r, pl.ds(r0, C), :] + o[r]
            zg = zg_ref[r]
            gate = zg * jax.nn.sigmoid(zg)
            for p in range(C_W // LANES):
                cols = slice(p * LANES, (p + 1) * LANES)
                t = tot[:, cols]
                ms = jnp.dot(t * t, pmat, precision=HIGHEST, preferred_element_type=F32) * (1.0 / HEAD_DIM)
                o_ref[r, :, cols] = (t * lax.rsqrt(ms + EPS) * nw_ref[:, cols] * gate[:, cols]).astype(o_ref.dtype)

    @pl.when(c == n - 1)
    def _():
        st_ref[:, 0] = s_scr[...]


def gdn_scan(q, k, v, gb, zg, s0, norm_w):
    B, L, W = q.shape
    C = C_CHUNK
    n = L // C
    R = min(B, GDN_REQS)
    assert L % C == 0 and W == C_W and B % R == 0
    chunk_of = lambda d, c: jnp.where(d == 0, c, n - 1 - c)
    seq_spec = pl.BlockSpec((R, C, W), lambda b, d, c: (b, chunk_of(d, c), 0))
    state_spec = pl.BlockSpec((R, 1, C_HEADS, HEAD_DIM, HEAD_DIM), lambda b, d, c: (b, d, 0, 0, 0))
    nw = jnp.tile(norm_w.astype(F32), C_HEADS)[None, :]
    return pl.pallas_call(
        _gdn_kernel,
        grid=(B // R, 2, n),
        in_specs=[
            seq_spec, seq_spec, seq_spec,
            pl.BlockSpec((R, 1, C, 2 * C_HEADS), lambda b, d, c: (b, d, chunk_of(d, c), 0)),
            seq_spec,
            state_spec,
            pl.BlockSpec((1, W), lambda b, d, c: (0, 0)),
        ],
        out_specs=[
            pl.BlockSpec((R, C, W), lambda b, d, c: (b, jnp.where(d == 0, n - 1, n - 1 - c), 0)),
            state_spec,
        ],
        out_shape=[jax.ShapeDtypeStruct((B, L, W), BF16), jax.ShapeDtypeStruct(s0.shape, F32)],
        scratch_shapes=[pltpu.VMEM((R, C_HEADS, HEAD_DIM, HEAD_DIM), F32), pltpu.VMEM((R, L, W), F32)],
        compiler_params=pltpu.CompilerParams(dimension_semantics=("arbitrary", "arbitrary", "arbitrary"),
                                             vmem_limit_bytes=VMEM_LIMIT_BYTES),
        name="gdn_scan",
    )(q, k, v, gb, zg, s0.astype(F32), nw)


MOE_TT = 512
MOE_RT = 512


def _moe_router_kernel(x_ref, g_ref, shift_ref, scale_ref, wr_ref, h_ref, aff_ref):
    x = x_ref[0]
    y = x * lax.rsqrt(jnp.mean(x * x, axis=-1, keepdims=True) + EPS) * g_ref[...]
    h = y * (1.0 + scale_ref[0]) + shift_ref[0]
    h_ref[0] = h.astype(BF16)
    logits = jnp.dot(h, wr_ref[...], precision=HIGHEST, preferred_element_type=F32)
    lane = lax.broadcasted_iota(jnp.int32, logits.shape, 1)
    logits = jnp.where(lane < N_EXPERTS, logits, NEG_INF)
    e = jnp.exp(logits - logits.max(axis=-1, keepdims=True))
    aff_ref[0] = e / e.sum(axis=-1, keepdims=True)


def _moe_select_kernel(aff_ref, slot_ref, start_ref, *, cap):
    T = aff_ref.shape[1]
    bits = pltpu.bitcast(aff_ref[0], jnp.int32)

    def bisect(i, v):
        cand = v | (1 << (30 - i))
        cnt = jnp.sum((bits >= cand).astype(jnp.int32), axis=0, keepdims=True)
        return jnp.where(cnt >= cap, cand, v)

    thr = lax.fori_loop(0, 31, bisect, jnp.zeros((1, LANES), jnp.int32))
    gt = (bits > thr).astype(F32)
    eq = (bits == thr).astype(F32)
    need = cap - jnp.sum(gt, axis=0, keepdims=True)
    blk = min(T, MOE_TT)
    r = lax.broadcasted_iota(jnp.int32, (blk, blk), 0)
    c = lax.broadcasted_iota(jnp.int32, (blk, blk), 1)
    before = (c < r).astype(BF16)
    carry_gt = jnp.zeros((1, LANES), F32)
    carry_eq = jnp.zeros((1, LANES), F32)
    n_tiles = T // blk
    start_ref[0] = jnp.zeros(start_ref.shape[1:], F32)
    for b in range(n_tiles):
        rows = slice(b * blk, (b + 1) * blk)
        gt_b, eq_b = gt[rows], eq[rows]
        pos_gt = _dot(before, gt_b.astype(BF16)) + carry_gt
        pos_eq = _dot(before, eq_b.astype(BF16)) + carry_eq
        chosen = gt_b + eq_b * (pos_eq < need).astype(F32)
        slot_ref[0, rows, :] = jnp.where(chosen > 0.5, pos_gt + jnp.minimum(pos_eq, need), -1.0)
        carry_gt = carry_gt + jnp.sum(gt_b, axis=0, keepdims=True)
        carry_eq = carry_eq + jnp.sum(eq_b, axis=0, keepdims=True)
        start_ref[0, b + 1:b + 2, :] = carry_gt + jnp.minimum(carry_eq, need)


def _expert_column(a, e):
    lane = lax.broadcasted_iota(jnp.int32, a.shape, 1)
    return jnp.sum(jnp.where(lane == e, a, 0.0), axis=1, keepdims=True)


MOE_START_ROWS = 16
MOE_WIN = 128


def _one_hot_slots(slot_col, first, width):
    s = lax.broadcasted_iota(jnp.int32, (slot_col.shape[0], width), 1).astype(F32)
    return (slot_col - first == s).astype(BF16)


def _slot_windows(start_ref, b, e, k):
    base = (b * N_EXPERTS + e) * MOE_START_ROWS + k
    lo, hi = start_ref[base], start_ref[base + 1]
    return lo // MOE_WIN, jnp.where(hi > lo, (hi - 1) // MOE_WIN + 1, lo // MOE_WIN)


def _moe_gather_kernel(start_ref, slot_ref, h_ref, xg_ref, acc_ref):
    b, e, k = pl.program_id(0), pl.program_id(1), pl.program_id(2)

    @pl.when(k == 0)
    def _():
        acc_ref[...] = jnp.zeros_like(acc_ref)

    col = _expert_column(slot_ref[0], e)
    w0, w1 = _slot_windows(start_ref, b, e, k)

    def window(w, carry):
        s0 = pl.multiple_of(w * MOE_WIN, MOE_WIN)
        pt = _one_hot_slots(col, s0.astype(F32), MOE_WIN)
        acc_ref[pl.ds(s0, MOE_WIN), :] += lax.dot_general(pt, h_ref[0], (((0,), (0,)), ((), ())),
                                                          preferred_element_type=F32)
        return carry

    lax.fori_loop(w0, w1, window, 0)

    @pl.when(k == pl.num_programs(2) - 1)
    def _():
        xg_ref[0] = acc_ref[...].astype(xg_ref.dtype)


def _moe_gather_short_kernel(slot_ref, h_ref, xg_ref, *, cap):
    slot = slot_ref[0]
    h = h_ref[0]
    for e in range(N_EXPERTS):
        pt = _one_hot_slots(slot[:, e:e + 1], 0.0, cap)
        xg_ref[e] = lax.dot_general(pt, h, (((0,), (0,)), ((), ())), preferred_element_type=F32).astype(xg_ref.dtype)


def _moe_ffn_kernel(x_ref, wg_ref, wu_ref, wd_ref, y_ref, wg_s, wu_s, wd_s):
    @pl.when(pl.program_id(1) == 0)
    def _():
        wg_s[...] = wg_ref[0].astype(BF16)
        wu_s[...] = wu_ref[0].astype(BF16)
        wd_s[...] = wd_ref[0].astype(BF16)

    x = x_ref[0]
    a = _dot(x, wg_s[...])
    u = _dot(x, wu_s[...])
    hid = (a * jax.nn.sigmoid(a) * u).astype(BF16)
    y_ref[0] = _dot(hid, wd_s[...]).astype(y_ref.dtype)


def _moe_scatter_kernel(start_ref, slot_ref, aff_ref, y_ref, x_ref, gate_ref, o_ref, acc_ref):
    b, k, e = pl.program_id(0), pl.program_id(1), pl.program_id(2)

    @pl.when(e == 0)
    def _():
        acc_ref[...] = jnp.zeros_like(acc_ref)

    col = _expert_column(slot_ref[0], e)
    w0, w1 = _slot_windows(start_ref, b, e, k)

    weight = _expert_column(aff_ref[0], e)

    def window(w, carry):
        s0 = pl.multiple_of(w * MOE_WIN, MOE_WIN)
        pt = _one_hot_slots(col, s0.astype(F32), MOE_WIN)
        acc_ref[...] += weight * _dot(pt, y_ref[0, pl.ds(s0, MOE_WIN), :])
        return carry

    lax.fori_loop(w0, w1, window, 0)

    @pl.when(e == pl.num_programs(2) - 1)
    def _():
        o_ref[0] = x_ref[0] + gate_ref[0] * acc_ref[...]


def _moe_scatter_short_kernel(slot_ref, aff_ref, y_ref, x_ref, gate_ref, o_ref, *, cap):
    slot, aff = slot_ref[0], aff_ref[0]
    acc = jnp.zeros(x_ref.shape[1:], F32)
    for e in range(N_EXPERTS):
        pt = _one_hot_slots(slot[:, e:e + 1], 0.0, cap)
        acc = acc + aff[:, e:e + 1] * _dot(pt, y_ref[e])
    o_ref[0] = x_ref[0] + gate_ref[0] * acc


def moe_route(x, g, shift, scale, w_router):
    B, T, D = x.shape
    tt = min(T, MOE_TT)
    cap = EC_CAPACITY * T // N_EXPERTS
    per_request = shift.shape[0] == B
    mod_spec = pl.BlockSpec((1, 1, D), (lambda b, k: (b, 0, 0)) if per_request else (lambda b, k: (0, 0, 0)))
    wr = jnp.concatenate([w_router.astype(F32), jnp.zeros((D, LANES - N_EXPERTS), F32)], axis=1)
    h, aff = pl.pallas_call(
        _moe_router_kernel,
        grid=(B, T // tt),
        in_specs=[pl.BlockSpec((1, tt, D), lambda b, k: (b, k, 0)), pl.BlockSpec((1, D), lambda b, k: (0, 0)),
                  mod_spec, mod_spec, pl.BlockSpec((D, LANES), lambda b, k: (0, 0))],
        out_specs=[pl.BlockSpec((1, tt, D), lambda b, k: (b, k, 0)), pl.BlockSpec((1, tt, LANES), lambda b, k: (b, k, 0))],
        out_shape=[jax.ShapeDtypeStruct((B, T, D), BF16), jax.ShapeDtypeStruct((B, T, LANES), F32)],
        compiler_params=pltpu.CompilerParams(dimension_semantics=("arbitrary", "arbitrary"),
                                             vmem_limit_bytes=VMEM_LIMIT_BYTES),
        name="moe_router",
    )(x, g.astype(F32)[None, :], shift, scale, wr)
    assert T // tt < MOE_START_ROWS
    slot, start = pl.pallas_call(
        functools.partial(_moe_select_kernel, cap=cap),
        grid=(B,),
        in_specs=[pl.BlockSpec((1, T, LANES), lambda b: (b, 0, 0))],
        out_specs=[pl.BlockSpec((1, T, LANES), lambda b: (b, 0, 0)),
                   pl.BlockSpec((1, MOE_START_ROWS, LANES), lambda b: (b, 0, 0))],
        out_shape=[jax.ShapeDtypeStruct((B, T, LANES), F32), jax.ShapeDtypeStruct((B, MOE_START_ROWS, LANES), F32)],
        compiler_params=pltpu.CompilerParams(dimension_semantics=("arbitrary",), vmem_limit_bytes=VMEM_LIMIT_BYTES),
        name="moe_select",
    )(aff)
    start = start[:, :, :N_EXPERTS].astype(jnp.int32).transpose(0, 2, 1).reshape(-1)
    return h, aff, slot, start


def moe_gather(h, slot, start):
    B, T, D = h.shape
    cap = EC_CAPACITY * T // N_EXPERTS
    out_shape = jax.ShapeDtypeStruct((N_EXPERTS, B * cap, D), BF16)
    if T <= MOE_TT:
        return pl.pallas_call(
            functools.partial(_moe_gather_short_kernel, cap=cap),
            grid=(B,),
            in_specs=[pl.BlockSpec((1, T, LANES), lambda b: (b, 0, 0)), pl.BlockSpec((1, T, D), lambda b: (b, 0, 0))],
            out_specs=pl.BlockSpec((N_EXPERTS, cap, D), lambda b: (0, b, 0)),
            out_shape=out_shape,
            compiler_params=pltpu.CompilerParams(dimension_semantics=("arbitrary",), vmem_limit_bytes=VMEM_LIMIT_BYTES),
            name="moe_gather_short",
        )(slot, h)
    tt = MOE_TT
    assert cap % MOE_WIN == 0
    return pl.pallas_call(
        _moe_gather_kernel,
        grid_spec=pltpu.PrefetchScalarGridSpec(
            num_scalar_prefetch=1,
            grid=(B, N_EXPERTS, T // tt),
            in_specs=[pl.BlockSpec((1, tt, LANES), lambda b, e, k, st: (b, k, 0)),
                      pl.BlockSpec((1, tt, D), lambda b, e, k, st: (b, k, 0))],
            out_specs=pl.BlockSpec((1, cap, D), lambda b, e, k, st: (e, b, 0)),
            scratch_shapes=[pltpu.VMEM((cap, D), F32)],
        ),
        out_shape=out_shape,
        compiler_params=pltpu.CompilerParams(dimension_semantics=("arbitrary",) * 3, vmem_limit_bytes=VMEM_LIMIT_BYTES),
        name="moe_gather",
    )(start, slot, h)


def moe_ffn(xg, w_gate, w_up, w_down, layer):
    E, R, D = xg.shape
    rt = min(R, MOE_RT)
    w_spec = pl.BlockSpec((None, 1, D, MOE_D_FF), lambda e, r: (layer, e, 0, 0))
    return pl.pallas_call(
        _moe_ffn_kernel,
        grid=(E, R // rt),
        in_specs=[pl.BlockSpec((1, rt, D), lambda e, r: (e, r, 0)), w_spec, w_spec,
                  pl.BlockSpec((None, 1, MOE_D_FF, D), lambda e, r: (layer, e, 0, 0))],
        out_specs=pl.BlockSpec((1, rt, D), lambda e, r: (e, r, 0)),
        out_shape=jax.ShapeDtypeStruct((E, R, D), BF16),
        scratch_shapes=[pltpu.VMEM((D, MOE_D_FF), BF16), pltpu.VMEM((D, MOE_D_FF), BF16), pltpu.VMEM((MOE_D_FF, D), BF16)],
        compiler_params=pltpu.CompilerParams(dimension_semantics=("arbitrary", "arbitrary"),
                                             vmem_limit_bytes=VMEM_LIMIT_BYTES),
        name="moe_ffn",
    )(xg, w_gate, w_up, w_down)


def moe_scatter(slot, aff, start, y, x, gate):
    B, T, D = x.shape
    cap = EC_CAPACITY * T // N_EXPERTS
    per_request = gate.shape[0] == B
    out_shape = jax.ShapeDtypeStruct((B, T, D), F32)
    if T <= MOE_TT:
        whole = lambda b: (b, 0, 0)
        return pl.pallas_call(
            functools.partial(_moe_scatter_short_kernel, cap=cap),
            grid=(B,),
            in_specs=[pl.BlockSpec((1, T, LANES), whole), pl.BlockSpec((1, T, LANES), whole),
                      pl.BlockSpec((N_EXPERTS, cap, D), lambda b: (0, b, 0)), pl.BlockSpec((1, T, D), whole),
                      pl.BlockSpec((1, 1, D), whole if per_request else (lambda b: (0, 0, 0)))],
            out_specs=pl.BlockSpec((1, T, D), whole),
            out_shape=out_shape,
            compiler_params=pltpu.CompilerParams(dimension_semantics=("arbitrary",), vmem_limit_bytes=VMEM_LIMIT_BYTES),
            name="moe_scatter_short",
        )(slot, aff, y, x, gate)
    tt = MOE_TT
    tile = lambda b, k, e, st: (b, k, 0)
    return pl.pallas_call(
        _moe_scatter_kernel,
        grid_spec=pltpu.PrefetchScalarGridSpec(
            num_scalar_prefetch=1,
            grid=(B, T // tt, N_EXPERTS),
            in_specs=[pl.BlockSpec((1, tt, LANES), tile), pl.BlockSpec((1, tt, LANES), tile),
                      pl.BlockSpec((1, cap, D), lambda b, k, e, st: (e, b, 0)), pl.BlockSpec((1, tt, D), tile),
                      pl.BlockSpec((1, 1, D), (lambda b, k, e, st: (b, 0, 0)) if per_request else (lambda b, k, e, st: (0, 0, 0)))],
            out_specs=pl.BlockSpec((1, tt, D), tile),
            scratch_shapes=[pltpu.VMEM((tt, D), F32)],
        ),
        out_shape=out_shape,
        compiler_params=pltpu.CompilerParams(dimension_semantics=("arbitrary",) * 3, vmem_limit_bytes=VMEM_LIMIT_BYTES),
        name="moe_scatter",
    )(start, slot, aff, y, x, gate)


def moe_block(x, g, shift, scale, gate, w_router, w_gate, w_up, w_down, layer):
    h, aff, slot, start = moe_route(x, g, shift, scale, w_router)
    y = moe_ffn(moe_gather(h, slot, start), w_gate, w_up, w_down, layer)
    return moe_scatter(slot, aff, start, y, x, gate)


HY_TM = 512
HY_BG = 8
HY_BG_LONG = 4


def dft_tables(L):
    f = jnp.arange(L, dtype=jnp.int32)
    r = (f[:, None] * f[None, :]) % (2 * L)
    ang = r.astype(F32) * (math.pi / L)
    out = []
    for tab in (jnp.cos(ang), jnp.sin(ang)):
        hi = tab.astype(BF16)
        out += [hi, (tab - hi.astype(F32)).astype(BF16)]
    return out


def _split_bf16(x):
    hi = x.astype(BF16)
    return hi, (x - hi.astype(F32)).astype(BF16)


def _mm3(t_hi, t_lo, x_hi, x_lo):
    return _dot(t_hi, x_hi) + _dot(t_hi, x_lo) + _dot(t_lo, x_hi)


def _alt_sign(rows, first_row):
    t = lax.broadcasted_iota(jnp.int32, (rows, 1), 0) + first_row
    return jnp.where(t % 2 == 0, 1.0, -1.0).astype(F32)


def _hy_prep_kernel(x_ref, prev_ref, next_ref, cw_ref, cb_ref, o_ref):
    i = pl.program_id(1)
    n = pl.num_programs(1)
    x = x_ref[0]
    tm = x.shape[0]
    row = lax.broadcasted_iota(jnp.int32, x.shape, 0)
    prev_row = jnp.where(i > 0, prev_ref[0, SUBLANES - 1:SUBLANES, :], 0.0)
    next_row = jnp.where(i < n - 1, next_ref[0, 0:1, :], 0.0)
    x_prev = jnp.where(row == 0, prev_row, pltpu.roll(x, 1, axis=0))
    x_next = jnp.where(row == tm - 1, next_row, pltpu.roll(x, tm - 1, axis=0))
    o_ref[0] = x_prev * cw_ref[0:1, :] + x * cw_ref[1:2, :] + x_next * cw_ref[2:3, :] + cb_ref[...]


def hyena_prep(zh, conv_w, conv_b):
    B, L, W = zh.shape
    tm = min(L, GDN_TM)
    nb = tm // SUBLANES
    return pl.pallas_call(
        _hy_prep_kernel,
        grid=(B, L // tm),
        in_specs=[
            pl.BlockSpec((1, tm, W), lambda b, i: (b, i, 0)),
            pl.BlockSpec((1, SUBLANES, W), lambda b, i: (b, jnp.maximum(i * nb - 1, 0), 0)),
            pl.BlockSpec((1, SUBLANES, W), lambda b, i: (b, jnp.minimum((i + 1) * nb, L // SUBLANES - 1), 0)),
            pl.BlockSpec((HY_SHORT, W), lambda b, i: (0, 0)),
            pl.BlockSpec((1, W), lambda b, i: (0, 0)),
        ],
        out_specs=pl.BlockSpec((1, tm, W), lambda b, i: (b, i, 0)),
        out_shape=jax.ShapeDtypeStruct((B, L, W), F32),
        compiler_params=pltpu.CompilerParams(dimension_semantics=("arbitrary", "arbitrary"),
                                             vmem_limit_bytes=VMEM_LIMIT_BYTES),
        name="hyena_prep",
    )(zh, zh, zh, conv_w.astype(F32), conv_b.astype(F32)[None, :])


def _hy_taps_kernel(feat_ref, w1_ref, b1_ref, w2_ref, b2_ref, fr_ref, w3f_ref, w3b_ref, dec_ref, sum_ref, dif_ref):
    L = feat_ref.shape[0]
    fr = fr_ref[...]
    h = jnp.sin(fr * (jnp.dot(feat_ref[...], w1_ref[...], precision=HIGHEST, preferred_element_type=F32) + b1_ref[...]))
    h = jnp.sin(fr * (jnp.dot(h, w2_ref[...], precision=HIGHEST, preferred_element_type=F32) + b2_ref[...]))
    t = lax.broadcasted_iota(jnp.int32, (L, 1), 0)
    window = jnp.exp(-(t.astype(F32) / (L - 1)) * dec_ref[...]) + HY_SHIFT
    fwd = jnp.dot(h, w3f_ref[...], precision=HIGHEST, preferred_element_type=F32) * window
    bwd = jnp.where(t == 0, 0.0, jnp.dot(h, w3b_ref[...], precision=HIGHEST, preferred_element_type=F32) * window)
    inv = 1.0 / (jnp.sum(jnp.abs(fwd), axis=0, keepdims=True) + jnp.sum(jnp.abs(bwd), axis=0, keepdims=True))
    sum_ref[...] = (fwd + bwd) * inv
    dif_ref[...] = (bwd - fwd) * inv


def hyena_taps(L, w1, b1, w2, b2, w3, freq):
    f32 = F32
    t = jnp.linspace(0.0, 1.0, L, dtype=f32)[:, None]
    bands = (HY_EMB - 1) // 2
    omega = 2.0 * math.pi * jnp.arange(L, dtype=f32)[:, None] / L
    fb = jnp.linspace(1e-4, bands - 1, bands, dtype=f32)[None, :]
    feats = jnp.concatenate([t, jnp.cos(fb * omega), -jnp.sin(fb * omega)], axis=-1)
    max_decay = math.log(HY_DECAY_TARGET) / HY_FAST_DECAY
    min_decay = math.log(HY_DECAY_TARGET) / HY_SLOW_DECAY
    deltas = jnp.abs(jnp.linspace(min_decay, max_decay, HY_CH, dtype=f32))
    dec = jnp.tile(deltas, HY_ORDER)[None, :]
    n_col = HY_ORDER * HY_CH
    cb = 256
    full = lambda shape: pl.BlockSpec(shape, lambda j: (0, 0))
    col = lambda rows: pl.BlockSpec((rows, cb), lambda j: (0, j))
    return pl.pallas_call(
        _hy_taps_kernel,
        grid=(n_col // cb,),
        in_specs=[full((L, HY_EMB)), full((HY_EMB, HY_FILT_W)), full((1, HY_FILT_W)), full((HY_FILT_W, HY_FILT_W)),
                  full((1, HY_FILT_W)), full((1, HY_FILT_W)), col(HY_FILT_W), col(HY_FILT_W), col(1)],
        out_specs=[col(L), col(L)],
        out_shape=[jax.ShapeDtypeStruct((L, n_col), f32)] * 2,
        compiler_params=pltpu.CompilerParams(dimension_semantics=("arbitrary",), vmem_limit_bytes=VMEM_LIMIT_BYTES),
        name="hyena_taps",
    )(feats, w1.astype(f32), b1.astype(f32)[None, :], w2.astype(f32), b2.astype(f32)[None, :], freq.astype(f32)[None, :],
      w3.astype(f32)[:, :n_col], w3.astype(f32)[:, n_col:], dec)


def _hy_spec_kernel(ch_ref, cl_ref, sh_ref, sl_ref, sum_ref, dif_ref, hr_ref, hi_ref, ny_ref, acc_r, acc_i, acc_n):
    m, k = pl.program_id(1), pl.program_id(2)
    tk = sum_ref.shape[0]

    @pl.when(k == 0)
    def _():
        acc_r[...] = jnp.zeros_like(acc_r)
        acc_i[...] = jnp.zeros_like(acc_i)

    @pl.when(jnp.logical_and(k == 0, m == 0))
    def _():
        acc_n[...] = jnp.zeros_like(acc_n)

    a = sum_ref[...]
    acc_r[...] += _mm3(ch_ref[...], cl_ref[...], *_split_bf16(a))
    acc_i[...] += _mm3(sh_ref[...], sl_ref[...], *_split_bf16(dif_ref[...]))

    @pl.when(m == 0)
    def _():
        acc_n[...] += jnp.sum(a * _alt_sign(tk, k * tk), axis=0, keepdims=True)

    @pl.when(k == pl.num_programs(2) - 1)
    def _():
        hr_ref[...] = acc_r[...]
        hi_ref[...] = acc_i[...]
        ny_ref[...] = jnp.broadcast_to(acc_n[...], ny_ref.shape)


def hyena_spectrum(tables, tap_sum, tap_dif):
    L, N = tap_sum.shape
    tm = min(L, HY_TM)
    cb = 512
    tab = pl.BlockSpec((tm, tm), lambda j, m, k: (m, k))
    dat = pl.BlockSpec((tm, cb), lambda j, m, k: (k, j))
    return pl.pallas_call(
        _hy_spec_kernel,
        grid=(N // cb, L // tm, L // tm),
        in_specs=[tab, tab, tab, tab, dat, dat],
        out_specs=[pl.BlockSpec((tm, cb), lambda j, m, k: (m, j)), pl.BlockSpec((tm, cb), lambda j, m, k: (m, j)),
                   pl.BlockSpec((SUBLANES, cb), lambda j, m, k: (0, j))],
        out_shape=[jax.ShapeDtypeStruct((L, N), F32), jax.ShapeDtypeStruct((L, N), F32),
                   jax.ShapeDtypeStruct((SUBLANES, N), F32)],
        scratch_shapes=[pltpu.VMEM((tm, cb), F32), pltpu.VMEM((tm, cb), F32), pltpu.VMEM((1, cb), F32)],
        compiler_params=pltpu.CompilerParams(dimension_semantics=("arbitrary",) * 3, vmem_limit_bytes=VMEM_LIMIT_BYTES),
        name="hyena_spectrum",
    )(*tables, tap_sum, tap_dif)


def _hy_fwd_kernel(ch_ref, sh_ref, u_ref, hr_ref, hi_ref, hny_ref, yr_ref, yi_ref, yny_ref,
                   acc_c, acc_s, acc_n):
    m, k = pl.program_id(1), pl.program_id(2)
    nb, tk = u_ref.shape[0], u_ref.shape[1]
    tm = acc_c.shape[1]

    @pl.when(k == 0)
    def _():
        acc_c[...] = jnp.zeros_like(acc_c)
        acc_s[...] = jnp.zeros_like(acc_s)

    @pl.when(jnp.logical_and(k == 0, m == 0))
    def _():
        acc_n[...] = jnp.zeros_like(acc_n)

    sign = _alt_sign(tk, k * tk)
    for b in range(nb):
        u = u_ref[b]
        ub = u.astype(BF16)
        acc_c[b] += _dot(ch_ref[...], ub)
        acc_s[b] += _dot(sh_ref[...], ub)

        @pl.when(m == 0)
        def _():
            acc_n[b] += jnp.sum(u * sign, axis=0, keepdims=True)

    @pl.when(k == pl.num_programs(2) - 1)
    def _():
        f = lax.broadcasted_iota(jnp.int32, (tm, 1), 0) + m * tm
        dc = jnp.where(f == 0, 0.5, 1.0).astype(F32)
        hr, hi = hr_ref[...], hi_ref[...]
        for b in range(nb):
            xr, xs = acc_c[b], acc_s[b]
            yr_ref[b] = ((xr * hr + xs * hi) * dc).astype(yr_ref.dtype)
            yi_ref[b] = (xr * hi - xs * hr).astype(yi_ref.dtype)
            yny_ref[b] = jnp.broadcast_to(acc_n[b] * hny_ref[0:1, :], yny_ref.shape[1:])


def _hy_inv_kernel(ch_ref, sh_ref, yr_ref, yi_ref, yny_ref, u_ref, xg_ref, skip_ref, o_ref, acc):
    m, k = pl.program_id(1), pl.program_id(2)
    nb = yr_ref.shape[0]
    tm = acc.shape[1]
    L = tm * pl.num_programs(1)

    @pl.when(k == 0)
    def _():
        acc[...] = jnp.zeros_like(acc)

    for b in range(nb):
        acc[b] += _dot(ch_ref[...], yr_ref[b]) - _dot(sh_ref[...], yi_ref[b])

    @pl.when(k == pl.num_programs(2) - 1)
    def _():
        sign = _alt_sign(tm, m * tm)
        for b in range(nb):
            u = u_ref[b]
            y = acc[b] * (1.0 / L) + sign * yny_ref[b, 0:1, :] * (0.5 / L)
            o_ref[b] = (xg_ref[b] * (y + u * skip_ref[...])).astype(o_ref.dtype)


def hyena_long_conv(tables, zf, u, u_col, gate_col, hr, hi, hny, order, skip, out_dtype):
    B, L, _ = u.shape
    C = HY_CH
    tm = min(L, HY_TM)
    bg = min(B, HY_BG_LONG if L >= HY_TM else HY_BG)
    assert B % bg == 0
    grid = (B // bg, L // tm, L // tm)
    tab = pl.BlockSpec((tm, tm), lambda g, m, k: (m, k))
    params = pltpu.CompilerParams(dimension_semantics=("arbitrary",) * 3, vmem_limit_bytes=VMEM_LIMIT_BYTES)
    spec_m = pl.BlockSpec((tm, C), lambda g, m, k: (m, order))
    yr, yi, yny = pl.pallas_call(
        _hy_fwd_kernel,
        grid=grid,
        in_specs=[tab, tab, pl.BlockSpec((bg, tm, C), lambda g, m, k: (g, k, u_col)), spec_m, spec_m,
                  pl.BlockSpec((SUBLANES, C), lambda g, m, k: (0, order))],
        out_specs=[pl.BlockSpec((bg, tm, C), lambda g, m, k: (g, m, 0)), pl.BlockSpec((bg, tm, C), lambda g, m, k: (g, m, 0)),
                   pl.BlockSpec((bg, SUBLANES, C), lambda g, m, k: (g, 0, 0))],
        out_shape=[jax.ShapeDtypeStruct((B, L, C), BF16), jax.ShapeDtypeStruct((B, L, C), BF16),
                   jax.ShapeDtypeStruct((B, SUBLANES, C), F32)],
        scratch_shapes=[pltpu.VMEM((bg, tm, C), F32), pltpu.VMEM((bg, tm, C), F32), pltpu.VMEM((bg, 1, C), F32)],
        compiler_params=params,
        name="hyena_fwd",
    )(tables[0], tables[2], u, hr, hi, hny)
    return pl.pallas_call(
        _hy_inv_kernel,
        grid=grid,
        in_specs=[tab, tab, pl.BlockSpec((bg, tm, C), lambda g, m, k: (g, k, 0)),
                  pl.BlockSpec((bg, tm, C), lambda g, m, k: (g, k, 0)),
                  pl.BlockSpec((bg, SUBLANES, C), lambda g, m, k: (g, 0, 0)),
                  pl.BlockSpec((bg, tm, C), lambda g, m, k: (g, m, u_col)),
                  pl.BlockSpec((bg, tm, C), lambda g, m, k: (g, m, gate_col)),
                  pl.BlockSpec((1, C), lambda g, m, k: (0, 0))],
        out_specs=pl.BlockSpec((bg, tm, C), lambda g, m, k: (g, m, 0)),
        out_shape=jax.ShapeDtypeStruct((B, L, C), out_dtype),
        scratch_shapes=[pltpu.VMEM((bg, tm, C), F32)],
        compiler_params=params,
        name="hyena_inv",
    )(tables[0], tables[2], yr, yi, yny, u, zf, skip.astype(F32)[order][None, :])


def hyena_filter_spectrum(L, w1, b1, w2, b2, w3, freq):
    tables = dft_tables(L)
    tap_sum, tap_dif = hyena_taps(L, w1, b1, w2, b2, w3, freq)
    return tables, hyena_spectrum(tables, tap_sum, tap_dif)


def hyena_mixer_pallas(zh, conv_w, conv_b, skip, tables, spectrum):
    hr, hi, hny = spectrum
    zf = hyena_prep(zh, conv_w, conv_b)
    y1 = hyena_long_conv(tables, zf, zf, 0, 1, hr, hi, hny, 0, skip, F32)
    return hyena_long_conv(tables, zf, y1, 0, 2, hr, hi, hny, 1, skip, BF16)


def rms_norm(x, g):
    xf = x.astype(jnp.float32)
    y = xf * lax.rsqrt(jnp.mean(xf * xf, axis=-1, keepdims=True) + EPS)
    return (y * g.astype(jnp.float32)).astype(x.dtype)


def _ada_kernel(c_ref, w_ref, b_ref, o_ref):
    cnd = c_ref[...]
    act = cnd * jax.nn.sigmoid(cnd)
    o_ref[0] = jnp.dot(act, w_ref[0], precision=HIGHEST, preferred_element_type=F32) + b_ref[0]


def ada_params_all(cond, w_ada, b_ada):
    N, D = cond.shape
    depth, _, W = w_ada.shape
    rows = -(-N // SUBLANES) * SUBLANES
    cond_p = jnp.concatenate([cond.astype(F32), jnp.zeros((rows - N, D), F32)], axis=0)
    cb = D
    out = pl.pallas_call(
        _ada_kernel,
        grid=(depth, W // cb),
        in_specs=[pl.BlockSpec((rows, D), lambda l, j: (0, 0)), pl.BlockSpec((1, D, cb), lambda l, j: (l, 0, j)),
                  pl.BlockSpec((1, 1, cb), lambda l, j: (l, 0, j))],
        out_specs=pl.BlockSpec((1, rows, cb), lambda l, j: (l, 0, j)),
        out_shape=jax.ShapeDtypeStruct((depth, rows, W), F32),
        compiler_params=pltpu.CompilerParams(dimension_semantics=("arbitrary", "arbitrary"),
                                             vmem_limit_bytes=VMEM_LIMIT_BYTES),
        name="ada_params",
    )(cond_p, w_ada.astype(F32), b_ada.astype(F32)[:, None, :])
    return out[:, :N]


def _final_norm_kernel(x_ref, g_ref, o_ref):
    x = x_ref[...]
    o_ref[...] = x * lax.rsqrt(jnp.mean(x * x, axis=-1, keepdims=True) + EPS) * g_ref[...]


def final_norm(x, g, tm=512):
    B, L, D = x.shape
    M = B * L
    assert M % tm == 0
    out = pl.pallas_call(
        _final_norm_kernel,
        grid=(M // tm,),
        in_specs=[pl.BlockSpec((tm, D), lambda i: (i, 0)), pl.BlockSpec((1, D), lambda i: (0, 0))],
        out_specs=pl.BlockSpec((tm, D), lambda i: (i, 0)),
        out_shape=jax.ShapeDtypeStruct((M, D), F32),
        compiler_params=pltpu.CompilerParams(dimension_semantics=("arbitrary",), vmem_limit_bytes=VMEM_LIMIT_BYTES),
        name="final_norm",
    )(x.reshape(M, D), g.astype(F32)[None, :])
    return out.reshape(B, L, D)


def ada_params(cond, w, b):
    m = jax.nn.silu(cond) @ w + b
    return jnp.split(m[:, None, :], 6, axis=-1)


def modulate(x, g, shift, scale):
    return rms_norm(x, g) * (1 + scale) + shift


def axial_rope(T):
    t = jnp.arange(T)
    n_freq = HEAD_DIM // 4
    inv = ROPE_BASE ** (-jnp.arange(n_freq, dtype=jnp.float32) / n_freq)
    ang = jnp.concatenate([(t // GRID_W).astype(jnp.float32)[:, None] * inv,
                           (t % GRID_W).astype(jnp.float32)[:, None] * inv], axis=-1)
    return jnp.cos(ang)[:, None, :], jnp.sin(ang)[:, None, :]


def apply_rope(x, cos, sin):
    xf = x.astype(jnp.float32)
    x1, x2 = jnp.split(xf, 2, axis=-1)
    return jnp.concatenate([x1 * cos - x2 * sin, x2 * cos + x1 * sin], axis=-1).astype(x.dtype)


def softmax_parts(parts, sink=None):
    sizes = [p.shape[-1] for p in parts]
    cols = list(parts)
    if sink is not None:
        cols.append(jnp.broadcast_to(sink, parts[0].shape[:-1] + (1,)))
    p = jax.nn.softmax(jnp.concatenate(cols, axis=-1), axis=-1)
    pieces = jnp.split(p, np.cumsum(sizes).tolist(), axis=-1)
    return pieces[:len(sizes)]


def context_attention(q, k, v, sink):
    B, S, HQ, hd = q.shape
    HK = k.shape[2]
    G = HQ // HK
    nb = S // Q_BLOCK
    scale = hd ** -0.5
    sink_b = None if sink is None else sink.astype(jnp.float32).reshape(1, HK, G, 1, 1)
    qb = q.reshape(B, nb, Q_BLOCK, HK, G, hd).swapaxes(0, 1)

    def one(qi):
        s = jnp.einsum('bqkgd,bskd->bkgqs', qi, k, preferred_element_type=jnp.float32) * scale
        (p,) = softmax_parts([s], sink_b)
        return jnp.einsum('bkgqs,bskd->bqkgd', p.astype(v.dtype), v)

    o = lax.map(one, qb)
    return o.swapaxes(0, 1).reshape(B, S, HQ, hd)


def window_attention(q, k, v, ck, cv, sink):
    B, T, HQ, hd = q.shape
    HK = k.shape[2]
    G = HQ // HK
    nb = T // A_BLOCK
    scale = hd ** -0.5
    qb = q.reshape(B, nb, A_BLOCK, HK, G, hd).swapaxes(0, 1)

    def band(x):
        xp = jnp.pad(x, ((0, 0), (A_BLOCK, A_BLOCK), (0, 0), (0, 0))).reshape(B, nb + 2, A_BLOCK, HK, hd)
        return jnp.concatenate([xp[:, :-2], xp[:, 1:-1], xp[:, 2:]], axis=2).swapaxes(0, 1)

    kb, vb = band(k), band(v)
    qpos = jnp.arange(nb)[:, None, None] * A_BLOCK + jnp.arange(A_BLOCK)[None, :, None]
    kpos = jnp.arange(nb)[:, None, None] * A_BLOCK - A_BLOCK + jnp.arange(3 * A_BLOCK)[None, None, :]
    mask = (jnp.abs(kpos - qpos) <= A_WINDOW) & (kpos >= 0) & (kpos < T)
    sink_b = sink.astype(jnp.float32).reshape(1, HK, G, 1, 1)

    def one(xs):
        qi, ki, vi, mi = xs
        s_loc = jnp.einsum('bqkgd,bskd->bkgqs', qi, ki, preferred_element_type=jnp.float32) * scale
        s_loc = jnp.where(mi[None, None, None], s_loc, NEG_INF)
        s_ctx = jnp.einsum('bqkgd,bpkd->bkgqp', qi, ck, preferred_element_type=jnp.float32) * scale
        p_loc, p_ctx = softmax_parts([s_loc, s_ctx], sink_b)
        return (jnp.einsum('bkgqs,bskd->bqkgd', p_loc.astype(vi.dtype), vi)
                + jnp.einsum('bkgqp,bpkd->bqkgd', p_ctx.astype(cv.dtype), cv))

    o = lax.map(one, (qb, kb, vb, mask))
    return o.swapaxes(0, 1).reshape(B, T, HQ, hd)


def short_conv(x, w):
    K = w.shape[0]
    L = x.shape[1]
    pad = K // 2
    xp = jnp.pad(x, ((0, 0), (pad, pad), (0, 0)))
    return sum(xp[:, i:i + L] * w[i] for i in range(K))


def hyena_filter_bank(L, w1, b1, w2, b2, w3, freq):
    f32 = jnp.float32
    t = jnp.linspace(0.0, 1.0, L, dtype=f32)[:, None]
    bands = (HY_EMB - 1) // 2
    omega = 2.0 * math.pi * jnp.arange(L, dtype=f32)[:, None] / L
    fb = jnp.linspace(1e-4, bands - 1, bands, dtype=f32)[None, :]
    feats = jnp.concatenate([t, jnp.cos(fb * omega), -jnp.sin(fb * omega)], axis=-1)
    fr = freq.astype(f32)
    h = jnp.sin(fr * (feats @ w1.astype(f32) + b1.astype(f32)))
    h = jnp.sin(fr * (h @ w2.astype(f32) + b2.astype(f32)))
    h = (h @ w3.astype(f32)).reshape(L, 2, HY_ORDER, HY_CH)
    max_decay = math.log(HY_DECAY_TARGET) / HY_FAST_DECAY
    min_decay = math.log(HY_DECAY_TARGET) / HY_SLOW_DECAY
    deltas = jnp.abs(jnp.linspace(min_decay, max_decay, HY_CH, dtype=f32))
    h = h * (jnp.exp(-t * deltas) + HY_SHIFT)[:, None, None, :]
    taps = jnp.concatenate([h[:, 0], jnp.zeros((1, HY_ORDER, HY_CH), f32), h[:0:-1, 1]], axis=0)
    taps = taps / jnp.sum(jnp.abs(taps), axis=0, keepdims=True)
    return jnp.fft.rfft(taps, axis=0)


def hyena_mixer(z, conv_w, conv_b, w1, b1, w2, b2, w3, freq, skip):
    L = z.shape[1]
    zf = (short_conv(z, conv_w) + conv_b).astype(jnp.float32)
    v, x1, x2 = jnp.split(zf, 3, axis=-1)
    filt = hyena_filter_bank(L, w1, b1, w2, b2, w3, freq)
    skip = skip.astype(jnp.float32)

    def long_conv(u, o):
        y = jnp.fft.irfft(jnp.fft.rfft(u, n=2 * L, axis=1) * filt[None, :, o], n=2 * L, axis=1)[:, :L]
        return y + u * skip[o]

    y = x1 * long_conv(v, 0)
    y = x2 * long_conv(y, 1)
    return y.astype(z.dtype)


def l2norm(x):
    xf = x.astype(jnp.float32)
    return xf * lax.rsqrt(jnp.sum(xf * xf, axis=-1, keepdims=True) + EPS)


def chunk_gated_delta(q, k, v, g, beta, s0):
    B, L, H, dk = q.shape
    dv = v.shape[-1]
    n = L // C_CHUNK

    def chunks(x):
        x = x.reshape((B, n, C_CHUNK, H) + x.shape[3:])
        return jnp.moveaxis(jnp.moveaxis(x, 1, 0), 3, 2)

    qc, kc, vc, bc = chunks(q), chunks(k), chunks(v), chunks(beta)
    gc = jnp.cumsum(chunks(g), axis=-1)
    tri = jnp.tril(jnp.ones((C_CHUNK, C_CHUNK), bool))
    strict = jnp.tril(jnp.ones((C_CHUNK, C_CHUNK), bool), k=-1)
    gamma = jnp.exp(jnp.where(tri, gc[..., :, None] - gc[..., None, :], NEG_INF))
    kb = kc * bc[..., None]
    a_mat = jnp.where(strict, jnp.einsum('nbhid,nbhjd->nbhij', kb, kc) * gamma, 0.0) + jnp.eye(C_CHUNK, dtype=jnp.float32)
    rhs = jnp.concatenate([vc * bc[..., None], kb * jnp.exp(gc)[..., None]], axis=-1)
    sol = lax.linalg.triangular_solve(a_mat, rhs, left_side=True, lower=True)
    u, w = sol[..., :dv], sol[..., dv:]
    attn = jnp.where(tri, jnp.einsum('nbhid,nbhjd->nbhij', qc, kc) * gamma, 0.0)
    g_last = gc[..., -1]
    q_dec = qc * jnp.exp(gc)[..., None]
    k_dec = kc * jnp.exp(g_last[..., None] - gc)[..., None]

    def step(S, xs):
        u_i, w_i, a_i, qd, kd, gl = xs
        v_new = u_i - jnp.einsum('bhck,bhkv->bhcv', w_i, S)
        o = jnp.einsum('bhck,bhkv->bhcv', qd, S) + jnp.einsum('bhij,bhjv->bhiv', a_i, v_new)
        S = S * jnp.exp(gl)[..., None, None] + jnp.einsum('bhck,bhcv->bhkv', kd, v_new)
        return S, o

    S, o = lax.scan(step, s0.astype(jnp.float32), (u, w, attn, q_dec, k_dec, g_last))
    o = jnp.moveaxis(jnp.moveaxis(o, 2, 3), 0, 1).reshape(B, L, H, dv)
    return o, S


def deltanet_mixer(zq, zk, zv, zg, za, zb, conv_w, a_log, dt_bias, norm_w, s0):
    B, L, _ = zq.shape
    qkv = jax.nn.silu(short_conv(jnp.concatenate([zq, zk, zv], axis=-1), conv_w))
    q, k, v = [t.reshape(B, L, C_HEADS, HEAD_DIM) for t in jnp.split(qkv, 3, axis=-1)]
    q = l2norm(q) * (HEAD_DIM ** -0.5)
    k = l2norm(k)
    v = v.astype(jnp.float32)
    beta = jax.nn.sigmoid(zb.astype(jnp.float32))
    g = -jnp.exp(a_log.astype(jnp.float32)) * jax.nn.softplus(za.astype(jnp.float32) + dt_bias.astype(jnp.float32))
    o_f, s_f = chunk_gated_delta(q, k, v, g[:, :, 0], beta[:, :, 0], s0[:, 0])
    o_b, s_b = chunk_gated_delta(q[:, ::-1], k[:, ::-1], v[:, ::-1], g[:, ::-1, 1], beta[:, ::-1, 1], s0[:, 1])
    o = o_f + o_b[:, ::-1]
    gate = jax.nn.silu(zg.reshape(B, L, C_HEADS, HEAD_DIM).astype(jnp.float32))
    o = rms_norm(o, norm_w) * gate
    return o.reshape(B, L, C_W).astype(zq.dtype), jnp.stack([s_f, s_b], axis=1)


def neighbourhood_attention(q, k, v, ck, cv, rpb):
    B, T, H, hd = q.shape
    rows = T // GRID_W
    kh = min(NA_KH_MAX, rows)
    scale = hd ** -0.5
    r = jnp.arange(rows)
    rs = jnp.clip(r - kh // 2, 0, rows - kh)
    key_rows = rs[:, None] + jnp.arange(kh)[None, :]
    idx = (key_rows[:, :, None] * GRID_W + jnp.arange(GRID_W)).reshape(rows, kh * GRID_W)
    col = jnp.arange(GRID_W)
    cs = jnp.clip(col - NA_KW // 2, 0, GRID_W - NA_KW)
    kcol = jnp.tile(col, kh)
    col_ok = (kcol[None, :] >= cs[:, None]) & (kcol[None, :] < cs[:, None] + NA_KW)
    roff = jnp.repeat(key_rows - r[:, None], GRID_W, axis=1) + NA_KH_MAX - 1
    coff = jnp.clip(kcol[None, :] - col[:, None] + NA_KW - 1, 0, 2 * NA_KW - 2)
    qr = q.reshape(B, rows, GRID_W, H, hd).swapaxes(0, 1)
    rpb_f = rpb.astype(jnp.float32)

    def one(xs):
        qi, ii, ro = xs
        ki = k[:, ii]
        vi = v[:, ii]
        bias = rpb_f[:, ro[None, :], coff]
        s_loc = jnp.einsum('bqhd,bkhd->bhqk', qi, ki, preferred_element_type=jnp.float32) * scale + bias[None]
        s_loc = jnp.where(col_ok[None, None], s_loc, NEG_INF)
        s_ctx = jnp.einsum('bqhd,bphd->bhqp', qi, ck, preferred_element_type=jnp.float32) * scale
        p_loc, p_ctx = softmax_parts([s_loc, s_ctx])
        return (jnp.einsum('bhqk,bkhd->bqhd', p_loc.astype(vi.dtype), vi)
                + jnp.einsum('bhqp,bphd->bqhd', p_ctx.astype(cv.dtype), cv))

    o = lax.map(one, (qr, idx, roff))
    return o.swapaxes(0, 1).reshape(B, T, H, hd)


def expert_choice_ffn(h, w_router, w_gate, w_up, w_down):
    B, T, D = h.shape
    cap = EC_CAPACITY * T // N_EXPERTS
    aff = jax.nn.softmax(jnp.einsum('btd,de->bte', h, w_router, preferred_element_type=jnp.float32), axis=-1)
    gate, idx = lax.top_k(aff.swapaxes(1, 2), cap)
    xg = jax.vmap(lambda hb, ib: hb[ib])(h, idx)
    a = jnp.einsum('becd,edf->becf', xg, w_gate)
    u = jnp.einsum('becd,edf->becf', xg, w_up)
    y = jnp.einsum('becf,efd->becd', jax.nn.silu(a) * u, w_down) * gate[..., None].astype(h.dtype)
    return jax.vmap(lambda yb, ib: jnp.zeros((T, D), yb.dtype).at[ib.reshape(-1)].add(yb.reshape(-1, D)))(y, idx)


def split_even(z):
    B, L = z.shape[:2]
    q = z[..., :A_Q_W].reshape(B, L, A_HEADS, HEAD_DIM)
    k = z[..., A_Q_W:A_Q_W + A_KV_W].reshape(B, L, A_KV_HEADS, HEAD_DIM)
    v = z[..., A_Q_W + A_KV_W:A_Q_W + 2 * A_KV_W].reshape(B, L, A_KV_HEADS, HEAD_DIM)
    return q, k, v, z[..., A_Q_W + 2 * A_KV_W:]


def split_odd(z):
    B, L = z.shape[:2]
    zq, zk, zv, zg = [z[..., i * C_W:(i + 1) * C_W] for i in range(4)]
    off = 4 * C_W
    za = z[..., off:off + 2 * C_HEADS].reshape(B, L, 2, C_HEADS)
    zb = z[..., off + 2 * C_HEADS:off + 4 * C_HEADS].reshape(B, L, 2, C_HEADS)
    off = off + 4 * C_HEADS
    nq, nk, nv = [z[..., off + i * D_W:off + (i + 1) * D_W].reshape(B, L, D_HEADS, HEAD_DIM) for i in range(3)]
    return zq, zk, zv, zg, za, zb, nq, nk, nv


def kernel(x_prompt, x_sample, cache_attn_k, cache_attn_v, state_delta, cache_na_k, cache_na_v,
           c, c_ctx, w_ada, b_ada, norm_mix, norm_ffn, norm_final,
           even_w_in, even_w_out, attn_sink, hy_conv_w, hy_conv_b, hy_w1, hy_b1, hy_w2, hy_b2,
           hy_w3, hy_freq, hy_skip, odd_w_in, odd_w_out, gdn_conv_w, gdn_a_log, gdn_dt_bias,
           gdn_norm, na_rpb, moe_router, moe_w_gate, moe_w_up, moe_w_down):
    xp, xs = x_prompt, x_sample
    bp = xp.shape[0]
    dft_p, dft_s = dft_tables(xp.shape[1]), dft_tables(xs.shape[1])
    ada = ada_params_all(jnp.concatenate([c_ctx[None, :], c], axis=0), w_ada, b_ada)
    new_ak, new_av, new_st, new_nk, new_nv = [], [], [], [], []
    for l in range(DEPTH):
        j = l // 2
        mp = jnp.split(ada[l, :1, None, :], 6, axis=-1)
        ms = jnp.split(ada[l, 1:, None, :], 6, axis=-1)
        mod_p = (norm_mix[l], mp[0], mp[1])
        mod_s = (norm_mix[l], ms[0], ms[1])
        if l % 2 == 0:
            def hyena(zh, tables):
                taps = hyena_taps(zh.shape[1], hy_w1[j], hy_b1[j], hy_w2[j], hy_b2[j], hy_w3[j], hy_freq[j])
                return hyena_mixer_pallas(zh, hy_conv_w[j], hy_conv_b[j], hy_skip[j], tables, hyena_spectrum(tables, *taps))

            w_in = even_w_in[j]
            w_groups = [w_in[:, :A_Q_W], w_in[:, A_Q_W:A_Q_W + A_KV_W], w_in[:, A_Q_W + A_KV_W:A_Q_W + 2 * A_KV_W],
                        w_in[:, A_Q_W + 2 * A_KV_W:]]
            q, k, v, zh = proj_multi(xp, *mod_p, w_groups, [F32] * 4)
            oa = context_attention_pallas(q, k, v, attn_sink[j], A_HEADS, A_KV_HEADS)
            xp_new = proj_concat(oa, hyena(zh, dft_p), even_w_out[j], xp, mp[2])
            new_ak.append(k.reshape(bp, SEQ, A_KV_HEADS, HEAD_DIM))
            new_av.append(v.reshape(bp, SEQ, A_KV_HEADS, HEAD_DIM))
            q, k, v, zh = proj_multi(xs, *mod_s, w_groups, [F32] * 4)
            ck = cache_attn_k[:, j].reshape(DEC_BATCH, PAST_LEN, A_KV_W).astype(BF16)
            cv = cache_attn_v[:, j].reshape(DEC_BATCH, PAST_LEN, A_KV_W).astype(BF16)
            oa = window_attention_pallas(q, k, v, ck, cv, attn_sink[j])
            xs_new = proj_concat(oa, hyena(zh, dft_s), even_w_out[j], xs, ms[2])
        else:
            w_in = odd_w_in[j]
            ab0 = 4 * C_W
            ab_cols = [w_in[:, ab0 + o * C_HEADS:ab0 + (o + 1) * C_HEADS] for o in (0, 2, 1, 3)]
            w_ab = jnp.concatenate(ab_cols + [jnp.zeros((D_MODEL, LANES - 4 * C_HEADS), w_in.dtype)], axis=1)
            n0 = ab0 + 4 * C_HEADS
            w_groups = [w_in[:, :3 * C_W], w_in[:, 3 * C_W:4 * C_W], w_ab,
                        w_in[:, n0:n0 + D_W], w_in[:, n0 + D_W:n0 + 2 * D_W], w_in[:, n0 + 2 * D_W:]]

            def deltanet(zqkv, zab, zg, s0):
                qd, kd, vd, gb = gdn_prep(zqkv, zab, gdn_conv_w[j], gdn_a_log[j], gdn_dt_bias[j])
                return gdn_scan(qd, kd, vd, gb, zg, s0, gdn_norm[j])

            zqkv, zg, zab, nq, nk, nv = proj_multi(xp, *mod_p, w_groups, [F32] * 6)
            oc, st = deltanet(zqkv, zab, zg, jnp.zeros((bp, 2, C_HEADS, HEAD_DIM, HEAD_DIM), F32))
            od = context_attention_pallas(nq, nk, nv, None, D_HEADS, D_HEADS)
            xp_new = proj_concat(oc, od, odd_w_out[j], xp, mp[2])
            new_st.append(st)
            new_nk.append(nk.reshape(bp, SEQ, D_HEADS, HEAD_DIM))
            new_nv.append(nv.reshape(bp, SEQ, D_HEADS, HEAD_DIM))
            zqkv, zg, zab, nq, nk, nv = proj_multi(xs, *mod_s, w_groups, [F32, F32, F32, BF16, BF16, BF16])
            oc, _ = deltanet(zqkv, zab, zg, state_delta[:, j])
            ck = cache_na_k[:, j].reshape(DEC_BATCH, PAST_LEN, D_W).astype(BF16)
            cv = cache_na_v[:, j].reshape(DEC_BATCH, PAST_LEN, D_W).astype(BF16)
            od = neighbourhood_attention_pallas(nq, nk, nv, ck, cv, na_rpb[j])
            xs_new = proj_concat(oc, od, odd_w_out[j], xs, ms[2])
        xp, xs = xp_new, xs_new
        moe = (moe_router[l], moe_w_gate, moe_w_up, moe_w_down, l)
        xp = moe_block(xp, norm_ffn[l], mp[3], mp[4], mp[5], *moe)
        xs = moe_block(xs, norm_ffn[l], ms[3], ms[4], ms[5], *moe)
    y_prompt = final_norm(xp, norm_final)
    y_sample = final_norm(xs, norm_final)
    return (y_prompt, y_sample, jnp.stack(new_ak, axis=1), jnp.stack(new_av, axis=1), jnp.stack(new_st, axis=1),
            jnp.stack(new_nk, axis=1), jnp.stack(new_nv, axis=1))
```

```python
import functools
import math
import jax, jax.numpy as jnp
from jax import lax
import numpy as np
from jax.experimental import pallas as pl
from jax.experimental.pallas import tpu as pltpu

D_MODEL = 1024
BATCH = 32
SEQ = 256
DEPTH = 4
DEC_BATCH = 4
DEC_SEQ = 4096
PAST_LEN = 512

GRID_W = 64
HEAD_DIM = 64
N_EVEN = (DEPTH + 1) // 2
N_ODD = DEPTH // 2
Q_BLOCK = 128
A_HEADS = D_MODEL // 128
A_KV_HEADS = A_HEADS // 4
A_WINDOW = 128
A_BLOCK = 128
ROPE_BASE = 10000.0
HY_CH = D_MODEL // 2
HY_ORDER = 2
HY_SHORT = 3
HY_EMB = 33
HY_FILT_W = 64
HY_FAST_DECAY = 0.3
HY_SLOW_DECAY = 1.5
HY_DECAY_TARGET = 1e-2
HY_SHIFT = 0.05
C_HEADS = D_MODEL // 128
C_SHORT = 3
C_CHUNK = 64
D_HEADS = D_MODEL // 128
NA_KH_MAX = 8
NA_KW = 16
N_EXPERTS = 16
EC_CAPACITY = 2
MOE_D_FF = D_MODEL
EPS = 1e-6
NEG_INF = -1e30

A_Q_W = A_HEADS * HEAD_DIM
A_KV_W = A_KV_HEADS * HEAD_DIM
EVEN_IN = A_Q_W + 2 * A_KV_W + 3 * HY_CH
EVEN_MIX = A_Q_W + HY_CH
C_W = C_HEADS * HEAD_DIM
D_W = D_HEADS * HEAD_DIM
ODD_IN = 4 * C_W + 4 * C_HEADS + 3 * D_W
ODD_MIX = C_W + D_W

VMEM_LIMIT_BYTES = 48 * 1024 * 1024


def _mm_kernel(x_ref, w_ref, o_ref):
    o_ref[...] = jnp.dot(x_ref[...].astype(jnp.bfloat16), w_ref[...], preferred_element_type=jnp.float32)


def pallas_matmul(x, w, tm=256):
    M, K = x.shape
    N = w.shape[1]
    assert M % tm == 0
    return pl.pallas_call(
        _mm_kernel,
        grid=(M // tm,),
        in_specs=[pl.BlockSpec((tm, K), lambda i: (i, 0)), pl.BlockSpec((K, N), lambda i: (0, 0))],
        out_specs=pl.BlockSpec((tm, N), lambda i: (i, 0)),
        out_shape=jax.ShapeDtypeStruct((M, N), jnp.float32),
        compiler_params=pltpu.CompilerParams(dimension_semantics=("arbitrary",), vmem_limit_bytes=VMEM_LIMIT_BYTES),
    )(x, w.astype(jnp.bfloat16))


def proj(x, w):
    B, L, K = x.shape
    return pallas_matmul(x.reshape(B * L, K), w).reshape(B, L, w.shape[1])


def _request_of_tile(tm, L, per_request):
    return (lambda i: ((i * tm) // L, 0, 0)) if per_request else (lambda i: (0, 0, 0))


def _mm_multi_kernel(x_ref, g_ref, shift_ref, scale_ref, *refs):
    n = len(refs) // 2
    x = x_ref[...]
    y = x * lax.rsqrt(jnp.mean(x * x, axis=-1, keepdims=True) + EPS) * g_ref[...]
    h = (y * (1.0 + scale_ref[0]) + shift_ref[0]).astype(jnp.bfloat16)
    for w_ref, o_ref in zip(refs[:n], refs[n:]):
        o_ref[...] = jnp.dot(h, w_ref[...], preferred_element_type=jnp.float32).astype(o_ref.dtype)


def proj_multi(x, g, shift, scale, weights, out_dtypes, tm=256):
    B, L, K = x.shape
    M = B * L
    assert L % tm == 0
    mod_spec = pl.BlockSpec((1, 1, K), _request_of_tile(tm, L, shift.shape[0] == B))
    outs = pl.pallas_call(
        _mm_multi_kernel,
        grid=(M // tm,),
        in_specs=[pl.BlockSpec((tm, K), lambda i: (i, 0)), pl.BlockSpec((1, K), lambda i: (0, 0)), mod_spec, mod_spec]
        + [pl.BlockSpec(w.shape, lambda i: (0, 0)) for w in weights],
        out_specs=[pl.BlockSpec((tm, w.shape[1]), lambda i: (i, 0)) for w in weights],
        out_shape=[jax.ShapeDtypeStruct((M, w.shape[1]), dt) for w, dt in zip(weights, out_dtypes)],
        compiler_params=pltpu.CompilerParams(dimension_semantics=("arbitrary",), vmem_limit_bytes=VMEM_LIMIT_BYTES),
        name="in_projection",
    )(x.reshape(M, K), g.astype(jnp.float32)[None, :], shift, scale, *[w.astype(jnp.bfloat16) for w in weights])
    return [o.reshape(B, L, o.shape[1]) for o in outs]


def _mm2_kernel(a_ref, b_ref, wa_ref, wb_ref, x_ref, gate_ref, o_ref):
    mix = (jnp.dot(a_ref[...].astype(jnp.bfloat16), wa_ref[...], preferred_element_type=jnp.float32)
           + jnp.dot(b_ref[...].astype(jnp.bfloat16), wb_ref[...], preferred_element_type=jnp.float32))
    o_ref[...] = x_ref[...] + gate_ref[0] * mix


def proj_concat(a, b, w, x, gate, tm=256):
    B, L, Ka = a.shape
    Kb = b.shape[2]
    N = w.shape[1]
    M = B * L
    assert L % tm == 0 and w.shape[0] == Ka + Kb
    wb16 = w.astype(jnp.bfloat16)
    out = pl.pallas_call(
        _mm2_kernel,
        grid=(M // tm,),
        in_specs=[pl.BlockSpec((tm, Ka), lambda i: (i, 0)), pl.BlockSpec((tm, Kb), lambda i: (i, 0)),
                  pl.BlockSpec((Ka, N), lambda i: (0, 0)), pl.BlockSpec((Kb, N), lambda i: (0, 0)),
                  pl.BlockSpec((tm, N), lambda i: (i, 0)),
                  pl.BlockSpec((1, 1, N), _request_of_tile(tm, L, gate.shape[0] == B))],
        out_specs=pl.BlockSpec((tm, N), lambda i: (i, 0)),
        out_shape=jax.ShapeDtypeStruct((M, N), jnp.float32),
        compiler_params=pltpu.CompilerParams(dimension_semantics=("arbitrary",), vmem_limit_bytes=VMEM_LIMIT_BYTES),
        name="out_projection",
    )(a.reshape(M, Ka), b.reshape(M, Kb), wb16[:Ka], wb16[Ka:], x.reshape(M, N), gate)
    return out.reshape(B, L, N)


LANES = 128
BF16 = jnp.bfloat16
F32 = jnp.float32


def _dot_nt(a, b):
    return lax.dot_general(a, b, (((1,), (1,)), ((), ())), preferred_element_type=F32)


def _dot(a, b):
    return jnp.dot(a, b, preferred_element_type=F32)


def _low_half(shape):
    return lax.broadcasted_iota(jnp.int32, shape, 1) < HEAD_DIM


def _softmax_pv(units):
    n = range(len(units))
    m = [functools.reduce(jnp.maximum, [s.max(axis=-1, keepdims=True) for s in units[u][0]]) for u in n]
    m = [m[u] if units[u][2] is None else jnp.maximum(m[u], units[u][2]) for u in n]
    p = [[jnp.exp(s - m[u]) for s in units[u][0]] for u in n]
    l = [functools.reduce(lambda a, b: a + b, [x.sum(axis=-1, keepdims=True) for x in p[u]]) for u in n]
    l = [l[u] if units[u][2] is None else l[u] + jnp.exp(units[u][2] - m[u]) for u in n]
    o = [functools.reduce(lambda a, b: a + b, [_dot(x.astype(BF16), v) for x, v in zip(p[u], units[u][1])]) for u in n]
    return [o[u] / l[u] for u in n]


def _place_head(q_slab, src_half, dst_half, low):
    x = q_slab if src_half == dst_half else pltpu.roll(q_slab, HEAD_DIM, axis=1)
    return jnp.where(low if dst_half == 0 else ~low, x, jnp.zeros_like(x))


NA_KEYS = NA_KH_MAX * GRID_W


def _na_kernel(q_ref, k_ref, v_ref, ck_ref, cv_ref, tab_ref, o_ref):
    r = pl.program_id(1)
    rows = k_ref.shape[1] // GRID_W
    rs = jnp.clip(r - NA_KH_MAX // 2, 0, rows - NA_KH_MAX)
    start = pl.multiple_of(rs * GRID_W, GRID_W)
    scale = HEAD_DIM ** -0.5
    low = _low_half((GRID_W, LANES))
    pairs = range(D_HEADS // 2)
    cols = [slice(p * LANES, (p + 1) * LANES) for p in pairs]
    qp = [q_ref[0, :, c] for c in cols]
    kp = [k_ref[0, pl.ds(start, NA_KEYS), c] for c in cols]
    vp = [v_ref[0, pl.ds(start, NA_KEYS), c] for c in cols]
    ckp = [ck_ref[0, :, c] for c in cols]
    cvp = [cv_ref[0, :, c] for c in cols]
    heads = [(p, half) for p in pairs for half in range(2)]
    qm = [jnp.where(low if half == 0 else ~low, qp[p], jnp.zeros_like(qp[p])) for p, half in heads]
    s_loc = [_dot_nt(qm[i], kp[p]) * scale + tab_ref[i, 0] for i, (p, _) in enumerate(heads)]
    s_ctx = [_dot_nt(qm[i], ckp[p]) * scale for i, (p, _) in enumerate(heads)]
    outs = _softmax_pv([([s_loc[i], s_ctx[i]], [vp[p], cvp[p]], None) for i, (p, _) in enumerate(heads)])
    for p in pairs:
        o_ref[0, :, cols[p]] = jnp.where(low, outs[2 * p], outs[2 * p + 1]).astype(o_ref.dtype)


def na_bias_table(rpb):
    col = jnp.arange(GRID_W)
    cs = jnp.clip(col - NA_KW // 2, 0, GRID_W - NA_KW)
    col_ok = (col[None, :] >= cs[:, None]) & (col[None, :] < cs[:, None] + NA_KW)
    coff = jnp.clip(col[None, :] - col[:, None] + NA_KW - 1, 0, 2 * NA_KW - 2)
    base = jnp.where(col_ok[None, None], rpb.astype(F32)[:, :, coff], NEG_INF)
    tab = jnp.stack([base[:, o:o + NA_KH_MAX] for o in range(NA_KH_MAX)], axis=1)
    return tab.transpose(0, 1, 3, 2, 4).reshape(D_HEADS, NA_KH_MAX, GRID_W, NA_KEYS)


def neighbourhood_attention_pallas(q, k, v, ck, cv, rpb):
    B, T, W = q.shape
    P = ck.shape[1]
    rows = T // GRID_W
    assert rows >= NA_KH_MAX and W == D_W
    tab = na_bias_table(rpb)

    def tab_index(b, r):
        rs = jnp.clip(r - NA_KH_MAX // 2, 0, rows - NA_KH_MAX)
        return (0, rs - r + NA_KH_MAX - 1, 0, 0)

    return pl.pallas_call(
        _na_kernel,
        grid=(B, rows),
        in_specs=[
            pl.BlockSpec((1, GRID_W, W), lambda b, r: (b, r, 0)),
            pl.BlockSpec((1, T, W), lambda b, r: (b, 0, 0)),
            pl.BlockSpec((1, T, W), lambda b, r: (b, 0, 0)),
            pl.BlockSpec((1, P, W), lambda b, r: (b, 0, 0)),
            pl.BlockSpec((1, P, W), lambda b, r: (b, 0, 0)),
            pl.BlockSpec((D_HEADS, 1, GRID_W, NA_KEYS), tab_index),
        ],
        out_specs=pl.BlockSpec((1, GRID_W, W), lambda b, r: (b, r, 0)),
        out_shape=jax.ShapeDtypeStruct((B, T, W), BF16),
        compiler_params=pltpu.CompilerParams(dimension_semantics=("arbitrary", "arbitrary"),
                                             vmem_limit_bytes=VMEM_LIMIT_BYTES),
        name="na_attention",
    )(q, k, v, ck, cv, tab)


def rope_tables(T):
    cos, sin = axial_rope(T)
    cos, sin = cos[:, 0, :], sin[:, 0, :]
    cos_t = jnp.concatenate([cos, cos, cos, cos], axis=-1)
    sin_t = jnp.concatenate([-sin, sin, -sin, sin], axis=-1)
    return cos_t, sin_t


def _rope(x, cos_t, sin_t):
    half = HEAD_DIM // 2
    lane = lax.broadcasted_iota(jnp.int32, x.shape, 1)
    first = (lane % HEAD_DIM) < half
    swapped = jnp.where(first, pltpu.roll(x, LANES - half, axis=1), pltpu.roll(x, half, axis=1))
    return x * cos_t + swapped * sin_t


def _win_kernel(sink_ref, q_ref, k_ref, v_ref, ck_ref, cv_ref, cos_ref, sin_ref, o_ref):
    i = pl.program_id(1)
    T = k_ref.shape[1]
    span = 3 * A_BLOCK
    start = pl.multiple_of(jnp.clip((i - 1) * A_BLOCK, 0, T - span), A_BLOCK)
    delta = i * A_BLOCK - start
    q0 = pl.multiple_of(i * A_BLOCK, A_BLOCK)
    scale = HEAD_DIM ** -0.5
    kw = _rope(k_ref[0, pl.ds(start, span), :], cos_ref[pl.ds(start, span), :], sin_ref[pl.ds(start, span), :]).astype(BF16)
    vw = v_ref[0, pl.ds(start, span), :].astype(BF16)
    ck = ck_ref[0]
    cv = cv_ref[0]
    cos_q = cos_ref[pl.ds(q0, A_BLOCK), :]
    sin_q = sin_ref[pl.ds(q0, A_BLOCK), :]
    qi = lax.broadcasted_iota(jnp.int32, (A_BLOCK, span), 0)
    kj = lax.broadcasted_iota(jnp.int32, (A_BLOCK, span), 1)
    band = jnp.abs(kj - delta - qi) <= A_WINDOW
    low = _low_half((A_BLOCK, LANES))
    group = A_HEADS // A_KV_HEADS
    pairs = range(A_HEADS // 2)
    cols = [slice(p * LANES, (p + 1) * LANES) for p in pairs]
    q_slab = [_rope(q_ref[0, :, c], cos_q, sin_q) for c in cols]
    heads = range(A_HEADS)
    kv_of = [h // group for h in heads]
    qm = [_place_head(q_slab[h // 2], h % 2, kv_of[h], low).astype(BF16) for h in heads]
    s_loc = [jnp.where(band, _dot_nt(qm[h], kw) * scale, NEG_INF) for h in heads]
    s_ctx = [_dot_nt(qm[h], ck) * scale for h in heads]
    outs = _softmax_pv([([s_loc[h], s_ctx[h]], [vw, cv], sink_ref[h]) for h in heads])
    outs = [outs[h] if kv_of[h] == h % 2 else pltpu.roll(outs[h], HEAD_DIM, axis=1) for h in heads]
    for p in pairs:
        o_ref[0, :, cols[p]] = jnp.where(low, outs[2 * p], outs[2 * p + 1]).astype(o_ref.dtype)


def window_attention_pallas(q, k, v, ck, cv, sink):
    B, T, QW = q.shape
    KW = k.shape[2]
    P = ck.shape[1]
    assert KW == LANES and QW == A_Q_W and T % A_BLOCK == 0 and T >= 3 * A_BLOCK
    cos_t, sin_t = rope_tables(T)
    return pl.pallas_call(
        _win_kernel,
        grid=(B, T // A_BLOCK),
        in_specs=[
            pl.BlockSpec(memory_space=pltpu.SMEM),
            pl.BlockSpec((1, A_BLOCK, QW), lambda b, i: (b, i, 0)),
            pl.BlockSpec((1, T, KW), lambda b, i: (b, 0, 0)),
            pl.BlockSpec((1, T, KW), lambda b, i: (b, 0, 0)),
            pl.BlockSpec((1, P, KW), lambda b, i: (b, 0, 0)),
            pl.BlockSpec((1, P, KW), lambda b, i: (b, 0, 0)),
            pl.BlockSpec((T, LANES), lambda b, i: (0, 0)),
            pl.BlockSpec((T, LANES), lambda b, i: (0, 0)),
        ],
        out_specs=pl.BlockSpec((1, A_BLOCK, QW), lambda b, i: (b, i, 0)),
        out_shape=jax.ShapeDtypeStruct((B, T, QW), BF16),
        compiler_params=pltpu.CompilerParams(dimension_semantics=("arbitrary", "arbitrary"),
                                             vmem_limit_bytes=VMEM_LIMIT_BYTES),
        name="window_attention",
    )(sink.astype(F32), q, k, v, ck, cv, cos_t, sin_t)


def _ctx_kernel(sink_ref, q_ref, k_ref, v_ref, o_ref, *, n_q_heads, n_kv_heads, use_sink):
    S = q_ref.shape[1]
    scale = HEAD_DIM ** -0.5
    low = _low_half((S, LANES))
    group = n_q_heads // n_kv_heads
    pairs = range(n_q_heads // 2)
    cols = [slice(p * LANES, (p + 1) * LANES) for p in pairs]
    q_slab = [q_ref[0, :, c] for c in cols]
    heads = range(n_q_heads)
    kv_of = [h // group for h in heads]
    kcols = [slice((kv // 2) * LANES, (kv // 2 + 1) * LANES) for kv in kv_of]
    qm = [_place_head(q_slab[h // 2], h % 2, kv_of[h] % 2, low).astype(BF16) for h in heads]
    s = [_dot_nt(qm[h], k_ref[0, :, kcols[h]].astype(BF16)) * scale for h in heads]
    outs = _softmax_pv([([s[h]], [v_ref[0, :, kcols[h]].astype(BF16)], sink_ref[h] if use_sink else None) for h in heads])
    outs = [outs[h] if kv_of[h] % 2 == h % 2 else pltpu.roll(outs[h], HEAD_DIM, axis=1) for h in heads]
    for p in pairs:
        o_ref[0, :, cols[p]] = jnp.where(low, outs[2 * p], outs[2 * p + 1]).astype(o_ref.dtype)


def context_attention_pallas(q, k, v, sink, n_q_heads, n_kv_heads):
    B, S, QW = q.shape
    KW = k.shape[2]
    use_sink = sink is not None
    sink_arr = sink.astype(F32) if use_sink else jnp.zeros((n_q_heads,), F32)
    return pl.pallas_call(
        functools.partial(_ctx_kernel, n_q_heads=n_q_heads, n_kv_heads=n_kv_heads, use_sink=use_sink),
        grid=(B,),
        in_specs=[
            pl.BlockSpec(memory_space=pltpu.SMEM),
            pl.BlockSpec((1, S, QW), lambda b: (b, 0, 0)),
            pl.BlockSpec((1, S, KW), lambda b: (b, 0, 0)),
            pl.BlockSpec((1, S, KW), lambda b: (b, 0, 0)),
        ],
        out_specs=pl.BlockSpec((1, S, QW), lambda b: (b, 0, 0)),
        out_shape=jax.ShapeDtypeStruct((B, S, QW), BF16),
        compiler_params=pltpu.CompilerParams(dimension_semantics=("arbitrary",), vmem_limit_bytes=VMEM_LIMIT_BYTES),
        name="context_attention",
    )(sink_arr, q, k, v)


HIGHEST = lax.Precision.HIGHEST
GDN_TM = 256
GDN_REQS = 2
SUBLANES = 8


def _head_pair_sum_matrix():
    a = lax.broadcasted_iota(jnp.int32, (LANES, LANES), 0) // HEAD_DIM
    b = lax.broadcasted_iota(jnp.int32, (LANES, LANES), 1) // HEAD_DIM
    return (a == b).astype(F32)


def _gdn_prep_kernel(x_ref, prev_ref, next_ref, ab_ref, cw_ref, a_ref, dtb_ref, q_ref, k_ref, v_ref, gb_ref):
    i = pl.program_id(1)
    n = pl.num_programs(1)
    x = x_ref[0]
    tm = x.shape[0]
    row = lax.broadcasted_iota(jnp.int32, x.shape, 0)
    prev_row = jnp.where(i > 0, prev_ref[0, SUBLANES - 1:SUBLANES, :], 0.0)
    next_row = jnp.where(i < n - 1, next_ref[0, 0:1, :], 0.0)
    x_prev = jnp.where(row == 0, prev_row, pltpu.roll(x, 1, axis=0))
    x_next = jnp.where(row == tm - 1, next_row, pltpu.roll(x, tm - 1, axis=0))
    y = x_prev * cw_ref[0:1, :] + x * cw_ref[1:2, :] + x_next * cw_ref[2:3, :]
    y = y * jax.nn.sigmoid(y)
    pmat = _head_pair_sum_matrix()
    for p in range(C_W // LANES):
        qs = y[:, p * LANES:(p + 1) * LANES]
        ks = y[:, C_W + p * LANES:C_W + (p + 1) * LANES]
        q_ref[0, :, p * LANES:(p + 1) * LANES] = qs * lax.rsqrt(jnp.dot(qs * qs, pmat, precision=HIGHEST, preferred_element_type=F32) + EPS) * (HEAD_DIM ** -0.5)
        k_ref[0, :, p * LANES:(p + 1) * LANES] = ks * lax.rsqrt(jnp.dot(ks * ks, pmat, precision=HIGHEST, preferred_element_type=F32) + EPS)
    v_ref[0] = y[:, 2 * C_W:]
    ab = ab_ref[0, :, 0:4 * C_HEADS]
    lane = lax.broadcasted_iota(jnp.int32, ab.shape, 1)
    is_beta = (lane // C_HEADS) % 2 == 1
    t = ab + dtb_ref[...]
    softplus = jnp.maximum(t, 0.0) + jnp.log1p(jnp.exp(-jnp.abs(t)))
    gb = jnp.where(is_beta, jax.nn.sigmoid(ab), -jnp.exp(a_ref[...]) * softplus)
    gb_ref[0, 0] = gb[:, 0:2 * C_HEADS]
    gb_ref[0, 1] = gb[:, 2 * C_HEADS:4 * C_HEADS]


def gdn_prep(zqkv, zab, conv_w, a_log, dt_bias):
    B, L, W3 = zqkv.shape
    tm = GDN_TM
    assert L % tm == 0
    nb = tm // SUBLANES
    zero = jnp.zeros((C_HEADS,), F32)
    a_lane = jnp.concatenate([a_log[0], zero, a_log[1], zero]).astype(F32)[None, :]
    dtb_lane = jnp.concatenate([dt_bias[0], zero, dt_bias[1], zero]).astype(F32)[None, :]
    outs = pl.pallas_call(
        _gdn_prep_kernel,
        grid=(B, L // tm),
        in_specs=[
            pl.BlockSpec((1, tm, W3), lambda b, i: (b, i, 0)),
            pl.BlockSpec((1, SUBLANES, W3), lambda b, i: (b, jnp.maximum(i * nb - 1, 0), 0)),
            pl.BlockSpec((1, SUBLANES, W3), lambda b, i: (b, jnp.minimum((i + 1) * nb, L // SUBLANES - 1), 0)),
            pl.BlockSpec((1, tm, LANES), lambda b, i: (b, i, 0)),
            pl.BlockSpec((C_SHORT, W3), lambda b, i: (0, 0)),
            pl.BlockSpec((1, 4 * C_HEADS), lambda b, i: (0, 0)),
            pl.BlockSpec((1, 4 * C_HEADS), lambda b, i: (0, 0)),
        ],
        out_specs=[
            pl.BlockSpec((1, tm, C_W), lambda b, i: (b, i, 0)),
            pl.BlockSpec((1, tm, C_W), lambda b, i: (b, i, 0)),
            pl.BlockSpec((1, tm, C_W), lambda b, i: (b, i, 0)),
            pl.BlockSpec((1, 2, tm, 2 * C_HEADS), lambda b, i: (b, 0, i, 0)),
        ],
        out_shape=[jax.ShapeDtypeStruct((B, L, C_W), F32)] * 3 + [jax.ShapeDtypeStruct((B, 2, L, 2 * C_HEADS), F32)],
        compiler_params=pltpu.CompilerParams(dimension_semantics=("arbitrary", "arbitrary"),
                                             vmem_limit_bytes=VMEM_LIMIT_BYTES),
        name="gdn_prep",
    )(zqkv, zqkv, zqkv, zab, conv_w.astype(F32), a_lane, dtb_lane)
    return outs


def _gdn_kernel(q_ref, k_ref, v_ref, gb_ref, zg_ref, s0_ref, nw_ref, o_ref, st_ref, s_scr, of_scr):
    d = pl.program_id(1)
    c = pl.program_id(2)
    n = pl.num_programs(2)
    C = C_CHUNK
    fwd = d == 0
    chunk = jnp.where(fwd, c, n - 1 - c)
    r0 = pl.multiple_of(chunk * C, C)

    @pl.when(c == 0)
    def _():
        s_scr[...] = s0_ref[:, 0]

    row = lax.broadcasted_iota(jnp.int32, (C, C), 0)
    col = lax.broadcasted_iota(jnp.int32, (C, C), 1)
    ahead = jnp.where(fwd, row - col, col - row)
    incl = ahead >= 0
    strict = ahead > 0
    incl_f = incl.astype(F32)
    nreq = q_ref.shape[0]
    gc, gc_t, g_last, beta = [], [], [], []
    for r in range(nreq):
        g = gb_ref[r, 0, :, 0:C_HEADS]
        beta.append(gb_ref[r, 0, :, C_HEADS:2 * C_HEADS])
        gc.append(jnp.dot(incl_f, g, precision=HIGHEST, preferred_element_type=F32))
        gc_t.append(gc[r].T)
        g_last.append(jnp.where(fwd, gc[r][C - 1:C, :], gc[r][0:1, :]))
    units = [(r, h) for r in range(nreq) for h in range(C_HEADS)]
    H = range(len(units))
    heads = [slice(h * HEAD_DIM, (h + 1) * HEAD_DIM) for _, h in units]
    s_old = [s_scr[r, h] for r, h in units]
    qs = [q_ref[r, :, heads[i]] for i, (r, _) in enumerate(units)]
    ks = [k_ref[r, :, heads[i]] for i, (r, _) in enumerate(units)]
    vs = [v_ref[r, :, heads[i]] for i, (r, _) in enumerate(units)]
    gcol = [gc[r][:, h:h + 1] for r, h in units]
    bcol = [beta[r][:, h:h + 1] for r, h in units]
    gl = [g_last[r][:, h:h + 1] for r, h in units]
    gamma = [jnp.exp(jnp.where(incl, gcol[i] - gc_t[r][h:h + 1, :], NEG_INF)) for i, (r, h) in enumerate(units)]
    egc = [jnp.exp(gcol[h]) for h in H]
    kb = [ks[h].astype(BF16) for h in H]
    nmat = [jnp.where(strict, _dot_nt(kb[h], kb[h]) * gamma[h], 0.0) * bcol[h] for h in H]
    attn = [(_dot_nt(qs[h].astype(BF16), kb[h]) * gamma[h]).astype(BF16) for h in H]
    xr = row ^ col
    eye = (row == col).astype(F32)
    tinv = [eye - jnp.where((xr >> 1) == 0, nmat[h], 0.0) for h in H]
    for lvl in range(1, 6):
        off_diag = (xr >> lvl) == 1
        wmat = [_dot(jnp.where(off_diag, nmat[h], 0.0).astype(BF16), tinv[h].astype(BF16)).astype(BF16) for h in H]
        tinv = [tinv[h] - _dot(tinv[h].astype(BF16), wmat[h]) for h in H]
    x = [jnp.concatenate([vs[h] * bcol[h], ks[h] * (bcol[h] * egc[h])], axis=1) for h in H]
    x = [x[h] + _dot((tinv[h] - eye).astype(BF16), x[h].astype(BF16)) for h in H]
    sb = [s_old[h].astype(BF16) for h in H]
    v_new = [x[h][:, :HEAD_DIM] - _dot(x[h][:, HEAD_DIM:].astype(BF16), sb[h]) for h in H]
    vb = [v_new[h].astype(BF16) for h in H]
    outs = [_dot((qs[h] * egc[h]).astype(BF16), sb[h]) + _dot(attn[h], vb[h]) for h in H]
    kd = [(ks[h] * jnp.exp(gl[h] - gcol[h])).astype(BF16) for h in H]
    s_new = [s_old[h] * jnp.exp(gl[h]) + lax.dot_general(kd[h], vb[h], (((0,), (0,)), ((), ())), preferred_element_type=F32)
             for h in H]
    for i, (r, h) in enumerate(units):
        s_scr[r, h] = s_new[i]
    o = [jnp.concatenate(outs[r * C_HEADS:(r + 1) * C_HEADS], axis=1) for r in range(nreq)]

    @pl.when(fwd)
    def _():
        for r in range(nreq):
            of_scr[r, pl.ds(r0, C), :] = o[r]

    @pl.when(jnp.logical_not(fwd))
    def _():
        pmat = _head_pair_sum_matrix()
        for r in range(nreq):
            tot = of_scr[r, pl.ds(r0, C), :] + o[r]
            zg = zg_ref[r]
            gate = zg * jax.nn.sigmoid(zg)
            for p in range(C_W // LANES):
                cols = slice(p * LANES, (p + 1) * LANES)
                t = tot[:, cols]
                ms = jnp.dot(t * t, pmat, precision=HIGHEST, preferred_element_type=F32) * (1.0 / HEAD_DIM)
                o_ref[r, :, cols] = (t * lax.rsqrt(ms + EPS) * nw_ref[:, cols] * gate[:, cols]).astype(o_ref.dtype)

    @pl.when(c == n - 1)
    def _():
        st_ref[:, 0] = s_scr[...]


def gdn_scan(q, k, v, gb, zg, s0, norm_w):
    B, L, W = q.shape
    C = C_CHUNK
    n = L // C
    R = min(B, GDN_REQS)
    assert L % C == 0 and W == C_W and B % R == 0
    chunk_of = lambda d, c: jnp.where(d == 0, c, n - 1 - c)
    seq_spec = pl.BlockSpec((R, C, W), lambda b, d, c: (b, chunk_of(d, c), 0))
    state_spec = pl.BlockSpec((R, 1, C_HEADS, HEAD_DIM, HEAD_DIM), lambda b, d, c: (b, d, 0, 0, 0))
    nw = jnp.tile(norm_w.astype(F32), C_HEADS)[None, :]
    return pl.pallas_call(
        _gdn_kernel,
        grid=(B // R, 2, n),
        in_specs=[
            seq_spec, seq_spec, seq_spec,
            pl.BlockSpec((R, 1, C, 2 * C_HEADS), lambda b, d, c: (b, d, chunk_of(d, c), 0)),
            seq_spec,
            state_spec,
            pl.BlockSpec((1, W), lambda b, d, c: (0, 0)),
        ],
        out_specs=[
            pl.BlockSpec((R, C, W), lambda b, d, c: (b, jnp.where(d == 0, n - 1, n - 1 - c), 0)),
            state_spec,
        ],
        out_shape=[jax.ShapeDtypeStruct((B, L, W), BF16), jax.ShapeDtypeStruct(s0.shape, F32)],
        scratch_shapes=[pltpu.VMEM((R, C_HEADS, HEAD_DIM, HEAD_DIM), F32), pltpu.VMEM((R, L, W), F32)],
        compiler_params=pltpu.CompilerParams(dimension_semantics=("arbitrary", "arbitrary", "arbitrary"),
                                             vmem_limit_bytes=VMEM_LIMIT_BYTES),
        name="gdn_scan",
    )(q, k, v, gb, zg, s0.astype(F32), nw)


MOE_TT = 512
MOE_RT = 512


def _moe_router_kernel(x_ref, g_ref, shift_ref, scale_ref, wr_ref, h_ref, aff_ref):
    x = x_ref[0]
    y = x * lax.rsqrt(jnp.mean(x * x, axis=-1, keepdims=True) + EPS) * g_ref[...]
    h = y * (1.0 + scale_ref[0]) + shift_ref[0]
    h_ref[0] = h.astype(BF16)
    logits = jnp.dot(h, wr_ref[...], precision=HIGHEST, preferred_element_type=F32)
    lane = lax.broadcasted_iota(jnp.int32, logits.shape, 1)
    logits = jnp.where(lane < N_EXPERTS, logits, NEG_INF)
    e = jnp.exp(logits - logits.max(axis=-1, keepdims=True))
    aff_ref[0] = e / e.sum(axis=-1, keepdims=True)


def _moe_select_kernel(aff_ref, slot_ref, start_ref, *, cap):
    T = aff_ref.shape[1]
    bits = pltpu.bitcast(aff_ref[0], jnp.int32)

    def bisect(i, v):
        cand = v | (1 << (30 - i))
        cnt = jnp.sum((bits >= cand).astype(jnp.int32), axis=0, keepdims=True)
        return jnp.where(cnt >= cap, cand, v)

    thr = lax.fori_loop(0, 31, bisect, jnp.zeros((1, LANES), jnp.int32))
    gt = (bits > thr).astype(F32)
    eq = (bits == thr).astype(F32)
    need = cap - jnp.sum(gt, axis=0, keepdims=True)
    blk = min(T, MOE_TT)
    r = lax.broadcasted_iota(jnp.int32, (blk, blk), 0)
    c = lax.broadcasted_iota(jnp.int32, (blk, blk), 1)
    before = (c < r).astype(BF16)
    carry_gt = jnp.zeros((1, LANES), F32)
    carry_eq = jnp.zeros((1, LANES), F32)
    n_tiles = T // blk
    start_ref[0] = jnp.zeros(start_ref.shape[1:], F32)
    for b in range(n_tiles):
        rows = slice(b * blk, (b + 1) * blk)
        gt_b, eq_b = gt[rows], eq[rows]
        pos_gt = _dot(before, gt_b.astype(BF16)) + carry_gt
        pos_eq = _dot(before, eq_b.astype(BF16)) + carry_eq
        chosen = gt_b + eq_b * (pos_eq < need).astype(F32)
        slot_ref[0, rows, :] = jnp.where(chosen > 0.5, pos_gt + jnp.minimum(pos_eq, need), -1.0)
        carry_gt = carry_gt + jnp.sum(gt_b, axis=0, keepdims=True)
        carry_eq = carry_eq + jnp.sum(eq_b, axis=0, keepdims=True)
        start_ref[0, b + 1:b + 2, :] = carry_gt + jnp.minimum(carry_eq, need)


def _expert_column(a, e):
    lane = lax.broadcasted_iota(jnp.int32, a.shape, 1)
    return jnp.sum(jnp.where(lane == e, a, 0.0), axis=1, keepdims=True)


MOE_START_ROWS = 16
MOE_WIN = 128


def _one_hot_slots(slot_col, first, width):
    s = lax.broadcasted_iota(jnp.int32, (slot_col.shape[0], width), 1).astype(F32)
    return (slot_col - first == s).astype(BF16)


def _slot_windows(start_ref, b, e, k):
    base = (b * N_EXPERTS + e) * MOE_START_ROWS + k
    lo, hi = start_ref[base], start_ref[base + 1]
    return lo // MOE_WIN, jnp.where(hi > lo, (hi - 1) // MOE_WIN + 1, lo // MOE_WIN)


def _moe_gather_kernel(start_ref, slot_ref, h_ref, xg_ref, acc_ref):
    b, e = pl.program_id(0), pl.program_id(1)
    acc_ref[...] = jnp.zeros_like(acc_ref)
    for k in range(slot_ref.shape[1] // MOE_TT):
        rows = slice(k * MOE_TT, (k + 1) * MOE_TT)
        col = _expert_column(slot_ref[0, rows, :], e)
        w0, w1 = _slot_windows(start_ref, b, e, k)

        def window(w, carry, col=col, rows=rows):
            s0 = pl.multiple_of(w * MOE_WIN, MOE_WIN)
            pt = _one_hot_slots(col, s0.astype(F32), MOE_WIN)
            acc_ref[pl.ds(s0, MOE_WIN), :] += lax.dot_general(pt, h_ref[0, rows, :], (((0,), (0,)), ((), ())),
                                                              preferred_element_type=F32)
            return carry

        lax.fori_loop(w0, w1, window, 0)
    xg_ref[0] = acc_ref[...].astype(xg_ref.dtype)


def _moe_gather_short_kernel(slot_ref, h_ref, xg_ref, *, cap):
    slot = slot_ref[0]
    h = h_ref[0]
    for e in range(N_EXPERTS):
        pt = _one_hot_slots(slot[:, e:e + 1], 0.0, cap)
        xg_ref[e] = lax.dot_general(pt, h, (((0,), (0,)), ((), ())), preferred_element_type=F32).astype(xg_ref.dtype)


def _moe_ffn_kernel(x_ref, wg_ref, wu_ref, wd_ref, y_ref, wg_s, wu_s, wd_s):
    @pl.when(pl.program_id(1) == 0)
    def _():
        wg_s[...] = wg_ref[0].astype(BF16)
        wu_s[...] = wu_ref[0].astype(BF16)
        wd_s[...] = wd_ref[0].astype(BF16)

    x = x_ref[0]
    a = _dot(x, wg_s[...])
    u = _dot(x, wu_s[...])
    hid = (a * jax.nn.sigmoid(a) * u).astype(BF16)
    y_ref[0] = _dot(hid, wd_s[...]).astype(y_ref.dtype)


def _moe_scatter_kernel(start_ref, slot_ref, aff_ref, y_ref, x_ref, gate_ref, o_ref, acc_ref):
    b, k = pl.program_id(0), pl.program_id(1)
    acc_ref[...] = jnp.zeros_like(acc_ref)
    slot, aff = slot_ref[0], aff_ref[0]
    for e in range(N_EXPERTS):
        col, weight = slot[:, e:e + 1], aff[:, e:e + 1]
        w0, w1 = _slot_windows(start_ref, b, e, k)

        def window(w, carry, e=e, col=col, weight=weight):
            s0 = pl.multiple_of(w * MOE_WIN, MOE_WIN)
            pt = _one_hot_slots(col, s0.astype(F32), MOE_WIN)
            acc_ref[...] += weight * _dot(pt, y_ref[e, pl.ds(s0, MOE_WIN), :])
            return carry

        lax.fori_loop(w0, w1, window, 0)
    o_ref[0] = x_ref[0] + gate_ref[0] * acc_ref[...]


def _moe_scatter_short_kernel(slot_ref, aff_ref, y_ref, x_ref, gate_ref, o_ref, *, cap):
    slot, aff = slot_ref[0], aff_ref[0]
    acc = jnp.zeros(x_ref.shape[1:], F32)
    for e in range(N_EXPERTS):
        pt = _one_hot_slots(slot[:, e:e + 1], 0.0, cap)
        acc = acc + aff[:, e:e + 1] * _dot(pt, y_ref[e])
    o_ref[0] = x_ref[0] + gate_ref[0] * acc


def moe_route(x, g, shift, scale, w_router):
    B, T, D = x.shape
    tt = min(T, MOE_TT)
    cap = EC_CAPACITY * T // N_EXPERTS
    per_request = shift.shape[0] == B
    mod_spec = pl.BlockSpec((1, 1, D), (lambda b, k: (b, 0, 0)) if per_request else (lambda b, k: (0, 0, 0)))
    wr = jnp.concatenate([w_router.astype(F32), jnp.zeros((D, LANES - N_EXPERTS), F32)], axis=1)
    h, aff = pl.pallas_call(
        _moe_router_kernel,
        grid=(B, T // tt),
        in_specs=[pl.BlockSpec((1, tt, D), lambda b, k: (b, k, 0)), pl.BlockSpec((1, D), lambda b, k: (0, 0)),
                  mod_spec, mod_spec, pl.BlockSpec((D, LANES), lambda b, k: (0, 0))],
        out_specs=[pl.BlockSpec((1, tt, D), lambda b, k: (b, k, 0)), pl.BlockSpec((1, tt, LANES), lambda b, k: (b, k, 0))],
        out_shape=[jax.ShapeDtypeStruct((B, T, D), BF16), jax.ShapeDtypeStruct((B, T, LANES), F32)],
        compiler_params=pltpu.CompilerParams(dimension_semantics=("arbitrary", "arbitrary"),
                                             vmem_limit_bytes=VMEM_LIMIT_BYTES),
        name="moe_router",
    )(x, g.astype(F32)[None, :], shift, scale, wr)
    assert T // tt < MOE_START_ROWS
    slot, start = pl.pallas_call(
        functools.partial(_moe_select_kernel, cap=cap),
        grid=(B,),
        in_specs=[pl.BlockSpec((1, T, LANES), lambda b: (b, 0, 0))],
        out_specs=[pl.BlockSpec((1, T, LANES), lambda b: (b, 0, 0)),
                   pl.BlockSpec((1, MOE_START_ROWS, LANES), lambda b: (b, 0, 0))],
        out_shape=[jax.ShapeDtypeStruct((B, T, LANES), F32), jax.ShapeDtypeStruct((B, MOE_START_ROWS, LANES), F32)],
        compiler_params=pltpu.CompilerParams(dimension_semantics=("arbitrary",), vmem_limit_bytes=VMEM_LIMIT_BYTES),
        name="moe_select",
    )(aff)
    start = start[:, :, :N_EXPERTS].astype(jnp.int32).transpose(0, 2, 1).reshape(-1)
    return h, aff, slot, start


def moe_gather(h, slot, start):
    B, T, D = h.shape
    cap = EC_CAPACITY * T // N_EXPERTS
    out_shape = jax.ShapeDtypeStruct((N_EXPERTS, B * cap, D), BF16)
    if T <= MOE_TT:
        return pl.pallas_call(
            functools.partial(_moe_gather_short_kernel, cap=cap),
            grid=(B,),
            in_specs=[pl.BlockSpec((1, T, LANES), lambda b: (b, 0, 0)), pl.BlockSpec((1, T, D), lambda b: (b, 0, 0))],
            out_specs=pl.BlockSpec((N_EXPERTS, cap, D), lambda b: (0, b, 0)),
            out_shape=out_shape,
            compiler_params=pltpu.CompilerParams(dimension_semantics=("arbitrary",), vmem_limit_bytes=VMEM_LIMIT_BYTES),
            name="moe_gather_short",
        )(slot, h)
    assert cap % MOE_WIN == 0 and T % MOE_TT == 0
    return pl.pallas_call(
        _moe_gather_kernel,
        grid_spec=pltpu.PrefetchScalarGridSpec(
            num_scalar_prefetch=1,
            grid=(B, N_EXPERTS),
            in_specs=[pl.BlockSpec((1, T, LANES), lambda b, e, st: (b, 0, 0)),
                      pl.BlockSpec((1, T, D), lambda b, e, st: (b, 0, 0))],
            out_specs=pl.BlockSpec((1, cap, D), lambda b, e, st: (e, b, 0)),
            scratch_shapes=[pltpu.VMEM((cap, D), F32)],
        ),
        out_shape=out_shape,
        compiler_params=pltpu.CompilerParams(dimension_semantics=("arbitrary",) * 2, vmem_limit_bytes=VMEM_LIMIT_BYTES),
        name="moe_gather",
    )(start, slot, h)


def moe_ffn(xg, w_gate, w_up, w_down, layer):
    E, R, D = xg.shape
    rt = min(R, MOE_RT)
    w_spec = pl.BlockSpec((None, 1, D, MOE_D_FF), lambda e, r: (layer, e, 0, 0))
    return pl.pallas_call(
        _moe_ffn_kernel,
        grid=(E, R // rt),
        in_specs=[pl.BlockSpec((1, rt, D), lambda e, r: (e, r, 0)), w_spec, w_spec,
                  pl.BlockSpec((None, 1, MOE_D_FF, D), lambda e, r: (layer, e, 0, 0))],
        out_specs=pl.BlockSpec((1, rt, D), lambda e, r: (e, r, 0)),
        out_shape=jax.ShapeDtypeStruct((E, R, D), BF16),
        scratch_shapes=[pltpu.VMEM((D, MOE_D_FF), BF16), pltpu.VMEM((D, MOE_D_FF), BF16), pltpu.VMEM((MOE_D_FF, D), BF16)],
        compiler_params=pltpu.CompilerParams(dimension_semantics=("arbitrary", "arbitrary"),
                                             vmem_limit_bytes=VMEM_LIMIT_BYTES),
        name="moe_ffn",
    )(xg, w_gate, w_up, w_down)


def moe_scatter(slot, aff, start, y, x, gate):
    B, T, D = x.shape
    cap = EC_CAPACITY * T // N_EXPERTS
    per_request = gate.shape[0] == B
    out_shape = jax.ShapeDtypeStruct((B, T, D), F32)
    if T <= MOE_TT:
        whole = lambda b: (b, 0, 0)
        return pl.pallas_call(
            functools.partial(_moe_scatter_short_kernel, cap=cap),
            grid=(B,),
            in_specs=[pl.BlockSpec((1, T, LANES), whole), pl.BlockSpec((1, T, LANES), whole),
                      pl.BlockSpec((N_EXPERTS, cap, D), lambda b: (0, b, 0)), pl.BlockSpec((1, T, D), whole),
                      pl.BlockSpec((1, 1, D), whole if per_request else (lambda b: (0, 0, 0)))],
            out_specs=pl.BlockSpec((1, T, D), whole),
            out_shape=out_shape,
            compiler_params=pltpu.CompilerParams(dimension_semantics=("arbitrary",), vmem_limit_bytes=VMEM_LIMIT_BYTES),
            name="moe_scatter_short",
        )(slot, aff, y, x, gate)
    tt = MOE_TT
    tile = lambda b, k, st: (b, k, 0)
    return pl.pallas_call(
        _moe_scatter_kernel,
        grid_spec=pltpu.PrefetchScalarGridSpec(
            num_scalar_prefetch=1,
            grid=(B, T // tt),
            in_specs=[pl.BlockSpec((1, tt, LANES), tile), pl.BlockSpec((1, tt, LANES), tile),
                      pl.BlockSpec((N_EXPERTS, cap, D), lambda b, k, st: (0, b, 0), pipeline_mode=pl.Buffered(1)),
                      pl.BlockSpec((1, tt, D), tile),
                      pl.BlockSpec((1, 1, D), (lambda b, k, st: (b, 0, 0)) if per_request else (lambda b, k, st: (0, 0, 0)))],
            out_specs=pl.BlockSpec((1, tt, D), tile),
            scratch_shapes=[pltpu.VMEM((tt, D), F32)],
        ),
        out_shape=out_shape,
        compiler_params=pltpu.CompilerParams(dimension_semantics=("arbitrary",) * 2, vmem_limit_bytes=VMEM_LIMIT_BYTES),
        name="moe_scatter",
    )(start, slot, aff, y, x, gate)


def moe_block(x, g, shift, scale, gate, w_router, w_gate, w_up, w_down, layer):
    h, aff, slot, start = moe_route(x, g, shift, scale, w_router)
    y = moe_ffn(moe_gather(h, slot, start), w_gate, w_up, w_down, layer)
    return moe_scatter(slot, aff, start, y, x, gate)


HY_TM = 512
HY_BG = 8
HY_BG_LONG = 4


def dft_tables(L):
    blk = min(L, HY_TM)
    t = jnp.arange(L, dtype=jnp.int32)

    def angles(f):
        return ((f[:, None] * t[None, :]) % (2 * L)).astype(F32) * (math.pi / L)

    a_hi = angles(jnp.arange(0, L, blk, dtype=jnp.int32))[:, None, :]
    a_lo = angles(jnp.arange(blk, dtype=jnp.int32))[None, :, :]
    cos_t = (jnp.cos(a_hi) * jnp.cos(a_lo) - jnp.sin(a_hi) * jnp.sin(a_lo)).reshape(L, L)
    sin_t = (jnp.sin(a_hi) * jnp.cos(a_lo) + jnp.cos(a_hi) * jnp.sin(a_lo)).reshape(L, L)
    out = []
    for tab in (cos_t, sin_t):
        hi = tab.astype(BF16)
        out += [hi, (tab - hi.astype(F32)).astype(BF16)]
    return out


def _split_bf16(x):
    hi = x.astype(BF16)
    return hi, (x - hi.astype(F32)).astype(BF16)


def _mm3(t_hi, t_lo, x_hi, x_lo):
    return _dot(t_hi, x_hi) + _dot(t_hi, x_lo) + _dot(t_lo, x_hi)


def _alt_sign(rows, first_row):
    t = lax.broadcasted_iota(jnp.int32, (rows, 1), 0) + first_row
    return jnp.where(t % 2 == 0, 1.0, -1.0).astype(F32)


def _hy_prep_kernel(x_ref, prev_ref, next_ref, cw_ref, cb_ref, o_ref):
    i = pl.program_id(1)
    n = pl.num_programs(1)
    x = x_ref[0]
    tm = x.shape[0]
    row = lax.broadcasted_iota(jnp.int32, x.shape, 0)
    prev_row = jnp.where(i > 0, prev_ref[0, SUBLANES - 1:SUBLANES, :], 0.0)
    next_row = jnp.where(i < n - 1, next_ref[0, 0:1, :], 0.0)
    x_prev = jnp.where(row == 0, prev_row, pltpu.roll(x, 1, axis=0))
    x_next = jnp.where(row == tm - 1, next_row, pltpu.roll(x, tm - 1, axis=0))
    o_ref[0] = x_prev * cw_ref[0:1, :] + x * cw_ref[1:2, :] + x_next * cw_ref[2:3, :] + cb_ref[...]


def hyena_prep(zh, conv_w, conv_b):
    B, L, W = zh.shape
    tm = min(L, GDN_TM)
    nb = tm // SUBLANES
    return pl.pallas_call(
        _hy_prep_kernel,
        grid=(B, L // tm),
        in_specs=[
            pl.BlockSpec((1, tm, W), lambda b, i: (b, i, 0)),
            pl.BlockSpec((1, SUBLANES, W), lambda b, i: (b, jnp.maximum(i * nb - 1, 0), 0)),
            pl.BlockSpec((1, SUBLANES, W), lambda b, i: (b, jnp.minimum((i + 1) * nb, L // SUBLANES - 1), 0)),
            pl.BlockSpec((HY_SHORT, W), lambda b, i: (0, 0)),
            pl.BlockSpec((1, W), lambda b, i: (0, 0)),
        ],
        out_specs=pl.BlockSpec((1, tm, W), lambda b, i: (b, i, 0)),
        out_shape=jax.ShapeDtypeStruct((B, L, W), F32),
        compiler_params=pltpu.CompilerParams(dimension_semantics=("arbitrary", "arbitrary"),
                                             vmem_limit_bytes=VMEM_LIMIT_BYTES),
        name="hyena_prep",
    )(zh, zh, zh, conv_w.astype(F32), conv_b.astype(F32)[None, :])


def _hy_taps_kernel(feat_ref, w1_ref, b1_ref, w2_ref, b2_ref, fr_ref, w3f_ref, w3b_ref, dec_ref, sum_ref, dif_ref):
    L = feat_ref.shape[0]
    fr = fr_ref[...]
    h = jnp.sin(fr * (jnp.dot(feat_ref[...], w1_ref[...], precision=HIGHEST, preferred_element_type=F32) + b1_ref[...]))
    h = jnp.sin(fr * (jnp.dot(h, w2_ref[...], precision=HIGHEST, preferred_element_type=F32) + b2_ref[...]))
    t = lax.broadcasted_iota(jnp.int32, (L, 1), 0)
    window = jnp.exp(-(t.astype(F32) / (L - 1)) * dec_ref[...]) + HY_SHIFT
    fwd = jnp.dot(h, w3f_ref[...], precision=HIGHEST, preferred_element_type=F32) * window
    bwd = jnp.where(t == 0, 0.0, jnp.dot(h, w3b_ref[...], precision=HIGHEST, preferred_element_type=F32) * window)
    inv = 1.0 / (jnp.sum(jnp.abs(fwd), axis=0, keepdims=True) + jnp.sum(jnp.abs(bwd), axis=0, keepdims=True))
    sum_ref[...] = (fwd + bwd) * inv
    dif_ref[...] = (bwd - fwd) * inv


def hyena_taps(L, w1, b1, w2, b2, w3, freq):
    f32 = F32
    t = jnp.linspace(0.0, 1.0, L, dtype=f32)[:, None]
    bands = (HY_EMB - 1) // 2
    omega = 2.0 * math.pi * jnp.arange(L, dtype=f32)[:, None] / L
    fb = jnp.linspace(1e-4, bands - 1, bands, dtype=f32)[None, :]
    feats = jnp.concatenate([t, jnp.cos(fb * omega), -jnp.sin(fb * omega)], axis=-1)
    max_decay = math.log(HY_DECAY_TARGET) / HY_FAST_DECAY
    min_decay = math.log(HY_DECAY_TARGET) / HY_SLOW_DECAY
    deltas = jnp.abs(jnp.linspace(min_decay, max_decay, HY_CH, dtype=f32))
    dec = jnp.tile(deltas, HY_ORDER)[None, :]
    n_col = HY_ORDER * HY_CH
    cb = 256
    full = lambda shape: pl.BlockSpec(shape, lambda j: (0, 0))
    col = lambda rows: pl.BlockSpec((rows, cb), lambda j: (0, j))
    return pl.pallas_call(
        _hy_taps_kernel,
        grid=(n_col // cb,),
        in_specs=[full((L, HY_EMB)), full((HY_EMB, HY_FILT_W)), full((1, HY_FILT_W)), full((HY_FILT_W, HY_FILT_W)),
                  full((1, HY_FILT_W)), full((1, HY_FILT_W)), col(HY_FILT_W), col(HY_FILT_W), col(1)],
        out_specs=[col(L), col(L)],
        out_shape=[jax.ShapeDtypeStruct((L, n_col), f32)] * 2,
        compiler_params=pltpu.CompilerParams(dimension_semantics=("arbitrary",), vmem_limit_bytes=VMEM_LIMIT_BYTES),
        name="hyena_taps",
    )(feats, w1.astype(f32), b1.astype(f32)[None, :], w2.astype(f32), b2.astype(f32)[None, :], freq.astype(f32)[None, :],
      w3.astype(f32)[:, :n_col], w3.astype(f32)[:, n_col:], dec)


def _hy_spec_kernel(ch_ref, cl_ref, sh_ref, sl_ref, sum_ref, dif_ref, hr_ref, hi_ref, ny_ref, acc_r, acc_i, acc_n):
    m, k = pl.program_id(1), pl.program_id(2)
    tk = sum_ref.shape[0]

    @pl.when(k == 0)
    def _():
        acc_r[...] = jnp.zeros_like(acc_r)
        acc_i[...] = jnp.zeros_like(acc_i)

    @pl.when(jnp.logical_and(k == 0, m == 0))
    def _():
        acc_n[...] = jnp.zeros_like(acc_n)

    a = sum_ref[...]
    acc_r[...] += _mm3(ch_ref[...], cl_ref[...], *_split_bf16(a))
    acc_i[...] += _mm3(sh_ref[...], sl_ref[...], *_split_bf16(dif_ref[...]))

    @pl.when(m == 0)
    def _():
        acc_n[...] += jnp.sum(a * _alt_sign(tk, k * tk), axis=0, keepdims=True)

    @pl.when(k == pl.num_programs(2) - 1)
    def _():
        hr_ref[...] = acc_r[...]
        hi_ref[...] = acc_i[...]
        ny_ref[...] = jnp.broadcast_to(acc_n[...], ny_ref.shape)


def hyena_spectrum(tables, tap_sum, tap_dif):
    L, N = tap_sum.shape
    tm = min(L, HY_TM)
    cb = 512
    tab = pl.BlockSpec((tm, tm), lambda j, m, k: (m, k))
    dat = pl.BlockSpec((tm, cb), lambda j, m, k: (k, j))
    return pl.pallas_call(
        _hy_spec_kernel,
        grid=(N // cb, L // tm, L // tm),
        in_specs=[tab, tab, tab, tab, dat, dat],
        out_specs=[pl.BlockSpec((tm, cb), lambda j, m, k: (m, j)), pl.BlockSpec((tm, cb), lambda j, m, k: (m, j)),
                   pl.BlockSpec((SUBLANES, cb), lambda j, m, k: (0, j))],
        out_shape=[jax.ShapeDtypeStruct((L, N), F32), jax.ShapeDtypeStruct((L, N), F32),
                   jax.ShapeDtypeStruct((SUBLANES, N), F32)],
        scratch_shapes=[pltpu.VMEM((tm, cb), F32), pltpu.VMEM((tm, cb), F32), pltpu.VMEM((1, cb), F32)],
        compiler_params=pltpu.CompilerParams(dimension_semantics=("arbitrary",) * 3, vmem_limit_bytes=VMEM_LIMIT_BYTES),
        name="hyena_spectrum",
    )(*tables, tap_sum, tap_dif)


def _hy_fwd_kernel(ch_ref, sh_ref, u_ref, hr_ref, hi_ref, hny_ref, yr_ref, yi_ref, yny_ref,
                   acc_c, acc_s, acc_n):
    m, k = pl.program_id(1), pl.program_id(2)
    nb, tk = u_ref.shape[0], u_ref.shape[1]
    tm = acc_c.shape[1]

    @pl.when(k == 0)
    def _():
        acc_c[...] = jnp.zeros_like(acc_c)
        acc_s[...] = jnp.zeros_like(acc_s)

    @pl.when(jnp.logical_and(k == 0, m == 0))
    def _():
        acc_n[...] = jnp.zeros_like(acc_n)

    sign = _alt_sign(tk, k * tk)
    for b in range(nb):
        u = u_ref[b]
        ub = u.astype(BF16)
        acc_c[b] += _dot(ch_ref[...], ub)
        acc_s[b] += _dot(sh_ref[...], ub)

        @pl.when(m == 0)
        def _():
            acc_n[b] += jnp.sum(u * sign, axis=0, keepdims=True)

    @pl.when(k == pl.num_programs(2) - 1)
    def _():
        f = lax.broadcasted_iota(jnp.int32, (tm, 1), 0) + m * tm
        dc = jnp.where(f == 0, 0.5, 1.0).astype(F32)
        hr, hi = hr_ref[...], hi_ref[...]
        for b in range(nb):
            xr, xs = acc_c[b], acc_s[b]
            yr_ref[b] = ((xr * hr + xs * hi) * dc).astype(yr_ref.dtype)
            yi_ref[b] = (xr * hi - xs * hr).astype(yi_ref.dtype)
            yny_ref[b] = jnp.broadcast_to(acc_n[b] * hny_ref[0:1, :], yny_ref.shape[1:])


def _hy_inv_kernel(ch_ref, sh_ref, yr_ref, yi_ref, yny_ref, u_ref, xg_ref, skip_ref, o_ref, acc):
    m, k = pl.program_id(1), pl.program_id(2)
    nb = yr_ref.shape[0]
    tm = acc.shape[1]
    L = tm * pl.num_programs(1)

    @pl.when(k == 0)
    def _():
        acc[...] = jnp.zeros_like(acc)

    for b in range(nb):
        acc[b] += _dot(ch_ref[...], yr_ref[b]) - _dot(sh_ref[...], yi_ref[b])

    @pl.when(k == pl.num_programs(2) - 1)
    def _():
        sign = _alt_sign(tm, m * tm)
        for b in range(nb):
            u = u_ref[b]
            y = acc[b] * (1.0 / L) + sign * yny_ref[b, 0:1, :] * (0.5 / L)
            o_ref[b] = (xg_ref[b] * (y + u * skip_ref[...])).astype(o_ref.dtype)


def hyena_long_conv(tables, zf, u, u_col, gate_col, hr, hi, hny, order, skip, out_dtype):
    B, L, _ = u.shape
    C = HY_CH
    tm = min(L, HY_TM)
    bg = min(B, HY_BG_LONG if L >= HY_TM else HY_BG)
    assert B % bg == 0
    grid = (B // bg, L // tm, L // tm)
    tab = pl.BlockSpec((tm, tm), lambda g, m, k: (m, k))
    params = pltpu.CompilerParams(dimension_semantics=("arbitrary",) * 3, vmem_limit_bytes=VMEM_LIMIT_BYTES)
    spec_m = pl.BlockSpec((tm, C), lambda g, m, k: (m, order))
    yr, yi, yny = pl.pallas_call(
        _hy_fwd_kernel,
        grid=grid,
        in_specs=[tab, tab, pl.BlockSpec((bg, tm, C), lambda g, m, k: (g, k, u_col)), spec_m, spec_m,
                  pl.BlockSpec((SUBLANES, C), lambda g, m, k: (0, order))],
        out_specs=[pl.BlockSpec((bg, tm, C), lambda g, m, k: (g, m, 0)), pl.BlockSpec((bg, tm, C), lambda g, m, k: (g, m, 0)),
                   pl.BlockSpec((bg, SUBLANES, C), lambda g, m, k: (g, 0, 0))],
        out_shape=[jax.ShapeDtypeStruct((B, L, C), BF16), jax.ShapeDtypeStruct((B, L, C), BF16),
                   jax.ShapeDtypeStruct((B, SUBLANES, C), F32)],
        scratch_shapes=[pltpu.VMEM((bg, tm, C), F32), pltpu.VMEM((bg, tm, C), F32), pltpu.VMEM((bg, 1, C), F32)],
        compiler_params=params,
        name="hyena_fwd",
    )(tables[0], tables[2], u, hr, hi, hny)
    return pl.pallas_call(
        _hy_inv_kernel,
        grid=grid,
        in_specs=[tab, tab, pl.BlockSpec((bg, tm, C), lambda g, m, k: (g, k, 0)),
                  pl.BlockSpec((bg, tm, C), lambda g, m, k: (g, k, 0)),
                  pl.BlockSpec((bg, SUBLANES, C), lambda g, m, k: (g, 0, 0)),
                  pl.BlockSpec((bg, tm, C), lambda g, m, k: (g, m, u_col)),
                  pl.BlockSpec((bg, tm, C), lambda g, m, k: (g, m, gate_col)),
                  pl.BlockSpec((1, C), lambda g, m, k: (0, 0))],
        out_specs=pl.BlockSpec((bg, tm, C), lambda g, m, k: (g, m, 0)),
        out_shape=jax.ShapeDtypeStruct((B, L, C), out_dtype),
        scratch_shapes=[pltpu.VMEM((bg, tm, C), F32)],
        compiler_params=params,
        name="hyena_inv",
    )(tables[0], tables[2], yr, yi, yny, u, zf, skip.astype(F32)[order][None, :])


def hyena_filter_spectrum(L, w1, b1, w2, b2, w3, freq):
    tables = dft_tables(L)
    tap_sum, tap_dif = hyena_taps(L, w1, b1, w2, b2, w3, freq)
    return tables, hyena_spectrum(tables, tap_sum, tap_dif)


def hyena_mixer_pallas(zh, conv_w, conv_b, skip, tables, spectrum):
    hr, hi, hny = spectrum
    zf = hyena_prep(zh, conv_w, conv_b)
    y1 = hyena_long_conv(tables, zf, zf, 0, 1, hr, hi, hny, 0, skip, F32)
    return hyena_long_conv(tables, zf, y1, 0, 2, hr, hi, hny, 1, skip, BF16)


def rms_norm(x, g):
    xf = x.astype(jnp.float32)
    y = xf * lax.rsqrt(jnp.mean(xf * xf, axis=-1, keepdims=True) + EPS)
    return (y * g.astype(jnp.float32)).astype(x.dtype)


def _ada_kernel(c_ref, w_ref, b_ref, o_ref):
    cnd = c_ref[...]
    act = cnd * jax.nn.sigmoid(cnd)
    o_ref[0] = jnp.dot(act, w_ref[0], precision=HIGHEST, preferred_element_type=F32) + b_ref[0]


def ada_params_all(cond, w_ada, b_ada):
    N, D = cond.shape
    depth, _, W = w_ada.shape
    rows = -(-N // SUBLANES) * SUBLANES
    cond_p = jnp.concatenate([cond.astype(F32), jnp.zeros((rows - N, D), F32)], axis=0)
    cb = D
    out = pl.pallas_call(
        _ada_kernel,
        grid=(depth, W // cb),
        in_specs=[pl.BlockSpec((rows, D), lambda l, j: (0, 0)), pl.BlockSpec((1, D, cb), lambda l, j: (l, 0, j)),
                  pl.BlockSpec((1, 1, cb), lambda l, j: (l, 0, j))],
        out_specs=pl.BlockSpec((1, rows, cb), lambda l, j: (l, 0, j)),
        out_shape=jax.ShapeDtypeStruct((depth, rows, W), F32),
        compiler_params=pltpu.CompilerParams(dimension_semantics=("arbitrary", "arbitrary"),
                                             vmem_limit_bytes=VMEM_LIMIT_BYTES),
        name="ada_params",
    )(cond_p, w_ada.astype(F32), b_ada.astype(F32)[:, None, :])
    return out[:, :N]


def _final_norm_kernel(x_ref, g_ref, o_ref):
    x = x_ref[...]
    o_ref[...] = x * lax.rsqrt(jnp.mean(x * x, axis=-1, keepdims=True) + EPS) * g_ref[...]


def final_norm(x, g, tm=512):
    B, L, D = x.shape
    M = B * L
    assert M % tm == 0
    out = pl.pallas_call(
        _final_norm_kernel,
        grid=(M // tm,),
        in_specs=[pl.BlockSpec((tm, D), lambda i: (i, 0)), pl.BlockSpec((1, D), lambda i: (0, 0))],
        out_specs=pl.BlockSpec((tm, D), lambda i: (i, 0)),
        out_shape=jax.ShapeDtypeStruct((M, D), F32),
        compiler_params=pltpu.CompilerParams(dimension_semantics=("arbitrary",), vmem_limit_bytes=VMEM_LIMIT_BYTES),
        name="final_norm",
    )(x.reshape(M, D), g.astype(F32)[None, :])
    return out.reshape(B, L, D)


def ada_params(cond, w, b):
    m = jax.nn.silu(cond) @ w + b
    return jnp.split(m[:, None, :], 6, axis=-1)


def modulate(x, g, shift, scale):
    return rms_norm(x, g) * (1 + scale) + shift


def axial_rope(T):
    t = jnp.arange(T)
    n_freq = HEAD_DIM // 4
    inv = ROPE_BASE ** (-jnp.arange(n_freq, dtype=jnp.float32) / n_freq)
    ang = jnp.concatenate([(t // GRID_W).astype(jnp.float32)[:, None] * inv,
                           (t % GRID_W).astype(jnp.float32)[:, None] * inv], axis=-1)
    return jnp.cos(ang)[:, None, :], jnp.sin(ang)[:, None, :]


def apply_rope(x, cos, sin):
    xf = x.astype(jnp.float32)
    x1, x2 = jnp.split(xf, 2, axis=-1)
    return jnp.concatenate([x1 * cos - x2 * sin, x2 * cos + x1 * sin], axis=-1).astype(x.dtype)


def softmax_parts(parts, sink=None):
    sizes = [p.shape[-1] for p in parts]
    cols = list(parts)
    if sink is not None:
        cols.append(jnp.broadcast_to(sink, parts[0].shape[:-1] + (1,)))
    p = jax.nn.softmax(jnp.concatenate(cols, axis=-1), axis=-1)
    pieces = jnp.split(p, np.cumsum(sizes).tolist(), axis=-1)
    return pieces[:len(sizes)]


def context_attention(q, k, v, sink):
    B, S, HQ, hd = q.shape
    HK = k.shape[2]
    G = HQ // HK
    nb = S // Q_BLOCK
    scale = hd ** -0.5
    sink_b = None if sink is None else sink.astype(jnp.float32).reshape(1, HK, G, 1, 1)
    qb = q.reshape(B, nb, Q_BLOCK, HK, G, hd).swapaxes(0, 1)

    def one(qi):
        s = jnp.einsum('bqkgd,bskd->bkgqs', qi, k, preferred_element_type=jnp.float32) * scale
        (p,) = softmax_parts([s], sink_b)
        return jnp.einsum('bkgqs,bskd->bqkgd', p.astype(v.dtype), v)

    o = lax.map(one, qb)
    return o.swapaxes(0, 1).reshape(B, S, HQ, hd)


def window_attention(q, k, v, ck, cv, sink):
    B, T, HQ, hd = q.shape
    HK = k.shape[2]
    G = HQ // HK
    nb = T // A_BLOCK
    scale = hd ** -0.5
    qb = q.reshape(B, nb, A_BLOCK, HK, G, hd).swapaxes(0, 1)

    def band(x):
        xp = jnp.pad(x, ((0, 0), (A_BLOCK, A_BLOCK), (0, 0), (0, 0))).reshape(B, nb + 2, A_BLOCK, HK, hd)
        return jnp.concatenate([xp[:, :-2], xp[:, 1:-1], xp[:, 2:]], axis=2).swapaxes(0, 1)

    kb, vb = band(k), band(v)
    qpos = jnp.arange(nb)[:, None, None] * A_BLOCK + jnp.arange(A_BLOCK)[None, :, None]
    kpos = jnp.arange(nb)[:, None, None] * A_BLOCK - A_BLOCK + jnp.arange(3 * A_BLOCK)[None, None, :]
    mask = (jnp.abs(kpos - qpos) <= A_WINDOW) & (kpos >= 0) & (kpos < T)
    sink_b = sink.astype(jnp.float32).reshape(1, HK, G, 1, 1)

    def one(xs):
        qi, ki, vi, mi = xs
        s_loc = jnp.einsum('bqkgd,bskd->bkgqs', qi, ki, preferred_element_type=jnp.float32) * scale
        s_loc = jnp.where(mi[None, None, None], s_loc, NEG_INF)
        s_ctx = jnp.einsum('bqkgd,bpkd->bkgqp', qi, ck, preferred_element_type=jnp.float32) * scale
        p_loc, p_ctx = softmax_parts([s_loc, s_ctx], sink_b)
        return (jnp.einsum('bkgqs,bskd->bqkgd', p_loc.astype(vi.dtype), vi)
                + jnp.einsum('bkgqp,bpkd->bqkgd', p_ctx.astype(cv.dtype), cv))

    o = lax.map(one, (qb, kb, vb, mask))
    return o.swapaxes(0, 1).reshape(B, T, HQ, hd)


def short_conv(x, w):
    K = w.shape[0]
    L = x.shape[1]
    pad = K // 2
    xp = jnp.pad(x, ((0, 0), (pad, pad), (0, 0)))
    return sum(xp[:, i:i + L] * w[i] for i in range(K))


def hyena_filter_bank(L, w1, b1, w2, b2, w3, freq):
    f32 = jnp.float32
    t = jnp.linspace(0.0, 1.0, L, dtype=f32)[:, None]
    bands = (HY_EMB - 1) // 2
    omega = 2.0 * math.pi * jnp.arange(L, dtype=f32)[:, None] / L
    fb = jnp.linspace(1e-4, bands - 1, bands, dtype=f32)[None, :]
    feats = jnp.concatenate([t, jnp.cos(fb * omega), -jnp.sin(fb * omega)], axis=-1)
    fr = freq.astype(f32)
    h = jnp.sin(fr * (feats @ w1.astype(f32) + b1.astype(f32)))
    h = jnp.sin(fr * (h @ w2.astype(f32) + b2.astype(f32)))
    h = (h @ w3.astype(f32)).reshape(L, 2, HY_ORDER, HY_CH)
    max_decay = math.log(HY_DECAY_TARGET) / HY_FAST_DECAY
    min_decay = math.log(HY_DECAY_TARGET) / HY_SLOW_DECAY
    deltas = jnp.abs(jnp.linspace(min_decay, max_decay, HY_CH, dtype=f32))
    h = h * (jnp.exp(-t * deltas) + HY_SHIFT)[:, None, None, :]
    taps = jnp.concatenate([h[:, 0], jnp.zeros((1, HY_ORDER, HY_CH), f32), h[:0:-1, 1]], axis=0)
    taps = taps / jnp.sum(jnp.abs(taps), axis=0, keepdims=True)
    return jnp.fft.rfft(taps, axis=0)


def hyena_mixer(z, conv_w, conv_b, w1, b1, w2, b2, w3, freq, skip):
    L = z.shape[1]
    zf = (short_conv(z, conv_w) + conv_b).astype(jnp.float32)
    v, x1, x2 = jnp.split(zf, 3, axis=-1)
    filt = hyena_filter_bank(L, w1, b1, w2, b2, w3, freq)
    skip = skip.astype(jnp.float32)

    def long_conv(u, o):
        y = jnp.fft.irfft(jnp.fft.rfft(u, n=2 * L, axis=1) * filt[None, :, o], n=2 * L, axis=1)[:, :L]
        return y + u * skip[o]

    y = x1 * long_conv(v, 0)
    y = x2 * long_conv(y, 1)
    return y.astype(z.dtype)


def l2norm(x):
    xf = x.astype(jnp.float32)
    return xf * lax.rsqrt(jnp.sum(xf * xf, axis=-1, keepdims=True) + EPS)


def chunk_gated_delta(q, k, v, g, beta, s0):
    B, L, H, dk = q.shape
    dv = v.shape[-1]
    n = L // C_CHUNK

    def chunks(x):
        x = x.reshape((B, n, C_CHUNK, H) + x.shape[3:])
        return jnp.moveaxis(jnp.moveaxis(x, 1, 0), 3, 2)

    qc, kc, vc, bc = chunks(q), chunks(k), chunks(v), chunks(beta)
    gc = jnp.cumsum(chunks(g), axis=-1)
    tri = jnp.tril(jnp.ones((C_CHUNK, C_CHUNK), bool))
    strict = jnp.tril(jnp.ones((C_CHUNK, C_CHUNK), bool), k=-1)
    gamma = jnp.exp(jnp.where(tri, gc[..., :, None] - gc[..., None, :], NEG_INF))
    kb = kc * bc[..., None]
    a_mat = jnp.where(strict, jnp.einsum('nbhid,nbhjd->nbhij', kb, kc) * gamma, 0.0) + jnp.eye(C_CHUNK, dtype=jnp.float32)
    rhs = jnp.concatenate([vc * bc[..., None], kb * jnp.exp(gc)[..., None]], axis=-1)
    sol = lax.linalg.triangular_solve(a_mat, rhs, left_side=True, lower=True)
    u, w = sol[..., :dv], sol[..., dv:]
    attn = jnp.where(tri, jnp.einsum('nbhid,nbhjd->nbhij', qc, kc) * gamma, 0.0)
    g_last = gc[..., -1]
    q_dec = qc * jnp.exp(gc)[..., None]
    k_dec = kc * jnp.exp(g_last[..., None] - gc)[..., None]

    def step(S, xs):
        u_i, w_i, a_i, qd, kd, gl = xs
        v_new = u_i - jnp.einsum('bhck,bhkv->bhcv', w_i, S)
        o = jnp.einsum('bhck,bhkv->bhcv', qd, S) + jnp.einsum('bhij,bhjv->bhiv', a_i, v_new)
        S = S * jnp.exp(gl)[..., None, None] + jnp.einsum('bhck,bhcv->bhkv', kd, v_new)
        return S, o

    S, o = lax.scan(step, s0.astype(jnp.float32), (u, w, attn, q_dec, k_dec, g_last))
    o = jnp.moveaxis(jnp.moveaxis(o, 2, 3), 0, 1).reshape(B, L, H, dv)
    return o, S


def deltanet_mixer(zq, zk, zv, zg, za, zb, conv_w, a_log, dt_bias, norm_w, s0):
    B, L, _ = zq.shape
    qkv = jax.nn.silu(short_conv(jnp.concatenate([zq, zk, zv], axis=-1), conv_w))
    q, k, v = [t.reshape(B, L, C_HEADS, HEAD_DIM) for t in jnp.split(qkv, 3, axis=-1)]
    q = l2norm(q) * (HEAD_DIM ** -0.5)
    k = l2norm(k)
    v = v.astype(jnp.float32)
    beta = jax.nn.sigmoid(zb.astype(jnp.float32))
    g = -jnp.exp(a_log.astype(jnp.float32)) * jax.nn.softplus(za.astype(jnp.float32) + dt_bias.astype(jnp.float32))
    o_f, s_f = chunk_gated_delta(q, k, v, g[:, :, 0], beta[:, :, 0], s0[:, 0])
    o_b, s_b = chunk_gated_delta(q[:, ::-1], k[:, ::-1], v[:, ::-1], g[:, ::-1, 1], beta[:, ::-1, 1], s0[:, 1])
    o = o_f + o_b[:, ::-1]
    gate = jax.nn.silu(zg.reshape(B, L, C_HEADS, HEAD_DIM).astype(jnp.float32))
    o = rms_norm(o, norm_w) * gate
    return o.reshape(B, L, C_W).astype(zq.dtype), jnp.stack([s_f, s_b], axis=1)


def neighbourhood_attention(q, k, v, ck, cv, rpb):
    B, T, H, hd = q.shape
    rows = T // GRID_W
    kh = min(NA_KH_MAX, rows)
    scale = hd ** -0.5
    r = jnp.arange(rows)
    rs = jnp.clip(r - kh // 2, 0, rows - kh)
    key_rows = rs[:, None] + jnp.arange(kh)[None, :]
    idx = (key_rows[:, :, None] * GRID_W + jnp.arange(GRID_W)).reshape(rows, kh * GRID_W)
    col = jnp.arange(GRID_W)
    cs = jnp.clip(col - NA_KW // 2, 0, GRID_W - NA_KW)
    kcol = jnp.tile(col, kh)
    col_ok = (kcol[None, :] >= cs[:, None]) & (kcol[None, :] < cs[:, None] + NA_KW)
    roff = jnp.repeat(key_rows - r[:, None], GRID_W, axis=1) + NA_KH_MAX - 1
    coff = jnp.clip(kcol[None, :] - col[:, None] + NA_KW - 1, 0, 2 * NA_KW - 2)
    qr = q.reshape(B, rows, GRID_W, H, hd).swapaxes(0, 1)
    rpb_f = rpb.astype(jnp.float32)

    def one(xs):
        qi, ii, ro = xs
        ki = k[:, ii]
        vi = v[:, ii]
        bias = rpb_f[:, ro[None, :], coff]
        s_loc = jnp.einsum('bqhd,bkhd->bhqk', qi, ki, preferred_element_type=jnp.float32) * scale + bias[None]
        s_loc = jnp.where(col_ok[None, None], s_loc, NEG_INF)
        s_ctx = jnp.einsum('bqhd,bphd->bhqp', qi, ck, preferred_element_type=jnp.float32) * scale
        p_loc, p_ctx = softmax_parts([s_loc, s_ctx])
        return (jnp.einsum('bhqk,bkhd->bqhd', p_loc.astype(vi.dtype), vi)
                + jnp.einsum('bhqp,bphd->bqhd', p_ctx.astype(cv.dtype), cv))

    o = lax.map(one, (qr, idx, roff))
    return o.swapaxes(0, 1).reshape(B, T, H, hd)


def expert_choice_ffn(h, w_router, w_gate, w_up, w_down):
    B, T, D = h.shape
    cap = EC_CAPACITY * T // N_EXPERTS
    aff = jax.nn.softmax(jnp.einsum('btd,de->bte', h, w_router, preferred_element_type=jnp.float32), axis=-1)
    gate, idx = lax.top_k(aff.swapaxes(1, 2), cap)
    xg = jax.vmap(lambda hb, ib: hb[ib])(h, idx)
    a = jnp.einsum('becd,edf->becf', xg, w_gate)
    u = jnp.einsum('becd,edf->becf', xg, w_up)
    y = jnp.einsum('becf,efd->becd', jax.nn.silu(a) * u, w_down) * gate[..., None].astype(h.dtype)
    return jax.vmap(lambda yb, ib: jnp.zeros((T, D), yb.dtype).at[ib.reshape(-1)].add(yb.reshape(-1, D)))(y, idx)


def split_even(z):
    B, L = z.shape[:2]
    q = z[..., :A_Q_W].reshape(B, L, A_HEADS, HEAD_DIM)
    k = z[..., A_Q_W:A_Q_W + A_KV_W].reshape(B, L, A_KV_HEADS, HEAD_DIM)
    v = z[..., A_Q_W + A_KV_W:A_Q_W + 2 * A_KV_W].reshape(B, L, A_KV_HEADS, HEAD_DIM)
    return q, k, v, z[..., A_Q_W + 2 * A_KV_W:]


def split_odd(z):
    B, L = z.shape[:2]
    zq, zk, zv, zg = [z[..., i * C_W:(i + 1) * C_W] for i in range(4)]
    off = 4 * C_W
    za = z[..., off:off + 2 * C_HEADS].reshape(B, L, 2, C_HEADS)
    zb = z[..., off + 2 * C_HEADS:off + 4 * C_HEADS].reshape(B, L, 2, C_HEADS)
    off = off + 4 * C_HEADS
    nq, nk, nv = [z[..., off + i * D_W:off + (i + 1) * D_W].reshape(B, L, D_HEADS, HEAD_DIM) for i in range(3)]
    return zq, zk, zv, zg, za, zb, nq, nk, nv


def kernel(x_prompt, x_sample, cache_attn_k, cache_attn_v, state_delta, cache_na_k, cache_na_v,
           c, c_ctx, w_ada, b_ada, norm_mix, norm_ffn, norm_final,
           even_w_in, even_w_out, attn_sink, hy_conv_w, hy_conv_b, hy_w1, hy_b1, hy_w2, hy_b2,
           hy_w3, hy_freq, hy_skip, odd_w_in, odd_w_out, gdn_conv_w, gdn_a_log, gdn_dt_bias,
           gdn_norm, na_rpb, moe_router, moe_w_gate, moe_w_up, moe_w_down):
    xp, xs = x_prompt, x_sample
    bp = xp.shape[0]
    dft_p, dft_s = dft_tables(xp.shape[1]), dft_tables(xs.shape[1])
    ada = ada_params_all(jnp.concatenate([c_ctx[None, :], c], axis=0), w_ada, b_ada)
    new_ak, new_av, new_st, new_nk, new_nv = [], [], [], [], []
    for l in range(DEPTH):
        j = l // 2
        mp = jnp.split(ada[l, :1, None, :], 6, axis=-1)
        ms = jnp.split(ada[l, 1:, None, :], 6, axis=-1)
        mod_p = (norm_mix[l], mp[0], mp[1])
        mod_s = (norm_mix[l], ms[0], ms[1])
        if l % 2 == 0:
            def hyena(zh, tables):
                taps = hyena_taps(zh.shape[1], hy_w1[j], hy_b1[j], hy_w2[j], hy_b2[j], hy_w3[j], hy_freq[j])
                return hyena_mixer_pallas(zh, hy_conv_w[j], hy_conv_b[j], hy_skip[j], tables, hyena_spectrum(tables, *taps))

            w_in = even_w_in[j]
            w_groups = [w_in[:, :A_Q_W], w_in[:, A_Q_W:A_Q_W + A_KV_W], w_in[:, A_Q_W + A_KV_W:A_Q_W + 2 * A_KV_W],
                        w_in[:, A_Q_W + 2 * A_KV_W:]]
            q, k, v, zh = proj_multi(xp, *mod_p, w_groups, [F32] * 4)
            oa = context_attention_pallas(q, k, v, attn_sink[j], A_HEADS, A_KV_HEADS)
            xp_new = proj_concat(oa, hyena(zh, dft_p), even_w_out[j], xp, mp[2])
            new_ak.append(k.reshape(bp, SEQ, A_KV_HEADS, HEAD_DIM))
            new_av.append(v.reshape(bp, SEQ, A_KV_HEADS, HEAD_DIM))
            q, k, v, zh = proj_multi(xs, *mod_s, w_groups, [F32] * 4)
            ck = cache_attn_k[:, j].reshape(DEC_BATCH, PAST_LEN, A_KV_W).astype(BF16)
            cv = cache_attn_v[:, j].reshape(DEC_BATCH, PAST_LEN, A_KV_W).astype(BF16)
            oa = window_attention_pallas(q, k, v, ck, cv, attn_sink[j])
            xs_new = proj_concat(oa, hyena(zh, dft_s), even_w_out[j], xs, ms[2])
        else:
            w_in = odd_w_in[j]
            ab0 = 4 * C_W
            ab_cols = [w_in[:, ab0 + o * C_HEADS:ab0 + (o + 1) * C_HEADS] for o in (0, 2, 1, 3)]
            w_ab = jnp.concatenate(ab_cols + [jnp.zeros((D_MODEL, LANES - 4 * C_HEADS), w_in.dtype)], axis=1)
            n0 = ab0 + 4 * C_HEADS
            w_groups = [w_in[:, :3 * C_W], w_in[:, 3 * C_W:4 * C_W], w_ab,
                        w_in[:, n0:n0 + D_W], w_in[:, n0 + D_W:n0 + 2 * D_W], w_in[:, n0 + 2 * D_W:]]

            def deltanet(zqkv, zab, zg, s0):
                qd, kd, vd, gb = gdn_prep(zqkv, zab, gdn_conv_w[j], gdn_a_log[j], gdn_dt_bias[j])
                return gdn_scan(qd, kd, vd, gb, zg, s0, gdn_norm[j])

            zqkv, zg, zab, nq, nk, nv = proj_multi(xp, *mod_p, w_groups, [F32] * 6)
            oc, st = deltanet(zqkv, zab, zg, jnp.zeros((bp, 2, C_HEADS, HEAD_DIM, HEAD_DIM), F32))
            od = context_attention_pallas(nq, nk, nv, None, D_HEADS, D_HEADS)
            xp_new = proj_concat(oc, od, odd_w_out[j], xp, mp[2])
            new_st.append(st)
            new_nk.append(nk.reshape(bp, SEQ, D_HEADS, HEAD_DIM))
            new_nv.append(nv.reshape(bp, SEQ, D_HEADS, HEAD_DIM))
            zqkv, zg, zab, nq, nk, nv = proj_multi(xs, *mod_s, w_groups, [F32, F32, F32, BF16, BF16, BF16])
            oc, _ = deltanet(zqkv, zab, zg, state_delta[:, j])
            ck = cache_na_k[:, j].reshape(DEC_BATCH, PAST_LEN, D_W).astype(BF16)
            cv = cache_na_v[:, j].reshape(DEC_BATCH, PAST_LEN, D_W).astype(BF16)
            od = neighbourhood_attention_pallas(nq, nk, nv, ck, cv, na_rpb[j])
            xs_new = proj_concat(oc, od, odd_w_out[j], xs, ms[2])
        xp, xs = xp_new, xs_new
        moe = (moe_router[l], moe_w_gate, moe_w_up, moe_w_down, l)
        xp = moe_block(xp, norm_ffn[l], mp[3], mp[4], mp[5], *moe)
        xs = moe_block(xs, norm_ffn[l], ms[3], ms[4], ms[5], *moe)
    y_prompt = final_norm(xp, norm_final)
    y_sample = final_norm(xs, norm_final)
    return (y_prompt, y_sample, jnp.stack(new_ak, axis=1), jnp.stack(new_av, axis=1), jnp.stack(new_st, axis=1),
            jnp.stack(new_nk, axis=1), jnp.stack(new_nv, axis=1))
```

```python
import functools
import math
import jax, jax.numpy as jnp
from jax import lax
import numpy as np
from jax.experimental import pallas as pl
from jax.experimental.pallas import tpu as pltpu

D_MODEL = 1024
BATCH = 32
SEQ = 256
DEPTH = 4
DEC_BATCH = 4
DEC_SEQ = 4096
PAST_LEN = 512

GRID_W = 64
HEAD_DIM = 64
N_EVEN = (DEPTH + 1) // 2
N_ODD = DEPTH // 2
Q_BLOCK = 128
A_HEADS = D_MODEL // 128
A_KV_HEADS = A_HEADS // 4
A_WINDOW = 128
A_BLOCK = 128
ROPE_BASE = 10000.0
HY_CH = D_MODEL // 2
HY_ORDER = 2
HY_SHORT = 3
HY_EMB = 33
HY_FILT_W = 64
HY_FAST_DECAY = 0.3
HY_SLOW_DECAY = 1.5
HY_DECAY_TARGET = 1e-2
HY_SHIFT = 0.05
C_HEADS = D_MODEL // 128
C_SHORT = 3
C_CHUNK = 64
D_HEADS = D_MODEL // 128
NA_KH_MAX = 8
NA_KW = 16
N_EXPERTS = 16
EC_CAPACITY = 2
MOE_D_FF = D_MODEL
EPS = 1e-6
NEG_INF = -1e30

A_Q_W = A_HEADS * HEAD_DIM
A_KV_W = A_KV_HEADS * HEAD_DIM
EVEN_IN = A_Q_W + 2 * A_KV_W + 3 * HY_CH
EVEN_MIX = A_Q_W + HY_CH
C_W = C_HEADS * HEAD_DIM
D_W = D_HEADS * HEAD_DIM
ODD_IN = 4 * C_W + 4 * C_HEADS + 3 * D_W
ODD_MIX = C_W + D_W

VMEM_LIMIT_BYTES = 48 * 1024 * 1024


def _mm_kernel(x_ref, w_ref, o_ref):
    o_ref[...] = jnp.dot(x_ref[...].astype(jnp.bfloat16), w_ref[...], preferred_element_type=jnp.float32)


def pallas_matmul(x, w, tm=256):
    M, K = x.shape
    N = w.shape[1]
    assert M % tm == 0
    return pl.pallas_call(
        _mm_kernel,
        grid=(M // tm,),
        in_specs=[pl.BlockSpec((tm, K), lambda i: (i, 0)), pl.BlockSpec((K, N), lambda i: (0, 0))],
        out_specs=pl.BlockSpec((tm, N), lambda i: (i, 0)),
        out_shape=jax.ShapeDtypeStruct((M, N), jnp.float32),
        compiler_params=pltpu.CompilerParams(dimension_semantics=("arbitrary",), vmem_limit_bytes=VMEM_LIMIT_BYTES),
    )(x, w.astype(jnp.bfloat16))


def proj(x, w):
    B, L, K = x.shape
    return pallas_matmul(x.reshape(B * L, K), w).reshape(B, L, w.shape[1])


def _request_of_tile(tm, L, per_request):
    return (lambda i: ((i * tm) // L, 0, 0)) if per_request else (lambda i: (0, 0, 0))


def _mm_multi_kernel(x_ref, g_ref, shift_ref, scale_ref, *refs):
    n = len(refs) // 2
    x = x_ref[...]
    y = x * lax.rsqrt(jnp.mean(x * x, axis=-1, keepdims=True) + EPS) * g_ref[...]
    h = (y * (1.0 + scale_ref[0]) + shift_ref[0]).astype(jnp.bfloat16)
    for w_ref, o_ref in zip(refs[:n], refs[n:]):
        o_ref[...] = jnp.dot(h, w_ref[...], preferred_element_type=jnp.float32).astype(o_ref.dtype)


def proj_multi(x, g, shift, scale, weights, out_dtypes, tm=256):
    B, L, K = x.shape
    M = B * L
    assert L % tm == 0
    mod_spec = pl.BlockSpec((1, 1, K), _request_of_tile(tm, L, shift.shape[0] == B))
    outs = pl.pallas_call(
        _mm_multi_kernel,
        grid=(M // tm,),
        in_specs=[pl.BlockSpec((tm, K), lambda i: (i, 0)), pl.BlockSpec((1, K), lambda i: (0, 0)), mod_spec, mod_spec]
        + [pl.BlockSpec(w.shape, lambda i: (0, 0)) for w in weights],
        out_specs=[pl.BlockSpec((tm, w.shape[1]), lambda i: (i, 0)) for w in weights],
        out_shape=[jax.ShapeDtypeStruct((M, w.shape[1]), dt) for w, dt in zip(weights, out_dtypes)],
        compiler_params=pltpu.CompilerParams(dimension_semantics=("arbitrary",), vmem_limit_bytes=VMEM_LIMIT_BYTES),
        name="in_projection",
    )(x.reshape(M, K), g.astype(jnp.float32)[None, :], shift, scale, *[w.astype(jnp.bfloat16) for w in weights])
    return [o.reshape(B, L, o.shape[1]) for o in outs]


def _mm2_kernel(a_ref, b_ref, wa_ref, wb_ref, x_ref, gate_ref, o_ref):
    mix = (jnp.dot(a_ref[...].astype(jnp.bfloat16), wa_ref[...], preferred_element_type=jnp.float32)
           + jnp.dot(b_ref[...].astype(jnp.bfloat16), wb_ref[...], preferred_element_type=jnp.float32))
    o_ref[...] = x_ref[...] + gate_ref[0] * mix


def proj_concat(a, b, w, x, gate, tm=256):
    B, L, Ka = a.shape
    Kb = b.shape[2]
    N = w.shape[1]
    M = B * L
    assert L % tm == 0 and w.shape[0] == Ka + Kb
    wb16 = w.astype(jnp.bfloat16)
    out = pl.pallas_call(
        _mm2_kernel,
        grid=(M // tm,),
        in_specs=[pl.BlockSpec((tm, Ka), lambda i: (i, 0)), pl.BlockSpec((tm, Kb), lambda i: (i, 0)),
                  pl.BlockSpec((Ka, N), lambda i: (0, 0)), pl.BlockSpec((Kb, N), lambda i: (0, 0)),
                  pl.BlockSpec((tm, N), lambda i: (i, 0)),
                  pl.BlockSpec((1, 1, N), _request_of_tile(tm, L, gate.shape[0] == B))],
        out_specs=pl.BlockSpec((tm, N), lambda i: (i, 0)),
        out_shape=jax.ShapeDtypeStruct((M, N), jnp.float32),
        compiler_params=pltpu.CompilerParams(dimension_semantics=("arbitrary",), vmem_limit_bytes=VMEM_LIMIT_BYTES),
        name="out_projection",
    )(a.reshape(M, Ka), b.reshape(M, Kb), wb16[:Ka], wb16[Ka:], x.reshape(M, N), gate)
    return out.reshape(B, L, N)


LANES = 128
BF16 = jnp.bfloat16
F32 = jnp.float32


def _dot_nt(a, b):
    return lax.dot_general(a, b, (((1,), (1,)), ((), ())), preferred_element_type=F32)


def _dot(a, b):
    return jnp.dot(a, b, preferred_element_type=F32)


def _low_half(shape):
    return lax.broadcasted_iota(jnp.int32, shape, 1) < HEAD_DIM


def _softmax_pv(units):
    n = range(len(units))
    m = [functools.reduce(jnp.maximum, [s.max(axis=-1, keepdims=True) for s in units[u][0]]) for u in n]
    m = [m[u] if units[u][2] is None else jnp.maximum(m[u], units[u][2]) for u in n]
    p = [[jnp.exp(s - m[u]) for s in units[u][0]] for u in n]
    l = [functools.reduce(lambda a, b: a + b, [x.sum(axis=-1, keepdims=True) for x in p[u]]) for u in n]
    l = [l[u] if units[u][2] is None else l[u] + jnp.exp(units[u][2] - m[u]) for u in n]
    o = [functools.reduce(lambda a, b: a + b, [_dot(x.astype(BF16), v) for x, v in zip(p[u], units[u][1])]) for u in n]
    return [o[u] / l[u] for u in n]


def _place_head(q_slab, src_half, dst_half, low):
    x = q_slab if src_half == dst_half else pltpu.roll(q_slab, HEAD_DIM, axis=1)
    return jnp.where(low if dst_half == 0 else ~low, x, jnp.zeros_like(x))


NA_KEYS = NA_KH_MAX * GRID_W


def _na_kernel(q_ref, k_ref, v_ref, ck_ref, cv_ref, tab_ref, o_ref):
    r = pl.program_id(1)
    rows = k_ref.shape[1] // GRID_W
    rs = jnp.clip(r - NA_KH_MAX // 2, 0, rows - NA_KH_MAX)
    start = pl.multiple_of(rs * GRID_W, GRID_W)
    scale = HEAD_DIM ** -0.5
    low = _low_half((GRID_W, LANES))
    pairs = range(D_HEADS // 2)
    cols = [slice(p * LANES, (p + 1) * LANES) for p in pairs]
    qp = [q_ref[0, :, c] for c in cols]
    kp = [k_ref[0, pl.ds(start, NA_KEYS), c] for c in cols]
    vp = [v_ref[0, pl.ds(start, NA_KEYS), c] for c in cols]
    ckp = [ck_ref[0, :, c] for c in cols]
    cvp = [cv_ref[0, :, c] for c in cols]
    heads = [(p, half) for p in pairs for half in range(2)]
    qm = [jnp.where(low if half == 0 else ~low, qp[p], jnp.zeros_like(qp[p])) for p, half in heads]
    s_loc = [_dot_nt(qm[i], kp[p]) * scale + tab_ref[i, 0] for i, (p, _) in enumerate(heads)]
    s_ctx = [_dot_nt(qm[i], ckp[p]) * scale for i, (p, _) in enumerate(heads)]
    outs = _softmax_pv([([s_loc[i], s_ctx[i]], [vp[p], cvp[p]], None) for i, (p, _) in enumerate(heads)])
    for p in pairs:
        o_ref[0, :, cols[p]] = jnp.where(low, outs[2 * p], outs[2 * p + 1]).astype(o_ref.dtype)


def na_bias_table(rpb):
    col = jnp.arange(GRID_W)
    cs = jnp.clip(col - NA_KW // 2, 0, GRID_W - NA_KW)
    col_ok = (col[None, :] >= cs[:, None]) & (col[None, :] < cs[:, None] + NA_KW)
    coff = jnp.clip(col[None, :] - col[:, None] + NA_KW - 1, 0, 2 * NA_KW - 2)
    base = jnp.where(col_ok[None, None], rpb.astype(F32)[:, :, coff], NEG_INF)
    tab = jnp.stack([base[:, o:o + NA_KH_MAX] for o in range(NA_KH_MAX)], axis=1)
    return tab.transpose(0, 1, 3, 2, 4).reshape(D_HEADS, NA_KH_MAX, GRID_W, NA_KEYS)


def neighbourhood_attention_pallas(q, k, v, ck, cv, rpb):
    B, T, W = q.shape
    P = ck.shape[1]
    rows = T // GRID_W
    assert rows >= NA_KH_MAX and W == D_W
    tab = na_bias_table(rpb)

    def tab_index(b, r):
        rs = jnp.clip(r - NA_KH_MAX // 2, 0, rows - NA_KH_MAX)
        return (0, rs - r + NA_KH_MAX - 1, 0, 0)

    return pl.pallas_call(
        _na_kernel,
        grid=(B, rows),
        in_specs=[
            pl.BlockSpec((1, GRID_W, W), lambda b, r: (b, r, 0)),
            pl.BlockSpec((1, T, W), lambda b, r: (b, 0, 0)),
            pl.BlockSpec((1, T, W), lambda b, r: (b, 0, 0)),
            pl.BlockSpec((1, P, W), lambda b, r: (b, 0, 0)),
            pl.BlockSpec((1, P, W), lambda b, r: (b, 0, 0)),
            pl.BlockSpec((D_HEADS, 1, GRID_W, NA_KEYS), tab_index),
        ],
        out_specs=pl.BlockSpec((1, GRID_W, W), lambda b, r: (b, r, 0)),
        out_shape=jax.ShapeDtypeStruct((B, T, W), BF16),
        compiler_params=pltpu.CompilerParams(dimension_semantics=("arbitrary", "arbitrary"),
                                             vmem_limit_bytes=VMEM_LIMIT_BYTES),
        name="na_attention",
    )(q, k, v, ck, cv, tab)


def rope_tables(T):
    cos, sin = axial_rope(T)
    cos, sin = cos[:, 0, :], sin[:, 0, :]
    cos_t = jnp.concatenate([cos, cos, cos, cos], axis=-1)
    sin_t = jnp.concatenate([-sin, sin, -sin, sin], axis=-1)
    return cos_t, sin_t


def _rope(x, cos_t, sin_t):
    half = HEAD_DIM // 2
    lane = lax.broadcasted_iota(jnp.int32, x.shape, 1)
    first = (lane % HEAD_DIM) < half
    swapped = jnp.where(first, pltpu.roll(x, LANES - half, axis=1), pltpu.roll(x, half, axis=1))
    return x * cos_t + swapped * sin_t


def _win_kernel(sink_ref, q_ref, k_ref, v_ref, ck_ref, cv_ref, cos_ref, sin_ref, o_ref):
    i = pl.program_id(1)
    T = k_ref.shape[1]
    span = 3 * A_BLOCK
    start = pl.multiple_of(jnp.clip((i - 1) * A_BLOCK, 0, T - span), A_BLOCK)
    delta = i * A_BLOCK - start
    q0 = pl.multiple_of(i * A_BLOCK, A_BLOCK)
    scale = HEAD_DIM ** -0.5
    kw = _rope(k_ref[0, pl.ds(start, span), :], cos_ref[pl.ds(start, span), :], sin_ref[pl.ds(start, span), :]).astype(BF16)
    vw = v_ref[0, pl.ds(start, span), :].astype(BF16)
    ck = ck_ref[0]
    cv = cv_ref[0]
    cos_q = cos_ref[pl.ds(q0, A_BLOCK), :]
    sin_q = sin_ref[pl.ds(q0, A_BLOCK), :]
    qi = lax.broadcasted_iota(jnp.int32, (A_BLOCK, span), 0)
    kj = lax.broadcasted_iota(jnp.int32, (A_BLOCK, span), 1)
    band = jnp.abs(kj - delta - qi) <= A_WINDOW
    low = _low_half((A_BLOCK, LANES))
    group = A_HEADS // A_KV_HEADS
    pairs = range(A_HEADS // 2)
    cols = [slice(p * LANES, (p + 1) * LANES) for p in pairs]
    q_slab = [_rope(q_ref[0, :, c], cos_q, sin_q) for c in cols]
    heads = range(A_HEADS)
    kv_of = [h // group for h in heads]
    qm = [_place_head(q_slab[h // 2], h % 2, kv_of[h], low).astype(BF16) for h in heads]
    s_loc = [jnp.where(band, _dot_nt(qm[h], kw) * scale, NEG_INF) for h in heads]
    s_ctx = [_dot_nt(qm[h], ck) * scale for h in heads]
    outs = _softmax_pv([([s_loc[h], s_ctx[h]], [vw, cv], sink_ref[h]) for h in heads])
    outs = [outs[h] if kv_of[h] == h % 2 else pltpu.roll(outs[h], HEAD_DIM, axis=1) for h in heads]
    for p in pairs:
        o_ref[0, :, cols[p]] = jnp.where(low, outs[2 * p], outs[2 * p + 1]).astype(o_ref.dtype)


def window_attention_pallas(q, k, v, ck, cv, sink):
    B, T, QW = q.shape
    KW = k.shape[2]
    P = ck.shape[1]
    assert KW == LANES and QW == A_Q_W and T % A_BLOCK == 0 and T >= 3 * A_BLOCK
    cos_t, sin_t = rope_tables(T)
    return pl.pallas_call(
        _win_kernel,
        grid=(B, T // A_BLOCK),
        in_specs=[
            pl.BlockSpec(memory_space=pltpu.SMEM),
            pl.BlockSpec((1, A_BLOCK, QW), lambda b, i: (b, i, 0)),
            pl.BlockSpec((1, T, KW), lambda b, i: (b, 0, 0)),
            pl.BlockSpec((1, T, KW), lambda b, i: (b, 0, 0)),
            pl.BlockSpec((1, P, KW), lambda b, i: (b, 0, 0)),
            pl.BlockSpec((1, P, KW), lambda b, i: (b, 0, 0)),
            pl.BlockSpec((T, LANES), lambda b, i: (0, 0)),
            pl.BlockSpec((T, LANES), lambda b, i: (0, 0)),
        ],
        out_specs=pl.BlockSpec((1, A_BLOCK, QW), lambda b, i: (b, i, 0)),
        out_shape=jax.ShapeDtypeStruct((B, T, QW), BF16),
        compiler_params=pltpu.CompilerParams(dimension_semantics=("arbitrary", "arbitrary"),
                                             vmem_limit_bytes=VMEM_LIMIT_BYTES),
        name="window_attention",
    )(sink.astype(F32), q, k, v, ck, cv, cos_t, sin_t)


def _ctx_kernel(sink_ref, q_ref, k_ref, v_ref, o_ref, *, n_q_heads, n_kv_heads, use_sink):
    S = q_ref.shape[1]
    scale = HEAD_DIM ** -0.5
    low = _low_half((S, LANES))
    group = n_q_heads // n_kv_heads
    pairs = range(n_q_heads // 2)
    cols = [slice(p * LANES, (p + 1) * LANES) for p in pairs]
    q_slab = [q_ref[0, :, c] for c in cols]
    heads = range(n_q_heads)
    kv_of = [h // group for h in heads]
    kcols = [slice((kv // 2) * LANES, (kv // 2 + 1) * LANES) for kv in kv_of]
    qm = [_place_head(q_slab[h // 2], h % 2, kv_of[h] % 2, low).astype(BF16) for h in heads]
    s = [_dot_nt(qm[h], k_ref[0, :, kcols[h]].astype(BF16)) * scale for h in heads]
    outs = _softmax_pv([([s[h]], [v_ref[0, :, kcols[h]].astype(BF16)], sink_ref[h] if use_sink else None) for h in heads])
    outs = [outs[h] if kv_of[h] % 2 == h % 2 else pltpu.roll(outs[h], HEAD_DIM, axis=1) for h in heads]
    for p in pairs:
        o_ref[0, :, cols[p]] = jnp.where(low, outs[2 * p], outs[2 * p + 1]).astype(o_ref.dtype)


def context_attention_pallas(q, k, v, sink, n_q_heads, n_kv_heads):
    B, S, QW = q.shape
    KW = k.shape[2]
    use_sink = sink is not None
    sink_arr = sink.astype(F32) if use_sink else jnp.zeros((n_q_heads,), F32)
    return pl.pallas_call(
        functools.partial(_ctx_kernel, n_q_heads=n_q_heads, n_kv_heads=n_kv_heads, use_sink=use_sink),
        grid=(B,),
        in_specs=[
            pl.BlockSpec(memory_space=pltpu.SMEM),
            pl.BlockSpec((1, S, QW), lambda b: (b, 0, 0)),
            pl.BlockSpec((1, S, KW), lambda b: (b, 0, 0)),
            pl.BlockSpec((1, S, KW), lambda b: (b, 0, 0)),
        ],
        out_specs=pl.BlockSpec((1, S, QW), lambda b: (b, 0, 0)),
        out_shape=jax.ShapeDtypeStruct((B, S, QW), BF16),
        compiler_params=pltpu.CompilerParams(dimension_semantics=("arbitrary",), vmem_limit_bytes=VMEM_LIMIT_BYTES),
        name="context_attention",
    )(sink_arr, q, k, v)


HIGHEST = lax.Precision.HIGHEST
GDN_TM = 256
GDN_REQS = 2
SUBLANES = 8


def _head_pair_sum_matrix():
    a = lax.broadcasted_iota(jnp.int32, (LANES, LANES), 0) // HEAD_DIM
    b = lax.broadcasted_iota(jnp.int32, (LANES, LANES), 1) // HEAD_DIM
    return (a == b).astype(F32)


def _gdn_prep_kernel(x_ref, prev_ref, next_ref, ab_ref, cw_ref, a_ref, dtb_ref, q_ref, k_ref, v_ref, gb_ref):
    i = pl.program_id(1)
    n = pl.num_programs(1)
    x = x_ref[0]
    tm = x.shape[0]
    row = lax.broadcasted_iota(jnp.int32, x.shape, 0)
    prev_row = jnp.where(i > 0, prev_ref[0, SUBLANES - 1:SUBLANES, :], 0.0)
    next_row = jnp.where(i < n - 1, next_ref[0, 0:1, :], 0.0)
    x_prev = jnp.where(row == 0, prev_row, pltpu.roll(x, 1, axis=0))
    x_next = jnp.where(row == tm - 1, next_row, pltpu.roll(x, tm - 1, axis=0))
    y = x_prev * cw_ref[0:1, :] + x * cw_ref[1:2, :] + x_next * cw_ref[2:3, :]
    y = y * jax.nn.sigmoid(y)
    pmat = _head_pair_sum_matrix()
    for p in range(C_W // LANES):
        qs = y[:, p * LANES:(p + 1) * LANES]
        ks = y[:, C_W + p * LANES:C_W + (p + 1) * LANES]
        q_ref[0, :, p * LANES:(p + 1) * LANES] = qs * lax.rsqrt(jnp.dot(qs * qs, pmat, precision=HIGHEST, preferred_element_type=F32) + EPS) * (HEAD_DIM ** -0.5)
        k_ref[0, :, p * LANES:(p + 1) * LANES] = ks * lax.rsqrt(jnp.dot(ks * ks, pmat, precision=HIGHEST, preferred_element_type=F32) + EPS)
    v_ref[0] = y[:, 2 * C_W:]
    ab = ab_ref[0, :, 0:4 * C_HEADS]
    lane = lax.broadcasted_iota(jnp.int32, ab.shape, 1)
    is_beta = (lane // C_HEADS) % 2 == 1
    t = ab + dtb_ref[...]
    softplus = jnp.maximum(t, 0.0) + jnp.log1p(jnp.exp(-jnp.abs(t)))
    gb = jnp.where(is_beta, jax.nn.sigmoid(ab), -jnp.exp(a_ref[...]) * softplus)
    gb_ref[0, 0] = gb[:, 0:2 * C_HEADS]
    gb_ref[0, 1] = gb[:, 2 * C_HEADS:4 * C_HEADS]


def gdn_prep(zqkv, zab, conv_w, a_log, dt_bias):
    B, L, W3 = zqkv.shape
    tm = GDN_TM
    assert L % tm == 0
    nb = tm // SUBLANES
    zero = jnp.zeros((C_HEADS,), F32)
    a_lane = jnp.concatenate([a_log[0], zero, a_log[1], zero]).astype(F32)[None, :]
    dtb_lane = jnp.concatenate([dt_bias[0], zero, dt_bias[1], zero]).astype(F32)[None, :]
    outs = pl.pallas_call(
        _gdn_prep_kernel,
        grid=(B, L // tm),
        in_specs=[
            pl.BlockSpec((1, tm, W3), lambda b, i: (b, i, 0)),
            pl.BlockSpec((1, SUBLANES, W3), lambda b, i: (b, jnp.maximum(i * nb - 1, 0), 0)),
            pl.BlockSpec((1, SUBLANES, W3), lambda b, i: (b, jnp.minimum((i + 1) * nb, L // SUBLANES - 1), 0)),
            pl.BlockSpec((1, tm, LANES), lambda b, i: (b, i, 0)),
            pl.BlockSpec((C_SHORT, W3), lambda b, i: (0, 0)),
            pl.BlockSpec((1, 4 * C_HEADS), lambda b, i: (0, 0)),
            pl.BlockSpec((1, 4 * C_HEADS), lambda b, i: (0, 0)),
        ],
        out_specs=[
            pl.BlockSpec((1, tm, C_W), lambda b, i: (b, i, 0)),
            pl.BlockSpec((1, tm, C_W), lambda b, i: (b, i, 0)),
            pl.BlockSpec((1, tm, C_W), lambda b, i: (b, i, 0)),
            pl.BlockSpec((1, 2, tm, 2 * C_HEADS), lambda b, i: (b, 0, i, 0)),
        ],
        out_shape=[jax.ShapeDtypeStruct((B, L, C_W), F32)] * 3 + [jax.ShapeDtypeStruct((B, 2, L, 2 * C_HEADS), F32)],
        compiler_params=pltpu.CompilerParams(dimension_semantics=("arbitrary", "arbitrary"),
                                             vmem_limit_bytes=VMEM_LIMIT_BYTES),
        name="gdn_prep",
    )(zqkv, zqkv, zqkv, zab, conv_w.astype(F32), a_lane, dtb_lane)
    return outs


def _gdn_kernel(q_ref, k_ref, v_ref, gb_ref, zg_ref, s0_ref, nw_ref, o_ref, st_ref, s_scr, of_scr):
    d = pl.program_id(1)
    c = pl.program_id(2)
    n = pl.num_programs(2)
    C = C_CHUNK
    fwd = d == 0
    chunk = jnp.where(fwd, c, n - 1 - c)
    r0 = pl.multiple_of(chunk * C, C)

    @pl.when(c == 0)
    def _():
        s_scr[...] = s0_ref[:, 0]

    row = lax.broadcasted_iota(jnp.int32, (C, C), 0)
    col = lax.broadcasted_iota(jnp.int32, (C, C), 1)
    ahead = jnp.where(fwd, row - col, col - row)
    incl = ahead >= 0
    strict = ahead > 0
    incl_f = incl.astype(F32)
    nreq = q_ref.shape[0]
    gc, gc_t, g_last, beta = [], [], [], []
    for r in range(nreq):
        g = gb_ref[r, 0, :, 0:C_HEADS]
        beta.append(gb_ref[r, 0, :, C_HEADS:2 * C_HEADS])
        gc.append(jnp.dot(incl_f, g, precision=HIGHEST, preferred_element_type=F32))
        gc_t.append(gc[r].T)
        g_last.append(jnp.where(fwd, gc[r][C - 1:C, :], gc[r][0:1, :]))
    units = [(r, h) for r in range(nreq) for h in range(C_HEADS)]
    H = range(len(units))
    heads = [slice(h * HEAD_DIM, (h + 1) * HEAD_DIM) for _, h in units]
    s_old = [s_scr[r, h] for r, h in units]
    qs = [q_ref[r, :, heads[i]] for i, (r, _) in enumerate(units)]
    ks = [k_ref[r, :, heads[i]] for i, (r, _) in enumerate(units)]
    vs = [v_ref[r, :, heads[i]] for i, (r, _) in enumerate(units)]
    gcol = [gc[r][:, h:h + 1] for r, h in units]
    bcol = [beta[r][:, h:h + 1] for r, h in units]
    gl = [g_last[r][:, h:h + 1] for r, h in units]
    gamma = [jnp.exp(jnp.where(incl, gcol[i] - gc_t[r][h:h + 1, :], NEG_INF)) for i, (r, h) in enumerate(units)]
    egc = [jnp.exp(gcol[h]) for h in H]
    kb = [ks[h].astype(BF16) for h in H]
    nmat = [jnp.where(strict, _dot_nt(kb[h], kb[h]) * gamma[h], 0.0) * bcol[h] for h in H]
    attn = [(_dot_nt(qs[h].astype(BF16), kb[h]) * gamma[h]).astype(BF16) for h in H]
    xr = row ^ col
    eye = (row == col).astype(F32)
    tinv = [eye - jnp.where((xr >> 1) == 0, nmat[h], 0.0) for h in H]
    for lvl in range(1, 6):
        off_diag = (xr >> lvl) == 1
        wmat = [_dot(jnp.where(off_diag, nmat[h], 0.0).astype(BF16), tinv[h].astype(BF16)).astype(BF16) for h in H]
        tinv = [tinv[h] - _dot(tinv[h].astype(BF16), wmat[h]) for h in H]
    x = [jnp.concatenate([vs[h] * bcol[h], ks[h] * (bcol[h] * egc[h])], axis=1) for h in H]
    x = [x[h] + _dot((tinv[h] - eye).astype(BF16), x[h].astype(BF16)) for h in H]
    sb = [s_old[h].astype(BF16) for h in H]
    v_new = [x[h][:, :HEAD_DIM] - _dot(x[h][:, HEAD_DIM:].astype(BF16), sb[h]) for h in H]
    vb = [v_new[h].astype(BF16) for h in H]
    outs = [_dot((qs[h] * egc[h]).astype(BF16), sb[h]) + _dot(attn[h], vb[h]) for h in H]
    kd = [(ks[h] * jnp.exp(gl[h] - gcol[h])).astype(BF16) for h in H]
    s_new = [s_old[h] * jnp.exp(gl[h]) + lax.dot_general(kd[h], vb[h], (((0,), (0,)), ((), ())), preferred_element_type=F32)
             for h in H]
    for i, (r, h) in enumerate(units):
        s_scr[r, h] = s_new[i]
    o = [jnp.concatenate(outs[r * C_HEADS:(r + 1) * C_HEADS], axis=1) for r in range(nreq)]

    @pl.when(fwd)
    def _():
        for r in range(nreq):
            of_scr[r, pl.ds(r0, C), :] = o[r]

    @pl.when(jnp.logical_not(fwd))
    def _():
        pmat = _head_pair_sum_matrix()
        for r in range(nreq):
            tot = of_scr[r, pl.ds(r0, C), :] + o[r]
            zg = zg_ref[r]
            gate = zg * jax.nn.sigmoid(zg)
            for p in range(C_W // LANES):
                cols = slice(p * LANES, (p + 1) * LANES)
                t = tot[:, cols]
                ms = jnp.dot(t * t, pmat, precision=HIGHEST, preferred_element_type=F32) * (1.0 / HEAD_DIM)
                o_ref[r, :, cols] = (t * lax.rsqrt(ms + EPS) * nw_ref[:, cols] * gate[:, cols]).astype(o_ref.dtype)

    @pl.when(c == n - 1)
    def _():
        st_ref[:, 0] = s_scr[...]


def gdn_scan(q, k, v, gb, zg, s0, norm_w):
    B, L, W = q.shape
    C = C_CHUNK
    n = L // C
    R = min(B, GDN_REQS)
    assert L % C == 0 and W == C_W and B % R == 0
    chunk_of = lambda d, c: jnp.where(d == 0, c, n - 1 - c)
    seq_spec = pl.BlockSpec((R, C, W), lambda b, d, c: (b, chunk_of(d, c), 0))
    state_spec = pl.BlockSpec((R, 1, C_HEADS, HEAD_DIM, HEAD_DIM), lambda b, d, c: (b, d, 0, 0, 0))
    nw = jnp.tile(norm_w.astype(F32), C_HEADS)[None, :]
    return pl.pallas_call(
        _gdn_kernel,
        grid=(B // R, 2, n),
        in_specs=[
            seq_spec, seq_spec, seq_spec,
            pl.BlockSpec((R, 1, C, 2 * C_HEADS), lambda b, d, c: (b, d, chunk_of(d, c), 0)),
            seq_spec,
            state_spec,
            pl.BlockSpec((1, W), lambda b, d, c: (0, 0)),
        ],
        out_specs=[
            pl.BlockSpec((R, C, W), lambda b, d, c: (b, jnp.where(d == 0, n - 1, n - 1 - c), 0)),
            state_spec,
        ],
        out_shape=[jax.ShapeDtypeStruct((B, L, W), BF16), jax.ShapeDtypeStruct(s0.shape, F32)],
        scratch_shapes=[pltpu.VMEM((R, C_HEADS, HEAD_DIM, HEAD_DIM), F32), pltpu.VMEM((R, L, W), F32)],
        compiler_params=pltpu.CompilerParams(dimension_semantics=("arbitrary", "arbitrary", "arbitrary"),
                                             vmem_limit_bytes=VMEM_LIMIT_BYTES),
        name="gdn_scan",
    )(q, k, v, gb, zg, s0.astype(F32), nw)


MOE_TT = 512
MOE_RT = 512


def _moe_router_kernel(x_ref, g_ref, shift_ref, scale_ref, wr_ref, h_ref, aff_ref):
    x = x_ref[0]
    y = x * lax.rsqrt(jnp.mean(x * x, axis=-1, keepdims=True) + EPS) * g_ref[...]
    h = y * (1.0 + scale_ref[0]) + shift_ref[0]
    h_ref[0] = h.astype(BF16)
    logits = jnp.dot(h, wr_ref[...], precision=HIGHEST, preferred_element_type=F32)
    lane = lax.broadcasted_iota(jnp.int32, logits.shape, 1)
    logits = jnp.where(lane < N_EXPERTS, logits, NEG_INF)
    e = jnp.exp(logits - logits.max(axis=-1, keepdims=True))
    aff_ref[0] = e / e.sum(axis=-1, keepdims=True)


def _moe_select_kernel(aff_ref, slot_ref, start_ref, *, cap):
    T = aff_ref.shape[1]
    bits = pltpu.bitcast(aff_ref[0], jnp.int32)

    def bisect(i, v):
        cand = v | (1 << (30 - i))
        cnt = jnp.sum((bits >= cand).astype(jnp.int32), axis=0, keepdims=True)
        return jnp.where(cnt >= cap, cand, v)

    thr = lax.fori_loop(0, 31, bisect, jnp.zeros((1, LANES), jnp.int32))
    gt = (bits > thr).astype(F32)
    eq = (bits == thr).astype(F32)
    need = cap - jnp.sum(gt, axis=0, keepdims=True)
    blk = min(T, MOE_TT)
    r = lax.broadcasted_iota(jnp.int32, (blk, blk), 0)
    c = lax.broadcasted_iota(jnp.int32, (blk, blk), 1)
    before = (c < r).astype(BF16)
    carry_gt = jnp.zeros((1, LANES), F32)
    carry_eq = jnp.zeros((1, LANES), F32)
    n_tiles = T // blk
    start_ref[0] = jnp.zeros(start_ref.shape[1:], F32)
    for b in range(n_tiles):
        rows = slice(b * blk, (b + 1) * blk)
        gt_b, eq_b = gt[rows], eq[rows]
        pos_gt = _dot(before, gt_b.astype(BF16)) + carry_gt
        pos_eq = _dot(before, eq_b.astype(BF16)) + carry_eq
        chosen = gt_b + eq_b * (pos_eq < need).astype(F32)
        slot_ref[0, rows, :] = jnp.where(chosen > 0.5, pos_gt + jnp.minimum(pos_eq, need), -1.0)
        carry_gt = carry_gt + jnp.sum(gt_b, axis=0, keepdims=True)
        carry_eq = carry_eq + jnp.sum(eq_b, axis=0, keepdims=True)
        start_ref[0, b + 1:b + 2, :] = carry_gt + jnp.minimum(carry_eq, need)


def _expert_column(a, e):
    lane = lax.broadcasted_iota(jnp.int32, a.shape, 1)
    return jnp.sum(jnp.where(lane == e, a, 0.0), axis=1, keepdims=True)


MOE_START_ROWS = 16
MOE_WIN = 128


def _one_hot_slots(slot_col, first, width):
    s = lax.broadcasted_iota(jnp.int32, (slot_col.shape[0], width), 1).astype(F32)
    return (slot_col - first == s).astype(BF16)


def _slot_windows(start_ref, b, e, k):
    base = (b * N_EXPERTS + e) * MOE_START_ROWS + k
    lo, hi = start_ref[base], start_ref[base + 1]
    return lo // MOE_WIN, jnp.where(hi > lo, (hi - 1) // MOE_WIN + 1, lo // MOE_WIN)


def _moe_gather_kernel(start_ref, slot_ref, h_ref, xg_ref, acc_ref):
    b, e = pl.program_id(0), pl.program_id(1)
    acc_ref[...] = jnp.zeros_like(acc_ref)
    for k in range(slot_ref.shape[1] // MOE_TT):
        rows = slice(k * MOE_TT, (k + 1) * MOE_TT)
        col = _expert_column(slot_ref[0, rows, :], e)
        w0, w1 = _slot_windows(start_ref, b, e, k)

        def window(w, carry, col=col, rows=rows):
            s0 = pl.multiple_of(w * MOE_WIN, MOE_WIN)
            pt = _one_hot_slots(col, s0.astype(F32), MOE_WIN)
            acc_ref[pl.ds(s0, MOE_WIN), :] += lax.dot_general(pt, h_ref[0, rows, :], (((0,), (0,)), ((), ())),
                                                              preferred_element_type=F32)
            return carry

        lax.fori_loop(w0, w1, window, 0)
    xg_ref[0] = acc_ref[...].astype(xg_ref.dtype)


def _moe_gather_short_kernel(slot_ref, h_ref, xg_ref, *, cap):
    slot = slot_ref[0]
    h = h_ref[0]
    for e in range(N_EXPERTS):
        pt = _one_hot_slots(slot[:, e:e + 1], 0.0, cap)
        xg_ref[e] = lax.dot_general(pt, h, (((0,), (0,)), ((), ())), preferred_element_type=F32).astype(xg_ref.dtype)


def _moe_ffn_kernel(x_ref, wg_ref, wu_ref, wd_ref, y_ref, wg_s, wu_s, wd_s):
    @pl.when(pl.program_id(1) == 0)
    def _():
        wg_s[...] = wg_ref[0].astype(BF16)
        wu_s[...] = wu_ref[0].astype(BF16)
        wd_s[...] = wd_ref[0].astype(BF16)

    x = x_ref[0]
    a = _dot(x, wg_s[...])
    u = _dot(x, wu_s[...])
    hid = (a * jax.nn.sigmoid(a) * u).astype(BF16)
    y_ref[0] = _dot(hid, wd_s[...]).astype(y_ref.dtype)


def _moe_scatter_kernel(start_ref, slot_ref, aff_ref, y_ref, x_ref, gate_ref, o_ref, acc_ref):
    b, k = pl.program_id(0), pl.program_id(1)
    acc_ref[...] = jnp.zeros_like(acc_ref)
    slot, aff = slot_ref[0], aff_ref[0]
    for e in range(N_EXPERTS):
        col, weight = slot[:, e:e + 1], aff[:, e:e + 1]
        w0, w1 = _slot_windows(start_ref, b, e, k)

        def window(w, carry, e=e, col=col, weight=weight):
            s0 = pl.multiple_of(w * MOE_WIN, MOE_WIN)
            pt = _one_hot_slots(col, s0.astype(F32), MOE_WIN)
            acc_ref[...] += weight * _dot(pt, y_ref[e, pl.ds(s0, MOE_WIN), :])
            return carry

        lax.fori_loop(w0, w1, window, 0)
    o_ref[0] = x_ref[0] + gate_ref[0] * acc_ref[...]


def _moe_scatter_short_kernel(slot_ref, aff_ref, y_ref, x_ref, gate_ref, o_ref, *, cap):
    slot, aff = slot_ref[0], aff_ref[0]
    acc = jnp.zeros(x_ref.shape[1:], F32)
    for e in range(N_EXPERTS):
        pt = _one_hot_slots(slot[:, e:e + 1], 0.0, cap)
        acc = acc + aff[:, e:e + 1] * _dot(pt, y_ref[e])
    o_ref[0] = x_ref[0] + gate_ref[0] * acc


def moe_route(x, g, shift, scale, w_router):
    B, T, D = x.shape
    tt = min(T, MOE_TT)
    cap = EC_CAPACITY * T // N_EXPERTS
    per_request = shift.shape[0] == B
    mod_spec = pl.BlockSpec((1, 1, D), (lambda b, k: (b, 0, 0)) if per_request else (lambda b, k: (0, 0, 0)))
    wr = jnp.concatenate([w_router.astype(F32), jnp.zeros((D, LANES - N_EXPERTS), F32)], axis=1)
    h, aff = pl.pallas_call(
        _moe_router_kernel,
        grid=(B, T // tt),
        in_specs=[pl.BlockSpec((1, tt, D), lambda b, k: (b, k, 0)), pl.BlockSpec((1, D), lambda b, k: (0, 0)),
                  mod_spec, mod_spec, pl.BlockSpec((D, LANES), lambda b, k: (0, 0))],
        out_specs=[pl.BlockSpec((1, tt, D), lambda b, k: (b, k, 0)), pl.BlockSpec((1, tt, LANES), lambda b, k: (b, k, 0))],
        out_shape=[jax.ShapeDtypeStruct((B, T, D), BF16), jax.ShapeDtypeStruct((B, T, LANES), F32)],
        compiler_params=pltpu.CompilerParams(dimension_semantics=("arbitrary", "arbitrary"),
                                             vmem_limit_bytes=VMEM_LIMIT_BYTES),
        name="moe_router",
    )(x, g.astype(F32)[None, :], shift, scale, wr)
    assert T // tt < MOE_START_ROWS
    slot, start = pl.pallas_call(
        functools.partial(_moe_select_kernel, cap=cap),
        grid=(B,),
        in_specs=[pl.BlockSpec((1, T, LANES), lambda b: (b, 0, 0))],
        out_specs=[pl.BlockSpec((1, T, LANES), lambda b: (b, 0, 0)),
                   pl.BlockSpec((1, MOE_START_ROWS, LANES), lambda b: (b, 0, 0))],
        out_shape=[jax.ShapeDtypeStruct((B, T, LANES), F32), jax.ShapeDtypeStruct((B, MOE_START_ROWS, LANES), F32)],
        compiler_params=pltpu.CompilerParams(dimension_semantics=("arbitrary",), vmem_limit_bytes=VMEM_LIMIT_BYTES),
        name="moe_select",
    )(aff)
    start = start[:, :, :N_EXPERTS].astype(jnp.int32).transpose(0, 2, 1).reshape(-1)
    return h, aff, slot, start


def moe_gather(h, slot, start):
    B, T, D = h.shape
    cap = EC_CAPACITY * T // N_EXPERTS
    out_shape = jax.ShapeDtypeStruct((N_EXPERTS, B * cap, D), BF16)
    if T <= MOE_TT:
        return pl.pallas_call(
            functools.partial(_moe_gather_short_kernel, cap=cap),
            grid=(B,),
            in_specs=[pl.BlockSpec((1, T, LANES), lambda b: (b, 0, 0)), pl.BlockSpec((1, T, D), lambda b: (b, 0, 0))],
            out_specs=pl.BlockSpec((N_EXPERTS, cap, D), lambda b: (0, b, 0)),
            out_shape=out_shape,
            compiler_params=pltpu.CompilerParams(dimension_semantics=("arbitrary",), vmem_limit_bytes=VMEM_LIMIT_BYTES),
            name="moe_gather_short",
        )(slot, h)
    assert cap % MOE_WIN == 0 and T % MOE_TT == 0
    return pl.pallas_call(
        _moe_gather_kernel,
        grid_spec=pltpu.PrefetchScalarGridSpec(
            num_scalar_prefetch=1,
            grid=(B, N_EXPERTS),
            in_specs=[pl.BlockSpec((1, T, LANES), lambda b, e, st: (b, 0, 0)),
                      pl.BlockSpec((1, T, D), lambda b, e, st: (b, 0, 0))],
            out_specs=pl.BlockSpec((1, cap, D), lambda b, e, st: (e, b, 0)),
            scratch_shapes=[pltpu.VMEM((cap, D), F32)],
        ),
        out_shape=out_shape,
        compiler_params=pltpu.CompilerParams(dimension_semantics=("arbitrary",) * 2, vmem_limit_bytes=VMEM_LIMIT_BYTES),
        name="moe_gather",
    )(start, slot, h)


def moe_ffn(xg, w_gate, w_up, w_down, layer):
    E, R, D = xg.shape
    rt = min(R, MOE_RT)
    w_spec = pl.BlockSpec((None, 1, D, MOE_D_FF), lambda e, r: (layer, e, 0, 0))
    return pl.pallas_call(
        _moe_ffn_kernel,
        grid=(E, R // rt),
        in_specs=[pl.BlockSpec((1, rt, D), lambda e, r: (e, r, 0)), w_spec, w_spec,
                  pl.BlockSpec((None, 1, MOE_D_FF, D), lambda e, r: (layer, e, 0, 0))],
        out_specs=pl.BlockSpec((1, rt, D), lambda e, r: (e, r, 0)),
        out_shape=jax.ShapeDtypeStruct((E, R, D), BF16),
        scratch_shapes=[pltpu.VMEM((D, MOE_D_FF), BF16), pltpu.VMEM((D, MOE_D_FF), BF16), pltpu.VMEM((MOE_D_FF, D), BF16)],
        compiler_params=pltpu.CompilerParams(dimension_semantics=("arbitrary", "arbitrary"),
                                             vmem_limit_bytes=VMEM_LIMIT_BYTES),
        name="moe_ffn",
    )(xg, w_gate, w_up, w_down)


def moe_scatter(slot, aff, start, y, x, gate):
    B, T, D = x.shape
    cap = EC_CAPACITY * T // N_EXPERTS
    per_request = gate.shape[0] == B
    out_shape = jax.ShapeDtypeStruct((B, T, D), F32)
    if T <= MOE_TT:
        whole = lambda b: (b, 0, 0)
        return pl.pallas_call(
            functools.partial(_moe_scatter_short_kernel, cap=cap),
            grid=(B,),
            in_specs=[pl.BlockSpec((1, T, LANES), whole), pl.BlockSpec((1, T, LANES), whole),
                      pl.BlockSpec((N_EXPERTS, cap, D), lambda b: (0, b, 0)), pl.BlockSpec((1, T, D), whole),
                      pl.BlockSpec((1, 1, D), whole if per_request else (lambda b: (0, 0, 0)))],
            out_specs=pl.BlockSpec((1, T, D), whole),
            out_shape=out_shape,
            compiler_params=pltpu.CompilerParams(dimension_semantics=("arbitrary",), vmem_limit_bytes=VMEM_LIMIT_BYTES),
            name="moe_scatter_short",
        )(slot, aff, y, x, gate)
    tt = MOE_TT
    tile = lambda b, k, st: (b, k, 0)
    return pl.pallas_call(
        _moe_scatter_kernel,
        grid_spec=pltpu.PrefetchScalarGridSpec(
            num_scalar_prefetch=1,
            grid=(B, T // tt),
            in_specs=[pl.BlockSpec((1, tt, LANES), tile), pl.BlockSpec((1, tt, LANES), tile),
                      pl.BlockSpec((N_EXPERTS, cap, D), lambda b, k, st: (0, b, 0), pipeline_mode=pl.Buffered(1)),
                      pl.BlockSpec((1, tt, D), tile),
                      pl.BlockSpec((1, 1, D), (lambda b, k, st: (b, 0, 0)) if per_request else (lambda b, k, st: (0, 0, 0)))],
            out_specs=pl.BlockSpec((1, tt, D), tile),
            scratch_shapes=[pltpu.VMEM((tt, D), F32)],
        ),
        out_shape=out_shape,
        compiler_params=pltpu.CompilerParams(dimension_semantics=("arbitrary",) * 2, vmem_limit_bytes=VMEM_LIMIT_BYTES),
        name="moe_scatter",
    )(start, slot, aff, y, x, gate)


def moe_block(x, g, shift, scale, gate, w_router, w_gate, w_up, w_down, layer):
    h, aff, slot, start = moe_route(x, g, shift, scale, w_router)
    y = moe_ffn(moe_gather(h, slot, start), w_gate, w_up, w_down, layer)
    return moe_scatter(slot, aff, start, y, x, gate)


HY_TM = 512
HY_TK = 1024
HY_BG = 8
HY_BG_LONG = 2


def dft_tables(L):
    blk = min(L, HY_TM)
    t = jnp.arange(L, dtype=jnp.int32)

    def angles(f):
        return ((f[:, None] * t[None, :]) % (2 * L)).astype(F32) * (math.pi / L)

    a_hi = angles(jnp.arange(0, L, blk, dtype=jnp.int32))[:, None, :]
    a_lo = angles(jnp.arange(blk, dtype=jnp.int32))[None, :, :]
    cos_t = (jnp.cos(a_hi) * jnp.cos(a_lo) - jnp.sin(a_hi) * jnp.sin(a_lo)).reshape(L, L)
    sin_t = (jnp.sin(a_hi) * jnp.cos(a_lo) + jnp.cos(a_hi) * jnp.sin(a_lo)).reshape(L, L)
    out = []
    for tab in (cos_t, sin_t):
        hi = tab.astype(BF16)
        out += [hi, (tab - hi.astype(F32)).astype(BF16)]
    return out


def _split_bf16(x):
    hi = x.astype(BF16)
    return hi, (x - hi.astype(F32)).astype(BF16)


def _mm3(t_hi, t_lo, x_hi, x_lo):
    return _dot(t_hi, x_hi) + _dot(t_hi, x_lo) + _dot(t_lo, x_hi)


def _alt_sign(rows, first_row):
    t = lax.broadcasted_iota(jnp.int32, (rows, 1), 0) + first_row
    return jnp.where(t % 2 == 0, 1.0, -1.0).astype(F32)


def _hy_prep_kernel(x_ref, prev_ref, next_ref, cw_ref, cb_ref, o_ref, v16_ref):
    i = pl.program_id(1)
    n = pl.num_programs(1)
    x = x_ref[0]
    tm = x.shape[0]
    row = lax.broadcasted_iota(jnp.int32, x.shape, 0)
    prev_row = jnp.where(i > 0, prev_ref[0, SUBLANES - 1:SUBLANES, :], 0.0)
    next_row = jnp.where(i < n - 1, next_ref[0, 0:1, :], 0.0)
    x_prev = jnp.where(row == 0, prev_row, pltpu.roll(x, 1, axis=0))
    x_next = jnp.where(row == tm - 1, next_row, pltpu.roll(x, tm - 1, axis=0))
    y = x_prev * cw_ref[0:1, :] + x * cw_ref[1:2, :] + x_next * cw_ref[2:3, :] + cb_ref[...]
    o_ref[0] = y
    v16_ref[0] = y[:, :HY_CH].astype(BF16)


def hyena_prep(zh, conv_w, conv_b):
    B, L, W = zh.shape
    tm = min(L, GDN_TM)
    nb = tm // SUBLANES
    return pl.pallas_call(
        _hy_prep_kernel,
        grid=(B, L // tm),
        in_specs=[
            pl.BlockSpec((1, tm, W), lambda b, i: (b, i, 0)),
            pl.BlockSpec((1, SUBLANES, W), lambda b, i: (b, jnp.maximum(i * nb - 1, 0), 0)),
            pl.BlockSpec((1, SUBLANES, W), lambda b, i: (b, jnp.minimum((i + 1) * nb, L // SUBLANES - 1), 0)),
            pl.BlockSpec((HY_SHORT, W), lambda b, i: (0, 0)),
            pl.BlockSpec((1, W), lambda b, i: (0, 0)),
        ],
        out_specs=[pl.BlockSpec((1, tm, W), lambda b, i: (b, i, 0)), pl.BlockSpec((1, tm, HY_CH), lambda b, i: (b, i, 0))],
        out_shape=[jax.ShapeDtypeStruct((B, L, W), F32), jax.ShapeDtypeStruct((B, L, HY_CH), BF16)],
        compiler_params=pltpu.CompilerParams(dimension_semantics=("arbitrary", "arbitrary"),
                                             vmem_limit_bytes=VMEM_LIMIT_BYTES),
        name="hyena_prep",
    )(zh, zh, zh, conv_w.astype(F32), conv_b.astype(F32)[None, :])


def _hy_taps_kernel(feat_ref, w1_ref, b1_ref, w2_ref, b2_ref, fr_ref, w3f_ref, w3b_ref, dec_ref, sum_ref, dif_ref):
    L = feat_ref.shape[0]
    fr = fr_ref[...]
    h = jnp.sin(fr * (jnp.dot(feat_ref[...], w1_ref[...], precision=HIGHEST, preferred_element_type=F32) + b1_ref[...]))
    h = jnp.sin(fr * (jnp.dot(h, w2_ref[...], precision=HIGHEST, preferred_element_type=F32) + b2_ref[...]))
    t = lax.broadcasted_iota(jnp.int32, (L, 1), 0)
    window = jnp.exp(-(t.astype(F32) / (L - 1)) * dec_ref[...]) + HY_SHIFT
    fwd = jnp.dot(h, w3f_ref[...], precision=HIGHEST, preferred_element_type=F32) * window
    bwd = jnp.where(t == 0, 0.0, jnp.dot(h, w3b_ref[...], precision=HIGHEST, preferred_element_type=F32) * window)
    inv = 1.0 / (jnp.sum(jnp.abs(fwd), axis=0, keepdims=True) + jnp.sum(jnp.abs(bwd), axis=0, keepdims=True))
    sum_ref[...] = (fwd + bwd) * inv
    dif_ref[...] = (bwd - fwd) * inv


def hyena_taps(L, w1, b1, w2, b2, w3, freq):
    f32 = F32
    t = jnp.linspace(0.0, 1.0, L, dtype=f32)[:, None]
    bands = (HY_EMB - 1) // 2
    omega = 2.0 * math.pi * jnp.arange(L, dtype=f32)[:, None] / L
    fb = jnp.linspace(1e-4, bands - 1, bands, dtype=f32)[None, :]
    feats = jnp.concatenate([t, jnp.cos(fb * omega), -jnp.sin(fb * omega)], axis=-1)
    max_decay = math.log(HY_DECAY_TARGET) / HY_FAST_DECAY
    min_decay = math.log(HY_DECAY_TARGET) / HY_SLOW_DECAY
    deltas = jnp.abs(jnp.linspace(min_decay, max_decay, HY_CH, dtype=f32))
    dec = jnp.tile(deltas, HY_ORDER)[None, :]
    n_col = HY_ORDER * HY_CH
    cb = 256
    full = lambda shape: pl.BlockSpec(shape, lambda j: (0, 0))
    col = lambda rows: pl.BlockSpec((rows, cb), lambda j: (0, j))
    return pl.pallas_call(
        _hy_taps_kernel,
        grid=(n_col // cb,),
        in_specs=[full((L, HY_EMB)), full((HY_EMB, HY_FILT_W)), full((1, HY_FILT_W)), full((HY_FILT_W, HY_FILT_W)),
                  full((1, HY_FILT_W)), full((1, HY_FILT_W)), col(HY_FILT_W), col(HY_FILT_W), col(1)],
        out_specs=[col(L), col(L)],
        out_shape=[jax.ShapeDtypeStruct((L, n_col), f32)] * 2,
        compiler_params=pltpu.CompilerParams(dimension_semantics=("arbitrary",), vmem_limit_bytes=VMEM_LIMIT_BYTES),
        name="hyena_taps",
    )(feats, w1.astype(f32), b1.astype(f32)[None, :], w2.astype(f32), b2.astype(f32)[None, :], freq.astype(f32)[None, :],
      w3.astype(f32)[:, :n_col], w3.astype(f32)[:, n_col:], dec)


def _hy_spec_kernel(ch_ref, cl_ref, sh_ref, sl_ref, sum_ref, dif_ref, hr_ref, hi_ref, ny_ref, acc_r, acc_i, acc_n):
    m, k = pl.program_id(1), pl.program_id(2)
    tk = sum_ref.shape[0]

    @pl.when(k == 0)
    def _():
        acc_r[...] = jnp.zeros_like(acc_r)
        acc_i[...] = jnp.zeros_like(acc_i)

    @pl.when(jnp.logical_and(k == 0, m == 0))
    def _():
        acc_n[...] = jnp.zeros_like(acc_n)

    a = sum_ref[...]
    acc_r[...] += _mm3(ch_ref[...], cl_ref[...], *_split_bf16(a))
    acc_i[...] += _mm3(sh_ref[...], sl_ref[...], *_split_bf16(dif_ref[...]))

    @pl.when(m == 0)
    def _():
        acc_n[...] += jnp.sum(a * _alt_sign(tk, k * tk), axis=0, keepdims=True)

    @pl.when(k == pl.num_programs(2) - 1)
    def _():
        hr_ref[...] = acc_r[...]
        hi_ref[...] = acc_i[...]
        ny_ref[...] = jnp.broadcast_to(acc_n[...], ny_ref.shape)


def hyena_spectrum(tables, tap_sum, tap_dif):
    L, N = tap_sum.shape
    tm = min(L, HY_TM)
    cb = 512
    tab = pl.BlockSpec((tm, tm), lambda j, m, k: (m, k))
    dat = pl.BlockSpec((tm, cb), lambda j, m, k: (k, j))
    return pl.pallas_call(
        _hy_spec_kernel,
        grid=(N // cb, L // tm, L // tm),
        in_specs=[tab, tab, tab, tab, dat, dat],
        out_specs=[pl.BlockSpec((tm, cb), lambda j, m, k: (m, j)), pl.BlockSpec((tm, cb), lambda j, m, k: (m, j)),
                   pl.BlockSpec((SUBLANES, cb), lambda j, m, k: (0, j))],
        out_shape=[jax.ShapeDtypeStruct((L, N), F32), jax.ShapeDtypeStruct((L, N), F32),
                   jax.ShapeDtypeStruct((SUBLANES, N), F32)],
        scratch_shapes=[pltpu.VMEM((tm, cb), F32), pltpu.VMEM((tm, cb), F32), pltpu.VMEM((1, cb), F32)],
        compiler_params=pltpu.CompilerParams(dimension_semantics=("arbitrary",) * 3, vmem_limit_bytes=VMEM_LIMIT_BYTES),
        name="hyena_spectrum",
    )(*tables, tap_sum, tap_dif)


def _hy_fwd_kernel(ch_ref, sh_ref, u_ref, hr_ref, hi_ref, hny_ref, yr_ref, yi_ref, yny_ref,
                   acc_c, acc_s, acc_n):
    m, k = pl.program_id(1), pl.program_id(2)
    nb, tk = u_ref.shape[0], u_ref.shape[1]
    tm = acc_c.shape[1]

    @pl.when(k == 0)
    def _():
        acc_c[...] = jnp.zeros_like(acc_c)
        acc_s[...] = jnp.zeros_like(acc_s)

    @pl.when(jnp.logical_and(k == 0, m == 0))
    def _():
        acc_n[...] = jnp.zeros_like(acc_n)

    sign = _alt_sign(tk, k * tk)
    for b in range(nb):
        ub = u_ref[b]
        acc_c[b] += _dot(ch_ref[...], ub)
        acc_s[b] += _dot(sh_ref[...], ub)

        @pl.when(m == 0)
        def _():
            acc_n[b] += jnp.sum(ub.astype(F32) * sign, axis=0, keepdims=True)

    @pl.when(k == pl.num_programs(2) - 1)
    def _():
        f = lax.broadcasted_iota(jnp.int32, (tm, 1), 0) + m * tm
        dc = jnp.where(f == 0, 0.5, 1.0).astype(F32)
        hr, hi = hr_ref[...], hi_ref[...]
        for b in range(nb):
            xr, xs = acc_c[b], acc_s[b]
            yr_ref[b] = ((xr * hr + xs * hi) * dc).astype(yr_ref.dtype)
            yi_ref[b] = (xr * hi - xs * hr).astype(yi_ref.dtype)
            yny_ref[b] = jnp.broadcast_to(acc_n[b] * hny_ref[0:1, :], yny_ref.shape[1:])


def _hy_inv_kernel(ch_ref, sh_ref, yr_ref, yi_ref, yny_ref, u_ref, xg_ref, skip_ref, *rest):
    o_refs, acc = rest[:-1], rest[-1]
    m, k = pl.program_id(1), pl.program_id(2)
    nb = yr_ref.shape[0]
    tm = acc.shape[1]
    L = tm * pl.num_programs(1)

    @pl.when(k == 0)
    def _():
        acc[...] = jnp.zeros_like(acc)

    for b in range(nb):
        acc[b] += _dot(ch_ref[...], yr_ref[b]) - _dot(sh_ref[...], yi_ref[b])

    @pl.when(k == pl.num_programs(2) - 1)
    def _():
        sign = _alt_sign(tm, m * tm)
        for b in range(nb):
            u = u_ref[b]
            y = acc[b] * (1.0 / L) + sign * yny_ref[b, 0:1, :] * (0.5 / L)
            res = xg_ref[b] * (y + u * skip_ref[...])
            for o_ref in o_refs:
                o_ref[b] = res.astype(o_ref.dtype)


def hyena_long_conv(tables, zf, u, u16, u_col, gate_col, hr, hi, hny, order, skip, out_dtypes):
    B, L, _ = u.shape
    C = HY_CH
    tm = min(L, HY_TM)
    tk = min(L, HY_TK)
    bg = min(B, HY_BG_LONG if L >= HY_TM else HY_BG)
    assert B % bg == 0
    grid = (B // bg, L // tm, L // tk)
    tab = pl.BlockSpec((tm, tk), lambda g, m, k: (m, k))
    params = pltpu.CompilerParams(dimension_semantics=("arbitrary",) * 3, vmem_limit_bytes=VMEM_LIMIT_BYTES)
    spec_m = pl.BlockSpec((tm, C), lambda g, m, k: (m, order))
    yr, yi, yny = pl.pallas_call(
        _hy_fwd_kernel,
        grid=grid,
        in_specs=[tab, tab, pl.BlockSpec((bg, tk, C), lambda g, m, k: (g, k, 0)), spec_m, spec_m,
                  pl.BlockSpec((SUBLANES, C), lambda g, m, k: (0, order))],
        out_specs=[pl.BlockSpec((bg, tm, C), lambda g, m, k: (g, m, 0)), pl.BlockSpec((bg, tm, C), lambda g, m, k: (g, m, 0)),
                   pl.BlockSpec((bg, SUBLANES, C), lambda g, m, k: (g, 0, 0))],
        out_shape=[jax.ShapeDtypeStruct((B, L, C), BF16), jax.ShapeDtypeStruct((B, L, C), BF16),
                   jax.ShapeDtypeStruct((B, SUBLANES, C), F32)],
        scratch_shapes=[pltpu.VMEM((bg, tm, C), F32), pltpu.VMEM((bg, tm, C), F32), pltpu.VMEM((bg, 1, C), F32)],
        compiler_params=params,
        name="hyena_fwd",
    )(tables[0], tables[2], u16, hr, hi, hny)
    return pl.pallas_call(
        _hy_inv_kernel,
        grid=grid,
        in_specs=[tab, tab, pl.BlockSpec((bg, tk, C), lambda g, m, k: (g, k, 0)),
                  pl.BlockSpec((bg, tk, C), lambda g, m, k: (g, k, 0)),
                  pl.BlockSpec((bg, SUBLANES, C), lambda g, m, k: (g, 0, 0)),
                  pl.BlockSpec((bg, tm, C), lambda g, m, k: (g, m, u_col)),
                  pl.BlockSpec((bg, tm, C), lambda g, m, k: (g, m, gate_col)),
                  pl.BlockSpec((1, C), lambda g, m, k: (0, 0))],
        out_specs=[pl.BlockSpec((bg, tm, C), lambda g, m, k: (g, m, 0)) for _ in out_dtypes],
        out_shape=[jax.ShapeDtypeStruct((B, L, C), dt) for dt in out_dtypes],
        scratch_shapes=[pltpu.VMEM((bg, tm, C), F32)],
        compiler_params=params,
        name="hyena_inv",
    )(tables[0], tables[2], yr, yi, yny, u, zf, skip.astype(F32)[order][None, :])


def hyena_filter_spectrum(L, w1, b1, w2, b2, w3, freq):
    tables = dft_tables(L)
    tap_sum, tap_dif = hyena_taps(L, w1, b1, w2, b2, w3, freq)
    return tables, hyena_spectrum(tables, tap_sum, tap_dif)


def hyena_mixer_pallas(zh, conv_w, conv_b, skip, tables, spectrum):
    hr, hi, hny = spectrum
    zf, v16 = hyena_prep(zh, conv_w, conv_b)
    y1, y1_16 = hyena_long_conv(tables, zf, zf, v16, 0, 1, hr, hi, hny, 0, skip, [F32, BF16])
    return hyena_long_conv(tables, zf, y1, y1_16, 0, 2, hr, hi, hny, 1, skip, [BF16])[0]


def rms_norm(x, g):
    xf = x.astype(jnp.float32)
    y = xf * lax.rsqrt(jnp.mean(xf * xf, axis=-1, keepdims=True) + EPS)
    return (y * g.astype(jnp.float32)).astype(x.dtype)


def _ada_kernel(c_ref, w_ref, b_ref, o_ref):
    cnd = c_ref[...]
    act = cnd * jax.nn.sigmoid(cnd)
    o_ref[0] = jnp.dot(act, w_ref[0], precision=HIGHEST, preferred_element_type=F32) + b_ref[0]


def ada_params_all(cond, w_ada, b_ada):
    N, D = cond.shape
    depth, _, W = w_ada.shape
    rows = -(-N // SUBLANES) * SUBLANES
    cond_p = jnp.concatenate([cond.astype(F32), jnp.zeros((rows - N, D), F32)], axis=0)
    cb = D
    out = pl.pallas_call(
        _ada_kernel,
        grid=(depth, W // cb),
        in_specs=[pl.BlockSpec((rows, D), lambda l, j: (0, 0)), pl.BlockSpec((1, D, cb), lambda l, j: (l, 0, j)),
                  pl.BlockSpec((1, 1, cb), lambda l, j: (l, 0, j))],
        out_specs=pl.BlockSpec((1, rows, cb), lambda l, j: (l, 0, j)),
        out_shape=jax.ShapeDtypeStruct((depth, rows, W), F32),
        compiler_params=pltpu.CompilerParams(dimension_semantics=("arbitrary", "arbitrary"),
                                             vmem_limit_bytes=VMEM_LIMIT_BYTES),
        name="ada_params",
    )(cond_p, w_ada.astype(F32), b_ada.astype(F32)[:, None, :])
    return out[:, :N]


def _final_norm_kernel(x_ref, g_ref, o_ref):
    x = x_ref[...]
    o_ref[...] = x * lax.rsqrt(jnp.mean(x * x, axis=-1, keepdims=True) + EPS) * g_ref[...]


def final_norm(x, g, tm=512):
    B, L, D = x.shape
    M = B * L
    assert M % tm == 0
    out = pl.pallas_call(
        _final_norm_kernel,
        grid=(M // tm,),
        in_specs=[pl.BlockSpec((tm, D), lambda i: (i, 0)), pl.BlockSpec((1, D), lambda i: (0, 0))],
        out_specs=pl.BlockSpec((tm, D), lambda i: (i, 0)),
        out_shape=jax.ShapeDtypeStruct((M, D), F32),
        compiler_params=pltpu.CompilerParams(dimension_semantics=("arbitrary",), vmem_limit_bytes=VMEM_LIMIT_BYTES),
        name="final_norm",
    )(x.reshape(M, D), g.astype(F32)[None, :])
    return out.reshape(B, L, D)


def ada_params(cond, w, b):
    m = jax.nn.silu(cond) @ w + b
    return jnp.split(m[:, None, :], 6, axis=-1)


def modulate(x, g, shift, scale):
    return rms_norm(x, g) * (1 + scale) + shift


def axial_rope(T):
    t = jnp.arange(T)
    n_freq = HEAD_DIM // 4
    inv = ROPE_BASE ** (-jnp.arange(n_freq, dtype=jnp.float32) / n_freq)
    ang = jnp.concatenate([(t // GRID_W).astype(jnp.float32)[:, None] * inv,
                           (t % GRID_W).astype(jnp.float32)[:, None] * inv], axis=-1)
    return jnp.cos(ang)[:, None, :], jnp.sin(ang)[:, None, :]


def apply_rope(x, cos, sin):
    xf = x.astype(jnp.float32)
    x1, x2 = jnp.split(xf, 2, axis=-1)
    return jnp.concatenate([x1 * cos - x2 * sin, x2 * cos + x1 * sin], axis=-1).astype(x.dtype)


def softmax_parts(parts, sink=None):
    sizes = [p.shape[-1] for p in parts]
    cols = list(parts)
    if sink is not None:
        cols.append(jnp.broadcast_to(sink, parts[0].shape[:-1] + (1,)))
    p = jax.nn.softmax(jnp.concatenate(cols, axis=-1), axis=-1)
    pieces = jnp.split(p, np.cumsum(sizes).tolist(), axis=-1)
    return pieces[:len(sizes)]


def context_attention(q, k, v, sink):
    B, S, HQ, hd = q.shape
    HK = k.shape[2]
    G = HQ // HK
    nb = S // Q_BLOCK
    scale = hd ** -0.5
    sink_b = None if sink is None else sink.astype(jnp.float32).reshape(1, HK, G, 1, 1)
    qb = q.reshape(B, nb, Q_BLOCK, HK, G, hd).swapaxes(0, 1)

    def one(qi):
        s = jnp.einsum('bqkgd,bskd->bkgqs', qi, k, preferred_element_type=jnp.float32) * scale
        (p,) = softmax_parts([s], sink_b)
        return jnp.einsum('bkgqs,bskd->bqkgd', p.astype(v.dtype), v)

    o = lax.map(one, qb)
    return o.swapaxes(0, 1).reshape(B, S, HQ, hd)


def window_attention(q, k, v, ck, cv, sink):
    B, T, HQ, hd = q.shape
    HK = k.shape[2]
    G = HQ // HK
    nb = T // A_BLOCK
    scale = hd ** -0.5
    qb = q.reshape(B, nb, A_BLOCK, HK, G, hd).swapaxes(0, 1)

    def band(x):
        xp = jnp.pad(x, ((0, 0), (A_BLOCK, A_BLOCK), (0, 0), (0, 0))).reshape(B, nb + 2, A_BLOCK, HK, hd)
        return jnp.concatenate([xp[:, :-2], xp[:, 1:-1], xp[:, 2:]], axis=2).swapaxes(0, 1)

    kb, vb = band(k), band(v)
    qpos = jnp.arange(nb)[:, None, None] * A_BLOCK + jnp.arange(A_BLOCK)[None, :, None]
    kpos = jnp.arange(nb)[:, None, None] * A_BLOCK - A_BLOCK + jnp.arange(3 * A_BLOCK)[None, None, :]
    mask = (jnp.abs(kpos - qpos) <= A_WINDOW) & (kpos >= 0) & (kpos < T)
    sink_b = sink.astype(jnp.float32).reshape(1, HK, G, 1, 1)

    def one(xs):
        qi, ki, vi, mi = xs
        s_loc = jnp.einsum('bqkgd,bskd->bkgqs', qi, ki, preferred_element_type=jnp.float32) * scale
        s_loc = jnp.where(mi[None, None, None], s_loc, NEG_INF)
        s_ctx = jnp.einsum('bqkgd,bpkd->bkgqp', qi, ck, preferred_element_type=jnp.float32) * scale
        p_loc, p_ctx = softmax_parts([s_loc, s_ctx], sink_b)
        return (jnp.einsum('bkgqs,bskd->bqkgd', p_loc.astype(vi.dtype), vi)
                + jnp.einsum('bkgqp,bpkd->bqkgd', p_ctx.astype(cv.dtype), cv))

    o = lax.map(one, (qb, kb, vb, mask))
    return o.swapaxes(0, 1).reshape(B, T, HQ, hd)


def short_conv(x, w):
    K = w.shape[0]
    L = x.shape[1]
    pad = K // 2
    xp = jnp.pad(x, ((0, 0), (pad, pad), (0, 0)))
    return sum(xp[:, i:i + L] * w[i] for i in range(K))


def hyena_filter_bank(L, w1, b1, w2, b2, w3, freq):
    f32 = jnp.float32
    t = jnp.linspace(0.0, 1.0, L, dtype=f32)[:, None]
    bands = (HY_EMB - 1) // 2
    omega = 2.0 * math.pi * jnp.arange(L, dtype=f32)[:, None] / L
    fb = jnp.linspace(1e-4, bands - 1, bands, dtype=f32)[None, :]
    feats = jnp.concatenate([t, jnp.cos(fb * omega), -jnp.sin(fb * omega)], axis=-1)
    fr = freq.astype(f32)
    h = jnp.sin(fr * (feats @ w1.astype(f32) + b1.astype(f32)))
    h = jnp.sin(fr * (h @ w2.astype(f32) + b2.astype(f32)))
    h = (h @ w3.astype(f32)).reshape(L, 2, HY_ORDER, HY_CH)
    max_decay = math.log(HY_DECAY_TARGET) / HY_FAST_DECAY
    min_decay = math.log(HY_DECAY_TARGET) / HY_SLOW_DECAY
    deltas = jnp.abs(jnp.linspace(min_decay, max_decay, HY_CH, dtype=f32))
    h = h * (jnp.exp(-t * deltas) + HY_SHIFT)[:, None, None, :]
    taps = jnp.concatenate([h[:, 0], jnp.zeros((1, HY_ORDER, HY_CH), f32), h[:0:-1, 1]], axis=0)
    taps = taps / jnp.sum(jnp.abs(taps), axis=0, keepdims=True)
    return jnp.fft.rfft(taps, axis=0)


def hyena_mixer(z, conv_w, conv_b, w1, b1, w2, b2, w3, freq, skip):
    L = z.shape[1]
    zf = (short_conv(z, conv_w) + conv_b).astype(jnp.float32)
    v, x1, x2 = jnp.split(zf, 3, axis=-1)
    filt = hyena_filter_bank(L, w1, b1, w2, b2, w3, freq)
    skip = skip.astype(jnp.float32)

    def long_conv(u, o):
        y = jnp.fft.irfft(jnp.fft.rfft(u, n=2 * L, axis=1) * filt[None, :, o], n=2 * L, axis=1)[:, :L]
        return y + u * skip[o]

    y = x1 * long_conv(v, 0)
    y = x2 * long_conv(y, 1)
    return y.astype(z.dtype)


def l2norm(x):
    xf = x.astype(jnp.float32)
    return xf * lax.rsqrt(jnp.sum(xf * xf, axis=-1, keepdims=True) + EPS)


def chunk_gated_delta(q, k, v, g, beta, s0):
    B, L, H, dk = q.shape
    dv = v.shape[-1]
    n = L // C_CHUNK

    def chunks(x):
        x = x.reshape((B, n, C_CHUNK, H) + x.shape[3:])
        return jnp.moveaxis(jnp.moveaxis(x, 1, 0), 3, 2)

    qc, kc, vc, bc = chunks(q), chunks(k), chunks(v), chunks(beta)
    gc = jnp.cumsum(chunks(g), axis=-1)
    tri = jnp.tril(jnp.ones((C_CHUNK, C_CHUNK), bool))
    strict = jnp.tril(jnp.ones((C_CHUNK, C_CHUNK), bool), k=-1)
    gamma = jnp.exp(jnp.where(tri, gc[..., :, None] - gc[..., None, :], NEG_INF))
    kb = kc * bc[..., None]
    a_mat = jnp.where(strict, jnp.einsum('nbhid,nbhjd->nbhij', kb, kc) * gamma, 0.0) + jnp.eye(C_CHUNK, dtype=jnp.float32)
    rhs = jnp.concatenate([vc * bc[..., None], kb * jnp.exp(gc)[..., None]], axis=-1)
    sol = lax.linalg.triangular_solve(a_mat, rhs, left_side=True, lower=True)
    u, w = sol[..., :dv], sol[..., dv:]
    attn = jnp.where(tri, jnp.einsum('nbhid,nbhjd->nbhij', qc, kc) * gamma, 0.0)
    g_last = gc[..., -1]
    q_dec = qc * jnp.exp(gc)[..., None]
    k_dec = kc * jnp.exp(g_last[..., None] - gc)[..., None]

    def step(S, xs):
        u_i, w_i, a_i, qd, kd, gl = xs
        v_new = u_i - jnp.einsum('bhck,bhkv->bhcv', w_i, S)
        o = jnp.einsum('bhck,bhkv->bhcv', qd, S) + jnp.einsum('bhij,bhjv->bhiv', a_i, v_new)
        S = S * jnp.exp(gl)[..., None, None] + jnp.einsum('bhck,bhcv->bhkv', kd, v_new)
        return S, o

    S, o = lax.scan(step, s0.astype(jnp.float32), (u, w, attn, q_dec, k_dec, g_last))
    o = jnp.moveaxis(jnp.moveaxis(o, 2, 3), 0, 1).reshape(B, L, H, dv)
    return o, S


def deltanet_mixer(zq, zk, zv, zg, za, zb, conv_w, a_log, dt_bias, norm_w, s0):
    B, L, _ = zq.shape
    qkv = jax.nn.silu(short_conv(jnp.concatenate([zq, zk, zv], axis=-1), conv_w))
    q, k, v = [t.reshape(B, L, C_HEADS, HEAD_DIM) for t in jnp.split(qkv, 3, axis=-1)]
    q = l2norm(q) * (HEAD_DIM ** -0.5)
    k = l2norm(k)
    v = v.astype(jnp.float32)
    beta = jax.nn.sigmoid(zb.astype(jnp.float32))
    g = -jnp.exp(a_log.astype(jnp.float32)) * jax.nn.softplus(za.astype(jnp.float32) + dt_bias.astype(jnp.float32))
    o_f, s_f = chunk_gated_delta(q, k, v, g[:, :, 0], beta[:, :, 0], s0[:, 0])
    o_b, s_b = chunk_gated_delta(q[:, ::-1], k[:, ::-1], v[:, ::-1], g[:, ::-1, 1], beta[:, ::-1, 1], s0[:, 1])
    o = o_f + o_b[:, ::-1]
    gate = jax.nn.silu(zg.reshape(B, L, C_HEADS, HEAD_DIM).astype(jnp.float32))
    o = rms_norm(o, norm_w) * gate
    return o.reshape(B, L, C_W).astype(zq.dtype), jnp.stack([s_f, s_b], axis=1)


def neighbourhood_attention(q, k, v, ck, cv, rpb):
    B, T, H, hd = q.shape
    rows = T // GRID_W
    kh = min(NA_KH_MAX, rows)
    scale = hd ** -0.5
    r = jnp.arange(rows)
    rs = jnp.clip(r - kh // 2, 0, rows - kh)
    key_rows = rs[:, None] + jnp.arange(kh)[None, :]
    idx = (key_rows[:, :, None] * GRID_W + jnp.arange(GRID_W)).reshape(rows, kh * GRID_W)
    col = jnp.arange(GRID_W)
    cs = jnp.clip(col - NA_KW // 2, 0, GRID_W - NA_KW)
    kcol = jnp.tile(col, kh)
    col_ok = (kcol[None, :] >= cs[:, None]) & (kcol[None, :] < cs[:, None] + NA_KW)
    roff = jnp.repeat(key_rows - r[:, None], GRID_W, axis=1) + NA_KH_MAX - 1
    coff = jnp.clip(kcol[None, :] - col[:, None] + NA_KW - 1, 0, 2 * NA_KW - 2)
    qr = q.reshape(B, rows, GRID_W, H, hd).swapaxes(0, 1)
    rpb_f = rpb.astype(jnp.float32)

    def one(xs):
        qi, ii, ro = xs
        ki = k[:, ii]
        vi = v[:, ii]
        bias = rpb_f[:, ro[None, :], coff]
        s_loc = jnp.einsum('bqhd,bkhd->bhqk', qi, ki, preferred_element_type=jnp.float32) * scale + bias[None]
        s_loc = jnp.where(col_ok[None, None], s_loc, NEG_INF)
        s_ctx = jnp.einsum('bqhd,bphd->bhqp', qi, ck, preferred_element_type=jnp.float32) * scale
        p_loc, p_ctx = softmax_parts([s_loc, s_ctx])
        return (jnp.einsum('bhqk,bkhd->bqhd', p_loc.astype(vi.dtype), vi)
                + jnp.einsum('bhqp,bphd->bqhd', p_ctx.astype(cv.dtype), cv))

    o = lax.map(one, (qr, idx, roff))
    return o.swapaxes(0, 1).reshape(B, T, H, hd)


def expert_choice_ffn(h, w_router, w_gate, w_up, w_down):
    B, T, D = h.shape
    cap = EC_CAPACITY * T // N_EXPERTS
    aff = jax.nn.softmax(jnp.einsum('btd,de->bte', h, w_router, preferred_element_type=jnp.float32), axis=-1)
    gate, idx = lax.top_k(aff.swapaxes(1, 2), cap)
    xg = jax.vmap(lambda hb, ib: hb[ib])(h, idx)
    a = jnp.einsum('becd,edf->becf', xg, w_gate)
    u = jnp.einsum('becd,edf->becf', xg, w_up)
    y = jnp.einsum('becf,efd->becd', jax.nn.silu(a) * u, w_down) * gate[..., None].astype(h.dtype)
    return jax.vmap(lambda yb, ib: jnp.zeros((T, D), yb.dtype).at[ib.reshape(-1)].add(yb.reshape(-1, D)))(y, idx)


def split_even(z):
    B, L = z.shape[:2]
    q = z[..., :A_Q_W].reshape(B, L, A_HEADS, HEAD_DIM)
    k = z[..., A_Q_W:A_Q_W + A_KV_W].reshape(B, L, A_KV_HEADS, HEAD_DIM)
    v = z[..., A_Q_W + A_KV_W:A_Q_W + 2 * A_KV_W].reshape(B, L, A_KV_HEADS, HEAD_DIM)
    return q, k, v, z[..., A_Q_W + 2 * A_KV_W:]


def split_odd(z):
    B, L = z.shape[:2]
    zq, zk, zv, zg = [z[..., i * C_W:(i + 1) * C_W] for i in range(4)]
    off = 4 * C_W
    za = z[..., off:off + 2 * C_HEADS].reshape(B, L, 2, C_HEADS)
    zb = z[..., off + 2 * C_HEADS:off + 4 * C_HEADS].reshape(B, L, 2, C_HEADS)
    off = off + 4 * C_HEADS
    nq, nk, nv = [z[..., off + i * D_W:off + (i + 1) * D_W].reshape(B, L, D_HEADS, HEAD_DIM) for i in range(3)]
    return zq, zk, zv, zg, za, zb, nq, nk, nv


def kernel(x_prompt, x_sample, cache_attn_k, cache_attn_v, state_delta, cache_na_k, cache_na_v,
           c, c_ctx, w_ada, b_ada, norm_mix, norm_ffn, norm_final,
           even_w_in, even_w_out, attn_sink, hy_conv_w, hy_conv_b, hy_w1, hy_b1, hy_w2, hy_b2,
           hy_w3, hy_freq, hy_skip, odd_w_in, odd_w_out, gdn_conv_w, gdn_a_log, gdn_dt_bias,
           gdn_norm, na_rpb, moe_router, moe_w_gate, moe_w_up, moe_w_down):
    xp, xs = x_prompt, x_sample
    bp = xp.shape[0]
    dft_p, dft_s = dft_tables(xp.shape[1]), dft_tables(xs.shape[1])
    ada = ada_params_all(jnp.concatenate([c_ctx[None, :], c], axis=0), w_ada, b_ada)
    new_ak, new_av, new_st, new_nk, new_nv = [], [], [], [], []
    for l in range(DEPTH):
        j = l // 2
        mp = jnp.split(ada[l, :1, None, :], 6, axis=-1)
        ms = jnp.split(ada[l, 1:, None, :], 6, axis=-1)
        mod_p = (norm_mix[l], mp[0], mp[1])
        mod_s = (norm_mix[l], ms[0], ms[1])
        if l % 2 == 0:
            def hyena(zh, tables):
                taps = hyena_taps(zh.shape[1], hy_w1[j], hy_b1[j], hy_w2[j], hy_b2[j], hy_w3[j], hy_freq[j])
                return hyena_mixer_pallas(zh, hy_conv_w[j], hy_conv_b[j], hy_skip[j], tables, hyena_spectrum(tables, *taps))

            w_in = even_w_in[j]
            w_groups = [w_in[:, :A_Q_W], w_in[:, A_Q_W:A_Q_W + A_KV_W], w_in[:, A_Q_W + A_KV_W:A_Q_W + 2 * A_KV_W],
                        w_in[:, A_Q_W + 2 * A_KV_W:]]
            q, k, v, zh = proj_multi(xp, *mod_p, w_groups, [F32] * 4)
            oa = context_attention_pallas(q, k, v, attn_sink[j], A_HEADS, A_KV_HEADS)
            xp_new = proj_concat(oa, hyena(zh, dft_p), even_w_out[j], xp, mp[2])
            new_ak.append(k.reshape(bp, SEQ, A_KV_HEADS, HEAD_DIM))
            new_av.append(v.reshape(bp, SEQ, A_KV_HEADS, HEAD_DIM))
            q, k, v, zh = proj_multi(xs, *mod_s, w_groups, [F32] * 4)
            ck = cache_attn_k[:, j].reshape(DEC_BATCH, PAST_LEN, A_KV_W).astype(BF16)
            cv = cache_attn_v[:, j].reshape(DEC_BATCH, PAST_LEN, A_KV_W).astype(BF16)
            oa = window_attention_pallas(q, k, v, ck, cv, attn_sink[j])
            xs_new = proj_concat(oa, hyena(zh, dft_s), even_w_out[j], xs, ms[2])
        else:
            w_in = odd_w_in[j]
            ab0 = 4 * C_W
            ab_cols = [w_in[:, ab0 + o * C_HEADS:ab0 + (o + 1) * C_HEADS] for o in (0, 2, 1, 3)]
            w_ab = jnp.concatenate(ab_cols + [jnp.zeros((D_MODEL, LANES - 4 * C_HEADS), w_in.dtype)], axis=1)
            n0 = ab0 + 4 * C_HEADS
            w_groups = [w_in[:, :3 * C_W], w_in[:, 3 * C_W:4 * C_W], w_ab,
                        w_in[:, n0:n0 + D_W], w_in[:, n0 + D_W:n0 + 2 * D_W], w_in[:, n0 + 2 * D_W:]]

            def deltanet(zqkv, zab, zg, s0):
                qd, kd, vd, gb = gdn_prep(zqkv, zab, gdn_conv_w[j], gdn_a_log[j], gdn_dt_bias[j])
                return gdn_scan(qd, kd, vd, gb, zg, s0, gdn_norm[j])

            zqkv, zg, zab, nq, nk, nv = proj_multi(xp, *mod_p, w_groups, [F32] * 6)
            oc, st = deltanet(zqkv, zab, zg, jnp.zeros((bp, 2, C_HEADS, HEAD_DIM, HEAD_DIM), F32))
            od = context_attention_pallas(nq, nk, nv, None, D_HEADS, D_HEADS)
            xp_new = proj_concat(oc, od, odd_w_out[j], xp, mp[2])
            new_st.append(st)
            new_nk.append(nk.reshape(bp, SEQ, D_HEADS, HEAD_DIM))
            new_nv.append(nv.reshape(bp, SEQ, D_HEADS, HEAD_DIM))
            zqkv, zg, zab, nq, nk, nv = proj_multi(xs, *mod_s, w_groups, [F32, F32, F32, BF16, BF16, BF16])
            oc, _ = deltanet(zqkv, zab, zg, state_delta[:, j])
            ck = cache_na_k[:, j].reshape(DEC_BATCH, PAST_LEN, D_W).astype(BF16)
            cv = cache_na_v[:, j].reshape(DEC_BATCH, PAST_LEN, D_W).astype(BF16)
            od = neighbourhood_attention_pallas(nq, nk, nv, ck, cv, na_rpb[j])
            xs_new = proj_concat(oc, od, odd_w_out[j], xs, ms[2])
        xp, xs = xp_new, xs_new
        moe = (moe_router[l], moe_w_gate, moe_w_up, moe_w_down, l)
        xp = moe_block(xp, norm_ffn[l], mp[3], mp[4], mp[5], *moe)
        xs = moe_block(xs, norm_ffn[l], ms[3], ms[4], ms[5], *moe)
    y_prompt = final_norm(xp, norm_final)
    y_sample = final_norm(xs, norm_final)
    return (y_prompt, y_sample, jnp.stack(new_ak, axis=1), jnp.stack(new_av, axis=1), jnp.stack(new_st, axis=1),
            jnp.stack(new_nk, axis=1), jnp.stack(new_nv, axis=1))
```

```python
import functools
import math
import jax, jax.numpy as jnp
from jax import lax
import numpy as np
from jax.experimental import pallas as pl
from jax.experimental.pallas import tpu as pltpu

D_MODEL = 1024
BATCH = 32
SEQ = 256
DEPTH = 4
DEC_BATCH = 4
DEC_SEQ = 4096
PAST_LEN = 512

GRID_W = 64
HEAD_DIM = 64
N_EVEN = (DEPTH + 1) // 2
N_ODD = DEPTH // 2
Q_BLOCK = 128
A_HEADS = D_MODEL // 128
A_KV_HEADS = A_HEADS // 4
A_WINDOW = 128
A_BLOCK = 128
ROPE_BASE = 10000.0
HY_CH = D_MODEL // 2
HY_ORDER = 2
HY_SHORT = 3
HY_EMB = 33
HY_FILT_W = 64
HY_FAST_DECAY = 0.3
HY_SLOW_DECAY = 1.5
HY_DECAY_TARGET = 1e-2
HY_SHIFT = 0.05
C_HEADS = D_MODEL // 128
C_SHORT = 3
C_CHUNK = 64
D_HEADS = D_MODEL // 128
NA_KH_MAX = 8
NA_KW = 16
N_EXPERTS = 16
EC_CAPACITY = 2
MOE_D_FF = D_MODEL
EPS = 1e-6
NEG_INF = -1e30

A_Q_W = A_HEADS * HEAD_DIM
A_KV_W = A_KV_HEADS * HEAD_DIM
EVEN_IN = A_Q_W + 2 * A_KV_W + 3 * HY_CH
EVEN_MIX = A_Q_W + HY_CH
C_W = C_HEADS * HEAD_DIM
D_W = D_HEADS * HEAD_DIM
ODD_IN = 4 * C_W + 4 * C_HEADS + 3 * D_W
ODD_MIX = C_W + D_W

VMEM_LIMIT_BYTES = 48 * 1024 * 1024


def _mm_kernel(x_ref, w_ref, o_ref):
    o_ref[...] = jnp.dot(x_ref[...].astype(jnp.bfloat16), w_ref[...], preferred_element_type=jnp.float32)


def pallas_matmul(x, w, tm=256):
    M, K = x.shape
    N = w.shape[1]
    assert M % tm == 0
    return pl.pallas_call(
        _mm_kernel,
        grid=(M // tm,),
        in_specs=[pl.BlockSpec((tm, K), lambda i: (i, 0)), pl.BlockSpec((K, N), lambda i: (0, 0))],
        out_specs=pl.BlockSpec((tm, N), lambda i: (i, 0)),
        out_shape=jax.ShapeDtypeStruct((M, N), jnp.float32),
        compiler_params=pltpu.CompilerParams(dimension_semantics=("arbitrary",), vmem_limit_bytes=VMEM_LIMIT_BYTES),
    )(x, w.astype(jnp.bfloat16))


def proj(x, w):
    B, L, K = x.shape
    return pallas_matmul(x.reshape(B * L, K), w).reshape(B, L, w.shape[1])


PROJ_TM = 512


def _request_of_tile(tm, L, per_request):
    return (lambda i: ((i * tm) // L, 0, 0)) if per_request else (lambda i: (0, 0, 0))


def _mm_multi_kernel(x_ref, g_ref, shift_ref, scale_ref, *refs):
    n = len(refs) // 2
    x = x_ref[...]
    y = x * lax.rsqrt(jnp.mean(x * x, axis=-1, keepdims=True) + EPS) * g_ref[...]
    h = (y * (1.0 + scale_ref[0]) + shift_ref[0]).astype(jnp.bfloat16)
    for w_ref, o_ref in zip(refs[:n], refs[n:]):
        o_ref[...] = jnp.dot(h, w_ref[...], preferred_element_type=jnp.float32).astype(o_ref.dtype)


def proj_multi(x, g, shift, scale, weights, out_dtypes, tm=PROJ_TM):
    B, L, K = x.shape
    M = B * L
    per_request = shift.shape[0] == B
    assert L % tm == 0 if per_request else M % tm == 0
    mod_spec = pl.BlockSpec((1, 1, K), _request_of_tile(tm, L, per_request))
    outs = pl.pallas_call(
        _mm_multi_kernel,
        grid=(M // tm,),
        in_specs=[pl.BlockSpec((tm, K), lambda i: (i, 0)), pl.BlockSpec((1, K), lambda i: (0, 0)), mod_spec, mod_spec]
        + [pl.BlockSpec(w.shape, lambda i: (0, 0)) for w in weights],
        out_specs=[pl.BlockSpec((tm, w.shape[1]), lambda i: (i, 0)) for w in weights],
        out_shape=[jax.ShapeDtypeStruct((M, w.shape[1]), dt) for w, dt in zip(weights, out_dtypes)],
        compiler_params=pltpu.CompilerParams(dimension_semantics=("arbitrary",), vmem_limit_bytes=VMEM_LIMIT_BYTES),
        name="in_projection",
    )(x.reshape(M, K), g.astype(jnp.float32)[None, :], shift, scale, *[w.astype(jnp.bfloat16) for w in weights])
    return [o.reshape(B, L, o.shape[1]) for o in outs]


def _mm2_kernel(a_ref, b_ref, wa_ref, wb_ref, x_ref, gate_ref, o_ref):
    mix = (jnp.dot(a_ref[...].astype(jnp.bfloat16), wa_ref[...], preferred_element_type=jnp.float32)
           + jnp.dot(b_ref[...].astype(jnp.bfloat16), wb_ref[...], preferred_element_type=jnp.float32))
    o_ref[...] = x_ref[...] + gate_ref[0] * mix


def proj_concat(a, b, w, x, gate, tm=PROJ_TM):
    B, L, Ka = a.shape
    Kb = b.shape[2]
    N = w.shape[1]
    M = B * L
    assert (L % tm == 0 if gate.shape[0] == B else M % tm == 0) and w.shape[0] == Ka + Kb
    wb16 = w.astype(jnp.bfloat16)
    out = pl.pallas_call(
        _mm2_kernel,
        grid=(M // tm,),
        in_specs=[pl.BlockSpec((tm, Ka), lambda i: (i, 0)), pl.BlockSpec((tm, Kb), lambda i: (i, 0)),
                  pl.BlockSpec((Ka, N), lambda i: (0, 0)), pl.BlockSpec((Kb, N), lambda i: (0, 0)),
                  pl.BlockSpec((tm, N), lambda i: (i, 0)),
                  pl.BlockSpec((1, 1, N), _request_of_tile(tm, L, gate.shape[0] == B))],
        out_specs=pl.BlockSpec((tm, N), lambda i: (i, 0)),
        out_shape=jax.ShapeDtypeStruct((M, N), jnp.float32),
        compiler_params=pltpu.CompilerParams(dimension_semantics=("arbitrary",), vmem_limit_bytes=VMEM_LIMIT_BYTES),
        name="out_projection",
    )(a.reshape(M, Ka), b.reshape(M, Kb), wb16[:Ka], wb16[Ka:], x.reshape(M, N), gate)
    return out.reshape(B, L, N)


LANES = 128
BF16 = jnp.bfloat16
F32 = jnp.float32


def _dot_nt(a, b):
    return lax.dot_general(a, b, (((1,), (1,)), ((), ())), preferred_element_type=F32)


def _dot(a, b):
    return jnp.dot(a, b, preferred_element_type=F32)


def _low_half(shape):
    return lax.broadcasted_iota(jnp.int32, shape, 1) < HEAD_DIM


def _softmax_pv(units):
    n = range(len(units))
    m = [functools.reduce(jnp.maximum, [s.max(axis=-1, keepdims=True) for s in units[u][0]]) for u in n]
    m = [m[u] if units[u][2] is None else jnp.maximum(m[u], units[u][2]) for u in n]
    p = [[jnp.exp(s - m[u]) for s in units[u][0]] for u in n]
    l = [functools.reduce(lambda a, b: a + b, [x.sum(axis=-1, keepdims=True) for x in p[u]]) for u in n]
    l = [l[u] if units[u][2] is None else l[u] + jnp.exp(units[u][2] - m[u]) for u in n]
    o = [functools.reduce(lambda a, b: a + b, [_dot(x.astype(BF16), v) for x, v in zip(p[u], units[u][1])]) for u in n]
    return [o[u] / l[u] for u in n]


def _place_head(q_slab, src_half, dst_half, low):
    x = q_slab if src_half == dst_half else pltpu.roll(q_slab, HEAD_DIM, axis=1)
    return jnp.where(low if dst_half == 0 else ~low, x, jnp.zeros_like(x))


NA_KEYS = NA_KH_MAX * GRID_W


def _na_kernel(q_ref, k_ref, v_ref, ck_ref, cv_ref, tab_ref, o_ref):
    r = pl.program_id(1)
    rows = k_ref.shape[1] // GRID_W
    rs = jnp.clip(r - NA_KH_MAX // 2, 0, rows - NA_KH_MAX)
    start = pl.multiple_of(rs * GRID_W, GRID_W)
    scale = HEAD_DIM ** -0.5
    low = _low_half((GRID_W, LANES))
    pairs = range(D_HEADS // 2)
    cols = [slice(p * LANES, (p + 1) * LANES) for p in pairs]
    qp = [q_ref[0, :, c] for c in cols]
    kp = [k_ref[0, pl.ds(start, NA_KEYS), c] for c in cols]
    vp = [v_ref[0, pl.ds(start, NA_KEYS), c] for c in cols]
    ckp = [ck_ref[0, :, c] for c in cols]
    cvp = [cv_ref[0, :, c] for c in cols]
    heads = [(p, half) for p in pairs for half in range(2)]
    qm = [jnp.where(low if half == 0 else ~low, qp[p], jnp.zeros_like(qp[p])) for p, half in heads]
    s_loc = [_dot_nt(qm[i], kp[p]) * scale + tab_ref[i, 0] for i, (p, _) in enumerate(heads)]
    s_ctx = [_dot_nt(qm[i], ckp[p]) * scale for i, (p, _) in enumerate(heads)]
    outs = _softmax_pv([([s_loc[i], s_ctx[i]], [vp[p], cvp[p]], None) for i, (p, _) in enumerate(heads)])
    for p in pairs:
        o_ref[0, :, cols[p]] = jnp.where(low, outs[2 * p], outs[2 * p + 1]).astype(o_ref.dtype)


def na_bias_table(rpb):
    col = jnp.arange(GRID_W)
    cs = jnp.clip(col - NA_KW // 2, 0, GRID_W - NA_KW)
    col_ok = (col[None, :] >= cs[:, None]) & (col[None, :] < cs[:, None] + NA_KW)
    coff = jnp.clip(col[None, :] - col[:, None] + NA_KW - 1, 0, 2 * NA_KW - 2)
    base = jnp.where(col_ok[None, None], rpb.astype(F32)[:, :, coff], NEG_INF)
    tab = jnp.stack([base[:, o:o + NA_KH_MAX] for o in range(NA_KH_MAX)], axis=1)
    return tab.transpose(0, 1, 3, 2, 4).reshape(D_HEADS, NA_KH_MAX, GRID_W, NA_KEYS)


def neighbourhood_attention_pallas(q, k, v, ck, cv, rpb):
    B, T, W = q.shape
    P = ck.shape[1]
    rows = T // GRID_W
    assert rows >= NA_KH_MAX and W == D_W
    tab = na_bias_table(rpb)

    def tab_index(b, r):
        rs = jnp.clip(r - NA_KH_MAX // 2, 0, rows - NA_KH_MAX)
        return (0, rs - r + NA_KH_MAX - 1, 0, 0)

    return pl.pallas_call(
        _na_kernel,
        grid=(B, rows),
        in_specs=[
            pl.BlockSpec((1, GRID_W, W), lambda b, r: (b, r, 0)),
            pl.BlockSpec((1, T, W), lambda b, r: (b, 0, 0)),
            pl.BlockSpec((1, T, W), lambda b, r: (b, 0, 0)),
            pl.BlockSpec((1, P, W), lambda b, r: (b, 0, 0)),
            pl.BlockSpec((1, P, W), lambda b, r: (b, 0, 0)),
            pl.BlockSpec((D_HEADS, 1, GRID_W, NA_KEYS), tab_index),
        ],
        out_specs=pl.BlockSpec((1, GRID_W, W), lambda b, r: (b, r, 0)),
        out_shape=jax.ShapeDtypeStruct((B, T, W), BF16),
        compiler_params=pltpu.CompilerParams(dimension_semantics=("arbitrary", "arbitrary"),
                                             vmem_limit_bytes=VMEM_LIMIT_BYTES),
        name="na_attention",
    )(q, k, v, ck, cv, tab)


def rope_tables(T):
    cos, sin = axial_rope(T)
    cos, sin = cos[:, 0, :], sin[:, 0, :]
    cos_t = jnp.concatenate([cos, cos, cos, cos], axis=-1)
    sin_t = jnp.concatenate([-sin, sin, -sin, sin], axis=-1)
    return cos_t, sin_t


def _rope(x, cos_t, sin_t):
    half = HEAD_DIM // 2
    lane = lax.broadcasted_iota(jnp.int32, x.shape, 1)
    first = (lane % HEAD_DIM) < half
    swapped = jnp.where(first, pltpu.roll(x, LANES - half, axis=1), pltpu.roll(x, half, axis=1))
    return x * cos_t + swapped * sin_t


def _win_kernel(sink_ref, q_ref, k_ref, v_ref, ck_ref, cv_ref, cos_ref, sin_ref, o_ref):
    i = pl.program_id(1)
    T = k_ref.shape[1]
    span = 3 * A_BLOCK
    start = pl.multiple_of(jnp.clip((i - 1) * A_BLOCK, 0, T - span), A_BLOCK)
    delta = i * A_BLOCK - start
    q0 = pl.multiple_of(i * A_BLOCK, A_BLOCK)
    scale = HEAD_DIM ** -0.5
    kw = _rope(k_ref[0, pl.ds(start, span), :], cos_ref[pl.ds(start, span), :], sin_ref[pl.ds(start, span), :]).astype(BF16)
    vw = v_ref[0, pl.ds(start, span), :].astype(BF16)
    ck = ck_ref[0]
    cv = cv_ref[0]
    cos_q = cos_ref[pl.ds(q0, A_BLOCK), :]
    sin_q = sin_ref[pl.ds(q0, A_BLOCK), :]
    qi = lax.broadcasted_iota(jnp.int32, (A_BLOCK, span), 0)
    kj = lax.broadcasted_iota(jnp.int32, (A_BLOCK, span), 1)
    band = jnp.abs(kj - delta - qi) <= A_WINDOW
    low = _low_half((A_BLOCK, LANES))
    group = A_HEADS // A_KV_HEADS
    pairs = range(A_HEADS // 2)
    cols = [slice(p * LANES, (p + 1) * LANES) for p in pairs]
    q_slab = [_rope(q_ref[0, :, c], cos_q, sin_q) for c in cols]
    heads = range(A_HEADS)
    kv_of = [h // group for h in heads]
    qm = [_place_head(q_slab[h // 2], h % 2, kv_of[h], low).astype(BF16) for h in heads]
    s_loc = [jnp.where(band, _dot_nt(qm[h], kw) * scale, NEG_INF) for h in heads]
    s_ctx = [_dot_nt(qm[h], ck) * scale for h in heads]
    outs = _softmax_pv([([s_loc[h], s_ctx[h]], [vw, cv], sink_ref[h]) for h in heads])
    outs = [outs[h] if kv_of[h] == h % 2 else pltpu.roll(outs[h], HEAD_DIM, axis=1) for h in heads]
    for p in pairs:
        o_ref[0, :, cols[p]] = jnp.where(low, outs[2 * p], outs[2 * p + 1]).astype(o_ref.dtype)


def window_attention_pallas(q, k, v, ck, cv, sink):
    B, T, QW = q.shape
    KW = k.shape[2]
    P = ck.shape[1]
    assert KW == LANES and QW == A_Q_W and T % A_BLOCK == 0 and T >= 3 * A_BLOCK
    cos_t, sin_t = rope_tables(T)
    return pl.pallas_call(
        _win_kernel,
        grid=(B, T // A_BLOCK),
        in_specs=[
            pl.BlockSpec(memory_space=pltpu.SMEM),
            pl.BlockSpec((1, A_BLOCK, QW), lambda b, i: (b, i, 0)),
            pl.BlockSpec((1, T, KW), lambda b, i: (b, 0, 0)),
            pl.BlockSpec((1, T, KW), lambda b, i: (b, 0, 0)),
            pl.BlockSpec((1, P, KW), lambda b, i: (b, 0, 0)),
            pl.BlockSpec((1, P, KW), lambda b, i: (b, 0, 0)),
            pl.BlockSpec((T, LANES), lambda b, i: (0, 0)),
            pl.BlockSpec((T, LANES), lambda b, i: (0, 0)),
        ],
        out_specs=pl.BlockSpec((1, A_BLOCK, QW), lambda b, i: (b, i, 0)),
        out_shape=jax.ShapeDtypeStruct((B, T, QW), BF16),
        compiler_params=pltpu.CompilerParams(dimension_semantics=("arbitrary", "arbitrary"),
                                             vmem_limit_bytes=VMEM_LIMIT_BYTES),
        name="window_attention",
    )(sink.astype(F32), q, k, v, ck, cv, cos_t, sin_t)


def _ctx_kernel(sink_ref, q_ref, k_ref, v_ref, o_ref, *, n_q_heads, n_kv_heads, use_sink):
    S = q_ref.shape[1]
    scale = HEAD_DIM ** -0.5
    low = _low_half((S, LANES))
    group = n_q_heads // n_kv_heads
    pairs = range(n_q_heads // 2)
    cols = [slice(p * LANES, (p + 1) * LANES) for p in pairs]
    q_slab = [q_ref[0, :, c] for c in cols]
    heads = range(n_q_heads)
    kv_of = [h // group for h in heads]
    kcols = [slice((kv // 2) * LANES, (kv // 2 + 1) * LANES) for kv in kv_of]
    qm = [_place_head(q_slab[h // 2], h % 2, kv_of[h] % 2, low).astype(BF16) for h in heads]
    s = [_dot_nt(qm[h], k_ref[0, :, kcols[h]].astype(BF16)) * scale for h in heads]
    outs = _softmax_pv([([s[h]], [v_ref[0, :, kcols[h]].astype(BF16)], sink_ref[h] if use_sink else None) for h in heads])
    outs = [outs[h] if kv_of[h] % 2 == h % 2 else pltpu.roll(outs[h], HEAD_DIM, axis=1) for h in heads]
    for p in pairs:
        o_ref[0, :, cols[p]] = jnp.where(low, outs[2 * p], outs[2 * p + 1]).astype(o_ref.dtype)


def context_attention_pallas(q, k, v, sink, n_q_heads, n_kv_heads):
    B, S, QW = q.shape
    KW = k.shape[2]
    use_sink = sink is not None
    sink_arr = sink.astype(F32) if use_sink else jnp.zeros((n_q_heads,), F32)
    return pl.pallas_call(
        functools.partial(_ctx_kernel, n_q_heads=n_q_heads, n_kv_heads=n_kv_heads, use_sink=use_sink),
        grid=(B,),
        in_specs=[
            pl.BlockSpec(memory_space=pltpu.SMEM),
            pl.BlockSpec((1, S, QW), lambda b: (b, 0, 0)),
            pl.BlockSpec((1, S, KW), lambda b: (b, 0, 0)),
            pl.BlockSpec((1, S, KW), lambda b: (b, 0, 0)),
        ],
        out_specs=pl.BlockSpec((1, S, QW), lambda b: (b, 0, 0)),
        out_shape=jax.ShapeDtypeStruct((B, S, QW), BF16),
        compiler_params=pltpu.CompilerParams(dimension_semantics=("arbitrary",), vmem_limit_bytes=VMEM_LIMIT_BYTES),
        name="context_attention",
    )(sink_arr, q, k, v)


HIGHEST = lax.Precision.HIGHEST
GDN_TM = 256
GDN_REQS = 2
SUBLANES = 8


def _head_pair_sum_matrix():
    a = lax.broadcasted_iota(jnp.int32, (LANES, LANES), 0) // HEAD_DIM
    b = lax.broadcasted_iota(jnp.int32, (LANES, LANES), 1) // HEAD_DIM
    return (a == b).astype(F32)


def _gdn_prep_kernel(x_ref, prev_ref, next_ref, ab_ref, cw_ref, a_ref, dtb_ref, q_ref, k_ref, v_ref, gb_ref):
    i = pl.program_id(1)
    n = pl.num_programs(1)
    x = x_ref[0]
    tm = x.shape[0]
    row = lax.broadcasted_iota(jnp.int32, x.shape, 0)
    prev_row = jnp.where(i > 0, prev_ref[0, SUBLANES - 1:SUBLANES, :], 0.0)
    next_row = jnp.where(i < n - 1, next_ref[0, 0:1, :], 0.0)
    x_prev = jnp.where(row == 0, prev_row, pltpu.roll(x, 1, axis=0))
    x_next = jnp.where(row == tm - 1, next_row, pltpu.roll(x, tm - 1, axis=0))
    y = x_prev * cw_ref[0:1, :] + x * cw_ref[1:2, :] + x_next * cw_ref[2:3, :]
    y = y * jax.nn.sigmoid(y)
    pmat = _head_pair_sum_matrix()
    for p in range(C_W // LANES):
        qs = y[:, p * LANES:(p + 1) * LANES]
        ks = y[:, C_W + p * LANES:C_W + (p + 1) * LANES]
        q_ref[0, :, p * LANES:(p + 1) * LANES] = qs * lax.rsqrt(jnp.dot(qs * qs, pmat, precision=HIGHEST, preferred_element_type=F32) + EPS) * (HEAD_DIM ** -0.5)
        k_ref[0, :, p * LANES:(p + 1) * LANES] = ks * lax.rsqrt(jnp.dot(ks * ks, pmat, precision=HIGHEST, preferred_element_type=F32) + EPS)
    v_ref[0] = y[:, 2 * C_W:]
    ab = ab_ref[0, :, 0:4 * C_HEADS]
    lane = lax.broadcasted_iota(jnp.int32, ab.shape, 1)
    is_beta = (lane // C_HEADS) % 2 == 1
    t = ab + dtb_ref[...]
    softplus = jnp.maximum(t, 0.0) + jnp.log1p(jnp.exp(-jnp.abs(t)))
    gb = jnp.where(is_beta, jax.nn.sigmoid(ab), -jnp.exp(a_ref[...]) * softplus)
    gb_ref[0, 0] = gb[:, 0:2 * C_HEADS]
    gb_ref[0, 1] = gb[:, 2 * C_HEADS:4 * C_HEADS]


def gdn_prep(zqkv, zab, conv_w, a_log, dt_bias):
    B, L, W3 = zqkv.shape
    tm = GDN_TM
    assert L % tm == 0
    nb = tm // SUBLANES
    zero = jnp.zeros((C_HEADS,), F32)
    a_lane = jnp.concatenate([a_log[0], zero, a_log[1], zero]).astype(F32)[None, :]
    dtb_lane = jnp.concatenate([dt_bias[0], zero, dt_bias[1], zero]).astype(F32)[None, :]
    outs = pl.pallas_call(
        _gdn_prep_kernel,
        grid=(B, L // tm),
        in_specs=[
            pl.BlockSpec((1, tm, W3), lambda b, i: (b, i, 0)),
            pl.BlockSpec((1, SUBLANES, W3), lambda b, i: (b, jnp.maximum(i * nb - 1, 0), 0)),
            pl.BlockSpec((1, SUBLANES, W3), lambda b, i: (b, jnp.minimum((i + 1) * nb, L // SUBLANES - 1), 0)),
            pl.BlockSpec((1, tm, LANES), lambda b, i: (b, i, 0)),
            pl.BlockSpec((C_SHORT, W3), lambda b, i: (0, 0)),
            pl.BlockSpec((1, 4 * C_HEADS), lambda b, i: (0, 0)),
            pl.BlockSpec((1, 4 * C_HEADS), lambda b, i: (0, 0)),
        ],
        out_specs=[
            pl.BlockSpec((1, tm, C_W), lambda b, i: (b, i, 0)),
            pl.BlockSpec((1, tm, C_W), lambda b, i: (b, i, 0)),
            pl.BlockSpec((1, tm, C_W), lambda b, i: (b, i, 0)),
            pl.BlockSpec((1, 2, tm, 2 * C_HEADS), lambda b, i: (b, 0, i, 0)),
        ],
        out_shape=[jax.ShapeDtypeStruct((B, L, C_W), F32)] * 3 + [jax.ShapeDtypeStruct((B, 2, L, 2 * C_HEADS), F32)],
        compiler_params=pltpu.CompilerParams(dimension_semantics=("arbitrary", "arbitrary"),
                                             vmem_limit_bytes=VMEM_LIMIT_BYTES),
        name="gdn_prep",
    )(zqkv, zqkv, zqkv, zab, conv_w.astype(F32), a_lane, dtb_lane)
    return outs


def _gdn_kernel(q_ref, k_ref, v_ref, gb_ref, zg_ref, s0_ref, nw_ref, o_ref, st_ref, s_scr, of_scr):
    d = pl.program_id(1)
    c = pl.program_id(2)
    n = pl.num_programs(2)
    C = C_CHUNK
    fwd = d == 0
    chunk = jnp.where(fwd, c, n - 1 - c)
    r0 = pl.multiple_of(chunk * C, C)

    @pl.when(c == 0)
    def _():
        s_scr[...] = s0_ref[:, 0]

    row = lax.broadcasted_iota(jnp.int32, (C, C), 0)
    col = lax.broadcasted_iota(jnp.int32, (C, C), 1)
    ahead = jnp.where(fwd, row - col, col - row)
    incl = ahead >= 0
    strict = ahead > 0
    incl_f = incl.astype(F32)
    nreq = q_ref.shape[0]
    gc, gc_t, g_last, beta = [], [], [], []
    for r in range(nreq):
        g = gb_ref[r, 0, :, 0:C_HEADS]
        beta.append(gb_ref[r, 0, :, C_HEADS:2 * C_HEADS])
        gc.append(jnp.dot(incl_f, g, precision=HIGHEST, preferred_element_type=F32))
        gc_t.append(gc[r].T)
        g_last.append(jnp.where(fwd, gc[r][C - 1:C, :], gc[r][0:1, :]))
    units = [(r, h) for r in range(nreq) for h in range(C_HEADS)]
    H = range(len(units))
    heads = [slice(h * HEAD_DIM, (h + 1) * HEAD_DIM) for _, h in units]
    s_old = [s_scr[r, h] for r, h in units]
    qs = [q_ref[r, :, heads[i]] for i, (r, _) in enumerate(units)]
    ks = [k_ref[r, :, heads[i]] for i, (r, _) in enumerate(units)]
    vs = [v_ref[r, :, heads[i]] for i, (r, _) in enumerate(units)]
    gcol = [gc[r][:, h:h + 1] for r, h in units]
    bcol = [beta[r][:, h:h + 1] for r, h in units]
    gl = [g_last[r][:, h:h + 1] for r, h in units]
    gamma = [jnp.exp(jnp.where(incl, gcol[i] - gc_t[r][h:h + 1, :], NEG_INF)) for i, (r, h) in enumerate(units)]
    egc = [jnp.exp(gcol[h]) for h in H]
    kb = [ks[h].astype(BF16) for h in H]
    nmat = [jnp.where(strict, _dot_nt(kb[h], kb[h]) * gamma[h], 0.0) * bcol[h] for h in H]
    attn = [(_dot_nt(qs[h].astype(BF16), kb[h]) * gamma[h]).astype(BF16) for h in H]
    xr = row ^ col
    eye = (row == col).astype(F32)
    tinv = [eye - jnp.where((xr >> 1) == 0, nmat[h], 0.0) for h in H]
    for lvl in range(1, 6):
        off_diag = (xr >> lvl) == 1
        wmat = [_dot(jnp.where(off_diag, nmat[h], 0.0).astype(BF16), tinv[h].astype(BF16)).astype(BF16) for h in H]
        tinv = [tinv[h] - _dot(tinv[h].astype(BF16), wmat[h]) for h in H]
    x = [jnp.concatenate([vs[h] * bcol[h], ks[h] * (bcol[h] * egc[h])], axis=1) for h in H]
    x = [x[h] + _dot((tinv[h] - eye).astype(BF16), x[h].astype(BF16)) for h in H]
    sb = [s_old[h].astype(BF16) for h in H]
    v_new = [x[h][:, :HEAD_DIM] - _dot(x[h][:, HEAD_DIM:].astype(BF16), sb[h]) for h in H]
    vb = [v_new[h].astype(BF16) for h in H]
    outs = [_dot((qs[h] * egc[h]).astype(BF16), sb[h]) + _dot(attn[h], vb[h]) for h in H]
    kd = [(ks[h] * jnp.exp(gl[h] - gcol[h])).astype(BF16) for h in H]
    s_new = [s_old[h] * jnp.exp(gl[h]) + lax.dot_general(kd[h], vb[h], (((0,), (0,)), ((), ())), preferred_element_type=F32)
             for h in H]
    for i, (r, h) in enumerate(units):
        s_scr[r, h] = s_new[i]
    o = [jnp.concatenate(outs[r * C_HEADS:(r + 1) * C_HEADS], axis=1) for r in range(nreq)]

    @pl.when(fwd)
    def _():
        for r in range(nreq):
            of_scr[r, pl.ds(r0, C), :] = o[r]

    @pl.when(jnp.logical_not(fwd))
    def _():
        pmat = _head_pair_sum_matrix()
        for r in range(nreq):
            tot = of_scr[r, pl.ds(r0, C), :] + o[r]
            zg = zg_ref[r]
            gate = zg * jax.nn.sigmoid(zg)
            for p in range(C_W // LANES):
                cols = slice(p * LANES, (p + 1) * LANES)
                t = tot[:, cols]
                ms = jnp.dot(t * t, pmat, precision=HIGHEST, preferred_element_type=F32) * (1.0 / HEAD_DIM)
                o_ref[r, :, cols] = (t * lax.rsqrt(ms + EPS) * nw_ref[:, cols] * gate[:, cols]).astype(o_ref.dtype)

    @pl.when(c == n - 1)
    def _():
        st_ref[:, 0] = s_scr[...]


def gdn_scan(q, k, v, gb, zg, s0, norm_w):
    B, L, W = q.shape
    C = C_CHUNK
    n = L // C
    R = min(B, GDN_REQS)
    assert L % C == 0 and W == C_W and B % R == 0
    chunk_of = lambda d, c: jnp.where(d == 0, c, n - 1 - c)
    seq_spec = pl.BlockSpec((R, C, W), lambda b, d, c: (b, chunk_of(d, c), 0))
    state_spec = pl.BlockSpec((R, 1, C_HEADS, HEAD_DIM, HEAD_DIM), lambda b, d, c: (b, d, 0, 0, 0))
    nw = jnp.tile(norm_w.astype(F32), C_HEADS)[None, :]
    return pl.pallas_call(
        _gdn_kernel,
        grid=(B // R, 2, n),
        in_specs=[
            seq_spec, seq_spec, seq_spec,
            pl.BlockSpec((R, 1, C, 2 * C_HEADS), lambda b, d, c: (b, d, chunk_of(d, c), 0)),
            seq_spec,
            state_spec,
            pl.BlockSpec((1, W), lambda b, d, c: (0, 0)),
        ],
        out_specs=[
            pl.BlockSpec((R, C, W), lambda b, d, c: (b, jnp.where(d == 0, n - 1, n - 1 - c), 0)),
            state_spec,
        ],
        out_shape=[jax.ShapeDtypeStruct((B, L, W), BF16), jax.ShapeDtypeStruct(s0.shape, F32)],
        scratch_shapes=[pltpu.VMEM((R, C_HEADS, HEAD_DIM, HEAD_DIM), F32), pltpu.VMEM((R, L, W), F32)],
        compiler_params=pltpu.CompilerParams(dimension_semantics=("arbitrary", "arbitrary", "arbitrary"),
                                             vmem_limit_bytes=VMEM_LIMIT_BYTES),
        name="gdn_scan",
    )(q, k, v, gb, zg, s0.astype(F32), nw)


MOE_TT = 512
MOE_RT = 512
MOE_SELECT_REQS = 4


def _moe_router_kernel(x_ref, g_ref, shift_ref, scale_ref, wr_ref, h_ref, aff_ref):
    x = x_ref[0]
    y = x * lax.rsqrt(jnp.mean(x * x, axis=-1, keepdims=True) + EPS) * g_ref[...]
    h = y * (1.0 + scale_ref[0]) + shift_ref[0]
    h_ref[0] = h.astype(BF16)
    logits = jnp.dot(h, wr_ref[...], precision=HIGHEST, preferred_element_type=F32)
    lane = lax.broadcasted_iota(jnp.int32, logits.shape, 1)
    logits = jnp.where(lane < N_EXPERTS, logits, NEG_INF)
    e = jnp.exp(logits - logits.max(axis=-1, keepdims=True))
    aff_ref[0] = e / e.sum(axis=-1, keepdims=True)


def _moe_select_kernel(aff_ref, slot_ref, start_ref, *, cap):
    nreq, T = aff_ref.shape[0], aff_ref.shape[1]
    reqs = range(nreq)
    bits = [pltpu.bitcast(aff_ref[q], jnp.int32) for q in reqs]

    def bisect(i, v):
        cand = [v[q] | (1 << (30 - i)) for q in reqs]
        cnt = [jnp.sum((bits[q] >= cand[q]).astype(jnp.int32), axis=0, keepdims=True) for q in reqs]
        return tuple(jnp.where(cnt[q] >= cap, cand[q], v[q]) for q in reqs)

    thr = lax.fori_loop(0, 31, bisect, tuple(jnp.zeros((1, LANES), jnp.int32) for _ in reqs))
    blk = min(T, MOE_TT)
    r = lax.broadcasted_iota(jnp.int32, (blk, blk), 0)
    c = lax.broadcasted_iota(jnp.int32, (blk, blk), 1)
    before = (c < r).astype(BF16)
    n_tiles = T // blk
    for q in reqs:
        gt = (bits[q] > thr[q]).astype(F32)
        eq = (bits[q] == thr[q]).astype(F32)
        need = cap - jnp.sum(gt, axis=0, keepdims=True)
        carry_gt = jnp.zeros((1, LANES), F32)
        carry_eq = jnp.zeros((1, LANES), F32)
        start_ref[q] = jnp.zeros(start_ref.shape[1:], F32)
        for b in range(n_tiles):
            rows = slice(b * blk, (b + 1) * blk)
            gt_b, eq_b = gt[rows], eq[rows]
            pos_gt = _dot(before, gt_b.astype(BF16)) + carry_gt
            pos_eq = _dot(before, eq_b.astype(BF16)) + carry_eq
            chosen = gt_b + eq_b * (pos_eq < need).astype(F32)
            slot_ref[q, rows, :] = jnp.where(chosen > 0.5, pos_gt + jnp.minimum(pos_eq, need), -1.0)
            carry_gt = carry_gt + jnp.sum(gt_b, axis=0, keepdims=True)
            carry_eq = carry_eq + jnp.sum(eq_b, axis=0, keepdims=True)
            start_ref[q, b + 1:b + 2, :] = carry_gt + jnp.minimum(carry_eq, need)


def _expert_column(a, e):
    lane = lax.broadcasted_iota(jnp.int32, a.shape, 1)
    return jnp.sum(jnp.where(lane == e, a, 0.0), axis=1, keepdims=True)


MOE_START_ROWS = 16
MOE_WIN = 128


def _one_hot_slots(slot_col, first, width):
    s = lax.broadcasted_iota(jnp.int32, (slot_col.shape[0], width), 1).astype(F32)
    return (slot_col - first == s).astype(BF16)


def _slot_windows(start_ref, b, e, k):
    base = (b * N_EXPERTS + e) * MOE_START_ROWS + k
    lo, hi = start_ref[base], start_ref[base + 1]
    return lo // MOE_WIN, jnp.where(hi > lo, (hi - 1) // MOE_WIN + 1, lo // MOE_WIN)


def _moe_gather_kernel(start_ref, slot_ref, h_ref, xg_ref, acc_ref):
    b, e = pl.program_id(0), pl.program_id(1)
    acc_ref[...] = jnp.zeros_like(acc_ref)
    for k in range(slot_ref.shape[1] // MOE_TT):
        rows = slice(k * MOE_TT, (k + 1) * MOE_TT)
        col = _expert_column(slot_ref[0, rows, :], e)
        w0, w1 = _slot_windows(start_ref, b, e, k)

        def window(w, carry, col=col, rows=rows):
            s0 = pl.multiple_of(w * MOE_WIN, MOE_WIN)
            pt = _one_hot_slots(col, s0.astype(F32), MOE_WIN)
            acc_ref[pl.ds(s0, MOE_WIN), :] += lax.dot_general(pt, h_ref[0, rows, :], (((0,), (0,)), ((), ())),
                                                              preferred_element_type=F32)
            return carry

        lax.fori_loop(w0, w1, window, 0)
    xg_ref[0] = acc_ref[...].astype(xg_ref.dtype)


def _moe_gather_short_kernel(slot_ref, h_ref, xg_ref, *, cap):
    slot = slot_ref[0]
    h = h_ref[0]
    for e in range(N_EXPERTS):
        pt = _one_hot_slots(slot[:, e:e + 1], 0.0, cap)
        xg_ref[e] = lax.dot_general(pt, h, (((0,), (0,)), ((), ())), preferred_element_type=F32).astype(xg_ref.dtype)


def _moe_ffn_kernel(x_ref, wg_ref, wu_ref, wd_ref, y_ref, wg_s, wu_s, wd_s):
    @pl.when(pl.program_id(1) == 0)
    def _():
        wg_s[...] = wg_ref[0].astype(BF16)
        wu_s[...] = wu_ref[0].astype(BF16)
        wd_s[...] = wd_ref[0].astype(BF16)

    x = x_ref[0]
    a = _dot(x, wg_s[...])
    u = _dot(x, wu_s[...])
    hid = (a * jax.nn.sigmoid(a) * u).astype(BF16)
    y_ref[0] = _dot(hid, wd_s[...]).astype(y_ref.dtype)


def _moe_scatter_kernel(start_ref, slot_ref, aff_ref, y_ref, x_ref, gate_ref, o_ref, acc_ref):
    b, k = pl.program_id(0), pl.program_id(1)
    acc_ref[...] = jnp.zeros_like(acc_ref)
    slot, aff = slot_ref[0], aff_ref[0]
    for e in range(N_EXPERTS):
        col, weight = slot[:, e:e + 1], aff[:, e:e + 1]
        w0, w1 = _slot_windows(start_ref, b, e, k)

        def window(w, carry, e=e, col=col, weight=weight):
            s0 = pl.multiple_of(w * MOE_WIN, MOE_WIN)
            pt = _one_hot_slots(col, s0.astype(F32), MOE_WIN)
            acc_ref[...] += weight * _dot(pt, y_ref[e, pl.ds(s0, MOE_WIN), :])
            return carry

        lax.fori_loop(w0, w1, window, 0)
    o_ref[0] = x_ref[0] + gate_ref[0] * acc_ref[...]


def _moe_scatter_short_kernel(slot_ref, aff_ref, y_ref, x_ref, gate_ref, o_ref, *, cap):
    slot, aff = slot_ref[0], aff_ref[0]
    acc = jnp.zeros(x_ref.shape[1:], F32)
    for e in range(N_EXPERTS):
        pt = _one_hot_slots(slot[:, e:e + 1], 0.0, cap)
        acc = acc + aff[:, e:e + 1] * _dot(pt, y_ref[e])
    o_ref[0] = x_ref[0] + gate_ref[0] * acc


def moe_route(x, g, shift, scale, w_router):
    B, T, D = x.shape
    tt = min(T, MOE_TT)
    cap = EC_CAPACITY * T // N_EXPERTS
    per_request = shift.shape[0] == B
    mod_spec = pl.BlockSpec((1, 1, D), (lambda b, k: (b, 0, 0)) if per_request else (lambda b, k: (0, 0, 0)))
    wr = jnp.concatenate([w_router.astype(F32), jnp.zeros((D, LANES - N_EXPERTS), F32)], axis=1)
    h, aff = pl.pallas_call(
        _moe_router_kernel,
        grid=(B, T // tt),
        in_specs=[pl.BlockSpec((1, tt, D), lambda b, k: (b, k, 0)), pl.BlockSpec((1, D), lambda b, k: (0, 0)),
                  mod_spec, mod_spec, pl.BlockSpec((D, LANES), lambda b, k: (0, 0))],
        out_specs=[pl.BlockSpec((1, tt, D), lambda b, k: (b, k, 0)), pl.BlockSpec((1, tt, LANES), lambda b, k: (b, k, 0))],
        out_shape=[jax.ShapeDtypeStruct((B, T, D), BF16), jax.ShapeDtypeStruct((B, T, LANES), F32)],
        compiler_params=pltpu.CompilerParams(dimension_semantics=("arbitrary", "arbitrary"),
                                             vmem_limit_bytes=VMEM_LIMIT_BYTES),
        name="moe_router",
    )(x, g.astype(F32)[None, :], shift, scale, wr)
    assert T // tt < MOE_START_ROWS
    rq = MOE_SELECT_REQS if (T <= MOE_TT and B % MOE_SELECT_REQS == 0) else 1
    slot, start = pl.pallas_call(
        functools.partial(_moe_select_kernel, cap=cap),
        grid=(B // rq,),
        in_specs=[pl.BlockSpec((rq, T, LANES), lambda b: (b, 0, 0))],
        out_specs=[pl.BlockSpec((rq, T, LANES), lambda b: (b, 0, 0)),
                   pl.BlockSpec((rq, MOE_START_ROWS, LANES), lambda b: (b, 0, 0))],
        out_shape=[jax.ShapeDtypeStruct((B, T, LANES), F32), jax.ShapeDtypeStruct((B, MOE_START_ROWS, LANES), F32)],
        compiler_params=pltpu.CompilerParams(dimension_semantics=("arbitrary",), vmem_limit_bytes=VMEM_LIMIT_BYTES),
        name="moe_select",
    )(aff)
    start = start[:, :, :N_EXPERTS].astype(jnp.int32).transpose(0, 2, 1).reshape(-1)
    return h, aff, slot, start


def moe_gather(h, slot, start):
    B, T, D = h.shape
    cap = EC_CAPACITY * T // N_EXPERTS
    out_shape = jax.ShapeDtypeStruct((N_EXPERTS, B * cap, D), BF16)
    if T <= MOE_TT:
        return pl.pallas_call(
            functools.partial(_moe_gather_short_kernel, cap=cap),
            grid=(B,),
            in_specs=[pl.BlockSpec((1, T, LANES), lambda b: (b, 0, 0)), pl.BlockSpec((1, T, D), lambda b: (b, 0, 0))],
            out_specs=pl.BlockSpec((N_EXPERTS, cap, D), lambda b: (0, b, 0)),
            out_shape=out_shape,
            compiler_params=pltpu.CompilerParams(dimension_semantics=("arbitrary",), vmem_limit_bytes=VMEM_LIMIT_BYTES),
            name="moe_gather_short",
        )(slot, h)
    assert cap % MOE_WIN == 0 and T % MOE_TT == 0
    return pl.pallas_call(
        _moe_gather_kernel,
        grid_spec=pltpu.PrefetchScalarGridSpec(
            num_scalar_prefetch=1,
            grid=(B, N_EXPERTS),
            in_specs=[pl.BlockSpec((1, T, LANES), lambda b, e, st: (b, 0, 0)),
                      pl.BlockSpec((1, T, D), lambda b, e, st: (b, 0, 0))],
            out_specs=pl.BlockSpec((1, cap, D), lambda b, e, st: (e, b, 0)),
            scratch_shapes=[pltpu.VMEM((cap, D), F32)],
        ),
        out_shape=out_shape,
        compiler_params=pltpu.CompilerParams(dimension_semantics=("arbitrary",) * 2, vmem_limit_bytes=VMEM_LIMIT_BYTES),
        name="moe_gather",
    )(start, slot, h)


def moe_ffn(xg, w_gate, w_up, w_down, layer):
    E, R, D = xg.shape
    rt = min(R, MOE_RT)
    w_spec = pl.BlockSpec((None, 1, D, MOE_D_FF), lambda e, r: (layer, e, 0, 0))
    return pl.pallas_call(
        _moe_ffn_kernel,
        grid=(E, R // rt),
        in_specs=[pl.BlockSpec((1, rt, D), lambda e, r: (e, r, 0)), w_spec, w_spec,
                  pl.BlockSpec((None, 1, MOE_D_FF, D), lambda e, r: (layer, e, 0, 0))],
        out_specs=pl.BlockSpec((1, rt, D), lambda e, r: (e, r, 0)),
        out_shape=jax.ShapeDtypeStruct((E, R, D), BF16),
        scratch_shapes=[pltpu.VMEM((D, MOE_D_FF), BF16), pltpu.VMEM((D, MOE_D_FF), BF16), pltpu.VMEM((MOE_D_FF, D), BF16)],
        compiler_params=pltpu.CompilerParams(dimension_semantics=("arbitrary", "arbitrary"),
                                             vmem_limit_bytes=VMEM_LIMIT_BYTES),
        name="moe_ffn",
    )(xg, w_gate, w_up, w_down)


def moe_scatter(slot, aff, start, y, x, gate):
    B, T, D = x.shape
    cap = EC_CAPACITY * T // N_EXPERTS
    per_request = gate.shape[0] == B
    out_shape = jax.ShapeDtypeStruct((B, T, D), F32)
    if T <= MOE_TT:
        whole = lambda b: (b, 0, 0)
        return pl.pallas_call(
            functools.partial(_moe_scatter_short_kernel, cap=cap),
            grid=(B,),
            in_specs=[pl.BlockSpec((1, T, LANES), whole), pl.BlockSpec((1, T, LANES), whole),
                      pl.BlockSpec((N_EXPERTS, cap, D), lambda b: (0, b, 0)), pl.BlockSpec((1, T, D), whole),
                      pl.BlockSpec((1, 1, D), whole if per_request else (lambda b: (0, 0, 0)))],
            out_specs=pl.BlockSpec((1, T, D), whole),
            out_shape=out_shape,
            compiler_params=pltpu.CompilerParams(dimension_semantics=("arbitrary",), vmem_limit_bytes=VMEM_LIMIT_BYTES),
            name="moe_scatter_short",
        )(slot, aff, y, x, gate)
    tt = MOE_TT
    tile = lambda b, k, st: (b, k, 0)
    return pl.pallas_call(
        _moe_scatter_kernel,
        grid_spec=pltpu.PrefetchScalarGridSpec(
            num_scalar_prefetch=1,
            grid=(B, T // tt),
            in_specs=[pl.BlockSpec((1, tt, LANES), tile), pl.BlockSpec((1, tt, LANES), tile),
                      pl.BlockSpec((N_EXPERTS, cap, D), lambda b, k, st: (0, b, 0), pipeline_mode=pl.Buffered(1)),
                      pl.BlockSpec((1, tt, D), tile),
                      pl.BlockSpec((1, 1, D), (lambda b, k, st: (b, 0, 0)) if per_request else (lambda b, k, st: (0, 0, 0)))],
            out_specs=pl.BlockSpec((1, tt, D), tile),
            scratch_shapes=[pltpu.VMEM((tt, D), F32)],
        ),
        out_shape=out_shape,
        compiler_params=pltpu.CompilerParams(dimension_semantics=("arbitrary",) * 2, vmem_limit_bytes=VMEM_LIMIT_BYTES),
        name="moe_scatter",
    )(start, slot, aff, y, x, gate)


def moe_block(x, g, shift, scale, gate, w_router, w_gate, w_up, w_down, layer):
    h, aff, slot, start = moe_route(x, g, shift, scale, w_router)
    y = moe_ffn(moe_gather(h, slot, start), w_gate, w_up, w_down, layer)
    return moe_scatter(slot, aff, start, y, x, gate)


HY_TM = 512
HY_TK = 1024
HY_BG = 8
HY_BG_LONG = 2


def dft_tables(L):
    blk = min(L, HY_TM)
    t = jnp.arange(L, dtype=jnp.int32)

    def angles(f):
        return ((f[:, None] * t[None, :]) % (2 * L)).astype(F32) * (math.pi / L)

    a_hi = angles(jnp.arange(0, L, blk, dtype=jnp.int32))[:, None, :]
    a_lo = angles(jnp.arange(blk, dtype=jnp.int32))[None, :, :]
    cos_t = (jnp.cos(a_hi) * jnp.cos(a_lo) - jnp.sin(a_hi) * jnp.sin(a_lo)).reshape(L, L)
    sin_t = (jnp.sin(a_hi) * jnp.cos(a_lo) + jnp.cos(a_hi) * jnp.sin(a_lo)).reshape(L, L)
    return cos_t.astype(BF16), sin_t.astype(BF16)


def _alt_sign(rows, first_row):
    t = lax.broadcasted_iota(jnp.int32, (rows, 1), 0) + first_row
    return jnp.where(t % 2 == 0, 1.0, -1.0).astype(F32)


def _hy_prep_kernel(x_ref, prev_ref, next_ref, cw_ref, cb_ref, o_ref, v16_ref):
    i = pl.program_id(1)
    n = pl.num_programs(1)
    x = x_ref[0]
    tm = x.shape[0]
    row = lax.broadcasted_iota(jnp.int32, x.shape, 0)
    prev_row = jnp.where(i > 0, prev_ref[0, SUBLANES - 1:SUBLANES, :], 0.0)
    next_row = jnp.where(i < n - 1, next_ref[0, 0:1, :], 0.0)
    x_prev = jnp.where(row == 0, prev_row, pltpu.roll(x, 1, axis=0))
    x_next = jnp.where(row == tm - 1, next_row, pltpu.roll(x, tm - 1, axis=0))
    y = x_prev * cw_ref[0:1, :] + x * cw_ref[1:2, :] + x_next * cw_ref[2:3, :] + cb_ref[...]
    o_ref[0] = y
    v16_ref[0] = y[:, :HY_CH].astype(BF16)


def hyena_prep(zh, conv_w, conv_b):
    B, L, W = zh.shape
    tm = min(L, GDN_TM)
    nb = tm // SUBLANES
    return pl.pallas_call(
        _hy_prep_kernel,
        grid=(B, L // tm),
        in_specs=[
            pl.BlockSpec((1, tm, W), lambda b, i: (b, i, 0)),
            pl.BlockSpec((1, SUBLANES, W), lambda b, i: (b, jnp.maximum(i * nb - 1, 0), 0)),
            pl.BlockSpec((1, SUBLANES, W), lambda b, i: (b, jnp.minimum((i + 1) * nb, L // SUBLANES - 1), 0)),
            pl.BlockSpec((HY_SHORT, W), lambda b, i: (0, 0)),
            pl.BlockSpec((1, W), lambda b, i: (0, 0)),
        ],
        out_specs=[pl.BlockSpec((1, tm, W), lambda b, i: (b, i, 0)), pl.BlockSpec((1, tm, HY_CH), lambda b, i: (b, i, 0))],
        out_shape=[jax.ShapeDtypeStruct((B, L, W), F32), jax.ShapeDtypeStruct((B, L, HY_CH), BF16)],
        compiler_params=pltpu.CompilerParams(dimension_semantics=("arbitrary", "arbitrary"),
                                             vmem_limit_bytes=VMEM_LIMIT_BYTES),
        name="hyena_prep",
    )(zh, zh, zh, conv_w.astype(F32), conv_b.astype(F32)[None, :])


def _hy_taps_kernel(feat_ref, w1_ref, b1_ref, w2_ref, b2_ref, fr_ref, w3f_ref, w3b_ref, dec_ref, sum_ref, dif_ref):
    L = feat_ref.shape[0]
    fr = fr_ref[...]
    h = jnp.sin(fr * (jnp.dot(feat_ref[...], w1_ref[...], precision=HIGHEST, preferred_element_type=F32) + b1_ref[...]))
    h = jnp.sin(fr * (jnp.dot(h, w2_ref[...], precision=HIGHEST, preferred_element_type=F32) + b2_ref[...]))
    t = lax.broadcasted_iota(jnp.int32, (L, 1), 0)
    window = jnp.exp(-(t.astype(F32) / (L - 1)) * dec_ref[...]) + HY_SHIFT
    fwd = jnp.dot(h, w3f_ref[...], precision=HIGHEST, preferred_element_type=F32) * window
    bwd = jnp.where(t == 0, 0.0, jnp.dot(h, w3b_ref[...], precision=HIGHEST, preferred_element_type=F32) * window)
    inv = 1.0 / (jnp.sum(jnp.abs(fwd), axis=0, keepdims=True) + jnp.sum(jnp.abs(bwd), axis=0, keepdims=True))
    sum_ref[...] = (fwd + bwd) * inv
    dif_ref[...] = (bwd - fwd) * inv


def hyena_taps(L, w1, b1, w2, b2, w3, freq):
    f32 = F32
    t = jnp.linspace(0.0, 1.0, L, dtype=f32)[:, None]
    bands = (HY_EMB - 1) // 2
    omega = 2.0 * math.pi * jnp.arange(L, dtype=f32)[:, None] / L
    fb = jnp.linspace(1e-4, bands - 1, bands, dtype=f32)[None, :]
    feats = jnp.concatenate([t, jnp.cos(fb * omega), -jnp.sin(fb * omega)], axis=-1)
    max_decay = math.log(HY_DECAY_TARGET) / HY_FAST_DECAY
    min_decay = math.log(HY_DECAY_TARGET) / HY_SLOW_DECAY
    deltas = jnp.abs(jnp.linspace(min_decay, max_decay, HY_CH, dtype=f32))
    dec = jnp.tile(deltas, HY_ORDER)[None, :]
    n_col = HY_ORDER * HY_CH
    cb = 256
    full = lambda shape: pl.BlockSpec(shape, lambda j: (0, 0))
    col = lambda rows: pl.BlockSpec((rows, cb), lambda j: (0, j))
    return pl.pallas_call(
        _hy_taps_kernel,
        grid=(n_col // cb,),
        in_specs=[full((L, HY_EMB)), full((HY_EMB, HY_FILT_W)), full((1, HY_FILT_W)), full((HY_FILT_W, HY_FILT_W)),
                  full((1, HY_FILT_W)), full((1, HY_FILT_W)), col(HY_FILT_W), col(HY_FILT_W), col(1)],
        out_specs=[col(L), col(L)],
        out_shape=[jax.ShapeDtypeStruct((L, n_col), f32)] * 2,
        compiler_params=pltpu.CompilerParams(dimension_semantics=("arbitrary",), vmem_limit_bytes=VMEM_LIMIT_BYTES),
        name="hyena_taps",
    )(feats, w1.astype(f32), b1.astype(f32)[None, :], w2.astype(f32), b2.astype(f32)[None, :], freq.astype(f32)[None, :],
      w3.astype(f32)[:, :n_col], w3.astype(f32)[:, n_col:], dec)


def _hy_spec_kernel(ch_ref, sh_ref, sum_ref, dif_ref, hr_ref, hi_ref, ny_ref, acc_r, acc_i, acc_n):
    m, k = pl.program_id(1), pl.program_id(2)
    tk = sum_ref.shape[0]

    @pl.when(k == 0)
    def _():
        acc_r[...] = jnp.zeros_like(acc_r)
        acc_i[...] = jnp.zeros_like(acc_i)

    @pl.when(jnp.logical_and(k == 0, m == 0))
    def _():
        acc_n[...] = jnp.zeros_like(acc_n)

    a = sum_ref[...]
    acc_r[...] += _dot(ch_ref[...], a.astype(BF16))
    acc_i[...] += _dot(sh_ref[...], dif_ref[...].astype(BF16))

    @pl.when(m == 0)
    def _():
        acc_n[...] += jnp.sum(a * _alt_sign(tk, k * tk), axis=0, keepdims=True)

    @pl.when(k == pl.num_programs(2) - 1)
    def _():
        hr_ref[...] = acc_r[...]
        hi_ref[...] = acc_i[...]
        ny_ref[...] = jnp.broadcast_to(acc_n[...], ny_ref.shape)


def hyena_spectrum(tables, tap_sum, tap_dif):
    L, N = tap_sum.shape
    tm = min(L, HY_TM)
    cb = 512
    tab = pl.BlockSpec((tm, tm), lambda j, m, k: (m, k))
    dat = pl.BlockSpec((tm, cb), lambda j, m, k: (k, j))
    return pl.pallas_call(
        _hy_spec_kernel,
        grid=(N // cb, L // tm, L // tm),
        in_specs=[tab, tab, dat, dat],
        out_specs=[pl.BlockSpec((tm, cb), lambda j, m, k: (m, j)), pl.BlockSpec((tm, cb), lambda j, m, k: (m, j)),
                   pl.BlockSpec((SUBLANES, cb), lambda j, m, k: (0, j))],
        out_shape=[jax.ShapeDtypeStruct((L, N), F32), jax.ShapeDtypeStruct((L, N), F32),
                   jax.ShapeDtypeStruct((SUBLANES, N), F32)],
        scratch_shapes=[pltpu.VMEM((tm, cb), F32), pltpu.VMEM((tm, cb), F32), pltpu.VMEM((1, cb), F32)],
        compiler_params=pltpu.CompilerParams(dimension_semantics=("arbitrary",) * 3, vmem_limit_bytes=VMEM_LIMIT_BYTES),
        name="hyena_spectrum",
    )(*tables, tap_sum, tap_dif)


def _hy_fwd_kernel(ch_ref, sh_ref, u_ref, hr_ref, hi_ref, hny_ref, yr_ref, yi_ref, yny_ref,
                   acc_c, acc_s, acc_n):
    m, k = pl.program_id(1), pl.program_id(2)
    nb, tk = u_ref.shape[0], u_ref.shape[1]
    tm = acc_c.shape[1]

    @pl.when(k == 0)
    def _():
        acc_c[...] = jnp.zeros_like(acc_c)
        acc_s[...] = jnp.zeros_like(acc_s)

    @pl.when(jnp.logical_and(k == 0, m == 0))
    def _():
        acc_n[...] = jnp.zeros_like(acc_n)

    sign = _alt_sign(tk, k * tk)
    for b in range(nb):
        ub = u_ref[b]
        acc_c[b] += _dot(ch_ref[...], ub)
        acc_s[b] += _dot(sh_ref[...], ub)

        @pl.when(m == 0)
        def _():
            acc_n[b] += jnp.sum(ub.astype(F32) * sign, axis=0, keepdims=True)

    @pl.when(k == pl.num_programs(2) - 1)
    def _():
        f = lax.broadcasted_iota(jnp.int32, (tm, 1), 0) + m * tm
        dc = jnp.where(f == 0, 0.5, 1.0).astype(F32)
        hr, hi = hr_ref[...], hi_ref[...]
        for b in range(nb):
            xr, xs = acc_c[b], acc_s[b]
            yr_ref[b] = ((xr * hr + xs * hi) * dc).astype(yr_ref.dtype)
            yi_ref[b] = (xr * hi - xs * hr).astype(yi_ref.dtype)
            yny_ref[b] = jnp.broadcast_to(acc_n[b] * hny_ref[0:1, :], yny_ref.shape[1:])


def _hy_inv_kernel(ch_ref, sh_ref, yr_ref, yi_ref, yny_ref, u_ref, xg_ref, skip_ref, *rest):
    o_refs, acc = rest[:-1], rest[-1]
    m, k = pl.program_id(1), pl.program_id(2)
    nb = yr_ref.shape[0]
    tm = acc.shape[1]
    L = tm * pl.num_programs(1)

    @pl.when(k == 0)
    def _():
        acc[...] = jnp.zeros_like(acc)

    for b in range(nb):
        acc[b] += _dot(ch_ref[...], yr_ref[b]) - _dot(sh_ref[...], yi_ref[b])

    @pl.when(k == pl.num_programs(2) - 1)
    def _():
        sign = _alt_sign(tm, m * tm)
        for b in range(nb):
            u = u_ref[b]
            y = acc[b] * (1.0 / L) + sign * yny_ref[b, 0:1, :] * (0.5 / L)
            res = xg_ref[b] * (y + u * skip_ref[...])
            for o_ref in o_refs:
                o_ref[b] = res.astype(o_ref.dtype)


def hyena_long_conv(tables, zf, u, u16, u_col, gate_col, hr, hi, hny, order, skip, out_dtypes):
    B, L, _ = u.shape
    C = HY_CH
    tm = min(L, HY_TM)
    tk = min(L, HY_TK)
    bg = min(B, HY_BG_LONG if L >= HY_TM else HY_BG)
    assert B % bg == 0
    grid = (B // bg, L // tm, L // tk)
    tab = pl.BlockSpec((tm, tk), lambda g, m, k: (m, k))
    params = pltpu.CompilerParams(dimension_semantics=("arbitrary",) * 3, vmem_limit_bytes=VMEM_LIMIT_BYTES)
    spec_m = pl.BlockSpec((tm, C), lambda g, m, k: (m, order))
    yr, yi, yny = pl.pallas_call(
        _hy_fwd_kernel,
        grid=grid,
        in_specs=[tab, tab, pl.BlockSpec((bg, tk, C), lambda g, m, k: (g, k, 0)), spec_m, spec_m,
                  pl.BlockSpec((SUBLANES, C), lambda g, m, k: (0, order))],
        out_specs=[pl.BlockSpec((bg, tm, C), lambda g, m, k: (g, m, 0)), pl.BlockSpec((bg, tm, C), lambda g, m, k: (g, m, 0)),
                   pl.BlockSpec((bg, SUBLANES, C), lambda g, m, k: (g, 0, 0))],
        out_shape=[jax.ShapeDtypeStruct((B, L, C), BF16), jax.ShapeDtypeStruct((B, L, C), BF16),
                   jax.ShapeDtypeStruct((B, SUBLANES, C), F32)],
        scratch_shapes=[pltpu.VMEM((bg, tm, C), F32), pltpu.VMEM((bg, tm, C), F32), pltpu.VMEM((bg, 1, C), F32)],
        compiler_params=params,
        name="hyena_fwd",
    )(*tables, u16, hr, hi, hny)
    return pl.pallas_call(
        _hy_inv_kernel,
        grid=grid,
        in_specs=[tab, tab, pl.BlockSpec((bg, tk, C), lambda g, m, k: (g, k, 0)),
                  pl.BlockSpec((bg, tk, C), lambda g, m, k: (g, k, 0)),
                  pl.BlockSpec((bg, SUBLANES, C), lambda g, m, k: (g, 0, 0)),
                  pl.BlockSpec((bg, tm, C), lambda g, m, k: (g, m, u_col)),
                  pl.BlockSpec((bg, tm, C), lambda g, m, k: (g, m, gate_col)),
                  pl.BlockSpec((1, C), lambda g, m, k: (0, 0))],
        out_specs=[pl.BlockSpec((bg, tm, C), lambda g, m, k: (g, m, 0)) for _ in out_dtypes],
        out_shape=[jax.ShapeDtypeStruct((B, L, C), dt) for dt in out_dtypes],
        scratch_shapes=[pltpu.VMEM((bg, tm, C), F32)],
        compiler_params=params,
        name="hyena_inv",
    )(*tables, yr, yi, yny, u, zf, skip.astype(F32)[order][None, :])


def hyena_filter_spectrum(L, w1, b1, w2, b2, w3, freq):
    tables = dft_tables(L)
    tap_sum, tap_dif = hyena_taps(L, w1, b1, w2, b2, w3, freq)
    return tables, hyena_spectrum(tables, tap_sum, tap_dif)


def hyena_mixer_pallas(zh, conv_w, conv_b, skip, tables, spectrum):
    hr, hi, hny = spectrum
    zf, v16 = hyena_prep(zh, conv_w, conv_b)
    y1, y1_16 = hyena_long_conv(tables, zf, zf, v16, 0, 1, hr, hi, hny, 0, skip, [F32, BF16])
    return hyena_long_conv(tables, zf, y1, y1_16, 0, 2, hr, hi, hny, 1, skip, [BF16])[0]


def rms_norm(x, g):
    xf = x.astype(jnp.float32)
    y = xf * lax.rsqrt(jnp.mean(xf * xf, axis=-1, keepdims=True) + EPS)
    return (y * g.astype(jnp.float32)).astype(x.dtype)


def _ada_kernel(c_ref, w_ref, b_ref, o_ref):
    cnd = c_ref[...]
    act = cnd * jax.nn.sigmoid(cnd)
    o_ref[0] = jnp.dot(act, w_ref[0], precision=HIGHEST, preferred_element_type=F32) + b_ref[0]


def ada_params_all(cond, w_ada, b_ada):
    N, D = cond.shape
    depth, _, W = w_ada.shape
    rows = -(-N // SUBLANES) * SUBLANES
    cond_p = jnp.concatenate([cond.astype(F32), jnp.zeros((rows - N, D), F32)], axis=0)
    cb = D
    out = pl.pallas_call(
        _ada_kernel,
        grid=(depth, W // cb),
        in_specs=[pl.BlockSpec((rows, D), lambda l, j: (0, 0)), pl.BlockSpec((1, D, cb), lambda l, j: (l, 0, j)),
                  pl.BlockSpec((1, 1, cb), lambda l, j: (l, 0, j))],
        out_specs=pl.BlockSpec((1, rows, cb), lambda l, j: (l, 0, j)),
        out_shape=jax.ShapeDtypeStruct((depth, rows, W), F32),
        compiler_params=pltpu.CompilerParams(dimension_semantics=("arbitrary", "arbitrary"),
                                             vmem_limit_bytes=VMEM_LIMIT_BYTES),
        name="ada_params",
    )(cond_p, w_ada.astype(F32), b_ada.astype(F32)[:, None, :])
    return out[:, :N]


def _final_norm_kernel(x_ref, g_ref, o_ref):
    x = x_ref[...]
    o_ref[...] = x * lax.rsqrt(jnp.mean(x * x, axis=-1, keepdims=True) + EPS) * g_ref[...]


def final_norm(x, g, tm=512):
    B, L, D = x.shape
    M = B * L
    assert M % tm == 0
    out = pl.pallas_call(
        _final_norm_kernel,
        grid=(M // tm,),
        in_specs=[pl.BlockSpec((tm, D), lambda i: (i, 0)), pl.BlockSpec((1, D), lambda i: (0, 0))],
        out_specs=pl.BlockSpec((tm, D), lambda i: (i, 0)),
        out_shape=jax.ShapeDtypeStruct((M, D), F32),
        compiler_params=pltpu.CompilerParams(dimension_semantics=("arbitrary",), vmem_limit_bytes=VMEM_LIMIT_BYTES),
        name="final_norm",
    )(x.reshape(M, D), g.astype(F32)[None, :])
    return out.reshape(B, L, D)


def ada_params(cond, w, b):
    m = jax.nn.silu(cond) @ w + b
    return jnp.split(m[:, None, :], 6, axis=-1)


def modulate(x, g, shift, scale):
    return rms_norm(x, g) * (1 + scale) + shift


def axial_rope(T):
    t = jnp.arange(T)
    n_freq = HEAD_DIM // 4
    inv = ROPE_BASE ** (-jnp.arange(n_freq, dtype=jnp.float32) / n_freq)
    ang = jnp.concatenate([(t // GRID_W).astype(jnp.float32)[:, None] * inv,
                           (t % GRID_W).astype(jnp.float32)[:, None] * inv], axis=-1)
    return jnp.cos(ang)[:, None, :], jnp.sin(ang)[:, None, :]


def apply_rope(x, cos, sin):
    xf = x.astype(jnp.float32)
    x1, x2 = jnp.split(xf, 2, axis=-1)
    return jnp.concatenate([x1 * cos - x2 * sin, x2 * cos + x1 * sin], axis=-1).astype(x.dtype)


def softmax_parts(parts, sink=None):
    sizes = [p.shape[-1] for p in parts]
    cols = list(parts)
    if sink is not None:
        cols.append(jnp.broadcast_to(sink, parts[0].shape[:-1] + (1,)))
    p = jax.nn.softmax(jnp.concatenate(cols, axis=-1), axis=-1)
    pieces = jnp.split(p, np.cumsum(sizes).tolist(), axis=-1)
    return pieces[:len(sizes)]


def context_attention(q, k, v, sink):
    B, S, HQ, hd = q.shape
    HK = k.shape[2]
    G = HQ // HK
    nb = S // Q_BLOCK
    scale = hd ** -0.5
    sink_b = None if sink is None else sink.astype(jnp.float32).reshape(1, HK, G, 1, 1)
    qb = q.reshape(B, nb, Q_BLOCK, HK, G, hd).swapaxes(0, 1)

    def one(qi):
        s = jnp.einsum('bqkgd,bskd->bkgqs', qi, k, preferred_element_type=jnp.float32) * scale
        (p,) = softmax_parts([s], sink_b)
        return jnp.einsum('bkgqs,bskd->bqkgd', p.astype(v.dtype), v)

    o = lax.map(one, qb)
    return o.swapaxes(0, 1).reshape(B, S, HQ, hd)


def window_attention(q, k, v, ck, cv, sink):
    B, T, HQ, hd = q.shape
    HK = k.shape[2]
    G = HQ // HK
    nb = T // A_BLOCK
    scale = hd ** -0.5
    qb = q.reshape(B, nb, A_BLOCK, HK, G, hd).swapaxes(0, 1)

    def band(x):
        xp = jnp.pad(x, ((0, 0), (A_BLOCK, A_BLOCK), (0, 0), (0, 0))).reshape(B, nb + 2, A_BLOCK, HK, hd)
        return jnp.concatenate([xp[:, :-2], xp[:, 1:-1], xp[:, 2:]], axis=2).swapaxes(0, 1)

    kb, vb = band(k), band(v)
    qpos = jnp.arange(nb)[:, None, None] * A_BLOCK + jnp.arange(A_BLOCK)[None, :, None]
    kpos = jnp.arange(nb)[:, None, None] * A_BLOCK - A_BLOCK + jnp.arange(3 * A_BLOCK)[None, None, :]
    mask = (jnp.abs(kpos - qpos) <= A_WINDOW) & (kpos >= 0) & (kpos < T)
    sink_b = sink.astype(jnp.float32).reshape(1, HK, G, 1, 1)

    def one(xs):
        qi, ki, vi, mi = xs
        s_loc = jnp.einsum('bqkgd,bskd->bkgqs', qi, ki, preferred_element_type=jnp.float32) * scale
        s_loc = jnp.where(mi[None, None, None], s_loc, NEG_INF)
        s_ctx = jnp.einsum('bqkgd,bpkd->bkgqp', qi, ck, preferred_element_type=jnp.float32) * scale
        p_loc, p_ctx = softmax_parts([s_loc, s_ctx], sink_b)
        return (jnp.einsum('bkgqs,bskd->bqkgd', p_loc.astype(vi.dtype), vi)
                + jnp.einsum('bkgqp,bpkd->bqkgd', p_ctx.astype(cv.dtype), cv))

    o = lax.map(one, (qb, kb, vb, mask))
    return o.swapaxes(0, 1).reshape(B, T, HQ, hd)


def short_conv(x, w):
    K = w.shape[0]
    L = x.shape[1]
    pad = K // 2
    xp = jnp.pad(x, ((0, 0), (pad, pad), (0, 0)))
    return sum(xp[:, i:i + L] * w[i] for i in range(K))


def hyena_filter_bank(L, w1, b1, w2, b2, w3, freq):
    f32 = jnp.float32
    t = jnp.linspace(0.0, 1.0, L, dtype=f32)[:, None]
    bands = (HY_EMB - 1) // 2
    omega = 2.0 * math.pi * jnp.arange(L, dtype=f32)[:, None] / L
    fb = jnp.linspace(1e-4, bands - 1, bands, dtype=f32)[None, :]
    feats = jnp.concatenate([t, jnp.cos(fb * omega), -jnp.sin(fb * omega)], axis=-1)
    fr = freq.astype(f32)
    h = jnp.sin(fr * (feats @ w1.astype(f32) + b1.astype(f32)))
    h = jnp.sin(fr * (h @ w2.astype(f32) + b2.astype(f32)))
    h = (h @ w3.astype(f32)).reshape(L, 2, HY_ORDER, HY_CH)
    max_decay = math.log(HY_DECAY_TARGET) / HY_FAST_DECAY
    min_decay = math.log(HY_DECAY_TARGET) / HY_SLOW_DECAY
    deltas = jnp.abs(jnp.linspace(min_decay, max_decay, HY_CH, dtype=f32))
    h = h * (jnp.exp(-t * deltas) + HY_SHIFT)[:, None, None, :]
    taps = jnp.concatenate([h[:, 0], jnp.zeros((1, HY_ORDER, HY_CH), f32), h[:0:-1, 1]], axis=0)
    taps = taps / jnp.sum(jnp.abs(taps), axis=0, keepdims=True)
    return jnp.fft.rfft(taps, axis=0)


def hyena_mixer(z, conv_w, conv_b, w1, b1, w2, b2, w3, freq, skip):
    L = z.shape[1]
    zf = (short_conv(z, conv_w) + conv_b).astype(jnp.float32)
    v, x1, x2 = jnp.split(zf, 3, axis=-1)
    filt = hyena_filter_bank(L, w1, b1, w2, b2, w3, freq)
    skip = skip.astype(jnp.float32)

    def long_conv(u, o):
        y = jnp.fft.irfft(jnp.fft.rfft(u, n=2 * L, axis=1) * filt[None, :, o], n=2 * L, axis=1)[:, :L]
        return y + u * skip[o]

    y = x1 * long_conv(v, 0)
    y = x2 * long_conv(y, 1)
    return y.astype(z.dtype)


def l2norm(x):
    xf = x.astype(jnp.float32)
    return xf * lax.rsqrt(jnp.sum(xf * xf, axis=-1, keepdims=True) + EPS)


def chunk_gated_delta(q, k, v, g, beta, s0):
    B, L, H, dk = q.shape
    dv = v.shape[-1]
    n = L // C_CHUNK

    def chunks(x):
        x = x.reshape((B, n, C_CHUNK, H) + x.shape[3:])
        return jnp.moveaxis(jnp.moveaxis(x, 1, 0), 3, 2)

    qc, kc, vc, bc = chunks(q), chunks(k), chunks(v), chunks(beta)
    gc = jnp.cumsum(chunks(g), axis=-1)
    tri = jnp.tril(jnp.ones((C_CHUNK, C_CHUNK), bool))
    strict = jnp.tril(jnp.ones((C_CHUNK, C_CHUNK), bool), k=-1)
    gamma = jnp.exp(jnp.where(tri, gc[..., :, None] - gc[..., None, :], NEG_INF))
    kb = kc * bc[..., None]
    a_mat = jnp.where(strict, jnp.einsum('nbhid,nbhjd->nbhij', kb, kc) * gamma, 0.0) + jnp.eye(C_CHUNK, dtype=jnp.float32)
    rhs = jnp.concatenate([vc * bc[..., None], kb * jnp.exp(gc)[..., None]], axis=-1)
    sol = lax.linalg.triangular_solve(a_mat, rhs, left_side=True, lower=True)
    u, w = sol[..., :dv], sol[..., dv:]
    attn = jnp.where(tri, jnp.einsum('nbhid,nbhjd->nbhij', qc, kc) * gamma, 0.0)
    g_last = gc[..., -1]
    q_dec = qc * jnp.exp(gc)[..., None]
    k_dec = kc * jnp.exp(g_last[..., None] - gc)[..., None]

    def step(S, xs):
        u_i, w_i, a_i, qd, kd, gl = xs
        v_new = u_i - jnp.einsum('bhck,bhkv->bhcv', w_i, S)
        o = jnp.einsum('bhck,bhkv->bhcv', qd, S) + jnp.einsum('bhij,bhjv->bhiv', a_i, v_new)
        S = S * jnp.exp(gl)[..., None, None] + jnp.einsum('bhck,bhcv->bhkv', kd, v_new)
        return S, o

    S, o = lax.scan(step, s0.astype(jnp.float32), (u, w, attn, q_dec, k_dec, g_last))
    o = jnp.moveaxis(jnp.moveaxis(o, 2, 3), 0, 1).reshape(B, L, H, dv)
    return o, S


def deltanet_mixer(zq, zk, zv, zg, za, zb, conv_w, a_log, dt_bias, norm_w, s0):
    B, L, _ = zq.shape
    qkv = jax.nn.silu(short_conv(jnp.concatenate([zq, zk, zv], axis=-1), conv_w))
    q, k, v = [t.reshape(B, L, C_HEADS, HEAD_DIM) for t in jnp.split(qkv, 3, axis=-1)]
    q = l2norm(q) * (HEAD_DIM ** -0.5)
    k = l2norm(k)
    v = v.astype(jnp.float32)
    beta = jax.nn.sigmoid(zb.astype(jnp.float32))
    g = -jnp.exp(a_log.astype(jnp.float32)) * jax.nn.softplus(za.astype(jnp.float32) + dt_bias.astype(jnp.float32))
    o_f, s_f = chunk_gated_delta(q, k, v, g[:, :, 0], beta[:, :, 0], s0[:, 0])
    o_b, s_b = chunk_gated_delta(q[:, ::-1], k[:, ::-1], v[:, ::-1], g[:, ::-1, 1], beta[:, ::-1, 1], s0[:, 1])
    o = o_f + o_b[:, ::-1]
    gate = jax.nn.silu(zg.reshape(B, L, C_HEADS, HEAD_DIM).astype(jnp.float32))
    o = rms_norm(o, norm_w) * gate
    return o.reshape(B, L, C_W).astype(zq.dtype), jnp.stack([s_f, s_b], axis=1)


def neighbourhood_attention(q, k, v, ck, cv, rpb):
    B, T, H, hd = q.shape
    rows = T // GRID_W
    kh = min(NA_KH_MAX, rows)
    scale = hd ** -0.5
    r = jnp.arange(rows)
    rs = jnp.clip(r - kh // 2, 0, rows - kh)
    key_rows = rs[:, None] + jnp.arange(kh)[None, :]
    idx = (key_rows[:, :, None] * GRID_W + jnp.arange(GRID_W)).reshape(rows, kh * GRID_W)
    col = jnp.arange(GRID_W)
    cs = jnp.clip(col - NA_KW // 2, 0, GRID_W - NA_KW)
    kcol = jnp.tile(col, kh)
    col_ok = (kcol[None, :] >= cs[:, None]) & (kcol[None, :] < cs[:, None] + NA_KW)
    roff = jnp.repeat(key_rows - r[:, None], GRID_W, axis=1) + NA_KH_MAX - 1
    coff = jnp.clip(kcol[None, :] - col[:, None] + NA_KW - 1, 0, 2 * NA_KW - 2)
    qr = q.reshape(B, rows, GRID_W, H, hd).swapaxes(0, 1)
    rpb_f = rpb.astype(jnp.float32)

    def one(xs):
        qi, ii, ro = xs
        ki = k[:, ii]
        vi = v[:, ii]
        bias = rpb_f[:, ro[None, :], coff]
        s_loc = jnp.einsum('bqhd,bkhd->bhqk', qi, ki, preferred_element_type=jnp.float32) * scale + bias[None]
        s_loc = jnp.where(col_ok[None, None], s_loc, NEG_INF)
        s_ctx = jnp.einsum('bqhd,bphd->bhqp', qi, ck, preferred_element_type=jnp.float32) * scale
        p_loc, p_ctx = softmax_parts([s_loc, s_ctx])
        return (jnp.einsum('bhqk,bkhd->bqhd', p_loc.astype(vi.dtype), vi)
                + jnp.einsum('bhqp,bphd->bqhd', p_ctx.astype(cv.dtype), cv))

    o = lax.map(one, (qr, idx, roff))
    return o.swapaxes(0, 1).reshape(B, T, H, hd)


def expert_choice_ffn(h, w_router, w_gate, w_up, w_down):
    B, T, D = h.shape
    cap = EC_CAPACITY * T // N_EXPERTS
    aff = jax.nn.softmax(jnp.einsum('btd,de->bte', h, w_router, preferred_element_type=jnp.float32), axis=-1)
    gate, idx = lax.top_k(aff.swapaxes(1, 2), cap)
    xg = jax.vmap(lambda hb, ib: hb[ib])(h, idx)
    a = jnp.einsum('becd,edf->becf', xg, w_gate)
    u = jnp.einsum('becd,edf->becf', xg, w_up)
    y = jnp.einsum('becf,efd->becd', jax.nn.silu(a) * u, w_down) * gate[..., None].astype(h.dtype)
    return jax.vmap(lambda yb, ib: jnp.zeros((T, D), yb.dtype).at[ib.reshape(-1)].add(yb.reshape(-1, D)))(y, idx)


def split_even(z):
    B, L = z.shape[:2]
    q = z[..., :A_Q_W].reshape(B, L, A_HEADS, HEAD_DIM)
    k = z[..., A_Q_W:A_Q_W + A_KV_W].reshape(B, L, A_KV_HEADS, HEAD_DIM)
    v = z[..., A_Q_W + A_KV_W:A_Q_W + 2 * A_KV_W].reshape(B, L, A_KV_HEADS, HEAD_DIM)
    return q, k, v, z[..., A_Q_W + 2 * A_KV_W:]


def split_odd(z):
    B, L = z.shape[:2]
    zq, zk, zv, zg = [z[..., i * C_W:(i + 1) * C_W] for i in range(4)]
    off = 4 * C_W
    za = z[..., off:off + 2 * C_HEADS].reshape(B, L, 2, C_HEADS)
    zb = z[..., off + 2 * C_HEADS:off + 4 * C_HEADS].reshape(B, L, 2, C_HEADS)
    off = off + 4 * C_HEADS
    nq, nk, nv = [z[..., off + i * D_W:off + (i + 1) * D_W].reshape(B, L, D_HEADS, HEAD_DIM) for i in range(3)]
    return zq, zk, zv, zg, za, zb, nq, nk, nv


def kernel(x_prompt, x_sample, cache_attn_k, cache_attn_v, state_delta, cache_na_k, cache_na_v,
           c, c_ctx, w_ada, b_ada, norm_mix, norm_ffn, norm_final,
           even_w_in, even_w_out, attn_sink, hy_conv_w, hy_conv_b, hy_w1, hy_b1, hy_w2, hy_b2,
           hy_w3, hy_freq, hy_skip, odd_w_in, odd_w_out, gdn_conv_w, gdn_a_log, gdn_dt_bias,
           gdn_norm, na_rpb, moe_router, moe_w_gate, moe_w_up, moe_w_down):
    xp, xs = x_prompt, x_sample
    bp = xp.shape[0]
    dft_p, dft_s = dft_tables(xp.shape[1]), dft_tables(xs.shape[1])
    ada = ada_params_all(jnp.concatenate([c_ctx[None, :], c], axis=0), w_ada, b_ada)
    new_ak, new_av, new_st, new_nk, new_nv = [], [], [], [], []
    for l in range(DEPTH):
        j = l // 2
        mp = jnp.split(ada[l, :1, None, :], 6, axis=-1)
        ms = jnp.split(ada[l, 1:, None, :], 6, axis=-1)
        mod_p = (norm_mix[l], mp[0], mp[1])
        mod_s = (norm_mix[l], ms[0], ms[1])
        if l % 2 == 0:
            def hyena(zh, tables):
                taps = hyena_taps(zh.shape[1], hy_w1[j], hy_b1[j], hy_w2[j], hy_b2[j], hy_w3[j], hy_freq[j])
                return hyena_mixer_pallas(zh, hy_conv_w[j], hy_conv_b[j], hy_skip[j], tables, hyena_spectrum(tables, *taps))

            w_in = even_w_in[j]
            w_groups = [w_in[:, :A_Q_W], w_in[:, A_Q_W:A_Q_W + A_KV_W], w_in[:, A_Q_W + A_KV_W:A_Q_W + 2 * A_KV_W],
                        w_in[:, A_Q_W + 2 * A_KV_W:]]
            q, k, v, zh = proj_multi(xp, *mod_p, w_groups, [F32] * 4)
            oa = context_attention_pallas(q, k, v, attn_sink[j], A_HEADS, A_KV_HEADS)
            xp_new = proj_concat(oa, hyena(zh, dft_p), even_w_out[j], xp, mp[2])
            new_ak.append(k.reshape(bp, SEQ, A_KV_HEADS, HEAD_DIM))
            new_av.append(v.reshape(bp, SEQ, A_KV_HEADS, HEAD_DIM))
            q, k, v, zh = proj_multi(xs, *mod_s, w_groups, [F32] * 4)
            ck = cache_attn_k[:, j].reshape(DEC_BATCH, PAST_LEN, A_KV_W).astype(BF16)
            cv = cache_attn_v[:, j].reshape(DEC_BATCH, PAST_LEN, A_KV_W).astype(BF16)
            oa = window_attention_pallas(q, k, v, ck, cv, attn_sink[j])
            xs_new = proj_concat(oa, hyena(zh, dft_s), even_w_out[j], xs, ms[2])
        else:
            w_in = odd_w_in[j]
            ab0 = 4 * C_W
            ab_cols = [w_in[:, ab0 + o * C_HEADS:ab0 + (o + 1) * C_HEADS] for o in (0, 2, 1, 3)]
            w_ab = jnp.concatenate(ab_cols + [jnp.zeros((D_MODEL, LANES - 4 * C_HEADS), w_in.dtype)], axis=1)
            n0 = ab0 + 4 * C_HEADS
            w_groups = [w_in[:, :3 * C_W], w_in[:, 3 * C_W:4 * C_W], w_ab,
                        w_in[:, n0:n0 + D_W], w_in[:, n0 + D_W:n0 + 2 * D_W], w_in[:, n0 + 2 * D_W:]]

            def deltanet(zqkv, zab, zg, s0):
                qd, kd, vd, gb = gdn_prep(zqkv, zab, gdn_conv_w[j], gdn_a_log[j], gdn_dt_bias[j])
                return gdn_scan(qd, kd, vd, gb, zg, s0, gdn_norm[j])

            zqkv, zg, zab, nq, nk, nv = proj_multi(xp, *mod_p, w_groups, [F32] * 6)
            oc, st = deltanet(zqkv, zab, zg, jnp.zeros((bp, 2, C_HEADS, HEAD_DIM, HEAD_DIM), F32))
            od = context_attention_pallas(nq, nk, nv, None, D_HEADS, D_HEADS)
            xp_new = proj_concat(oc, od, odd_w_out[j], xp, mp[2])
            new_st.append(st)
            new_nk.append(nk.reshape(bp, SEQ, D_HEADS, HEAD_DIM))
            new_nv.append(nv.reshape(bp, SEQ, D_HEADS, HEAD_DIM))
            zqkv, zg, zab, nq, nk, nv = proj_multi(xs, *mod_s, w_groups, [F32, F32, F32, BF16, BF16, BF16])
            oc, _ = deltanet(zqkv, zab, zg, state_delta[:, j])
            ck = cache_na_k[:, j].reshape(DEC_BATCH, PAST_LEN, D_W).astype(BF16)
            cv = cache_na_v[:, j].reshape(DEC_BATCH, PAST_LEN, D_W).astype(BF16)
            od = neighbourhood_attention_pallas(nq, nk, nv, ck, cv, na_rpb[j])
            xs_new = proj_concat(oc, od, odd_w_out[j], xs, ms[2])
        xp, xs = xp_new, xs_new
        moe = (moe_router[l], moe_w_gate, moe_w_up, moe_w_down, l)
        xp = moe_block(xp, norm_ffn[l], mp[3], mp[4], mp[5], *moe)
        xs = moe_block(xs, norm_ffn[l], ms[3], ms[4], ms[5], *moe)
    y_prompt = final_norm(xp, norm_final)
    y_sample = final_norm(xs, norm_final)
    return (y_prompt, y_sample, jnp.stack(new_ak, axis=1), jnp.stack(new_av, axis=1), jnp.stack(new_st, axis=1),
            jnp.stack(new_nk, axis=1), jnp.stack(new_nv, axis=1))
```

```python
import functools
import math
import jax, jax.numpy as jnp
from jax import lax
import numpy as np
from jax.experimental import pallas as pl
from jax.experimental.pallas import tpu as pltpu

D_MODEL = 1024
BATCH = 32
SEQ = 256
DEPTH = 4
DEC_BATCH = 4
DEC_SEQ = 4096
PAST_LEN = 512

GRID_W = 64
HEAD_DIM = 64
N_EVEN = (DEPTH + 1) // 2
N_ODD = DEPTH // 2
Q_BLOCK = 128
A_HEADS = D_MODEL // 128
A_KV_HEADS = A_HEADS // 4
A_WINDOW = 128
A_BLOCK = 128
ROPE_BASE = 10000.0
HY_CH = D_MODEL // 2
HY_ORDER = 2
HY_SHORT = 3
HY_EMB = 33
HY_FILT_W = 64
HY_FAST_DECAY = 0.3
HY_SLOW_DECAY = 1.5
HY_DECAY_TARGET = 1e-2
HY_SHIFT = 0.05
C_HEADS = D_MODEL // 128
C_SHORT = 3
C_CHUNK = 64
D_HEADS = D_MODEL // 128
NA_KH_MAX = 8
NA_KW = 16
N_EXPERTS = 16
EC_CAPACITY = 2
MOE_D_FF = D_MODEL
EPS = 1e-6
NEG_INF = -1e30

A_Q_W = A_HEADS * HEAD_DIM
A_KV_W = A_KV_HEADS * HEAD_DIM
EVEN_IN = A_Q_W + 2 * A_KV_W + 3 * HY_CH
EVEN_MIX = A_Q_W + HY_CH
C_W = C_HEADS * HEAD_DIM
D_W = D_HEADS * HEAD_DIM
ODD_IN = 4 * C_W + 4 * C_HEADS + 3 * D_W
ODD_MIX = C_W + D_W

VMEM_LIMIT_BYTES = 48 * 1024 * 1024


def _mm_kernel(x_ref, w_ref, o_ref):
    o_ref[...] = jnp.dot(x_ref[...].astype(jnp.bfloat16), w_ref[...], preferred_element_type=jnp.float32)


def pallas_matmul(x, w, tm=256):
    M, K = x.shape
    N = w.shape[1]
    assert M % tm == 0
    return pl.pallas_call(
        _mm_kernel,
        grid=(M // tm,),
        in_specs=[pl.BlockSpec((tm, K), lambda i: (i, 0)), pl.BlockSpec((K, N), lambda i: (0, 0))],
        out_specs=pl.BlockSpec((tm, N), lambda i: (i, 0)),
        out_shape=jax.ShapeDtypeStruct((M, N), jnp.float32),
        compiler_params=pltpu.CompilerParams(dimension_semantics=("arbitrary",), vmem_limit_bytes=VMEM_LIMIT_BYTES),
    )(x, w.astype(jnp.bfloat16))


def proj(x, w):
    B, L, K = x.shape
    return pallas_matmul(x.reshape(B * L, K), w).reshape(B, L, w.shape[1])


PROJ_TM = 512


def _request_of_tile(tm, L, per_request):
    return (lambda i: ((i * tm) // L, 0, 0)) if per_request else (lambda i: (0, 0, 0))


def _mm_multi_kernel(x_ref, g_ref, shift_ref, scale_ref, *refs):
    n = len(refs) // 2
    x = x_ref[...]
    y = x * lax.rsqrt(jnp.mean(x * x, axis=-1, keepdims=True) + EPS) * g_ref[...]
    h = (y * (1.0 + scale_ref[0]) + shift_ref[0]).astype(jnp.bfloat16)
    for w_ref, o_ref in zip(refs[:n], refs[n:]):
        o_ref[...] = jnp.dot(h, w_ref[...], preferred_element_type=jnp.float32).astype(o_ref.dtype)


def proj_multi(x, g, shift, scale, weights, out_dtypes, tm=PROJ_TM):
    B, L, K = x.shape
    M = B * L
    per_request = shift.shape[0] == B
    assert L % tm == 0 if per_request else M % tm == 0
    mod_spec = pl.BlockSpec((1, 1, K), _request_of_tile(tm, L, per_request))
    outs = pl.pallas_call(
        _mm_multi_kernel,
        grid=(M // tm,),
        in_specs=[pl.BlockSpec((tm, K), lambda i: (i, 0)), pl.BlockSpec((1, K), lambda i: (0, 0)), mod_spec, mod_spec]
        + [pl.BlockSpec(w.shape, lambda i: (0, 0)) for w in weights],
        out_specs=[pl.BlockSpec((tm, w.shape[1]), lambda i: (i, 0)) for w in weights],
        out_shape=[jax.ShapeDtypeStruct((M, w.shape[1]), dt) for w, dt in zip(weights, out_dtypes)],
        compiler_params=pltpu.CompilerParams(dimension_semantics=("arbitrary",), vmem_limit_bytes=VMEM_LIMIT_BYTES),
        name="in_projection",
    )(x.reshape(M, K), g.astype(jnp.float32)[None, :], shift, scale, *[w.astype(jnp.bfloat16) for w in weights])
    return [o.reshape(B, L, o.shape[1]) for o in outs]


def _mm2_kernel(a_ref, b_ref, wa_ref, wb_ref, x_ref, gate_ref, o_ref):
    mix = (jnp.dot(a_ref[...].astype(jnp.bfloat16), wa_ref[...], preferred_element_type=jnp.float32)
           + jnp.dot(b_ref[...].astype(jnp.bfloat16), wb_ref[...], preferred_element_type=jnp.float32))
    o_ref[...] = x_ref[...] + gate_ref[0] * mix


def proj_concat(a, b, w, x, gate, tm=PROJ_TM):
    B, L, Ka = a.shape
    Kb = b.shape[2]
    N = w.shape[1]
    M = B * L
    assert (L % tm == 0 if gate.shape[0] == B else M % tm == 0) and w.shape[0] == Ka + Kb
    wb16 = w.astype(jnp.bfloat16)
    out = pl.pallas_call(
        _mm2_kernel,
        grid=(M // tm,),
        in_specs=[pl.BlockSpec((tm, Ka), lambda i: (i, 0)), pl.BlockSpec((tm, Kb), lambda i: (i, 0)),
                  pl.BlockSpec((Ka, N), lambda i: (0, 0)), pl.BlockSpec((Kb, N), lambda i: (0, 0)),
                  pl.BlockSpec((tm, N), lambda i: (i, 0)),
                  pl.BlockSpec((1, 1, N), _request_of_tile(tm, L, gate.shape[0] == B))],
        out_specs=pl.BlockSpec((tm, N), lambda i: (i, 0)),
        out_shape=jax.ShapeDtypeStruct((M, N), jnp.float32),
        compiler_params=pltpu.CompilerParams(dimension_semantics=("arbitrary",), vmem_limit_bytes=VMEM_LIMIT_BYTES),
        name="out_projection",
    )(a.reshape(M, Ka), b.reshape(M, Kb), wb16[:Ka], wb16[Ka:], x.reshape(M, N), gate)
    return out.reshape(B, L, N)


LANES = 128
BF16 = jnp.bfloat16
F32 = jnp.float32


def _dot_nt(a, b):
    return lax.dot_general(a, b, (((1,), (1,)), ((), ())), preferred_element_type=F32)


def _dot(a, b):
    return jnp.dot(a, b, preferred_element_type=F32)


def _low_half(shape):
    return lax.broadcasted_iota(jnp.int32, shape, 1) < HEAD_DIM


def _softmax_pv(units):
    n = range(len(units))
    m = [functools.reduce(jnp.maximum, [s.max(axis=-1, keepdims=True) for s in units[u][0]]) for u in n]
    m = [m[u] if units[u][2] is None else jnp.maximum(m[u], units[u][2]) for u in n]
    p = [[jnp.exp(s - m[u]) for s in units[u][0]] for u in n]
    l = [functools.reduce(lambda a, b: a + b, [x.sum(axis=-1, keepdims=True) for x in p[u]]) for u in n]
    l = [l[u] if units[u][2] is None else l[u] + jnp.exp(units[u][2] - m[u]) for u in n]
    o = [functools.reduce(lambda a, b: a + b, [_dot(x.astype(BF16), v) for x, v in zip(p[u], units[u][1])]) for u in n]
    return [o[u] / l[u] for u in n]


def _place_head(q_slab, src_half, dst_half, low):
    x = q_slab if src_half == dst_half else pltpu.roll(q_slab, HEAD_DIM, axis=1)
    return jnp.where(low if dst_half == 0 else ~low, x, jnp.zeros_like(x))


NA_KEYS = NA_KH_MAX * GRID_W


def _na_kernel(q_ref, k_ref, v_ref, ck_ref, cv_ref, tab_ref, o_ref):
    r = pl.program_id(1)
    rows = k_ref.shape[1] // GRID_W
    rs = jnp.clip(r - NA_KH_MAX // 2, 0, rows - NA_KH_MAX)
    start = pl.multiple_of(rs * GRID_W, GRID_W)
    scale = HEAD_DIM ** -0.5
    low = _low_half((GRID_W, LANES))
    pairs = range(D_HEADS // 2)
    cols = [slice(p * LANES, (p + 1) * LANES) for p in pairs]
    qp = [q_ref[0, :, c] for c in cols]
    kp = [k_ref[0, pl.ds(start, NA_KEYS), c] for c in cols]
    vp = [v_ref[0, pl.ds(start, NA_KEYS), c] for c in cols]
    ckp = [ck_ref[0, :, c] for c in cols]
    cvp = [cv_ref[0, :, c] for c in cols]
    heads = [(p, half) for p in pairs for half in range(2)]
    qm = [jnp.where(low if half == 0 else ~low, qp[p], jnp.zeros_like(qp[p])) for p, half in heads]
    s_loc = [_dot_nt(qm[i], kp[p]) * scale + tab_ref[i, 0] for i, (p, _) in enumerate(heads)]
    s_ctx = [_dot_nt(qm[i], ckp[p]) * scale for i, (p, _) in enumerate(heads)]
    outs = _softmax_pv([([s_loc[i], s_ctx[i]], [vp[p], cvp[p]], None) for i, (p, _) in enumerate(heads)])
    for p in pairs:
        o_ref[0, :, cols[p]] = jnp.where(low, outs[2 * p], outs[2 * p + 1]).astype(o_ref.dtype)


def na_bias_table(rpb):
    col = jnp.arange(GRID_W)
    cs = jnp.clip(col - NA_KW // 2, 0, GRID_W - NA_KW)
    col_ok = (col[None, :] >= cs[:, None]) & (col[None, :] < cs[:, None] + NA_KW)
    coff = jnp.clip(col[None, :] - col[:, None] + NA_KW - 1, 0, 2 * NA_KW - 2)
    base = jnp.where(col_ok[None, None], rpb.astype(F32)[:, :, coff], NEG_INF)
    tab = jnp.stack([base[:, o:o + NA_KH_MAX] for o in range(NA_KH_MAX)], axis=1)
    return tab.transpose(0, 1, 3, 2, 4).reshape(D_HEADS, NA_KH_MAX, GRID_W, NA_KEYS)


def neighbourhood_attention_pallas(q, k, v, ck, cv, rpb):
    B, T, W = q.shape
    P = ck.shape[1]
    rows = T // GRID_W
    assert rows >= NA_KH_MAX and W == D_W
    tab = na_bias_table(rpb)

    def tab_index(b, r):
        rs = jnp.clip(r - NA_KH_MAX // 2, 0, rows - NA_KH_MAX)
        return (0, rs - r + NA_KH_MAX - 1, 0, 0)

    return pl.pallas_call(
        _na_kernel,
        grid=(B, rows),
        in_specs=[
            pl.BlockSpec((1, GRID_W, W), lambda b, r: (b, r, 0)),
            pl.BlockSpec((1, T, W), lambda b, r: (b, 0, 0)),
            pl.BlockSpec((1, T, W), lambda b, r: (b, 0, 0)),
            pl.BlockSpec((1, P, W), lambda b, r: (b, 0, 0)),
            pl.BlockSpec((1, P, W), lambda b, r: (b, 0, 0)),
            pl.BlockSpec((D_HEADS, 1, GRID_W, NA_KEYS), tab_index),
        ],
        out_specs=pl.BlockSpec((1, GRID_W, W), lambda b, r: (b, r, 0)),
        out_shape=jax.ShapeDtypeStruct((B, T, W), BF16),
        compiler_params=pltpu.CompilerParams(dimension_semantics=("arbitrary", "arbitrary"),
                                             vmem_limit_bytes=VMEM_LIMIT_BYTES),
        name="na_attention",
    )(q, k, v, ck, cv, tab)


def rope_tables(T):
    cos, sin = axial_rope(T)
    cos, sin = cos[:, 0, :], sin[:, 0, :]
    cos_t = jnp.concatenate([cos, cos, cos, cos], axis=-1)
    sin_t = jnp.concatenate([-sin, sin, -sin, sin], axis=-1)
    return cos_t, sin_t


def _rope(x, cos_t, sin_t):
    half = HEAD_DIM // 2
    lane = lax.broadcasted_iota(jnp.int32, x.shape, 1)
    first = (lane % HEAD_DIM) < half
    swapped = jnp.where(first, pltpu.roll(x, LANES - half, axis=1), pltpu.roll(x, half, axis=1))
    return x * cos_t + swapped * sin_t


def _win_kernel(sink_ref, q_ref, k_ref, v_ref, ck_ref, cv_ref, cos_ref, sin_ref, o_ref):
    i = pl.program_id(1)
    T = k_ref.shape[1]
    span = 3 * A_BLOCK
    start = pl.multiple_of(jnp.clip((i - 1) * A_BLOCK, 0, T - span), A_BLOCK)
    delta = i * A_BLOCK - start
    q0 = pl.multiple_of(i * A_BLOCK, A_BLOCK)
    scale = HEAD_DIM ** -0.5
    kw = _rope(k_ref[0, pl.ds(start, span), :], cos_ref[pl.ds(start, span), :], sin_ref[pl.ds(start, span), :]).astype(BF16)
    vw = v_ref[0, pl.ds(start, span), :].astype(BF16)
    ck = ck_ref[0]
    cv = cv_ref[0]
    cos_q = cos_ref[pl.ds(q0, A_BLOCK), :]
    sin_q = sin_ref[pl.ds(q0, A_BLOCK), :]
    qi = lax.broadcasted_iota(jnp.int32, (A_BLOCK, span), 0)
    kj = lax.broadcasted_iota(jnp.int32, (A_BLOCK, span), 1)
    band = jnp.abs(kj - delta - qi) <= A_WINDOW
    low = _low_half((A_BLOCK, LANES))
    group = A_HEADS // A_KV_HEADS
    pairs = range(A_HEADS // 2)
    cols = [slice(p * LANES, (p + 1) * LANES) for p in pairs]
    q_slab = [_rope(q_ref[0, :, c], cos_q, sin_q) for c in cols]
    heads = range(A_HEADS)
    kv_of = [h // group for h in heads]
    qm = [_place_head(q_slab[h // 2], h % 2, kv_of[h], low).astype(BF16) for h in heads]
    s_loc = [jnp.where(band, _dot_nt(qm[h], kw) * scale, NEG_INF) for h in heads]
    s_ctx = [_dot_nt(qm[h], ck) * scale for h in heads]
    outs = _softmax_pv([([s_loc[h], s_ctx[h]], [vw, cv], sink_ref[h]) for h in heads])
    outs = [outs[h] if kv_of[h] == h % 2 else pltpu.roll(outs[h], HEAD_DIM, axis=1) for h in heads]
    for p in pairs:
        o_ref[0, :, cols[p]] = jnp.where(low, outs[2 * p], outs[2 * p + 1]).astype(o_ref.dtype)


def window_attention_pallas(q, k, v, ck, cv, sink):
    B, T, QW = q.shape
    KW = k.shape[2]
    P = ck.shape[1]
    assert KW == LANES and QW == A_Q_W and T % A_BLOCK == 0 and T >= 3 * A_BLOCK
    cos_t, sin_t = rope_tables(T)
    return pl.pallas_call(
        _win_kernel,
        grid=(B, T // A_BLOCK),
        in_specs=[
            pl.BlockSpec(memory_space=pltpu.SMEM),
            pl.BlockSpec((1, A_BLOCK, QW), lambda b, i: (b, i, 0)),
            pl.BlockSpec((1, T, KW), lambda b, i: (b, 0, 0)),
            pl.BlockSpec((1, T, KW), lambda b, i: (b, 0, 0)),
            pl.BlockSpec((1, P, KW), lambda b, i: (b, 0, 0)),
            pl.BlockSpec((1, P, KW), lambda b, i: (b, 0, 0)),
            pl.BlockSpec((T, LANES), lambda b, i: (0, 0)),
            pl.BlockSpec((T, LANES), lambda b, i: (0, 0)),
        ],
        out_specs=pl.BlockSpec((1, A_BLOCK, QW), lambda b, i: (b, i, 0)),
        out_shape=jax.ShapeDtypeStruct((B, T, QW), BF16),
        compiler_params=pltpu.CompilerParams(dimension_semantics=("arbitrary", "arbitrary"),
                                             vmem_limit_bytes=VMEM_LIMIT_BYTES),
        name="window_attention",
    )(sink.astype(F32), q, k, v, ck, cv, cos_t, sin_t)


def _ctx_kernel(sink_ref, q_ref, k_ref, v_ref, o_ref, *, n_q_heads, n_kv_heads, use_sink):
    S = q_ref.shape[1]
    scale = HEAD_DIM ** -0.5
    low = _low_half((S, LANES))
    group = n_q_heads // n_kv_heads
    pairs = range(n_q_heads // 2)
    cols = [slice(p * LANES, (p + 1) * LANES) for p in pairs]
    q_slab = [q_ref[0, :, c] for c in cols]
    heads = range(n_q_heads)
    kv_of = [h // group for h in heads]
    kcols = [slice((kv // 2) * LANES, (kv // 2 + 1) * LANES) for kv in kv_of]
    qm = [_place_head(q_slab[h // 2], h % 2, kv_of[h] % 2, low).astype(BF16) for h in heads]
    s = [_dot_nt(qm[h], k_ref[0, :, kcols[h]].astype(BF16)) * scale for h in heads]
    outs = _softmax_pv([([s[h]], [v_ref[0, :, kcols[h]].astype(BF16)], sink_ref[h] if use_sink else None) for h in heads])
    outs = [outs[h] if kv_of[h] % 2 == h % 2 else pltpu.roll(outs[h], HEAD_DIM, axis=1) for h in heads]
    for p in pairs:
        o_ref[0, :, cols[p]] = jnp.where(low, outs[2 * p], outs[2 * p + 1]).astype(o_ref.dtype)


def context_attention_pallas(q, k, v, sink, n_q_heads, n_kv_heads):
    B, S, QW = q.shape
    KW = k.shape[2]
    use_sink = sink is not None
    sink_arr = sink.astype(F32) if use_sink else jnp.zeros((n_q_heads,), F32)
    return pl.pallas_call(
        functools.partial(_ctx_kernel, n_q_heads=n_q_heads, n_kv_heads=n_kv_heads, use_sink=use_sink),
        grid=(B,),
        in_specs=[
            pl.BlockSpec(memory_space=pltpu.SMEM),
            pl.BlockSpec((1, S, QW), lambda b: (b, 0, 0)),
            pl.BlockSpec((1, S, KW), lambda b: (b, 0, 0)),
            pl.BlockSpec((1, S, KW), lambda b: (b, 0, 0)),
        ],
        out_specs=pl.BlockSpec((1, S, QW), lambda b: (b, 0, 0)),
        out_shape=jax.ShapeDtypeStruct((B, S, QW), BF16),
        compiler_params=pltpu.CompilerParams(dimension_semantics=("arbitrary",), vmem_limit_bytes=VMEM_LIMIT_BYTES),
        name="context_attention",
    )(sink_arr, q, k, v)


HIGHEST = lax.Precision.HIGHEST
GDN_TM = 256
GDN_REQS = 2
SUBLANES = 8


def _head_pair_sum_matrix():
    a = lax.broadcasted_iota(jnp.int32, (LANES, LANES), 0) // HEAD_DIM
    b = lax.broadcasted_iota(jnp.int32, (LANES, LANES), 1) // HEAD_DIM
    return (a == b).astype(BF16)


def _head_sums(x, pmat):
    hi = x.astype(BF16)
    lo = (x - hi.astype(F32)).astype(BF16)
    return _dot(hi, pmat) + _dot(lo, pmat)


def _gdn_prep_kernel(x_ref, prev_ref, next_ref, ab_ref, cw_ref, a_ref, dtb_ref, q_ref, k_ref, v_ref, gb_ref):
    i = pl.program_id(1)
    n = pl.num_programs(1)
    x = x_ref[0]
    tm = x.shape[0]
    row = lax.broadcasted_iota(jnp.int32, x.shape, 0)
    prev_row = jnp.where(i > 0, prev_ref[0, SUBLANES - 1:SUBLANES, :], 0.0)
    next_row = jnp.where(i < n - 1, next_ref[0, 0:1, :], 0.0)
    x_prev = jnp.where(row == 0, prev_row, pltpu.roll(x, 1, axis=0))
    x_next = jnp.where(row == tm - 1, next_row, pltpu.roll(x, tm - 1, axis=0))
    y = x_prev * cw_ref[0:1, :] + x * cw_ref[1:2, :] + x_next * cw_ref[2:3, :]
    y = y * jax.nn.sigmoid(y)
    pmat = _head_pair_sum_matrix()
    for p in range(C_W // LANES):
        qs = y[:, p * LANES:(p + 1) * LANES]
        ks = y[:, C_W + p * LANES:C_W + (p + 1) * LANES]
        q_ref[0, :, p * LANES:(p + 1) * LANES] = qs * lax.rsqrt(_head_sums(qs * qs, pmat) + EPS) * (HEAD_DIM ** -0.5)
        k_ref[0, :, p * LANES:(p + 1) * LANES] = ks * lax.rsqrt(_head_sums(ks * ks, pmat) + EPS)
    v_ref[0] = y[:, 2 * C_W:]
    ab = ab_ref[0, :, 0:4 * C_HEADS]
    lane = lax.broadcasted_iota(jnp.int32, ab.shape, 1)
    is_beta = (lane // C_HEADS) % 2 == 1
    t = ab + dtb_ref[...]
    softplus = jnp.maximum(t, 0.0) + jnp.log1p(jnp.exp(-jnp.abs(t)))
    gb = jnp.where(is_beta, jax.nn.sigmoid(ab), -jnp.exp(a_ref[...]) * softplus)
    gb_ref[0, 0] = gb[:, 0:2 * C_HEADS]
    gb_ref[0, 1] = gb[:, 2 * C_HEADS:4 * C_HEADS]


def gdn_prep(zqkv, zab, conv_w, a_log, dt_bias):
    B, L, W3 = zqkv.shape
    tm = GDN_TM
    assert L % tm == 0
    nb = tm // SUBLANES
    zero = jnp.zeros((C_HEADS,), F32)
    a_lane = jnp.concatenate([a_log[0], zero, a_log[1], zero]).astype(F32)[None, :]
    dtb_lane = jnp.concatenate([dt_bias[0], zero, dt_bias[1], zero]).astype(F32)[None, :]
    outs = pl.pallas_call(
        _gdn_prep_kernel,
        grid=(B, L // tm),
        in_specs=[
            pl.BlockSpec((1, tm, W3), lambda b, i: (b, i, 0)),
            pl.BlockSpec((1, SUBLANES, W3), lambda b, i: (b, jnp.maximum(i * nb - 1, 0), 0)),
            pl.BlockSpec((1, SUBLANES, W3), lambda b, i: (b, jnp.minimum((i + 1) * nb, L // SUBLANES - 1), 0)),
            pl.BlockSpec((1, tm, LANES), lambda b, i: (b, i, 0)),
            pl.BlockSpec((C_SHORT, W3), lambda b, i: (0, 0)),
            pl.BlockSpec((1, 4 * C_HEADS), lambda b, i: (0, 0)),
            pl.BlockSpec((1, 4 * C_HEADS), lambda b, i: (0, 0)),
        ],
        out_specs=[
            pl.BlockSpec((1, tm, C_W), lambda b, i: (b, i, 0)),
            pl.BlockSpec((1, tm, C_W), lambda b, i: (b, i, 0)),
            pl.BlockSpec((1, tm, C_W), lambda b, i: (b, i, 0)),
            pl.BlockSpec((1, 2, tm, 2 * C_HEADS), lambda b, i: (b, 0, i, 0)),
        ],
        out_shape=[jax.ShapeDtypeStruct((B, L, C_W), F32)] * 3 + [jax.ShapeDtypeStruct((B, 2, L, 2 * C_HEADS), F32)],
        compiler_params=pltpu.CompilerParams(dimension_semantics=("arbitrary", "arbitrary"),
                                             vmem_limit_bytes=VMEM_LIMIT_BYTES),
        name="gdn_prep",
    )(zqkv, zqkv, zqkv, zab, conv_w.astype(F32), a_lane, dtb_lane)
    return outs


def _gdn_kernel(q_ref, k_ref, v_ref, gb_ref, zg_ref, s0_ref, nw_ref, o_ref, st_ref, s_scr, of_scr):
    d = pl.program_id(1)
    c = pl.program_id(2)
    n = pl.num_programs(2)
    C = C_CHUNK
    fwd = d == 0
    chunk = jnp.where(fwd, c, n - 1 - c)
    r0 = pl.multiple_of(chunk * C, C)

    @pl.when(c == 0)
    def _():
        s_scr[...] = s0_ref[:, 0]

    row = lax.broadcasted_iota(jnp.int32, (C, C), 0)
    col = lax.broadcasted_iota(jnp.int32, (C, C), 1)
    ahead = jnp.where(fwd, row - col, col - row)
    incl = ahead >= 0
    strict = ahead > 0
    incl_f = incl.astype(F32)
    nreq = q_ref.shape[0]
    gc, gc_t, g_last, beta = [], [], [], []
    for r in range(nreq):
        g = gb_ref[r, 0, :, 0:C_HEADS]
        beta.append(gb_ref[r, 0, :, C_HEADS:2 * C_HEADS])
        gc.append(jnp.dot(incl_f, g, precision=HIGHEST, preferred_element_type=F32))
        gc_t.append(gc[r].T)
        g_last.append(jnp.where(fwd, gc[r][C - 1:C, :], gc[r][0:1, :]))
    units = [(r, h) for r in range(nreq) for h in range(C_HEADS)]
    H = range(len(units))
    heads = [slice(h * HEAD_DIM, (h + 1) * HEAD_DIM) for _, h in units]
    s_old = [s_scr[r, h] for r, h in units]
    qs = [q_ref[r, :, heads[i]] for i, (r, _) in enumerate(units)]
    ks = [k_ref[r, :, heads[i]] for i, (r, _) in enumerate(units)]
    vs = [v_ref[r, :, heads[i]] for i, (r, _) in enumerate(units)]
    gcol = [gc[r][:, h:h + 1] for r, h in units]
    bcol = [beta[r][:, h:h + 1] for r, h in units]
    gl = [g_last[r][:, h:h + 1] for r, h in units]
    gamma = [jnp.exp(jnp.where(incl, gcol[i] - gc_t[r][h:h + 1, :], NEG_INF)) for i, (r, h) in enumerate(units)]
    egc = [jnp.exp(gcol[h]) for h in H]
    kb = [ks[h].astype(BF16) for h in H]
    nmat = [jnp.where(strict, _dot_nt(kb[h], kb[h]) * gamma[h], 0.0) * bcol[h] for h in H]
    attn = [(_dot_nt(qs[h].astype(BF16), kb[h]) * gamma[h]).astype(BF16) for h in H]
    xr = row ^ col
    eye = (row == col).astype(F32)
    tinv = [eye - jnp.where((xr >> 1) == 0, nmat[h], 0.0) for h in H]
    for lvl in range(1, 6):
        off_diag = (xr >> lvl) == 1
        wmat = [_dot(jnp.where(off_diag, nmat[h], 0.0).astype(BF16), tinv[h].astype(BF16)).astype(BF16) for h in H]
        tinv = [tinv[h] - _dot(tinv[h].astype(BF16), wmat[h]) for h in H]
    x = [jnp.concatenate([vs[h] * bcol[h], ks[h] * (bcol[h] * egc[h])], axis=1) for h in H]
    x = [x[h] + _dot((tinv[h] - eye).astype(BF16), x[h].astype(BF16)) for h in H]
    sb = [s_old[h].astype(BF16) for h in H]
    v_new = [x[h][:, :HEAD_DIM] - _dot(x[h][:, HEAD_DIM:].astype(BF16), sb[h]) for h in H]
    vb = [v_new[h].astype(BF16) for h in H]
    outs = [_dot((qs[h] * egc[h]).astype(BF16), sb[h]) + _dot(attn[h], vb[h]) for h in H]
    kd = [(ks[h] * jnp.exp(gl[h] - gcol[h])).astype(BF16) for h in H]
    s_new = [s_old[h] * jnp.exp(gl[h]) + lax.dot_general(kd[h], vb[h], (((0,), (0,)), ((), ())), preferred_element_type=F32)
             for h in H]
    for i, (r, h) in enumerate(units):
        s_scr[r, h] = s_new[i]
    o = [jnp.concatenate(outs[r * C_HEADS:(r + 1) * C_HEADS], axis=1) for r in range(nreq)]

    @pl.when(fwd)
    def _():
        for r in range(nreq):
            of_scr[r, pl.ds(r0, C), :] = o[r]

    @pl.when(jnp.logical_not(fwd))
    def _():
        pmat = _head_pair_sum_matrix()
        for r in range(nreq):
            tot = of_scr[r, pl.ds(r0, C), :] + o[r]
            zg = zg_ref[r]
            gate = zg * jax.nn.sigmoid(zg)
            for p in range(C_W // LANES):
                cols = slice(p * LANES, (p + 1) * LANES)
                t = tot[:, cols]
                ms = _head_sums(t * t, pmat) * (1.0 / HEAD_DIM)
                o_ref[r, :, cols] = (t * lax.rsqrt(ms + EPS) * nw_ref[:, cols] * gate[:, cols]).astype(o_ref.dtype)

    @pl.when(c == n - 1)
    def _():
        st_ref[:, 0] = s_scr[...]


def gdn_scan(q, k, v, gb, zg, s0, norm_w):
    B, L, W = q.shape
    C = C_CHUNK
    n = L // C
    R = min(B, GDN_REQS)
    assert L % C == 0 and W == C_W and B % R == 0
    chunk_of = lambda d, c: jnp.where(d == 0, c, n - 1 - c)
    seq_spec = pl.BlockSpec((R, C, W), lambda b, d, c: (b, chunk_of(d, c), 0))
    state_spec = pl.BlockSpec((R, 1, C_HEADS, HEAD_DIM, HEAD_DIM), lambda b, d, c: (b, d, 0, 0, 0))
    nw = jnp.tile(norm_w.astype(F32), C_HEADS)[None, :]
    return pl.pallas_call(
        _gdn_kernel,
        grid=(B // R, 2, n),
        in_specs=[
            seq_spec, seq_spec, seq_spec,
            pl.BlockSpec((R, 1, C, 2 * C_HEADS), lambda b, d, c: (b, d, chunk_of(d, c), 0)),
            seq_spec,
            state_spec,
            pl.BlockSpec((1, W), lambda b, d, c: (0, 0)),
        ],
        out_specs=[
            pl.BlockSpec((R, C, W), lambda b, d, c: (b, jnp.where(d == 0, n - 1, n - 1 - c), 0)),
            state_spec,
        ],
        out_shape=[jax.ShapeDtypeStruct((B, L, W), BF16), jax.ShapeDtypeStruct(s0.shape, F32)],
        scratch_shapes=[pltpu.VMEM((R, C_HEADS, HEAD_DIM, HEAD_DIM), F32), pltpu.VMEM((R, L, W), F32)],
        compiler_params=pltpu.CompilerParams(dimension_semantics=("arbitrary", "arbitrary", "arbitrary"),
                                             vmem_limit_bytes=VMEM_LIMIT_BYTES),
        name="gdn_scan",
    )(q, k, v, gb, zg, s0.astype(F32), nw)


MOE_TT = 512
MOE_RT = 512
MOE_SELECT_REQS = 4
MOE_SELECT_REQS_LONG = 2


def _moe_router_kernel(x_ref, g_ref, shift_ref, scale_ref, wr_ref, h_ref, aff_ref):
    x = x_ref[0]
    y = x * lax.rsqrt(jnp.mean(x * x, axis=-1, keepdims=True) + EPS) * g_ref[...]
    h = y * (1.0 + scale_ref[0]) + shift_ref[0]
    h_ref[0] = h.astype(BF16)
    logits = jnp.dot(h, wr_ref[...], precision=HIGHEST, preferred_element_type=F32)
    lane = lax.broadcasted_iota(jnp.int32, logits.shape, 1)
    logits = jnp.where(lane < N_EXPERTS, logits, NEG_INF)
    e = jnp.exp(logits - logits.max(axis=-1, keepdims=True))
    aff_ref[0] = e / e.sum(axis=-1, keepdims=True)


def _moe_select_kernel(aff_ref, slot_ref, start_ref, *, cap):
    nreq, T = aff_ref.shape[0], aff_ref.shape[1]
    reqs = range(nreq)
    bits = [pltpu.bitcast(aff_ref[q], jnp.int32) for q in reqs]

    def bisect(i, v):
        cand = [v[q] | (1 << (30 - i)) for q in reqs]
        cnt = [jnp.sum((bits[q] >= cand[q]).astype(jnp.int32), axis=0, keepdims=True) for q in reqs]
        return tuple(jnp.where(cnt[q] >= cap, cand[q], v[q]) for q in reqs)

    thr = lax.fori_loop(0, 31, bisect, tuple(jnp.zeros((1, LANES), jnp.int32) for _ in reqs))
    blk = min(T, MOE_TT)
    r = lax.broadcasted_iota(jnp.int32, (blk, blk), 0)
    c = lax.broadcasted_iota(jnp.int32, (blk, blk), 1)
    before = (c < r).astype(BF16)
    n_tiles = T // blk
    for q in reqs:
        gt = (bits[q] > thr[q]).astype(F32)
        eq = (bits[q] == thr[q]).astype(F32)
        need = cap - jnp.sum(gt, axis=0, keepdims=True)
        carry_gt = jnp.zeros((1, LANES), F32)
        carry_eq = jnp.zeros((1, LANES), F32)
        start_ref[q] = jnp.zeros(start_ref.shape[1:], F32)
        for b in range(n_tiles):
            rows = slice(b * blk, (b + 1) * blk)
            gt_b, eq_b = gt[rows], eq[rows]
            pos_gt = _dot(before, gt_b.astype(BF16)) + carry_gt
            pos_eq = _dot(before, eq_b.astype(BF16)) + carry_eq
            chosen = gt_b + eq_b * (pos_eq < need).astype(F32)
            slot_ref[q, rows, :] = jnp.where(chosen > 0.5, pos_gt + jnp.minimum(pos_eq, need), -1.0)
            carry_gt = carry_gt + jnp.sum(gt_b, axis=0, keepdims=True)
            carry_eq = carry_eq + jnp.sum(eq_b, axis=0, keepdims=True)
            start_ref[q, b + 1:b + 2, :] = carry_gt + jnp.minimum(carry_eq, need)


def _expert_column(a, e):
    lane = lax.broadcasted_iota(jnp.int32, a.shape, 1)
    return jnp.sum(jnp.where(lane == e, a, 0.0), axis=1, keepdims=True)


MOE_START_ROWS = 16
MOE_WIN = 128


def _one_hot_slots(slot_col, first, width):
    s = lax.broadcasted_iota(jnp.int32, (slot_col.shape[0], width), 1).astype(F32)
    return (slot_col - first == s).astype(BF16)


MOE_SLOT_ALIGN = 16


def _slot_windows(start_ref, b, e, k):
    base = (b * N_EXPERTS + e) * MOE_START_ROWS + k
    lo, hi = start_ref[base], start_ref[base + 1]
    first = (lo // MOE_SLOT_ALIGN) * MOE_SLOT_ALIGN
    return first, jnp.where(hi > lo, (hi - first + MOE_WIN - 1) // MOE_WIN, 0)


def _window_one_hot(col, first, w, cap):
    nominal = first + w * MOE_WIN
    s0 = pl.multiple_of(jnp.minimum(nominal, cap - MOE_WIN), MOE_SLOT_ALIGN)
    rel = jnp.where(col >= nominal.astype(F32), col - s0.astype(F32), -1.0)
    return s0, _one_hot_slots(rel, 0.0, MOE_WIN)


def _moe_gather_kernel(start_ref, slot_ref, h_ref, xg_ref, acc_ref):
    b, e = pl.program_id(0), pl.program_id(1)
    acc_ref[...] = jnp.zeros_like(acc_ref)
    for k in range(slot_ref.shape[1] // MOE_TT):
        rows = slice(k * MOE_TT, (k + 1) * MOE_TT)
        col = _expert_column(slot_ref[0, rows, :], e)
        first, count = _slot_windows(start_ref, b, e, k)

        def window(w, carry, col=col, rows=rows, first=first):
            s0, pt = _window_one_hot(col, first, w, acc_ref.shape[0])
            acc_ref[pl.ds(s0, MOE_WIN), :] += lax.dot_general(pt, h_ref[0, rows, :], (((0,), (0,)), ((), ())),
                                                              preferred_element_type=F32)
            return carry

        lax.fori_loop(0, count, window, 0)
    xg_ref[0] = acc_ref[...].astype(xg_ref.dtype)


def _moe_gather_short_kernel(slot_ref, h_ref, xg_ref, *, cap):
    T = slot_ref.shape[1]
    width = N_EXPERTS * cap
    spread = (lax.broadcasted_iota(jnp.int32, (LANES, width), 1) // cap
              == lax.broadcasted_iota(jnp.int32, (LANES, width), 0)).astype(BF16)
    slot_wide = _dot(slot_ref[0].astype(BF16), spread)
    lane_slot = (lax.broadcasted_iota(jnp.int32, (T, width), 1) % cap).astype(F32)
    pt = (slot_wide == lane_slot).astype(BF16)
    rows = lax.dot_general(pt, h_ref[0], (((0,), (0,)), ((), ())), preferred_element_type=F32)
    for e in range(N_EXPERTS):
        xg_ref[e] = rows[e * cap:(e + 1) * cap].astype(xg_ref.dtype)


def _moe_ffn_kernel(x_ref, wg_ref, wu_ref, wd_ref, y_ref, wg_s, wu_s, wd_s):
    @pl.when(pl.program_id(1) == 0)
    def _():
        wg_s[...] = wg_ref[0].astype(BF16)
        wu_s[...] = wu_ref[0].astype(BF16)
        wd_s[...] = wd_ref[0].astype(BF16)

    x = x_ref[0]
    a = _dot(x, wg_s[...])
    u = _dot(x, wu_s[...])
    hid = (a * jax.nn.sigmoid(a) * u).astype(BF16)
    y_ref[0] = _dot(hid, wd_s[...]).astype(y_ref.dtype)


def _moe_scatter_kernel(start_ref, slot_ref, aff_ref, y_ref, x_ref, gate_ref, o_ref, acc_ref):
    b, k = pl.program_id(0), pl.program_id(1)
    acc_ref[...] = jnp.zeros_like(acc_ref)
    slot, aff = slot_ref[0], aff_ref[0]
    for e in range(N_EXPERTS):
        col, weight = slot[:, e:e + 1], aff[:, e:e + 1]
        first, count = _slot_windows(start_ref, b, e, k)

        def window(w, carry, e=e, col=col, weight=weight, first=first):
            s0, pt = _window_one_hot(col, first, w, y_ref.shape[1])
            acc_ref[...] += weight * _dot(pt, y_ref[e, pl.ds(s0, MOE_WIN), :])
            return carry

        lax.fori_loop(0, count, window, 0)
    o_ref[0] = x_ref[0] + gate_ref[0] * acc_ref[...]


def _moe_scatter_short_kernel(slot_ref, aff_ref, y_ref, x_ref, gate_ref, o_ref, *, cap):
    slot, aff = slot_ref[0], aff_ref[0]
    acc = jnp.zeros(x_ref.shape[1:], F32)
    for e in range(N_EXPERTS):
        pt = _one_hot_slots(slot[:, e:e + 1], 0.0, cap)
        acc = acc + aff[:, e:e + 1] * _dot(pt, y_ref[e])
    o_ref[0] = x_ref[0] + gate_ref[0] * acc


def moe_route(x, g, shift, scale, w_router):
    B, T, D = x.shape
    tt = min(T, MOE_TT)
    cap = EC_CAPACITY * T // N_EXPERTS
    per_request = shift.shape[0] == B
    mod_spec = pl.BlockSpec((1, 1, D), (lambda b, k: (b, 0, 0)) if per_request else (lambda b, k: (0, 0, 0)))
    wr = jnp.concatenate([w_router.astype(F32), jnp.zeros((D, LANES - N_EXPERTS), F32)], axis=1)
    h, aff = pl.pallas_call(
        _moe_router_kernel,
        grid=(B, T // tt),
        in_specs=[pl.BlockSpec((1, tt, D), lambda b, k: (b, k, 0)), pl.BlockSpec((1, D), lambda b, k: (0, 0)),
                  mod_spec, mod_spec, pl.BlockSpec((D, LANES), lambda b, k: (0, 0))],
        out_specs=[pl.BlockSpec((1, tt, D), lambda b, k: (b, k, 0)), pl.BlockSpec((1, tt, LANES), lambda b, k: (b, k, 0))],
        out_shape=[jax.ShapeDtypeStruct((B, T, D), BF16), jax.ShapeDtypeStruct((B, T, LANES), F32)],
        compiler_params=pltpu.CompilerParams(dimension_semantics=("arbitrary", "arbitrary"),
                                             vmem_limit_bytes=VMEM_LIMIT_BYTES),
        name="moe_router",
    )(x, g.astype(F32)[None, :], shift, scale, wr)
    assert T // tt < MOE_START_ROWS
    rq = MOE_SELECT_REQS if T <= MOE_TT else MOE_SELECT_REQS_LONG
    rq = rq if B % rq == 0 else 1
    slot, start = pl.pallas_call(
        functools.partial(_moe_select_kernel, cap=cap),
        grid=(B // rq,),
        in_specs=[pl.BlockSpec((rq, T, LANES), lambda b: (b, 0, 0))],
        out_specs=[pl.BlockSpec((rq, T, LANES), lambda b: (b, 0, 0)),
                   pl.BlockSpec((rq, MOE_START_ROWS, LANES), lambda b: (b, 0, 0))],
        out_shape=[jax.ShapeDtypeStruct((B, T, LANES), F32), jax.ShapeDtypeStruct((B, MOE_START_ROWS, LANES), F32)],
        compiler_params=pltpu.CompilerParams(dimension_semantics=("arbitrary",), vmem_limit_bytes=VMEM_LIMIT_BYTES),
        name="moe_select",
    )(aff)
    start = start[:, :, :N_EXPERTS].astype(jnp.int32).transpose(0, 2, 1).reshape(-1)
    return h, aff, slot, start


def moe_gather(h, slot, start):
    B, T, D = h.shape
    cap = EC_CAPACITY * T // N_EXPERTS
    out_shape = jax.ShapeDtypeStruct((N_EXPERTS, B * cap, D), BF16)
    if T <= MOE_TT:
        return pl.pallas_call(
            functools.partial(_moe_gather_short_kernel, cap=cap),
            grid=(B,),
            in_specs=[pl.BlockSpec((1, T, LANES), lambda b: (b, 0, 0)), pl.BlockSpec((1, T, D), lambda b: (b, 0, 0))],
            out_specs=pl.BlockSpec((N_EXPERTS, cap, D), lambda b: (0, b, 0)),
            out_shape=out_shape,
            compiler_params=pltpu.CompilerParams(dimension_semantics=("arbitrary",), vmem_limit_bytes=VMEM_LIMIT_BYTES),
            name="moe_gather_short",
        )(slot, h)
    assert cap % MOE_WIN == 0 and T % MOE_TT == 0
    return pl.pallas_call(
        _moe_gather_kernel,
        grid_spec=pltpu.PrefetchScalarGridSpec(
            num_scalar_prefetch=1,
            grid=(B, N_EXPERTS),
            in_specs=[pl.BlockSpec((1, T, LANES), lambda b, e, st: (b, 0, 0)),
                      pl.BlockSpec((1, T, D), lambda b, e, st: (b, 0, 0))],
            out_specs=pl.BlockSpec((1, cap, D), lambda b, e, st: (e, b, 0)),
            scratch_shapes=[pltpu.VMEM((cap, D), F32)],
        ),
        out_shape=out_shape,
        compiler_params=pltpu.CompilerParams(dimension_semantics=("arbitrary",) * 2, vmem_limit_bytes=VMEM_LIMIT_BYTES),
        name="moe_gather",
    )(start, slot, h)


def moe_ffn(xg, w_gate, w_up, w_down, layer):
    E, R, D = xg.shape
    rt = min(R, MOE_RT)
    w_spec = pl.BlockSpec((None, 1, D, MOE_D_FF), lambda e, r: (layer, e, 0, 0))
    return pl.pallas_call(
        _moe_ffn_kernel,
        grid=(E, R // rt),
        in_specs=[pl.BlockSpec((1, rt, D), lambda e, r: (e, r, 0)), w_spec, w_spec,
                  pl.BlockSpec((None, 1, MOE_D_FF, D), lambda e, r: (layer, e, 0, 0))],
        out_specs=pl.BlockSpec((1, rt, D), lambda e, r: (e, r, 0)),
        out_shape=jax.ShapeDtypeStruct((E, R, D), BF16),
        scratch_shapes=[pltpu.VMEM((D, MOE_D_FF), BF16), pltpu.VMEM((D, MOE_D_FF), BF16), pltpu.VMEM((MOE_D_FF, D), BF16)],
        compiler_params=pltpu.CompilerParams(dimension_semantics=("arbitrary", "arbitrary"),
                                             vmem_limit_bytes=VMEM_LIMIT_BYTES),
        name="moe_ffn",
    )(xg, w_gate, w_up, w_down)


def moe_scatter(slot, aff, start, y, x, gate):
    B, T, D = x.shape
    cap = EC_CAPACITY * T // N_EXPERTS
    per_request = gate.shape[0] == B
    out_shape = jax.ShapeDtypeStruct((B, T, D), F32)
    if T <= MOE_TT:
        whole = lambda b: (b, 0, 0)
        return pl.pallas_call(
            functools.partial(_moe_scatter_short_kernel, cap=cap),
            grid=(B,),
            in_specs=[pl.BlockSpec((1, T, LANES), whole), pl.BlockSpec((1, T, LANES), whole),
                      pl.BlockSpec((N_EXPERTS, cap, D), lambda b: (0, b, 0)), pl.BlockSpec((1, T, D), whole),
                      pl.BlockSpec((1, 1, D), whole if per_request else (lambda b: (0, 0, 0)))],
            out_specs=pl.BlockSpec((1, T, D), whole),
            out_shape=out_shape,
            compiler_params=pltpu.CompilerParams(dimension_semantics=("arbitrary",), vmem_limit_bytes=VMEM_LIMIT_BYTES),
            name="moe_scatter_short",
        )(slot, aff, y, x, gate)
    tt = MOE_TT
    tile = lambda b, k, st: (b, k, 0)
    return pl.pallas_call(
        _moe_scatter_kernel,
        grid_spec=pltpu.PrefetchScalarGridSpec(
            num_scalar_prefetch=1,
            grid=(B, T // tt),
            in_specs=[pl.BlockSpec((1, tt, LANES), tile), pl.BlockSpec((1, tt, LANES), tile),
                      pl.BlockSpec((N_EXPERTS, cap, D), lambda b, k, st: (0, b, 0), pipeline_mode=pl.Buffered(1)),
                      pl.BlockSpec((1, tt, D), tile),
                      pl.BlockSpec((1, 1, D), (lambda b, k, st: (b, 0, 0)) if per_request else (lambda b, k, st: (0, 0, 0)))],
            out_specs=pl.BlockSpec((1, tt, D), tile),
            scratch_shapes=[pltpu.VMEM((tt, D), F32)],
        ),
        out_shape=out_shape,
        compiler_params=pltpu.CompilerParams(dimension_semantics=("arbitrary",) * 2, vmem_limit_bytes=VMEM_LIMIT_BYTES),
        name="moe_scatter",
    )(start, slot, aff, y, x, gate)


def moe_block(x, g, shift, scale, gate, w_router, w_gate, w_up, w_down, layer):
    h, aff, slot, start = moe_route(x, g, shift, scale, w_router)
    y = moe_ffn(moe_gather(h, slot, start), w_gate, w_up, w_down, layer)
    return moe_scatter(slot, aff, start, y, x, gate)


HY_TM = 512
HY_TK = 1024
HY_BG = 8
HY_BG_LONG = 2


def dft_tables(L):
    blk = min(L, HY_TM)
    t = jnp.arange(L, dtype=jnp.int32)

    def angles(f):
        return ((f[:, None] * t[None, :]) % (2 * L)).astype(F32) * (math.pi / L)

    a_hi = angles(jnp.arange(0, L, blk, dtype=jnp.int32))[:, None, :]
    a_lo = angles(jnp.arange(blk, dtype=jnp.int32))[None, :, :]
    cos_t = (jnp.cos(a_hi) * jnp.cos(a_lo) - jnp.sin(a_hi) * jnp.sin(a_lo)).reshape(L, L)
    sin_t = (jnp.sin(a_hi) * jnp.cos(a_lo) + jnp.cos(a_hi) * jnp.sin(a_lo)).reshape(L, L)
    return cos_t.astype(BF16), sin_t.astype(BF16)


def _alt_sign(rows, first_row):
    t = lax.broadcasted_iota(jnp.int32, (rows, 1), 0) + first_row
    return jnp.where(t % 2 == 0, 1.0, -1.0).astype(F32)


def _hy_prep_kernel(x_ref, prev_ref, next_ref, cw_ref, cb_ref, o_ref, v16_ref):
    i = pl.program_id(1)
    n = pl.num_programs(1)
    x = x_ref[0]
    tm = x.shape[0]
    row = lax.broadcasted_iota(jnp.int32, x.shape, 0)
    prev_row = jnp.where(i > 0, prev_ref[0, SUBLANES - 1:SUBLANES, :], 0.0)
    next_row = jnp.where(i < n - 1, next_ref[0, 0:1, :], 0.0)
    x_prev = jnp.where(row == 0, prev_row, pltpu.roll(x, 1, axis=0))
    x_next = jnp.where(row == tm - 1, next_row, pltpu.roll(x, tm - 1, axis=0))
    y = x_prev * cw_ref[0:1, :] + x * cw_ref[1:2, :] + x_next * cw_ref[2:3, :] + cb_ref[...]
    o_ref[0] = y
    v16_ref[0] = y[:, :HY_CH].astype(BF16)


def hyena_prep(zh, conv_w, conv_b):
    B, L, W = zh.shape
    tm = min(L, GDN_TM)
    nb = tm // SUBLANES
    return pl.pallas_call(
        _hy_prep_kernel,
        grid=(B, L // tm),
        in_specs=[
            pl.BlockSpec((1, tm, W), lambda b, i: (b, i, 0)),
            pl.BlockSpec((1, SUBLANES, W), lambda b, i: (b, jnp.maximum(i * nb - 1, 0), 0)),
            pl.BlockSpec((1, SUBLANES, W), lambda b, i: (b, jnp.minimum((i + 1) * nb, L // SUBLANES - 1), 0)),
            pl.BlockSpec((HY_SHORT, W), lambda b, i: (0, 0)),
            pl.BlockSpec((1, W), lambda b, i: (0, 0)),
        ],
        out_specs=[pl.BlockSpec((1, tm, W), lambda b, i: (b, i, 0)), pl.BlockSpec((1, tm, HY_CH), lambda b, i: (b, i, 0))],
        out_shape=[jax.ShapeDtypeStruct((B, L, W), F32), jax.ShapeDtypeStruct((B, L, HY_CH), BF16)],
        compiler_params=pltpu.CompilerParams(dimension_semantics=("arbitrary", "arbitrary"),
                                             vmem_limit_bytes=VMEM_LIMIT_BYTES),
        name="hyena_prep",
    )(zh, zh, zh, conv_w.astype(F32), conv_b.astype(F32)[None, :])


def _hy_taps_kernel(feat_ref, w1_ref, b1_ref, w2_ref, b2_ref, fr_ref, w3f_ref, w3b_ref, dec_ref, sum_ref, dif_ref):
    L = feat_ref.shape[0]
    fr = fr_ref[...]
    h = jnp.sin(fr * (jnp.dot(feat_ref[...], w1_ref[...], precision=HIGHEST, preferred_element_type=F32) + b1_ref[...]))
    h = jnp.sin(fr * (jnp.dot(h, w2_ref[...], precision=HIGHEST, preferred_element_type=F32) + b2_ref[...]))
    t = lax.broadcasted_iota(jnp.int32, (L, 1), 0)
    window = jnp.exp(-(t.astype(F32) / (L - 1)) * dec_ref[...]) + HY_SHIFT
    fwd = jnp.dot(h, w3f_ref[...], precision=HIGHEST, preferred_element_type=F32) * window
    bwd = jnp.where(t == 0, 0.0, jnp.dot(h, w3b_ref[...], precision=HIGHEST, preferred_element_type=F32) * window)
    inv = 1.0 / (jnp.sum(jnp.abs(fwd), axis=0, keepdims=True) + jnp.sum(jnp.abs(bwd), axis=0, keepdims=True))
    sum_ref[...] = (fwd + bwd) * inv
    dif_ref[...] = (bwd - fwd) * inv


def hyena_taps(L, w1, b1, w2, b2, w3, freq):
    f32 = F32
    t = jnp.linspace(0.0, 1.0, L, dtype=f32)[:, None]
    bands = (HY_EMB - 1) // 2
    omega = 2.0 * math.pi * jnp.arange(L, dtype=f32)[:, None] / L
    fb = jnp.linspace(1e-4, bands - 1, bands, dtype=f32)[None, :]
    feats = jnp.concatenate([t, jnp.cos(fb * omega), -jnp.sin(fb * omega)], axis=-1)
    max_decay = math.log(HY_DECAY_TARGET) / HY_FAST_DECAY
    min_decay = math.log(HY_DECAY_TARGET) / HY_SLOW_DECAY
    deltas = jnp.abs(jnp.linspace(min_decay, max_decay, HY_CH, dtype=f32))
    dec = jnp.tile(deltas, HY_ORDER)[None, :]
    n_col = HY_ORDER * HY_CH
    cb = 256
    full = lambda shape: pl.BlockSpec(shape, lambda j: (0, 0))
    col = lambda rows: pl.BlockSpec((rows, cb), lambda j: (0, j))
    return pl.pallas_call(
        _hy_taps_kernel,
        grid=(n_col // cb,),
        in_specs=[full((L, HY_EMB)), full((HY_EMB, HY_FILT_W)), full((1, HY_FILT_W)), full((HY_FILT_W, HY_FILT_W)),
                  full((1, HY_FILT_W)), full((1, HY_FILT_W)), col(HY_FILT_W), col(HY_FILT_W), col(1)],
        out_specs=[col(L), col(L)],
        out_shape=[jax.ShapeDtypeStruct((L, n_col), f32)] * 2,
        compiler_params=pltpu.CompilerParams(dimension_semantics=("arbitrary",), vmem_limit_bytes=VMEM_LIMIT_BYTES),
        name="hyena_taps",
    )(feats, w1.astype(f32), b1.astype(f32)[None, :], w2.astype(f32), b2.astype(f32)[None, :], freq.astype(f32)[None, :],
      w3.astype(f32)[:, :n_col], w3.astype(f32)[:, n_col:], dec)


def _hy_spec_kernel(ch_ref, sh_ref, sum_ref, dif_ref, hr_ref, hi_ref, ny_ref, acc_r, acc_i, acc_n):
    m, k = pl.program_id(1), pl.program_id(2)
    tk = sum_ref.shape[0]

    @pl.when(k == 0)
    def _():
        acc_r[...] = jnp.zeros_like(acc_r)
        acc_i[...] = jnp.zeros_like(acc_i)

    @pl.when(jnp.logical_and(k == 0, m == 0))
    def _():
        acc_n[...] = jnp.zeros_like(acc_n)

    a = sum_ref[...]
    acc_r[...] += _dot(ch_ref[...], a.astype(BF16))
    acc_i[...] += _dot(sh_ref[...], dif_ref[...].astype(BF16))

    @pl.when(m == 0)
    def _():
        acc_n[...] += jnp.sum(a * _alt_sign(tk, k * tk), axis=0, keepdims=True)

    @pl.when(k == pl.num_programs(2) - 1)
    def _():
        hr_ref[...] = acc_r[...]
        hi_ref[...] = acc_i[...]
        ny_ref[...] = jnp.broadcast_to(acc_n[...], ny_ref.shape)


def hyena_spectrum(tables, tap_sum, tap_dif):
    L, N = tap_sum.shape
    tm = min(L, HY_TM)
    cb = 512
    tab = pl.BlockSpec((tm, tm), lambda j, m, k: (m, k))
    dat = pl.BlockSpec((tm, cb), lambda j, m, k: (k, j))
    return pl.pallas_call(
        _hy_spec_kernel,
        grid=(N // cb, L // tm, L // tm),
        in_specs=[tab, tab, dat, dat],
        out_specs=[pl.BlockSpec((tm, cb), lambda j, m, k: (m, j)), pl.BlockSpec((tm, cb), lambda j, m, k: (m, j)),
                   pl.BlockSpec((SUBLANES, cb), lambda j, m, k: (0, j))],
        out_shape=[jax.ShapeDtypeStruct((L, N), F32), jax.ShapeDtypeStruct((L, N), F32),
                   jax.ShapeDtypeStruct((SUBLANES, N), F32)],
        scratch_shapes=[pltpu.VMEM((tm, cb), F32), pltpu.VMEM((tm, cb), F32), pltpu.VMEM((1, cb), F32)],
        compiler_params=pltpu.CompilerParams(dimension_semantics=("arbitrary",) * 3, vmem_limit_bytes=VMEM_LIMIT_BYTES),
        name="hyena_spectrum",
    )(*tables, tap_sum, tap_dif)


def _hy_fwd_kernel(ch_ref, sh_ref, u_ref, hr_ref, hi_ref, hny_ref, yr_ref, yi_ref, yny_ref,
                   acc_c, acc_s, acc_n):
    m, k = pl.program_id(1), pl.program_id(2)
    nb, tk = u_ref.shape[0], u_ref.shape[1]
    tm = acc_c.shape[1]

    @pl.when(k == 0)
    def _():
        acc_c[...] = jnp.zeros_like(acc_c)
        acc_s[...] = jnp.zeros_like(acc_s)

    @pl.when(jnp.logical_and(k == 0, m == 0))
    def _():
        acc_n[...] = jnp.zeros_like(acc_n)

    sign = _alt_sign(tk, k * tk)
    for b in range(nb):
        ub = u_ref[b]
        acc_c[b] += _dot(ch_ref[...], ub)
        acc_s[b] += _dot(sh_ref[...], ub)

        @pl.when(m == 0)
        def _():
            acc_n[b] += jnp.sum(ub.astype(F32) * sign, axis=0, keepdims=True)

    @pl.when(k == pl.num_programs(2) - 1)
    def _():
        f = lax.broadcasted_iota(jnp.int32, (tm, 1), 0) + m * tm
        dc = jnp.where(f == 0, 0.5, 1.0).astype(F32)
        hr, hi = hr_ref[...], hi_ref[...]
        for b in range(nb):
            xr, xs = acc_c[b], acc_s[b]
            yr_ref[b] = ((xr * hr + xs * hi) * dc).astype(yr_ref.dtype)
            yi_ref[b] = (xr * hi - xs * hr).astype(yi_ref.dtype)
            yny_ref[b] = jnp.broadcast_to(acc_n[b] * hny_ref[0:1, :], yny_ref.shape[1:])


def _hy_inv_kernel(ch_ref, sh_ref, yr_ref, yi_ref, yny_ref, u_ref, xg_ref, skip_ref, *rest):
    o_refs, acc = rest[:-1], rest[-1]
    m, k = pl.program_id(1), pl.program_id(2)
    nb = yr_ref.shape[0]
    tm = acc.shape[1]
    L = tm * pl.num_programs(1)

    @pl.when(k == 0)
    def _():
        acc[...] = jnp.zeros_like(acc)

    for b in range(nb):
        acc[b] += _dot(ch_ref[...], yr_ref[b]) - _dot(sh_ref[...], yi_ref[b])

    @pl.when(k == pl.num_programs(2) - 1)
    def _():
        sign = _alt_sign(tm, m * tm)
        for b in range(nb):
            u = u_ref[b]
            y = acc[b] * (1.0 / L) + sign * yny_ref[b, 0:1, :] * (0.5 / L)
            res = xg_ref[b] * (y + u * skip_ref[...])
            for o_ref in o_refs:
                o_ref[b] = res.astype(o_ref.dtype)


def hyena_long_conv(tables, zf, u, u16, u_col, gate_col, hr, hi, hny, order, skip, out_dtypes):
    B, L, _ = u.shape
    C = HY_CH
    tm = min(L, HY_TM)
    tk = min(L, HY_TK)
    bg = min(B, HY_BG_LONG if L >= HY_TM else HY_BG)
    assert B % bg == 0
    grid = (B // bg, L // tm, L // tk)
    tab = pl.BlockSpec((tm, tk), lambda g, m, k: (m, k))
    params = pltpu.CompilerParams(dimension_semantics=("arbitrary",) * 3, vmem_limit_bytes=VMEM_LIMIT_BYTES)
    spec_m = pl.BlockSpec((tm, C), lambda g, m, k: (m, order))
    yr, yi, yny = pl.pallas_call(
        _hy_fwd_kernel,
        grid=grid,
        in_specs=[tab, tab, pl.BlockSpec((bg, tk, C), lambda g, m, k: (g, k, 0)), spec_m, spec_m,
                  pl.BlockSpec((SUBLANES, C), lambda g, m, k: (0, order))],
        out_specs=[pl.BlockSpec((bg, tm, C), lambda g, m, k: (g, m, 0)), pl.BlockSpec((bg, tm, C), lambda g, m, k: (g, m, 0)),
                   pl.BlockSpec((bg, SUBLANES, C), lambda g, m, k: (g, 0, 0))],
        out_shape=[jax.ShapeDtypeStruct((B, L, C), BF16), jax.ShapeDtypeStruct((B, L, C), BF16),
                   jax.ShapeDtypeStruct((B, SUBLANES, C), F32)],
        scratch_shapes=[pltpu.VMEM((bg, tm, C), F32), pltpu.VMEM((bg, tm, C), F32), pltpu.VMEM((bg, 1, C), F32)],
        compiler_params=params,
        name="hyena_fwd",
    )(*tables, u16, hr, hi, hny)
    return pl.pallas_call(
        _hy_inv_kernel,
        grid=grid,
        in_specs=[tab, tab, pl.BlockSpec((bg, tk, C), lambda g, m, k: (g, k, 0)),
                  pl.BlockSpec((bg, tk, C), lambda g, m, k: (g, k, 0)),
                  pl.BlockSpec((bg, SUBLANES, C), lambda g, m, k: (g, 0, 0)),
                  pl.BlockSpec((bg, tm, C), lambda g, m, k: (g, m, u_col)),
                  pl.BlockSpec((bg, tm, C), lambda g, m, k: (g, m, gate_col)),
                  pl.BlockSpec((1, C), lambda g, m, k: (0, 0))],
        out_specs=[pl.BlockSpec((bg, tm, C), lambda g, m, k: (g, m, 0)) for _ in out_dtypes],
        out_shape=[jax.ShapeDtypeStruct((B, L, C), dt) for dt in out_dtypes],
        scratch_shapes=[pltpu.VMEM((bg, tm, C), F32)],
        compiler_params=params,
        name="hyena_inv",
    )(*tables, yr, yi, yny, u, zf, skip.astype(F32)[order][None, :])


def hyena_filter_spectrum(L, w1, b1, w2, b2, w3, freq):
    tables = dft_tables(L)
    tap_sum, tap_dif = hyena_taps(L, w1, b1, w2, b2, w3, freq)
    return tables, hyena_spectrum(tables, tap_sum, tap_dif)


def hyena_mixer_pallas(zh, conv_w, conv_b, skip, tables, spectrum):
    hr, hi, hny = spectrum
    zf, v16 = hyena_prep(zh, conv_w, conv_b)
    y1, y1_16 = hyena_long_conv(tables, zf, zf, v16, 0, 1, hr, hi, hny, 0, skip, [F32, BF16])
    return hyena_long_conv(tables, zf, y1, y1_16, 0, 2, hr, hi, hny, 1, skip, [BF16])[0]


def rms_norm(x, g):
    xf = x.astype(jnp.float32)
    y = xf * lax.rsqrt(jnp.mean(xf * xf, axis=-1, keepdims=True) + EPS)
    return (y * g.astype(jnp.float32)).astype(x.dtype)


def _ada_kernel(c_ref, w_ref, b_ref, o_ref):
    cnd = c_ref[...]
    act = cnd * jax.nn.sigmoid(cnd)
    o_ref[0] = jnp.dot(act, w_ref[0], precision=HIGHEST, preferred_element_type=F32) + b_ref[0]


def ada_params_all(cond, w_ada, b_ada):
    N, D = cond.shape
    depth, _, W = w_ada.shape
    rows = -(-N // SUBLANES) * SUBLANES
    cond_p = jnp.concatenate([cond.astype(F32), jnp.zeros((rows - N, D), F32)], axis=0)
    cb = D
    out = pl.pallas_call(
        _ada_kernel,
        grid=(depth, W // cb),
        in_specs=[pl.BlockSpec((rows, D), lambda l, j: (0, 0)), pl.BlockSpec((1, D, cb), lambda l, j: (l, 0, j)),
                  pl.BlockSpec((1, 1, cb), lambda l, j: (l, 0, j))],
        out_specs=pl.BlockSpec((1, rows, cb), lambda l, j: (l, 0, j)),
        out_shape=jax.ShapeDtypeStruct((depth, rows, W), F32),
        compiler_params=pltpu.CompilerParams(dimension_semantics=("arbitrary", "arbitrary"),
                                             vmem_limit_bytes=VMEM_LIMIT_BYTES),
        name="ada_params",
    )(cond_p, w_ada.astype(F32), b_ada.astype(F32)[:, None, :])
    return out[:, :N]


def _final_norm_kernel(x_ref, g_ref, o_ref):
    x = x_ref[...]
    o_ref[...] = x * lax.rsqrt(jnp.mean(x * x, axis=-1, keepdims=True) + EPS) * g_ref[...]


def final_norm(x, g, tm=512):
    B, L, D = x.shape
    M = B * L
    assert M % tm == 0
    out = pl.pallas_call(
        _final_norm_kernel,
        grid=(M // tm,),
        in_specs=[pl.BlockSpec((tm, D), lambda i: (i, 0)), pl.BlockSpec((1, D), lambda i: (0, 0))],
        out_specs=pl.BlockSpec((tm, D), lambda i: (i, 0)),
        out_shape=jax.ShapeDtypeStruct((M, D), F32),
        compiler_params=pltpu.CompilerParams(dimension_semantics=("arbitrary",), vmem_limit_bytes=VMEM_LIMIT_BYTES),
        name="final_norm",
    )(x.reshape(M, D), g.astype(F32)[None, :])
    return out.reshape(B, L, D)


def ada_params(cond, w, b):
    m = jax.nn.silu(cond) @ w + b
    return jnp.split(m[:, None, :], 6, axis=-1)


def modulate(x, g, shift, scale):
    return rms_norm(x, g) * (1 + scale) + shift


def axial_rope(T):
    t = jnp.arange(T)
    n_freq = HEAD_DIM // 4
    inv = ROPE_BASE ** (-jnp.arange(n_freq, dtype=jnp.float32) / n_freq)
    ang = jnp.concatenate([(t // GRID_W).astype(jnp.float32)[:, None] * inv,
                           (t % GRID_W).astype(jnp.float32)[:, None] * inv], axis=-1)
    return jnp.cos(ang)[:, None, :], jnp.sin(ang)[:, None, :]


def apply_rope(x, cos, sin):
    xf = x.astype(jnp.float32)
    x1, x2 = jnp.split(xf, 2, axis=-1)
    return jnp.concatenate([x1 * cos - x2 * sin, x2 * cos + x1 * sin], axis=-1).astype(x.dtype)


def softmax_parts(parts, sink=None):
    sizes = [p.shape[-1] for p in parts]
    cols = list(parts)
    if sink is not None:
        cols.append(jnp.broadcast_to(sink, parts[0].shape[:-1] + (1,)))
    p = jax.nn.softmax(jnp.concatenate(cols, axis=-1), axis=-1)
    pieces = jnp.split(p, np.cumsum(sizes).tolist(), axis=-1)
    return pieces[:len(sizes)]


def context_attention(q, k, v, sink):
    B, S, HQ, hd = q.shape
    HK = k.shape[2]
    G = HQ // HK
    nb = S // Q_BLOCK
    scale = hd ** -0.5
    sink_b = None if sink is None else sink.astype(jnp.float32).reshape(1, HK, G, 1, 1)
    qb = q.reshape(B, nb, Q_BLOCK, HK, G, hd).swapaxes(0, 1)

    def one(qi):
        s = jnp.einsum('bqkgd,bskd->bkgqs', qi, k, preferred_element_type=jnp.float32) * scale
        (p,) = softmax_parts([s], sink_b)
        return jnp.einsum('bkgqs,bskd->bqkgd', p.astype(v.dtype), v)

    o = lax.map(one, qb)
    return o.swapaxes(0, 1).reshape(B, S, HQ, hd)


def window_attention(q, k, v, ck, cv, sink):
    B, T, HQ, hd = q.shape
    HK = k.shape[2]
    G = HQ // HK
    nb = T // A_BLOCK
    scale = hd ** -0.5
    qb = q.reshape(B, nb, A_BLOCK, HK, G, hd).swapaxes(0, 1)

    def band(x):
        xp = jnp.pad(x, ((0, 0), (A_BLOCK, A_BLOCK), (0, 0), (0, 0))).reshape(B, nb + 2, A_BLOCK, HK, hd)
        return jnp.concatenate([xp[:, :-2], xp[:, 1:-1], xp[:, 2:]], axis=2).swapaxes(0, 1)

    kb, vb = band(k), band(v)
    qpos = jnp.arange(nb)[:, None, None] * A_BLOCK + jnp.arange(A_BLOCK)[None, :, None]
    kpos = jnp.arange(nb)[:, None, None] * A_BLOCK - A_BLOCK + jnp.arange(3 * A_BLOCK)[None, None, :]
    mask = (jnp.abs(kpos - qpos) <= A_WINDOW) & (kpos >= 0) & (kpos < T)
    sink_b = sink.astype(jnp.float32).reshape(1, HK, G, 1, 1)

    def one(xs):
        qi, ki, vi, mi = xs
        s_loc = jnp.einsum('bqkgd,bskd->bkgqs', qi, ki, preferred_element_type=jnp.float32) * scale
        s_loc = jnp.where(mi[None, None, None], s_loc, NEG_INF)
        s_ctx = jnp.einsum('bqkgd,bpkd->bkgqp', qi, ck, preferred_element_type=jnp.float32) * scale
        p_loc, p_ctx = softmax_parts([s_loc, s_ctx], sink_b)
        return (jnp.einsum('bkgqs,bskd->bqkgd', p_loc.astype(vi.dtype), vi)
                + jnp.einsum('bkgqp,bpkd->bqkgd', p_ctx.astype(cv.dtype), cv))

    o = lax.map(one, (qb, kb, vb, mask))
    return o.swapaxes(0, 1).reshape(B, T, HQ, hd)


def short_conv(x, w):
    K = w.shape[0]
    L = x.shape[1]
    pad = K // 2
    xp = jnp.pad(x, ((0, 0), (pad, pad), (0, 0)))
    return sum(xp[:, i:i + L] * w[i] for i in range(K))


def hyena_filter_bank(L, w1, b1, w2, b2, w3, freq):
    f32 = jnp.float32
    t = jnp.linspace(0.0, 1.0, L, dtype=f32)[:, None]
    bands = (HY_EMB - 1) // 2
    omega = 2.0 * math.pi * jnp.arange(L, dtype=f32)[:, None] / L
    fb = jnp.linspace(1e-4, bands - 1, bands, dtype=f32)[None, :]
    feats = jnp.concatenate([t, jnp.cos(fb * omega), -jnp.sin(fb * omega)], axis=-1)
    fr = freq.astype(f32)
    h = jnp.sin(fr * (feats @ w1.astype(f32) + b1.astype(f32)))
    h = jnp.sin(fr * (h @ w2.astype(f32) + b2.astype(f32)))
    h = (h @ w3.astype(f32)).reshape(L, 2, HY_ORDER, HY_CH)
    max_decay = math.log(HY_DECAY_TARGET) / HY_FAST_DECAY
    min_decay = math.log(HY_DECAY_TARGET) / HY_SLOW_DECAY
    deltas = jnp.abs(jnp.linspace(min_decay, max_decay, HY_CH, dtype=f32))
    h = h * (jnp.exp(-t * deltas) + HY_SHIFT)[:, None, None, :]
    taps = jnp.concatenate([h[:, 0], jnp.zeros((1, HY_ORDER, HY_CH), f32), h[:0:-1, 1]], axis=0)
    taps = taps / jnp.sum(jnp.abs(taps), axis=0, keepdims=True)
    return jnp.fft.rfft(taps, axis=0)


def hyena_mixer(z, conv_w, conv_b, w1, b1, w2, b2, w3, freq, skip):
    L = z.shape[1]
    zf = (short_conv(z, conv_w) + conv_b).astype(jnp.float32)
    v, x1, x2 = jnp.split(zf, 3, axis=-1)
    filt = hyena_filter_bank(L, w1, b1, w2, b2, w3, freq)
    skip = skip.astype(jnp.float32)

    def long_conv(u, o):
        y = jnp.fft.irfft(jnp.fft.rfft(u, n=2 * L, axis=1) * filt[None, :, o], n=2 * L, axis=1)[:, :L]
        return y + u * skip[o]

    y = x1 * long_conv(v, 0)
    y = x2 * long_conv(y, 1)
    return y.astype(z.dtype)


def l2norm(x):
    xf = x.astype(jnp.float32)
    return xf * lax.rsqrt(jnp.sum(xf * xf, axis=-1, keepdims=True) + EPS)


def chunk_gated_delta(q, k, v, g, beta, s0):
    B, L, H, dk = q.shape
    dv = v.shape[-1]
    n = L // C_CHUNK

    def chunks(x):
        x = x.reshape((B, n, C_CHUNK, H) + x.shape[3:])
        return jnp.moveaxis(jnp.moveaxis(x, 1, 0), 3, 2)

    qc, kc, vc, bc = chunks(q), chunks(k), chunks(v), chunks(beta)
    gc = jnp.cumsum(chunks(g), axis=-1)
    tri = jnp.tril(jnp.ones((C_CHUNK, C_CHUNK), bool))
    strict = jnp.tril(jnp.ones((C_CHUNK, C_CHUNK), bool), k=-1)
    gamma = jnp.exp(jnp.where(tri, gc[..., :, None] - gc[..., None, :], NEG_INF))
    kb = kc * bc[..., None]
    a_mat = jnp.where(strict, jnp.einsum('nbhid,nbhjd->nbhij', kb, kc) * gamma, 0.0) + jnp.eye(C_CHUNK, dtype=jnp.float32)
    rhs = jnp.concatenate([vc * bc[..., None], kb * jnp.exp(gc)[..., None]], axis=-1)
    sol = lax.linalg.triangular_solve(a_mat, rhs, left_side=True, lower=True)
    u, w = sol[..., :dv], sol[..., dv:]
    attn = jnp.where(tri, jnp.einsum('nbhid,nbhjd->nbhij', qc, kc) * gamma, 0.0)
    g_last = gc[..., -1]
    q_dec = qc * jnp.exp(gc)[..., None]
    k_dec = kc * jnp.exp(g_last[..., None] - gc)[..., None]

    def step(S, xs):
        u_i, w_i, a_i, qd, kd, gl = xs
        v_new = u_i - jnp.einsum('bhck,bhkv->bhcv', w_i, S)
        o = jnp.einsum('bhck,bhkv->bhcv', qd, S) + jnp.einsum('bhij,bhjv->bhiv', a_i, v_new)
        S = S * jnp.exp(gl)[..., None, None] + jnp.einsum('bhck,bhcv->bhkv', kd, v_new)
        return S, o

    S, o = lax.scan(step, s0.astype(jnp.float32), (u, w, attn, q_dec, k_dec, g_last))
    o = jnp.moveaxis(jnp.moveaxis(o, 2, 3), 0, 1).reshape(B, L, H, dv)
    return o, S


def deltanet_mixer(zq, zk, zv, zg, za, zb, conv_w, a_log, dt_bias, norm_w, s0):
    B, L, _ = zq.shape
    qkv = jax.nn.silu(short_conv(jnp.concatenate([zq, zk, zv], axis=-1), conv_w))
    q, k, v = [t.reshape(B, L, C_HEADS, HEAD_DIM) for t in jnp.split(qkv, 3, axis=-1)]
    q = l2norm(q) * (HEAD_DIM ** -0.5)
    k = l2norm(k)
    v = v.astype(jnp.float32)
    beta = jax.nn.sigmoid(zb.astype(jnp.float32))
    g = -jnp.exp(a_log.astype(jnp.float32)) * jax.nn.softplus(za.astype(jnp.float32) + dt_bias.astype(jnp.float32))
    o_f, s_f = chunk_gated_delta(q, k, v, g[:, :, 0], beta[:, :, 0], s0[:, 0])
    o_b, s_b = chunk_gated_delta(q[:, ::-1], k[:, ::-1], v[:, ::-1], g[:, ::-1, 1], beta[:, ::-1, 1], s0[:, 1])
    o = o_f + o_b[:, ::-1]
    gate = jax.nn.silu(zg.reshape(B, L, C_HEADS, HEAD_DIM).astype(jnp.float32))
    o = rms_norm(o, norm_w) * gate
    return o.reshape(B, L, C_W).astype(zq.dtype), jnp.stack([s_f, s_b], axis=1)


def neighbourhood_attention(q, k, v, ck, cv, rpb):
    B, T, H, hd = q.shape
    rows = T // GRID_W
    kh = min(NA_KH_MAX, rows)
    scale = hd ** -0.5
    r = jnp.arange(rows)
    rs = jnp.clip(r - kh // 2, 0, rows - kh)
    key_rows = rs[:, None] + jnp.arange(kh)[None, :]
    idx = (key_rows[:, :, None] * GRID_W + jnp.arange(GRID_W)).reshape(rows, kh * GRID_W)
    col = jnp.arange(GRID_W)
    cs = jnp.clip(col - NA_KW // 2, 0, GRID_W - NA_KW)
    kcol = jnp.tile(col, kh)
    col_ok = (kcol[None, :] >= cs[:, None]) & (kcol[None, :] < cs[:, None] + NA_KW)
    roff = jnp.repeat(key_rows - r[:, None], GRID_W, axis=1) + NA_KH_MAX - 1
    coff = jnp.clip(kcol[None, :] - col[:, None] + NA_KW - 1, 0, 2 * NA_KW - 2)
    qr = q.reshape(B, rows, GRID_W, H, hd).swapaxes(0, 1)
    rpb_f = rpb.astype(jnp.float32)

    def one(xs):
        qi, ii, ro = xs
        ki = k[:, ii]
        vi = v[:, ii]
        bias = rpb_f[:, ro[None, :], coff]
        s_loc = jnp.einsum('bqhd,bkhd->bhqk', qi, ki, preferred_element_type=jnp.float32) * scale + bias[None]
        s_loc = jnp.where(col_ok[None, None], s_loc, NEG_INF)
        s_ctx = jnp.einsum('bqhd,bphd->bhqp', qi, ck, preferred_element_type=jnp.float32) * scale
        p_loc, p_ctx = softmax_parts([s_loc, s_ctx])
        return (jnp.einsum('bhqk,bkhd->bqhd', p_loc.astype(vi.dtype), vi)
                + jnp.einsum('bhqp,bphd->bqhd', p_ctx.astype(cv.dtype), cv))

    o = lax.map(one, (qr, idx, roff))
    return o.swapaxes(0, 1).reshape(B, T, H, hd)


def expert_choice_ffn(h, w_router, w_gate, w_up, w_down):
    B, T, D = h.shape
    cap = EC_CAPACITY * T // N_EXPERTS
    aff = jax.nn.softmax(jnp.einsum('btd,de->bte', h, w_router, preferred_element_type=jnp.float32), axis=-1)
    gate, idx = lax.top_k(aff.swapaxes(1, 2), cap)
    xg = jax.vmap(lambda hb, ib: hb[ib])(h, idx)
    a = jnp.einsum('becd,edf->becf', xg, w_gate)
    u = jnp.einsum('becd,edf->becf', xg, w_up)
    y = jnp.einsum('becf,efd->becd', jax.nn.silu(a) * u, w_down) * gate[..., None].astype(h.dtype)
    return jax.vmap(lambda yb, ib: jnp.zeros((T, D), yb.dtype).at[ib.reshape(-1)].add(yb.reshape(-1, D)))(y, idx)


def split_even(z):
    B, L = z.shape[:2]
    q = z[..., :A_Q_W].reshape(B, L, A_HEADS, HEAD_DIM)
    k = z[..., A_Q_W:A_Q_W + A_KV_W].reshape(B, L, A_KV_HEADS, HEAD_DIM)
    v = z[..., A_Q_W + A_KV_W:A_Q_W + 2 * A_KV_W].reshape(B, L, A_KV_HEADS, HEAD_DIM)
    return q, k, v, z[..., A_Q_W + 2 * A_KV_W:]


def split_odd(z):
    B, L = z.shape[:2]
    zq, zk, zv, zg = [z[..., i * C_W:(i + 1) * C_W] for i in range(4)]
    off = 4 * C_W
    za = z[..., off:off + 2 * C_HEADS].reshape(B, L, 2, C_HEADS)
    zb = z[..., off + 2 * C_HEADS:off + 4 * C_HEADS].reshape(B, L, 2, C_HEADS)
    off = off + 4 * C_HEADS
    nq, nk, nv = [z[..., off + i * D_W:off + (i + 1) * D_W].reshape(B, L, D_HEADS, HEAD_DIM) for i in range(3)]
    return zq, zk, zv, zg, za, zb, nq, nk, nv


def kernel(x_prompt, x_sample, cache_attn_k, cache_attn_v, state_delta, cache_na_k, cache_na_v,
           c, c_ctx, w_ada, b_ada, norm_mix, norm_ffn, norm_final,
           even_w_in, even_w_out, attn_sink, hy_conv_w, hy_conv_b, hy_w1, hy_b1, hy_w2, hy_b2,
           hy_w3, hy_freq, hy_skip, odd_w_in, odd_w_out, gdn_conv_w, gdn_a_log, gdn_dt_bias,
           gdn_norm, na_rpb, moe_router, moe_w_gate, moe_w_up, moe_w_down):
    xp, xs = x_prompt, x_sample
    bp = xp.shape[0]
    dft_p, dft_s = dft_tables(xp.shape[1]), dft_tables(xs.shape[1])
    ada = ada_params_all(jnp.concatenate([c_ctx[None, :], c], axis=0), w_ada, b_ada)
    new_ak, new_av, new_st, new_nk, new_nv = [], [], [], [], []
    for l in range(DEPTH):
        j = l // 2
        mp = jnp.split(ada[l, :1, None, :], 6, axis=-1)
        ms = jnp.split(ada[l, 1:, None, :], 6, axis=-1)
        mod_p = (norm_mix[l], mp[0], mp[1])
        mod_s = (norm_mix[l], ms[0], ms[1])
        if l % 2 == 0:
            def hyena(zh, tables):
                taps = hyena_taps(zh.shape[1], hy_w1[j], hy_b1[j], hy_w2[j], hy_b2[j], hy_w3[j], hy_freq[j])
                return hyena_mixer_pallas(zh, hy_conv_w[j], hy_conv_b[j], hy_skip[j], tables, hyena_spectrum(tables, *taps))

            w_in = even_w_in[j]
            w_groups = [w_in[:, :A_Q_W], w_in[:, A_Q_W:A_Q_W + A_KV_W], w_in[:, A_Q_W + A_KV_W:A_Q_W + 2 * A_KV_W],
                        w_in[:, A_Q_W + 2 * A_KV_W:]]
            q, k, v, zh = proj_multi(xp, *mod_p, w_groups, [F32] * 4)
            oa = context_attention_pallas(q, k, v, attn_sink[j], A_HEADS, A_KV_HEADS)
            xp_new = proj_concat(oa, hyena(zh, dft_p), even_w_out[j], xp, mp[2])
            new_ak.append(k.reshape(bp, SEQ, A_KV_HEADS, HEAD_DIM))
            new_av.append(v.reshape(bp, SEQ, A_KV_HEADS, HEAD_DIM))
            q, k, v, zh = proj_multi(xs, *mod_s, w_groups, [F32] * 4)
            ck = cache_attn_k[:, j].reshape(DEC_BATCH, PAST_LEN, A_KV_W).astype(BF16)
            cv = cache_attn_v[:, j].reshape(DEC_BATCH, PAST_LEN, A_KV_W).astype(BF16)
            oa = window_attention_pallas(q, k, v, ck, cv, attn_sink[j])
            xs_new = proj_concat(oa, hyena(zh, dft_s), even_w_out[j], xs, ms[2])
        else:
            w_in = odd_w_in[j]
            ab0 = 4 * C_W
            ab_cols = [w_in[:, ab0 + o * C_HEADS:ab0 + (o + 1) * C_HEADS] for o in (0, 2, 1, 3)]
            w_ab = jnp.concatenate(ab_cols + [jnp.zeros((D_MODEL, LANES - 4 * C_HEADS), w_in.dtype)], axis=1)
            n0 = ab0 + 4 * C_HEADS
            w_groups = [w_in[:, :3 * C_W], w_in[:, 3 * C_W:4 * C_W], w_ab,
                        w_in[:, n0:n0 + D_W], w_in[:, n0 + D_W:n0 + 2 * D_W], w_in[:, n0 + 2 * D_W:]]

            def deltanet(zqkv, zab, zg, s0):
                qd, kd, vd, gb = gdn_prep(zqkv, zab, gdn_conv_w[j], gdn_a_log[j], gdn_dt_bias[j])
                return gdn_scan(qd, kd, vd, gb, zg, s0, gdn_norm[j])

            zqkv, zg, zab, nq, nk, nv = proj_multi(xp, *mod_p, w_groups, [F32] * 6)
            oc, st = deltanet(zqkv, zab, zg, jnp.zeros((bp, 2, C_HEADS, HEAD_DIM, HEAD_DIM), F32))
            od = context_attention_pallas(nq, nk, nv, None, D_HEADS, D_HEADS)
            xp_new = proj_concat(oc, od, odd_w_out[j], xp, mp[2])
            new_st.append(st)
            new_nk.append(nk.reshape(bp, SEQ, D_HEADS, HEAD_DIM))
            new_nv.append(nv.reshape(bp, SEQ, D_HEADS, HEAD_DIM))
            zqkv, zg, zab, nq, nk, nv = proj_multi(xs, *mod_s, w_groups, [F32, F32, F32, BF16, BF16, BF16])
            oc, _ = deltanet(zqkv, zab, zg, state_delta[:, j])
            ck = cache_na_k[:, j].reshape(DEC_BATCH, PAST_LEN, D_W).astype(BF16)
            cv = cache_na_v[:, j].reshape(DEC_BATCH, PAST_LEN, D_W).astype(BF16)
            od = neighbourhood_attention_pallas(nq, nk, nv, ck, cv, na_rpb[j])
            xs_new = proj_concat(oc, od, odd_w_out[j], xs, ms[2])
        xp, xs = xp_new, xs_new
        moe = (moe_router[l], moe_w_gate, moe_w_up, moe_w_down, l)
        xp = moe_block(xp, norm_ffn[l], mp[3], mp[4], mp[5], *moe)
        xs = moe_block(xs, norm_ffn[l], ms[3], ms[4], ms[5], *moe)
    y_prompt = final_norm(xp, norm_final)
    y_sample = final_norm(xs, norm_final)
    return (y_prompt, y_sample, jnp.stack(new_ak, axis=1), jnp.stack(new_av, axis=1), jnp.stack(new_st, axis=1),
            jnp.stack(new_nk, axis=1), jnp.stack(new_nv, axis=1))
```

```python
import functools
import math
import jax, jax.numpy as jnp
from jax import lax
import numpy as np
from jax.experimental import pallas as pl
from jax.experimental.pallas import tpu as pltpu

D_MODEL = 1024
BATCH = 32
SEQ = 256
DEPTH = 4
DEC_BATCH = 4
DEC_SEQ = 4096
PAST_LEN = 512

GRID_W = 64
HEAD_DIM = 64
N_EVEN = (DEPTH + 1) // 2
N_ODD = DEPTH // 2
Q_BLOCK = 128
A_HEADS = D_MODEL // 128
A_KV_HEADS = A_HEADS // 4
A_WINDOW = 128
A_BLOCK = 128
ROPE_BASE = 10000.0
HY_CH = D_MODEL // 2
HY_ORDER = 2
HY_SHORT = 3
HY_EMB = 33
HY_FILT_W = 64
HY_FAST_DECAY = 0.3
HY_SLOW_DECAY = 1.5
HY_DECAY_TARGET = 1e-2
HY_SHIFT = 0.05
C_HEADS = D_MODEL // 128
C_SHORT = 3
C_CHUNK = 64
D_HEADS = D_MODEL // 128
NA_KH_MAX = 8
NA_KW = 16
N_EXPERTS = 16
EC_CAPACITY = 2
MOE_D_FF = D_MODEL
EPS = 1e-6
NEG_INF = -1e30

A_Q_W = A_HEADS * HEAD_DIM
A_KV_W = A_KV_HEADS * HEAD_DIM
EVEN_IN = A_Q_W + 2 * A_KV_W + 3 * HY_CH
EVEN_MIX = A_Q_W + HY_CH
C_W = C_HEADS * HEAD_DIM
D_W = D_HEADS * HEAD_DIM
ODD_IN = 4 * C_W + 4 * C_HEADS + 3 * D_W
ODD_MIX = C_W + D_W

VMEM_LIMIT_BYTES = 48 * 1024 * 1024


def _mm_kernel(x_ref, w_ref, o_ref):
    o_ref[...] = jnp.dot(x_ref[...].astype(jnp.bfloat16), w_ref[...], preferred_element_type=jnp.float32)


def pallas_matmul(x, w, tm=256):
    M, K = x.shape
    N = w.shape[1]
    assert M % tm == 0
    return pl.pallas_call(
        _mm_kernel,
        grid=(M // tm,),
        in_specs=[pl.BlockSpec((tm, K), lambda i: (i, 0)), pl.BlockSpec((K, N), lambda i: (0, 0))],
        out_specs=pl.BlockSpec((tm, N), lambda i: (i, 0)),
        out_shape=jax.ShapeDtypeStruct((M, N), jnp.float32),
        compiler_params=pltpu.CompilerParams(dimension_semantics=("arbitrary",), vmem_limit_bytes=VMEM_LIMIT_BYTES),
    )(x, w.astype(jnp.bfloat16))


def proj(x, w):
    B, L, K = x.shape
    return pallas_matmul(x.reshape(B * L, K), w).reshape(B, L, w.shape[1])


PROJ_TM = 512


def _request_of_tile(tm, L, per_request):
    return (lambda i: ((i * tm) // L, 0, 0)) if per_request else (lambda i: (0, 0, 0))


def _mm_multi_kernel(x_ref, g_ref, shift_ref, scale_ref, *refs):
    n = len(refs) // 2
    x = x_ref[...]
    y = x * lax.rsqrt(jnp.mean(x * x, axis=-1, keepdims=True) + EPS) * g_ref[...]
    h = (y * (1.0 + scale_ref[0]) + shift_ref[0]).astype(jnp.bfloat16)
    for w_ref, o_ref in zip(refs[:n], refs[n:]):
        o_ref[...] = jnp.dot(h, w_ref[...], preferred_element_type=jnp.float32).astype(o_ref.dtype)


def proj_multi(x, g, shift, scale, weights, out_dtypes, tm=PROJ_TM):
    B, L, K = x.shape
    M = B * L
    per_request = shift.shape[0] == B
    assert L % tm == 0 if per_request else M % tm == 0
    mod_spec = pl.BlockSpec((1, 1, K), _request_of_tile(tm, L, per_request))
    outs = pl.pallas_call(
        _mm_multi_kernel,
        grid=(M // tm,),
        in_specs=[pl.BlockSpec((tm, K), lambda i: (i, 0)), pl.BlockSpec((1, K), lambda i: (0, 0)), mod_spec, mod_spec]
        + [pl.BlockSpec(w.shape, lambda i: (0, 0)) for w in weights],
        out_specs=[pl.BlockSpec((tm, w.shape[1]), lambda i: (i, 0)) for w in weights],
        out_shape=[jax.ShapeDtypeStruct((M, w.shape[1]), dt) for w, dt in zip(weights, out_dtypes)],
        compiler_params=pltpu.CompilerParams(dimension_semantics=("arbitrary",), vmem_limit_bytes=VMEM_LIMIT_BYTES),
        name="in_projection",
    )(x.reshape(M, K), g.astype(jnp.float32)[None, :], shift, scale, *[w.astype(jnp.bfloat16) for w in weights])
    return [o.reshape(B, L, o.shape[1]) for o in outs]


def _mm2_kernel(a_ref, b_ref, wa_ref, wb_ref, x_ref, gate_ref, o_ref):
    mix = (jnp.dot(a_ref[...].astype(jnp.bfloat16), wa_ref[...], preferred_element_type=jnp.float32)
           + jnp.dot(b_ref[...].astype(jnp.bfloat16), wb_ref[...], preferred_element_type=jnp.float32))
    o_ref[...] = x_ref[...] + gate_ref[0] * mix


def proj_concat(a, b, w, x, gate, tm=PROJ_TM):
    B, L, Ka = a.shape
    Kb = b.shape[2]
    N = w.shape[1]
    M = B * L
    assert (L % tm == 0 if gate.shape[0] == B else M % tm == 0) and w.shape[0] == Ka + Kb
    wb16 = w.astype(jnp.bfloat16)
    out = pl.pallas_call(
        _mm2_kernel,
        grid=(M // tm,),
        in_specs=[pl.BlockSpec((tm, Ka), lambda i: (i, 0)), pl.BlockSpec((tm, Kb), lambda i: (i, 0)),
                  pl.BlockSpec((Ka, N), lambda i: (0, 0)), pl.BlockSpec((Kb, N), lambda i: (0, 0)),
                  pl.BlockSpec((tm, N), lambda i: (i, 0)),
                  pl.BlockSpec((1, 1, N), _request_of_tile(tm, L, gate.shape[0] == B))],
        out_specs=pl.BlockSpec((tm, N), lambda i: (i, 0)),
        out_shape=jax.ShapeDtypeStruct((M, N), jnp.float32),
        compiler_params=pltpu.CompilerParams(dimension_semantics=("arbitrary",), vmem_limit_bytes=VMEM_LIMIT_BYTES),
        name="out_projection",
    )(a.reshape(M, Ka), b.reshape(M, Kb), wb16[:Ka], wb16[Ka:], x.reshape(M, N), gate)
    return out.reshape(B, L, N)


LANES = 128
BF16 = jnp.bfloat16
F32 = jnp.float32


def _dot_nt(a, b):
    return lax.dot_general(a, b, (((1,), (1,)), ((), ())), preferred_element_type=F32)


def _dot(a, b):
    return jnp.dot(a, b, preferred_element_type=F32)


def _low_half(shape):
    return lax.broadcasted_iota(jnp.int32, shape, 1) < HEAD_DIM


def _softmax_pv(units):
    n = range(len(units))
    m = [functools.reduce(jnp.maximum, [s.max(axis=-1, keepdims=True) for s in units[u][0]]) for u in n]
    m = [m[u] if units[u][2] is None else jnp.maximum(m[u], units[u][2]) for u in n]
    p = [[jnp.exp(s - m[u]) for s in units[u][0]] for u in n]
    l = [functools.reduce(lambda a, b: a + b, [x.sum(axis=-1, keepdims=True) for x in p[u]]) for u in n]
    l = [l[u] if units[u][2] is None else l[u] + jnp.exp(units[u][2] - m[u]) for u in n]
    o = [functools.reduce(lambda a, b: a + b, [_dot(x.astype(BF16), v) for x, v in zip(p[u], units[u][1])]) for u in n]
    return [o[u] / l[u] for u in n]


def _place_head(q_slab, src_half, dst_half, low):
    x = q_slab if src_half == dst_half else pltpu.roll(q_slab, HEAD_DIM, axis=1)
    return jnp.where(low if dst_half == 0 else ~low, x, jnp.zeros_like(x))


NA_KEYS = NA_KH_MAX * GRID_W


def _na_kernel(q_ref, k_ref, v_ref, ck_ref, cv_ref, tab_ref, o_ref):
    r = pl.program_id(1)
    rows = k_ref.shape[1] // GRID_W
    rs = jnp.clip(r - NA_KH_MAX // 2, 0, rows - NA_KH_MAX)
    start = pl.multiple_of(rs * GRID_W, GRID_W)
    scale = HEAD_DIM ** -0.5
    low = _low_half((GRID_W, LANES))
    pairs = range(D_HEADS // 2)
    cols = [slice(p * LANES, (p + 1) * LANES) for p in pairs]
    qp = [q_ref[0, :, c] for c in cols]
    kp = [k_ref[0, pl.ds(start, NA_KEYS), c] for c in cols]
    vp = [v_ref[0, pl.ds(start, NA_KEYS), c] for c in cols]
    ckp = [ck_ref[0, :, c] for c in cols]
    cvp = [cv_ref[0, :, c] for c in cols]
    heads = [(p, half) for p in pairs for half in range(2)]
    qm = [jnp.where(low if half == 0 else ~low, qp[p], jnp.zeros_like(qp[p])) for p, half in heads]
    s_loc = [_dot_nt(qm[i], kp[p]) * scale + tab_ref[i, 0] for i, (p, _) in enumerate(heads)]
    s_ctx = [_dot_nt(qm[i], ckp[p]) * scale for i, (p, _) in enumerate(heads)]
    outs = _softmax_pv([([s_loc[i], s_ctx[i]], [vp[p], cvp[p]], None) for i, (p, _) in enumerate(heads)])
    for p in pairs:
        o_ref[0, :, cols[p]] = jnp.where(low, outs[2 * p], outs[2 * p + 1]).astype(o_ref.dtype)


def na_bias_table(rpb):
    col = jnp.arange(GRID_W)
    cs = jnp.clip(col - NA_KW // 2, 0, GRID_W - NA_KW)
    col_ok = (col[None, :] >= cs[:, None]) & (col[None, :] < cs[:, None] + NA_KW)
    coff = jnp.clip(col[None, :] - col[:, None] + NA_KW - 1, 0, 2 * NA_KW - 2)
    base = jnp.where(col_ok[None, None], rpb.astype(F32)[:, :, coff], NEG_INF)
    tab = jnp.stack([base[:, o:o + NA_KH_MAX] for o in range(NA_KH_MAX)], axis=1)
    return tab.transpose(0, 1, 3, 2, 4).reshape(D_HEADS, NA_KH_MAX, GRID_W, NA_KEYS)


def neighbourhood_attention_pallas(q, k, v, ck, cv, rpb):
    B, T, W = q.shape
    P = ck.shape[1]
    rows = T // GRID_W
    assert rows >= NA_KH_MAX and W == D_W
    tab = na_bias_table(rpb)

    def tab_index(b, r):
        rs = jnp.clip(r - NA_KH_MAX // 2, 0, rows - NA_KH_MAX)
        return (0, rs - r + NA_KH_MAX - 1, 0, 0)

    return pl.pallas_call(
        _na_kernel,
        grid=(B, rows),
        in_specs=[
            pl.BlockSpec((1, GRID_W, W), lambda b, r: (b, r, 0)),
            pl.BlockSpec((1, T, W), lambda b, r: (b, 0, 0)),
            pl.BlockSpec((1, T, W), lambda b, r: (b, 0, 0)),
            pl.BlockSpec((1, P, W), lambda b, r: (b, 0, 0)),
            pl.BlockSpec((1, P, W), lambda b, r: (b, 0, 0)),
            pl.BlockSpec((D_HEADS, 1, GRID_W, NA_KEYS), tab_index),
        ],
        out_specs=pl.BlockSpec((1, GRID_W, W), lambda b, r: (b, r, 0)),
        out_shape=jax.ShapeDtypeStruct((B, T, W), BF16),
        compiler_params=pltpu.CompilerParams(dimension_semantics=("arbitrary", "arbitrary"),
                                             vmem_limit_bytes=VMEM_LIMIT_BYTES),
        name="na_attention",
    )(q, k, v, ck, cv, tab)


def rope_tables(T):
    cos, sin = axial_rope(T)
    cos, sin = cos[:, 0, :], sin[:, 0, :]
    cos_t = jnp.concatenate([cos, cos, cos, cos], axis=-1)
    sin_t = jnp.concatenate([-sin, sin, -sin, sin], axis=-1)
    return cos_t, sin_t


def _rope(x, cos_t, sin_t):
    half = HEAD_DIM // 2
    lane = lax.broadcasted_iota(jnp.int32, x.shape, 1)
    first = (lane % HEAD_DIM) < half
    swapped = jnp.where(first, pltpu.roll(x, LANES - half, axis=1), pltpu.roll(x, half, axis=1))
    return x * cos_t + swapped * sin_t


def _win_kernel(sink_ref, q_ref, k_ref, v_ref, ck_ref, cv_ref, cos_ref, sin_ref, o_ref):
    i = pl.program_id(1)
    T = k_ref.shape[1]
    span = 3 * A_BLOCK
    start = pl.multiple_of(jnp.clip((i - 1) * A_BLOCK, 0, T - span), A_BLOCK)
    delta = i * A_BLOCK - start
    q0 = pl.multiple_of(i * A_BLOCK, A_BLOCK)
    scale = HEAD_DIM ** -0.5
    kw = _rope(k_ref[0, pl.ds(start, span), :], cos_ref[pl.ds(start, span), :], sin_ref[pl.ds(start, span), :]).astype(BF16)
    vw = v_ref[0, pl.ds(start, span), :].astype(BF16)
    ck = ck_ref[0]
    cv = cv_ref[0]
    cos_q = cos_ref[pl.ds(q0, A_BLOCK), :]
    sin_q = sin_ref[pl.ds(q0, A_BLOCK), :]
    qi = lax.broadcasted_iota(jnp.int32, (A_BLOCK, span), 0)
    kj = lax.broadcasted_iota(jnp.int32, (A_BLOCK, span), 1)
    band = jnp.abs(kj - delta - qi) <= A_WINDOW
    low = _low_half((A_BLOCK, LANES))
    group = A_HEADS // A_KV_HEADS
    pairs = range(A_HEADS // 2)
    cols = [slice(p * LANES, (p + 1) * LANES) for p in pairs]
    q_slab = [_rope(q_ref[0, :, c], cos_q, sin_q) for c in cols]
    heads = range(A_HEADS)
    kv_of = [h // group for h in heads]
    qm = [_place_head(q_slab[h // 2], h % 2, kv_of[h], low).astype(BF16) for h in heads]
    s_loc = [jnp.where(band, _dot_nt(qm[h], kw) * scale, NEG_INF) for h in heads]
    s_ctx = [_dot_nt(qm[h], ck) * scale for h in heads]
    outs = _softmax_pv([([s_loc[h], s_ctx[h]], [vw, cv], sink_ref[h]) for h in heads])
    outs = [outs[h] if kv_of[h] == h % 2 else pltpu.roll(outs[h], HEAD_DIM, axis=1) for h in heads]
    for p in pairs:
        o_ref[0, :, cols[p]] = jnp.where(low, outs[2 * p], outs[2 * p + 1]).astype(o_ref.dtype)


def window_attention_pallas(q, k, v, ck, cv, sink):
    B, T, QW = q.shape
    KW = k.shape[2]
    P = ck.shape[1]
    assert KW == LANES and QW == A_Q_W and T % A_BLOCK == 0 and T >= 3 * A_BLOCK
    cos_t, sin_t = rope_tables(T)
    return pl.pallas_call(
        _win_kernel,
        grid=(B, T // A_BLOCK),
        in_specs=[
            pl.BlockSpec(memory_space=pltpu.SMEM),
            pl.BlockSpec((1, A_BLOCK, QW), lambda b, i: (b, i, 0)),
            pl.BlockSpec((1, T, KW), lambda b, i: (b, 0, 0)),
            pl.BlockSpec((1, T, KW), lambda b, i: (b, 0, 0)),
            pl.BlockSpec((1, P, KW), lambda b, i: (b, 0, 0)),
            pl.BlockSpec((1, P, KW), lambda b, i: (b, 0, 0)),
            pl.BlockSpec((T, LANES), lambda b, i: (0, 0)),
            pl.BlockSpec((T, LANES), lambda b, i: (0, 0)),
        ],
        out_specs=pl.BlockSpec((1, A_BLOCK, QW), lambda b, i: (b, i, 0)),
        out_shape=jax.ShapeDtypeStruct((B, T, QW), BF16),
        compiler_params=pltpu.CompilerParams(dimension_semantics=("arbitrary", "arbitrary"),
                                             vmem_limit_bytes=VMEM_LIMIT_BYTES),
        name="window_attention",
    )(sink.astype(F32), q, k, v, ck, cv, cos_t, sin_t)


def _ctx_kernel(sink_ref, q_ref, k_ref, v_ref, o_ref, *, n_q_heads, n_kv_heads, use_sink):
    S = q_ref.shape[1]
    scale = HEAD_DIM ** -0.5
    low = _low_half((S, LANES))
    group = n_q_heads // n_kv_heads
    pairs = range(n_q_heads // 2)
    cols = [slice(p * LANES, (p + 1) * LANES) for p in pairs]
    q_slab = [q_ref[0, :, c] for c in cols]
    heads = range(n_q_heads)
    kv_of = [h // group for h in heads]
    kcols = [slice((kv // 2) * LANES, (kv // 2 + 1) * LANES) for kv in kv_of]
    qm = [_place_head(q_slab[h // 2], h % 2, kv_of[h] % 2, low).astype(BF16) for h in heads]
    s = [_dot_nt(qm[h], k_ref[0, :, kcols[h]].astype(BF16)) * scale for h in heads]
    outs = _softmax_pv([([s[h]], [v_ref[0, :, kcols[h]].astype(BF16)], sink_ref[h] if use_sink else None) for h in heads])
    outs = [outs[h] if kv_of[h] % 2 == h % 2 else pltpu.roll(outs[h], HEAD_DIM, axis=1) for h in heads]
    for p in pairs:
        o_ref[0, :, cols[p]] = jnp.where(low, outs[2 * p], outs[2 * p + 1]).astype(o_ref.dtype)


def context_attention_pallas(q, k, v, sink, n_q_heads, n_kv_heads):
    B, S, QW = q.shape
    KW = k.shape[2]
    use_sink = sink is not None
    sink_arr = sink.astype(F32) if use_sink else jnp.zeros((n_q_heads,), F32)
    return pl.pallas_call(
        functools.partial(_ctx_kernel, n_q_heads=n_q_heads, n_kv_heads=n_kv_heads, use_sink=use_sink),
        grid=(B,),
        in_specs=[
            pl.BlockSpec(memory_space=pltpu.SMEM),
            pl.BlockSpec((1, S, QW), lambda b: (b, 0, 0)),
            pl.BlockSpec((1, S, KW), lambda b: (b, 0, 0)),
            pl.BlockSpec((1, S, KW), lambda b: (b, 0, 0)),
        ],
        out_specs=pl.BlockSpec((1, S, QW), lambda b: (b, 0, 0)),
        out_shape=jax.ShapeDtypeStruct((B, S, QW), BF16),
        compiler_params=pltpu.CompilerParams(dimension_semantics=("arbitrary",), vmem_limit_bytes=VMEM_LIMIT_BYTES),
        name="context_attention",
    )(sink_arr, q, k, v)


HIGHEST = lax.Precision.HIGHEST
GDN_TM = 256
GDN_REQS = 2
GDN_SUB = 4
SUBLANES = 8


def _head_pair_sum_matrix():
    a = lax.broadcasted_iota(jnp.int32, (LANES, LANES), 0) // HEAD_DIM
    b = lax.broadcasted_iota(jnp.int32, (LANES, LANES), 1) // HEAD_DIM
    return (a == b).astype(BF16)


def _head_sums(x, pmat):
    hi = x.astype(BF16)
    lo = (x - hi.astype(F32)).astype(BF16)
    return _dot(hi, pmat) + _dot(lo, pmat)


def _gdn_prep_kernel(x_ref, prev_ref, next_ref, ab_ref, cw_ref, a_ref, dtb_ref, q_ref, k_ref, v_ref, gb_ref):
    i = pl.program_id(1)
    n = pl.num_programs(1)
    x = x_ref[0]
    tm = x.shape[0]
    row = lax.broadcasted_iota(jnp.int32, x.shape, 0)
    prev_row = jnp.where(i > 0, prev_ref[0, SUBLANES - 1:SUBLANES, :], 0.0)
    next_row = jnp.where(i < n - 1, next_ref[0, 0:1, :], 0.0)
    x_prev = jnp.where(row == 0, prev_row, pltpu.roll(x, 1, axis=0))
    x_next = jnp.where(row == tm - 1, next_row, pltpu.roll(x, tm - 1, axis=0))
    y = x_prev * cw_ref[0:1, :] + x * cw_ref[1:2, :] + x_next * cw_ref[2:3, :]
    y = y * jax.nn.sigmoid(y)
    pmat = _head_pair_sum_matrix()
    for p in range(C_W // LANES):
        qs = y[:, p * LANES:(p + 1) * LANES]
        ks = y[:, C_W + p * LANES:C_W + (p + 1) * LANES]
        q_ref[0, :, p * LANES:(p + 1) * LANES] = qs * lax.rsqrt(_head_sums(qs * qs, pmat) + EPS) * (HEAD_DIM ** -0.5)
        k_ref[0, :, p * LANES:(p + 1) * LANES] = ks * lax.rsqrt(_head_sums(ks * ks, pmat) + EPS)
    v_ref[0] = y[:, 2 * C_W:]
    ab = ab_ref[0, :, 0:4 * C_HEADS]
    lane = lax.broadcasted_iota(jnp.int32, ab.shape, 1)
    is_beta = (lane // C_HEADS) % 2 == 1
    t = ab + dtb_ref[...]
    softplus = jnp.maximum(t, 0.0) + jnp.log1p(jnp.exp(-jnp.abs(t)))
    gb = jnp.where(is_beta, jax.nn.sigmoid(ab), -jnp.exp(a_ref[...]) * softplus)
    gb_ref[0, 0] = gb[:, 0:2 * C_HEADS]
    gb_ref[0, 1] = gb[:, 2 * C_HEADS:4 * C_HEADS]


def gdn_prep(zqkv, zab, conv_w, a_log, dt_bias):
    B, L, W3 = zqkv.shape
    tm = GDN_TM
    assert L % tm == 0
    nb = tm // SUBLANES
    zero = jnp.zeros((C_HEADS,), F32)
    a_lane = jnp.concatenate([a_log[0], zero, a_log[1], zero]).astype(F32)[None, :]
    dtb_lane = jnp.concatenate([dt_bias[0], zero, dt_bias[1], zero]).astype(F32)[None, :]
    outs = pl.pallas_call(
        _gdn_prep_kernel,
        grid=(B, L // tm),
        in_specs=[
            pl.BlockSpec((1, tm, W3), lambda b, i: (b, i, 0)),
            pl.BlockSpec((1, SUBLANES, W3), lambda b, i: (b, jnp.maximum(i * nb - 1, 0), 0)),
            pl.BlockSpec((1, SUBLANES, W3), lambda b, i: (b, jnp.minimum((i + 1) * nb, L // SUBLANES - 1), 0)),
            pl.BlockSpec((1, tm, LANES), lambda b, i: (b, i, 0)),
            pl.BlockSpec((C_SHORT, W3), lambda b, i: (0, 0)),
            pl.BlockSpec((1, 4 * C_HEADS), lambda b, i: (0, 0)),
            pl.BlockSpec((1, 4 * C_HEADS), lambda b, i: (0, 0)),
        ],
        out_specs=[
            pl.BlockSpec((1, tm, C_W), lambda b, i: (b, i, 0)),
            pl.BlockSpec((1, tm, C_W), lambda b, i: (b, i, 0)),
            pl.BlockSpec((1, tm, C_W), lambda b, i: (b, i, 0)),
            pl.BlockSpec((1, 2, tm, 2 * C_HEADS), lambda b, i: (b, 0, i, 0)),
        ],
        out_shape=[jax.ShapeDtypeStruct((B, L, C_W), F32)] * 3 + [jax.ShapeDtypeStruct((B, 2, L, 2 * C_HEADS), F32)],
        compiler_params=pltpu.CompilerParams(dimension_semantics=("arbitrary", "arbitrary"),
                                             vmem_limit_bytes=VMEM_LIMIT_BYTES),
        name="gdn_prep",
    )(zqkv, zqkv, zqkv, zab, conv_w.astype(F32), a_lane, dtb_lane)
    return outs


def _gdn_kernel(q_ref, k_ref, v_ref, gb_ref, zg_ref, s0_ref, nw_ref, o_ref, st_ref, s_scr, of_scr):
    d = pl.program_id(1)
    c = pl.program_id(2)
    n = pl.num_programs(2)
    C = C_CHUNK
    fwd = d == 0
    nsub = q_ref.shape[1] // C
    block = jnp.where(fwd, c, n - 1 - c)

    @pl.when(c == 0)
    def _():
        s_scr[...] = s0_ref[:, 0]

    row = lax.broadcasted_iota(jnp.int32, (C, C), 0)
    col = lax.broadcasted_iota(jnp.int32, (C, C), 1)
    ahead = jnp.where(fwd, row - col, col - row)
    incl = ahead >= 0
    strict = ahead > 0
    incl_f = incl.astype(F32)
    nreq = q_ref.shape[0]
    units = [(r, h) for r in range(nreq) for h in range(C_HEADS)]
    s_cur = [s_scr[r, h] for r, h in units]
    pmat = _head_pair_sum_matrix()
    done = []
    for j in range(nsub):
        sub = jnp.where(fwd, j, nsub - 1 - j)
        rows = pl.ds(pl.multiple_of(sub * C, C), C)
        r0 = pl.multiple_of((block * nsub + sub) * C, C)
        gc, gc_t, g_last, beta = [], [], [], []
        for r in range(nreq):
            g = gb_ref[r, 0, rows, 0:C_HEADS]
            beta.append(gb_ref[r, 0, rows, C_HEADS:2 * C_HEADS])
            gc.append(jnp.dot(incl_f, g, precision=HIGHEST, preferred_element_type=F32))
            gc_t.append(gc[r].T)
            g_last.append(jnp.where(fwd, gc[r][C - 1:C, :], gc[r][0:1, :]))
        H = range(len(units))
        heads = [slice(h * HEAD_DIM, (h + 1) * HEAD_DIM) for _, h in units]
        s_old = s_cur
        qs = [q_ref[r, rows, heads[i]] for i, (r, _) in enumerate(units)]
        ks = [k_ref[r, rows, heads[i]] for i, (r, _) in enumerate(units)]
        vs = [v_ref[r, rows, heads[i]] for i, (r, _) in enumerate(units)]
        gcol = [gc[r][:, h:h + 1] for r, h in units]
        bcol = [beta[r][:, h:h + 1] for r, h in units]
        gl = [g_last[r][:, h:h + 1] for r, h in units]
        gamma = [jnp.exp(jnp.where(incl, gcol[i] - gc_t[r][h:h + 1, :], NEG_INF)) for i, (r, h) in enumerate(units)]
        egc = [jnp.exp(gcol[h]) for h in H]
        kb = [ks[h].astype(BF16) for h in H]
        nmat = [jnp.where(strict, _dot_nt(kb[h], kb[h]) * gamma[h], 0.0) * bcol[h] for h in H]
        attn = [(_dot_nt(qs[h].astype(BF16), kb[h]) * gamma[h]).astype(BF16) for h in H]
        xr = row ^ col
        eye = (row == col).astype(F32)
        tinv = [eye - jnp.where((xr >> 1) == 0, nmat[h], 0.0) for h in H]
        for lvl in range(1, 6):
            off_diag = (xr >> lvl) == 1
            wmat = [_dot(jnp.where(off_diag, nmat[h], 0.0).astype(BF16), tinv[h].astype(BF16)).astype(BF16) for h in H]
            tinv = [tinv[h] - _dot(tinv[h].astype(BF16), wmat[h]) for h in H]
        x = [jnp.concatenate([vs[h] * bcol[h], ks[h] * (bcol[h] * egc[h])], axis=1) for h in H]
        x = [x[h] + _dot((tinv[h] - eye).astype(BF16), x[h].astype(BF16)) for h in H]
        sb = [s_old[h].astype(BF16) for h in H]
        v_new = [x[h][:, :HEAD_DIM] - _dot(x[h][:, HEAD_DIM:].astype(BF16), sb[h]) for h in H]
        vb = [v_new[h].astype(BF16) for h in H]
        outs = [_dot((qs[h] * egc[h]).astype(BF16), sb[h]) + _dot(attn[h], vb[h]) for h in H]
        kd = [(ks[h] * jnp.exp(gl[h] - gcol[h])).astype(BF16) for h in H]
        s_new = [s_old[h] * jnp.exp(gl[h]) + lax.dot_general(kd[h], vb[h], (((0,), (0,)), ((), ())), preferred_element_type=F32)
                 for h in H]
        s_cur = s_new
        o = [jnp.concatenate(outs[r * C_HEADS:(r + 1) * C_HEADS], axis=1) for r in range(nreq)]
        done.append((rows, r0, o))

    for i, (r, h) in enumerate(units):
        s_scr[r, h] = s_cur[i]

    @pl.when(fwd)
    def _():
        for rows, r0, o in done:
            for r in range(nreq):
                of_scr[r, pl.ds(r0, C), :] = o[r]

    @pl.when(jnp.logical_not(fwd))
    def _():
        for rows, r0, o in done:
            for r in range(nreq):
                tot = of_scr[r, pl.ds(r0, C), :] + o[r]
                zg = zg_ref[r, rows, :]
                gate = zg * jax.nn.sigmoid(zg)
                for p in range(C_W // LANES):
                    cols = slice(p * LANES, (p + 1) * LANES)
                    t = tot[:, cols]
                    ms = _head_sums(t * t, pmat) * (1.0 / HEAD_DIM)
                    o_ref[r, rows, cols] = (t * lax.rsqrt(ms + EPS) * nw_ref[:, cols] * gate[:, cols]).astype(o_ref.dtype)

    @pl.when(c == n - 1)
    def _():
        st_ref[:, 0] = s_scr[...]


def gdn_scan(q, k, v, gb, zg, s0, norm_w):
    B, L, W = q.shape
    C = GDN_SUB * C_CHUNK
    n = L // C
    R = min(B, GDN_REQS)
    assert L % C == 0 and W == C_W and B % R == 0
    chunk_of = lambda d, c: jnp.where(d == 0, c, n - 1 - c)
    seq_spec = pl.BlockSpec((R, C, W), lambda b, d, c: (b, chunk_of(d, c), 0))
    state_spec = pl.BlockSpec((R, 1, C_HEADS, HEAD_DIM, HEAD_DIM), lambda b, d, c: (b, d, 0, 0, 0))
    nw = jnp.tile(norm_w.astype(F32), C_HEADS)[None, :]
    return pl.pallas_call(
        _gdn_kernel,
        grid=(B // R, 2, n),
        in_specs=[
            seq_spec, seq_spec, seq_spec,
            pl.BlockSpec((R, 1, C, 2 * C_HEADS), lambda b, d, c: (b, d, chunk_of(d, c), 0)),
            seq_spec,
            state_spec,
            pl.BlockSpec((1, W), lambda b, d, c: (0, 0)),
        ],
        out_specs=[
            pl.BlockSpec((R, C, W), lambda b, d, c: (b, jnp.where(d == 0, n - 1, n - 1 - c), 0)),
            state_spec,
        ],
        out_shape=[jax.ShapeDtypeStruct((B, L, W), BF16), jax.ShapeDtypeStruct(s0.shape, F32)],
        scratch_shapes=[pltpu.VMEM((R, C_HEADS, HEAD_DIM, HEAD_DIM), F32), pltpu.VMEM((R, L, W), F32)],
        compiler_params=pltpu.CompilerParams(dimension_semantics=("arbitrary", "arbitrary", "arbitrary"),
                                             vmem_limit_bytes=VMEM_LIMIT_BYTES),
        name="gdn_scan",
    )(q, k, v, gb, zg, s0.astype(F32), nw)


MOE_TT = 512
MOE_RT = 512
MOE_SELECT_REQS = 4
MOE_SELECT_REQS_LONG = 2


def _moe_router_kernel(x_ref, g_ref, shift_ref, scale_ref, wr_ref, h_ref, aff_ref):
    x = x_ref[0]
    y = x * lax.rsqrt(jnp.mean(x * x, axis=-1, keepdims=True) + EPS) * g_ref[...]
    h = y * (1.0 + scale_ref[0]) + shift_ref[0]
    h_hi = h.astype(BF16)
    h_ref[0] = h_hi
    h_lo = (h - h_hi.astype(F32)).astype(BF16)
    w = wr_ref[...]
    w_hi = w.astype(BF16)
    w_lo = (w - w_hi.astype(F32)).astype(BF16)
    logits = _dot(h_hi, w_hi) + _dot(h_hi, w_lo) + _dot(h_lo, w_hi)
    lane = lax.broadcasted_iota(jnp.int32, logits.shape, 1)
    logits = jnp.where(lane < N_EXPERTS, logits, NEG_INF)
    e = jnp.exp(logits - logits.max(axis=-1, keepdims=True))
    aff_ref[0] = e / e.sum(axis=-1, keepdims=True)


def _moe_select_kernel(aff_ref, slot_ref, start_ref, *, cap):
    nreq, T = aff_ref.shape[0], aff_ref.shape[1]
    reqs = range(nreq)
    bits = [pltpu.bitcast(aff_ref[q], jnp.int32) for q in reqs]

    def bisect(i, v):
        cand = [v[q] | (1 << (30 - i)) for q in reqs]
        cnt = [jnp.sum((bits[q] >= cand[q]).astype(jnp.int32), axis=0, keepdims=True) for q in reqs]
        return tuple(jnp.where(cnt[q] >= cap, cand[q], v[q]) for q in reqs)

    thr = lax.fori_loop(0, 31, bisect, tuple(jnp.zeros((1, LANES), jnp.int32) for _ in reqs))
    blk = min(T, MOE_TT)
    r = lax.broadcasted_iota(jnp.int32, (blk, blk), 0)
    c = lax.broadcasted_iota(jnp.int32, (blk, blk), 1)
    before = (c < r).astype(BF16)
    n_tiles = T // blk
    for q in reqs:
        gt = (bits[q] > thr[q]).astype(F32)
        eq = (bits[q] == thr[q]).astype(F32)
        need = cap - jnp.sum(gt, axis=0, keepdims=True)
        carry_gt = jnp.zeros((1, LANES), F32)
        carry_eq = jnp.zeros((1, LANES), F32)
        start_ref[q] = jnp.zeros(start_ref.shape[1:], F32)
        for b in range(n_tiles):
            rows = slice(b * blk, (b + 1) * blk)
            gt_b, eq_b = gt[rows], eq[rows]
            pos_gt = _dot(before, gt_b.astype(BF16)) + carry_gt
            pos_eq = _dot(before, eq_b.astype(BF16)) + carry_eq
            chosen = gt_b + eq_b * (pos_eq < need).astype(F32)
            slot_ref[q, rows, :] = jnp.where(chosen > 0.5, pos_gt + jnp.minimum(pos_eq, need), -1.0)
            carry_gt = carry_gt + jnp.sum(gt_b, axis=0, keepdims=True)
            carry_eq = carry_eq + jnp.sum(eq_b, axis=0, keepdims=True)
            start_ref[q, b + 1:b + 2, :] = carry_gt + jnp.minimum(carry_eq, need)


def _expert_column(a, e):
    lane = lax.broadcasted_iota(jnp.int32, a.shape, 1)
    return jnp.sum(jnp.where(lane == e, a, 0.0), axis=1, keepdims=True)


MOE_START_ROWS = 16
MOE_WIN = 128


def _one_hot_slots(slot_col, first, width):
    s = lax.broadcasted_iota(jnp.int32, (slot_col.shape[0], width), 1).astype(F32)
    return (slot_col - first == s).astype(BF16)


MOE_SLOT_ALIGN = 16


def _slot_windows(start_ref, b, e, k):
    base = (b * N_EXPERTS + e) * MOE_START_ROWS + k
    lo, hi = start_ref[base], start_ref[base + 1]
    first = (lo // MOE_SLOT_ALIGN) * MOE_SLOT_ALIGN
    return first, jnp.where(hi > lo, (hi - first + MOE_WIN - 1) // MOE_WIN, 0)


def _window_one_hot(col, first, w, cap):
    nominal = first + w * MOE_WIN
    s0 = pl.multiple_of(jnp.minimum(nominal, cap - MOE_WIN), MOE_SLOT_ALIGN)
    rel = jnp.where(col >= nominal.astype(F32), col - s0.astype(F32), -1.0)
    return s0, _one_hot_slots(rel, 0.0, MOE_WIN)


def _moe_gather_kernel(start_ref, slot_ref, h_ref, xg_ref, acc_ref):
    b, e = pl.program_id(0), pl.program_id(1)
    acc_ref[...] = jnp.zeros_like(acc_ref)
    for k in range(slot_ref.shape[1] // MOE_TT):
        rows = slice(k * MOE_TT, (k + 1) * MOE_TT)
        col = _expert_column(slot_ref[0, rows, :], e)
        first, count = _slot_windows(start_ref, b, e, k)

        def window(w, carry, col=col, rows=rows, first=first):
            s0, pt = _window_one_hot(col, first, w, acc_ref.shape[0])
            acc_ref[pl.ds(s0, MOE_WIN), :] += lax.dot_general(pt, h_ref[0, rows, :], (((0,), (0,)), ((), ())),
                                                              preferred_element_type=F32)
            return carry

        lax.fori_loop(0, count, window, 0)
    xg_ref[0] = acc_ref[...].astype(xg_ref.dtype)


def _moe_gather_short_kernel(slot_ref, h_ref, xg_ref, *, cap):
    T = slot_ref.shape[1]
    width = N_EXPERTS * cap
    spread = (lax.broadcasted_iota(jnp.int32, (LANES, width), 1) // cap
              == lax.broadcasted_iota(jnp.int32, (LANES, width), 0)).astype(BF16)
    slot_wide = _dot(slot_ref[0].astype(BF16), spread)
    lane_slot = (lax.broadcasted_iota(jnp.int32, (T, width), 1) % cap).astype(F32)
    pt = (slot_wide == lane_slot).astype(BF16)
    rows = lax.dot_general(pt, h_ref[0], (((0,), (0,)), ((), ())), preferred_element_type=F32)
    for e in range(N_EXPERTS):
        xg_ref[e] = rows[e * cap:(e + 1) * cap].astype(xg_ref.dtype)


def _moe_ffn_kernel(x_ref, wg_ref, wu_ref, wd_ref, y_ref, wg_s, wu_s, wd_s):
    @pl.when(pl.program_id(1) == 0)
    def _():
        wg_s[...] = wg_ref[0].astype(BF16)
        wu_s[...] = wu_ref[0].astype(BF16)
        wd_s[...] = wd_ref[0].astype(BF16)

    x = x_ref[0]
    a = _dot(x, wg_s[...])
    u = _dot(x, wu_s[...])
    hid = (a * jax.nn.sigmoid(a) * u).astype(BF16)
    y_ref[0] = _dot(hid, wd_s[...]).astype(y_ref.dtype)


def _moe_scatter_kernel(start_ref, slot_ref, aff_ref, y_ref, x_ref, gate_ref, o_ref, acc_ref):
    b, k = pl.program_id(0), pl.program_id(1)
    acc_ref[...] = jnp.zeros_like(acc_ref)
    slot, aff = slot_ref[0], aff_ref[0]
    for e in range(N_EXPERTS):
        col, weight = slot[:, e:e + 1], aff[:, e:e + 1]
        first, count = _slot_windows(start_ref, b, e, k)

        def window(w, carry, e=e, col=col, weight=weight, first=first):
            s0, pt = _window_one_hot(col, first, w, y_ref.shape[1])
            acc_ref[...] += weight * _dot(pt, y_ref[e, pl.ds(s0, MOE_WIN), :])
            return carry

        lax.fori_loop(0, count, window, 0)
    o_ref[0] = x_ref[0] + gate_ref[0] * acc_ref[...]


def _moe_scatter_short_kernel(slot_ref, aff_ref, y_ref, x_ref, gate_ref, o_ref, *, cap):
    slot, aff = slot_ref[0], aff_ref[0]
    acc = jnp.zeros(x_ref.shape[1:], F32)
    for e in range(N_EXPERTS):
        pt = _one_hot_slots(slot[:, e:e + 1], 0.0, cap)
        acc = acc + aff[:, e:e + 1] * _dot(pt, y_ref[e])
    o_ref[0] = x_ref[0] + gate_ref[0] * acc


def moe_route(x, g, shift, scale, w_router):
    B, T, D = x.shape
    tt = min(T, MOE_TT)
    cap = EC_CAPACITY * T // N_EXPERTS
    per_request = shift.shape[0] == B
    mod_spec = pl.BlockSpec((1, 1, D), (lambda b, k: (b, 0, 0)) if per_request else (lambda b, k: (0, 0, 0)))
    wr = jnp.concatenate([w_router.astype(F32), jnp.zeros((D, LANES - N_EXPERTS), F32)], axis=1)
    h, aff = pl.pallas_call(
        _moe_router_kernel,
        grid=(B, T // tt),
        in_specs=[pl.BlockSpec((1, tt, D), lambda b, k: (b, k, 0)), pl.BlockSpec((1, D), lambda b, k: (0, 0)),
                  mod_spec, mod_spec, pl.BlockSpec((D, LANES), lambda b, k: (0, 0))],
        out_specs=[pl.BlockSpec((1, tt, D), lambda b, k: (b, k, 0)), pl.BlockSpec((1, tt, LANES), lambda b, k: (b, k, 0))],
        out_shape=[jax.ShapeDtypeStruct((B, T, D), BF16), jax.ShapeDtypeStruct((B, T, LANES), F32)],
        compiler_params=pltpu.CompilerParams(dimension_semantics=("arbitrary", "arbitrary"),
                                             vmem_limit_bytes=VMEM_LIMIT_BYTES),
        name="moe_router",
    )(x, g.astype(F32)[None, :], shift, scale, wr)
    assert T // tt < MOE_START_ROWS
    rq = MOE_SELECT_REQS if T <= MOE_TT else MOE_SELECT_REQS_LONG
    rq = rq if B % rq == 0 else 1
    slot, start = pl.pallas_call(
        functools.partial(_moe_select_kernel, cap=cap),
        grid=(B // rq,),
        in_specs=[pl.BlockSpec((rq, T, LANES), lambda b: (b, 0, 0))],
        out_specs=[pl.BlockSpec((rq, T, LANES), lambda b: (b, 0, 0)),
                   pl.BlockSpec((rq, MOE_START_ROWS, LANES), lambda b: (b, 0, 0))],
        out_shape=[jax.ShapeDtypeStruct((B, T, LANES), F32), jax.ShapeDtypeStruct((B, MOE_START_ROWS, LANES), F32)],
        compiler_params=pltpu.CompilerParams(dimension_semantics=("arbitrary",), vmem_limit_bytes=VMEM_LIMIT_BYTES),
        name="moe_select",
    )(aff)
    start = start[:, :, :N_EXPERTS].astype(jnp.int32).transpose(0, 2, 1).reshape(-1)
    return h, aff, slot, start


def moe_gather(h, slot, start):
    B, T, D = h.shape
    cap = EC_CAPACITY * T // N_EXPERTS
    out_shape = jax.ShapeDtypeStruct((N_EXPERTS, B * cap, D), BF16)
    if T <= MOE_TT:
        return pl.pallas_call(
            functools.partial(_moe_gather_short_kernel, cap=cap),
            grid=(B,),
            in_specs=[pl.BlockSpec((1, T, LANES), lambda b: (b, 0, 0)), pl.BlockSpec((1, T, D), lambda b: (b, 0, 0))],
            out_specs=pl.BlockSpec((N_EXPERTS, cap, D), lambda b: (0, b, 0)),
            out_shape=out_shape,
            compiler_params=pltpu.CompilerParams(dimension_semantics=("arbitrary",), vmem_limit_bytes=VMEM_LIMIT_BYTES),
            name="moe_gather_short",
        )(slot, h)
    assert cap % MOE_WIN == 0 and T % MOE_TT == 0
    return pl.pallas_call(
        _moe_gather_kernel,
        grid_spec=pltpu.PrefetchScalarGridSpec(
            num_scalar_prefetch=1,
            grid=(B, N_EXPERTS),
            in_specs=[pl.BlockSpec((1, T, LANES), lambda b, e, st: (b, 0, 0)),
                      pl.BlockSpec((1, T, D), lambda b, e, st: (b, 0, 0))],
            out_specs=pl.BlockSpec((1, cap, D), lambda b, e, st: (e, b, 0)),
            scratch_shapes=[pltpu.VMEM((cap, D), F32)],
        ),
        out_shape=out_shape,
        compiler_params=pltpu.CompilerParams(dimension_semantics=("arbitrary",) * 2, vmem_limit_bytes=VMEM_LIMIT_BYTES),
        name="moe_gather",
    )(start, slot, h)


def moe_ffn(xg, w_gate, w_up, w_down, layer):
    E, R, D = xg.shape
    rt = min(R, MOE_RT)
    w_spec = pl.BlockSpec((None, 1, D, MOE_D_FF), lambda e, r: (layer, e, 0, 0))
    return pl.pallas_call(
        _moe_ffn_kernel,
        grid=(E, R // rt),
        in_specs=[pl.BlockSpec((1, rt, D), lambda e, r: (e, r, 0)), w_spec, w_spec,
                  pl.BlockSpec((None, 1, MOE_D_FF, D), lambda e, r: (layer, e, 0, 0))],
        out_specs=pl.BlockSpec((1, rt, D), lambda e, r: (e, r, 0)),
        out_shape=jax.ShapeDtypeStruct((E, R, D), BF16),
        scratch_shapes=[pltpu.VMEM((D, MOE_D_FF), BF16), pltpu.VMEM((D, MOE_D_FF), BF16), pltpu.VMEM((MOE_D_FF, D), BF16)],
        compiler_params=pltpu.CompilerParams(dimension_semantics=("arbitrary", "arbitrary"),
                                             vmem_limit_bytes=VMEM_LIMIT_BYTES),
        name="moe_ffn",
    )(xg, w_gate, w_up, w_down)


def moe_scatter(slot, aff, start, y, x, gate):
    B, T, D = x.shape
    cap = EC_CAPACITY * T // N_EXPERTS
    per_request = gate.shape[0] == B
    out_shape = jax.ShapeDtypeStruct((B, T, D), F32)
    if T <= MOE_TT:
        whole = lambda b: (b, 0, 0)
        return pl.pallas_call(
            functools.partial(_moe_scatter_short_kernel, cap=cap),
            grid=(B,),
            in_specs=[pl.BlockSpec((1, T, LANES), whole), pl.BlockSpec((1, T, LANES), whole),
                      pl.BlockSpec((N_EXPERTS, cap, D), lambda b: (0, b, 0)), pl.BlockSpec((1, T, D), whole),
                      pl.BlockSpec((1, 1, D), whole if per_request else (lambda b: (0, 0, 0)))],
            out_specs=pl.BlockSpec((1, T, D), whole),
            out_shape=out_shape,
            compiler_params=pltpu.CompilerParams(dimension_semantics=("arbitrary",), vmem_limit_bytes=VMEM_LIMIT_BYTES),
            name="moe_scatter_short",
        )(slot, aff, y, x, gate)
    tt = MOE_TT
    tile = lambda b, k, st: (b, k, 0)
    return pl.pallas_call(
        _moe_scatter_kernel,
        grid_spec=pltpu.PrefetchScalarGridSpec(
            num_scalar_prefetch=1,
            grid=(B, T // tt),
            in_specs=[pl.BlockSpec((1, tt, LANES), tile), pl.BlockSpec((1, tt, LANES), tile),
                      pl.BlockSpec((N_EXPERTS, cap, D), lambda b, k, st: (0, b, 0), pipeline_mode=pl.Buffered(1)),
                      pl.BlockSpec((1, tt, D), tile),
                      pl.BlockSpec((1, 1, D), (lambda b, k, st: (b, 0, 0)) if per_request else (lambda b, k, st: (0, 0, 0)))],
            out_specs=pl.BlockSpec((1, tt, D), tile),
            scratch_shapes=[pltpu.VMEM((tt, D), F32)],
        ),
        out_shape=out_shape,
        compiler_params=pltpu.CompilerParams(dimension_semantics=("arbitrary",) * 2, vmem_limit_bytes=VMEM_LIMIT_BYTES),
        name="moe_scatter",
    )(start, slot, aff, y, x, gate)


def moe_block(x, g, shift, scale, gate, w_router, w_gate, w_up, w_down, layer):
    h, aff, slot, start = moe_route(x, g, shift, scale, w_router)
    y = moe_ffn(moe_gather(h, slot, start), w_gate, w_up, w_down, layer)
    return moe_scatter(slot, aff, start, y, x, gate)


HY_TM = 512
HY_TK = 1024
HY_BG = 8
HY_BG_LONG = 2


def dft_tables(L):
    blk = min(L, HY_TM)
    t = jnp.arange(L, dtype=jnp.int32)

    def angles(f):
        return ((f[:, None] * t[None, :]) % (2 * L)).astype(F32) * (math.pi / L)

    a_hi = angles(jnp.arange(0, L, blk, dtype=jnp.int32))[:, None, :]
    a_lo = angles(jnp.arange(blk, dtype=jnp.int32))[None, :, :]
    cos_t = (jnp.cos(a_hi) * jnp.cos(a_lo) - jnp.sin(a_hi) * jnp.sin(a_lo)).reshape(L, L)
    sin_t = (jnp.sin(a_hi) * jnp.cos(a_lo) + jnp.cos(a_hi) * jnp.sin(a_lo)).reshape(L, L)
    return cos_t.astype(BF16), sin_t.astype(BF16)


def _alt_sign(rows, first_row):
    t = lax.broadcasted_iota(jnp.int32, (rows, 1), 0) + first_row
    return jnp.where(t % 2 == 0, 1.0, -1.0).astype(F32)


def _hy_prep_kernel(x_ref, prev_ref, next_ref, cw_ref, cb_ref, o_ref, v16_ref):
    i = pl.program_id(1)
    n = pl.num_programs(1)
    x = x_ref[0]
    tm = x.shape[0]
    row = lax.broadcasted_iota(jnp.int32, x.shape, 0)
    prev_row = jnp.where(i > 0, prev_ref[0, SUBLANES - 1:SUBLANES, :], 0.0)
    next_row = jnp.where(i < n - 1, next_ref[0, 0:1, :], 0.0)
    x_prev = jnp.where(row == 0, prev_row, pltpu.roll(x, 1, axis=0))
    x_next = jnp.where(row == tm - 1, next_row, pltpu.roll(x, tm - 1, axis=0))
    y = x_prev * cw_ref[0:1, :] + x * cw_ref[1:2, :] + x_next * cw_ref[2:3, :] + cb_ref[...]
    o_ref[0] = y
    v16_ref[0] = y[:, :HY_CH].astype(BF16)


def hyena_prep(zh, conv_w, conv_b):
    B, L, W = zh.shape
    tm = min(L, GDN_TM)
    nb = tm // SUBLANES
    return pl.pallas_call(
        _hy_prep_kernel,
        grid=(B, L // tm),
        in_specs=[
            pl.BlockSpec((1, tm, W), lambda b, i: (b, i, 0)),
            pl.BlockSpec((1, SUBLANES, W), lambda b, i: (b, jnp.maximum(i * nb - 1, 0), 0)),
            pl.BlockSpec((1, SUBLANES, W), lambda b, i: (b, jnp.minimum((i + 1) * nb, L // SUBLANES - 1), 0)),
            pl.BlockSpec((HY_SHORT, W), lambda b, i: (0, 0)),
            pl.BlockSpec((1, W), lambda b, i: (0, 0)),
        ],
        out_specs=[pl.BlockSpec((1, tm, W), lambda b, i: (b, i, 0)), pl.BlockSpec((1, tm, HY_CH), lambda b, i: (b, i, 0))],
        out_shape=[jax.ShapeDtypeStruct((B, L, W), F32), jax.ShapeDtypeStruct((B, L, HY_CH), BF16)],
        compiler_params=pltpu.CompilerParams(dimension_semantics=("arbitrary", "arbitrary"),
                                             vmem_limit_bytes=VMEM_LIMIT_BYTES),
        name="hyena_prep",
    )(zh, zh, zh, conv_w.astype(F32), conv_b.astype(F32)[None, :])


def _hy_taps_kernel(feat_ref, w1_ref, b1_ref, w2_ref, b2_ref, fr_ref, w3f_ref, w3b_ref, dec_ref, sum_ref, dif_ref):
    L = feat_ref.shape[0]
    fr = fr_ref[...]
    h = jnp.sin(fr * (jnp.dot(feat_ref[...], w1_ref[...], precision=HIGHEST, preferred_element_type=F32) + b1_ref[...]))
    h = jnp.sin(fr * (jnp.dot(h, w2_ref[...], precision=HIGHEST, preferred_element_type=F32) + b2_ref[...]))
    t = lax.broadcasted_iota(jnp.int32, (L, 1), 0)
    window = jnp.exp(-(t.astype(F32) / (L - 1)) * dec_ref[...]) + HY_SHIFT
    fwd = jnp.dot(h, w3f_ref[...], precision=HIGHEST, preferred_element_type=F32) * window
    bwd = jnp.where(t == 0, 0.0, jnp.dot(h, w3b_ref[...], precision=HIGHEST, preferred_element_type=F32) * window)
    inv = 1.0 / (jnp.sum(jnp.abs(fwd), axis=0, keepdims=True) + jnp.sum(jnp.abs(bwd), axis=0, keepdims=True))
    sum_ref[...] = (fwd + bwd) * inv
    dif_ref[...] = (bwd - fwd) * inv


def hyena_taps(L, w1, b1, w2, b2, w3, freq):
    f32 = F32
    t = jnp.linspace(0.0, 1.0, L, dtype=f32)[:, None]
    bands = (HY_EMB - 1) // 2
    omega = 2.0 * math.pi * jnp.arange(L, dtype=f32)[:, None] / L
    fb = jnp.linspace(1e-4, bands - 1, bands, dtype=f32)[None, :]
    feats = jnp.concatenate([t, jnp.cos(fb * omega), -jnp.sin(fb * omega)], axis=-1)
    max_decay = math.log(HY_DECAY_TARGET) / HY_FAST_DECAY
    min_decay = math.log(HY_DECAY_TARGET) / HY_SLOW_DECAY
    deltas = jnp.abs(jnp.linspace(min_decay, max_decay, HY_CH, dtype=f32))
    dec = jnp.tile(deltas, HY_ORDER)[None, :]
    n_col = HY_ORDER * HY_CH
    cb = 256
    full = lambda shape: pl.BlockSpec(shape, lambda j: (0, 0))
    col = lambda rows: pl.BlockSpec((rows, cb), lambda j: (0, j))
    return pl.pallas_call(
        _hy_taps_kernel,
        grid=(n_col // cb,),
        in_specs=[full((L, HY_EMB)), full((HY_EMB, HY_FILT_W)), full((1, HY_FILT_W)), full((HY_FILT_W, HY_FILT_W)),
                  full((1, HY_FILT_W)), full((1, HY_FILT_W)), col(HY_FILT_W), col(HY_FILT_W), col(1)],
        out_specs=[col(L), col(L)],
        out_shape=[jax.ShapeDtypeStruct((L, n_col), f32)] * 2,
        compiler_params=pltpu.CompilerParams(dimension_semantics=("arbitrary",), vmem_limit_bytes=VMEM_LIMIT_BYTES),
        name="hyena_taps",
    )(feats, w1.astype(f32), b1.astype(f32)[None, :], w2.astype(f32), b2.astype(f32)[None, :], freq.astype(f32)[None, :],
      w3.astype(f32)[:, :n_col], w3.astype(f32)[:, n_col:], dec)


def _hy_spec_kernel(ch_ref, sh_ref, sum_ref, dif_ref, hr_ref, hi_ref, ny_ref, acc_r, acc_i, acc_n):
    m, k = pl.program_id(1), pl.program_id(2)
    tk = sum_ref.shape[0]

    @pl.when(k == 0)
    def _():
        acc_r[...] = jnp.zeros_like(acc_r)
        acc_i[...] = jnp.zeros_like(acc_i)

    @pl.when(jnp.logical_and(k == 0, m == 0))
    def _():
        acc_n[...] = jnp.zeros_like(acc_n)

    a = sum_ref[...]
    acc_r[...] += _dot(ch_ref[...], a.astype(BF16))
    acc_i[...] += _dot(sh_ref[...], dif_ref[...].astype(BF16))

    @pl.when(m == 0)
    def _():
        acc_n[...] += jnp.sum(a * _alt_sign(tk, k * tk), axis=0, keepdims=True)

    @pl.when(k == pl.num_programs(2) - 1)
    def _():
        hr_ref[...] = acc_r[...]
        hi_ref[...] = acc_i[...]
        ny_ref[...] = jnp.broadcast_to(acc_n[...], ny_ref.shape)


def hyena_spectrum(tables, tap_sum, tap_dif):
    L, N = tap_sum.shape
    tm = min(L, HY_TM)
    cb = 512
    tab = pl.BlockSpec((tm, tm), lambda j, m, k: (m, k))
    dat = pl.BlockSpec((tm, cb), lambda j, m, k: (k, j))
    return pl.pallas_call(
        _hy_spec_kernel,
        grid=(N // cb, L // tm, L // tm),
        in_specs=[tab, tab, dat, dat],
        out_specs=[pl.BlockSpec((tm, cb), lambda j, m, k: (m, j)), pl.BlockSpec((tm, cb), lambda j, m, k: (m, j)),
                   pl.BlockSpec((SUBLANES, cb), lambda j, m, k: (0, j))],
        out_shape=[jax.ShapeDtypeStruct((L, N), F32), jax.ShapeDtypeStruct((L, N), F32),
                   jax.ShapeDtypeStruct((SUBLANES, N), F32)],
        scratch_shapes=[pltpu.VMEM((tm, cb), F32), pltpu.VMEM((tm, cb), F32), pltpu.VMEM((1, cb), F32)],
        compiler_params=pltpu.CompilerParams(dimension_semantics=("arbitrary",) * 3, vmem_limit_bytes=VMEM_LIMIT_BYTES),
        name="hyena_spectrum",
    )(*tables, tap_sum, tap_dif)


def _hy_fwd_kernel(ch_ref, sh_ref, u_ref, hr_ref, hi_ref, hny_ref, yr_ref, yi_ref, yny_ref,
                   acc_c, acc_s, acc_n):
    m, k = pl.program_id(1), pl.program_id(2)
    nb, tk = u_ref.shape[0], u_ref.shape[1]
    tm = acc_c.shape[1]

    @pl.when(k == 0)
    def _():
        acc_c[...] = jnp.zeros_like(acc_c)
        acc_s[...] = jnp.zeros_like(acc_s)

    @pl.when(jnp.logical_and(k == 0, m == 0))
    def _():
        acc_n[...] = jnp.zeros_like(acc_n)

    sign = _alt_sign(tk, k * tk)
    for b in range(nb):
        ub = u_ref[b]
        acc_c[b] += _dot(ch_ref[...], ub)
        acc_s[b] += _dot(sh_ref[...], ub)

        @pl.when(m == 0)
        def _():
            acc_n[b] += jnp.sum(ub.astype(F32) * sign, axis=0, keepdims=True)

    @pl.when(k == pl.num_programs(2) - 1)
    def _():
        f = lax.broadcasted_iota(jnp.int32, (tm, 1), 0) + m * tm
        dc = jnp.where(f == 0, 0.5, 1.0).astype(F32)
        hr, hi = hr_ref[...], hi_ref[...]
        for b in range(nb):
            xr, xs = acc_c[b], acc_s[b]
            yr_ref[b] = ((xr * hr + xs * hi) * dc).astype(yr_ref.dtype)
            yi_ref[b] = (xr * hi - xs * hr).astype(yi_ref.dtype)
            yny_ref[b] = jnp.broadcast_to(acc_n[b] * hny_ref[0:1, :], yny_ref.shape[1:])


def _hy_inv_kernel(ch_ref, sh_ref, yr_ref, yi_ref, yny_ref, u_ref, xg_ref, skip_ref, *rest):
    o_refs, acc = rest[:-1], rest[-1]
    m, k = pl.program_id(1), pl.program_id(2)
    nb = yr_ref.shape[0]
    tm = acc.shape[1]
    L = tm * pl.num_programs(1)

    @pl.when(k == 0)
    def _():
        acc[...] = jnp.zeros_like(acc)

    for b in range(nb):
        acc[b] += _dot(ch_ref[...], yr_ref[b]) - _dot(sh_ref[...], yi_ref[b])

    @pl.when(k == pl.num_programs(2) - 1)
    def _():
        sign = _alt_sign(tm, m * tm)
        for b in range(nb):
            u = u_ref[b]
            y = acc[b] * (1.0 / L) + sign * yny_ref[b, 0:1, :] * (0.5 / L)
            res = xg_ref[b] * (y + u * skip_ref[...])
            for o_ref in o_refs:
                o_ref[b] = res.astype(o_ref.dtype)


def hyena_long_conv(tables, zf, u, u16, u_col, gate_col, hr, hi, hny, order, skip, out_dtypes):
    B, L, _ = u.shape
    C = HY_CH
    tm = min(L, HY_TM)
    tk = min(L, HY_TK)
    bg = min(B, HY_BG_LONG if L >= HY_TM else HY_BG)
    assert B % bg == 0
    grid = (B // bg, L // tm, L // tk)
    tab = pl.BlockSpec((tm, tk), lambda g, m, k: (m, k))
    params = pltpu.CompilerParams(dimension_semantics=("arbitrary",) * 3, vmem_limit_bytes=VMEM_LIMIT_BYTES)
    spec_m = pl.BlockSpec((tm, C), lambda g, m, k: (m, order))
    yr, yi, yny = pl.pallas_call(
        _hy_fwd_kernel,
        grid=grid,
        in_specs=[tab, tab, pl.BlockSpec((bg, tk, C), lambda g, m, k: (g, k, 0)), spec_m, spec_m,
                  pl.BlockSpec((SUBLANES, C), lambda g, m, k: (0, order))],
        out_specs=[pl.BlockSpec((bg, tm, C), lambda g, m, k: (g, m, 0)), pl.BlockSpec((bg, tm, C), lambda g, m, k: (g, m, 0)),
                   pl.BlockSpec((bg, SUBLANES, C), lambda g, m, k: (g, 0, 0))],
        out_shape=[jax.ShapeDtypeStruct((B, L, C), BF16), jax.ShapeDtypeStruct((B, L, C), BF16),
                   jax.ShapeDtypeStruct((B, SUBLANES, C), F32)],
        scratch_shapes=[pltpu.VMEM((bg, tm, C), F32), pltpu.VMEM((bg, tm, C), F32), pltpu.VMEM((bg, 1, C), F32)],
        compiler_params=params,
        name="hyena_fwd",
    )(*tables, u16, hr, hi, hny)
    return pl.pallas_call(
        _hy_inv_kernel,
        grid=grid,
        in_specs=[tab, tab, pl.BlockSpec((bg, tk, C), lambda g, m, k: (g, k, 0)),
                  pl.BlockSpec((bg, tk, C), lambda g, m, k: (g, k, 0)),
                  pl.BlockSpec((bg, SUBLANES, C), lambda g, m, k: (g, 0, 0)),
                  pl.BlockSpec((bg, tm, C), lambda g, m, k: (g, m, u_col)),
                  pl.BlockSpec((bg, tm, C), lambda g, m, k: (g, m, gate_col)),
                  pl.BlockSpec((1, C), lambda g, m, k: (0, 0))],
        out_specs=[pl.BlockSpec((bg, tm, C), lambda g, m, k: (g, m, 0)) for _ in out_dtypes],
        out_shape=[jax.ShapeDtypeStruct((B, L, C), dt) for dt in out_dtypes],
        scratch_shapes=[pltpu.VMEM((bg, tm, C), F32)],
        compiler_params=params,
        name="hyena_inv",
    )(*tables, yr, yi, yny, u, zf, skip.astype(F32)[order][None, :])


def hyena_filter_spectrum(L, w1, b1, w2, b2, w3, freq):
    tables = dft_tables(L)
    tap_sum, tap_dif = hyena_taps(L, w1, b1, w2, b2, w3, freq)
    return tables, hyena_spectrum(tables, tap_sum, tap_dif)


def hyena_mixer_pallas(zh, conv_w, conv_b, skip, tables, spectrum):
    hr, hi, hny = spectrum
    zf, v16 = hyena_prep(zh, conv_w, conv_b)
    y1, y1_16 = hyena_long_conv(tables, zf, zf, v16, 0, 1, hr, hi, hny, 0, skip, [F32, BF16])
    return hyena_long_conv(tables, zf, y1, y1_16, 0, 2, hr, hi, hny, 1, skip, [BF16])[0]


def rms_norm(x, g):
    xf = x.astype(jnp.float32)
    y = xf * lax.rsqrt(jnp.mean(xf * xf, axis=-1, keepdims=True) + EPS)
    return (y * g.astype(jnp.float32)).astype(x.dtype)


def _ada_kernel(c_ref, w_ref, b_ref, o_ref):
    cnd = c_ref[...]
    act = cnd * jax.nn.sigmoid(cnd)
    o_ref[0] = jnp.dot(act, w_ref[0], precision=HIGHEST, preferred_element_type=F32) + b_ref[0]


def ada_params_all(cond, w_ada, b_ada):
    N, D = cond.shape
    depth, _, W = w_ada.shape
    rows = -(-N // SUBLANES) * SUBLANES
    cond_p = jnp.concatenate([cond.astype(F32), jnp.zeros((rows - N, D), F32)], axis=0)
    cb = D
    out = pl.pallas_call(
        _ada_kernel,
        grid=(depth, W // cb),
        in_specs=[pl.BlockSpec((rows, D), lambda l, j: (0, 0)), pl.BlockSpec((1, D, cb), lambda l, j: (l, 0, j)),
                  pl.BlockSpec((1, 1, cb), lambda l, j: (l, 0, j))],
        out_specs=pl.BlockSpec((1, rows, cb), lambda l, j: (l, 0, j)),
        out_shape=jax.ShapeDtypeStruct((depth, rows, W), F32),
        compiler_params=pltpu.CompilerParams(dimension_semantics=("arbitrary", "arbitrary"),
                                             vmem_limit_bytes=VMEM_LIMIT_BYTES),
        name="ada_params",
    )(cond_p, w_ada.astype(F32), b_ada.astype(F32)[:, None, :])
    return out[:, :N]


def _final_norm_kernel(x_ref, g_ref, o_ref):
    x = x_ref[...]
    o_ref[...] = x * lax.rsqrt(jnp.mean(x * x, axis=-1, keepdims=True) + EPS) * g_ref[...]


def final_norm(x, g, tm=512):
    B, L, D = x.shape
    M = B * L
    assert M % tm == 0
    out = pl.pallas_call(
        _final_norm_kernel,
        grid=(M // tm,),
        in_specs=[pl.BlockSpec((tm, D), lambda i: (i, 0)), pl.BlockSpec((1, D), lambda i: (0, 0))],
        out_specs=pl.BlockSpec((tm, D), lambda i: (i, 0)),
        out_shape=jax.ShapeDtypeStruct((M, D), F32),
        compiler_params=pltpu.CompilerParams(dimension_semantics=("arbitrary",), vmem_limit_bytes=VMEM_LIMIT_BYTES),
        name="final_norm",
    )(x.reshape(M, D), g.astype(F32)[None, :])
    return out.reshape(B, L, D)


def ada_params(cond, w, b):
    m = jax.nn.silu(cond) @ w + b
    return jnp.split(m[:, None, :], 6, axis=-1)


def modulate(x, g, shift, scale):
    return rms_norm(x, g) * (1 + scale) + shift


def axial_rope(T):
    t = jnp.arange(T)
    n_freq = HEAD_DIM // 4
    inv = ROPE_BASE ** (-jnp.arange(n_freq, dtype=jnp.float32) / n_freq)
    ang = jnp.concatenate([(t // GRID_W).astype(jnp.float32)[:, None] * inv,
                           (t % GRID_W).astype(jnp.float32)[:, None] * inv], axis=-1)
    return jnp.cos(ang)[:, None, :], jnp.sin(ang)[:, None, :]


def apply_rope(x, cos, sin):
    xf = x.astype(jnp.float32)
    x1, x2 = jnp.split(xf, 2, axis=-1)
    return jnp.concatenate([x1 * cos - x2 * sin, x2 * cos + x1 * sin], axis=-1).astype(x.dtype)


def softmax_parts(parts, sink=None):
    sizes = [p.shape[-1] for p in parts]
    cols = list(parts)
    if sink is not None:
        cols.append(jnp.broadcast_to(sink, parts[0].shape[:-1] + (1,)))
    p = jax.nn.softmax(jnp.concatenate(cols, axis=-1), axis=-1)
    pieces = jnp.split(p, np.cumsum(sizes).tolist(), axis=-1)
    return pieces[:len(sizes)]


def context_attention(q, k, v, sink):
    B, S, HQ, hd = q.shape
    HK = k.shape[2]
    G = HQ // HK
    nb = S // Q_BLOCK
    scale = hd ** -0.5
    sink_b = None if sink is None else sink.astype(jnp.float32).reshape(1, HK, G, 1, 1)
    qb = q.reshape(B, nb, Q_BLOCK, HK, G, hd).swapaxes(0, 1)

    def one(qi):
        s = jnp.einsum('bqkgd,bskd->bkgqs', qi, k, preferred_element_type=jnp.float32) * scale
        (p,) = softmax_parts([s], sink_b)
        return jnp.einsum('bkgqs,bskd->bqkgd', p.astype(v.dtype), v)

    o = lax.map(one, qb)
    return o.swapaxes(0, 1).reshape(B, S, HQ, hd)


def window_attention(q, k, v, ck, cv, sink):
    B, T, HQ, hd = q.shape
    HK = k.shape[2]
    G = HQ // HK
    nb = T // A_BLOCK
    scale = hd ** -0.5
    qb = q.reshape(B, nb, A_BLOCK, HK, G, hd).swapaxes(0, 1)

    def band(x):
        xp = jnp.pad(x, ((0, 0), (A_BLOCK, A_BLOCK), (0, 0), (0, 0))).reshape(B, nb + 2, A_BLOCK, HK, hd)
        return jnp.concatenate([xp[:, :-2], xp[:, 1:-1], xp[:, 2:]], axis=2).swapaxes(0, 1)

    kb, vb = band(k), band(v)
    qpos = jnp.arange(nb)[:, None, None] * A_BLOCK + jnp.arange(A_BLOCK)[None, :, None]
    kpos = jnp.arange(nb)[:, None, None] * A_BLOCK - A_BLOCK + jnp.arange(3 * A_BLOCK)[None, None, :]
    mask = (jnp.abs(kpos - qpos) <= A_WINDOW) & (kpos >= 0) & (kpos < T)
    sink_b = sink.astype(jnp.float32).reshape(1, HK, G, 1, 1)

    def one(xs):
        qi, ki, vi, mi = xs
        s_loc = jnp.einsum('bqkgd,bskd->bkgqs', qi, ki, preferred_element_type=jnp.float32) * scale
        s_loc = jnp.where(mi[None, None, None], s_loc, NEG_INF)
        s_ctx = jnp.einsum('bqkgd,bpkd->bkgqp', qi, ck, preferred_element_type=jnp.float32) * scale
        p_loc, p_ctx = softmax_parts([s_loc, s_ctx], sink_b)
        return (jnp.einsum('bkgqs,bskd->bqkgd', p_loc.astype(vi.dtype), vi)
                + jnp.einsum('bkgqp,bpkd->bqkgd', p_ctx.astype(cv.dtype), cv))

    o = lax.map(one, (qb, kb, vb, mask))
    return o.swapaxes(0, 1).reshape(B, T, HQ, hd)


def short_conv(x, w):
    K = w.shape[0]
    L = x.shape[1]
    pad = K // 2
    xp = jnp.pad(x, ((0, 0), (pad, pad), (0, 0)))
    return sum(xp[:, i:i + L] * w[i] for i in range(K))


def hyena_filter_bank(L, w1, b1, w2, b2, w3, freq):
    f32 = jnp.float32
    t = jnp.linspace(0.0, 1.0, L, dtype=f32)[:, None]
    bands = (HY_EMB - 1) // 2
    omega = 2.0 * math.pi * jnp.arange(L, dtype=f32)[:, None] / L
    fb = jnp.linspace(1e-4, bands - 1, bands, dtype=f32)[None, :]
    feats = jnp.concatenate([t, jnp.cos(fb * omega), -jnp.sin(fb * omega)], axis=-1)
    fr = freq.astype(f32)
    h = jnp.sin(fr * (feats @ w1.astype(f32) + b1.astype(f32)))
    h = jnp.sin(fr * (h @ w2.astype(f32) + b2.astype(f32)))
    h = (h @ w3.astype(f32)).reshape(L, 2, HY_ORDER, HY_CH)
    max_decay = math.log(HY_DECAY_TARGET) / HY_FAST_DECAY
    min_decay = math.log(HY_DECAY_TARGET) / HY_SLOW_DECAY
    deltas = jnp.abs(jnp.linspace(min_decay, max_decay, HY_CH, dtype=f32))
    h = h * (jnp.exp(-t * deltas) + HY_SHIFT)[:, None, None, :]
    taps = jnp.concatenate([h[:, 0], jnp.zeros((1, HY_ORDER, HY_CH), f32), h[:0:-1, 1]], axis=0)
    taps = taps / jnp.sum(jnp.abs(taps), axis=0, keepdims=True)
    return jnp.fft.rfft(taps, axis=0)


def hyena_mixer(z, conv_w, conv_b, w1, b1, w2, b2, w3, freq, skip):
    L = z.shape[1]
    zf = (short_conv(z, conv_w) + conv_b).astype(jnp.float32)
    v, x1, x2 = jnp.split(zf, 3, axis=-1)
    filt = hyena_filter_bank(L, w1, b1, w2, b2, w3, freq)
    skip = skip.astype(jnp.float32)

    def long_conv(u, o):
        y = jnp.fft.irfft(jnp.fft.rfft(u, n=2 * L, axis=1) * filt[None, :, o], n=2 * L, axis=1)[:, :L]
        return y + u * skip[o]

    y = x1 * long_conv(v, 0)
    y = x2 * long_conv(y, 1)
    return y.astype(z.dtype)


def l2norm(x):
    xf = x.astype(jnp.float32)
    return xf * lax.rsqrt(jnp.sum(xf * xf, axis=-1, keepdims=True) + EPS)


def chunk_gated_delta(q, k, v, g, beta, s0):
    B, L, H, dk = q.shape
    dv = v.shape[-1]
    n = L // C_CHUNK

    def chunks(x):
        x = x.reshape((B, n, C_CHUNK, H) + x.shape[3:])
        return jnp.moveaxis(jnp.moveaxis(x, 1, 0), 3, 2)

    qc, kc, vc, bc = chunks(q), chunks(k), chunks(v), chunks(beta)
    gc = jnp.cumsum(chunks(g), axis=-1)
    tri = jnp.tril(jnp.ones((C_CHUNK, C_CHUNK), bool))
    strict = jnp.tril(jnp.ones((C_CHUNK, C_CHUNK), bool), k=-1)
    gamma = jnp.exp(jnp.where(tri, gc[..., :, None] - gc[..., None, :], NEG_INF))
    kb = kc * bc[..., None]
    a_mat = jnp.where(strict, jnp.einsum('nbhid,nbhjd->nbhij', kb, kc) * gamma, 0.0) + jnp.eye(C_CHUNK, dtype=jnp.float32)
    rhs = jnp.concatenate([vc * bc[..., None], kb * jnp.exp(gc)[..., None]], axis=-1)
    sol = lax.linalg.triangular_solve(a_mat, rhs, left_side=True, lower=True)
    u, w = sol[..., :dv], sol[..., dv:]
    attn = jnp.where(tri, jnp.einsum('nbhid,nbhjd->nbhij', qc, kc) * gamma, 0.0)
    g_last = gc[..., -1]
    q_dec = qc * jnp.exp(gc)[..., None]
    k_dec = kc * jnp.exp(g_last[..., None] - gc)[..., None]

    def step(S, xs):
        u_i, w_i, a_i, qd, kd, gl = xs
        v_new = u_i - jnp.einsum('bhck,bhkv->bhcv', w_i, S)
        o = jnp.einsum('bhck,bhkv->bhcv', qd, S) + jnp.einsum('bhij,bhjv->bhiv', a_i, v_new)
        S = S * jnp.exp(gl)[..., None, None] + jnp.einsum('bhck,bhcv->bhkv', kd, v_new)
        return S, o

    S, o = lax.scan(step, s0.astype(jnp.float32), (u, w, attn, q_dec, k_dec, g_last))
    o = jnp.moveaxis(jnp.moveaxis(o, 2, 3), 0, 1).reshape(B, L, H, dv)
    return o, S


def deltanet_mixer(zq, zk, zv, zg, za, zb, conv_w, a_log, dt_bias, norm_w, s0):
    B, L, _ = zq.shape
    qkv = jax.nn.silu(short_conv(jnp.concatenate([zq, zk, zv], axis=-1), conv_w))
    q, k, v = [t.reshape(B, L, C_HEADS, HEAD_DIM) for t in jnp.split(qkv, 3, axis=-1)]
    q = l2norm(q) * (HEAD_DIM ** -0.5)
    k = l2norm(k)
    v = v.astype(jnp.float32)
    beta = jax.nn.sigmoid(zb.astype(jnp.float32))
    g = -jnp.exp(a_log.astype(jnp.float32)) * jax.nn.softplus(za.astype(jnp.float32) + dt_bias.astype(jnp.float32))
    o_f, s_f = chunk_gated_delta(q, k, v, g[:, :, 0], beta[:, :, 0], s0[:, 0])
    o_b, s_b = chunk_gated_delta(q[:, ::-1], k[:, ::-1], v[:, ::-1], g[:, ::-1, 1], beta[:, ::-1, 1], s0[:, 1])
    o = o_f + o_b[:, ::-1]
    gate = jax.nn.silu(zg.reshape(B, L, C_HEADS, HEAD_DIM).astype(jnp.float32))
    o = rms_norm(o, norm_w) * gate
    return o.reshape(B, L, C_W).astype(zq.dtype), jnp.stack([s_f, s_b], axis=1)


def neighbourhood_attention(q, k, v, ck, cv, rpb):
    B, T, H, hd = q.shape
    rows = T // GRID_W
    kh = min(NA_KH_MAX, rows)
    scale = hd ** -0.5
    r = jnp.arange(rows)
    rs = jnp.clip(r - kh // 2, 0, rows - kh)
    key_rows = rs[:, None] + jnp.arange(kh)[None, :]
    idx = (key_rows[:, :, None] * GRID_W + jnp.arange(GRID_W)).reshape(rows, kh * GRID_W)
    col = jnp.arange(GRID_W)
    cs = jnp.clip(col - NA_KW // 2, 0, GRID_W - NA_KW)
    kcol = jnp.tile(col, kh)
    col_ok = (kcol[None, :] >= cs[:, None]) & (kcol[None, :] < cs[:, None] + NA_KW)
    roff = jnp.repeat(key_rows - r[:, None], GRID_W, axis=1) + NA_KH_MAX - 1
    coff = jnp.clip(kcol[None, :] - col[:, None] + NA_KW - 1, 0, 2 * NA_KW - 2)
    qr = q.reshape(B, rows, GRID_W, H, hd).swapaxes(0, 1)
    rpb_f = rpb.astype(jnp.float32)

    def one(xs):
        qi, ii, ro = xs
        ki = k[:, ii]
        vi = v[:, ii]
        bias = rpb_f[:, ro[None, :], coff]
        s_loc = jnp.einsum('bqhd,bkhd->bhqk', qi, ki, preferred_element_type=jnp.float32) * scale + bias[None]
        s_loc = jnp.where(col_ok[None, None], s_loc, NEG_INF)
        s_ctx = jnp.einsum('bqhd,bphd->bhqp', qi, ck, preferred_element_type=jnp.float32) * scale
        p_loc, p_ctx = softmax_parts([s_loc, s_ctx])
        return (jnp.einsum('bhqk,bkhd->bqhd', p_loc.astype(vi.dtype), vi)
                + jnp.einsum('bhqp,bphd->bqhd', p_ctx.astype(cv.dtype), cv))

    o = lax.map(one, (qr, idx, roff))
    return o.swapaxes(0, 1).reshape(B, T, H, hd)


def expert_choice_ffn(h, w_router, w_gate, w_up, w_down):
    B, T, D = h.shape
    cap = EC_CAPACITY * T // N_EXPERTS
    aff = jax.nn.softmax(jnp.einsum('btd,de->bte', h, w_router, preferred_element_type=jnp.float32), axis=-1)
    gate, idx = lax.top_k(aff.swapaxes(1, 2), cap)
    xg = jax.vmap(lambda hb, ib: hb[ib])(h, idx)
    a = jnp.einsum('becd,edf->becf', xg, w_gate)
    u = jnp.einsum('becd,edf->becf', xg, w_up)
    y = jnp.einsum('becf,efd->becd', jax.nn.silu(a) * u, w_down) * gate[..., None].astype(h.dtype)
    return jax.vmap(lambda yb, ib: jnp.zeros((T, D), yb.dtype).at[ib.reshape(-1)].add(yb.reshape(-1, D)))(y, idx)


def split_even(z):
    B, L = z.shape[:2]
    q = z[..., :A_Q_W].reshape(B, L, A_HEADS, HEAD_DIM)
    k = z[..., A_Q_W:A_Q_W + A_KV_W].reshape(B, L, A_KV_HEADS, HEAD_DIM)
    v = z[..., A_Q_W + A_KV_W:A_Q_W + 2 * A_KV_W].reshape(B, L, A_KV_HEADS, HEAD_DIM)
    return q, k, v, z[..., A_Q_W + 2 * A_KV_W:]


def split_odd(z):
    B, L = z.shape[:2]
    zq, zk, zv, zg = [z[..., i * C_W:(i + 1) * C_W] for i in range(4)]
    off = 4 * C_W
    za = z[..., off:off + 2 * C_HEADS].reshape(B, L, 2, C_HEADS)
    zb = z[..., off + 2 * C_HEADS:off + 4 * C_HEADS].reshape(B, L, 2, C_HEADS)
    off = off + 4 * C_HEADS
    nq, nk, nv = [z[..., off + i * D_W:off + (i + 1) * D_W].reshape(B, L, D_HEADS, HEAD_DIM) for i in range(3)]
    return zq, zk, zv, zg, za, zb, nq, nk, nv


def kernel(x_prompt, x_sample, cache_attn_k, cache_attn_v, state_delta, cache_na_k, cache_na_v,
           c, c_ctx, w_ada, b_ada, norm_mix, norm_ffn, norm_final,
           even_w_in, even_w_out, attn_sink, hy_conv_w, hy_conv_b, hy_w1, hy_b1, hy_w2, hy_b2,
           hy_w3, hy_freq, hy_skip, odd_w_in, odd_w_out, gdn_conv_w, gdn_a_log, gdn_dt_bias,
           gdn_norm, na_rpb, moe_router, moe_w_gate, moe_w_up, moe_w_down):
    xp, xs = x_prompt, x_sample
    bp = xp.shape[0]
    dft_p, dft_s = dft_tables(xp.shape[1]), dft_tables(xs.shape[1])
    ada = ada_params_all(jnp.concatenate([c_ctx[None, :], c], axis=0), w_ada, b_ada)
    new_ak, new_av, new_st, new_nk, new_nv = [], [], [], [], []
    for l in range(DEPTH):
        j = l // 2
        mp = jnp.split(ada[l, :1, None, :], 6, axis=-1)
        ms = jnp.split(ada[l, 1:, None, :], 6, axis=-1)
        mod_p = (norm_mix[l], mp[0], mp[1])
        mod_s = (norm_mix[l], ms[0], ms[1])
        if l % 2 == 0:
            def hyena(zh, tables):
                taps = hyena_taps(zh.shape[1], hy_w1[j], hy_b1[j], hy_w2[j], hy_b2[j], hy_w3[j], hy_freq[j])
                return hyena_mixer_pallas(zh, hy_conv_w[j], hy_conv_b[j], hy_skip[j], tables, hyena_spectrum(tables, *taps))

            w_in = even_w_in[j]
            w_groups = [w_in[:, :A_Q_W], w_in[:, A_Q_W:A_Q_W + A_KV_W], w_in[:, A_Q_W + A_KV_W:A_Q_W + 2 * A_KV_W],
                        w_in[:, A_Q_W + 2 * A_KV_W:]]
            q, k, v, zh = proj_multi(xp, *mod_p, w_groups, [F32] * 4)
            oa = context_attention_pallas(q, k, v, attn_sink[j], A_HEADS, A_KV_HEADS)
            xp_new = proj_concat(oa, hyena(zh, dft_p), even_w_out[j], xp, mp[2])
            new_ak.append(k.reshape(bp, SEQ, A_KV_HEADS, HEAD_DIM))
            new_av.append(v.reshape(bp, SEQ, A_KV_HEADS, HEAD_DIM))
            q, k, v, zh = proj_multi(xs, *mod_s, w_groups, [F32] * 4)
            ck = cache_attn_k[:, j].reshape(DEC_BATCH, PAST_LEN, A_KV_W).astype(BF16)
            cv = cache_attn_v[:, j].reshape(DEC_BATCH, PAST_LEN, A_KV_W).astype(BF16)
            oa = window_attention_pallas(q, k, v, ck, cv, attn_sink[j])
            xs_new = proj_concat(oa, hyena(zh, dft_s), even_w_out[j], xs, ms[2])
        else:
            w_in = odd_w_in[j]
            ab0 = 4 * C_W
            ab_cols = [w_in[:, ab0 + o * C_HEADS:ab0 + (o + 1) * C_HEADS] for o in (0, 2, 1, 3)]
            w_ab = jnp.concatenate(ab_cols + [jnp.zeros((D_MODEL, LANES - 4 * C_HEADS), w_in.dtype)], axis=1)
            n0 = ab0 + 4 * C_HEADS
            w_groups = [w_in[:, :3 * C_W], w_in[:, 3 * C_W:4 * C_W], w_ab,
                        w_in[:, n0:n0 + D_W], w_in[:, n0 + D_W:n0 + 2 * D_W], w_in[:, n0 + 2 * D_W:]]

            def deltanet(zqkv, zab, zg, s0):
                qd, kd, vd, gb = gdn_prep(zqkv, zab, gdn_conv_w[j], gdn_a_log[j], gdn_dt_bias[j])
                return gdn_scan(qd, kd, vd, gb, zg, s0, gdn_norm[j])

            zqkv, zg, zab, nq, nk, nv = proj_multi(xp, *mod_p, w_groups, [F32] * 6)
            oc, st = deltanet(zqkv, zab, zg, jnp.zeros((bp, 2, C_HEADS, HEAD_DIM, HEAD_DIM), F32))
            od = context_attention_pallas(nq, nk, nv, None, D_HEADS, D_HEADS)
            xp_new = proj_concat(oc, od, odd_w_out[j], xp, mp[2])
            new_st.append(st)
            new_nk.append(nk.reshape(bp, SEQ, D_HEADS, HEAD_DIM))
            new_nv.append(nv.reshape(bp, SEQ, D_HEADS, HEAD_DIM))
            zqkv, zg, zab, nq, nk, nv = proj_multi(xs, *mod_s, w_groups, [F32, F32, F32, BF16, BF16, BF16])
            oc, _ = deltanet(zqkv, zab, zg, state_delta[:, j])
            ck = cache_na_k[:, j].reshape(DEC_BATCH, PAST_LEN, D_W).astype(BF16)
            cv = cache_na_v[:, j].reshape(DEC_BATCH, PAST_LEN, D_W).astype(BF16)
            od = neighbourhood_attention_pallas(nq, nk, nv, ck, cv, na_rpb[j])
            xs_new = proj_concat(oc, od, odd_w_out[j], xs, ms[2])
        xp, xs = xp_new, xs_new
        moe = (moe_router[l], moe_w_gate, moe_w_up, moe_w_down, l)
        xp = moe_block(xp, norm_ffn[l], mp[3], mp[4], mp[5], *moe)
        xs = moe_block(xs, norm_ffn[l], ms[3], ms[4], ms[5], *moe)
    y_prompt = final_norm(xp, norm_final)
    y_sample = final_norm(xs, norm_final)
    return (y_prompt, y_sample, jnp.stack(new_ak, axis=1), jnp.stack(new_av, axis=1), jnp.stack(new_st, axis=1),
            jnp.stack(new_nk, axis=1), jnp.stack(new_nv, axis=1))
```

```python
import functools
import math
import jax, jax.numpy as jnp
from jax import lax
import numpy as np
from jax.experimental import pallas as pl
from jax.experimental.pallas import tpu as pltpu

D_MODEL = 1024
BATCH = 32
SEQ = 256
DEPTH = 4
DEC_BATCH = 4
DEC_SEQ = 4096
PAST_LEN = 512

GRID_W = 64
HEAD_DIM = 64
N_EVEN = (DEPTH + 1) // 2
N_ODD = DEPTH // 2
Q_BLOCK = 128
A_HEADS = D_MODEL // 128
A_KV_HEADS = A_HEADS // 4
A_WINDOW = 128
A_BLOCK = 128
ROPE_BASE = 10000.0
HY_CH = D_MODEL // 2
HY_ORDER = 2
HY_SHORT = 3
HY_EMB = 33
HY_FILT_W = 64
HY_FAST_DECAY = 0.3
HY_SLOW_DECAY = 1.5
HY_DECAY_TARGET = 1e-2
HY_SHIFT = 0.05
C_HEADS = D_MODEL // 128
C_SHORT = 3
C_CHUNK = 64
D_HEADS = D_MODEL // 128
NA_KH_MAX = 8
NA_KW = 16
N_EXPERTS = 16
EC_CAPACITY = 2
MOE_D_FF = D_MODEL
EPS = 1e-6
NEG_INF = -1e30

A_Q_W = A_HEADS * HEAD_DIM
A_KV_W = A_KV_HEADS * HEAD_DIM
EVEN_IN = A_Q_W + 2 * A_KV_W + 3 * HY_CH
EVEN_MIX = A_Q_W + HY_CH
C_W = C_HEADS * HEAD_DIM
D_W = D_HEADS * HEAD_DIM
ODD_IN = 4 * C_W + 4 * C_HEADS + 3 * D_W
ODD_MIX = C_W + D_W

VMEM_LIMIT_BYTES = 48 * 1024 * 1024


def _mm_kernel(x_ref, w_ref, o_ref):
    o_ref[...] = jnp.dot(x_ref[...].astype(jnp.bfloat16), w_ref[...], preferred_element_type=jnp.float32)


def pallas_matmul(x, w, tm=256):
    M, K = x.shape
    N = w.shape[1]
    assert M % tm == 0
    return pl.pallas_call(
        _mm_kernel,
        grid=(M // tm,),
        in_specs=[pl.BlockSpec((tm, K), lambda i: (i, 0)), pl.BlockSpec((K, N), lambda i: (0, 0))],
        out_specs=pl.BlockSpec((tm, N), lambda i: (i, 0)),
        out_shape=jax.ShapeDtypeStruct((M, N), jnp.float32),
        compiler_params=pltpu.CompilerParams(dimension_semantics=("arbitrary",), vmem_limit_bytes=VMEM_LIMIT_BYTES),
    )(x, w.astype(jnp.bfloat16))


def proj(x, w):
    B, L, K = x.shape
    return pallas_matmul(x.reshape(B * L, K), w).reshape(B, L, w.shape[1])


PROJ_TM = 512


def _request_of_tile(tm, L, per_request):
    return (lambda i: ((i * tm) // L, 0, 0)) if per_request else (lambda i: (0, 0, 0))


def _mm_multi_kernel(x_ref, g_ref, shift_ref, scale_ref, *refs):
    n = len(refs) // 2
    x = x_ref[...]
    y = x * lax.rsqrt(jnp.mean(x * x, axis=-1, keepdims=True) + EPS) * g_ref[...]
    h = (y * (1.0 + scale_ref[0]) + shift_ref[0]).astype(jnp.bfloat16)
    for w_ref, o_ref in zip(refs[:n], refs[n:]):
        o_ref[...] = jnp.dot(h, w_ref[...], preferred_element_type=jnp.float32).astype(o_ref.dtype)


def proj_multi(x, g, shift, scale, weights, out_dtypes, tm=PROJ_TM):
    B, L, K = x.shape
    M = B * L
    per_request = shift.shape[0] == B
    assert L % tm == 0 if per_request else M % tm == 0
    mod_spec = pl.BlockSpec((1, 1, K), _request_of_tile(tm, L, per_request))
    outs = pl.pallas_call(
        _mm_multi_kernel,
        grid=(M // tm,),
        in_specs=[pl.BlockSpec((tm, K), lambda i: (i, 0)), pl.BlockSpec((1, K), lambda i: (0, 0)), mod_spec, mod_spec]
        + [pl.BlockSpec(w.shape, lambda i: (0, 0)) for w in weights],
        out_specs=[pl.BlockSpec((tm, w.shape[1]), lambda i: (i, 0)) for w in weights],
        out_shape=[jax.ShapeDtypeStruct((M, w.shape[1]), dt) for w, dt in zip(weights, out_dtypes)],
        compiler_params=pltpu.CompilerParams(dimension_semantics=("arbitrary",), vmem_limit_bytes=VMEM_LIMIT_BYTES),
        name="in_projection",
    )(x.reshape(M, K), g.astype(jnp.float32)[None, :], shift, scale, *[w.astype(jnp.bfloat16) for w in weights])
    return [o.reshape(B, L, o.shape[1]) for o in outs]


def _mm2_kernel(a_ref, b_ref, wa_ref, wb_ref, x_ref, gate_ref, o_ref):
    mix = (jnp.dot(a_ref[...].astype(jnp.bfloat16), wa_ref[...], preferred_element_type=jnp.float32)
           + jnp.dot(b_ref[...].astype(jnp.bfloat16), wb_ref[...], preferred_element_type=jnp.float32))
    o_ref[...] = x_ref[...] + gate_ref[0] * mix


def proj_concat(a, b, w, x, gate, tm=PROJ_TM):
    B, L, Ka = a.shape
    Kb = b.shape[2]
    N = w.shape[1]
    M = B * L
    assert (L % tm == 0 if gate.shape[0] == B else M % tm == 0) and w.shape[0] == Ka + Kb
    wb16 = w.astype(jnp.bfloat16)
    out = pl.pallas_call(
        _mm2_kernel,
        grid=(M // tm,),
        in_specs=[pl.BlockSpec((tm, Ka), lambda i: (i, 0)), pl.BlockSpec((tm, Kb), lambda i: (i, 0)),
                  pl.BlockSpec((Ka, N), lambda i: (0, 0)), pl.BlockSpec((Kb, N), lambda i: (0, 0)),
                  pl.BlockSpec((tm, N), lambda i: (i, 0)),
                  pl.BlockSpec((1, 1, N), _request_of_tile(tm, L, gate.shape[0] == B))],
        out_specs=pl.BlockSpec((tm, N), lambda i: (i, 0)),
        out_shape=jax.ShapeDtypeStruct((M, N), jnp.float32),
        compiler_params=pltpu.CompilerParams(dimension_semantics=("arbitrary",), vmem_limit_bytes=VMEM_LIMIT_BYTES),
        name="out_projection",
    )(a.reshape(M, Ka), b.reshape(M, Kb), wb16[:Ka], wb16[Ka:], x.reshape(M, N), gate)
    return out.reshape(B, L, N)


LANES = 128
BF16 = jnp.bfloat16
F32 = jnp.float32


def _dot_nt(a, b):
    return lax.dot_general(a, b, (((1,), (1,)), ((), ())), preferred_element_type=F32)


def _dot(a, b):
    return jnp.dot(a, b, preferred_element_type=F32)


def _low_half(shape):
    return lax.broadcasted_iota(jnp.int32, shape, 1) < HEAD_DIM


def _softmax_pv(units):
    n = range(len(units))
    m = [functools.reduce(jnp.maximum, [s.max(axis=-1, keepdims=True) for s in units[u][0]]) for u in n]
    m = [m[u] if units[u][2] is None else jnp.maximum(m[u], units[u][2]) for u in n]
    p = [[jnp.exp(s - m[u]) for s in units[u][0]] for u in n]
    l = [functools.reduce(lambda a, b: a + b, [x.sum(axis=-1, keepdims=True) for x in p[u]]) for u in n]
    l = [l[u] if units[u][2] is None else l[u] + jnp.exp(units[u][2] - m[u]) for u in n]
    o = [functools.reduce(lambda a, b: a + b, [_dot(x.astype(BF16), v) for x, v in zip(p[u], units[u][1])]) for u in n]
    return [o[u] / l[u] for u in n]


def _place_head(q_slab, src_half, dst_half, low):
    x = q_slab if src_half == dst_half else pltpu.roll(q_slab, HEAD_DIM, axis=1)
    return jnp.where(low if dst_half == 0 else ~low, x, jnp.zeros_like(x))


NA_KEYS = NA_KH_MAX * GRID_W


def _na_kernel(q_ref, k_ref, v_ref, ck_ref, cv_ref, tab_ref, o_ref):
    r = pl.program_id(1)
    rows = k_ref.shape[1] // GRID_W
    rs = jnp.clip(r - NA_KH_MAX // 2, 0, rows - NA_KH_MAX)
    start = pl.multiple_of(rs * GRID_W, GRID_W)
    scale = HEAD_DIM ** -0.5
    low = _low_half((GRID_W, LANES))
    pairs = range(D_HEADS // 2)
    cols = [slice(p * LANES, (p + 1) * LANES) for p in pairs]
    qp = [q_ref[0, :, c] for c in cols]
    kp = [k_ref[0, pl.ds(start, NA_KEYS), c] for c in cols]
    vp = [v_ref[0, pl.ds(start, NA_KEYS), c] for c in cols]
    ckp = [ck_ref[0, :, c] for c in cols]
    cvp = [cv_ref[0, :, c] for c in cols]
    heads = [(p, half) for p in pairs for half in range(2)]
    qm = [jnp.where(low if half == 0 else ~low, qp[p], jnp.zeros_like(qp[p])) for p, half in heads]
    s_loc = [_dot_nt(qm[i], kp[p]) * scale + tab_ref[i, 0] for i, (p, _) in enumerate(heads)]
    s_ctx = [_dot_nt(qm[i], ckp[p]) * scale for i, (p, _) in enumerate(heads)]
    outs = _softmax_pv([([s_loc[i], s_ctx[i]], [vp[p], cvp[p]], None) for i, (p, _) in enumerate(heads)])
    for p in pairs:
        o_ref[0, :, cols[p]] = jnp.where(low, outs[2 * p], outs[2 * p + 1]).astype(o_ref.dtype)


def na_bias_table(rpb):
    col = jnp.arange(GRID_W)
    cs = jnp.clip(col - NA_KW // 2, 0, GRID_W - NA_KW)
    col_ok = (col[None, :] >= cs[:, None]) & (col[None, :] < cs[:, None] + NA_KW)
    coff = jnp.clip(col[None, :] - col[:, None] + NA_KW - 1, 0, 2 * NA_KW - 2)
    base = jnp.where(col_ok[None, None], rpb.astype(F32)[:, :, coff], NEG_INF)
    tab = jnp.stack([base[:, o:o + NA_KH_MAX] for o in range(NA_KH_MAX)], axis=1)
    return tab.transpose(0, 1, 3, 2, 4).reshape(D_HEADS, NA_KH_MAX, GRID_W, NA_KEYS)


def neighbourhood_attention_pallas(q, k, v, ck, cv, rpb):
    B, T, W = q.shape
    P = ck.shape[1]
    rows = T // GRID_W
    assert rows >= NA_KH_MAX and W == D_W
    tab = na_bias_table(rpb)

    def tab_index(b, r):
        rs = jnp.clip(r - NA_KH_MAX // 2, 0, rows - NA_KH_MAX)
        return (0, rs - r + NA_KH_MAX - 1, 0, 0)

    return pl.pallas_call(
        _na_kernel,
        grid=(B, rows),
        in_specs=[
            pl.BlockSpec((1, GRID_W, W), lambda b, r: (b, r, 0)),
            pl.BlockSpec((1, T, W), lambda b, r: (b, 0, 0)),
            pl.BlockSpec((1, T, W), lambda b, r: (b, 0, 0)),
            pl.BlockSpec((1, P, W), lambda b, r: (b, 0, 0)),
            pl.BlockSpec((1, P, W), lambda b, r: (b, 0, 0)),
            pl.BlockSpec((D_HEADS, 1, GRID_W, NA_KEYS), tab_index),
        ],
        out_specs=pl.BlockSpec((1, GRID_W, W), lambda b, r: (b, r, 0)),
        out_shape=jax.ShapeDtypeStruct((B, T, W), BF16),
        compiler_params=pltpu.CompilerParams(dimension_semantics=("arbitrary", "arbitrary"),
                                             vmem_limit_bytes=VMEM_LIMIT_BYTES),
        name="na_attention",
    )(q, k, v, ck, cv, tab)


def rope_tables(T):
    cos, sin = axial_rope(T)
    cos, sin = cos[:, 0, :], sin[:, 0, :]
    cos_t = jnp.concatenate([cos, cos, cos, cos], axis=-1)
    sin_t = jnp.concatenate([-sin, sin, -sin, sin], axis=-1)
    return cos_t, sin_t


def _rope(x, cos_t, sin_t):
    half = HEAD_DIM // 2
    lane = lax.broadcasted_iota(jnp.int32, x.shape, 1)
    first = (lane % HEAD_DIM) < half
    swapped = jnp.where(first, pltpu.roll(x, LANES - half, axis=1), pltpu.roll(x, half, axis=1))
    return x * cos_t + swapped * sin_t


def _win_kernel(sink_ref, q_ref, k_ref, v_ref, ck_ref, cv_ref, cos_ref, sin_ref, o_ref):
    i = pl.program_id(1)
    T = k_ref.shape[1]
    span = 3 * A_BLOCK
    start = pl.multiple_of(jnp.clip((i - 1) * A_BLOCK, 0, T - span), A_BLOCK)
    delta = i * A_BLOCK - start
    q0 = pl.multiple_of(i * A_BLOCK, A_BLOCK)
    scale = HEAD_DIM ** -0.5
    kw = _rope(k_ref[0, pl.ds(start, span), :], cos_ref[pl.ds(start, span), :], sin_ref[pl.ds(start, span), :]).astype(BF16)
    vw = v_ref[0, pl.ds(start, span), :].astype(BF16)
    ck = ck_ref[0]
    cv = cv_ref[0]
    cos_q = cos_ref[pl.ds(q0, A_BLOCK), :]
    sin_q = sin_ref[pl.ds(q0, A_BLOCK), :]
    qi = lax.broadcasted_iota(jnp.int32, (A_BLOCK, span), 0)
    kj = lax.broadcasted_iota(jnp.int32, (A_BLOCK, span), 1)
    band = jnp.abs(kj - delta - qi) <= A_WINDOW
    low = _low_half((A_BLOCK, LANES))
    group = A_HEADS // A_KV_HEADS
    pairs = range(A_HEADS // 2)
    cols = [slice(p * LANES, (p + 1) * LANES) for p in pairs]
    q_slab = [_rope(q_ref[0, :, c], cos_q, sin_q) for c in cols]
    heads = range(A_HEADS)
    kv_of = [h // group for h in heads]
    qm = [_place_head(q_slab[h // 2], h % 2, kv_of[h], low).astype(BF16) for h in heads]
    s_loc = [jnp.where(band, _dot_nt(qm[h], kw) * scale, NEG_INF) for h in heads]
    s_ctx = [_dot_nt(qm[h], ck) * scale for h in heads]
    outs = _softmax_pv([([s_loc[h], s_ctx[h]], [vw, cv], sink_ref[h]) for h in heads])
    outs = [outs[h] if kv_of[h] == h % 2 else pltpu.roll(outs[h], HEAD_DIM, axis=1) for h in heads]
    for p in pairs:
        o_ref[0, :, cols[p]] = jnp.where(low, outs[2 * p], outs[2 * p + 1]).astype(o_ref.dtype)


def window_attention_pallas(q, k, v, ck, cv, sink):
    B, T, QW = q.shape
    KW = k.shape[2]
    P = ck.shape[1]
    assert KW == LANES and QW == A_Q_W and T % A_BLOCK == 0 and T >= 3 * A_BLOCK
    cos_t, sin_t = rope_tables(T)
    return pl.pallas_call(
        _win_kernel,
        grid=(B, T // A_BLOCK),
        in_specs=[
            pl.BlockSpec(memory_space=pltpu.SMEM),
            pl.BlockSpec((1, A_BLOCK, QW), lambda b, i: (b, i, 0)),
            pl.BlockSpec((1, T, KW), lambda b, i: (b, 0, 0)),
            pl.BlockSpec((1, T, KW), lambda b, i: (b, 0, 0)),
            pl.BlockSpec((1, P, KW), lambda b, i: (b, 0, 0)),
            pl.BlockSpec((1, P, KW), lambda b, i: (b, 0, 0)),
            pl.BlockSpec((T, LANES), lambda b, i: (0, 0)),
            pl.BlockSpec((T, LANES), lambda b, i: (0, 0)),
        ],
        out_specs=pl.BlockSpec((1, A_BLOCK, QW), lambda b, i: (b, i, 0)),
        out_shape=jax.ShapeDtypeStruct((B, T, QW), BF16),
        compiler_params=pltpu.CompilerParams(dimension_semantics=("arbitrary", "arbitrary"),
                                             vmem_limit_bytes=VMEM_LIMIT_BYTES),
        name="window_attention",
    )(sink.astype(F32), q, k, v, ck, cv, cos_t, sin_t)


def _ctx_kernel(sink_ref, q_ref, k_ref, v_ref, o_ref, *, n_q_heads, n_kv_heads, use_sink):
    S = q_ref.shape[1]
    scale = HEAD_DIM ** -0.5
    low = _low_half((S, LANES))
    group = n_q_heads // n_kv_heads
    pairs = range(n_q_heads // 2)
    cols = [slice(p * LANES, (p + 1) * LANES) for p in pairs]
    q_slab = [q_ref[0, :, c] for c in cols]
    heads = range(n_q_heads)
    kv_of = [h // group for h in heads]
    kcols = [slice((kv // 2) * LANES, (kv // 2 + 1) * LANES) for kv in kv_of]
    qm = [_place_head(q_slab[h // 2], h % 2, kv_of[h] % 2, low).astype(BF16) for h in heads]
    s = [_dot_nt(qm[h], k_ref[0, :, kcols[h]].astype(BF16)) * scale for h in heads]
    outs = _softmax_pv([([s[h]], [v_ref[0, :, kcols[h]].astype(BF16)], sink_ref[h] if use_sink else None) for h in heads])
    outs = [outs[h] if kv_of[h] % 2 == h % 2 else pltpu.roll(outs[h], HEAD_DIM, axis=1) for h in heads]
    for p in pairs:
        o_ref[0, :, cols[p]] = jnp.where(low, outs[2 * p], outs[2 * p + 1]).astype(o_ref.dtype)


def context_attention_pallas(q, k, v, sink, n_q_heads, n_kv_heads):
    B, S, QW = q.shape
    KW = k.shape[2]
    use_sink = sink is not None
    sink_arr = sink.astype(F32) if use_sink else jnp.zeros((n_q_heads,), F32)
    return pl.pallas_call(
        functools.partial(_ctx_kernel, n_q_heads=n_q_heads, n_kv_heads=n_kv_heads, use_sink=use_sink),
        grid=(B,),
        in_specs=[
            pl.BlockSpec(memory_space=pltpu.SMEM),
            pl.BlockSpec((1, S, QW), lambda b: (b, 0, 0)),
            pl.BlockSpec((1, S, KW), lambda b: (b, 0, 0)),
            pl.BlockSpec((1, S, KW), lambda b: (b, 0, 0)),
        ],
        out_specs=pl.BlockSpec((1, S, QW), lambda b: (b, 0, 0)),
        out_shape=jax.ShapeDtypeStruct((B, S, QW), BF16),
        compiler_params=pltpu.CompilerParams(dimension_semantics=("arbitrary",), vmem_limit_bytes=VMEM_LIMIT_BYTES),
        name="context_attention",
    )(sink_arr, q, k, v)


HIGHEST = lax.Precision.HIGHEST
GDN_TM = 256
GDN_REQS = 2
GDN_SUB = 4
SUBLANES = 8


def _head_pair_sum_matrix():
    a = lax.broadcasted_iota(jnp.int32, (LANES, LANES), 0) // HEAD_DIM
    b = lax.broadcasted_iota(jnp.int32, (LANES, LANES), 1) // HEAD_DIM
    return (a == b).astype(BF16)


def _head_sums(x, pmat):
    hi = x.astype(BF16)
    lo = (x - hi.astype(F32)).astype(BF16)
    return _dot(hi, pmat) + _dot(lo, pmat)


def _gdn_prep_kernel(x_ref, prev_ref, next_ref, ab_ref, cw_ref, a_ref, dtb_ref, q_ref, k_ref, v_ref, gb_ref):
    i = pl.program_id(1)
    n = pl.num_programs(1)
    x = x_ref[0]
    tm = x.shape[0]
    row = lax.broadcasted_iota(jnp.int32, x.shape, 0)
    prev_row = jnp.where(i > 0, prev_ref[0, SUBLANES - 1:SUBLANES, :], 0.0)
    next_row = jnp.where(i < n - 1, next_ref[0, 0:1, :], 0.0)
    x_prev = jnp.where(row == 0, prev_row, pltpu.roll(x, 1, axis=0))
    x_next = jnp.where(row == tm - 1, next_row, pltpu.roll(x, tm - 1, axis=0))
    y = x_prev * cw_ref[0:1, :] + x * cw_ref[1:2, :] + x_next * cw_ref[2:3, :]
    y = y * jax.nn.sigmoid(y)
    pmat = _head_pair_sum_matrix()
    for p in range(C_W // LANES):
        qs = y[:, p * LANES:(p + 1) * LANES]
        ks = y[:, C_W + p * LANES:C_W + (p + 1) * LANES]
        q_ref[0, :, p * LANES:(p + 1) * LANES] = qs * lax.rsqrt(_head_sums(qs * qs, pmat) + EPS) * (HEAD_DIM ** -0.5)
        k_ref[0, :, p * LANES:(p + 1) * LANES] = ks * lax.rsqrt(_head_sums(ks * ks, pmat) + EPS)
    v_ref[0] = y[:, 2 * C_W:]
    ab = ab_ref[0, :, 0:4 * C_HEADS]
    lane = lax.broadcasted_iota(jnp.int32, ab.shape, 1)
    is_beta = (lane // C_HEADS) % 2 == 1
    t = ab + dtb_ref[...]
    softplus = jnp.maximum(t, 0.0) + jnp.log1p(jnp.exp(-jnp.abs(t)))
    gb = jnp.where(is_beta, jax.nn.sigmoid(ab), -jnp.exp(a_ref[...]) * softplus)
    gb_ref[0, 0] = gb[:, 0:2 * C_HEADS]
    gb_ref[0, 1] = gb[:, 2 * C_HEADS:4 * C_HEADS]


def gdn_prep(zqkv, zab, conv_w, a_log, dt_bias):
    B, L, W3 = zqkv.shape
    tm = GDN_TM
    assert L % tm == 0
    nb = tm // SUBLANES
    zero = jnp.zeros((C_HEADS,), F32)
    a_lane = jnp.concatenate([a_log[0], zero, a_log[1], zero]).astype(F32)[None, :]
    dtb_lane = jnp.concatenate([dt_bias[0], zero, dt_bias[1], zero]).astype(F32)[None, :]
    outs = pl.pallas_call(
        _gdn_prep_kernel,
        grid=(B, L // tm),
        in_specs=[
            pl.BlockSpec((1, tm, W3), lambda b, i: (b, i, 0)),
            pl.BlockSpec((1, SUBLANES, W3), lambda b, i: (b, jnp.maximum(i * nb - 1, 0), 0)),
            pl.BlockSpec((1, SUBLANES, W3), lambda b, i: (b, jnp.minimum((i + 1) * nb, L // SUBLANES - 1), 0)),
            pl.BlockSpec((1, tm, LANES), lambda b, i: (b, i, 0)),
            pl.BlockSpec((C_SHORT, W3), lambda b, i: (0, 0)),
            pl.BlockSpec((1, 4 * C_HEADS), lambda b, i: (0, 0)),
            pl.BlockSpec((1, 4 * C_HEADS), lambda b, i: (0, 0)),
        ],
        out_specs=[
            pl.BlockSpec((1, tm, C_W), lambda b, i: (b, i, 0)),
            pl.BlockSpec((1, tm, C_W), lambda b, i: (b, i, 0)),
            pl.BlockSpec((1, tm, C_W), lambda b, i: (b, i, 0)),
            pl.BlockSpec((1, 2, tm, 2 * C_HEADS), lambda b, i: (b, 0, i, 0)),
        ],
        out_shape=[jax.ShapeDtypeStruct((B, L, C_W), F32)] * 3 + [jax.ShapeDtypeStruct((B, 2, L, 2 * C_HEADS), F32)],
        compiler_params=pltpu.CompilerParams(dimension_semantics=("arbitrary", "arbitrary"),
                                             vmem_limit_bytes=VMEM_LIMIT_BYTES),
        name="gdn_prep",
    )(zqkv, zqkv, zqkv, zab, conv_w.astype(F32), a_lane, dtb_lane)
    return outs


def _gdn_kernel(q_ref, k_ref, v_ref, gb_ref, zg_ref, s0_ref, nw_ref, o_ref, st_ref, s_scr, of_scr):
    d = pl.program_id(1)
    c = pl.program_id(2)
    n = pl.num_programs(2)
    C = C_CHUNK
    fwd = d == 0
    nsub = q_ref.shape[1] // C
    block = jnp.where(fwd, c, n - 1 - c)

    @pl.when(c == 0)
    def _():
        s_scr[...] = s0_ref[:, 0]

    row = lax.broadcasted_iota(jnp.int32, (C, C), 0)
    col = lax.broadcasted_iota(jnp.int32, (C, C), 1)
    ahead = jnp.where(fwd, row - col, col - row)
    incl = ahead >= 0
    strict = ahead > 0
    incl_f = incl.astype(F32)
    nreq = q_ref.shape[0]
    units = [(r, h) for r in range(nreq) for h in range(C_HEADS)]
    s_cur = [s_scr[r, h] for r, h in units]
    pmat = _head_pair_sum_matrix()
    done = []
    for j in range(nsub):
        sub = jnp.where(fwd, j, nsub - 1 - j)
        rows = pl.ds(pl.multiple_of(sub * C, C), C)
        r0 = pl.multiple_of((block * nsub + sub) * C, C)
        gc, gc_t, g_last, beta = [], [], [], []
        for r in range(nreq):
            g = gb_ref[r, 0, rows, 0:C_HEADS]
            beta.append(gb_ref[r, 0, rows, C_HEADS:2 * C_HEADS])
            gc.append(jnp.dot(incl_f, g, precision=HIGHEST, preferred_element_type=F32))
            gc_t.append(gc[r].T)
            g_last.append(jnp.where(fwd, gc[r][C - 1:C, :], gc[r][0:1, :]))
        H = range(len(units))
        heads = [slice(h * HEAD_DIM, (h + 1) * HEAD_DIM) for _, h in units]
        s_old = s_cur
        qs = [q_ref[r, rows, heads[i]] for i, (r, _) in enumerate(units)]
        ks = [k_ref[r, rows, heads[i]] for i, (r, _) in enumerate(units)]
        vs = [v_ref[r, rows, heads[i]] for i, (r, _) in enumerate(units)]
        gcol = [gc[r][:, h:h + 1] for r, h in units]
        bcol = [beta[r][:, h:h + 1] for r, h in units]
        gl = [g_last[r][:, h:h + 1] for r, h in units]
        gamma = [jnp.exp(jnp.where(incl, gcol[i] - gc_t[r][h:h + 1, :], NEG_INF)) for i, (r, h) in enumerate(units)]
        egc = [jnp.exp(gcol[h]) for h in H]
        kb = [ks[h].astype(BF16) for h in H]
        nmat = [jnp.where(strict, _dot_nt(kb[h], kb[h]) * gamma[h], 0.0) * bcol[h] for h in H]
        attn = [(_dot_nt(qs[h].astype(BF16), kb[h]) * gamma[h]).astype(BF16) for h in H]
        xr = row ^ col
        eye = (row == col).astype(F32)
        tinv = [eye - jnp.where((xr >> 1) == 0, nmat[h], 0.0) for h in H]
        for lvl in range(1, 6):
            off_diag = (xr >> lvl) == 1
            wmat = [_dot(jnp.where(off_diag, nmat[h], 0.0).astype(BF16), tinv[h].astype(BF16)).astype(BF16) for h in H]
            tinv = [tinv[h] - _dot(tinv[h].astype(BF16), wmat[h]) for h in H]
        x = [jnp.concatenate([vs[h] * bcol[h], ks[h] * (bcol[h] * egc[h])], axis=1) for h in H]
        x = [x[h] + _dot((tinv[h] - eye).astype(BF16), x[h].astype(BF16)) for h in H]
        sb = [s_old[h].astype(BF16) for h in H]
        v_new = [x[h][:, :HEAD_DIM] - _dot(x[h][:, HEAD_DIM:].astype(BF16), sb[h]) for h in H]
        vb = [v_new[h].astype(BF16) for h in H]
        outs = [_dot((qs[h] * egc[h]).astype(BF16), sb[h]) + _dot(attn[h], vb[h]) for h in H]
        kd = [(ks[h] * jnp.exp(gl[h] - gcol[h])).astype(BF16) for h in H]
        s_new = [s_old[h] * jnp.exp(gl[h]) + lax.dot_general(kd[h], vb[h], (((0,), (0,)), ((), ())), preferred_element_type=F32)
                 for h in H]
        s_cur = s_new
        o = [jnp.concatenate(outs[r * C_HEADS:(r + 1) * C_HEADS], axis=1) for r in range(nreq)]
        done.append((rows, r0, o))

    for i, (r, h) in enumerate(units):
        s_scr[r, h] = s_cur[i]

    @pl.when(fwd)
    def _():
        for rows, r0, o in done:
            for r in range(nreq):
                of_scr[r, pl.ds(r0, C), :] = o[r]

    @pl.when(jnp.logical_not(fwd))
    def _():
        for rows, r0, o in done:
            for r in range(nreq):
                tot = of_scr[r, pl.ds(r0, C), :] + o[r]
                zg = zg_ref[r, rows, :]
                gate = zg * jax.nn.sigmoid(zg)
                for p in range(C_W // LANES):
                    cols = slice(p * LANES, (p + 1) * LANES)
                    t = tot[:, cols]
                    ms = _head_sums(t * t, pmat) * (1.0 / HEAD_DIM)
                    o_ref[r, rows, cols] = (t * lax.rsqrt(ms + EPS) * nw_ref[:, cols] * gate[:, cols]).astype(o_ref.dtype)

    @pl.when(c == n - 1)
    def _():
        st_ref[:, 0] = s_scr[...]


def gdn_scan(q, k, v, gb, zg, s0, norm_w):
    B, L, W = q.shape
    C = GDN_SUB * C_CHUNK
    n = L // C
    R = min(B, GDN_REQS)
    assert L % C == 0 and W == C_W and B % R == 0
    chunk_of = lambda d, c: jnp.where(d == 0, c, n - 1 - c)
    seq_spec = pl.BlockSpec((R, C, W), lambda b, d, c: (b, chunk_of(d, c), 0))
    state_spec = pl.BlockSpec((R, 1, C_HEADS, HEAD_DIM, HEAD_DIM), lambda b, d, c: (b, d, 0, 0, 0))
    nw = jnp.tile(norm_w.astype(F32), C_HEADS)[None, :]
    return pl.pallas_call(
        _gdn_kernel,
        grid=(B // R, 2, n),
        in_specs=[
            seq_spec, seq_spec, seq_spec,
            pl.BlockSpec((R, 1, C, 2 * C_HEADS), lambda b, d, c: (b, d, chunk_of(d, c), 0)),
            seq_spec,
            state_spec,
            pl.BlockSpec((1, W), lambda b, d, c: (0, 0)),
        ],
        out_specs=[
            pl.BlockSpec((R, C, W), lambda b, d, c: (b, jnp.where(d == 0, n - 1, n - 1 - c), 0)),
            state_spec,
        ],
        out_shape=[jax.ShapeDtypeStruct((B, L, W), BF16), jax.ShapeDtypeStruct(s0.shape, F32)],
        scratch_shapes=[pltpu.VMEM((R, C_HEADS, HEAD_DIM, HEAD_DIM), F32), pltpu.VMEM((R, L, W), F32)],
        compiler_params=pltpu.CompilerParams(dimension_semantics=("arbitrary", "arbitrary", "arbitrary"),
                                             vmem_limit_bytes=VMEM_LIMIT_BYTES),
        name="gdn_scan",
    )(q, k, v, gb, zg, s0.astype(F32), nw)


MOE_TT = 512
MOE_RT = 512
MOE_SELECT_REQS = 4
MOE_SELECT_REQS_LONG = 2


def _moe_router_kernel(x_ref, g_ref, shift_ref, scale_ref, wr_ref, h_ref, aff_ref):
    x = x_ref[0]
    y = x * lax.rsqrt(jnp.mean(x * x, axis=-1, keepdims=True) + EPS) * g_ref[...]
    h = y * (1.0 + scale_ref[0]) + shift_ref[0]
    h_hi = h.astype(BF16)
    h_ref[0] = h_hi
    h_lo = (h - h_hi.astype(F32)).astype(BF16)
    w = wr_ref[...]
    w_hi = w.astype(BF16)
    w_lo = (w - w_hi.astype(F32)).astype(BF16)
    logits = _dot(h_hi, w_hi) + _dot(h_hi, w_lo) + _dot(h_lo, w_hi)
    lane = lax.broadcasted_iota(jnp.int32, logits.shape, 1)
    logits = jnp.where(lane < N_EXPERTS, logits, NEG_INF)
    e = jnp.exp(logits - logits.max(axis=-1, keepdims=True))
    aff_ref[0] = e / e.sum(axis=-1, keepdims=True)


def _moe_select_kernel(aff_ref, slot_ref, start_ref, *, cap):
    nreq, T = aff_ref.shape[0], aff_ref.shape[1]
    reqs = range(nreq)
    bits = [pltpu.bitcast(aff_ref[q], jnp.int32) for q in reqs]

    def bisect(i, v):
        cand = [v[q] | (1 << (30 - i)) for q in reqs]
        cnt = [jnp.sum((bits[q] >= cand[q]).astype(jnp.int32), axis=0, keepdims=True) for q in reqs]
        return tuple(jnp.where(cnt[q] >= cap, cand[q], v[q]) for q in reqs)

    thr = lax.fori_loop(0, 31, bisect, tuple(jnp.zeros((1, LANES), jnp.int32) for _ in reqs))
    blk = min(T, MOE_TT)
    r = lax.broadcasted_iota(jnp.int32, (blk, blk), 0)
    c = lax.broadcasted_iota(jnp.int32, (blk, blk), 1)
    before = (c < r).astype(BF16)
    n_tiles = T // blk
    for q in reqs:
        gt = (bits[q] > thr[q]).astype(F32)
        eq = (bits[q] == thr[q]).astype(F32)
        need = cap - jnp.sum(gt, axis=0, keepdims=True)
        carry_gt = jnp.zeros((1, LANES), F32)
        carry_eq = jnp.zeros((1, LANES), F32)
        start_ref[q] = jnp.zeros(start_ref.shape[1:], F32)
        for b in range(n_tiles):
            rows = slice(b * blk, (b + 1) * blk)
            gt_b, eq_b = gt[rows], eq[rows]
            pos_gt = _dot(before, gt_b.astype(BF16)) + carry_gt
            pos_eq = _dot(before, eq_b.astype(BF16)) + carry_eq
            chosen = gt_b + eq_b * (pos_eq < need).astype(F32)
            slot_ref[q, rows, :] = jnp.where(chosen > 0.5, pos_gt + jnp.minimum(pos_eq, need), -1.0)
            carry_gt = carry_gt + jnp.sum(gt_b, axis=0, keepdims=True)
            carry_eq = carry_eq + jnp.sum(eq_b, axis=0, keepdims=True)
            start_ref[q, b + 1:b + 2, :] = carry_gt + jnp.minimum(carry_eq, need)


def _expert_column(a, e):
    lane = lax.broadcasted_iota(jnp.int32, a.shape, 1)
    return jnp.sum(jnp.where(lane == e, a, 0.0), axis=1, keepdims=True)


MOE_START_ROWS = 16
MOE_WIN = 128


def _one_hot_slots(slot_col, first, width):
    s = lax.broadcasted_iota(jnp.int32, (slot_col.shape[0], width), 1).astype(F32)
    return (slot_col - first == s).astype(BF16)


MOE_SLOT_ALIGN = 16


def _slot_windows(start_ref, b, e, k):
    base = (b * N_EXPERTS + e) * MOE_START_ROWS + k
    lo, hi = start_ref[base], start_ref[base + 1]
    first = (lo // MOE_SLOT_ALIGN) * MOE_SLOT_ALIGN
    return first, jnp.where(hi > lo, (hi - first + MOE_WIN - 1) // MOE_WIN, 0)


def _window_one_hot(col, first, w, cap):
    nominal = first + w * MOE_WIN
    s0 = pl.multiple_of(jnp.minimum(nominal, cap - MOE_WIN), MOE_SLOT_ALIGN)
    rel = jnp.where(col >= nominal.astype(F32), col - s0.astype(F32), -1.0)
    return s0, _one_hot_slots(rel, 0.0, MOE_WIN)


def _moe_gather_kernel(start_ref, slot_ref, h_ref, xg_ref, acc_ref):
    b, e = pl.program_id(0), pl.program_id(1)
    acc_ref[...] = jnp.zeros_like(acc_ref)
    for k in range(slot_ref.shape[1] // MOE_TT):
        rows = slice(k * MOE_TT, (k + 1) * MOE_TT)
        col = _expert_column(slot_ref[0, rows, :], e)
        first, count = _slot_windows(start_ref, b, e, k)

        def window(w, carry, col=col, rows=rows, first=first):
            s0, pt = _window_one_hot(col, first, w, acc_ref.shape[0])
            acc_ref[pl.ds(s0, MOE_WIN), :] += lax.dot_general(pt, h_ref[0, rows, :], (((0,), (0,)), ((), ())),
                                                              preferred_element_type=F32)
            return carry

        lax.fori_loop(0, count, window, 0)
    xg_ref[0] = acc_ref[...].astype(xg_ref.dtype)


def _all_expert_slots(slot, cap):
    width = N_EXPERTS * cap
    spread = (lax.broadcasted_iota(jnp.int32, (LANES, width), 1) // cap
              == lax.broadcasted_iota(jnp.int32, (LANES, width), 0)).astype(BF16)
    slot_wide = _dot(slot.astype(BF16), spread)
    lane_slot = (lax.broadcasted_iota(jnp.int32, (slot.shape[0], width), 1) % cap).astype(F32)
    return spread, slot_wide == lane_slot


def _moe_gather_short_kernel(slot_ref, h_ref, xg_ref, *, cap):
    _, chosen = _all_expert_slots(slot_ref[0], cap)
    rows = lax.dot_general(chosen.astype(BF16), h_ref[0], (((0,), (0,)), ((), ())), preferred_element_type=F32)
    for e in range(N_EXPERTS):
        xg_ref[e] = rows[e * cap:(e + 1) * cap].astype(xg_ref.dtype)


def _moe_ffn_kernel(x_ref, wg_ref, wu_ref, wd_ref, y_ref, wg_s, wu_s, wd_s):
    @pl.when(pl.program_id(1) == 0)
    def _():
        wg_s[...] = wg_ref[0].astype(BF16)
        wu_s[...] = wu_ref[0].astype(BF16)
        wd_s[...] = wd_ref[0].astype(BF16)

    x = x_ref[0]
    a = _dot(x, wg_s[...])
    u = _dot(x, wu_s[...])
    hid = (a * jax.nn.sigmoid(a) * u).astype(BF16)
    y_ref[0] = _dot(hid, wd_s[...]).astype(y_ref.dtype)


def _moe_scatter_kernel(start_ref, slot_ref, aff_ref, y_ref, x_ref, gate_ref, o_ref, acc_ref):
    b, k = pl.program_id(0), pl.program_id(1)
    acc_ref[...] = jnp.zeros_like(acc_ref)
    slot, aff = slot_ref[0], aff_ref[0]
    for e in range(N_EXPERTS):
        col, weight = slot[:, e:e + 1], aff[:, e:e + 1]
        first, count = _slot_windows(start_ref, b, e, k)

        def window(w, carry, e=e, col=col, weight=weight, first=first):
            s0, pt = _window_one_hot(col, first, w, y_ref.shape[1])
            acc_ref[...] += weight * _dot(pt, y_ref[e, pl.ds(s0, MOE_WIN), :])
            return carry

        lax.fori_loop(0, count, window, 0)
    o_ref[0] = x_ref[0] + gate_ref[0] * acc_ref[...]


def _moe_scatter_short_kernel(slot_ref, aff_ref, y_ref, x_ref, gate_ref, o_ref, *, cap):
    spread, chosen = _all_expert_slots(slot_ref[0], cap)
    rest = aff_ref[0]
    aff_wide = jnp.zeros(chosen.shape, F32)
    for _ in range(3):
        part = rest.astype(BF16)
        aff_wide = aff_wide + _dot(part, spread)
        rest = rest - part.astype(F32)
    weighted = jnp.where(chosen, aff_wide, 0.0)
    w_hi = weighted.astype(BF16)
    w_lo = (weighted - w_hi.astype(F32)).astype(BF16)
    y_all = jnp.concatenate([y_ref[e] for e in range(N_EXPERTS)], axis=0)
    o_ref[0] = x_ref[0] + gate_ref[0] * (_dot(w_hi, y_all) + _dot(w_lo, y_all))


def moe_route(x, g, shift, scale, w_router):
    B, T, D = x.shape
    tt = min(T, MOE_TT)
    cap = EC_CAPACITY * T // N_EXPERTS
    per_request = shift.shape[0] == B
    mod_spec = pl.BlockSpec((1, 1, D), (lambda b, k: (b, 0, 0)) if per_request else (lambda b, k: (0, 0, 0)))
    wr = jnp.concatenate([w_router.astype(F32), jnp.zeros((D, LANES - N_EXPERTS), F32)], axis=1)
    h, aff = pl.pallas_call(
        _moe_router_kernel,
        grid=(B, T // tt),
        in_specs=[pl.BlockSpec((1, tt, D), lambda b, k: (b, k, 0)), pl.BlockSpec((1, D), lambda b, k: (0, 0)),
                  mod_spec, mod_spec, pl.BlockSpec((D, LANES), lambda b, k: (0, 0))],
        out_specs=[pl.BlockSpec((1, tt, D), lambda b, k: (b, k, 0)), pl.BlockSpec((1, tt, LANES), lambda b, k: (b, k, 0))],
        out_shape=[jax.ShapeDtypeStruct((B, T, D), BF16), jax.ShapeDtypeStruct((B, T, LANES), F32)],
        compiler_params=pltpu.CompilerParams(dimension_semantics=("arbitrary", "arbitrary"),
                                             vmem_limit_bytes=VMEM_LIMIT_BYTES),
        name="moe_router",
    )(x, g.astype(F32)[None, :], shift, scale, wr)
    assert T // tt < MOE_START_ROWS
    rq = MOE_SELECT_REQS if T <= MOE_TT else MOE_SELECT_REQS_LONG
    rq = rq if B % rq == 0 else 1
    slot, start = pl.pallas_call(
        functools.partial(_moe_select_kernel, cap=cap),
        grid=(B // rq,),
        in_specs=[pl.BlockSpec((rq, T, LANES), lambda b: (b, 0, 0))],
        out_specs=[pl.BlockSpec((rq, T, LANES), lambda b: (b, 0, 0)),
                   pl.BlockSpec((rq, MOE_START_ROWS, LANES), lambda b: (b, 0, 0))],
        out_shape=[jax.ShapeDtypeStruct((B, T, LANES), F32), jax.ShapeDtypeStruct((B, MOE_START_ROWS, LANES), F32)],
        compiler_params=pltpu.CompilerParams(dimension_semantics=("arbitrary",), vmem_limit_bytes=VMEM_LIMIT_BYTES),
        name="moe_select",
    )(aff)
    start = start[:, :, :N_EXPERTS].astype(jnp.int32).transpose(0, 2, 1).reshape(-1)
    return h, aff, slot, start


def moe_gather(h, slot, start):
    B, T, D = h.shape
    cap = EC_CAPACITY * T // N_EXPERTS
    out_shape = jax.ShapeDtypeStruct((N_EXPERTS, B * cap, D), BF16)
    if T <= MOE_TT:
        return pl.pallas_call(
            functools.partial(_moe_gather_short_kernel, cap=cap),
            grid=(B,),
            in_specs=[pl.BlockSpec((1, T, LANES), lambda b: (b, 0, 0)), pl.BlockSpec((1, T, D), lambda b: (b, 0, 0))],
            out_specs=pl.BlockSpec((N_EXPERTS, cap, D), lambda b: (0, b, 0)),
            out_shape=out_shape,
            compiler_params=pltpu.CompilerParams(dimension_semantics=("arbitrary",), vmem_limit_bytes=VMEM_LIMIT_BYTES),
            name="moe_gather_short",
        )(slot, h)
    assert cap % MOE_WIN == 0 and T % MOE_TT == 0
    return pl.pallas_call(
        _moe_gather_kernel,
        grid_spec=pltpu.PrefetchScalarGridSpec(
            num_scalar_prefetch=1,
            grid=(B, N_EXPERTS),
            in_specs=[pl.BlockSpec((1, T, LANES), lambda b, e, st: (b, 0, 0)),
                      pl.BlockSpec((1, T, D), lambda b, e, st: (b, 0, 0))],
            out_specs=pl.BlockSpec((1, cap, D), lambda b, e, st: (e, b, 0)),
            scratch_shapes=[pltpu.VMEM((cap, D), F32)],
        ),
        out_shape=out_shape,
        compiler_params=pltpu.CompilerParams(dimension_semantics=("arbitrary",) * 2, vmem_limit_bytes=VMEM_LIMIT_BYTES),
        name="moe_gather",
    )(start, slot, h)


def moe_ffn(xg, w_gate, w_up, w_down, layer):
    E, R, D = xg.shape
    rt = min(R, MOE_RT)
    w_spec = pl.BlockSpec((None, 1, D, MOE_D_FF), lambda e, r: (layer, e, 0, 0))
    return pl.pallas_call(
        _moe_ffn_kernel,
        grid=(E, R // rt),
        in_specs=[pl.BlockSpec((1, rt, D), lambda e, r: (e, r, 0)), w_spec, w_spec,
                  pl.BlockSpec((None, 1, MOE_D_FF, D), lambda e, r: (layer, e, 0, 0))],
        out_specs=pl.BlockSpec((1, rt, D), lambda e, r: (e, r, 0)),
        out_shape=jax.ShapeDtypeStruct((E, R, D), BF16),
        scratch_shapes=[pltpu.VMEM((D, MOE_D_FF), BF16), pltpu.VMEM((D, MOE_D_FF), BF16), pltpu.VMEM((MOE_D_FF, D), BF16)],
        compiler_params=pltpu.CompilerParams(dimension_semantics=("arbitrary", "arbitrary"),
                                             vmem_limit_bytes=VMEM_LIMIT_BYTES),
        name="moe_ffn",
    )(xg, w_gate, w_up, w_down)


def moe_scatter(slot, aff, start, y, x, gate):
    B, T, D = x.shape
    cap = EC_CAPACITY * T // N_EXPERTS
    per_request = gate.shape[0] == B
    out_shape = jax.ShapeDtypeStruct((B, T, D), F32)
    if T <= MOE_TT:
        whole = lambda b: (b, 0, 0)
        return pl.pallas_call(
            functools.partial(_moe_scatter_short_kernel, cap=cap),
            grid=(B,),
            in_specs=[pl.BlockSpec((1, T, LANES), whole), pl.BlockSpec((1, T, LANES), whole),
                      pl.BlockSpec((N_EXPERTS, cap, D), lambda b: (0, b, 0)), pl.BlockSpec((1, T, D), whole),
                      pl.BlockSpec((1, 1, D), whole if per_request else (lambda b: (0, 0, 0)))],
            out_specs=pl.BlockSpec((1, T, D), whole),
            out_shape=out_shape,
            compiler_params=pltpu.CompilerParams(dimension_semantics=("arbitrary",), vmem_limit_bytes=VMEM_LIMIT_BYTES),
            name="moe_scatter_short",
        )(slot, aff, y, x, gate)
    tt = MOE_TT
    tile = lambda b, k, st: (b, k, 0)
    return pl.pallas_call(
        _moe_scatter_kernel,
        grid_spec=pltpu.PrefetchScalarGridSpec(
            num_scalar_prefetch=1,
            grid=(B, T // tt),
            in_specs=[pl.BlockSpec((1, tt, LANES), tile), pl.BlockSpec((1, tt, LANES), tile),
                      pl.BlockSpec((N_EXPERTS, cap, D), lambda b, k, st: (0, b, 0), pipeline_mode=pl.Buffered(1)),
                      pl.BlockSpec((1, tt, D), tile),
                      pl.BlockSpec((1, 1, D), (lambda b, k, st: (b, 0, 0)) if per_request else (lambda b, k, st: (0, 0, 0)))],
            out_specs=pl.BlockSpec((1, tt, D), tile),
            scratch_shapes=[pltpu.VMEM((tt, D), F32)],
        ),
        out_shape=out_shape,
        compiler_params=pltpu.CompilerParams(dimension_semantics=("arbitrary",) * 2, vmem_limit_bytes=VMEM_LIMIT_BYTES),
        name="moe_scatter",
    )(start, slot, aff, y, x, gate)


def moe_block(x, g, shift, scale, gate, w_router, w_gate, w_up, w_down, layer):
    h, aff, slot, start = moe_route(x, g, shift, scale, w_router)
    y = moe_ffn(moe_gather(h, slot, start), w_gate, w_up, w_down, layer)
    return moe_scatter(slot, aff, start, y, x, gate)


HY_TM = 512
HY_TK = 1024
HY_BG = 8
HY_BG_LONG = 2


def dft_tables(L):
    blk = min(L, HY_TM)
    t = jnp.arange(L, dtype=jnp.int32)

    def angles(f):
        return ((f[:, None] * t[None, :]) % (2 * L)).astype(F32) * (math.pi / L)

    a_hi = angles(jnp.arange(0, L, blk, dtype=jnp.int32))[:, None, :]
    a_lo = angles(jnp.arange(blk, dtype=jnp.int32))[None, :, :]
    cos_t = (jnp.cos(a_hi) * jnp.cos(a_lo) - jnp.sin(a_hi) * jnp.sin(a_lo)).reshape(L, L)
    sin_t = (jnp.sin(a_hi) * jnp.cos(a_lo) + jnp.cos(a_hi) * jnp.sin(a_lo)).reshape(L, L)
    return cos_t.astype(BF16), sin_t.astype(BF16)


def _alt_sign(rows, first_row):
    t = lax.broadcasted_iota(jnp.int32, (rows, 1), 0) + first_row
    return jnp.where(t % 2 == 0, 1.0, -1.0).astype(F32)


def _hy_prep_kernel(x_ref, prev_ref, next_ref, cw_ref, cb_ref, o_ref, v16_ref):
    i = pl.program_id(1)
    n = pl.num_programs(1)
    x = x_ref[0]
    tm = x.shape[0]
    row = lax.broadcasted_iota(jnp.int32, x.shape, 0)
    prev_row = jnp.where(i > 0, prev_ref[0, SUBLANES - 1:SUBLANES, :], 0.0)
    next_row = jnp.where(i < n - 1, next_ref[0, 0:1, :], 0.0)
    x_prev = jnp.where(row == 0, prev_row, pltpu.roll(x, 1, axis=0))
    x_next = jnp.where(row == tm - 1, next_row, pltpu.roll(x, tm - 1, axis=0))
    y = x_prev * cw_ref[0:1, :] + x * cw_ref[1:2, :] + x_next * cw_ref[2:3, :] + cb_ref[...]
    o_ref[0] = y
    v16_ref[0] = y[:, :HY_CH].astype(BF16)


def hyena_prep(zh, conv_w, conv_b):
    B, L, W = zh.shape
    tm = min(L, GDN_TM)
    nb = tm // SUBLANES
    return pl.pallas_call(
        _hy_prep_kernel,
        grid=(B, L // tm),
        in_specs=[
            pl.BlockSpec((1, tm, W), lambda b, i: (b, i, 0)),
            pl.BlockSpec((1, SUBLANES, W), lambda b, i: (b, jnp.maximum(i * nb - 1, 0), 0)),
            pl.BlockSpec((1, SUBLANES, W), lambda b, i: (b, jnp.minimum((i + 1) * nb, L // SUBLANES - 1), 0)),
            pl.BlockSpec((HY_SHORT, W), lambda b, i: (0, 0)),
            pl.BlockSpec((1, W), lambda b, i: (0, 0)),
        ],
        out_specs=[pl.BlockSpec((1, tm, W), lambda b, i: (b, i, 0)), pl.BlockSpec((1, tm, HY_CH), lambda b, i: (b, i, 0))],
        out_shape=[jax.ShapeDtypeStruct((B, L, W), F32), jax.ShapeDtypeStruct((B, L, HY_CH), BF16)],
        compiler_params=pltpu.CompilerParams(dimension_semantics=("arbitrary", "arbitrary"),
                                             vmem_limit_bytes=VMEM_LIMIT_BYTES),
        name="hyena_prep",
    )(zh, zh, zh, conv_w.astype(F32), conv_b.astype(F32)[None, :])


def _hy_taps_kernel(feat_ref, w1_ref, b1_ref, w2_ref, b2_ref, fr_ref, w3f_ref, w3b_ref, dec_ref, sum_ref, dif_ref):
    L = feat_ref.shape[0]
    fr = fr_ref[...]
    h = jnp.sin(fr * (jnp.dot(feat_ref[...], w1_ref[...], precision=HIGHEST, preferred_element_type=F32) + b1_ref[...]))
    h = jnp.sin(fr * (jnp.dot(h, w2_ref[...], precision=HIGHEST, preferred_element_type=F32) + b2_ref[...]))
    t = lax.broadcasted_iota(jnp.int32, (L, 1), 0)
    window = jnp.exp(-(t.astype(F32) / (L - 1)) * dec_ref[...]) + HY_SHIFT
    fwd = jnp.dot(h, w3f_ref[...], precision=HIGHEST, preferred_element_type=F32) * window
    bwd = jnp.where(t == 0, 0.0, jnp.dot(h, w3b_ref[...], precision=HIGHEST, preferred_element_type=F32) * window)
    inv = 1.0 / (jnp.sum(jnp.abs(fwd), axis=0, keepdims=True) + jnp.sum(jnp.abs(bwd), axis=0, keepdims=True))
    sum_ref[...] = (fwd + bwd) * inv
    dif_ref[...] = (bwd - fwd) * inv


def hyena_taps(L, w1, b1, w2, b2, w3, freq):
    f32 = F32
    t = jnp.linspace(0.0, 1.0, L, dtype=f32)[:, None]
    bands = (HY_EMB - 1) // 2
    omega = 2.0 * math.pi * jnp.arange(L, dtype=f32)[:, None] / L
    fb = jnp.linspace(1e-4, bands - 1, bands, dtype=f32)[None, :]
    feats = jnp.concatenate([t, jnp.cos(fb * omega), -jnp.sin(fb * omega)], axis=-1)
    max_decay = math.log(HY_DECAY_TARGET) / HY_FAST_DECAY
    min_decay = math.log(HY_DECAY_TARGET) / HY_SLOW_DECAY
    deltas = jnp.abs(jnp.linspace(min_decay, max_decay, HY_CH, dtype=f32))
    dec = jnp.tile(deltas, HY_ORDER)[None, :]
    n_col = HY_ORDER * HY_CH
    cb = 256
    full = lambda shape: pl.BlockSpec(shape, lambda j: (0, 0))
    col = lambda rows: pl.BlockSpec((rows, cb), lambda j: (0, j))
    return pl.pallas_call(
        _hy_taps_kernel,
        grid=(n_col // cb,),
        in_specs=[full((L, HY_EMB)), full((HY_EMB, HY_FILT_W)), full((1, HY_FILT_W)), full((HY_FILT_W, HY_FILT_W)),
                  full((1, HY_FILT_W)), full((1, HY_FILT_W)), col(HY_FILT_W), col(HY_FILT_W), col(1)],
        out_specs=[col(L), col(L)],
        out_shape=[jax.ShapeDtypeStruct((L, n_col), f32)] * 2,
        compiler_params=pltpu.CompilerParams(dimension_semantics=("arbitrary",), vmem_limit_bytes=VMEM_LIMIT_BYTES),
        name="hyena_taps",
    )(feats, w1.astype(f32), b1.astype(f32)[None, :], w2.astype(f32), b2.astype(f32)[None, :], freq.astype(f32)[None, :],
      w3.astype(f32)[:, :n_col], w3.astype(f32)[:, n_col:], dec)


def _hy_spec_kernel(ch_ref, sh_ref, sum_ref, dif_ref, hr_ref, hi_ref, ny_ref, acc_r, acc_i, acc_n):
    m, k = pl.program_id(1), pl.program_id(2)
    tk = sum_ref.shape[0]

    @pl.when(k == 0)
    def _():
        acc_r[...] = jnp.zeros_like(acc_r)
        acc_i[...] = jnp.zeros_like(acc_i)

    @pl.when(jnp.logical_and(k == 0, m == 0))
    def _():
        acc_n[...] = jnp.zeros_like(acc_n)

    a = sum_ref[...]
    acc_r[...] += _dot(ch_ref[...], a.astype(BF16))
    acc_i[...] += _dot(sh_ref[...], dif_ref[...].astype(BF16))

    @pl.when(m == 0)
    def _():
        acc_n[...] += jnp.sum(a * _alt_sign(tk, k * tk), axis=0, keepdims=True)

    @pl.when(k == pl.num_programs(2) - 1)
    def _():
        hr_ref[...] = acc_r[...]
        hi_ref[...] = acc_i[...]
        ny_ref[...] = jnp.broadcast_to(acc_n[...], ny_ref.shape)


def hyena_spectrum(tables, tap_sum, tap_dif):
    L, N = tap_sum.shape
    tm = min(L, HY_TM)
    tk = min(L, 2 * HY_TK)
    cb = 512
    tab = pl.BlockSpec((tm, tk), lambda j, m, k: (m, k))
    dat = pl.BlockSpec((tk, cb), lambda j, m, k: (k, j))
    return pl.pallas_call(
        _hy_spec_kernel,
        grid=(N // cb, L // tm, L // tk),
        in_specs=[tab, tab, dat, dat],
        out_specs=[pl.BlockSpec((tm, cb), lambda j, m, k: (m, j)), pl.BlockSpec((tm, cb), lambda j, m, k: (m, j)),
                   pl.BlockSpec((SUBLANES, cb), lambda j, m, k: (0, j))],
        out_shape=[jax.ShapeDtypeStruct((L, N), F32), jax.ShapeDtypeStruct((L, N), F32),
                   jax.ShapeDtypeStruct((SUBLANES, N), F32)],
        scratch_shapes=[pltpu.VMEM((tm, cb), F32), pltpu.VMEM((tm, cb), F32), pltpu.VMEM((1, cb), F32)],
        compiler_params=pltpu.CompilerParams(dimension_semantics=("arbitrary",) * 3, vmem_limit_bytes=VMEM_LIMIT_BYTES),
        name="hyena_spectrum",
    )(*tables, tap_sum, tap_dif)


def _hy_fwd_kernel(ch_ref, sh_ref, u_ref, hr_ref, hi_ref, hny_ref, yr_ref, yi_ref, yny_ref,
                   acc_c, acc_s, acc_n):
    m, k = pl.program_id(1), pl.program_id(2)
    nb, tk = u_ref.shape[0], u_ref.shape[1]
    tm = acc_c.shape[1]

    @pl.when(k == 0)
    def _():
        acc_c[...] = jnp.zeros_like(acc_c)
        acc_s[...] = jnp.zeros_like(acc_s)

    @pl.when(jnp.logical_and(k == 0, m == 0))
    def _():
        acc_n[...] = jnp.zeros_like(acc_n)

    sign = _alt_sign(tk, k * tk)
    for b in range(nb):
        ub = u_ref[b]
        acc_c[b] += _dot(ch_ref[...], ub)
        acc_s[b] += _dot(sh_ref[...], ub)

        @pl.when(m == 0)
        def _():
            acc_n[b] += jnp.sum(ub.astype(F32) * sign, axis=0, keepdims=True)

    @pl.when(k == pl.num_programs(2) - 1)
    def _():
        f = lax.broadcasted_iota(jnp.int32, (tm, 1), 0) + m * tm
        dc = jnp.where(f == 0, 0.5, 1.0).astype(F32)
        hr, hi = hr_ref[...], hi_ref[...]
        for b in range(nb):
            xr, xs = acc_c[b], acc_s[b]
            yr_ref[b] = ((xr * hr + xs * hi) * dc).astype(yr_ref.dtype)
            yi_ref[b] = (xr * hi - xs * hr).astype(yi_ref.dtype)
            yny_ref[b] = jnp.broadcast_to(acc_n[b] * hny_ref[0:1, :], yny_ref.shape[1:])


def _hy_inv_kernel(ch_ref, sh_ref, yr_ref, yi_ref, yny_ref, u_ref, xg_ref, skip_ref, *rest):
    o_refs, acc = rest[:-1], rest[-1]
    m, k = pl.program_id(1), pl.program_id(2)
    nb = yr_ref.shape[0]
    tm = acc.shape[1]
    L = tm * pl.num_programs(1)

    @pl.when(k == 0)
    def _():
        acc[...] = jnp.zeros_like(acc)

    for b in range(nb):
        acc[b] += _dot(ch_ref[...], yr_ref[b]) - _dot(sh_ref[...], yi_ref[b])

    @pl.when(k == pl.num_programs(2) - 1)
    def _():
        sign = _alt_sign(tm, m * tm)
        for b in range(nb):
            u = u_ref[b]
            y = acc[b] * (1.0 / L) + sign * yny_ref[b, 0:1, :] * (0.5 / L)
            res = xg_ref[b] * (y + u * skip_ref[...])
            for o_ref in o_refs:
                o_ref[b] = res.astype(o_ref.dtype)


def hyena_long_conv(tables, zf, u, u16, u_col, gate_col, hr, hi, hny, order, skip, out_dtypes):
    B, L, _ = u.shape
    C = HY_CH
    tm = min(L, HY_TM)
    tk = min(L, HY_TK)
    bg = min(B, HY_BG_LONG if L >= HY_TM else HY_BG)
    assert B % bg == 0
    grid = (B // bg, L // tm, L // tk)
    tab = pl.BlockSpec((tm, tk), lambda g, m, k: (m, k))
    params = pltpu.CompilerParams(dimension_semantics=("arbitrary",) * 3, vmem_limit_bytes=VMEM_LIMIT_BYTES)
    spec_m = pl.BlockSpec((tm, C), lambda g, m, k: (m, order))
    yr, yi, yny = pl.pallas_call(
        _hy_fwd_kernel,
        grid=grid,
        in_specs=[tab, tab, pl.BlockSpec((bg, tk, C), lambda g, m, k: (g, k, 0)), spec_m, spec_m,
                  pl.BlockSpec((SUBLANES, C), lambda g, m, k: (0, order))],
        out_specs=[pl.BlockSpec((bg, tm, C), lambda g, m, k: (g, m, 0)), pl.BlockSpec((bg, tm, C), lambda g, m, k: (g, m, 0)),
                   pl.BlockSpec((bg, SUBLANES, C), lambda g, m, k: (g, 0, 0))],
        out_shape=[jax.ShapeDtypeStruct((B, L, C), BF16), jax.ShapeDtypeStruct((B, L, C), BF16),
                   jax.ShapeDtypeStruct((B, SUBLANES, C), F32)],
        scratch_shapes=[pltpu.VMEM((bg, tm, C), F32), pltpu.VMEM((bg, tm, C), F32), pltpu.VMEM((bg, 1, C), F32)],
        compiler_params=params,
        name="hyena_fwd",
    )(*tables, u16, hr, hi, hny)
    return pl.pallas_call(
        _hy_inv_kernel,
        grid=grid,
        in_specs=[tab, tab, pl.BlockSpec((bg, tk, C), lambda g, m, k: (g, k, 0)),
                  pl.BlockSpec((bg, tk, C), lambda g, m, k: (g, k, 0)),
                  pl.BlockSpec((bg, SUBLANES, C), lambda g, m, k: (g, 0, 0)),
                  pl.BlockSpec((bg, tm, C), lambda g, m, k: (g, m, u_col)),
                  pl.BlockSpec((bg, tm, C), lambda g, m, k: (g, m, gate_col)),
                  pl.BlockSpec((1, C), lambda g, m, k: (0, 0))],
        out_specs=[pl.BlockSpec((bg, tm, C), lambda g, m, k: (g, m, 0)) for _ in out_dtypes],
        out_shape=[jax.ShapeDtypeStruct((B, L, C), dt) for dt in out_dtypes],
        scratch_shapes=[pltpu.VMEM((bg, tm, C), F32)],
        compiler_params=params,
        name="hyena_inv",
    )(*tables, yr, yi, yny, u, zf, skip.astype(F32)[order][None, :])


def hyena_filter_spectrum(L, w1, b1, w2, b2, w3, freq):
    tables = dft_tables(L)
    tap_sum, tap_dif = hyena_taps(L, w1, b1, w2, b2, w3, freq)
    return tables, hyena_spectrum(tables, tap_sum, tap_dif)


def hyena_mixer_pallas(zh, conv_w, conv_b, skip, tables, spectrum):
    hr, hi, hny = spectrum
    zf, v16 = hyena_prep(zh, conv_w, conv_b)
    y1, y1_16 = hyena_long_conv(tables, zf, zf, v16, 0, 1, hr, hi, hny, 0, skip, [F32, BF16])
    return hyena_long_conv(tables, zf, y1, y1_16, 0, 2, hr, hi, hny, 1, skip, [BF16])[0]


def rms_norm(x, g):
    xf = x.astype(jnp.float32)
    y = xf * lax.rsqrt(jnp.mean(xf * xf, axis=-1, keepdims=True) + EPS)
    return (y * g.astype(jnp.float32)).astype(x.dtype)


def _ada_kernel(c_ref, w_ref, b_ref, o_ref):
    cnd = c_ref[...]
    act = cnd * jax.nn.sigmoid(cnd)
    o_ref[0] = jnp.dot(act, w_ref[0], precision=HIGHEST, preferred_element_type=F32) + b_ref[0]


def ada_params_all(cond, w_ada, b_ada):
    N, D = cond.shape
    depth, _, W = w_ada.shape
    rows = -(-N // SUBLANES) * SUBLANES
    cond_p = jnp.concatenate([cond.astype(F32), jnp.zeros((rows - N, D), F32)], axis=0)
    cb = D
    out = pl.pallas_call(
        _ada_kernel,
        grid=(depth, W // cb),
        in_specs=[pl.BlockSpec((rows, D), lambda l, j: (0, 0)), pl.BlockSpec((1, D, cb), lambda l, j: (l, 0, j)),
                  pl.BlockSpec((1, 1, cb), lambda l, j: (l, 0, j))],
        out_specs=pl.BlockSpec((1, rows, cb), lambda l, j: (l, 0, j)),
        out_shape=jax.ShapeDtypeStruct((depth, rows, W), F32),
        compiler_params=pltpu.CompilerParams(dimension_semantics=("arbitrary", "arbitrary"),
                                             vmem_limit_bytes=VMEM_LIMIT_BYTES),
        name="ada_params",
    )(cond_p, w_ada.astype(F32), b_ada.astype(F32)[:, None, :])
    return out[:, :N]


def _final_norm_kernel(x_ref, g_ref, o_ref):
    x = x_ref[...]
    o_ref[...] = x * lax.rsqrt(jnp.mean(x * x, axis=-1, keepdims=True) + EPS) * g_ref[...]


def final_norm(x, g, tm=512):
    B, L, D = x.shape
    M = B * L
    assert M % tm == 0
    out = pl.pallas_call(
        _final_norm_kernel,
        grid=(M // tm,),
        in_specs=[pl.BlockSpec((tm, D), lambda i: (i, 0)), pl.BlockSpec((1, D), lambda i: (0, 0))],
        out_specs=pl.BlockSpec((tm, D), lambda i: (i, 0)),
        out_shape=jax.ShapeDtypeStruct((M, D), F32),
        compiler_params=pltpu.CompilerParams(dimension_semantics=("arbitrary",), vmem_limit_bytes=VMEM_LIMIT_BYTES),
        name="final_norm",
    )(x.reshape(M, D), g.astype(F32)[None, :])
    return out.reshape(B, L, D)


def ada_params(cond, w, b):
    m = jax.nn.silu(cond) @ w + b
    return jnp.split(m[:, None, :], 6, axis=-1)


def modulate(x, g, shift, scale):
    return rms_norm(x, g) * (1 + scale) + shift


def axial_rope(T):
    t = jnp.arange(T)
    n_freq = HEAD_DIM // 4
    inv = ROPE_BASE ** (-jnp.arange(n_freq, dtype=jnp.float32) / n_freq)
    ang = jnp.concatenate([(t // GRID_W).astype(jnp.float32)[:, None] * inv,
                           (t % GRID_W).astype(jnp.float32)[:, None] * inv], axis=-1)
    return jnp.cos(ang)[:, None, :], jnp.sin(ang)[:, None, :]


def apply_rope(x, cos, sin):
    xf = x.astype(jnp.float32)
    x1, x2 = jnp.split(xf, 2, axis=-1)
    return jnp.concatenate([x1 * cos - x2 * sin, x2 * cos + x1 * sin], axis=-1).astype(x.dtype)


def softmax_parts(parts, sink=None):
    sizes = [p.shape[-1] for p in parts]
    cols = list(parts)
    if sink is not None:
        cols.append(jnp.broadcast_to(sink, parts[0].shape[:-1] + (1,)))
    p = jax.nn.softmax(jnp.concatenate(cols, axis=-1), axis=-1)
    pieces = jnp.split(p, np.cumsum(sizes).tolist(), axis=-1)
    return pieces[:len(sizes)]


def context_attention(q, k, v, sink):
    B, S, HQ, hd = q.shape
    HK = k.shape[2]
    G = HQ // HK
    nb = S // Q_BLOCK
    scale = hd ** -0.5
    sink_b = None if sink is None else sink.astype(jnp.float32).reshape(1, HK, G, 1, 1)
    qb = q.reshape(B, nb, Q_BLOCK, HK, G, hd).swapaxes(0, 1)

    def one(qi):
        s = jnp.einsum('bqkgd,bskd->bkgqs', qi, k, preferred_element_type=jnp.float32) * scale
        (p,) = softmax_parts([s], sink_b)
        return jnp.einsum('bkgqs,bskd->bqkgd', p.astype(v.dtype), v)

    o = lax.map(one, qb)
    return o.swapaxes(0, 1).reshape(B, S, HQ, hd)


def window_attention(q, k, v, ck, cv, sink):
    B, T, HQ, hd = q.shape
    HK = k.shape[2]
    G = HQ // HK
    nb = T // A_BLOCK
    scale = hd ** -0.5
    qb = q.reshape(B, nb, A_BLOCK, HK, G, hd).swapaxes(0, 1)

    def band(x):
        xp = jnp.pad(x, ((0, 0), (A_BLOCK, A_BLOCK), (0, 0), (0, 0))).reshape(B, nb + 2, A_BLOCK, HK, hd)
        return jnp.concatenate([xp[:, :-2], xp[:, 1:-1], xp[:, 2:]], axis=2).swapaxes(0, 1)

    kb, vb = band(k), band(v)
    qpos = jnp.arange(nb)[:, None, None] * A_BLOCK + jnp.arange(A_BLOCK)[None, :, None]
    kpos = jnp.arange(nb)[:, None, None] * A_BLOCK - A_BLOCK + jnp.arange(3 * A_BLOCK)[None, None, :]
    mask = (jnp.abs(kpos - qpos) <= A_WINDOW) & (kpos >= 0) & (kpos < T)
    sink_b = sink.astype(jnp.float32).reshape(1, HK, G, 1, 1)

    def one(xs):
        qi, ki, vi, mi = xs
        s_loc = jnp.einsum('bqkgd,bskd->bkgqs', qi, ki, preferred_element_type=jnp.float32) * scale
        s_loc = jnp.where(mi[None, None, None], s_loc, NEG_INF)
        s_ctx = jnp.einsum('bqkgd,bpkd->bkgqp', qi, ck, preferred_element_type=jnp.float32) * scale
        p_loc, p_ctx = softmax_parts([s_loc, s_ctx], sink_b)
        return (jnp.einsum('bkgqs,bskd->bqkgd', p_loc.astype(vi.dtype), vi)
                + jnp.einsum('bkgqp,bpkd->bqkgd', p_ctx.astype(cv.dtype), cv))

    o = lax.map(one, (qb, kb, vb, mask))
    return o.swapaxes(0, 1).reshape(B, T, HQ, hd)


def short_conv(x, w):
    K = w.shape[0]
    L = x.shape[1]
    pad = K // 2
    xp = jnp.pad(x, ((0, 0), (pad, pad), (0, 0)))
    return sum(xp[:, i:i + L] * w[i] for i in range(K))


def hyena_filter_bank(L, w1, b1, w2, b2, w3, freq):
    f32 = jnp.float32
    t = jnp.linspace(0.0, 1.0, L, dtype=f32)[:, None]
    bands = (HY_EMB - 1) // 2
    omega = 2.0 * math.pi * jnp.arange(L, dtype=f32)[:, None] / L
    fb = jnp.linspace(1e-4, bands - 1, bands, dtype=f32)[None, :]
    feats = jnp.concatenate([t, jnp.cos(fb * omega), -jnp.sin(fb * omega)], axis=-1)
    fr = freq.astype(f32)
    h = jnp.sin(fr * (feats @ w1.astype(f32) + b1.astype(f32)))
    h = jnp.sin(fr * (h @ w2.astype(f32) + b2.astype(f32)))
    h = (h @ w3.astype(f32)).reshape(L, 2, HY_ORDER, HY_CH)
    max_decay = math.log(HY_DECAY_TARGET) / HY_FAST_DECAY
    min_decay = math.log(HY_DECAY_TARGET) / HY_SLOW_DECAY
    deltas = jnp.abs(jnp.linspace(min_decay, max_decay, HY_CH, dtype=f32))
    h = h * (jnp.exp(-t * deltas) + HY_SHIFT)[:, None, None, :]
    taps = jnp.concatenate([h[:, 0], jnp.zeros((1, HY_ORDER, HY_CH), f32), h[:0:-1, 1]], axis=0)
    taps = taps / jnp.sum(jnp.abs(taps), axis=0, keepdims=True)
    return jnp.fft.rfft(taps, axis=0)


def hyena_mixer(z, conv_w, conv_b, w1, b1, w2, b2, w3, freq, skip):
    L = z.shape[1]
    zf = (short_conv(z, conv_w) + conv_b).astype(jnp.float32)
    v, x1, x2 = jnp.split(zf, 3, axis=-1)
    filt = hyena_filter_bank(L, w1, b1, w2, b2, w3, freq)
    skip = skip.astype(jnp.float32)

    def long_conv(u, o):
        y = jnp.fft.irfft(jnp.fft.rfft(u, n=2 * L, axis=1) * filt[None, :, o], n=2 * L, axis=1)[:, :L]
        return y + u * skip[o]

    y = x1 * long_conv(v, 0)
    y = x2 * long_conv(y, 1)
    return y.astype(z.dtype)


def l2norm(x):
    xf = x.astype(jnp.float32)
    return xf * lax.rsqrt(jnp.sum(xf * xf, axis=-1, keepdims=True) + EPS)


def chunk_gated_delta(q, k, v, g, beta, s0):
    B, L, H, dk = q.shape
    dv = v.shape[-1]
    n = L // C_CHUNK

    def chunks(x):
        x = x.reshape((B, n, C_CHUNK, H) + x.shape[3:])
        return jnp.moveaxis(jnp.moveaxis(x, 1, 0), 3, 2)

    qc, kc, vc, bc = chunks(q), chunks(k), chunks(v), chunks(beta)
    gc = jnp.cumsum(chunks(g), axis=-1)
    tri = jnp.tril(jnp.ones((C_CHUNK, C_CHUNK), bool))
    strict = jnp.tril(jnp.ones((C_CHUNK, C_CHUNK), bool), k=-1)
    gamma = jnp.exp(jnp.where(tri, gc[..., :, None] - gc[..., None, :], NEG_INF))
    kb = kc * bc[..., None]
    a_mat = jnp.where(strict, jnp.einsum('nbhid,nbhjd->nbhij', kb, kc) * gamma, 0.0) + jnp.eye(C_CHUNK, dtype=jnp.float32)
    rhs = jnp.concatenate([vc * bc[..., None], kb * jnp.exp(gc)[..., None]], axis=-1)
    sol = lax.linalg.triangular_solve(a_mat, rhs, left_side=True, lower=True)
    u, w = sol[..., :dv], sol[..., dv:]
    attn = jnp.where(tri, jnp.einsum('nbhid,nbhjd->nbhij', qc, kc) * gamma, 0.0)
    g_last = gc[..., -1]
    q_dec = qc * jnp.exp(gc)[..., None]
    k_dec = kc * jnp.exp(g_last[..., None] - gc)[..., None]

    def step(S, xs):
        u_i, w_i, a_i, qd, kd, gl = xs
        v_new = u_i - jnp.einsum('bhck,bhkv->bhcv', w_i, S)
        o = jnp.einsum('bhck,bhkv->bhcv', qd, S) + jnp.einsum('bhij,bhjv->bhiv', a_i, v_new)
        S = S * jnp.exp(gl)[..., None, None] + jnp.einsum('bhck,bhcv->bhkv', kd, v_new)
        return S, o

    S, o = lax.scan(step, s0.astype(jnp.float32), (u, w, attn, q_dec, k_dec, g_last))
    o = jnp.moveaxis(jnp.moveaxis(o, 2, 3), 0, 1).reshape(B, L, H, dv)
    return o, S


def deltanet_mixer(zq, zk, zv, zg, za, zb, conv_w, a_log, dt_bias, norm_w, s0):
    B, L, _ = zq.shape
    qkv = jax.nn.silu(short_conv(jnp.concatenate([zq, zk, zv], axis=-1), conv_w))
    q, k, v = [t.reshape(B, L, C_HEADS, HEAD_DIM) for t in jnp.split(qkv, 3, axis=-1)]
    q = l2norm(q) * (HEAD_DIM ** -0.5)
    k = l2norm(k)
    v = v.astype(jnp.float32)
    beta = jax.nn.sigmoid(zb.astype(jnp.float32))
    g = -jnp.exp(a_log.astype(jnp.float32)) * jax.nn.softplus(za.astype(jnp.float32) + dt_bias.astype(jnp.float32))
    o_f, s_f = chunk_gated_delta(q, k, v, g[:, :, 0], beta[:, :, 0], s0[:, 0])
    o_b, s_b = chunk_gated_delta(q[:, ::-1], k[:, ::-1], v[:, ::-1], g[:, ::-1, 1], beta[:, ::-1, 1], s0[:, 1])
    o = o_f + o_b[:, ::-1]
    gate = jax.nn.silu(zg.reshape(B, L, C_HEADS, HEAD_DIM).astype(jnp.float32))
    o = rms_norm(o, norm_w) * gate
    return o.reshape(B, L, C_W).astype(zq.dtype), jnp.stack([s_f, s_b], axis=1)


def neighbourhood_attention(q, k, v, ck, cv, rpb):
    B, T, H, hd = q.shape
    rows = T // GRID_W
    kh = min(NA_KH_MAX, rows)
    scale = hd ** -0.5
    r = jnp.arange(rows)
    rs = jnp.clip(r - kh // 2, 0, rows - kh)
    key_rows = rs[:, None] + jnp.arange(kh)[None, :]
    idx = (key_rows[:, :, None] * GRID_W + jnp.arange(GRID_W)).reshape(rows, kh * GRID_W)
    col = jnp.arange(GRID_W)
    cs = jnp.clip(col - NA_KW // 2, 0, GRID_W - NA_KW)
    kcol = jnp.tile(col, kh)
    col_ok = (kcol[None, :] >= cs[:, None]) & (kcol[None, :] < cs[:, None] + NA_KW)
    roff = jnp.repeat(key_rows - r[:, None], GRID_W, axis=1) + NA_KH_MAX - 1
    coff = jnp.clip(kcol[None, :] - col[:, None] + NA_KW - 1, 0, 2 * NA_KW - 2)
    qr = q.reshape(B, rows, GRID_W, H, hd).swapaxes(0, 1)
    rpb_f = rpb.astype(jnp.float32)

    def one(xs):
        qi, ii, ro = xs
        ki = k[:, ii]
        vi = v[:, ii]
        bias = rpb_f[:, ro[None, :], coff]
        s_loc = jnp.einsum('bqhd,bkhd->bhqk', qi, ki, preferred_element_type=jnp.float32) * scale + bias[None]
        s_loc = jnp.where(col_ok[None, None], s_loc, NEG_INF)
        s_ctx = jnp.einsum('bqhd,bphd->bhqp', qi, ck, preferred_element_type=jnp.float32) * scale
        p_loc, p_ctx = softmax_parts([s_loc, s_ctx])
        return (jnp.einsum('bhqk,bkhd->bqhd', p_loc.astype(vi.dtype), vi)
                + jnp.einsum('bhqp,bphd->bqhd', p_ctx.astype(cv.dtype), cv))

    o = lax.map(one, (qr, idx, roff))
    return o.swapaxes(0, 1).reshape(B, T, H, hd)


def expert_choice_ffn(h, w_router, w_gate, w_up, w_down):
    B, T, D = h.shape
    cap = EC_CAPACITY * T // N_EXPERTS
    aff = jax.nn.softmax(jnp.einsum('btd,de->bte', h, w_router, preferred_element_type=jnp.float32), axis=-1)
    gate, idx = lax.top_k(aff.swapaxes(1, 2), cap)
    xg = jax.vmap(lambda hb, ib: hb[ib])(h, idx)
    a = jnp.einsum('becd,edf->becf', xg, w_gate)
    u = jnp.einsum('becd,edf->becf', xg, w_up)
    y = jnp.einsum('becf,efd->becd', jax.nn.silu(a) * u, w_down) * gate[..., None].astype(h.dtype)
    return jax.vmap(lambda yb, ib: jnp.zeros((T, D), yb.dtype).at[ib.reshape(-1)].add(yb.reshape(-1, D)))(y, idx)


def split_even(z):
    B, L = z.shape[:2]
    q = z[..., :A_Q_W].reshape(B, L, A_HEADS, HEAD_DIM)
    k = z[..., A_Q_W:A_Q_W + A_KV_W].reshape(B, L, A_KV_HEADS, HEAD_DIM)
    v = z[..., A_Q_W + A_KV_W:A_Q_W + 2 * A_KV_W].reshape(B, L, A_KV_HEADS, HEAD_DIM)
    return q, k, v, z[..., A_Q_W + 2 * A_KV_W:]


def split_odd(z):
    B, L = z.shape[:2]
    zq, zk, zv, zg = [z[..., i * C_W:(i + 1) * C_W] for i in range(4)]
    off = 4 * C_W
    za = z[..., off:off + 2 * C_HEADS].reshape(B, L, 2, C_HEADS)
    zb = z[..., off + 2 * C_HEADS:off + 4 * C_HEADS].reshape(B, L, 2, C_HEADS)
    off = off + 4 * C_HEADS
    nq, nk, nv = [z[..., off + i * D_W:off + (i + 1) * D_W].reshape(B, L, D_HEADS, HEAD_DIM) for i in range(3)]
    return zq, zk, zv, zg, za, zb, nq, nk, nv


def kernel(x_prompt, x_sample, cache_attn_k, cache_attn_v, state_delta, cache_na_k, cache_na_v,
           c, c_ctx, w_ada, b_ada, norm_mix, norm_ffn, norm_final,
           even_w_in, even_w_out, attn_sink, hy_conv_w, hy_conv_b, hy_w1, hy_b1, hy_w2, hy_b2,
           hy_w3, hy_freq, hy_skip, odd_w_in, odd_w_out, gdn_conv_w, gdn_a_log, gdn_dt_bias,
           gdn_norm, na_rpb, moe_router, moe_w_gate, moe_w_up, moe_w_down):
    xp, xs = x_prompt, x_sample
    bp = xp.shape[0]
    dft_p, dft_s = dft_tables(xp.shape[1]), dft_tables(xs.shape[1])
    ada = ada_params_all(jnp.concatenate([c_ctx[None, :], c], axis=0), w_ada, b_ada)
    new_ak, new_av, new_st, new_nk, new_nv = [], [], [], [], []
    for l in range(DEPTH):
        j = l // 2
        mp = jnp.split(ada[l, :1, None, :], 6, axis=-1)
        ms = jnp.split(ada[l, 1:, None, :], 6, axis=-1)
        mod_p = (norm_mix[l], mp[0], mp[1])
        mod_s = (norm_mix[l], ms[0], ms[1])
        if l % 2 == 0:
            def hyena(zh, tables):
                taps = hyena_taps(zh.shape[1], hy_w1[j], hy_b1[j], hy_w2[j], hy_b2[j], hy_w3[j], hy_freq[j])
                return hyena_mixer_pallas(zh, hy_conv_w[j], hy_conv_b[j], hy_skip[j], tables, hyena_spectrum(tables, *taps))

            w_in = even_w_in[j]
            w_groups = [w_in[:, :A_Q_W], w_in[:, A_Q_W:A_Q_W + A_KV_W], w_in[:, A_Q_W + A_KV_W:A_Q_W + 2 * A_KV_W],
                        w_in[:, A_Q_W + 2 * A_KV_W:]]
            q, k, v, zh = proj_multi(xp, *mod_p, w_groups, [F32] * 4)
            oa = context_attention_pallas(q, k, v, attn_sink[j], A_HEADS, A_KV_HEADS)
            xp_new = proj_concat(oa, hyena(zh, dft_p), even_w_out[j], xp, mp[2])
            new_ak.append(k.reshape(bp, SEQ, A_KV_HEADS, HEAD_DIM))
            new_av.append(v.reshape(bp, SEQ, A_KV_HEADS, HEAD_DIM))
            q, k, v, zh = proj_multi(xs, *mod_s, w_groups, [F32] * 4)
            ck = cache_attn_k[:, j].reshape(DEC_BATCH, PAST_LEN, A_KV_W).astype(BF16)
            cv = cache_attn_v[:, j].reshape(DEC_BATCH, PAST_LEN, A_KV_W).astype(BF16)
            oa = window_attention_pallas(q, k, v, ck, cv, attn_sink[j])
            xs_new = proj_concat(oa, hyena(zh, dft_s), even_w_out[j], xs, ms[2])
        else:
            w_in = odd_w_in[j]
            ab0 = 4 * C_W
            ab_cols = [w_in[:, ab0 + o * C_HEADS:ab0 + (o + 1) * C_HEADS] for o in (0, 2, 1, 3)]
            w_ab = jnp.concatenate(ab_cols + [jnp.zeros((D_MODEL, LANES - 4 * C_HEADS), w_in.dtype)], axis=1)
            n0 = ab0 + 4 * C_HEADS
            w_groups = [w_in[:, :3 * C_W], w_in[:, 3 * C_W:4 * C_W], w_ab,
                        w_in[:, n0:n0 + D_W], w_in[:, n0 + D_W:n0 + 2 * D_W], w_in[:, n0 + 2 * D_W:]]

            def deltanet(zqkv, zab, zg, s0):
                qd, kd, vd, gb = gdn_prep(zqkv, zab, gdn_conv_w[j], gdn_a_log[j], gdn_dt_bias[j])
                return gdn_scan(qd, kd, vd, gb, zg, s0, gdn_norm[j])

            zqkv, zg, zab, nq, nk, nv = proj_multi(xp, *mod_p, w_groups, [F32] * 6)
            oc, st = deltanet(zqkv, zab, zg, jnp.zeros((bp, 2, C_HEADS, HEAD_DIM, HEAD_DIM), F32))
            od = context_attention_pallas(nq, nk, nv, None, D_HEADS, D_HEADS)
            xp_new = proj_concat(oc, od, odd_w_out[j], xp, mp[2])
            new_st.append(st)
            new_nk.append(nk.reshape(bp, SEQ, D_HEADS, HEAD_DIM))
            new_nv.append(nv.reshape(bp, SEQ, D_HEADS, HEAD_DIM))
            zqkv, zg, zab, nq, nk, nv = proj_multi(xs, *mod_s, w_groups, [F32, F32, F32, BF16, BF16, BF16])
            oc, _ = deltanet(zqkv, zab, zg, state_delta[:, j])
            ck = cache_na_k[:, j].reshape(DEC_BATCH, PAST_LEN, D_W).astype(BF16)
            cv = cache_na_v[:, j].reshape(DEC_BATCH, PAST_LEN, D_W).astype(BF16)
            od = neighbourhood_attention_pallas(nq, nk, nv, ck, cv, na_rpb[j])
            xs_new = proj_concat(oc, od, odd_w_out[j], xs, ms[2])
        xp, xs = xp_new, xs_new
        moe = (moe_router[l], moe_w_gate, moe_w_up, moe_w_down, l)
        xp = moe_block(xp, norm_ffn[l], mp[3], mp[4], mp[5], *moe)
        xs = moe_block(xs, norm_ffn[l], ms[3], ms[4], ms[5], *moe)
    y_prompt = final_norm(xp, norm_final)
    y_sample = final_norm(xs, norm_final)
    return (y_prompt, y_sample, jnp.stack(new_ak, axis=1), jnp.stack(new_av, axis=1), jnp.stack(new_st, axis=1),
            jnp.stack(new_nk, axis=1), jnp.stack(new_nv, axis=1))
```

```python
import functools
import math
import jax, jax.numpy as jnp
from jax import lax
from jax.experimental import pallas as pl
from jax.experimental.pallas import tpu as pltpu

D_MODEL = 1024
SEQ = 256
DEPTH = 4
DEC_BATCH = 4
PAST_LEN = 512

GRID_W = 64
HEAD_DIM = 64
A_HEADS = D_MODEL // 128
A_KV_HEADS = A_HEADS // 4
A_WINDOW = 128
A_BLOCK = 128
ROPE_BASE = 10000.0
HY_CH = D_MODEL // 2
HY_ORDER = 2
HY_SHORT = 3
HY_EMB = 33
HY_FILT_W = 64
HY_FAST_DECAY = 0.3
HY_SLOW_DECAY = 1.5
HY_DECAY_TARGET = 1e-2
HY_SHIFT = 0.05
C_HEADS = D_MODEL // 128
C_SHORT = 3
C_CHUNK = 64
D_HEADS = D_MODEL // 128
NA_KH_MAX = 8
NA_KW = 16
N_EXPERTS = 16
EC_CAPACITY = 2
MOE_D_FF = D_MODEL
EPS = 1e-6
NEG_INF = -1e30

A_Q_W = A_HEADS * HEAD_DIM
A_KV_W = A_KV_HEADS * HEAD_DIM
C_W = C_HEADS * HEAD_DIM
D_W = D_HEADS * HEAD_DIM

VMEM_LIMIT_BYTES = 48 * 1024 * 1024


PROJ_TM = 512


def _request_of_tile(tm, L, per_request):
    return (lambda i: ((i * tm) // L, 0, 0)) if per_request else (lambda i: (0, 0, 0))


def _mm_multi_kernel(x_ref, g_ref, shift_ref, scale_ref, *refs):
    n = len(refs) // 2
    x = x_ref[...]
    y = x * lax.rsqrt(jnp.mean(x * x, axis=-1, keepdims=True) + EPS) * g_ref[...]
    h = (y * (1.0 + scale_ref[0]) + shift_ref[0]).astype(jnp.bfloat16)
    for w_ref, o_ref in zip(refs[:n], refs[n:]):
        o_ref[...] = jnp.dot(h, w_ref[...], preferred_element_type=jnp.float32).astype(o_ref.dtype)


def proj_multi(x, g, shift, scale, weights, out_dtypes, tm=PROJ_TM):
    B, L, K = x.shape
    M = B * L
    per_request = shift.shape[0] == B
    assert L % tm == 0 if per_request else M % tm == 0
    mod_spec = pl.BlockSpec((1, 1, K), _request_of_tile(tm, L, per_request))
    outs = pl.pallas_call(
        _mm_multi_kernel,
        grid=(M // tm,),
        in_specs=[pl.BlockSpec((tm, K), lambda i: (i, 0)), pl.BlockSpec((1, K), lambda i: (0, 0)), mod_spec, mod_spec]
        + [pl.BlockSpec(w.shape, lambda i: (0, 0)) for w in weights],
        out_specs=[pl.BlockSpec((tm, w.shape[1]), lambda i: (i, 0)) for w in weights],
        out_shape=[jax.ShapeDtypeStruct((M, w.shape[1]), dt) for w, dt in zip(weights, out_dtypes)],
        compiler_params=pltpu.CompilerParams(dimension_semantics=("arbitrary",), vmem_limit_bytes=VMEM_LIMIT_BYTES),
        name="in_projection",
    )(x.reshape(M, K), g.astype(jnp.float32)[None, :], shift, scale, *[w.astype(jnp.bfloat16) for w in weights])
    return [o.reshape(B, L, o.shape[1]) for o in outs]


def _mm2_kernel(a_ref, b_ref, wa_ref, wb_ref, x_ref, gate_ref, o_ref):
    mix = (jnp.dot(a_ref[...].astype(jnp.bfloat16), wa_ref[...], preferred_element_type=jnp.float32)
           + jnp.dot(b_ref[...].astype(jnp.bfloat16), wb_ref[...], preferred_element_type=jnp.float32))
    o_ref[...] = x_ref[...] + gate_ref[0] * mix


def proj_concat(a, b, w, x, gate, tm=PROJ_TM):
    B, L, Ka = a.shape
    Kb = b.shape[2]
    N = w.shape[1]
    M = B * L
    assert (L % tm == 0 if gate.shape[0] == B else M % tm == 0) and w.shape[0] == Ka + Kb
    wb16 = w.astype(jnp.bfloat16)
    out = pl.pallas_call(
        _mm2_kernel,
        grid=(M // tm,),
        in_specs=[pl.BlockSpec((tm, Ka), lambda i: (i, 0)), pl.BlockSpec((tm, Kb), lambda i: (i, 0)),
                  pl.BlockSpec((Ka, N), lambda i: (0, 0)), pl.BlockSpec((Kb, N), lambda i: (0, 0)),
                  pl.BlockSpec((tm, N), lambda i: (i, 0)),
                  pl.BlockSpec((1, 1, N), _request_of_tile(tm, L, gate.shape[0] == B))],
        out_specs=pl.BlockSpec((tm, N), lambda i: (i, 0)),
        out_shape=jax.ShapeDtypeStruct((M, N), jnp.float32),
        compiler_params=pltpu.CompilerParams(dimension_semantics=("arbitrary",), vmem_limit_bytes=VMEM_LIMIT_BYTES),
        name="out_projection",
    )(a.reshape(M, Ka), b.reshape(M, Kb), wb16[:Ka], wb16[Ka:], x.reshape(M, N), gate)
    return out.reshape(B, L, N)


LANES = 128
BF16 = jnp.bfloat16
F32 = jnp.float32


def _dot_nt(a, b):
    return lax.dot_general(a, b, (((1,), (1,)), ((), ())), preferred_element_type=F32)


def _dot(a, b):
    return jnp.dot(a, b, preferred_element_type=F32)


def _low_half(shape):
    return lax.broadcasted_iota(jnp.int32, shape, 1) < HEAD_DIM


def _softmax_pv(units):
    n = range(len(units))
    m = [functools.reduce(jnp.maximum, [s.max(axis=-1, keepdims=True) for s in units[u][0]]) for u in n]
    m = [m[u] if units[u][2] is None else jnp.maximum(m[u], units[u][2]) for u in n]
    p = [[jnp.exp(s - m[u]) for s in units[u][0]] for u in n]
    l = [functools.reduce(lambda a, b: a + b, [x.sum(axis=-1, keepdims=True) for x in p[u]]) for u in n]
    l = [l[u] if units[u][2] is None else l[u] + jnp.exp(units[u][2] - m[u]) for u in n]
    o = [functools.reduce(lambda a, b: a + b, [_dot(x.astype(BF16), v) for x, v in zip(p[u], units[u][1])]) for u in n]
    return [o[u] / l[u] for u in n]


def _place_head(q_slab, src_half, dst_half, low):
    x = q_slab if src_half == dst_half else pltpu.roll(q_slab, HEAD_DIM, axis=1)
    return jnp.where(low if dst_half == 0 else ~low, x, jnp.zeros_like(x))


NA_KEYS = NA_KH_MAX * GRID_W


def _na_kernel(q_ref, k_ref, v_ref, ck_ref, cv_ref, tab_ref, o_ref):
    r = pl.program_id(1)
    rows = k_ref.shape[1] // GRID_W
    rs = jnp.clip(r - NA_KH_MAX // 2, 0, rows - NA_KH_MAX)
    start = pl.multiple_of(rs * GRID_W, GRID_W)
    scale = HEAD_DIM ** -0.5
    low = _low_half((GRID_W, LANES))
    pairs = range(D_HEADS // 2)
    cols = [slice(p * LANES, (p + 1) * LANES) for p in pairs]
    qp = [q_ref[0, :, c] for c in cols]
    kp = [k_ref[0, pl.ds(start, NA_KEYS), c] for c in cols]
    vp = [v_ref[0, pl.ds(start, NA_KEYS), c] for c in cols]
    ckp = [ck_ref[0, :, c] for c in cols]
    cvp = [cv_ref[0, :, c] for c in cols]
    heads = [(p, half) for p in pairs for half in range(2)]
    qm = [jnp.where(low if half == 0 else ~low, qp[p], jnp.zeros_like(qp[p])) for p, half in heads]
    s_loc = [_dot_nt(qm[i], kp[p]) * scale + tab_ref[i, 0] for i, (p, _) in enumerate(heads)]
    s_ctx = [_dot_nt(qm[i], ckp[p]) * scale for i, (p, _) in enumerate(heads)]
    outs = _softmax_pv([([s_loc[i], s_ctx[i]], [vp[p], cvp[p]], None) for i, (p, _) in enumerate(heads)])
    for p in pairs:
        o_ref[0, :, cols[p]] = jnp.where(low, outs[2 * p], outs[2 * p + 1]).astype(o_ref.dtype)


def na_bias_table(rpb):
    col = jnp.arange(GRID_W)
    cs = jnp.clip(col - NA_KW // 2, 0, GRID_W - NA_KW)
    col_ok = (col[None, :] >= cs[:, None]) & (col[None, :] < cs[:, None] + NA_KW)
    coff = jnp.clip(col[None, :] - col[:, None] + NA_KW - 1, 0, 2 * NA_KW - 2)
    base = jnp.where(col_ok[None, None], rpb.astype(F32)[:, :, coff], NEG_INF)
    tab = jnp.stack([base[:, o:o + NA_KH_MAX] for o in range(NA_KH_MAX)], axis=1)
    return tab.transpose(0, 1, 3, 2, 4).reshape(D_HEADS, NA_KH_MAX, GRID_W, NA_KEYS)


def neighbourhood_attention_pallas(q, k, v, ck, cv, rpb):
    B, T, W = q.shape
    P = ck.shape[1]
    rows = T // GRID_W
    assert rows >= NA_KH_MAX and W == D_W
    tab = na_bias_table(rpb)

    def tab_index(b, r):
        rs = jnp.clip(r - NA_KH_MAX // 2, 0, rows - NA_KH_MAX)
        return (0, rs - r + NA_KH_MAX - 1, 0, 0)

    return pl.pallas_call(
        _na_kernel,
        grid=(B, rows),
        in_specs=[
            pl.BlockSpec((1, GRID_W, W), lambda b, r: (b, r, 0)),
            pl.BlockSpec((1, T, W), lambda b, r: (b, 0, 0)),
            pl.BlockSpec((1, T, W), lambda b, r: (b, 0, 0)),
            pl.BlockSpec((1, P, W), lambda b, r: (b, 0, 0)),
            pl.BlockSpec((1, P, W), lambda b, r: (b, 0, 0)),
            pl.BlockSpec((D_HEADS, 1, GRID_W, NA_KEYS), tab_index),
        ],
        out_specs=pl.BlockSpec((1, GRID_W, W), lambda b, r: (b, r, 0)),
        out_shape=jax.ShapeDtypeStruct((B, T, W), BF16),
        compiler_params=pltpu.CompilerParams(dimension_semantics=("arbitrary", "arbitrary"),
                                             vmem_limit_bytes=VMEM_LIMIT_BYTES),
        name="na_attention",
    )(q, k, v, ck, cv, tab)


def rope_tables(T):
    cos, sin = axial_rope(T)
    cos, sin = cos[:, 0, :], sin[:, 0, :]
    cos_t = jnp.concatenate([cos, cos, cos, cos], axis=-1)
    sin_t = jnp.concatenate([-sin, sin, -sin, sin], axis=-1)
    return cos_t, sin_t


def _rope(x, cos_t, sin_t):
    half = HEAD_DIM // 2
    lane = lax.broadcasted_iota(jnp.int32, x.shape, 1)
    first = (lane % HEAD_DIM) < half
    swapped = jnp.where(first, pltpu.roll(x, LANES - half, axis=1), pltpu.roll(x, half, axis=1))
    return x * cos_t + swapped * sin_t


def _win_kernel(sink_ref, q_ref, k_ref, v_ref, ck_ref, cv_ref, cos_ref, sin_ref, o_ref):
    i = pl.program_id(1)
    T = k_ref.shape[1]
    span = 3 * A_BLOCK
    start = pl.multiple_of(jnp.clip((i - 1) * A_BLOCK, 0, T - span), A_BLOCK)
    delta = i * A_BLOCK - start
    q0 = pl.multiple_of(i * A_BLOCK, A_BLOCK)
    scale = HEAD_DIM ** -0.5
    kw = _rope(k_ref[0, pl.ds(start, span), :], cos_ref[pl.ds(start, span), :], sin_ref[pl.ds(start, span), :]).astype(BF16)
    vw = v_ref[0, pl.ds(start, span), :].astype(BF16)
    ck = ck_ref[0]
    cv = cv_ref[0]
    cos_q = cos_ref[pl.ds(q0, A_BLOCK), :]
    sin_q = sin_ref[pl.ds(q0, A_BLOCK), :]
    qi = lax.broadcasted_iota(jnp.int32, (A_BLOCK, span), 0)
    kj = lax.broadcasted_iota(jnp.int32, (A_BLOCK, span), 1)
    band = jnp.abs(kj - delta - qi) <= A_WINDOW
    low = _low_half((A_BLOCK, LANES))
    group = A_HEADS // A_KV_HEADS
    pairs = range(A_HEADS // 2)
    cols = [slice(p * LANES, (p + 1) * LANES) for p in pairs]
    q_slab = [_rope(q_ref[0, :, c], cos_q, sin_q) for c in cols]
    heads = range(A_HEADS)
    kv_of = [h // group for h in heads]
    qm = [_place_head(q_slab[h // 2], h % 2, kv_of[h], low).astype(BF16) for h in heads]
    s_loc = [jnp.where(band, _dot_nt(qm[h], kw) * scale, NEG_INF) for h in heads]
    s_ctx = [_dot_nt(qm[h], ck) * scale for h in heads]
    outs = _softmax_pv([([s_loc[h], s_ctx[h]], [vw, cv], sink_ref[h]) for h in heads])
    outs = [outs[h] if kv_of[h] == h % 2 else pltpu.roll(outs[h], HEAD_DIM, axis=1) for h in heads]
    for p in pairs:
        o_ref[0, :, cols[p]] = jnp.where(low, outs[2 * p], outs[2 * p + 1]).astype(o_ref.dtype)


def window_attention_pallas(q, k, v, ck, cv, sink):
    B, T, QW = q.shape
    KW = k.shape[2]
    P = ck.shape[1]
    assert KW == LANES and QW == A_Q_W and T % A_BLOCK == 0 and T >= 3 * A_BLOCK
    cos_t, sin_t = rope_tables(T)
    return pl.pallas_call(
        _win_kernel,
        grid=(B, T // A_BLOCK),
        in_specs=[
            pl.BlockSpec(memory_space=pltpu.SMEM),
            pl.BlockSpec((1, A_BLOCK, QW), lambda b, i: (b, i, 0)),
            pl.BlockSpec((1, T, KW), lambda b, i: (b, 0, 0)),
            pl.BlockSpec((1, T, KW), lambda b, i: (b, 0, 0)),
            pl.BlockSpec((1, P, KW), lambda b, i: (b, 0, 0)),
            pl.BlockSpec((1, P, KW), lambda b, i: (b, 0, 0)),
            pl.BlockSpec((T, LANES), lambda b, i: (0, 0)),
            pl.BlockSpec((T, LANES), lambda b, i: (0, 0)),
        ],
        out_specs=pl.BlockSpec((1, A_BLOCK, QW), lambda b, i: (b, i, 0)),
        out_shape=jax.ShapeDtypeStruct((B, T, QW), BF16),
        compiler_params=pltpu.CompilerParams(dimension_semantics=("arbitrary", "arbitrary"),
                                             vmem_limit_bytes=VMEM_LIMIT_BYTES),
        name="window_attention",
    )(sink.astype(F32), q, k, v, ck, cv, cos_t, sin_t)


def _ctx_kernel(sink_ref, q_ref, k_ref, v_ref, o_ref, *, n_q_heads, n_kv_heads, use_sink):
    S = q_ref.shape[1]
    scale = HEAD_DIM ** -0.5
    low = _low_half((S, LANES))
    group = n_q_heads // n_kv_heads
    pairs = range(n_q_heads // 2)
    cols = [slice(p * LANES, (p + 1) * LANES) for p in pairs]
    q_slab = [q_ref[0, :, c] for c in cols]
    heads = range(n_q_heads)
    kv_of = [h // group for h in heads]
    kcols = [slice((kv // 2) * LANES, (kv // 2 + 1) * LANES) for kv in kv_of]
    qm = [_place_head(q_slab[h // 2], h % 2, kv_of[h] % 2, low).astype(BF16) for h in heads]
    s = [_dot_nt(qm[h], k_ref[0, :, kcols[h]].astype(BF16)) * scale for h in heads]
    outs = _softmax_pv([([s[h]], [v_ref[0, :, kcols[h]].astype(BF16)], sink_ref[h] if use_sink else None) for h in heads])
    outs = [outs[h] if kv_of[h] % 2 == h % 2 else pltpu.roll(outs[h], HEAD_DIM, axis=1) for h in heads]
    for p in pairs:
        o_ref[0, :, cols[p]] = jnp.where(low, outs[2 * p], outs[2 * p + 1]).astype(o_ref.dtype)


def context_attention_pallas(q, k, v, sink, n_q_heads, n_kv_heads):
    B, S, QW = q.shape
    KW = k.shape[2]
    use_sink = sink is not None
    sink_arr = sink.astype(F32) if use_sink else jnp.zeros((n_q_heads,), F32)
    return pl.pallas_call(
        functools.partial(_ctx_kernel, n_q_heads=n_q_heads, n_kv_heads=n_kv_heads, use_sink=use_sink),
        grid=(B,),
        in_specs=[
            pl.BlockSpec(memory_space=pltpu.SMEM),
            pl.BlockSpec((1, S, QW), lambda b: (b, 0, 0)),
            pl.BlockSpec((1, S, KW), lambda b: (b, 0, 0)),
            pl.BlockSpec((1, S, KW), lambda b: (b, 0, 0)),
        ],
        out_specs=pl.BlockSpec((1, S, QW), lambda b: (b, 0, 0)),
        out_shape=jax.ShapeDtypeStruct((B, S, QW), BF16),
        compiler_params=pltpu.CompilerParams(dimension_semantics=("arbitrary",), vmem_limit_bytes=VMEM_LIMIT_BYTES),
        name="context_attention",
    )(sink_arr, q, k, v)


HIGHEST = lax.Precision.HIGHEST
GDN_TM = 256
GDN_REQS = 2
GDN_SUB = 4
SUBLANES = 8


def _head_pair_sum_matrix():
    a = lax.broadcasted_iota(jnp.int32, (LANES, LANES), 0) // HEAD_DIM
    b = lax.broadcasted_iota(jnp.int32, (LANES, LANES), 1) // HEAD_DIM
    return (a == b).astype(BF16)


def _head_sums(x, pmat):
    hi = x.astype(BF16)
    lo = (x - hi.astype(F32)).astype(BF16)
    return _dot(hi, pmat) + _dot(lo, pmat)


def _gdn_prep_kernel(x_ref, prev_ref, next_ref, ab_ref, cw_ref, a_ref, dtb_ref, q_ref, k_ref, v_ref, gb_ref):
    i = pl.program_id(1)
    n = pl.num_programs(1)
    x = x_ref[0]
    tm = x.shape[0]
    row = lax.broadcasted_iota(jnp.int32, x.shape, 0)
    prev_row = jnp.where(i > 0, prev_ref[0, SUBLANES - 1:SUBLANES, :], 0.0)
    next_row = jnp.where(i < n - 1, next_ref[0, 0:1, :], 0.0)
    x_prev = jnp.where(row == 0, prev_row, pltpu.roll(x, 1, axis=0))
    x_next = jnp.where(row == tm - 1, next_row, pltpu.roll(x, tm - 1, axis=0))
    y = x_prev * cw_ref[0:1, :] + x * cw_ref[1:2, :] + x_next * cw_ref[2:3, :]
    y = y * jax.nn.sigmoid(y)
    pmat = _head_pair_sum_matrix()
    for p in range(C_W // LANES):
        qs = y[:, p * LANES:(p + 1) * LANES]
        ks = y[:, C_W + p * LANES:C_W + (p + 1) * LANES]
        q_ref[0, :, p * LANES:(p + 1) * LANES] = qs * lax.rsqrt(_head_sums(qs * qs, pmat) + EPS) * (HEAD_DIM ** -0.5)
        k_ref[0, :, p * LANES:(p + 1) * LANES] = ks * lax.rsqrt(_head_sums(ks * ks, pmat) + EPS)
    v_ref[0] = y[:, 2 * C_W:]
    ab = ab_ref[0, :, 0:4 * C_HEADS]
    lane = lax.broadcasted_iota(jnp.int32, ab.shape, 1)
    is_beta = (lane // C_HEADS) % 2 == 1
    t = ab + dtb_ref[...]
    softplus = jnp.maximum(t, 0.0) + jnp.log1p(jnp.exp(-jnp.abs(t)))
    gb = jnp.where(is_beta, jax.nn.sigmoid(ab), -jnp.exp(a_ref[...]) * softplus)
    gb_ref[0, 0] = gb[:, 0:2 * C_HEADS]
    gb_ref[0, 1] = gb[:, 2 * C_HEADS:4 * C_HEADS]


def gdn_prep(zqkv, zab, conv_w, a_log, dt_bias):
    B, L, W3 = zqkv.shape
    tm = GDN_TM
    assert L % tm == 0
    nb = tm // SUBLANES
    zero = jnp.zeros((C_HEADS,), F32)
    a_lane = jnp.concatenate([a_log[0], zero, a_log[1], zero]).astype(F32)[None, :]
    dtb_lane = jnp.concatenate([dt_bias[0], zero, dt_bias[1], zero]).astype(F32)[None, :]
    outs = pl.pallas_call(
        _gdn_prep_kernel,
        grid=(B, L // tm),
        in_specs=[
            pl.BlockSpec((1, tm, W3), lambda b, i: (b, i, 0)),
            pl.BlockSpec((1, SUBLANES, W3), lambda b, i: (b, jnp.maximum(i * nb - 1, 0), 0)),
            pl.BlockSpec((1, SUBLANES, W3), lambda b, i: (b, jnp.minimum((i + 1) * nb, L // SUBLANES - 1), 0)),
            pl.BlockSpec((1, tm, LANES), lambda b, i: (b, i, 0)),
            pl.BlockSpec((C_SHORT, W3), lambda b, i: (0, 0)),
            pl.BlockSpec((1, 4 * C_HEADS), lambda b, i: (0, 0)),
            pl.BlockSpec((1, 4 * C_HEADS), lambda b, i: (0, 0)),
        ],
        out_specs=[
            pl.BlockSpec((1, tm, C_W), lambda b, i: (b, i, 0)),
            pl.BlockSpec((1, tm, C_W), lambda b, i: (b, i, 0)),
            pl.BlockSpec((1, tm, C_W), lambda b, i: (b, i, 0)),
            pl.BlockSpec((1, 2, tm, 2 * C_HEADS), lambda b, i: (b, 0, i, 0)),
        ],
        out_shape=[jax.ShapeDtypeStruct((B, L, C_W), F32)] * 3 + [jax.ShapeDtypeStruct((B, 2, L, 2 * C_HEADS), F32)],
        compiler_params=pltpu.CompilerParams(dimension_semantics=("arbitrary", "arbitrary"),
                                             vmem_limit_bytes=VMEM_LIMIT_BYTES),
        name="gdn_prep",
    )(zqkv, zqkv, zqkv, zab, conv_w.astype(F32), a_lane, dtb_lane)
    return outs


def _gdn_kernel(q_ref, k_ref, v_ref, gb_ref, zg_ref, s0_ref, nw_ref, o_ref, st_ref, s_scr, of_scr):
    d = pl.program_id(1)
    c = pl.program_id(2)
    n = pl.num_programs(2)
    C = C_CHUNK
    fwd = d == 0
    nsub = q_ref.shape[1] // C
    block = jnp.where(fwd, c, n - 1 - c)

    @pl.when(c == 0)
    def _():
        s_scr[...] = s0_ref[:, 0]

    row = lax.broadcasted_iota(jnp.int32, (C, C), 0)
    col = lax.broadcasted_iota(jnp.int32, (C, C), 1)
    ahead = jnp.where(fwd, row - col, col - row)
    incl = ahead >= 0
    strict = ahead > 0
    incl_f = incl.astype(F32)
    nreq = q_ref.shape[0]
    units = [(r, h) for r in range(nreq) for h in range(C_HEADS)]
    s_cur = [s_scr[r, h] for r, h in units]
    pmat = _head_pair_sum_matrix()
    done = []
    for j in range(nsub):
        sub = jnp.where(fwd, j, nsub - 1 - j)
        rows = pl.ds(pl.multiple_of(sub * C, C), C)
        r0 = pl.multiple_of((block * nsub + sub) * C, C)
        gc, gc_t, g_last, beta = [], [], [], []
        for r in range(nreq):
            g = gb_ref[r, 0, rows, 0:C_HEADS]
            beta.append(gb_ref[r, 0, rows, C_HEADS:2 * C_HEADS])
            gc.append(jnp.dot(incl_f, g, precision=HIGHEST, preferred_element_type=F32))
            gc_t.append(gc[r].T)
            g_last.append(jnp.where(fwd, gc[r][C - 1:C, :], gc[r][0:1, :]))
        H = range(len(units))
        heads = [slice(h * HEAD_DIM, (h + 1) * HEAD_DIM) for _, h in units]
        s_old = s_cur
        qs = [q_ref[r, rows, heads[i]] for i, (r, _) in enumerate(units)]
        ks = [k_ref[r, rows, heads[i]] for i, (r, _) in enumerate(units)]
        vs = [v_ref[r, rows, heads[i]] for i, (r, _) in enumerate(units)]
        gcol = [gc[r][:, h:h + 1] for r, h in units]
        bcol = [beta[r][:, h:h + 1] for r, h in units]
        gl = [g_last[r][:, h:h + 1] for r, h in units]
        gamma = [jnp.exp(jnp.where(incl, gcol[i] - gc_t[r][h:h + 1, :], NEG_INF)) for i, (r, h) in enumerate(units)]
        egc = [jnp.exp(gcol[h]) for h in H]
        kb = [ks[h].astype(BF16) for h in H]
        nmat = [jnp.where(strict, _dot_nt(kb[h], kb[h]) * gamma[h], 0.0) * bcol[h] for h in H]
        attn = [(_dot_nt(qs[h].astype(BF16), kb[h]) * gamma[h]).astype(BF16) for h in H]
        xr = row ^ col
        eye = (row == col).astype(F32)
        tinv = [eye - jnp.where((xr >> 1) == 0, nmat[h], 0.0) for h in H]
        for lvl in range(1, 6):
            off_diag = (xr >> lvl) == 1
            wmat = [_dot(jnp.where(off_diag, nmat[h], 0.0).astype(BF16), tinv[h].astype(BF16)).astype(BF16) for h in H]
            tinv = [tinv[h] - _dot(tinv[h].astype(BF16), wmat[h]) for h in H]
        x = [jnp.concatenate([vs[h] * bcol[h], ks[h] * (bcol[h] * egc[h])], axis=1) for h in H]
        x = [x[h] + _dot((tinv[h] - eye).astype(BF16), x[h].astype(BF16)) for h in H]
        sb = [s_old[h].astype(BF16) for h in H]
        v_new = [x[h][:, :HEAD_DIM] - _dot(x[h][:, HEAD_DIM:].astype(BF16), sb[h]) for h in H]
        vb = [v_new[h].astype(BF16) for h in H]
        outs = [_dot((qs[h] * egc[h]).astype(BF16), sb[h]) + _dot(attn[h], vb[h]) for h in H]
        kd = [(ks[h] * jnp.exp(gl[h] - gcol[h])).astype(BF16) for h in H]
        s_new = [s_old[h] * jnp.exp(gl[h]) + lax.dot_general(kd[h], vb[h], (((0,), (0,)), ((), ())), preferred_element_type=F32)
                 for h in H]
        s_cur = s_new
        o = [jnp.concatenate(outs[r * C_HEADS:(r + 1) * C_HEADS], axis=1) for r in range(nreq)]
        done.append((rows, r0, o))

    for i, (r, h) in enumerate(units):
        s_scr[r, h] = s_cur[i]

    @pl.when(fwd)
    def _():
        for rows, r0, o in done:
            for r in range(nreq):
                of_scr[r, pl.ds(r0, C), :] = o[r]

    @pl.when(jnp.logical_not(fwd))
    def _():
        for rows, r0, o in done:
            for r in range(nreq):
                tot = of_scr[r, pl.ds(r0, C), :] + o[r]
                zg = zg_ref[r, rows, :]
                gate = zg * jax.nn.sigmoid(zg)
                for p in range(C_W // LANES):
                    cols = slice(p * LANES, (p + 1) * LANES)
                    t = tot[:, cols]
                    ms = _head_sums(t * t, pmat) * (1.0 / HEAD_DIM)
                    o_ref[r, rows, cols] = (t * lax.rsqrt(ms + EPS) * nw_ref[:, cols] * gate[:, cols]).astype(o_ref.dtype)

    @pl.when(c == n - 1)
    def _():
        st_ref[:, 0] = s_scr[...]


def gdn_scan(q, k, v, gb, zg, s0, norm_w):
    B, L, W = q.shape
    C = GDN_SUB * C_CHUNK
    n = L // C
    R = min(B, GDN_REQS)
    assert L % C == 0 and W == C_W and B % R == 0
    chunk_of = lambda d, c: jnp.where(d == 0, c, n - 1 - c)
    seq_spec = pl.BlockSpec((R, C, W), lambda b, d, c: (b, chunk_of(d, c), 0))
    state_spec = pl.BlockSpec((R, 1, C_HEADS, HEAD_DIM, HEAD_DIM), lambda b, d, c: (b, d, 0, 0, 0))
    nw = jnp.tile(norm_w.astype(F32), C_HEADS)[None, :]
    return pl.pallas_call(
        _gdn_kernel,
        grid=(B // R, 2, n),
        in_specs=[
            seq_spec, seq_spec, seq_spec,
            pl.BlockSpec((R, 1, C, 2 * C_HEADS), lambda b, d, c: (b, d, chunk_of(d, c), 0)),
            seq_spec,
            state_spec,
            pl.BlockSpec((1, W), lambda b, d, c: (0, 0)),
        ],
        out_specs=[
            pl.BlockSpec((R, C, W), lambda b, d, c: (b, jnp.where(d == 0, n - 1, n - 1 - c), 0)),
            state_spec,
        ],
        out_shape=[jax.ShapeDtypeStruct((B, L, W), BF16), jax.ShapeDtypeStruct(s0.shape, F32)],
        scratch_shapes=[pltpu.VMEM((R, C_HEADS, HEAD_DIM, HEAD_DIM), F32), pltpu.VMEM((R, L, W), F32)],
        compiler_params=pltpu.CompilerParams(dimension_semantics=("arbitrary", "arbitrary", "arbitrary"),
                                             vmem_limit_bytes=VMEM_LIMIT_BYTES),
        name="gdn_scan",
    )(q, k, v, gb, zg, s0.astype(F32), nw)


MOE_TT = 512
MOE_RT = 512
MOE_BISECT_STEPS = 64
MOE_SELECT_REQS = 4
MOE_SELECT_REQS_LONG = 2


def _moe_router_kernel(x_ref, g_ref, shift_ref, scale_ref, wr_ref, h_ref, aff_ref):
    x = x_ref[0]
    y = x * lax.rsqrt(jnp.mean(x * x, axis=-1, keepdims=True) + EPS) * g_ref[...]
    h = y * (1.0 + scale_ref[0]) + shift_ref[0]
    h_hi = h.astype(BF16)
    h_ref[0] = h_hi
    h_lo = (h - h_hi.astype(F32)).astype(BF16)
    w = wr_ref[...]
    w_hi = w.astype(BF16)
    w_lo = (w - w_hi.astype(F32)).astype(BF16)
    logits = _dot(h_hi, w_hi) + _dot(h_hi, w_lo) + _dot(h_lo, w_hi)
    lane = lax.broadcasted_iota(jnp.int32, logits.shape, 1)
    logits = jnp.where(lane < N_EXPERTS, logits, NEG_INF)
    e = jnp.exp(logits - logits.max(axis=-1, keepdims=True))
    aff_ref[0] = e / e.sum(axis=-1, keepdims=True)


def _moe_select_kernel(aff_ref, slot_ref, slot_t_ref, start_ref, *, cap):
    nreq, T = aff_ref.shape[0], aff_ref.shape[1]
    reqs = range(nreq)
    affs = [aff_ref[q] for q in reqs]

    def bisect(i, bounds):
        lo, hi = bounds
        mid = [0.5 * (lo[q] + hi[q]) for q in reqs]
        cnt = [jnp.sum((affs[q] >= mid[q]).astype(F32), axis=0, keepdims=True) for q in reqs]
        take = [cnt[q] >= cap for q in reqs]
        return (tuple(jnp.where(take[q], mid[q], lo[q]) for q in reqs),
                tuple(jnp.where(take[q], hi[q], mid[q]) for q in reqs))

    init = (tuple(jnp.zeros((1, LANES), F32) for _ in reqs), tuple(jnp.full((1, LANES), 2.0, F32) for _ in reqs))
    _, hi = lax.fori_loop(0, MOE_BISECT_STEPS, bisect, init)
    thr = [jnp.max(jnp.where(affs[q] < hi[q], affs[q], -1.0), axis=0, keepdims=True) for q in reqs]
    blk = min(T, MOE_TT)
    r = lax.broadcasted_iota(jnp.int32, (blk, blk), 0)
    c = lax.broadcasted_iota(jnp.int32, (blk, blk), 1)
    before = (c < r).astype(BF16)
    n_tiles = T // blk
    for q in reqs:
        gt = (affs[q] > thr[q]).astype(F32)
        eq = (affs[q] == thr[q]).astype(F32)
        need = cap - jnp.sum(gt, axis=0, keepdims=True)
        carry_gt = jnp.zeros((1, LANES), F32)
        carry_eq = jnp.zeros((1, LANES), F32)
        start_ref[q] = jnp.zeros(start_ref.shape[1:], F32)
        for b in range(n_tiles):
            rows = slice(b * blk, (b + 1) * blk)
            gt_b, eq_b = gt[rows], eq[rows]
            pos_gt = _dot(before, gt_b.astype(BF16)) + carry_gt
            pos_eq = _dot(before, eq_b.astype(BF16)) + carry_eq
            chosen = gt_b + eq_b * (pos_eq < need).astype(F32)
            slots = jnp.where(chosen > 0.5, pos_gt + jnp.minimum(pos_eq, need), -1.0)
            slot_ref[q, rows, :] = slots
            slot_t_ref[q, :, rows] = slots.T[:N_EXPERTS, :]
            carry_gt = carry_gt + jnp.sum(gt_b, axis=0, keepdims=True)
            carry_eq = carry_eq + jnp.sum(eq_b, axis=0, keepdims=True)
            start_ref[q, b + 1:b + 2, :] = carry_gt + jnp.minimum(carry_eq, need)


def _expert_column(a, e):
    lane = lax.broadcasted_iota(jnp.int32, a.shape, 1)
    return jnp.sum(jnp.where(lane == e, a, 0.0), axis=1, keepdims=True)


MOE_START_ROWS = 16
MOE_WIN = 128


def _one_hot_slots(slot_col, first, width):
    s = lax.broadcasted_iota(jnp.int32, (slot_col.shape[0], width), 1).astype(F32)
    return (slot_col - first == s).astype(BF16)


MOE_SLOT_ALIGN = 16


def _slot_windows(start_ref, b, e, k):
    base = (b * N_EXPERTS + e) * MOE_START_ROWS + k
    lo, hi = start_ref[base], start_ref[base + 1]
    first = (lo // MOE_SLOT_ALIGN) * MOE_SLOT_ALIGN
    return first, jnp.where(hi > lo, (hi - first + MOE_WIN - 1) // MOE_WIN, 0)


def _window_one_hot(col, first, w, cap):
    nominal = first + w * MOE_WIN
    s0 = pl.multiple_of(jnp.minimum(nominal, cap - MOE_WIN), MOE_SLOT_ALIGN)
    rel = jnp.where(col >= nominal.astype(F32), col - s0.astype(F32), -1.0)
    return s0, _one_hot_slots(rel, 0.0, MOE_WIN)


def _moe_gather_kernel(start_ref, slot_t_ref, h_ref, xg_ref, acc_ref):
    b, e = pl.program_id(0), pl.program_id(1)
    cap = acc_ref.shape[0]
    acc_ref[...] = jnp.zeros_like(acc_ref)
    slot_id = lax.broadcasted_iota(jnp.int32, (MOE_WIN, MOE_TT), 0).astype(F32)
    for k in range(slot_t_ref.shape[2] // MOE_TT):
        rows = slice(k * MOE_TT, (k + 1) * MOE_TT)
        row = slot_t_ref[0, pl.ds(e, 1), rows]
        first, count = _slot_windows(start_ref, b, e, k)

        def window(w, carry, row=row, rows=rows, first=first):
            nominal = first + w * MOE_WIN
            s0 = pl.multiple_of(jnp.minimum(nominal, cap - MOE_WIN), MOE_SLOT_ALIGN)
            rel = jnp.where(row >= nominal.astype(F32), row - s0.astype(F32), -1.0)
            p = (slot_id == rel).astype(BF16)
            acc_ref[pl.ds(s0, MOE_WIN), :] += _dot(p, h_ref[0, rows, :])
            return carry

        lax.fori_loop(0, count, window, 0)
    xg_ref[0] = acc_ref[...].astype(xg_ref.dtype)


def _all_expert_slots(slot, cap):
    width = N_EXPERTS * cap
    spread = (lax.broadcasted_iota(jnp.int32, (LANES, width), 1) // cap
              == lax.broadcasted_iota(jnp.int32, (LANES, width), 0)).astype(BF16)
    slot_wide = _dot(slot.astype(BF16), spread)
    lane_slot = (lax.broadcasted_iota(jnp.int32, (slot.shape[0], width), 1) % cap).astype(F32)
    return spread, slot_wide == lane_slot


def _moe_gather_short_kernel(slot_ref, h_ref, xg_ref, *, cap):
    _, chosen = _all_expert_slots(slot_ref[0], cap)
    rows = lax.dot_general(chosen.astype(BF16), h_ref[0], (((0,), (0,)), ((), ())), preferred_element_type=F32)
    for e in range(N_EXPERTS):
        xg_ref[e] = rows[e * cap:(e + 1) * cap].astype(xg_ref.dtype)


def _moe_ffn_kernel(x_ref, wg_ref, wu_ref, wd_ref, y_ref, wg_s, wu_s, wd_s):
    @pl.when(pl.program_id(1) == 0)
    def _():
        wg_s[...] = wg_ref[0].astype(BF16)
        wu_s[...] = wu_ref[0].astype(BF16)
        wd_s[...] = wd_ref[0].astype(BF16)

    x = x_ref[0]
    a = _dot(x, wg_s[...])
    u = _dot(x, wu_s[...])
    hid = (a * jax.nn.sigmoid(a) * u).astype(BF16)
    y_ref[0] = _dot(hid, wd_s[...]).astype(y_ref.dtype)


def _moe_scatter_kernel(start_ref, slot_ref, aff_ref, y_ref, x_ref, gate_ref, o_ref, acc_ref):
    b, k = pl.program_id(0), pl.program_id(1)
    acc_ref[...] = jnp.zeros_like(acc_ref)
    slot, aff = slot_ref[0], aff_ref[0]
    for e in range(N_EXPERTS):
        col, weight = slot[:, e:e + 1], aff[:, e:e + 1]
        first, count = _slot_windows(start_ref, b, e, k)

        def window(w, carry, e=e, col=col, weight=weight, first=first):
            s0, pt = _window_one_hot(col, first, w, y_ref.shape[1])
            acc_ref[...] += weight * _dot(pt, y_ref[e, pl.ds(s0, MOE_WIN), :])
            return carry

        lax.fori_loop(0, count, window, 0)
    o_ref[0] = x_ref[0] + gate_ref[0] * acc_ref[...]


def _moe_scatter_short_kernel(slot_ref, aff_ref, y_ref, x_ref, gate_ref, o_ref, *, cap):
    spread, chosen = _all_expert_slots(slot_ref[0], cap)
    rest = aff_ref[0]
    aff_wide = jnp.zeros(chosen.shape, F32)
    for _ in range(3):
        part = rest.astype(BF16)
        aff_wide = aff_wide + _dot(part, spread)
        rest = rest - part.astype(F32)
    weighted = jnp.where(chosen, aff_wide, 0.0)
    w_hi = weighted.astype(BF16)
    w_lo = (weighted - w_hi.astype(F32)).astype(BF16)
    y_all = jnp.concatenate([y_ref[e] for e in range(N_EXPERTS)], axis=0)
    o_ref[0] = x_ref[0] + gate_ref[0] * (_dot(w_hi, y_all) + _dot(w_lo, y_all))


def moe_route(x, g, shift, scale, w_router):
    B, T, D = x.shape
    tt = min(T, MOE_TT)
    cap = EC_CAPACITY * T // N_EXPERTS
    per_request = shift.shape[0] == B
    mod_spec = pl.BlockSpec((1, 1, D), (lambda b, k: (b, 0, 0)) if per_request else (lambda b, k: (0, 0, 0)))
    wr = jnp.concatenate([w_router.astype(F32), jnp.zeros((D, LANES - N_EXPERTS), F32)], axis=1)
    h, aff = pl.pallas_call(
        _moe_router_kernel,
        grid=(B, T // tt),
        in_specs=[pl.BlockSpec((1, tt, D), lambda b, k: (b, k, 0)), pl.BlockSpec((1, D), lambda b, k: (0, 0)),
                  mod_spec, mod_spec, pl.BlockSpec((D, LANES), lambda b, k: (0, 0))],
        out_specs=[pl.BlockSpec((1, tt, D), lambda b, k: (b, k, 0)), pl.BlockSpec((1, tt, LANES), lambda b, k: (b, k, 0))],
        out_shape=[jax.ShapeDtypeStruct((B, T, D), BF16), jax.ShapeDtypeStruct((B, T, LANES), F32)],
        compiler_params=pltpu.CompilerParams(dimension_semantics=("arbitrary", "arbitrary"),
                                             vmem_limit_bytes=VMEM_LIMIT_BYTES),
        name="moe_router",
    )(x, g.astype(F32)[None, :], shift, scale, wr)
    assert T // tt < MOE_START_ROWS
    rq = MOE_SELECT_REQS if T <= MOE_TT else MOE_SELECT_REQS_LONG
    rq = rq if B % rq == 0 else 1
    slot, slot_t, start = pl.pallas_call(
        functools.partial(_moe_select_kernel, cap=cap),
        grid=(B // rq,),
        in_specs=[pl.BlockSpec((rq, T, LANES), lambda b: (b, 0, 0))],
        out_specs=[pl.BlockSpec((rq, T, LANES), lambda b: (b, 0, 0)),
                   pl.BlockSpec((rq, N_EXPERTS, T), lambda b: (b, 0, 0)),
                   pl.BlockSpec((rq, MOE_START_ROWS, LANES), lambda b: (b, 0, 0))],
        out_shape=[jax.ShapeDtypeStruct((B, T, LANES), F32), jax.ShapeDtypeStruct((B, N_EXPERTS, T), F32),
                   jax.ShapeDtypeStruct((B, MOE_START_ROWS, LANES), F32)],
        compiler_params=pltpu.CompilerParams(dimension_semantics=("arbitrary",), vmem_limit_bytes=VMEM_LIMIT_BYTES),
        name="moe_select",
    )(aff)
    start = start[:, :, :N_EXPERTS].astype(jnp.int32).transpose(0, 2, 1).reshape(-1)
    return h, aff, slot, slot_t, start


def moe_gather(h, slot, slot_t, start):
    B, T, D = h.shape
    cap = EC_CAPACITY * T // N_EXPERTS
    out_shape = jax.ShapeDtypeStruct((N_EXPERTS, B * cap, D), BF16)
    if T <= MOE_TT:
        return pl.pallas_call(
            functools.partial(_moe_gather_short_kernel, cap=cap),
            grid=(B,),
            in_specs=[pl.BlockSpec((1, T, LANES), lambda b: (b, 0, 0)), pl.BlockSpec((1, T, D), lambda b: (b, 0, 0))],
            out_specs=pl.BlockSpec((N_EXPERTS, cap, D), lambda b: (0, b, 0)),
            out_shape=out_shape,
            compiler_params=pltpu.CompilerParams(dimension_semantics=("arbitrary",), vmem_limit_bytes=VMEM_LIMIT_BYTES),
            name="moe_gather_short",
        )(slot, h)
    assert cap % MOE_WIN == 0 and T % MOE_TT == 0
    return pl.pallas_call(
        _moe_gather_kernel,
        grid_spec=pltpu.PrefetchScalarGridSpec(
            num_scalar_prefetch=1,
            grid=(B, N_EXPERTS),
            in_specs=[pl.BlockSpec((1, N_EXPERTS, T), lambda b, e, st: (b, 0, 0)),
                      pl.BlockSpec((1, T, D), lambda b, e, st: (b, 0, 0))],
            out_specs=pl.BlockSpec((1, cap, D), lambda b, e, st: (e, b, 0)),
            scratch_shapes=[pltpu.VMEM((cap, D), F32)],
        ),
        out_shape=out_shape,
        compiler_params=pltpu.CompilerParams(dimension_semantics=("arbitrary",) * 2, vmem_limit_bytes=VMEM_LIMIT_BYTES),
        name="moe_gather",
    )(start, slot_t, h)


def moe_ffn(xg, w_gate, w_up, w_down, layer):
    E, R, D = xg.shape
    rt = min(R, MOE_RT)
    w_spec = pl.BlockSpec((None, 1, D, MOE_D_FF), lambda e, r: (layer, e, 0, 0))
    return pl.pallas_call(
        _moe_ffn_kernel,
        grid=(E, R // rt),
        in_specs=[pl.BlockSpec((1, rt, D), lambda e, r: (e, r, 0)), w_spec, w_spec,
                  pl.BlockSpec((None, 1, MOE_D_FF, D), lambda e, r: (layer, e, 0, 0))],
        out_specs=pl.BlockSpec((1, rt, D), lambda e, r: (e, r, 0)),
        out_shape=jax.ShapeDtypeStruct((E, R, D), BF16),
        scratch_shapes=[pltpu.VMEM((D, MOE_D_FF), BF16), pltpu.VMEM((D, MOE_D_FF), BF16), pltpu.VMEM((MOE_D_FF, D), BF16)],
        compiler_params=pltpu.CompilerParams(dimension_semantics=("arbitrary", "arbitrary"),
                                             vmem_limit_bytes=VMEM_LIMIT_BYTES),
        name="moe_ffn",
    )(xg, w_gate, w_up, w_down)


def moe_scatter(slot, aff, start, y, x, gate):
    B, T, D = x.shape
    cap = EC_CAPACITY * T // N_EXPERTS
    per_request = gate.shape[0] == B
    out_shape = jax.ShapeDtypeStruct((B, T, D), F32)
    if T <= MOE_TT:
        whole = lambda b: (b, 0, 0)
        return pl.pallas_call(
            functools.partial(_moe_scatter_short_kernel, cap=cap),
            grid=(B,),
            in_specs=[pl.BlockSpec((1, T, LANES), whole), pl.BlockSpec((1, T, LANES), whole),
                      pl.BlockSpec((N_EXPERTS, cap, D), lambda b: (0, b, 0)), pl.BlockSpec((1, T, D), whole),
                      pl.BlockSpec((1, 1, D), whole if per_request else (lambda b: (0, 0, 0)))],
            out_specs=pl.BlockSpec((1, T, D), whole),
            out_shape=out_shape,
            compiler_params=pltpu.CompilerParams(dimension_semantics=("arbitrary",), vmem_limit_bytes=VMEM_LIMIT_BYTES),
            name="moe_scatter_short",
        )(slot, aff, y, x, gate)
    tt = MOE_TT
    tile = lambda b, k, st: (b, k, 0)
    return pl.pallas_call(
        _moe_scatter_kernel,
        grid_spec=pltpu.PrefetchScalarGridSpec(
            num_scalar_prefetch=1,
            grid=(B, T // tt),
            in_specs=[pl.BlockSpec((1, tt, LANES), tile), pl.BlockSpec((1, tt, LANES), tile),
                      pl.BlockSpec((N_EXPERTS, cap, D), lambda b, k, st: (0, b, 0), pipeline_mode=pl.Buffered(1)),
                      pl.BlockSpec((1, tt, D), tile),
                      pl.BlockSpec((1, 1, D), (lambda b, k, st: (b, 0, 0)) if per_request else (lambda b, k, st: (0, 0, 0)))],
            out_specs=pl.BlockSpec((1, tt, D), tile),
            scratch_shapes=[pltpu.VMEM((tt, D), F32)],
        ),
        out_shape=out_shape,
        compiler_params=pltpu.CompilerParams(dimension_semantics=("arbitrary",) * 2, vmem_limit_bytes=VMEM_LIMIT_BYTES),
        name="moe_scatter",
    )(start, slot, aff, y, x, gate)


def moe_block(x, g, shift, scale, gate, w_router, w_gate, w_up, w_down, layer):
    h, aff, slot, slot_t, start = moe_route(x, g, shift, scale, w_router)
    y = moe_ffn(moe_gather(h, slot, slot_t, start), w_gate, w_up, w_down, layer)
    return moe_scatter(slot, aff, start, y, x, gate)


HY_TM = 512
HY_TK = 1024
HY_BG = 8
HY_BG_LONG = 2


def dft_tables(L):
    blk = min(L, HY_TM)
    t = jnp.arange(L, dtype=jnp.int32)

    def angles(f):
        return ((f[:, None] * t[None, :]) % (2 * L)).astype(F32) * (math.pi / L)

    a_hi = angles(jnp.arange(0, L, blk, dtype=jnp.int32))[:, None, :]
    a_lo = angles(jnp.arange(blk, dtype=jnp.int32))[None, :, :]
    cos_t = (jnp.cos(a_hi) * jnp.cos(a_lo) - jnp.sin(a_hi) * jnp.sin(a_lo)).reshape(L, L)
    sin_t = (jnp.sin(a_hi) * jnp.cos(a_lo) + jnp.cos(a_hi) * jnp.sin(a_lo)).reshape(L, L)
    return cos_t.astype(BF16), sin_t.astype(BF16)


def _alt_sign(rows, first_row):
    t = lax.broadcasted_iota(jnp.int32, (rows, 1), 0) + first_row
    return jnp.where(t % 2 == 0, 1.0, -1.0).astype(F32)


def _hy_prep_kernel(x_ref, prev_ref, next_ref, cw_ref, cb_ref, o_ref, v16_ref):
    i = pl.program_id(1)
    n = pl.num_programs(1)
    x = x_ref[0]
    tm = x.shape[0]
    row = lax.broadcasted_iota(jnp.int32, x.shape, 0)
    prev_row = jnp.where(i > 0, prev_ref[0, SUBLANES - 1:SUBLANES, :], 0.0)
    next_row = jnp.where(i < n - 1, next_ref[0, 0:1, :], 0.0)
    x_prev = jnp.where(row == 0, prev_row, pltpu.roll(x, 1, axis=0))
    x_next = jnp.where(row == tm - 1, next_row, pltpu.roll(x, tm - 1, axis=0))
    y = x_prev * cw_ref[0:1, :] + x * cw_ref[1:2, :] + x_next * cw_ref[2:3, :] + cb_ref[...]
    o_ref[0] = y
    v16_ref[0] = y[:, :HY_CH].astype(BF16)


def hyena_prep(zh, conv_w, conv_b):
    B, L, W = zh.shape
    tm = min(L, GDN_TM)
    nb = tm // SUBLANES
    return pl.pallas_call(
        _hy_prep_kernel,
        grid=(B, L // tm),
        in_specs=[
            pl.BlockSpec((1, tm, W), lambda b, i: (b, i, 0)),
            pl.BlockSpec((1, SUBLANES, W), lambda b, i: (b, jnp.maximum(i * nb - 1, 0), 0)),
            pl.BlockSpec((1, SUBLANES, W), lambda b, i: (b, jnp.minimum((i + 1) * nb, L // SUBLANES - 1), 0)),
            pl.BlockSpec((HY_SHORT, W), lambda b, i: (0, 0)),
            pl.BlockSpec((1, W), lambda b, i: (0, 0)),
        ],
        out_specs=[pl.BlockSpec((1, tm, W), lambda b, i: (b, i, 0)), pl.BlockSpec((1, tm, HY_CH), lambda b, i: (b, i, 0))],
        out_shape=[jax.ShapeDtypeStruct((B, L, W), F32), jax.ShapeDtypeStruct((B, L, HY_CH), BF16)],
        compiler_params=pltpu.CompilerParams(dimension_semantics=("arbitrary", "arbitrary"),
                                             vmem_limit_bytes=VMEM_LIMIT_BYTES),
        name="hyena_prep",
    )(zh, zh, zh, conv_w.astype(F32), conv_b.astype(F32)[None, :])


def _hy_taps_kernel(feat_ref, w1_ref, b1_ref, w2_ref, b2_ref, fr_ref, w3f_ref, w3b_ref, dec_ref, sum_ref, dif_ref):
    L = feat_ref.shape[0]
    fr = fr_ref[...]
    h = jnp.sin(fr * (jnp.dot(feat_ref[...], w1_ref[...], precision=HIGHEST, preferred_element_type=F32) + b1_ref[...]))
    h = jnp.sin(fr * (jnp.dot(h, w2_ref[...], precision=HIGHEST, preferred_element_type=F32) + b2_ref[...]))
    t = lax.broadcasted_iota(jnp.int32, (L, 1), 0)
    window = jnp.exp(-(t.astype(F32) / (L - 1)) * dec_ref[...]) + HY_SHIFT
    fwd = jnp.dot(h, w3f_ref[...], precision=HIGHEST, preferred_element_type=F32) * window
    bwd = jnp.where(t == 0, 0.0, jnp.dot(h, w3b_ref[...], precision=HIGHEST, preferred_element_type=F32) * window)
    inv = 1.0 / (jnp.sum(jnp.abs(fwd), axis=0, keepdims=True) + jnp.sum(jnp.abs(bwd), axis=0, keepdims=True))
    sum_ref[...] = (fwd + bwd) * inv
    dif_ref[...] = (bwd - fwd) * inv


def hyena_taps(L, w1, b1, w2, b2, w3, freq):
    f32 = F32
    t = jnp.linspace(0.0, 1.0, L, dtype=f32)[:, None]
    bands = (HY_EMB - 1) // 2
    omega = 2.0 * math.pi * jnp.arange(L, dtype=f32)[:, None] / L
    fb = jnp.linspace(1e-4, bands - 1, bands, dtype=f32)[None, :]
    feats = jnp.concatenate([t, jnp.cos(fb * omega), -jnp.sin(fb * omega)], axis=-1)
    max_decay = math.log(HY_DECAY_TARGET) / HY_FAST_DECAY
    min_decay = math.log(HY_DECAY_TARGET) / HY_SLOW_DECAY
    deltas = jnp.abs(jnp.linspace(min_decay, max_decay, HY_CH, dtype=f32))
    dec = jnp.tile(deltas, HY_ORDER)[None, :]
    n_col = HY_ORDER * HY_CH
    cb = 256
    full = lambda shape: pl.BlockSpec(shape, lambda j: (0, 0))
    col = lambda rows: pl.BlockSpec((rows, cb), lambda j: (0, j))
    return pl.pallas_call(
        _hy_taps_kernel,
        grid=(n_col // cb,),
        in_specs=[full((L, HY_EMB)), full((HY_EMB, HY_FILT_W)), full((1, HY_FILT_W)), full((HY_FILT_W, HY_FILT_W)),
                  full((1, HY_FILT_W)), full((1, HY_FILT_W)), col(HY_FILT_W), col(HY_FILT_W), col(1)],
        out_specs=[col(L), col(L)],
        out_shape=[jax.ShapeDtypeStruct((L, n_col), f32)] * 2,
        compiler_params=pltpu.CompilerParams(dimension_semantics=("arbitrary",), vmem_limit_bytes=VMEM_LIMIT_BYTES),
        name="hyena_taps",
    )(feats, w1.astype(f32), b1.astype(f32)[None, :], w2.astype(f32), b2.astype(f32)[None, :], freq.astype(f32)[None, :],
      w3.astype(f32)[:, :n_col], w3.astype(f32)[:, n_col:], dec)


def _hy_spec_kernel(ch_ref, sh_ref, sum_ref, dif_ref, hr_ref, hi_ref, ny_ref, acc_r, acc_i, acc_n):
    m, k = pl.program_id(1), pl.program_id(2)
    tk = sum_ref.shape[0]

    @pl.when(k == 0)
    def _():
        acc_r[...] = jnp.zeros_like(acc_r)
        acc_i[...] = jnp.zeros_like(acc_i)

    @pl.when(jnp.logical_and(k == 0, m == 0))
    def _():
        acc_n[...] = jnp.zeros_like(acc_n)

    a = sum_ref[...]
    acc_r[...] += _dot(ch_ref[...], a.astype(BF16))
    acc_i[...] += _dot(sh_ref[...], dif_ref[...].astype(BF16))

    @pl.when(m == 0)
    def _():
        acc_n[...] += jnp.sum(a * _alt_sign(tk, k * tk), axis=0, keepdims=True)

    @pl.when(k == pl.num_programs(2) - 1)
    def _():
        hr_ref[...] = acc_r[...]
        hi_ref[...] = acc_i[...]
        ny_ref[...] = jnp.broadcast_to(acc_n[...], ny_ref.shape)


def hyena_spectrum(tables, tap_sum, tap_dif):
    L, N = tap_sum.shape
    tm = min(L, HY_TM)
    tk = min(L, 2 * HY_TK)
    cb = 512
    tab = pl.BlockSpec((tm, tk), lambda j, m, k: (m, k))
    dat = pl.BlockSpec((tk, cb), lambda j, m, k: (k, j))
    return pl.pallas_call(
        _hy_spec_kernel,
        grid=(N // cb, L // tm, L // tk),
        in_specs=[tab, tab, dat, dat],
        out_specs=[pl.BlockSpec((tm, cb), lambda j, m, k: (m, j)), pl.BlockSpec((tm, cb), lambda j, m, k: (m, j)),
                   pl.BlockSpec((SUBLANES, cb), lambda j, m, k: (0, j))],
        out_shape=[jax.ShapeDtypeStruct((L, N), F32), jax.ShapeDtypeStruct((L, N), F32),
                   jax.ShapeDtypeStruct((SUBLANES, N), F32)],
        scratch_shapes=[pltpu.VMEM((tm, cb), F32), pltpu.VMEM((tm, cb), F32), pltpu.VMEM((1, cb), F32)],
        compiler_params=pltpu.CompilerParams(dimension_semantics=("arbitrary",) * 3, vmem_limit_bytes=VMEM_LIMIT_BYTES),
        name="hyena_spectrum",
    )(*tables, tap_sum, tap_dif)


def _hy_fwd_kernel(ch_ref, sh_ref, u_ref, hr_ref, hi_ref, hny_ref, yr_ref, yi_ref, yny_ref,
                   acc_c, acc_s, acc_n):
    m, k = pl.program_id(1), pl.program_id(2)
    nb, tk = u_ref.shape[0], u_ref.shape[1]
    tm = acc_c.shape[1]

    @pl.when(k == 0)
    def _():
        acc_c[...] = jnp.zeros_like(acc_c)
        acc_s[...] = jnp.zeros_like(acc_s)

    @pl.when(jnp.logical_and(k == 0, m == 0))
    def _():
        acc_n[...] = jnp.zeros_like(acc_n)

    sign = _alt_sign(tk, k * tk)
    for b in range(nb):
        ub = u_ref[b]
        acc_c[b] += _dot(ch_ref[...], ub)
        acc_s[b] += _dot(sh_ref[...], ub)

        @pl.when(m == 0)
        def _():
            acc_n[b] += jnp.sum(ub.astype(F32) * sign, axis=0, keepdims=True)

    @pl.when(k == pl.num_programs(2) - 1)
    def _():
        f = lax.broadcasted_iota(jnp.int32, (tm, 1), 0) + m * tm
        dc = jnp.where(f == 0, 0.5, 1.0).astype(F32)
        hr, hi = hr_ref[...], hi_ref[...]
        for b in range(nb):
            xr, xs = acc_c[b], acc_s[b]
            yr_ref[b] = ((xr * hr + xs * hi) * dc).astype(yr_ref.dtype)
            yi_ref[b] = (xr * hi - xs * hr).astype(yi_ref.dtype)
            yny_ref[b] = jnp.broadcast_to(acc_n[b] * hny_ref[0:1, :], yny_ref.shape[1:])


def _hy_inv_kernel(ch_ref, sh_ref, yr_ref, yi_ref, yny_ref, u_ref, xg_ref, skip_ref, *rest):
    o_refs, acc = rest[:-1], rest[-1]
    m, k = pl.program_id(1), pl.program_id(2)
    nb = yr_ref.shape[0]
    tm = acc.shape[1]
    L = tm * pl.num_programs(1)

    @pl.when(k == 0)
    def _():
        acc[...] = jnp.zeros_like(acc)

    for b in range(nb):
        acc[b] += _dot(ch_ref[...], yr_ref[b]) - _dot(sh_ref[...], yi_ref[b])

    @pl.when(k == pl.num_programs(2) - 1)
    def _():
        sign = _alt_sign(tm, m * tm)
        for b in range(nb):
            u = u_ref[b]
            y = acc[b] * (1.0 / L) + sign * yny_ref[b, 0:1, :] * (0.5 / L)
            res = xg_ref[b] * (y + u * skip_ref[...])
            for o_ref in o_refs:
                o_ref[b] = res.astype(o_ref.dtype)


def hyena_long_conv(tables, zf, u, u16, u_col, gate_col, hr, hi, hny, order, skip, out_dtypes):
    B, L, _ = u.shape
    C = HY_CH
    tm = min(L, HY_TM)
    tk = min(L, HY_TK)
    bg = min(B, HY_BG_LONG if L >= HY_TM else HY_BG)
    assert B % bg == 0
    grid = (B // bg, L // tm, L // tk)
    tab = pl.BlockSpec((tm, tk), lambda g, m, k: (m, k))
    params = pltpu.CompilerParams(dimension_semantics=("arbitrary",) * 3, vmem_limit_bytes=VMEM_LIMIT_BYTES)
    spec_m = pl.BlockSpec((tm, C), lambda g, m, k: (m, order))
    yr, yi, yny = pl.pallas_call(
        _hy_fwd_kernel,
        grid=grid,
        in_specs=[tab, tab, pl.BlockSpec((bg, tk, C), lambda g, m, k: (g, k, 0)), spec_m, spec_m,
                  pl.BlockSpec((SUBLANES, C), lambda g, m, k: (0, order))],
        out_specs=[pl.BlockSpec((bg, tm, C), lambda g, m, k: (g, m, 0)), pl.BlockSpec((bg, tm, C), lambda g, m, k: (g, m, 0)),
                   pl.BlockSpec((bg, SUBLANES, C), lambda g, m, k: (g, 0, 0))],
        out_shape=[jax.ShapeDtypeStruct((B, L, C), BF16), jax.ShapeDtypeStruct((B, L, C), BF16),
                   jax.ShapeDtypeStruct((B, SUBLANES, C), F32)],
        scratch_shapes=[pltpu.VMEM((bg, tm, C), F32), pltpu.VMEM((bg, tm, C), F32), pltpu.VMEM((bg, 1, C), F32)],
        compiler_params=params,
        name="hyena_fwd",
    )(*tables, u16, hr, hi, hny)
    return pl.pallas_call(
        _hy_inv_kernel,
        grid=grid,
        in_specs=[tab, tab, pl.BlockSpec((bg, tk, C), lambda g, m, k: (g, k, 0)),
                  pl.BlockSpec((bg, tk, C), lambda g, m, k: (g, k, 0)),
                  pl.BlockSpec((bg, SUBLANES, C), lambda g, m, k: (g, 0, 0)),
                  pl.BlockSpec((bg, tm, C), lambda g, m, k: (g, m, u_col)),
                  pl.BlockSpec((bg, tm, C), lambda g, m, k: (g, m, gate_col)),
                  pl.BlockSpec((1, C), lambda g, m, k: (0, 0))],
        out_specs=[pl.BlockSpec((bg, tm, C), lambda g, m, k: (g, m, 0)) for _ in out_dtypes],
        out_shape=[jax.ShapeDtypeStruct((B, L, C), dt) for dt in out_dtypes],
        scratch_shapes=[pltpu.VMEM((bg, tm, C), F32)],
        compiler_params=params,
        name="hyena_inv",
    )(*tables, yr, yi, yny, u, zf, skip.astype(F32)[order][None, :])


def hyena_mixer_pallas(zh, conv_w, conv_b, skip, tables, spectrum):
    hr, hi, hny = spectrum
    zf, v16 = hyena_prep(zh, conv_w, conv_b)
    y1, y1_16 = hyena_long_conv(tables, zf, zf, v16, 0, 1, hr, hi, hny, 0, skip, [F32, BF16])
    return hyena_long_conv(tables, zf, y1, y1_16, 0, 2, hr, hi, hny, 1, skip, [BF16])[0]


def _ada_kernel(c_ref, w_ref, b_ref, o_ref):
    cnd = c_ref[...]
    act = cnd * jax.nn.sigmoid(cnd)
    o_ref[0] = jnp.dot(act, w_ref[0], precision=HIGHEST, preferred_element_type=F32) + b_ref[0]


def ada_params_all(cond, w_ada, b_ada):
    N, D = cond.shape
    depth, _, W = w_ada.shape
    rows = -(-N // SUBLANES) * SUBLANES
    cond_p = jnp.concatenate([cond.astype(F32), jnp.zeros((rows - N, D), F32)], axis=0)
    cb = D
    out = pl.pallas_call(
        _ada_kernel,
        grid=(depth, W // cb),
        in_specs=[pl.BlockSpec((rows, D), lambda l, j: (0, 0)), pl.BlockSpec((1, D, cb), lambda l, j: (l, 0, j)),
                  pl.BlockSpec((1, 1, cb), lambda l, j: (l, 0, j))],
        out_specs=pl.BlockSpec((1, rows, cb), lambda l, j: (l, 0, j)),
        out_shape=jax.ShapeDtypeStruct((depth, rows, W), F32),
        compiler_params=pltpu.CompilerParams(dimension_semantics=("arbitrary", "arbitrary"),
                                             vmem_limit_bytes=VMEM_LIMIT_BYTES),
        name="ada_params",
    )(cond_p, w_ada.astype(F32), b_ada.astype(F32)[:, None, :])
    return out[:, :N]


def _final_norm_kernel(x_ref, g_ref, o_ref):
    x = x_ref[...]
    o_ref[...] = x * lax.rsqrt(jnp.mean(x * x, axis=-1, keepdims=True) + EPS) * g_ref[...]


def final_norm(x, g, tm=512):
    B, L, D = x.shape
    M = B * L
    assert M % tm == 0
    out = pl.pallas_call(
        _final_norm_kernel,
        grid=(M // tm,),
        in_specs=[pl.BlockSpec((tm, D), lambda i: (i, 0)), pl.BlockSpec((1, D), lambda i: (0, 0))],
        out_specs=pl.BlockSpec((tm, D), lambda i: (i, 0)),
        out_shape=jax.ShapeDtypeStruct((M, D), F32),
        compiler_params=pltpu.CompilerParams(dimension_semantics=("arbitrary",), vmem_limit_bytes=VMEM_LIMIT_BYTES),
        name="final_norm",
    )(x.reshape(M, D), g.astype(F32)[None, :])
    return out.reshape(B, L, D)


def axial_rope(T):
    t = jnp.arange(T)
    n_freq = HEAD_DIM // 4
    inv = ROPE_BASE ** (-jnp.arange(n_freq, dtype=jnp.float32) / n_freq)
    ang = jnp.concatenate([(t // GRID_W).astype(jnp.float32)[:, None] * inv,
                           (t % GRID_W).astype(jnp.float32)[:, None] * inv], axis=-1)
    return jnp.cos(ang)[:, None, :], jnp.sin(ang)[:, None, :]


def kernel(x_prompt, x_sample, cache_attn_k, cache_attn_v, state_delta, cache_na_k, cache_na_v,
           c, c_ctx, w_ada, b_ada, norm_mix, norm_ffn, norm_final,
           even_w_in, even_w_out, attn_sink, hy_conv_w, hy_conv_b, hy_w1, hy_b1, hy_w2, hy_b2,
           hy_w3, hy_freq, hy_skip, odd_w_in, odd_w_out, gdn_conv_w, gdn_a_log, gdn_dt_bias,
           gdn_norm, na_rpb, moe_router, moe_w_gate, moe_w_up, moe_w_down):
    xp, xs = x_prompt, x_sample
    bp = xp.shape[0]
    dft_p, dft_s = dft_tables(xp.shape[1]), dft_tables(xs.shape[1])
    ada = ada_params_all(jnp.concatenate([c_ctx[None, :], c], axis=0), w_ada, b_ada)
    new_ak, new_av, new_st, new_nk, new_nv = [], [], [], [], []
    for l in range(DEPTH):
        j = l // 2
        mp = jnp.split(ada[l, :1, None, :], 6, axis=-1)
        ms = jnp.split(ada[l, 1:, None, :], 6, axis=-1)
        mod_p = (norm_mix[l], mp[0], mp[1])
        mod_s = (norm_mix[l], ms[0], ms[1])
        if l % 2 == 0:
            def hyena(zh, tables):
                taps = hyena_taps(zh.shape[1], hy_w1[j], hy_b1[j], hy_w2[j], hy_b2[j], hy_w3[j], hy_freq[j])
                return hyena_mixer_pallas(zh, hy_conv_w[j], hy_conv_b[j], hy_skip[j], tables, hyena_spectrum(tables, *taps))

            w_in = even_w_in[j]
            w_groups = [w_in[:, :A_Q_W], w_in[:, A_Q_W:A_Q_W + A_KV_W], w_in[:, A_Q_W + A_KV_W:A_Q_W + 2 * A_KV_W],
                        w_in[:, A_Q_W + 2 * A_KV_W:]]
            q, k, v, zh = proj_multi(xp, *mod_p, w_groups, [F32] * 4)
            oa = context_attention_pallas(q, k, v, attn_sink[j], A_HEADS, A_KV_HEADS)
            xp_new = proj_concat(oa, hyena(zh, dft_p), even_w_out[j], xp, mp[2])
            new_ak.append(k.reshape(bp, SEQ, A_KV_HEADS, HEAD_DIM))
            new_av.append(v.reshape(bp, SEQ, A_KV_HEADS, HEAD_DIM))
            q, k, v, zh = proj_multi(xs, *mod_s, w_groups, [F32] * 4)
            ck = cache_attn_k[:, j].reshape(DEC_BATCH, PAST_LEN, A_KV_W).astype(BF16)
            cv = cache_attn_v[:, j].reshape(DEC_BATCH, PAST_LEN, A_KV_W).astype(BF16)
            oa = window_attention_pallas(q, k, v, ck, cv, attn_sink[j])
            xs_new = proj_concat(oa, hyena(zh, dft_s), even_w_out[j], xs, ms[2])
        else:
            w_in = odd_w_in[j]
            ab0 = 4 * C_W
            ab_cols = [w_in[:, ab0 + o * C_HEADS:ab0 + (o + 1) * C_HEADS] for o in (0, 2, 1, 3)]
            w_ab = jnp.concatenate(ab_cols + [jnp.zeros((D_MODEL, LANES - 4 * C_HEADS), w_in.dtype)], axis=1)
            n0 = ab0 + 4 * C_HEADS
            w_groups = [w_in[:, :3 * C_W], w_in[:, 3 * C_W:4 * C_W], w_ab,
                        w_in[:, n0:n0 + D_W], w_in[:, n0 + D_W:n0 + 2 * D_W], w_in[:, n0 + 2 * D_W:]]

            def deltanet(zqkv, zab, zg, s0):
                qd, kd, vd, gb = gdn_prep(zqkv, zab, gdn_conv_w[j], gdn_a_log[j], gdn_dt_bias[j])
                return gdn_scan(qd, kd, vd, gb, zg, s0, gdn_norm[j])

            zqkv, zg, zab, nq, nk, nv = proj_multi(xp, *mod_p, w_groups, [F32] * 6)
            oc, st = deltanet(zqkv, zab, zg, jnp.zeros((bp, 2, C_HEADS, HEAD_DIM, HEAD_DIM), F32))
            od = context_attention_pallas(nq, nk, nv, None, D_HEADS, D_HEADS)
            xp_new = proj_concat(oc, od, odd_w_out[j], xp, mp[2])
            new_st.append(st)
            new_nk.append(nk.reshape(bp, SEQ, D_HEADS, HEAD_DIM))
            new_nv.append(nv.reshape(bp, SEQ, D_HEADS, HEAD_DIM))
            zqkv, zg, zab, nq, nk, nv = proj_multi(xs, *mod_s, w_groups, [F32, F32, F32, BF16, BF16, BF16])
            oc, _ = deltanet(zqkv, zab, zg, state_delta[:, j])
            ck = cache_na_k[:, j].reshape(DEC_BATCH, PAST_LEN, D_W).astype(BF16)
            cv = cache_na_v[:, j].reshape(DEC_BATCH, PAST_LEN, D_W).astype(BF16)
            od = neighbourhood_attention_pallas(nq, nk, nv, ck, cv, na_rpb[j])
            xs_new = proj_concat(oc, od, odd_w_out[j], xs, ms[2])
        xp, xs = xp_new, xs_new
        moe = (moe_router[l], moe_w_gate, moe_w_up, moe_w_down, l)
        xp = moe_block(xp, norm_ffn[l], mp[3], mp[4], mp[5], *moe)
        xs = moe_block(xs, norm_ffn[l], ms[3], ms[4], ms[5], *moe)
    y_prompt = final_norm(xp, norm_final)
    y_sample = final_norm(xs, norm_final)
    return (y_prompt, y_sample, jnp.stack(new_ak, axis=1), jnp.stack(new_av, axis=1), jnp.stack(new_st, axis=1),
            jnp.stack(new_nk, axis=1), jnp.stack(new_nv, axis=1))
```

```python
import functools
import math
import jax, jax.numpy as jnp
from jax import lax
from jax.experimental import pallas as pl
from jax.experimental.pallas import tpu as pltpu

D_MODEL = 1024
SEQ = 256
DEPTH = 4
DEC_BATCH = 4
PAST_LEN = 512

GRID_W = 64
HEAD_DIM = 64
A_HEADS = D_MODEL // 128
A_KV_HEADS = A_HEADS // 4
A_WINDOW = 128
A_BLOCK = 128
ROPE_BASE = 10000.0
HY_CH = D_MODEL // 2
HY_ORDER = 2
HY_SHORT = 3
HY_EMB = 33
HY_FILT_W = 64
HY_FAST_DECAY = 0.3
HY_SLOW_DECAY = 1.5
HY_DECAY_TARGET = 1e-2
HY_SHIFT = 0.05
C_HEADS = D_MODEL // 128
C_SHORT = 3
C_CHUNK = 64
D_HEADS = D_MODEL // 128
NA_KH_MAX = 8
NA_KW = 16
N_EXPERTS = 16
EC_CAPACITY = 2
MOE_D_FF = D_MODEL
EPS = 1e-6
NEG_INF = -1e30

A_Q_W = A_HEADS * HEAD_DIM
A_KV_W = A_KV_HEADS * HEAD_DIM
C_W = C_HEADS * HEAD_DIM
D_W = D_HEADS * HEAD_DIM

VMEM_LIMIT_BYTES = 48 * 1024 * 1024


PROJ_TM = 512


def _request_of_tile(tm, L, per_request):
    return (lambda i: ((i * tm) // L, 0, 0)) if per_request else (lambda i: (0, 0, 0))


def _mm_multi_kernel(x_ref, g_ref, shift_ref, scale_ref, *refs):
    n = len(refs) // 2
    x = x_ref[...]
    y = x * lax.rsqrt(jnp.mean(x * x, axis=-1, keepdims=True) + EPS) * g_ref[...]
    h = (y * (1.0 + scale_ref[0]) + shift_ref[0]).astype(jnp.bfloat16)
    for w_ref, o_ref in zip(refs[:n], refs[n:]):
        o_ref[...] = jnp.dot(h, w_ref[...], preferred_element_type=jnp.float32).astype(o_ref.dtype)


def proj_multi(x, g, shift, scale, weights, out_dtypes, tm=PROJ_TM):
    B, L, K = x.shape
    M = B * L
    per_request = shift.shape[0] == B
    assert L % tm == 0 if per_request else M % tm == 0
    mod_spec = pl.BlockSpec((1, 1, K), _request_of_tile(tm, L, per_request))
    outs = pl.pallas_call(
        _mm_multi_kernel,
        grid=(M // tm,),
        in_specs=[pl.BlockSpec((tm, K), lambda i: (i, 0)), pl.BlockSpec((1, K), lambda i: (0, 0)), mod_spec, mod_spec]
        + [pl.BlockSpec(w.shape, lambda i: (0, 0)) for w in weights],
        out_specs=[pl.BlockSpec((tm, w.shape[1]), lambda i: (i, 0)) for w in weights],
        out_shape=[jax.ShapeDtypeStruct((M, w.shape[1]), dt) for w, dt in zip(weights, out_dtypes)],
        compiler_params=pltpu.CompilerParams(dimension_semantics=("arbitrary",), vmem_limit_bytes=VMEM_LIMIT_BYTES),
        name="in_projection",
    )(x.reshape(M, K), g.astype(jnp.float32)[None, :], shift, scale, *[w.astype(jnp.bfloat16) for w in weights])
    return [o.reshape(B, L, o.shape[1]) for o in outs]


def _mm2_kernel(a_ref, b_ref, wa_ref, wb_ref, x_ref, gate_ref, o_ref):
    mix = (jnp.dot(a_ref[...].astype(jnp.bfloat16), wa_ref[...], preferred_element_type=jnp.float32)
           + jnp.dot(b_ref[...].astype(jnp.bfloat16), wb_ref[...], preferred_element_type=jnp.float32))
    o_ref[...] = x_ref[...] + gate_ref[0] * mix


def proj_concat(a, b, w, x, gate, tm=PROJ_TM):
    B, L, Ka = a.shape
    Kb = b.shape[2]
    N = w.shape[1]
    M = B * L
    assert (L % tm == 0 if gate.shape[0] == B else M % tm == 0) and w.shape[0] == Ka + Kb
    wb16 = w.astype(jnp.bfloat16)
    out = pl.pallas_call(
        _mm2_kernel,
        grid=(M // tm,),
        in_specs=[pl.BlockSpec((tm, Ka), lambda i: (i, 0)), pl.BlockSpec((tm, Kb), lambda i: (i, 0)),
                  pl.BlockSpec((Ka, N), lambda i: (0, 0)), pl.BlockSpec((Kb, N), lambda i: (0, 0)),
                  pl.BlockSpec((tm, N), lambda i: (i, 0)),
                  pl.BlockSpec((1, 1, N), _request_of_tile(tm, L, gate.shape[0] == B))],
        out_specs=pl.BlockSpec((tm, N), lambda i: (i, 0)),
        out_shape=jax.ShapeDtypeStruct((M, N), jnp.float32),
        compiler_params=pltpu.CompilerParams(dimension_semantics=("arbitrary",), vmem_limit_bytes=VMEM_LIMIT_BYTES),
        name="out_projection",
    )(a.reshape(M, Ka), b.reshape(M, Kb), wb16[:Ka], wb16[Ka:], x.reshape(M, N), gate)
    return out.reshape(B, L, N)


LANES = 128
BF16 = jnp.bfloat16
F32 = jnp.float32


def _dot_nt(a, b):
    return lax.dot_general(a, b, (((1,), (1,)), ((), ())), preferred_element_type=F32)


def _dot(a, b):
    return jnp.dot(a, b, preferred_element_type=F32)


def _low_half(shape):
    return lax.broadcasted_iota(jnp.int32, shape, 1) < HEAD_DIM


def _softmax_pv(units):
    n = range(len(units))
    m = [functools.reduce(jnp.maximum, [s.max(axis=-1, keepdims=True) for s in units[u][0]]) for u in n]
    m = [m[u] if units[u][2] is None else jnp.maximum(m[u], units[u][2]) for u in n]
    p = [[jnp.exp(s - m[u]) for s in units[u][0]] for u in n]
    l = [functools.reduce(lambda a, b: a + b, [x.sum(axis=-1, keepdims=True) for x in p[u]]) for u in n]
    l = [l[u] if units[u][2] is None else l[u] + jnp.exp(units[u][2] - m[u]) for u in n]
    o = [functools.reduce(lambda a, b: a + b, [_dot(x.astype(BF16), v) for x, v in zip(p[u], units[u][1])]) for u in n]
    return [o[u] / l[u] for u in n]


def _place_head(q_slab, src_half, dst_half, low):
    x = q_slab if src_half == dst_half else pltpu.roll(q_slab, HEAD_DIM, axis=1)
    return jnp.where(low if dst_half == 0 else ~low, x, jnp.zeros_like(x))


NA_KEYS = NA_KH_MAX * GRID_W


def _na_kernel(q_ref, k_ref, v_ref, ck_ref, cv_ref, tab_ref, o_ref):
    r = pl.program_id(1)
    rows = k_ref.shape[1] // GRID_W
    rs = jnp.clip(r - NA_KH_MAX // 2, 0, rows - NA_KH_MAX)
    start = pl.multiple_of(rs * GRID_W, GRID_W)
    scale = HEAD_DIM ** -0.5
    low = _low_half((GRID_W, LANES))
    pairs = range(D_HEADS // 2)
    cols = [slice(p * LANES, (p + 1) * LANES) for p in pairs]
    qp = [q_ref[0, :, c] for c in cols]
    kp = [k_ref[0, pl.ds(start, NA_KEYS), c] for c in cols]
    vp = [v_ref[0, pl.ds(start, NA_KEYS), c] for c in cols]
    ckp = [ck_ref[0, :, c] for c in cols]
    cvp = [cv_ref[0, :, c] for c in cols]
    heads = [(p, half) for p in pairs for half in range(2)]
    qm = [jnp.where(low if half == 0 else ~low, qp[p], jnp.zeros_like(qp[p])) for p, half in heads]
    s_loc = [_dot_nt(qm[i], kp[p]) * scale + tab_ref[i, 0] for i, (p, _) in enumerate(heads)]
    s_ctx = [_dot_nt(qm[i], ckp[p]) * scale for i, (p, _) in enumerate(heads)]
    outs = _softmax_pv([([s_loc[i], s_ctx[i]], [vp[p], cvp[p]], None) for i, (p, _) in enumerate(heads)])
    for p in pairs:
        o_ref[0, :, cols[p]] = jnp.where(low, outs[2 * p], outs[2 * p + 1]).astype(o_ref.dtype)


def na_bias_table(rpb):
    col = jnp.arange(GRID_W)
    cs = jnp.clip(col - NA_KW // 2, 0, GRID_W - NA_KW)
    col_ok = (col[None, :] >= cs[:, None]) & (col[None, :] < cs[:, None] + NA_KW)
    coff = jnp.clip(col[None, :] - col[:, None] + NA_KW - 1, 0, 2 * NA_KW - 2)
    base = jnp.where(col_ok[None, None], rpb.astype(F32)[:, :, coff], NEG_INF)
    tab = jnp.stack([base[:, o:o + NA_KH_MAX] for o in range(NA_KH_MAX)], axis=1)
    return tab.transpose(0, 1, 3, 2, 4).reshape(D_HEADS, NA_KH_MAX, GRID_W, NA_KEYS)


def neighbourhood_attention_pallas(q, k, v, ck, cv, rpb):
    B, T, W = q.shape
    P = ck.shape[1]
    rows = T // GRID_W
    assert rows >= NA_KH_MAX and W == D_W
    tab = na_bias_table(rpb)

    def tab_index(b, r):
        rs = jnp.clip(r - NA_KH_MAX // 2, 0, rows - NA_KH_MAX)
        return (0, rs - r + NA_KH_MAX - 1, 0, 0)

    return pl.pallas_call(
        _na_kernel,
        grid=(B, rows),
        in_specs=[
            pl.BlockSpec((1, GRID_W, W), lambda b, r: (b, r, 0)),
            pl.BlockSpec((1, T, W), lambda b, r: (b, 0, 0)),
            pl.BlockSpec((1, T, W), lambda b, r: (b, 0, 0)),
            pl.BlockSpec((1, P, W), lambda b, r: (b, 0, 0)),
            pl.BlockSpec((1, P, W), lambda b, r: (b, 0, 0)),
            pl.BlockSpec((D_HEADS, 1, GRID_W, NA_KEYS), tab_index),
        ],
        out_specs=pl.BlockSpec((1, GRID_W, W), lambda b, r: (b, r, 0)),
        out_shape=jax.ShapeDtypeStruct((B, T, W), BF16),
        compiler_params=pltpu.CompilerParams(dimension_semantics=("arbitrary", "arbitrary"),
                                             vmem_limit_bytes=VMEM_LIMIT_BYTES),
        name="na_attention",
    )(q, k, v, ck, cv, tab)


def rope_tables(T):
    cos, sin = axial_rope(T)
    cos, sin = cos[:, 0, :], sin[:, 0, :]
    cos_t = jnp.concatenate([cos, cos, cos, cos], axis=-1)
    sin_t = jnp.concatenate([-sin, sin, -sin, sin], axis=-1)
    return cos_t, sin_t


def _rope(x, cos_t, sin_t):
    half = HEAD_DIM // 2
    lane = lax.broadcasted_iota(jnp.int32, x.shape, 1)
    first = (lane % HEAD_DIM) < half
    swapped = jnp.where(first, pltpu.roll(x, LANES - half, axis=1), pltpu.roll(x, half, axis=1))
    return x * cos_t + swapped * sin_t


def _win_kernel(sink_ref, q_ref, k_ref, v_ref, ck_ref, cv_ref, cos_ref, sin_ref, o_ref):
    i = pl.program_id(1)
    T = k_ref.shape[1]
    span = 3 * A_BLOCK
    start = pl.multiple_of(jnp.clip((i - 1) * A_BLOCK, 0, T - span), A_BLOCK)
    delta = i * A_BLOCK - start
    q0 = pl.multiple_of(i * A_BLOCK, A_BLOCK)
    scale = HEAD_DIM ** -0.5
    kw = _rope(k_ref[0, pl.ds(start, span), :], cos_ref[pl.ds(start, span), :], sin_ref[pl.ds(start, span), :]).astype(BF16)
    vw = v_ref[0, pl.ds(start, span), :].astype(BF16)
    ck = ck_ref[0]
    cv = cv_ref[0]
    cos_q = cos_ref[pl.ds(q0, A_BLOCK), :]
    sin_q = sin_ref[pl.ds(q0, A_BLOCK), :]
    qi = lax.broadcasted_iota(jnp.int32, (A_BLOCK, span), 0)
    kj = lax.broadcasted_iota(jnp.int32, (A_BLOCK, span), 1)
    band = jnp.abs(kj - delta - qi) <= A_WINDOW
    low = _low_half((A_BLOCK, LANES))
    group = A_HEADS // A_KV_HEADS
    pairs = range(A_HEADS // 2)
    cols = [slice(p * LANES, (p + 1) * LANES) for p in pairs]
    q_slab = [_rope(q_ref[0, :, c], cos_q, sin_q) for c in cols]
    heads = range(A_HEADS)
    kv_of = [h // group for h in heads]
    qm = [_place_head(q_slab[h // 2], h % 2, kv_of[h], low).astype(BF16) for h in heads]
    s_loc = [jnp.where(band, _dot_nt(qm[h], kw) * scale, NEG_INF) for h in heads]
    s_ctx = [_dot_nt(qm[h], ck) * scale for h in heads]
    outs = _softmax_pv([([s_loc[h], s_ctx[h]], [vw, cv], sink_ref[h]) for h in heads])
    outs = [outs[h] if kv_of[h] == h % 2 else pltpu.roll(outs[h], HEAD_DIM, axis=1) for h in heads]
    for p in pairs:
        o_ref[0, :, cols[p]] = jnp.where(low, outs[2 * p], outs[2 * p + 1]).astype(o_ref.dtype)


def window_attention_pallas(q, k, v, ck, cv, sink):
    B, T, QW = q.shape
    KW = k.shape[2]
    P = ck.shape[1]
    assert KW == LANES and QW == A_Q_W and T % A_BLOCK == 0 and T >= 3 * A_BLOCK
    cos_t, sin_t = rope_tables(T)
    return pl.pallas_call(
        _win_kernel,
        grid=(B, T // A_BLOCK),
        in_specs=[
            pl.BlockSpec(memory_space=pltpu.SMEM),
            pl.BlockSpec((1, A_BLOCK, QW), lambda b, i: (b, i, 0)),
            pl.BlockSpec((1, T, KW), lambda b, i: (b, 0, 0)),
            pl.BlockSpec((1, T, KW), lambda b, i: (b, 0, 0)),
            pl.BlockSpec((1, P, KW), lambda b, i: (b, 0, 0)),
            pl.BlockSpec((1, P, KW), lambda b, i: (b, 0, 0)),
            pl.BlockSpec((T, LANES), lambda b, i: (0, 0)),
            pl.BlockSpec((T, LANES), lambda b, i: (0, 0)),
        ],
        out_specs=pl.BlockSpec((1, A_BLOCK, QW), lambda b, i: (b, i, 0)),
        out_shape=jax.ShapeDtypeStruct((B, T, QW), BF16),
        compiler_params=pltpu.CompilerParams(dimension_semantics=("arbitrary", "arbitrary"),
                                             vmem_limit_bytes=VMEM_LIMIT_BYTES),
        name="window_attention",
    )(sink.astype(F32), q, k, v, ck, cv, cos_t, sin_t)


def _ctx_kernel(sink_ref, q_ref, k_ref, v_ref, o_ref, *, n_q_heads, n_kv_heads, use_sink):
    S = q_ref.shape[1]
    scale = HEAD_DIM ** -0.5
    low = _low_half((S, LANES))
    group = n_q_heads // n_kv_heads
    pairs = range(n_q_heads // 2)
    cols = [slice(p * LANES, (p + 1) * LANES) for p in pairs]
    q_slab = [q_ref[0, :, c] for c in cols]
    heads = range(n_q_heads)
    kv_of = [h // group for h in heads]
    kcols = [slice((kv // 2) * LANES, (kv // 2 + 1) * LANES) for kv in kv_of]
    qm = [_place_head(q_slab[h // 2], h % 2, kv_of[h] % 2, low).astype(BF16) for h in heads]
    s = [_dot_nt(qm[h], k_ref[0, :, kcols[h]].astype(BF16)) * scale for h in heads]
    outs = _softmax_pv([([s[h]], [v_ref[0, :, kcols[h]].astype(BF16)], sink_ref[h] if use_sink else None) for h in heads])
    outs = [outs[h] if kv_of[h] % 2 == h % 2 else pltpu.roll(outs[h], HEAD_DIM, axis=1) for h in heads]
    for p in pairs:
        o_ref[0, :, cols[p]] = jnp.where(low, outs[2 * p], outs[2 * p + 1]).astype(o_ref.dtype)


def context_attention_pallas(q, k, v, sink, n_q_heads, n_kv_heads):
    B, S, QW = q.shape
    KW = k.shape[2]
    use_sink = sink is not None
    sink_arr = sink.astype(F32) if use_sink else jnp.zeros((n_q_heads,), F32)
    return pl.pallas_call(
        functools.partial(_ctx_kernel, n_q_heads=n_q_heads, n_kv_heads=n_kv_heads, use_sink=use_sink),
        grid=(B,),
        in_specs=[
            pl.BlockSpec(memory_space=pltpu.SMEM),
            pl.BlockSpec((1, S, QW), lambda b: (b, 0, 0)),
            pl.BlockSpec((1, S, KW), lambda b: (b, 0, 0)),
            pl.BlockSpec((1, S, KW), lambda b: (b, 0, 0)),
        ],
        out_specs=pl.BlockSpec((1, S, QW), lambda b: (b, 0, 0)),
        out_shape=jax.ShapeDtypeStruct((B, S, QW), BF16),
        compiler_params=pltpu.CompilerParams(dimension_semantics=("arbitrary",), vmem_limit_bytes=VMEM_LIMIT_BYTES),
        name="context_attention",
    )(sink_arr, q, k, v)


HIGHEST = lax.Precision.HIGHEST
GDN_TM = 256
GDN_REQS = 2
GDN_SUB = 4
SUBLANES = 8


def _head_pair_sum_matrix():
    a = lax.broadcasted_iota(jnp.int32, (LANES, LANES), 0) // HEAD_DIM
    b = lax.broadcasted_iota(jnp.int32, (LANES, LANES), 1) // HEAD_DIM
    return (a == b).astype(BF16)


def _head_sums(x, pmat):
    hi = x.astype(BF16)
    lo = (x - hi.astype(F32)).astype(BF16)
    return _dot(hi, pmat) + _dot(lo, pmat)


def _gdn_prep_kernel(x_ref, prev_ref, next_ref, ab_ref, cw_ref, a_ref, dtb_ref, q_ref, k_ref, v_ref, gb_ref):
    i = pl.program_id(1)
    n = pl.num_programs(1)
    x = x_ref[0]
    tm = x.shape[0]
    row = lax.broadcasted_iota(jnp.int32, x.shape, 0)
    prev_row = jnp.where(i > 0, prev_ref[0, SUBLANES - 1:SUBLANES, :], 0.0)
    next_row = jnp.where(i < n - 1, next_ref[0, 0:1, :], 0.0)
    x_prev = jnp.where(row == 0, prev_row, pltpu.roll(x, 1, axis=0))
    x_next = jnp.where(row == tm - 1, next_row, pltpu.roll(x, tm - 1, axis=0))
    y = x_prev * cw_ref[0:1, :] + x * cw_ref[1:2, :] + x_next * cw_ref[2:3, :]
    y = y * jax.nn.sigmoid(y)
    pmat = _head_pair_sum_matrix()
    for p in range(C_W // LANES):
        qs = y[:, p * LANES:(p + 1) * LANES]
        ks = y[:, C_W + p * LANES:C_W + (p + 1) * LANES]
        q_ref[0, :, p * LANES:(p + 1) * LANES] = qs * lax.rsqrt(_head_sums(qs * qs, pmat) + EPS) * (HEAD_DIM ** -0.5)
        k_ref[0, :, p * LANES:(p + 1) * LANES] = ks * lax.rsqrt(_head_sums(ks * ks, pmat) + EPS)
    v_ref[0] = y[:, 2 * C_W:]
    ab = ab_ref[0, :, 0:4 * C_HEADS]
    lane = lax.broadcasted_iota(jnp.int32, ab.shape, 1)
    is_beta = (lane // C_HEADS) % 2 == 1
    t = ab + dtb_ref[...]
    softplus = jnp.maximum(t, 0.0) + jnp.log1p(jnp.exp(-jnp.abs(t)))
    gb = jnp.where(is_beta, jax.nn.sigmoid(ab), -jnp.exp(a_ref[...]) * softplus)
    gb_ref[0, 0] = gb[:, 0:2 * C_HEADS]
    gb_ref[0, 1] = gb[:, 2 * C_HEADS:4 * C_HEADS]


def gdn_prep(zqkv, zab, conv_w, a_log, dt_bias):
    B, L, W3 = zqkv.shape
    tm = GDN_TM
    assert L % tm == 0
    nb = tm // SUBLANES
    zero = jnp.zeros((C_HEADS,), F32)
    a_lane = jnp.concatenate([a_log[0], zero, a_log[1], zero]).astype(F32)[None, :]
    dtb_lane = jnp.concatenate([dt_bias[0], zero, dt_bias[1], zero]).astype(F32)[None, :]
    outs = pl.pallas_call(
        _gdn_prep_kernel,
        grid=(B, L // tm),
        in_specs=[
            pl.BlockSpec((1, tm, W3), lambda b, i: (b, i, 0)),
            pl.BlockSpec((1, SUBLANES, W3), lambda b, i: (b, jnp.maximum(i * nb - 1, 0), 0)),
            pl.BlockSpec((1, SUBLANES, W3), lambda b, i: (b, jnp.minimum((i + 1) * nb, L // SUBLANES - 1), 0)),
            pl.BlockSpec((1, tm, LANES), lambda b, i: (b, i, 0)),
            pl.BlockSpec((C_SHORT, W3), lambda b, i: (0, 0)),
            pl.BlockSpec((1, 4 * C_HEADS), lambda b, i: (0, 0)),
            pl.BlockSpec((1, 4 * C_HEADS), lambda b, i: (0, 0)),
        ],
        out_specs=[
            pl.BlockSpec((1, tm, C_W), lambda b, i: (b, i, 0)),
            pl.BlockSpec((1, tm, C_W), lambda b, i: (b, i, 0)),
            pl.BlockSpec((1, tm, C_W), lambda b, i: (b, i, 0)),
            pl.BlockSpec((1, 2, tm, 2 * C_HEADS), lambda b, i: (b, 0, i, 0)),
        ],
        out_shape=[jax.ShapeDtypeStruct((B, L, C_W), F32)] * 3 + [jax.ShapeDtypeStruct((B, 2, L, 2 * C_HEADS), F32)],
        compiler_params=pltpu.CompilerParams(dimension_semantics=("arbitrary", "arbitrary"),
                                             vmem_limit_bytes=VMEM_LIMIT_BYTES),
        name="gdn_prep",
    )(zqkv, zqkv, zqkv, zab, conv_w.astype(F32), a_lane, dtb_lane)
    return outs


def _gdn_kernel(q_ref, k_ref, v_ref, gb_ref, zg_ref, s0_ref, nw_ref, o_ref, st_ref, s_scr, of_scr):
    d = pl.program_id(1)
    c = pl.program_id(2)
    n = pl.num_programs(2)
    C = C_CHUNK
    fwd = d == 0
    nsub = q_ref.shape[1] // C
    block = jnp.where(fwd, c, n - 1 - c)

    @pl.when(c == 0)
    def _():
        s_scr[...] = s0_ref[:, 0]

    row = lax.broadcasted_iota(jnp.int32, (C, C), 0)
    col = lax.broadcasted_iota(jnp.int32, (C, C), 1)
    ahead = jnp.where(fwd, row - col, col - row)
    incl = ahead >= 0
    strict = ahead > 0
    incl_f = incl.astype(F32)
    nreq = q_ref.shape[0]
    units = [(r, h) for r in range(nreq) for h in range(C_HEADS)]
    s_cur = [s_scr[r, h] for r, h in units]
    pmat = _head_pair_sum_matrix()
    done = []
    for j in range(nsub):
        sub = jnp.where(fwd, j, nsub - 1 - j)
        rows = pl.ds(pl.multiple_of(sub * C, C), C)
        r0 = pl.multiple_of((block * nsub + sub) * C, C)
        gc, gc_t, g_last, beta = [], [], [], []
        for r in range(nreq):
            g = gb_ref[r, 0, rows, 0:C_HEADS]
            beta.append(gb_ref[r, 0, rows, C_HEADS:2 * C_HEADS])
            gc.append(jnp.dot(incl_f, g, precision=HIGHEST, preferred_element_type=F32))
            gc_t.append(gc[r].T)
            g_last.append(jnp.where(fwd, gc[r][C - 1:C, :], gc[r][0:1, :]))
        H = range(len(units))
        heads = [slice(h * HEAD_DIM, (h + 1) * HEAD_DIM) for _, h in units]
        s_old = s_cur
        qs = [q_ref[r, rows, heads[i]] for i, (r, _) in enumerate(units)]
        ks = [k_ref[r, rows, heads[i]] for i, (r, _) in enumerate(units)]
        vs = [v_ref[r, rows, heads[i]] for i, (r, _) in enumerate(units)]
        gcol = [gc[r][:, h:h + 1] for r, h in units]
        bcol = [beta[r][:, h:h + 1] for r, h in units]
        gl = [g_last[r][:, h:h + 1] for r, h in units]
        gamma = [jnp.exp(jnp.where(incl, gcol[i] - gc_t[r][h:h + 1, :], NEG_INF)) for i, (r, h) in enumerate(units)]
        egc = [jnp.exp(gcol[h]) for h in H]
        kb = [ks[h].astype(BF16) for h in H]
        nmat = [jnp.where(strict, _dot_nt(kb[h], kb[h]) * gamma[h], 0.0) * bcol[h] for h in H]
        attn = [(_dot_nt(qs[h].astype(BF16), kb[h]) * gamma[h]).astype(BF16) for h in H]
        xr = row ^ col
        eye = (row == col).astype(F32)
        tinv = [eye - jnp.where((xr >> 1) == 0, nmat[h], 0.0) for h in H]
        for lvl in range(1, 6):
            off_diag = (xr >> lvl) == 1
            wmat = [_dot(jnp.where(off_diag, nmat[h], 0.0).astype(BF16), tinv[h].astype(BF16)).astype(BF16) for h in H]
            tinv = [tinv[h] - _dot(tinv[h].astype(BF16), wmat[h]) for h in H]
        x = [jnp.concatenate([vs[h] * bcol[h], ks[h] * (bcol[h] * egc[h])], axis=1) for h in H]
        x = [x[h] + _dot((tinv[h] - eye).astype(BF16), x[h].astype(BF16)) for h in H]
        sb = [s_old[h].astype(BF16) for h in H]
        v_new = [x[h][:, :HEAD_DIM] - _dot(x[h][:, HEAD_DIM:].astype(BF16), sb[h]) for h in H]
        vb = [v_new[h].astype(BF16) for h in H]
        outs = [_dot((qs[h] * egc[h]).astype(BF16), sb[h]) + _dot(attn[h], vb[h]) for h in H]
        kd = [(ks[h] * jnp.exp(gl[h] - gcol[h])).astype(BF16) for h in H]
        s_new = [s_old[h] * jnp.exp(gl[h]) + lax.dot_general(kd[h], vb[h], (((0,), (0,)), ((), ())), preferred_element_type=F32)
                 for h in H]
        s_cur = s_new
        o = [jnp.concatenate(outs[r * C_HEADS:(r + 1) * C_HEADS], axis=1) for r in range(nreq)]
        done.append((rows, r0, o))

    for i, (r, h) in enumerate(units):
        s_scr[r, h] = s_cur[i]

    @pl.when(fwd)
    def _():
        for rows, r0, o in done:
            for r in range(nreq):
                of_scr[r, pl.ds(r0, C), :] = o[r]

    @pl.when(jnp.logical_not(fwd))
    def _():
        for rows, r0, o in done:
            for r in range(nreq):
                tot = of_scr[r, pl.ds(r0, C), :] + o[r]
                zg = zg_ref[r, rows, :]
                gate = zg * jax.nn.sigmoid(zg)
                for p in range(C_W // LANES):
                    cols = slice(p * LANES, (p + 1) * LANES)
                    t = tot[:, cols]
                    ms = _head_sums(t * t, pmat) * (1.0 / HEAD_DIM)
                    o_ref[r, rows, cols] = (t * lax.rsqrt(ms + EPS) * nw_ref[:, cols] * gate[:, cols]).astype(o_ref.dtype)

    @pl.when(c == n - 1)
    def _():
        st_ref[:, 0] = s_scr[...]


def gdn_scan(q, k, v, gb, zg, s0, norm_w):
    B, L, W = q.shape
    C = GDN_SUB * C_CHUNK
    n = L // C
    R = min(B, GDN_REQS)
    assert L % C == 0 and W == C_W and B % R == 0
    chunk_of = lambda d, c: jnp.where(d == 0, c, n - 1 - c)
    seq_spec = pl.BlockSpec((R, C, W), lambda b, d, c: (b, chunk_of(d, c), 0))
    state_spec = pl.BlockSpec((R, 1, C_HEADS, HEAD_DIM, HEAD_DIM), lambda b, d, c: (b, d, 0, 0, 0))
    nw = jnp.tile(norm_w.astype(F32), C_HEADS)[None, :]
    return pl.pallas_call(
        _gdn_kernel,
        grid=(B // R, 2, n),
        in_specs=[
            seq_spec, seq_spec, seq_spec,
            pl.BlockSpec((R, 1, C, 2 * C_HEADS), lambda b, d, c: (b, d, chunk_of(d, c), 0)),
            seq_spec,
            state_spec,
            pl.BlockSpec((1, W), lambda b, d, c: (0, 0)),
        ],
        out_specs=[
            pl.BlockSpec((R, C, W), lambda b, d, c: (b, jnp.where(d == 0, n - 1, n - 1 - c), 0)),
            state_spec,
        ],
        out_shape=[jax.ShapeDtypeStruct((B, L, W), BF16), jax.ShapeDtypeStruct(s0.shape, F32)],
        scratch_shapes=[pltpu.VMEM((R, C_HEADS, HEAD_DIM, HEAD_DIM), F32), pltpu.VMEM((R, L, W), F32)],
        compiler_params=pltpu.CompilerParams(dimension_semantics=("arbitrary", "arbitrary", "arbitrary"),
                                             vmem_limit_bytes=VMEM_LIMIT_BYTES),
        name="gdn_scan",
    )(q, k, v, gb, zg, s0.astype(F32), nw)


MOE_TT = 512
MOE_RT = 512
MOE_BISECT_STEPS = 64
MOE_SELECT_REQS = 4
MOE_SELECT_REQS_LONG = 4


def _moe_router_kernel(x_ref, g_ref, shift_ref, scale_ref, wr_ref, h_ref, aff_ref):
    x = x_ref[0]
    y = x * lax.rsqrt(jnp.mean(x * x, axis=-1, keepdims=True) + EPS) * g_ref[...]
    h = y * (1.0 + scale_ref[0]) + shift_ref[0]
    h_hi = h.astype(BF16)
    h_ref[0] = h_hi
    h_lo = (h - h_hi.astype(F32)).astype(BF16)
    w = wr_ref[...]
    w_hi = w.astype(BF16)
    w_lo = (w - w_hi.astype(F32)).astype(BF16)
    logits = _dot(h_hi, w_hi) + _dot(h_hi, w_lo) + _dot(h_lo, w_hi)
    lane = lax.broadcasted_iota(jnp.int32, logits.shape, 1)
    logits = jnp.where(lane < N_EXPERTS, logits, NEG_INF)
    e = jnp.exp(logits - logits.max(axis=-1, keepdims=True))
    aff_ref[0] = e / e.sum(axis=-1, keepdims=True)


def _moe_select_kernel(aff_ref, slot_ref, slot_t_ref, start_ref, *, cap):
    nreq, T = aff_ref.shape[0], aff_ref.shape[1]
    reqs = range(nreq)
    affs = [aff_ref[q] for q in reqs]

    def bisect(i, bounds):
        lo, hi = bounds
        mid = [0.5 * (lo[q] + hi[q]) for q in reqs]
        cnt = [jnp.sum((affs[q] >= mid[q]).astype(F32), axis=0, keepdims=True) for q in reqs]
        take = [cnt[q] >= cap for q in reqs]
        return (tuple(jnp.where(take[q], mid[q], lo[q]) for q in reqs),
                tuple(jnp.where(take[q], hi[q], mid[q]) for q in reqs))

    init = (tuple(jnp.zeros((1, LANES), F32) for _ in reqs), tuple(jnp.full((1, LANES), 2.0, F32) for _ in reqs))
    _, hi = lax.fori_loop(0, MOE_BISECT_STEPS, bisect, init)
    thr = [jnp.max(jnp.where(affs[q] < hi[q], affs[q], -1.0), axis=0, keepdims=True) for q in reqs]
    blk = min(T, MOE_TT)
    r = lax.broadcasted_iota(jnp.int32, (blk, blk), 0)
    c = lax.broadcasted_iota(jnp.int32, (blk, blk), 1)
    before = (c < r).astype(BF16)
    n_tiles = T // blk
    for q in reqs:
        gt = (affs[q] > thr[q]).astype(F32)
        eq = (affs[q] == thr[q]).astype(F32)
        need = cap - jnp.sum(gt, axis=0, keepdims=True)
        carry_gt = jnp.zeros((1, LANES), F32)
        carry_eq = jnp.zeros((1, LANES), F32)
        start_ref[q] = jnp.zeros(start_ref.shape[1:], F32)
        for b in range(n_tiles):
            rows = slice(b * blk, (b + 1) * blk)
            gt_b, eq_b = gt[rows], eq[rows]
            pos_gt = _dot(before, gt_b.astype(BF16)) + carry_gt
            pos_eq = _dot(before, eq_b.astype(BF16)) + carry_eq
            chosen = gt_b + eq_b * (pos_eq < need).astype(F32)
            slots = jnp.where(chosen > 0.5, pos_gt + jnp.minimum(pos_eq, need), -1.0)
            slot_ref[q, rows, :] = slots
            slot_t_ref[q, :, rows] = slots.T[:N_EXPERTS, :]
            carry_gt = carry_gt + jnp.sum(gt_b, axis=0, keepdims=True)
            carry_eq = carry_eq + jnp.sum(eq_b, axis=0, keepdims=True)
            start_ref[q, b + 1:b + 2, :] = carry_gt + jnp.minimum(carry_eq, need)


MOE_START_ROWS = 16
MOE_WIN = 128


def _one_hot_slots(slot_col, first, width):
    s = lax.broadcasted_iota(jnp.int32, (slot_col.shape[0], width), 1).astype(F32)
    return (slot_col - first == s).astype(BF16)


MOE_SLOT_ALIGN = 16


def _slot_windows(start_ref, b, e, k):
    base = (b * N_EXPERTS + e) * MOE_START_ROWS + k
    lo, hi = start_ref[base], start_ref[base + 1]
    first = (lo // MOE_SLOT_ALIGN) * MOE_SLOT_ALIGN
    return first, jnp.where(hi > lo, (hi - first + MOE_WIN - 1) // MOE_WIN, 0)


def _window_one_hot(col, first, w, cap):
    nominal = first + w * MOE_WIN
    s0 = pl.multiple_of(jnp.minimum(nominal, cap - MOE_WIN), MOE_SLOT_ALIGN)
    rel = jnp.where(col >= nominal.astype(F32), col - s0.astype(F32), -1.0)
    return s0, _one_hot_slots(rel, 0.0, MOE_WIN)


def _moe_gather_kernel(start_ref, slot_t_ref, h_ref, xg_ref, acc_ref):
    b, e = pl.program_id(0), pl.program_id(1)
    cap = acc_ref.shape[0]
    acc_ref[...] = jnp.zeros_like(acc_ref)
    slot_id = lax.broadcasted_iota(jnp.int32, (MOE_WIN, MOE_TT), 0).astype(F32)
    for k in range(slot_t_ref.shape[2] // MOE_TT):
        rows = slice(k * MOE_TT, (k + 1) * MOE_TT)
        row = slot_t_ref[0, pl.ds(e, 1), rows]
        first, count = _slot_windows(start_ref, b, e, k)

        def window(w, carry, row=row, rows=rows, first=first):
            nominal = first + w * MOE_WIN
            s0 = pl.multiple_of(jnp.minimum(nominal, cap - MOE_WIN), MOE_SLOT_ALIGN)
            rel = jnp.where(row >= nominal.astype(F32), row - s0.astype(F32), -1.0)
            p = (slot_id == rel).astype(BF16)
            acc_ref[pl.ds(s0, MOE_WIN), :] += _dot(p, h_ref[0, rows, :])
            return carry

        lax.fori_loop(0, count, window, 0)
    xg_ref[0] = acc_ref[...].astype(xg_ref.dtype)


def _all_expert_slots(slot, cap):
    width = N_EXPERTS * cap
    spread = (lax.broadcasted_iota(jnp.int32, (LANES, width), 1) // cap
              == lax.broadcasted_iota(jnp.int32, (LANES, width), 0)).astype(BF16)
    slot_wide = _dot(slot.astype(BF16), spread)
    lane_slot = (lax.broadcasted_iota(jnp.int32, (slot.shape[0], width), 1) % cap).astype(F32)
    return spread, slot_wide == lane_slot


def _moe_gather_short_kernel(slot_ref, h_ref, xg_ref, *, cap):
    _, chosen = _all_expert_slots(slot_ref[0], cap)
    rows = lax.dot_general(chosen.astype(BF16), h_ref[0], (((0,), (0,)), ((), ())), preferred_element_type=F32)
    for e in range(N_EXPERTS):
        xg_ref[e] = rows[e * cap:(e + 1) * cap].astype(xg_ref.dtype)


def _moe_ffn_kernel(x_ref, wg_ref, wu_ref, wd_ref, y_ref, wg_s, wu_s, wd_s):
    @pl.when(pl.program_id(1) == 0)
    def _():
        wg_s[...] = wg_ref[0].astype(BF16)
        wu_s[...] = wu_ref[0].astype(BF16)
        wd_s[...] = wd_ref[0].astype(BF16)

    x = x_ref[0]
    a = _dot(x, wg_s[...])
    u = _dot(x, wu_s[...])
    hid = (a * jax.nn.sigmoid(a) * u).astype(BF16)
    y_ref[0] = _dot(hid, wd_s[...]).astype(y_ref.dtype)


def _moe_scatter_kernel(start_ref, slot_ref, aff_ref, y_ref, x_ref, gate_ref, o_ref, acc_ref):
    b, k = pl.program_id(0), pl.program_id(1)
    acc_ref[...] = jnp.zeros_like(acc_ref)
    slot, aff = slot_ref[0], aff_ref[0]
    for e in range(N_EXPERTS):
        col, weight = slot[:, e:e + 1], aff[:, e:e + 1]
        first, count = _slot_windows(start_ref, b, e, k)

        def window(w, carry, e=e, col=col, weight=weight, first=first):
            s0, pt = _window_one_hot(col, first, w, y_ref.shape[1])
            acc_ref[...] += weight * _dot(pt, y_ref[e, pl.ds(s0, MOE_WIN), :])
            return carry

        lax.fori_loop(0, count, window, 0)
    o_ref[0] = x_ref[0] + gate_ref[0] * acc_ref[...]


def _moe_scatter_short_kernel(slot_ref, aff_ref, y_ref, x_ref, gate_ref, o_ref, *, cap):
    spread, chosen = _all_expert_slots(slot_ref[0], cap)
    rest = aff_ref[0]
    aff_wide = jnp.zeros(chosen.shape, F32)
    for _ in range(3):
        part = rest.astype(BF16)
        aff_wide = aff_wide + _dot(part, spread)
        rest = rest - part.astype(F32)
    weighted = jnp.where(chosen, aff_wide, 0.0)
    w_hi = weighted.astype(BF16)
    w_lo = (weighted - w_hi.astype(F32)).astype(BF16)
    y_all = jnp.concatenate([y_ref[e] for e in range(N_EXPERTS)], axis=0)
    o_ref[0] = x_ref[0] + gate_ref[0] * (_dot(w_hi, y_all) + _dot(w_lo, y_all))


def moe_route(x, g, shift, scale, w_router):
    B, T, D = x.shape
    tt = min(T, MOE_TT)
    cap = EC_CAPACITY * T // N_EXPERTS
    per_request = shift.shape[0] == B
    mod_spec = pl.BlockSpec((1, 1, D), (lambda b, k: (b, 0, 0)) if per_request else (lambda b, k: (0, 0, 0)))
    wr = jnp.concatenate([w_router.astype(F32), jnp.zeros((D, LANES - N_EXPERTS), F32)], axis=1)
    h, aff = pl.pallas_call(
        _moe_router_kernel,
        grid=(B, T // tt),
        in_specs=[pl.BlockSpec((1, tt, D), lambda b, k: (b, k, 0)), pl.BlockSpec((1, D), lambda b, k: (0, 0)),
                  mod_spec, mod_spec, pl.BlockSpec((D, LANES), lambda b, k: (0, 0))],
        out_specs=[pl.BlockSpec((1, tt, D), lambda b, k: (b, k, 0)), pl.BlockSpec((1, tt, LANES), lambda b, k: (b, k, 0))],
        out_shape=[jax.ShapeDtypeStruct((B, T, D), BF16), jax.ShapeDtypeStruct((B, T, LANES), F32)],
        compiler_params=pltpu.CompilerParams(dimension_semantics=("arbitrary", "arbitrary"),
                                             vmem_limit_bytes=VMEM_LIMIT_BYTES),
        name="moe_router",
    )(x, g.astype(F32)[None, :], shift, scale, wr)
    assert T // tt < MOE_START_ROWS
    rq = MOE_SELECT_REQS if T <= MOE_TT else MOE_SELECT_REQS_LONG
    rq = rq if B % rq == 0 else 1
    slot, slot_t, start = pl.pallas_call(
        functools.partial(_moe_select_kernel, cap=cap),
        grid=(B // rq,),
        in_specs=[pl.BlockSpec((rq, T, LANES), lambda b: (b, 0, 0))],
        out_specs=[pl.BlockSpec((rq, T, LANES), lambda b: (b, 0, 0)),
                   pl.BlockSpec((rq, N_EXPERTS, T), lambda b: (b, 0, 0)),
                   pl.BlockSpec((rq, MOE_START_ROWS, LANES), lambda b: (b, 0, 0))],
        out_shape=[jax.ShapeDtypeStruct((B, T, LANES), F32), jax.ShapeDtypeStruct((B, N_EXPERTS, T), F32),
                   jax.ShapeDtypeStruct((B, MOE_START_ROWS, LANES), F32)],
        compiler_params=pltpu.CompilerParams(dimension_semantics=("arbitrary",), vmem_limit_bytes=VMEM_LIMIT_BYTES),
        name="moe_select",
    )(aff)
    start = start[:, :, :N_EXPERTS].astype(jnp.int32).transpose(0, 2, 1).reshape(-1)
    return h, aff, slot, slot_t, start


def moe_gather(h, slot, slot_t, start):
    B, T, D = h.shape
    cap = EC_CAPACITY * T // N_EXPERTS
    out_shape = jax.ShapeDtypeStruct((N_EXPERTS, B * cap, D), BF16)
    if T <= MOE_TT:
        return pl.pallas_call(
            functools.partial(_moe_gather_short_kernel, cap=cap),
            grid=(B,),
            in_specs=[pl.BlockSpec((1, T, LANES), lambda b: (b, 0, 0)), pl.BlockSpec((1, T, D), lambda b: (b, 0, 0))],
            out_specs=pl.BlockSpec((N_EXPERTS, cap, D), lambda b: (0, b, 0)),
            out_shape=out_shape,
            compiler_params=pltpu.CompilerParams(dimension_semantics=("arbitrary",), vmem_limit_bytes=VMEM_LIMIT_BYTES),
            name="moe_gather_short",
        )(slot, h)
    assert cap % MOE_WIN == 0 and T % MOE_TT == 0
    return pl.pallas_call(
        _moe_gather_kernel,
        grid_spec=pltpu.PrefetchScalarGridSpec(
            num_scalar_prefetch=1,
            grid=(B, N_EXPERTS),
            in_specs=[pl.BlockSpec((1, N_EXPERTS, T), lambda b, e, st: (b, 0, 0)),
                      pl.BlockSpec((1, T, D), lambda b, e, st: (b, 0, 0))],
            out_specs=pl.BlockSpec((1, cap, D), lambda b, e, st: (e, b, 0)),
            scratch_shapes=[pltpu.VMEM((cap, D), F32)],
        ),
        out_shape=out_shape,
        compiler_params=pltpu.CompilerParams(dimension_semantics=("arbitrary",) * 2, vmem_limit_bytes=VMEM_LIMIT_BYTES),
        name="moe_gather",
    )(start, slot_t, h)


def moe_ffn(xg, w_gate, w_up, w_down, layer):
    E, R, D = xg.shape
    rt = min(R, MOE_RT)
    w_spec = pl.BlockSpec((None, 1, D, MOE_D_FF), lambda e, r: (layer, e, 0, 0))
    return pl.pallas_call(
        _moe_ffn_kernel,
        grid=(E, R // rt),
        in_specs=[pl.BlockSpec((1, rt, D), lambda e, r: (e, r, 0)), w_spec, w_spec,
                  pl.BlockSpec((None, 1, MOE_D_FF, D), lambda e, r: (layer, e, 0, 0))],
        out_specs=pl.BlockSpec((1, rt, D), lambda e, r: (e, r, 0)),
        out_shape=jax.ShapeDtypeStruct((E, R, D), BF16),
        scratch_shapes=[pltpu.VMEM((D, MOE_D_FF), BF16), pltpu.VMEM((D, MOE_D_FF), BF16), pltpu.VMEM((MOE_D_FF, D), BF16)],
        compiler_params=pltpu.CompilerParams(dimension_semantics=("arbitrary", "arbitrary"),
                                             vmem_limit_bytes=VMEM_LIMIT_BYTES),
        name="moe_ffn",
    )(xg, w_gate, w_up, w_down)


def moe_scatter(slot, aff, start, y, x, gate):
    B, T, D = x.shape
    cap = EC_CAPACITY * T // N_EXPERTS
    per_request = gate.shape[0] == B
    out_shape = jax.ShapeDtypeStruct((B, T, D), F32)
    if T <= MOE_TT:
        whole = lambda b: (b, 0, 0)
        return pl.pallas_call(
            functools.partial(_moe_scatter_short_kernel, cap=cap),
            grid=(B,),
            in_specs=[pl.BlockSpec((1, T, LANES), whole), pl.BlockSpec((1, T, LANES), whole),
                      pl.BlockSpec((N_EXPERTS, cap, D), lambda b: (0, b, 0)), pl.BlockSpec((1, T, D), whole),
                      pl.BlockSpec((1, 1, D), whole if per_request else (lambda b: (0, 0, 0)))],
            out_specs=pl.BlockSpec((1, T, D), whole),
            out_shape=out_shape,
            compiler_params=pltpu.CompilerParams(dimension_semantics=("arbitrary",), vmem_limit_bytes=VMEM_LIMIT_BYTES),
            name="moe_scatter_short",
        )(slot, aff, y, x, gate)
    tt = MOE_TT
    tile = lambda b, k, st: (b, k, 0)
    return pl.pallas_call(
        _moe_scatter_kernel,
        grid_spec=pltpu.PrefetchScalarGridSpec(
            num_scalar_prefetch=1,
            grid=(B, T // tt),
            in_specs=[pl.BlockSpec((1, tt, LANES), tile), pl.BlockSpec((1, tt, LANES), tile),
                      pl.BlockSpec((N_EXPERTS, cap, D), lambda b, k, st: (0, b, 0), pipeline_mode=pl.Buffered(1)),
                      pl.BlockSpec((1, tt, D), tile),
                      pl.BlockSpec((1, 1, D), (lambda b, k, st: (b, 0, 0)) if per_request else (lambda b, k, st: (0, 0, 0)))],
            out_specs=pl.BlockSpec((1, tt, D), tile),
            scratch_shapes=[pltpu.VMEM((tt, D), F32)],
        ),
        out_shape=out_shape,
        compiler_params=pltpu.CompilerParams(dimension_semantics=("arbitrary",) * 2, vmem_limit_bytes=VMEM_LIMIT_BYTES),
        name="moe_scatter",
    )(start, slot, aff, y, x, gate)


def moe_block(x, g, shift, scale, gate, w_router, w_gate, w_up, w_down, layer):
    h, aff, slot, slot_t, start = moe_route(x, g, shift, scale, w_router)
    y = moe_ffn(moe_gather(h, slot, slot_t, start), w_gate, w_up, w_down, layer)
    return moe_scatter(slot, aff, start, y, x, gate)


HY_TM = 512
HY_TK = 1024
HY_BG = 8
HY_BG_LONG = 2


def dft_tables(L):
    blk = min(L, HY_TM)
    t = jnp.arange(L, dtype=jnp.int32)

    def angles(f):
        return ((f[:, None] * t[None, :]) % (2 * L)).astype(F32) * (math.pi / L)

    a_hi = angles(jnp.arange(0, L, blk, dtype=jnp.int32))[:, None, :]
    a_lo = angles(jnp.arange(blk, dtype=jnp.int32))[None, :, :]
    cos_t = (jnp.cos(a_hi) * jnp.cos(a_lo) - jnp.sin(a_hi) * jnp.sin(a_lo)).reshape(L, L)
    sin_t = (jnp.sin(a_hi) * jnp.cos(a_lo) + jnp.cos(a_hi) * jnp.sin(a_lo)).reshape(L, L)
    return cos_t.astype(BF16), sin_t.astype(BF16)


def _alt_sign(rows, first_row):
    t = lax.broadcasted_iota(jnp.int32, (rows, 1), 0) + first_row
    return jnp.where(t % 2 == 0, 1.0, -1.0).astype(F32)


def _hy_prep_kernel(x_ref, prev_ref, next_ref, cw_ref, cb_ref, o_ref, v16_ref):
    i = pl.program_id(1)
    n = pl.num_programs(1)
    x = x_ref[0]
    tm = x.shape[0]
    row = lax.broadcasted_iota(jnp.int32, x.shape, 0)
    prev_row = jnp.where(i > 0, prev_ref[0, SUBLANES - 1:SUBLANES, :], 0.0)
    next_row = jnp.where(i < n - 1, next_ref[0, 0:1, :], 0.0)
    x_prev = jnp.where(row == 0, prev_row, pltpu.roll(x, 1, axis=0))
    x_next = jnp.where(row == tm - 1, next_row, pltpu.roll(x, tm - 1, axis=0))
    y = x_prev * cw_ref[0:1, :] + x * cw_ref[1:2, :] + x_next * cw_ref[2:3, :] + cb_ref[...]
    o_ref[0] = y
    v16_ref[0] = y[:, :HY_CH].astype(BF16)


def hyena_prep(zh, conv_w, conv_b):
    B, L, W = zh.shape
    tm = min(L, GDN_TM)
    nb = tm // SUBLANES
    return pl.pallas_call(
        _hy_prep_kernel,
        grid=(B, L // tm),
        in_specs=[
            pl.BlockSpec((1, tm, W), lambda b, i: (b, i, 0)),
            pl.BlockSpec((1, SUBLANES, W), lambda b, i: (b, jnp.maximum(i * nb - 1, 0), 0)),
            pl.BlockSpec((1, SUBLANES, W), lambda b, i: (b, jnp.minimum((i + 1) * nb, L // SUBLANES - 1), 0)),
            pl.BlockSpec((HY_SHORT, W), lambda b, i: (0, 0)),
            pl.BlockSpec((1, W), lambda b, i: (0, 0)),
        ],
        out_specs=[pl.BlockSpec((1, tm, W), lambda b, i: (b, i, 0)), pl.BlockSpec((1, tm, HY_CH), lambda b, i: (b, i, 0))],
        out_shape=[jax.ShapeDtypeStruct((B, L, W), F32), jax.ShapeDtypeStruct((B, L, HY_CH), BF16)],
        compiler_params=pltpu.CompilerParams(dimension_semantics=("arbitrary", "arbitrary"),
                                             vmem_limit_bytes=VMEM_LIMIT_BYTES),
        name="hyena_prep",
    )(zh, zh, zh, conv_w.astype(F32), conv_b.astype(F32)[None, :])


def _hy_taps_kernel(feat_ref, w1_ref, b1_ref, w2_ref, b2_ref, fr_ref, w3f_ref, w3b_ref, dec_ref, sum_ref, dif_ref):
    L = feat_ref.shape[0]
    fr = fr_ref[...]
    h = jnp.sin(fr * (jnp.dot(feat_ref[...], w1_ref[...], precision=HIGHEST, preferred_element_type=F32) + b1_ref[...]))
    h = jnp.sin(fr * (jnp.dot(h, w2_ref[...], precision=HIGHEST, preferred_element_type=F32) + b2_ref[...]))
    t = lax.broadcasted_iota(jnp.int32, (L, 1), 0)
    window = jnp.exp(-(t.astype(F32) / (L - 1)) * dec_ref[...]) + HY_SHIFT
    fwd = jnp.dot(h, w3f_ref[...], precision=HIGHEST, preferred_element_type=F32) * window
    bwd = jnp.where(t == 0, 0.0, jnp.dot(h, w3b_ref[...], precision=HIGHEST, preferred_element_type=F32) * window)
    inv = 1.0 / (jnp.sum(jnp.abs(fwd), axis=0, keepdims=True) + jnp.sum(jnp.abs(bwd), axis=0, keepdims=True))
    sum_ref[...] = (fwd + bwd) * inv
    dif_ref[...] = (bwd - fwd) * inv


def hyena_taps(L, w1, b1, w2, b2, w3, freq):
    f32 = F32
    t = jnp.linspace(0.0, 1.0, L, dtype=f32)[:, None]
    bands = (HY_EMB - 1) // 2
    omega = 2.0 * math.pi * jnp.arange(L, dtype=f32)[:, None] / L
    fb = jnp.linspace(1e-4, bands - 1, bands, dtype=f32)[None, :]
    feats = jnp.concatenate([t, jnp.cos(fb * omega), -jnp.sin(fb * omega)], axis=-1)
    max_decay = math.log(HY_DECAY_TARGET) / HY_FAST_DECAY
    min_decay = math.log(HY_DECAY_TARGET) / HY_SLOW_DECAY
    deltas = jnp.abs(jnp.linspace(min_decay, max_decay, HY_CH, dtype=f32))
    dec = jnp.tile(deltas, HY_ORDER)[None, :]
    n_col = HY_ORDER * HY_CH
    cb = 256
    full = lambda shape: pl.BlockSpec(shape, lambda j: (0, 0))
    col = lambda rows: pl.BlockSpec((rows, cb), lambda j: (0, j))
    return pl.pallas_call(
        _hy_taps_kernel,
        grid=(n_col // cb,),
        in_specs=[full((L, HY_EMB)), full((HY_EMB, HY_FILT_W)), full((1, HY_FILT_W)), full((HY_FILT_W, HY_FILT_W)),
                  full((1, HY_FILT_W)), full((1, HY_FILT_W)), col(HY_FILT_W), col(HY_FILT_W), col(1)],
        out_specs=[col(L), col(L)],
        out_shape=[jax.ShapeDtypeStruct((L, n_col), f32)] * 2,
        compiler_params=pltpu.CompilerParams(dimension_semantics=("arbitrary",), vmem_limit_bytes=VMEM_LIMIT_BYTES),
        name="hyena_taps",
    )(feats, w1.astype(f32), b1.astype(f32)[None, :], w2.astype(f32), b2.astype(f32)[None, :], freq.astype(f32)[None, :],
      w3.astype(f32)[:, :n_col], w3.astype(f32)[:, n_col:], dec)


def _hy_spec_kernel(ch_ref, sh_ref, sum_ref, dif_ref, hr_ref, hi_ref, ny_ref, acc_r, acc_i, acc_n):
    m, k = pl.program_id(1), pl.program_id(2)
    tk = sum_ref.shape[0]

    @pl.when(k == 0)
    def _():
        acc_r[...] = jnp.zeros_like(acc_r)
        acc_i[...] = jnp.zeros_like(acc_i)

    @pl.when(jnp.logical_and(k == 0, m == 0))
    def _():
        acc_n[...] = jnp.zeros_like(acc_n)

    a = sum_ref[...]
    acc_r[...] += _dot(ch_ref[...], a.astype(BF16))
    acc_i[...] += _dot(sh_ref[...], dif_ref[...].astype(BF16))

    @pl.when(m == 0)
    def _():
        acc_n[...] += jnp.sum(a * _alt_sign(tk, k * tk), axis=0, keepdims=True)

    @pl.when(k == pl.num_programs(2) - 1)
    def _():
        hr_ref[...] = acc_r[...]
        hi_ref[...] = acc_i[...]
        ny_ref[...] = jnp.broadcast_to(acc_n[...], ny_ref.shape)


def hyena_spectrum(tables, tap_sum, tap_dif):
    L, N = tap_sum.shape
    tm = min(L, HY_TM)
    tk = min(L, 2 * HY_TK)
    cb = 512
    tab = pl.BlockSpec((tm, tk), lambda j, m, k: (m, k))
    dat = pl.BlockSpec((tk, cb), lambda j, m, k: (k, j))
    return pl.pallas_call(
        _hy_spec_kernel,
        grid=(N // cb, L // tm, L // tk),
        in_specs=[tab, tab, dat, dat],
        out_specs=[pl.BlockSpec((tm, cb), lambda j, m, k: (m, j)), pl.BlockSpec((tm, cb), lambda j, m, k: (m, j)),
                   pl.BlockSpec((SUBLANES, cb), lambda j, m, k: (0, j))],
        out_shape=[jax.ShapeDtypeStruct((L, N), F32), jax.ShapeDtypeStruct((L, N), F32),
                   jax.ShapeDtypeStruct((SUBLANES, N), F32)],
        scratch_shapes=[pltpu.VMEM((tm, cb), F32), pltpu.VMEM((tm, cb), F32), pltpu.VMEM((1, cb), F32)],
        compiler_params=pltpu.CompilerParams(dimension_semantics=("arbitrary",) * 3, vmem_limit_bytes=VMEM_LIMIT_BYTES),
        name="hyena_spectrum",
    )(*tables, tap_sum, tap_dif)


def _hy_fwd_kernel(ch_ref, sh_ref, u_ref, hr_ref, hi_ref, hny_ref, yr_ref, yi_ref, yny_ref,
                   acc_c, acc_s, acc_n):
    m, k = pl.program_id(1), pl.program_id(2)
    nb, tk = u_ref.shape[0], u_ref.shape[1]
    tm = acc_c.shape[1]

    @pl.when(k == 0)
    def _():
        acc_c[...] = jnp.zeros_like(acc_c)
        acc_s[...] = jnp.zeros_like(acc_s)

    @pl.when(jnp.logical_and(k == 0, m == 0))
    def _():
        acc_n[...] = jnp.zeros_like(acc_n)

    sign = _alt_sign(tk, k * tk)
    for b in range(nb):
        ub = u_ref[b]
        acc_c[b] += _dot(ch_ref[...], ub)
        acc_s[b] += _dot(sh_ref[...], ub)

        @pl.when(m == 0)
        def _():
            acc_n[b] += jnp.sum(ub.astype(F32) * sign, axis=0, keepdims=True)

    @pl.when(k == pl.num_programs(2) - 1)
    def _():
        f = lax.broadcasted_iota(jnp.int32, (tm, 1), 0) + m * tm
        dc = jnp.where(f == 0, 0.5, 1.0).astype(F32)
        hr, hi = hr_ref[...], hi_ref[...]
        for b in range(nb):
            xr, xs = acc_c[b], acc_s[b]
            yr_ref[b] = ((xr * hr + xs * hi) * dc).astype(yr_ref.dtype)
            yi_ref[b] = (xr * hi - xs * hr).astype(yi_ref.dtype)
            yny_ref[b] = jnp.broadcast_to(acc_n[b] * hny_ref[0:1, :], yny_ref.shape[1:])


def _hy_inv_kernel(ch_ref, sh_ref, yr_ref, yi_ref, yny_ref, u_ref, xg_ref, skip_ref, *rest):
    o_refs, acc = rest[:-1], rest[-1]
    m, k = pl.program_id(1), pl.program_id(2)
    nb = yr_ref.shape[0]
    tm = acc.shape[1]
    L = tm * pl.num_programs(1)

    @pl.when(k == 0)
    def _():
        acc[...] = jnp.zeros_like(acc)

    for b in range(nb):
        acc[b] += _dot(ch_ref[...], yr_ref[b]) - _dot(sh_ref[...], yi_ref[b])

    @pl.when(k == pl.num_programs(2) - 1)
    def _():
        sign = _alt_sign(tm, m * tm)
        for b in range(nb):
            u = u_ref[b]
            y = acc[b] * (1.0 / L) + sign * yny_ref[b, 0:1, :] * (0.5 / L)
            res = xg_ref[b] * (y + u * skip_ref[...])
            for o_ref in o_refs:
                o_ref[b] = res.astype(o_ref.dtype)


def hyena_long_conv(tables, zf, u, u16, u_col, gate_col, hr, hi, hny, order, skip, out_dtypes):
    B, L, _ = u.shape
    C = HY_CH
    tm = min(L, HY_TM)
    tk = min(L, HY_TK)
    bg = min(B, HY_BG_LONG if L >= HY_TM else HY_BG)
    assert B % bg == 0
    grid = (B // bg, L // tm, L // tk)
    tab = pl.BlockSpec((tm, tk), lambda g, m, k: (m, k))
    params = pltpu.CompilerParams(dimension_semantics=("arbitrary",) * 3, vmem_limit_bytes=VMEM_LIMIT_BYTES)
    spec_m = pl.BlockSpec((tm, C), lambda g, m, k: (m, order))
    yr, yi, yny = pl.pallas_call(
        _hy_fwd_kernel,
        grid=grid,
        in_specs=[tab, tab, pl.BlockSpec((bg, tk, C), lambda g, m, k: (g, k, 0)), spec_m, spec_m,
                  pl.BlockSpec((SUBLANES, C), lambda g, m, k: (0, order))],
        out_specs=[pl.BlockSpec((bg, tm, C), lambda g, m, k: (g, m, 0)), pl.BlockSpec((bg, tm, C), lambda g, m, k: (g, m, 0)),
                   pl.BlockSpec((bg, SUBLANES, C), lambda g, m, k: (g, 0, 0))],
        out_shape=[jax.ShapeDtypeStruct((B, L, C), BF16), jax.ShapeDtypeStruct((B, L, C), BF16),
                   jax.ShapeDtypeStruct((B, SUBLANES, C), F32)],
        scratch_shapes=[pltpu.VMEM((bg, tm, C), F32), pltpu.VMEM((bg, tm, C), F32), pltpu.VMEM((bg, 1, C), F32)],
        compiler_params=params,
        name="hyena_fwd",
    )(*tables, u16, hr, hi, hny)
    return pl.pallas_call(
        _hy_inv_kernel,
        grid=grid,
        in_specs=[tab, tab, pl.BlockSpec((bg, tk, C), lambda g, m, k: (g, k, 0)),
                  pl.BlockSpec((bg, tk, C), lambda g, m, k: (g, k, 0)),
                  pl.BlockSpec((bg, SUBLANES, C), lambda g, m, k: (g, 0, 0)),
                  pl.BlockSpec((bg, tm, C), lambda g, m, k: (g, m, u_col)),
                  pl.BlockSpec((bg, tm, C), lambda g, m, k: (g, m, gate_col)),
                  pl.BlockSpec((1, C), lambda g, m, k: (0, 0))],
        out_specs=[pl.BlockSpec((bg, tm, C), lambda g, m, k: (g, m, 0)) for _ in out_dtypes],
        out_shape=[jax.ShapeDtypeStruct((B, L, C), dt) for dt in out_dtypes],
        scratch_shapes=[pltpu.VMEM((bg, tm, C), F32)],
        compiler_params=params,
        name="hyena_inv",
    )(*tables, yr, yi, yny, u, zf, skip.astype(F32)[order][None, :])


def hyena_mixer_pallas(zh, conv_w, conv_b, skip, tables, spectrum):
    hr, hi, hny = spectrum
    zf, v16 = hyena_prep(zh, conv_w, conv_b)
    y1, y1_16 = hyena_long_conv(tables, zf, zf, v16, 0, 1, hr, hi, hny, 0, skip, [F32, BF16])
    return hyena_long_conv(tables, zf, y1, y1_16, 0, 2, hr, hi, hny, 1, skip, [BF16])[0]


def _ada_kernel(c_ref, w_ref, b_ref, o_ref):
    cnd = c_ref[...]
    act = cnd * jax.nn.sigmoid(cnd)
    o_ref[0] = jnp.dot(act, w_ref[0], precision=HIGHEST, preferred_element_type=F32) + b_ref[0]


def ada_params_all(cond, w_ada, b_ada):
    N, D = cond.shape
    depth, _, W = w_ada.shape
    rows = -(-N // SUBLANES) * SUBLANES
    cond_p = jnp.concatenate([cond.astype(F32), jnp.zeros((rows - N, D), F32)], axis=0)
    cb = D
    out = pl.pallas_call(
        _ada_kernel,
        grid=(depth, W // cb),
        in_specs=[pl.BlockSpec((rows, D), lambda l, j: (0, 0)), pl.BlockSpec((1, D, cb), lambda l, j: (l, 0, j)),
                  pl.BlockSpec((1, 1, cb), lambda l, j: (l, 0, j))],
        out_specs=pl.BlockSpec((1, rows, cb), lambda l, j: (l, 0, j)),
        out_shape=jax.ShapeDtypeStruct((depth, rows, W), F32),
        compiler_params=pltpu.CompilerParams(dimension_semantics=("arbitrary", "arbitrary"),
                                             vmem_limit_bytes=VMEM_LIMIT_BYTES),
        name="ada_params",
    )(cond_p, w_ada.astype(F32), b_ada.astype(F32)[:, None, :])
    return out[:, :N]


def _final_norm_kernel(x_ref, g_ref, o_ref):
    x = x_ref[...]
    o_ref[...] = x * lax.rsqrt(jnp.mean(x * x, axis=-1, keepdims=True) + EPS) * g_ref[...]


def final_norm(x, g, tm=512):
    B, L, D = x.shape
    M = B * L
    assert M % tm == 0
    out = pl.pallas_call(
        _final_norm_kernel,
        grid=(M // tm,),
        in_specs=[pl.BlockSpec((tm, D), lambda i: (i, 0)), pl.BlockSpec((1, D), lambda i: (0, 0))],
        out_specs=pl.BlockSpec((tm, D), lambda i: (i, 0)),
        out_shape=jax.ShapeDtypeStruct((M, D), F32),
        compiler_params=pltpu.CompilerParams(dimension_semantics=("arbitrary",), vmem_limit_bytes=VMEM_LIMIT_BYTES),
        name="final_norm",
    )(x.reshape(M, D), g.astype(F32)[None, :])
    return out.reshape(B, L, D)


def axial_rope(T):
    t = jnp.arange(T)
    n_freq = HEAD_DIM // 4
    inv = ROPE_BASE ** (-jnp.arange(n_freq, dtype=jnp.float32) / n_freq)
    ang = jnp.concatenate([(t // GRID_W).astype(jnp.float32)[:, None] * inv,
                           (t % GRID_W).astype(jnp.float32)[:, None] * inv], axis=-1)
    return jnp.cos(ang)[:, None, :], jnp.sin(ang)[:, None, :]


def kernel(x_prompt, x_sample, cache_attn_k, cache_attn_v, state_delta, cache_na_k, cache_na_v,
           c, c_ctx, w_ada, b_ada, norm_mix, norm_ffn, norm_final,
           even_w_in, even_w_out, attn_sink, hy_conv_w, hy_conv_b, hy_w1, hy_b1, hy_w2, hy_b2,
           hy_w3, hy_freq, hy_skip, odd_w_in, odd_w_out, gdn_conv_w, gdn_a_log, gdn_dt_bias,
           gdn_norm, na_rpb, moe_router, moe_w_gate, moe_w_up, moe_w_down):
    xp, xs = x_prompt, x_sample
    bp = xp.shape[0]
    dft_p, dft_s = dft_tables(xp.shape[1]), dft_tables(xs.shape[1])
    ada = ada_params_all(jnp.concatenate([c_ctx[None, :], c], axis=0), w_ada, b_ada)
    new_ak, new_av, new_st, new_nk, new_nv = [], [], [], [], []
    for l in range(DEPTH):
        j = l // 2
        mp = jnp.split(ada[l, :1, None, :], 6, axis=-1)
        ms = jnp.split(ada[l, 1:, None, :], 6, axis=-1)
        mod_p = (norm_mix[l], mp[0], mp[1])
        mod_s = (norm_mix[l], ms[0], ms[1])
        if l % 2 == 0:
            def hyena(zh, tables):
                taps = hyena_taps(zh.shape[1], hy_w1[j], hy_b1[j], hy_w2[j], hy_b2[j], hy_w3[j], hy_freq[j])
                return hyena_mixer_pallas(zh, hy_conv_w[j], hy_conv_b[j], hy_skip[j], tables, hyena_spectrum(tables, *taps))

            w_in = even_w_in[j]
            w_groups = [w_in[:, :A_Q_W], w_in[:, A_Q_W:A_Q_W + A_KV_W], w_in[:, A_Q_W + A_KV_W:A_Q_W + 2 * A_KV_W],
                        w_in[:, A_Q_W + 2 * A_KV_W:]]
            q, k, v, zh = proj_multi(xp, *mod_p, w_groups, [F32] * 4)
            oa = context_attention_pallas(q, k, v, attn_sink[j], A_HEADS, A_KV_HEADS)
            xp_new = proj_concat(oa, hyena(zh, dft_p), even_w_out[j], xp, mp[2])
            new_ak.append(k.reshape(bp, SEQ, A_KV_HEADS, HEAD_DIM))
            new_av.append(v.reshape(bp, SEQ, A_KV_HEADS, HEAD_DIM))
            q, k, v, zh = proj_multi(xs, *mod_s, w_groups, [F32] * 4)
            ck = cache_attn_k[:, j].reshape(DEC_BATCH, PAST_LEN, A_KV_W).astype(BF16)
            cv = cache_attn_v[:, j].reshape(DEC_BATCH, PAST_LEN, A_KV_W).astype(BF16)
            oa = window_attention_pallas(q, k, v, ck, cv, attn_sink[j])
            xs_new = proj_concat(oa, hyena(zh, dft_s), even_w_out[j], xs, ms[2])
        else:
            w_in = odd_w_in[j]
            ab0 = 4 * C_W
            ab_cols = [w_in[:, ab0 + o * C_HEADS:ab0 + (o + 1) * C_HEADS] for o in (0, 2, 1, 3)]
            w_ab = jnp.concatenate(ab_cols + [jnp.zeros((D_MODEL, LANES - 4 * C_HEADS), w_in.dtype)], axis=1)
            n0 = ab0 + 4 * C_HEADS
            w_groups = [w_in[:, :3 * C_W], w_in[:, 3 * C_W:4 * C_W], w_ab,
                        w_in[:, n0:n0 + D_W], w_in[:, n0 + D_W:n0 + 2 * D_W], w_in[:, n0 + 2 * D_W:]]

            def deltanet(zqkv, zab, zg, s0):
                qd, kd, vd, gb = gdn_prep(zqkv, zab, gdn_conv_w[j], gdn_a_log[j], gdn_dt_bias[j])
                return gdn_scan(qd, kd, vd, gb, zg, s0, gdn_norm[j])

            zqkv, zg, zab, nq, nk, nv = proj_multi(xp, *mod_p, w_groups, [F32] * 6)
            oc, st = deltanet(zqkv, zab, zg, jnp.zeros((bp, 2, C_HEADS, HEAD_DIM, HEAD_DIM), F32))
            od = context_attention_pallas(nq, nk, nv, None, D_HEADS, D_HEADS)
            xp_new = proj_concat(oc, od, odd_w_out[j], xp, mp[2])
            new_st.append(st)
            new_nk.append(nk.reshape(bp, SEQ, D_HEADS, HEAD_DIM))
            new_nv.append(nv.reshape(bp, SEQ, D_HEADS, HEAD_DIM))
            zqkv, zg, zab, nq, nk, nv = proj_multi(xs, *mod_s, w_groups, [F32, F32, F32, BF16, BF16, BF16])
            oc, _ = deltanet(zqkv, zab, zg, state_delta[:, j])
            ck = cache_na_k[:, j].reshape(DEC_BATCH, PAST_LEN, D_W).astype(BF16)
            cv = cache_na_v[:, j].reshape(DEC_BATCH, PAST_LEN, D_W).astype(BF16)
            od = neighbourhood_attention_pallas(nq, nk, nv, ck, cv, na_rpb[j])
            xs_new = proj_concat(oc, od, odd_w_out[j], xs, ms[2])
        xp, xs = xp_new, xs_new
        moe = (moe_router[l], moe_w_gate, moe_w_up, moe_w_down, l)
        xp = moe_block(xp, norm_ffn[l], mp[3], mp[4], mp[5], *moe)
        xs = moe_block(xs, norm_ffn[l], ms[3], ms[4], ms[5], *moe)
    y_prompt = final_norm(xp, norm_final)
    y_sample = final_norm(xs, norm_final)
    return (y_prompt, y_sample, jnp.stack(new_ak, axis=1), jnp.stack(new_av, axis=1), jnp.stack(new_st, axis=1),
            jnp.stack(new_nk, axis=1), jnp.stack(new_nv, axis=1))
```
